```python
import math
import jax, jax.numpy as jnp
from jax import lax
import numpy as np


D_MODEL = 1024
BATCH = 32
SEQ = 256
DEPTH = 2
DEC_BATCH = 8
DEC_SEQ = 4096
PAST_LEN = 512

GRID_W = 64
MIX_W = D_MODEL
HALF_W = MIX_W // 2
CONV_CH = HALF_W
CONV_K = 31
DIFF_HEADS = 4
DIFF_DH = HALF_W // (2 * DIFF_HEADS)
QK_W = DIFF_HEADS * 2 * DIFF_DH
AX_DIM = DIFF_DH // 2
ROPE_BASE = 10000.0
POOL_WINDOWS = (2, 4, 8, 16)
POOL_CH = HALF_W
POOL_GC = POOL_CH // len(POOL_WINDOWS)
SCONV_CH = HALF_W
SCONV_K = 3
N_GROUPS = 4
EXPERTS_PER_GROUP = 8
N_EXPERTS = N_GROUPS * EXPERTS_PER_GROUP
TOP_K = 2
EXPERT_HIDDEN = D_MODEL // 2
MOE_BLOCK = 128
Q_BLOCK = 128
N_EVEN = (DEPTH + 1) // 2
N_ODD = DEPTH // 2
N_ATTN_LAYERS = N_EVEN
EVEN_IN = 2 * CONV_CH + 3 * QK_W
ODD_IN = POOL_CH + 3 * SCONV_CH
EPS = 1e-6

kernel_name = 'hybrid_prefix_diffusion_step'


def rmsnorm(x, g):
    xf = x.astype(jnp.float32)
    y = xf * lax.rsqrt(jnp.mean(xf * xf, axis=-1, keepdims=True) + EPS)
    return (y * g.astype(jnp.float32)).astype(x.dtype)


def layernorm(x, g, b):
    xf = x.astype(jnp.float32)
    mu = jnp.mean(xf, axis=-1, keepdims=True)
    var = jnp.mean(jnp.square(xf - mu), axis=-1, keepdims=True)
    y = (xf - mu) * lax.rsqrt(var + EPS)
    return (y * g.astype(jnp.float32) + b.astype(jnp.float32)).astype(x.dtype)


def dwconv(x, w):
    k = w.shape[0]
    pad = k // 2
    return lax.conv_general_dilated(x, w[:, None, :].astype(x.dtype), window_strides=(1,),
                                    padding=[(pad, pad)], dimension_numbers=('NWC', 'WIO', 'NWC'),
                                    feature_group_count=x.shape[-1])


def axial_rope_tables(t_len):
    rows = t_len // GRID_W
    row = jnp.repeat(jnp.arange(rows, dtype=jnp.float32), GRID_W)
    col = jnp.tile(jnp.arange(GRID_W, dtype=jnp.float32), rows)
    inv = ROPE_BASE ** (-jnp.arange(0, AX_DIM, 2, dtype=jnp.float32) / AX_DIM)
    ar = row[:, None] * inv[None, :]
    ac = col[:, None] * inv[None, :]
    return (jnp.cos(ar), jnp.sin(ar), jnp.cos(ac), jnp.sin(ac))


def _rot(x, cos, sin):
    h = x.shape[-1] // 2
    x1, x2 = x[..., :h], x[..., h:]
    return jnp.concatenate([x1 * cos - x2 * sin, x1 * sin + x2 * cos], axis=-1)


def axial_rope(x, tables):
    cr, sr, cc, sc = [t[None, :, None, None, :].astype(x.dtype) for t in tables]
    return jnp.concatenate([_rot(x[..., :AX_DIM], cr, sr), _rot(x[..., AX_DIM:], cc, sc)], axis=-1)


def diff_attention(q, k, v, lam):
    b, t, h, _, dh = q.shape
    qb_len = min(Q_BLOCK, t)
    nq = t // qb_len
    qb = jnp.moveaxis(q.reshape(b, nq, qb_len, h, 2, dh), 1, 0)
    scale = dh ** -0.5

    def block(qblk):
        s = jnp.einsum('bqhcd,bkhcd->bhcqk', qblk, k).astype(jnp.float32) * scale
        pr = jax.nn.softmax(s, axis=-1)
        a = (pr[:, :, 0] - lam * pr[:, :, 1]).astype(v.dtype)
        return jnp.einsum('bhqk,bkhe->bqhe', a, v)

    o = lax.map(block, qb)
    return jnp.moveaxis(o, 0, 1).reshape(b, t, h, 2 * dh)


def multiscale_pool(xc, w_grp, scale):
    b, t, _ = xc.shape
    ng = len(POOL_WINDOWS)
    xg = xc.reshape(b, t, ng, POOL_GC).astype(jnp.float32)
    cs = jnp.concatenate([jnp.zeros((b, 1, ng, POOL_GC), jnp.float32), jnp.cumsum(xg, axis=1)], axis=1)
    pos = jnp.arange(t)
    outs = []
    for g, w in enumerate(POOL_WINDOWS):
        lo = jnp.maximum(pos - w // 2, 0)
        hi = jnp.minimum(pos + w // 2 - 1, t - 1)
        cnt = (hi - lo + 1).astype(jnp.float32)
        outs.append((cs[:, hi + 1, g] - cs[:, lo, g]) / cnt[None, :, None])
    d = (jnp.stack(outs, axis=2) - xg).astype(xc.dtype)
    y = jnp.einsum('btgc,gce->btge', d, w_grp).reshape(b, t, POOL_CH)
    return y * scale


def hmoe(x, w_rg, b_rg, w_re, b_re, w_g, w_u, w_d):
    n, d = x.shape
    gp = jax.nn.softmax((x @ w_rg + b_rg).astype(jnp.float32), axis=-1)
    g_idx = jnp.argmax(gp, axis=-1).astype(jnp.int32)
    g_p = jnp.take_along_axis(gp, g_idx[:, None], axis=-1)[:, 0]
    el_all = jnp.einsum('nd,gde->nge', x, w_re) + b_re
    el = jnp.take_along_axis(el_all, g_idx[:, None, None], axis=1)[:, 0].astype(jnp.float32)
    top_v, top_i = lax.top_k(el, TOP_K)
    wts = jax.nn.softmax(top_v, axis=-1) * g_p[:, None]
    eid = g_idx[:, None] * EXPERTS_PER_GROUP + top_i.astype(jnp.int32)

    a_len = n * TOP_K
    flat_e = eid.reshape(-1)
    flat_t = jnp.repeat(jnp.arange(n, dtype=jnp.int32), TOP_K)
    flat_w = wts.reshape(-1)
    order = jnp.argsort(flat_e)
    counts = jnp.zeros((N_EXPERTS,), jnp.int32).at[flat_e].add(1)
    pc = ((counts + MOE_BLOCK - 1) // MOE_BLOCK) * MOE_BLOCK
    start = jnp.cumsum(counts) - counts
    pend = jnp.cumsum(pc)
    pstart = pend - pc
    se = flat_e[order]
    dest = pstart[se] + (jnp.arange(a_len, dtype=jnp.int32) - start[se])
    p_len = a_len + N_EXPERTS * MOE_BLOCK
    nblk = p_len // MOE_BLOCK
    row_tok = jnp.full((p_len,), n, jnp.int32).at[dest].set(flat_t[order])
    row_w = jnp.zeros((p_len,), x.dtype).at[dest].set(flat_w[order].astype(x.dtype))
    blk_e = jnp.minimum(jnp.searchsorted(pend, jnp.arange(nblk, dtype=jnp.int32) * MOE_BLOCK, side='right'),
                        N_EXPERTS - 1).astype(jnp.int32)
    xp = jnp.concatenate([x, jnp.zeros((1, d), x.dtype)], axis=0)
    xb = xp[row_tok].reshape(nblk, MOE_BLOCK, d)

    def expert_block(args):
        xblk, e = args
        hid = jax.nn.silu(xblk @ w_g[e]) * (xblk @ w_u[e])
        return hid @ w_d[e]

    yb = lax.map(expert_block, (xb, blk_e)).reshape(p_len, d)
    y = jax.ops.segment_sum(yb * row_w[:, None], row_tok, num_segments=n + 1)
    return y[:n]


def even_mixer(h, p, j, i, rope, ctx_kv):
    b, t, _ = h.shape
    proj = h @ p['w_in_e'][j]
    ga, gb, q, k, v = jnp.split(proj, [CONV_CH, 2 * CONV_CH, 2 * CONV_CH + QK_W, 2 * CONV_CH + 2 * QK_W], axis=-1)
    u = ga * jax.nn.sigmoid(gb)
    u = dwconv(u, p['conv_a_w'][j]) + p['conv_a_b'][j]
    u = jax.nn.silu(layernorm(u, p['ln_a_g'][j], p['ln_a_b'][j]))
    q = q.reshape(b, t, DIFF_HEADS, 2, DIFF_DH)
    k = k.reshape(b, t, DIFF_HEADS, 2, DIFF_DH)
    v = v.reshape(b, t, DIFF_HEADS, 2 * DIFF_DH)
    if rope is not None:
        q = axial_rope(q, rope)
        k = axial_rope(k, rope)
    if ctx_kv is None:
        keys, vals = k, v
    else:
        keys = jnp.concatenate([ctx_kv[0].astype(k.dtype), k], axis=1)
        vals = jnp.concatenate([ctx_kv[1].astype(v.dtype), v], axis=1)
    lam_init = 0.8 - 0.6 * math.exp(-0.3 * i)
    f32 = jnp.float32
    lam = (jnp.exp(jnp.sum(p['lam_q1'][j].astype(f32) * p['lam_k1'][j].astype(f32)))
           - jnp.exp(jnp.sum(p['lam_q2'][j].astype(f32) * p['lam_k2'][j].astype(f32))) + lam_init)
    o = diff_attention(q, keys, vals, lam)
    o = rmsnorm(o, p['subln_g'][j]) * (1.0 - lam_init)
    out = jnp.concatenate([u, o.reshape(b, t, QK_W)], axis=-1) @ p['w_out_e'][j]
    return out, k, v


def odd_mixer(h, p, j):
    proj = h @ p['w_in_o'][j]
    xc, hd, bg, cg = jnp.split(proj, [POOL_CH, POOL_CH + SCONV_CH, POOL_CH + 2 * SCONV_CH], axis=-1)
    yc = multiscale_pool(xc, p['pool_w'][j], p['pool_scale'][j])
    yd = bg * dwconv(cg * hd, p['conv_d_w'][j])
    return jnp.concatenate([yc, yd], axis=-1) @ p['w_out_o'][j]


def run_trunk(x, cond, p, rope, ctx_k, ctx_v):
    new_k, new_v = [], []
    for i in range(DEPTH):
        j = i // 2
        mod = (cond @ p['w_ada'][i] + p['b_ada'][i])[:, None, :]
        sh1, sc1, g1, sh2, sc2, g2 = jnp.split(mod, 6, axis=-1)
        h = rmsnorm(x, p['g_norm1'][i]) * (1 + sc1) + sh1
        if i % 2 == 0:
            kv_in = None if ctx_k is None else (ctx_k[:, j], ctx_v[:, j])
            m, k_cur, v_cur = even_mixer(h, p, j, i, rope, kv_in)
            if ctx_k is None:
                new_k.append(k_cur)
                new_v.append(v_cur)
        else:
            m = odd_mixer(h, p, j)
        x = x + g1 * m
        h = rmsnorm(x, p['g_norm2'][i]) * (1 + sc2) + sh2
        b, t, d = h.shape
        y = hmoe(h.reshape(b * t, d), p['w_router_g'][i], p['b_router_g'][i], p['w_router_e'][i],
                 p['b_router_e'][i], p['w_gate'][i], p['w_up'][i], p['w_down'][i])
        x = x + g2 * y.reshape(b, t, d)
    return rmsnorm(x, p['g_final']), new_k, new_v


def setup_inputs(seed: int = 0) -> dict:
    key = jax.random.key(seed)
    ks = iter(jax.random.split(key, 40))
    D = D_MODEL

    def nrm(shape, s=1.0):
        return jax.random.normal(next(ks), shape, jnp.float32) * s

    def gain(shape):
        return 1.0 + nrm(shape, 0.02)

    return {
        'x_prompt': nrm((BATCH, SEQ, D)),
        'x_sample': nrm((DEC_BATCH, DEC_SEQ, D)),
        'cache_k': nrm((DEC_BATCH, N_ATTN_LAYERS, PAST_LEN, DIFF_HEADS, 2, DIFF_DH)),
        'cache_v': nrm((DEC_BATCH, N_ATTN_LAYERS, PAST_LEN, DIFF_HEADS, 2 * DIFF_DH)),
        'c': nrm((DEC_BATCH, D)),
        'c_ctx': nrm((D,)),
        'w_ada': nrm((DEPTH, D, 6 * D), 0.5 * D ** -0.5),
        'b_ada': nrm((DEPTH, 6 * D), 0.02),
        'g_norm1': gain((DEPTH, D)),
        'g_norm2': gain((DEPTH, D)),
        'g_final': gain((D,)),
        'w_in_e': nrm((N_EVEN, D, EVEN_IN), D ** -0.5),
        'conv_a_w': nrm((N_EVEN, CONV_K, CONV_CH), CONV_K ** -0.5),
        'conv_a_b': nrm((N_EVEN, CONV_CH), 0.02),
        'ln_a_g': gain((N_EVEN, CONV_CH)),
        'ln_a_b': nrm((N_EVEN, CONV_CH), 0.02),
        'lam_q1': nrm((N_EVEN, DIFF_DH), 0.1),
        'lam_k1': nrm((N_EVEN, DIFF_DH), 0.1),
        'lam_q2': nrm((N_EVEN, DIFF_DH), 0.1),
        'lam_k2': nrm((N_EVEN, DIFF_DH), 0.1),
        'subln_g': gain((N_EVEN, 2 * DIFF_DH)),
        'w_out_e': nrm((N_EVEN, MIX_W, D), MIX_W ** -0.5),
        'w_in_o': nrm((N_ODD, D, ODD_IN), D ** -0.5),
        'pool_w': nrm((N_ODD, len(POOL_WINDOWS), POOL_GC, POOL_GC), POOL_GC ** -0.5),
        'pool_scale': gain((N_ODD, POOL_CH)),
        'conv_d_w': nrm((N_ODD, SCONV_K, SCONV_CH), SCONV_K ** -0.5),
        'w_out_o': nrm((N_ODD, MIX_W, D), MIX_W ** -0.5),
        'w_router_g': nrm((DEPTH, D, N_GROUPS), D ** -0.5),
        'b_router_g': nrm((DEPTH, N_GROUPS), 0.01),
        'w_router_e': nrm((DEPTH, N_GROUPS, D, EXPERTS_PER_GROUP), D ** -0.5),
        'b_router_e': nrm((DEPTH, N_GROUPS, EXPERTS_PER_GROUP), 0.01),
        'w_gate': nrm((DEPTH, N_EXPERTS, D, EXPERT_HIDDEN), D ** -0.5),
        'w_up': nrm((DEPTH, N_EXPERTS, D, EXPERT_HIDDEN), D ** -0.5),
        'w_down': nrm((DEPTH, N_EXPERTS, EXPERT_HIDDEN, D), EXPERT_HIDDEN ** -0.5),
    }


def reference(x_prompt, x_sample, cache_k, cache_v, c, c_ctx, w_ada, b_ada, g_norm1, g_norm2, g_final,
              w_in_e, conv_a_w, conv_a_b, ln_a_g, ln_a_b, lam_q1, lam_k1, lam_q2, lam_k2, subln_g, w_out_e,
              w_in_o, pool_w, pool_scale, conv_d_w, w_out_o, w_router_g, b_router_g, w_router_e, b_router_e,
              w_gate, w_up, w_down):
    p = {
        'w_ada': w_ada, 'b_ada': b_ada, 'g_norm1': g_norm1, 'g_norm2': g_norm2, 'g_final': g_final,
        'w_in_e': w_in_e, 'conv_a_w': conv_a_w, 'conv_a_b': conv_a_b, 'ln_a_g': ln_a_g, 'ln_a_b': ln_a_b,
        'lam_q1': lam_q1, 'lam_k1': lam_k1, 'lam_q2': lam_q2, 'lam_k2': lam_k2, 'subln_g': subln_g,
        'w_out_e': w_out_e, 'w_in_o': w_in_o, 'pool_w': pool_w, 'pool_scale': pool_scale,
        'conv_d_w': conv_d_w, 'w_out_o': w_out_o, 'w_router_g': w_router_g, 'b_router_g': b_router_g,
        'w_router_e': w_router_e, 'b_router_e': b_router_e, 'w_gate': w_gate, 'w_up': w_up, 'w_down': w_down,
    }
    y_prompt, new_k, new_v = run_trunk(x_prompt, jax.nn.silu(c_ctx)[None, :], p, None, None, None)
    new_cache_k = jnp.stack(new_k, axis=1)
    new_cache_v = jnp.stack(new_v, axis=1)
    rope = axial_rope_tables(x_sample.shape[1])
    y_sample, _, _ = run_trunk(x_sample, jax.nn.silu(c), p, rope, cache_k, cache_v)
    return (y_prompt, y_sample, new_cache_k, new_cache_v)
```

```python
import functools
import math

import jax
import jax.numpy as jnp
from jax import lax
from jax.experimental import pallas as pl
from jax.experimental.pallas import tpu as pltpu

F32 = jnp.float32
BF16 = jnp.bfloat16

EPS = 1e-6
GRID_W = 64
ROPE_BASE = 10000.0
CONV_K = 31
SCONV_K = 3
POOL_WINDOWS = (2, 4, 8, 16)
DIFF_HEADS = 4
N_GROUPS = 4
EXPERTS_PER_GROUP = 8
N_EXPERTS = N_GROUPS * EXPERTS_PER_GROUP
TOP_K = 2

LANES = 128
TM = 256
HALO = 16
MOE_BLK = 256
ROUTER_PAD = 128
VMEM_LIMIT = 56 * 1024 * 1024


def _cparams(sem):
    return pltpu.CompilerParams(dimension_semantics=sem, vmem_limit_bytes=VMEM_LIMIT)


def _rms_mod(x, g, sc, sh):
    ms = jnp.mean(x * x, axis=-1, keepdims=True)
    return (x * lax.rsqrt(ms + EPS)) * g * (1.0 + sc) + sh


def _silu(x):
    return x * jax.nn.sigmoid(x)


def _ada_kernel(c_ref, w_ref, b_ref, o_ref):
    cs = _silu(c_ref[...])
    o_ref[0] = jnp.dot(cs, w_ref[0], precision=lax.Precision.HIGHEST,
                       preferred_element_type=F32) + b_ref[0]


def _ada_table(cond, w_ada, b_ada):
    depth, d, six_d = w_ada.shape
    rows = cond.shape[0]
    nj = six_d // d
    out = pl.pallas_call(
        _ada_kernel,
        grid=(depth, nj),
        in_specs=[pl.BlockSpec((rows, d), lambda l, j: (0, 0)),
                  pl.BlockSpec((1, d, d), lambda l, j: (l, 0, j)),
                  pl.BlockSpec((1, 1, d), lambda l, j: (l, 0, j))],
        out_specs=pl.BlockSpec((1, rows, d), lambda l, j: (l, 0, j)),
        out_shape=jax.ShapeDtypeStruct((depth, rows, six_d), F32),
        compiler_params=_cparams(("parallel", "parallel")),
        name="ada_table",
    )(cond, w_ada, b_ada.reshape(depth, 1, six_d))
    return out.reshape(depth, rows, nj, d)


def _rope(x, cos, sins, first_half):
    outs = []
    for j in range(x.shape[1] // LANES):
        sl = slice(j * LANES, (j + 1) * LANES)
        xs = x[:, sl]
        nxt = pltpu.roll(xs, LANES - 16, 1)
        prv = pltpu.roll(xs, 16, 1)
        rot = jnp.where(first_half, nxt, prv)
        outs.append(xs * cos[:, sl] + rot * sins[:, sl])
    return jnp.concatenate(outs, axis=1)


def _pre_even_kernel(x_ref, mod_ref, g_ref, w_ref, cos_ref, sin_ref,
                     u_ref, q_ref, k_ref, v_ref, kc_ref, vc_ref, *, npt, cw, qw, qscale):
    i = pl.program_id(0)
    h = _rms_mod(x_ref[...], g_ref[...], mod_ref[0, 1:2, :], mod_ref[0, 0:1, :])
    proj = jnp.dot(h.astype(BF16), w_ref[...], preferred_element_type=F32)
    ga = proj[:, :cw]
    gb = proj[:, cw:2 * cw]
    u_ref[...] = ga * jax.nn.sigmoid(gb)
    q = proj[:, 2 * cw:2 * cw + qw]
    k = proj[:, 2 * cw + qw:2 * cw + 2 * qw]
    v = proj[:, 2 * cw + 2 * qw:]

    @pl.when(i < npt)
    def _():
        kc_ref[...] = k
        vc_ref[...] = v

    lane = lax.broadcasted_iota(jnp.int32, (1, LANES), 1)
    first_half = (lane % 32) < 16
    cos = cos_ref[...]
    sins = sin_ref[...]
    q_ref[...] = (_rope(q, cos, sins, first_half) * qscale).astype(BF16)
    k_ref[...] = _rope(k, cos, sins, first_half).astype(BF16)
    v_ref[...] = v.astype(BF16)


def _attn_kernel(lam_ref, q_ref, k_ref, v_ref, g_ref, o_ref, m_sc, l_sc, acc_sc, *, nk, tk, post_scale):
    q = q_ref[...]
    dh = q.shape[1] // 2
    lane = lax.broadcasted_iota(jnp.int32, (1, q.shape[1]), 1)
    zero = jnp.zeros_like(q)
    qs = (jnp.where(lane < dh, q, zero), jnp.where(lane >= dh, q, zero))
    m_sc[...] = jnp.full(m_sc.shape, -jnp.inf, F32)
    l_sc[...] = jnp.zeros(l_sc.shape, F32)
    acc_sc[...] = jnp.zeros(acc_sc.shape, F32)

    def body(j, carry):
        off = pl.multiple_of(j * tk, tk)
        kk = k_ref[pl.ds(off, tk), :]
        vv = v_ref[pl.ds(off, tk), :]
        for c in range(2):
            s = lax.dot_general(qs[c], kk, (((1,), (1,)), ((), ())), preferred_element_type=F32)
            m_prev = m_sc[c]
            m_new = jnp.maximum(m_prev, jnp.max(s, axis=1, keepdims=True))
            alpha = jnp.exp(m_prev - m_new)
            p = jnp.exp(s - m_new)
            l_sc[c] = alpha * l_sc[c] + jnp.sum(p, axis=1, keepdims=True)
            acc_sc[c] = alpha * acc_sc[c] + jnp.dot(p.astype(BF16), vv, preferred_element_type=F32)
            m_sc[c] = m_new
        return carry

    lax.fori_loop(0, nk, body, 0)
    o = acc_sc[0] / l_sc[0] - lam_ref[0] * (acc_sc[1] / l_sc[1])
    ms = jnp.mean(o * o, axis=-1, keepdims=True)
    o_ref[...] = (o * lax.rsqrt(ms + EPS) * g_ref[...] * post_scale).astype(BF16)


def _attention(lam, q, k, v, g, *, batch, t_len, s_len, q_row0, kv_per_batch, tq, tk, post_scale):
    hw = q.shape[1] // DIFF_HEADS
    nq = t_len // tq
    assert q_row0 % tq == 0 and s_len % tk == 0 and kv_per_batch
    qb0 = q_row0 // tq
    return pl.pallas_call(
        functools.partial(_attn_kernel, nk=s_len // tk, tk=tk, post_scale=post_scale),
        grid=(batch, DIFF_HEADS, nq),
        in_specs=[pl.BlockSpec(memory_space=pltpu.SMEM),
                  pl.BlockSpec((tq, hw), lambda b, h, i: (qb0 + b * nq + i, h)),
                  pl.BlockSpec((s_len, hw), lambda b, h, i: (b, h)),
                  pl.BlockSpec((s_len, hw), lambda b, h, i: (b, h)),
                  pl.BlockSpec((1, hw), lambda b, h, i: (0, 0))],
        out_specs=pl.BlockSpec((tq, hw), lambda b, h, i: (b * nq + i, h)),
        out_shape=jax.ShapeDtypeStruct((batch * t_len, q.shape[1]), BF16),
        scratch_shapes=[pltpu.VMEM((2, tq, 1), F32), pltpu.VMEM((2, tq, 1), F32),
                        pltpu.VMEM((2, tq, hw), F32)],
        compiler_params=_cparams(("parallel", "parallel", "parallel")),
        name="diff_attention",
    )(lam, q, k, v, g)


def _seq_edges(i, npt, tps):
    r = (i - npt) % tps
    is_prompt = i < npt
    first = jnp.logical_or(is_prompt, r == 0)
    last = jnp.logical_or(is_prompt, r == tps - 1)
    return first, last


def _fill_ext(ext_sc, cur_ref, prev_ref, next_ref, first, last):
    prev = prev_ref[...]
    nxt = next_ref[...]
    ext_sc[0:HALO, :] = jnp.where(first, jnp.zeros_like(prev), prev)
    ext_sc[HALO:HALO + TM, :] = cur_ref[...]
    ext_sc[HALO + TM:, :] = jnp.where(last, jnp.zeros_like(nxt), nxt)


def _moe_prologue(x_new, mod_ref, g2_ref, wr_ref, h2_ref, lg_ref):
    h2 = _rms_mod(x_new, g2_ref[...], mod_ref[0, 4:5, :], mod_ref[0, 3:4, :]).astype(BF16)
    h2_ref[...] = h2
    lg_ref[...] = jnp.dot(h2, wr_ref[...], preferred_element_type=F32)


def _post_even_kernel(x_ref, u_ref, up_ref, un_ref, o_ref, mod_ref, cw_ref, cb_ref, lng_ref, lnb_ref,
                      wo_ref, g2_ref, wr_ref, x1_ref, h2_ref, lg_ref, ext_sc, *, npt, tps):
    i = pl.program_id(0)
    first, last = _seq_edges(i, npt, tps)
    _fill_ext(ext_sc, u_ref, up_ref, un_ref, first, last)
    cw = u_ref.shape[1]
    rows = 32
    chunks = []
    for rc in range(TM // rows):
        acc = None
        for k in range(CONV_K):
            start = rc * rows + HALO - CONV_K // 2 + k
            term = ext_sc[start:start + rows, :] * cw_ref[k:k + 1, :]
            acc = term if acc is None else acc + term
        chunks.append(acc)
    conv = jnp.concatenate(chunks, axis=0) + cb_ref[...]
    mu = jnp.mean(conv, axis=-1, keepdims=True)
    cen = conv - mu
    var = jnp.mean(cen * cen, axis=-1, keepdims=True)
    uu = _silu(cen * lax.rsqrt(var + EPS) * lng_ref[...] + lnb_ref[...])
    m = (jnp.dot(uu.astype(BF16), wo_ref[0:cw, :], preferred_element_type=F32)
         + jnp.dot(o_ref[...], wo_ref[cw:, :], preferred_element_type=F32))
    x1 = x_ref[...] + mod_ref[0, 2:3, :] * m
    x1_ref[...] = x1
    _moe_prologue(x1, mod_ref, g2_ref, wr_ref, h2_ref, lg_ref)


def _pre_odd_kernel(x_ref, y_ref, modp_ref, mod_ref, g_ref, w_ref, x2_ref, xc_ref, p_ref, bg_ref, *, cw):
    x2 = x_ref[...] + modp_ref[0, 5:6, :] * y_ref[...]
    x2_ref[...] = x2
    h = _rms_mod(x2, g_ref[...], mod_ref[0, 1:2, :], mod_ref[0, 0:1, :])
    proj = jnp.dot(h.astype(BF16), w_ref[...], preferred_element_type=F32)
    xc_ref[...] = proj[:, :cw]
    hd = proj[:, cw:2 * cw]
    bg_ref[...] = proj[:, 2 * cw:3 * cw]
    p_ref[...] = proj[:, 3 * cw:] * hd


def _post_odd_kernel(x_ref, xc_ref, xcp_ref, xcn_ref, p_ref, pp_ref, pn_ref, bg_ref, mod_ref,
                     pw_ref, ps_ref, dw_ref, wo_ref, g2_ref, wr_ref,
                     x3_ref, h2_ref, lg_ref, extc_sc, extp_sc, *, npt, tps):
    i = pl.program_id(0)
    first, last = _seq_edges(i, npt, tps)
    _fill_ext(extc_sc, xc_ref, xcp_ref, xcn_ref, first, last)
    _fill_ext(extp_sc, p_ref, pp_ref, pn_ref, first, last)
    cw = xc_ref.shape[1]
    gc = cw // len(POOL_WINDOWS)
    is_prompt = i < npt
    t_len = jnp.where(is_prompt, TM, TM * tps)
    pos = lax.broadcasted_iota(jnp.int32, (TM, 1), 0) + jnp.where(is_prompt, 0, ((i - npt) % tps) * TM)
    ds = []
    for g, w in enumerate(POOL_WINDOWS):
        sl = slice(g * gc, (g + 1) * gc)
        acc = None
        for o in range(-(w // 2), w // 2):
            term = extc_sc[HALO + o:HALO + o + TM, sl]
            acc = term if acc is None else acc + term
        lo = jnp.maximum(pos - w // 2, 0)
        hi = jnp.minimum(pos + w // 2 - 1, t_len - 1)
        cnt = (hi - lo + 1).astype(F32)
        ds.append(acc / cnt - extc_sc[HALO:HALO + TM, sl])
    d = jnp.concatenate(ds, axis=1).astype(BF16)
    yc = jnp.dot(d, pw_ref[...], preferred_element_type=F32) * ps_ref[...]
    conv = None
    for k in range(SCONV_K):
        start = HALO - SCONV_K // 2 + k
        term = extp_sc[start:start + TM, :] * dw_ref[k:k + 1, :]
        conv = term if conv is None else conv + term
    yd = bg_ref[...] * conv
    m = (jnp.dot(yc.astype(BF16), wo_ref[0:cw, :], preferred_element_type=F32)
         + jnp.dot(yd.astype(BF16), wo_ref[cw:, :], preferred_element_type=F32))
    x3 = x_ref[...] + mod_ref[0, 2:3, :] * m
    x3_ref[...] = x3
    _moe_prologue(x3, mod_ref, g2_ref, wr_ref, h2_ref, lg_ref)


def _final_kernel(x_ref, y_ref, modp_ref, g_ref, op_ref, os_ref, *, npt):
    i = pl.program_id(0)
    x = x_ref[...] + modp_ref[0, 5:6, :] * y_ref[...]
    ms = jnp.mean(x * x, axis=-1, keepdims=True)
    out = x * lax.rsqrt(ms + EPS) * g_ref[...]

    @pl.when(i < npt)
    def _():
        op_ref[...] = out

    @pl.when(i >= npt)
    def _():
        os_ref[...] = out


def _expert_kernel(be_ref, nv_ref, x_ref, w_ref, wg_ref, wu_ref, wd_ref, y_ref):
    i = pl.program_id(0)

    @pl.when(i < nv_ref[0])
    def _():
        x = x_ref[...]
        hg = jnp.dot(x, wg_ref[0], preferred_element_type=F32)
        hu = jnp.dot(x, wu_ref[0], preferred_element_type=F32)
        hid = (_silu(hg) * hu).astype(BF16)
        y_ref[...] = jnp.dot(hid, wd_ref[0], preferred_element_type=F32) * w_ref[...]

    @pl.when(i >= nv_ref[0])
    def _():
        y_ref[...] = jnp.zeros(y_ref.shape, F32)


def _experts(blk_e, nvalid, xb, row_w, wg, wu, wd):
    p_len, d = xb.shape
    hid = wg.shape[2]
    nblk = p_len // MOE_BLK
    grid_spec = pltpu.PrefetchScalarGridSpec(
        num_scalar_prefetch=2,
        grid=(nblk,),
        in_specs=[pl.BlockSpec((MOE_BLK, d), lambda i, be, nv: (i, 0)),
                  pl.BlockSpec((MOE_BLK, 1), lambda i, be, nv: (i, 0)),
                  pl.BlockSpec((1, d, hid), lambda i, be, nv: (be[i], 0, 0)),
                  pl.BlockSpec((1, d, hid), lambda i, be, nv: (be[i], 0, 0)),
                  pl.BlockSpec((1, hid, d), lambda i, be, nv: (be[i], 0, 0))],
        out_specs=pl.BlockSpec((MOE_BLK, d), lambda i, be, nv: (i, 0)),
    )
    return pl.pallas_call(
        _expert_kernel,
        grid_spec=grid_spec,
        out_shape=jax.ShapeDtypeStruct((p_len, d), F32),
        compiler_params=_cparams(("arbitrary",)),
        name="moe_experts",
    )(blk_e, nvalid, xb, row_w, wg, wu, wd)


def _moe(h2, logits, b_rg, b_re, wg, wu, wd):
    n, d = h2.shape
    gp = jax.nn.softmax(logits[:, :N_GROUPS] + b_rg, axis=-1)
    g_idx = jnp.argmax(gp, axis=-1).astype(jnp.int32)
    g_p = jnp.take_along_axis(gp, g_idx[:, None], axis=-1)[:, 0]
    el_all = logits[:, N_GROUPS:N_GROUPS + N_EXPERTS].reshape(n, N_GROUPS, EXPERTS_PER_GROUP) + b_re
    el = jnp.take_along_axis(el_all, g_idx[:, None, None], axis=1)[:, 0]
    top_v, top_i = lax.top_k(el, TOP_K)
    wts = jax.nn.softmax(top_v, axis=-1) * g_p[:, None]
    eid = g_idx[:, None] * EXPERTS_PER_GROUP + top_i.astype(jnp.int32)

    a_len = n * TOP_K
    flat_e = eid.reshape(-1)
    flat_w = wts.reshape(-1)
    order = jnp.argsort(flat_e)
    counts = jnp.zeros((N_EXPERTS,), jnp.int32).at[flat_e].add(1)
    pc = ((counts + MOE_BLK - 1) // MOE_BLK) * MOE_BLK
    start = jnp.cumsum(counts) - counts
    pend = jnp.cumsum(pc)
    pstart = pend - pc
    se = flat_e[order]
    dest = pstart[se] + (jnp.arange(a_len, dtype=jnp.int32) - start[se])
    p_len = a_len + N_EXPERTS * MOE_BLK
    nblk = p_len // MOE_BLK
    row_tok = jnp.zeros((p_len,), jnp.int32).at[dest].set((order // TOP_K).astype(jnp.int32))
    row_w = jnp.zeros((p_len,), F32).at[dest].set(flat_w[order])
    pos = jnp.zeros((a_len,), jnp.int32).at[order].set(dest)
    blk_e = jnp.minimum(jnp.searchsorted(pend, jnp.arange(nblk, dtype=jnp.int32) * MOE_BLK, side='right'),
                        N_EXPERTS - 1).astype(jnp.int32)
    nvalid = (pend[-1:] // MOE_BLK).astype(jnp.int32)

    xb = h2[row_tok]
    yb = _experts(blk_e, nvalid, xb, row_w[:, None], wg, wu, wd)
    pos2 = pos.reshape(n, TOP_K)
    return yb[pos2[:, 0]] + yb[pos2[:, 1]]


def _rope_tables(t_len, qw):
    rows = t_len // GRID_W
    ax = qw // (DIFF_HEADS * 2) // 2
    row = jnp.repeat(jnp.arange(rows, dtype=F32), GRID_W)
    col = jnp.tile(jnp.arange(GRID_W, dtype=F32), rows)
    inv = ROPE_BASE ** (-jnp.arange(0, ax, 2, dtype=F32) / ax)
    ar = row[:, None] * inv[None, :]
    ac = col[:, None] * inv[None, :]
    cr, sr, cc, sc = jnp.cos(ar), jnp.sin(ar), jnp.cos(ac), jnp.sin(ac)
    cos = jnp.concatenate([cr, cr, cc, cc], axis=1)
    sins = jnp.concatenate([-sr, sr, -sc, sc], axis=1)
    reps = qw // cos.shape[1]
    cos = jnp.concatenate([jnp.ones((TM, cos.shape[1]), F32), cos], axis=0)
    sins = jnp.concatenate([jnp.zeros((TM, sins.shape[1]), F32), sins], axis=0)
    return jnp.tile(cos, (1, reps)), jnp.tile(sins, (1, reps))


def kernel(x_prompt, x_sample, cache_k, cache_v, c, c_ctx, w_ada, b_ada, g_norm1, g_norm2, g_final, w_in_e, conv_a_w, conv_a_b, ln_a_g, ln_a_b, lam_q1, lam_k1, lam_q2, lam_k2, subln_g, w_out_e, w_in_o, pool_w, pool_scale, conv_d_w, w_out_o, w_router_g, b_router_g, w_router_e, b_router_e, w_gate, w_up, w_down):
    bp, tp, d = x_prompt.shape
    bs, ts, _ = x_sample.shape
    past = cache_k.shape[2]
    assert tp == TM and ts % TM == 0 and d % LANES == 0
    npt = bp
    tps = ts // TM
    n_p, n_s = bp * tp, bs * ts
    n = n_p + n_s
    nt = n // TM
    cw = conv_a_w.shape[2]
    qw = (w_in_e.shape[2] - 2 * cw) // 3
    dh = qw // (DIFF_HEADS * 2)
    hpt = TM // HALO

    def cond_of(i):
        return jnp.where(i < npt, 0, 1 + (i - npt) // tps)

    def tile(w):
        return pl.BlockSpec((TM, w), lambda i: (i, 0))

    def halo_prev(w):
        return pl.BlockSpec((HALO, w), lambda i: (jnp.maximum(i * hpt - 1, 0), 0))

    def halo_next(w):
        return pl.BlockSpec((HALO, w), lambda i: (jnp.minimum((i + 1) * hpt, nt * hpt - 1), 0))

    def whole(shape):
        return pl.BlockSpec(shape, lambda i: (0,) * len(shape))

    mod_spec = pl.BlockSpec((1, 6, d), lambda i: (cond_of(i), 0, 0))

    rows = 8 * ((1 + bs + 7) // 8)
    cond = jnp.concatenate([c_ctx[None, :], c, jnp.zeros((rows - 1 - bs, d), F32)], axis=0)
    mod = _ada_table(cond, w_ada, b_ada)

    x = jnp.concatenate([x_prompt.reshape(n_p, d), x_sample.reshape(n_s, d)], axis=0)

    def router_w(i):
        wre = jnp.transpose(w_router_e[i], (1, 0, 2)).reshape(d, N_EXPERTS)
        wr = jnp.concatenate([w_router_g[i], wre], axis=1)
        return jnp.pad(wr, ((0, 0), (0, ROUTER_PAD - wr.shape[1]))).astype(BF16)

    cos, sins = _rope_tables(ts, qw)
    tab_spec = pl.BlockSpec((TM, qw), lambda i: (jnp.where(i < npt, 0, 1 + (i - npt) % tps), 0))
    cache_spec = pl.BlockSpec((TM, qw), lambda i: (jnp.minimum(i, npt - 1), 0))
    u, q, k, v, kc, vc = pl.pallas_call(
        functools.partial(_pre_even_kernel, npt=npt, cw=cw, qw=qw, qscale=dh ** -0.5),
        grid=(nt,),
        in_specs=[tile(d), mod_spec, whole((1, d)), whole(w_in_e.shape[1:]), tab_spec, tab_spec],
        out_specs=[tile(cw), tile(qw), tile(qw), tile(qw), cache_spec, cache_spec],
        out_shape=[jax.ShapeDtypeStruct((n, cw), F32), jax.ShapeDtypeStruct((n, qw), BF16),
                   jax.ShapeDtypeStruct((n, qw), BF16), jax.ShapeDtypeStruct((n, qw), BF16),
                   jax.ShapeDtypeStruct((n_p, qw), F32), jax.ShapeDtypeStruct((n_p, qw), F32)],
        compiler_params=_cparams(("arbitrary",)),
        name="pre_even",
    )(x, mod[0], g_norm1[0][None, :], w_in_e[0].astype(BF16), cos, sins)

    lam_init = 0.8 - 0.6 * math.exp(-0.3 * 0)
    lam = (jnp.exp(jnp.sum(lam_q1[0] * lam_k1[0])) - jnp.exp(jnp.sum(lam_q2[0] * lam_k2[0])) + lam_init)
    lam = lam.reshape(1).astype(F32)
    sub_g = subln_g[0][None, :]
    o_p = _attention(lam, q, k, v, sub_g, batch=bp, t_len=tp, s_len=tp, q_row0=0, kv_per_batch=True,
                     tq=TM, tk=TM, post_scale=1.0 - lam_init)
    kcat = jnp.concatenate([cache_k[:, 0].reshape(bs, past, qw).astype(BF16),
                            k[n_p:].reshape(bs, ts, qw)], axis=1).reshape(bs * (past + ts), qw)
    vcat = jnp.concatenate([cache_v[:, 0].reshape(bs, past, qw).astype(BF16),
                            v[n_p:].reshape(bs, ts, qw)], axis=1).reshape(bs * (past + ts), qw)
    o_s = _attention(lam, q, kcat, vcat, sub_g, batch=bs, t_len=ts, s_len=past + ts, q_row0=n_p,
                     kv_per_batch=True, tq=512, tk=512, post_scale=1.0 - lam_init)
    o = jnp.concatenate([o_p, o_s], axis=0)

    moe_outs = [jax.ShapeDtypeStruct((n, d), F32), jax.ShapeDtypeStruct((n, d), BF16),
                jax.ShapeDtypeStruct((n, ROUTER_PAD), F32)]
    moe_out_specs = [tile(d), tile(d), tile(ROUTER_PAD)]
    x1, h2, logits = pl.pallas_call(
        functools.partial(_post_even_kernel, npt=npt, tps=tps),
        grid=(nt,),
        in_specs=[tile(d), tile(cw), halo_prev(cw), halo_next(cw), tile(qw), mod_spec,
                  whole((CONV_K, cw)), whole((1, cw)), whole((1, cw)), whole((1, cw)),
                  whole(w_out_e.shape[1:]), whole((1, d)), whole((d, ROUTER_PAD))],
        out_specs=moe_out_specs,
        out_shape=moe_outs,
        scratch_shapes=[pltpu.VMEM((TM + 2 * HALO, cw), F32)],
        compiler_params=_cparams(("parallel",)),
        name="post_even",
    )(x, u, u, u, o, mod[0], conv_a_w[0], conv_a_b[0][None, :], ln_a_g[0][None, :], ln_a_b[0][None, :],
      w_out_e[0].astype(BF16), g_norm2[0][None, :], router_w(0))
    y = _moe(h2, logits, b_router_g[0], b_router_e[0],
             w_gate[0].astype(BF16), w_up[0].astype(BF16), w_down[0].astype(BF16))

    pw = pool_w.shape[2]
    x2, xc, p, bg = pl.pallas_call(
        functools.partial(_pre_odd_kernel, cw=cw),
        grid=(nt,),
        in_specs=[tile(d), tile(d), mod_spec, mod_spec, whole((1, d)), whole(w_in_o.shape[1:])],
        out_specs=[tile(d), tile(cw), tile(cw), tile(cw)],
        out_shape=[jax.ShapeDtypeStruct((n, d), F32)] + [jax.ShapeDtypeStruct((n, cw), F32)] * 3,
        compiler_params=_cparams(("parallel",)),
        name="pre_odd",
    )(x1, y, mod[0], mod[1], g_norm1[1][None, :], w_in_o[0].astype(BF16))
    pool_bd = jnp.zeros((cw, cw), F32)
    for g in range(len(POOL_WINDOWS)):
        pool_bd = pool_bd.at[g * pw:(g + 1) * pw, g * pw:(g + 1) * pw].set(pool_w[0, g])
    x3, h2, logits = pl.pallas_call(
        functools.partial(_post_odd_kernel, npt=npt, tps=tps),
        grid=(nt,),
        in_specs=[tile(d), tile(cw), halo_prev(cw), halo_next(cw), tile(cw), halo_prev(cw), halo_next(cw),
                  tile(cw), mod_spec, whole((cw, cw)), whole((1, cw)), whole((SCONV_K, cw)),
                  whole(w_out_o.shape[1:]), whole((1, d)), whole((d, ROUTER_PAD))],
        out_specs=moe_out_specs,
        out_shape=moe_outs,
        scratch_shapes=[pltpu.VMEM((TM + 2 * HALO, cw), F32), pltpu.VMEM((TM + 2 * HALO, cw), F32)],
        compiler_params=_cparams(("parallel",)),
        name="post_odd",
    )(x2, xc, xc, xc, p, p, p, bg, mod[1], pool_bd.astype(BF16), pool_scale[0][None, :], conv_d_w[0],
      w_out_o[0].astype(BF16), g_norm2[1][None, :], router_w(1))
    y = _moe(h2, logits, b_router_g[1], b_router_e[1],
             w_gate[1].astype(BF16), w_up[1].astype(BF16), w_down[1].astype(BF16))

    out_p, out_s = pl.pallas_call(
        functools.partial(_final_kernel, npt=npt),
        grid=(nt,),
        in_specs=[tile(d), tile(d), mod_spec, whole((1, d))],
        out_specs=[pl.BlockSpec((TM, d), lambda i: (jnp.minimum(i, npt - 1), 0)),
                   pl.BlockSpec((TM, d), lambda i: (jnp.maximum(i - npt, 0), 0))],
        out_shape=[jax.ShapeDtypeStruct((n_p, d), F32), jax.ShapeDtypeStruct((n_s, d), F32)],
        compiler_params=_cparams(("arbitrary",)),
        name="final_norm",
    )(x3, y, mod[1], g_final[None, :])

    new_k = kc.reshape(bp, 1, tp, DIFF_HEADS, 2, dh)
    new_v = vc.reshape(bp, 1, tp, DIFF_HEADS, 2 * dh)
    return (out_p.reshape(bp, tp, d), out_s.reshape(bs, ts, d), new_k, new_v)
```

```python
import functools
import math

import jax
import jax.numpy as jnp
from jax import lax
from jax.experimental import pallas as pl
from jax.experimental.pallas import tpu as pltpu

F32 = jnp.float32
BF16 = jnp.bfloat16

EPS = 1e-6
GRID_W = 64
ROPE_BASE = 10000.0
CONV_K = 31
SCONV_K = 3
POOL_WINDOWS = (2, 4, 8, 16)
DIFF_HEADS = 4
N_GROUPS = 4
EXPERTS_PER_GROUP = 8
N_EXPERTS = N_GROUPS * EXPERTS_PER_GROUP
TOP_K = 2

LANES = 128
TM = 256
HALO = 16
MOE_BLK = 256
ROUTER_PAD = 128
VMEM_LIMIT = 56 * 1024 * 1024


def _cparams(sem):
    return pltpu.CompilerParams(dimension_semantics=sem, vmem_limit_bytes=VMEM_LIMIT)


def _rms_mod(x, g, sc, sh):
    ms = jnp.mean(x * x, axis=-1, keepdims=True)
    return (x * lax.rsqrt(ms + EPS)) * g * (1.0 + sc) + sh


def _silu(x):
    return x * jax.nn.sigmoid(x)


def _ada_kernel(c_ref, w_ref, b_ref, o_ref):
    cs = _silu(c_ref[...])
    o_ref[0] = jnp.dot(cs, w_ref[0], precision=lax.Precision.HIGHEST,
                       preferred_element_type=F32) + b_ref[0]


def _ada_table(cond, w_ada, b_ada):
    depth, d, six_d = w_ada.shape
    rows = cond.shape[0]
    nj = six_d // d
    out = pl.pallas_call(
        _ada_kernel,
        grid=(depth, nj),
        in_specs=[pl.BlockSpec((rows, d), lambda l, j: (0, 0)),
                  pl.BlockSpec((1, d, d), lambda l, j: (l, 0, j)),
                  pl.BlockSpec((1, 1, d), lambda l, j: (l, 0, j))],
        out_specs=pl.BlockSpec((1, rows, d), lambda l, j: (l, 0, j)),
        out_shape=jax.ShapeDtypeStruct((depth, rows, six_d), F32),
        compiler_params=_cparams(("parallel", "parallel")),
        name="ada_table",
    )(cond, w_ada, b_ada.reshape(depth, 1, six_d))
    return out.reshape(depth, rows, nj, d)


def _rope(x, cos, sins, first_half):
    outs = []
    for j in range(x.shape[1] // LANES):
        sl = slice(j * LANES, (j + 1) * LANES)
        xs = x[:, sl]
        nxt = pltpu.roll(xs, LANES - 16, 1)
        prv = pltpu.roll(xs, 16, 1)
        rot = jnp.where(first_half, nxt, prv)
        outs.append(xs * cos[:, sl] + rot * sins[:, sl])
    return jnp.concatenate(outs, axis=1)


def _pre_even_kernel(x_ref, mod_ref, g_ref, w_ref, cos_ref, sin_ref,
                     u_ref, q_ref, k_ref, vt_ref, kc_ref, vc_ref, *, npt, cw, qw, qscale):
    i = pl.program_id(0)
    h = _rms_mod(x_ref[...], g_ref[...], mod_ref[0, 1:2, :], mod_ref[0, 0:1, :])
    proj = jnp.dot(h.astype(BF16), w_ref[...], preferred_element_type=F32)
    ga = proj[:, :cw]
    gb = proj[:, cw:2 * cw]
    u_ref[...] = ga * jax.nn.sigmoid(gb)
    q = proj[:, 2 * cw:2 * cw + qw]
    k = proj[:, 2 * cw + qw:2 * cw + 2 * qw]
    v = proj[:, 2 * cw + 2 * qw:]

    @pl.when(i < npt)
    def _():
        kc_ref[...] = k
        vc_ref[...] = v

    lane = lax.broadcasted_iota(jnp.int32, (1, LANES), 1)
    first_half = (lane % 32) < 16
    cos = cos_ref[...]
    sins = sin_ref[...]
    q_ref[...] = (_rope(q, cos, sins, first_half) * qscale).astype(BF16)
    k_ref[...] = _rope(k, cos, sins, first_half).astype(BF16)
    vt_ref[...] = v.T.astype(BF16)


def _attn_kernel(lam_ref, q_ref, k_ref, vt_ref, *rest, nk, tk, nk_ctx, post_scale):
    if nk_ctx:
        kc_ref, vtc_ref, g_ref, o_ref, m_sc, l_sc, acc_sc = rest
    else:
        g_ref, o_ref, m_sc, l_sc, acc_sc = rest
    q = q_ref[...]
    dh = q.shape[1] // 2
    lane = lax.broadcasted_iota(jnp.int32, (1, q.shape[1]), 1)
    zero = jnp.zeros_like(q)
    qs = (jnp.where(lane < dh, q, zero), jnp.where(lane >= dh, q, zero))
    m_sc[...] = jnp.full(m_sc.shape, -jnp.inf, F32)
    l_sc[...] = jnp.zeros(l_sc.shape, F32)
    acc_sc[...] = jnp.zeros(acc_sc.shape, F32)

    def chunk(kk, vt):
        for c in range(2):
            s = lax.dot_general(kk, qs[c], (((1,), (1,)), ((), ())), preferred_element_type=F32)
            m_prev = m_sc[c]
            m_new = jnp.maximum(m_prev, jnp.max(s, axis=0, keepdims=True))
            alpha = jnp.exp2(m_prev - m_new)
            p = jnp.exp2(s - m_new)
            l_sc[c] = alpha * l_sc[c] + jnp.sum(p, axis=0, keepdims=True)
            acc_sc[c] = alpha * acc_sc[c] + jnp.dot(vt, p.astype(BF16), preferred_element_type=F32)
            m_sc[c] = m_new

    for j in range(nk_ctx):
        chunk(kc_ref[j * tk:(j + 1) * tk, :], vtc_ref[0, :, j * tk:(j + 1) * tk])

    def body(j, carry):
        off = pl.multiple_of(j * tk, tk)
        chunk(k_ref[pl.ds(off, tk), :], vt_ref[:, pl.ds(off, tk)])
        return carry

    lax.fori_loop(0, nk, body, 0)
    o = acc_sc[0] / l_sc[0] - lam_ref[0] * (acc_sc[1] / l_sc[1])
    ms = jnp.mean(o * o, axis=0, keepdims=True)
    o = o * lax.rsqrt(ms + EPS) * g_ref[...] * post_scale
    o_ref[...] = o.T.astype(BF16)


def _attention(lam, q, k, vt, g_col, ctx, *, batch, t_len, row0, tq, tk, post_scale):
    hw = q.shape[1] // DIFF_HEADS
    nq = t_len // tq
    assert row0 % t_len == 0 and t_len % tk == 0
    qb0 = row0 // tq
    sb0 = row0 // t_len
    in_specs = [pl.BlockSpec(memory_space=pltpu.SMEM),
                pl.BlockSpec((tq, hw), lambda b, h, i: (qb0 + b * nq + i, h)),
                pl.BlockSpec((t_len, hw), lambda b, h, i: (sb0 + b, h)),
                pl.BlockSpec((hw, t_len), lambda b, h, i: (h, sb0 + b))]
    args = [lam, q, k, vt]
    nk_ctx = 0
    if ctx is not None:
        k_ctx, vt_ctx = ctx
        past = vt_ctx.shape[2]
        assert past % tk == 0
        nk_ctx = past // tk
        in_specs += [pl.BlockSpec((past, hw), lambda b, h, i: (b, h)),
                     pl.BlockSpec((1, hw, past), lambda b, h, i: (b, h, 0))]
        args += [k_ctx, vt_ctx]
    in_specs.append(pl.BlockSpec((hw, 1), lambda b, h, i: (0, 0)))
    args.append(g_col)
    return pl.pallas_call(
        functools.partial(_attn_kernel, nk=t_len // tk, tk=tk, nk_ctx=nk_ctx, post_scale=post_scale),
        grid=(batch, DIFF_HEADS, nq),
        in_specs=in_specs,
        out_specs=pl.BlockSpec((tq, hw), lambda b, h, i: (b * nq + i, h)),
        out_shape=jax.ShapeDtypeStruct((batch * t_len, q.shape[1]), BF16),
        scratch_shapes=[pltpu.VMEM((2, 1, tq), F32), pltpu.VMEM((2, 1, tq), F32),
                        pltpu.VMEM((2, hw, tq), F32)],
        compiler_params=_cparams(("parallel", "parallel", "parallel")),
        name="diff_attention",
    )(*args)


def _seq_edges(i, npt, tps):
    r = (i - npt) % tps
    is_prompt = i < npt
    first = jnp.logical_or(is_prompt, r == 0)
    last = jnp.logical_or(is_prompt, r == tps - 1)
    return first, last


def _fill_ext(ext_sc, cur_ref, prev_ref, next_ref, first, last):
    prev = prev_ref[...]
    nxt = next_ref[...]
    ext_sc[0:HALO, :] = jnp.where(first, jnp.zeros_like(prev), prev)
    ext_sc[HALO:HALO + TM, :] = cur_ref[...]
    ext_sc[HALO + TM:, :] = jnp.where(last, jnp.zeros_like(nxt), nxt)


def _moe_prologue(x_new, mod_ref, g2_ref, wr_ref, h2_ref, lg_ref):
    h2 = _rms_mod(x_new, g2_ref[...], mod_ref[0, 4:5, :], mod_ref[0, 3:4, :]).astype(BF16)
    h2_ref[...] = h2
    lg_ref[...] = jnp.dot(h2, wr_ref[...], preferred_element_type=F32)


def _post_even_kernel(x_ref, u_ref, up_ref, un_ref, o_ref, mod_ref, cw_ref, cb_ref, lng_ref, lnb_ref,
                      wo_ref, g2_ref, wr_ref, x1_ref, h2_ref, lg_ref, ext_sc, *, npt, tps):
    i = pl.program_id(0)
    first, last = _seq_edges(i, npt, tps)
    _fill_ext(ext_sc, u_ref, up_ref, un_ref, first, last)
    cw = u_ref.shape[1]
    rows = 32
    chunks = []
    for rc in range(TM // rows):
        acc = None
        for k in range(CONV_K):
            start = rc * rows + HALO - CONV_K // 2 + k
            term = ext_sc[start:start + rows, :] * cw_ref[k:k + 1, :]
            acc = term if acc is None else acc + term
        chunks.append(acc)
    conv = jnp.concatenate(chunks, axis=0) + cb_ref[...]
    mu = jnp.mean(conv, axis=-1, keepdims=True)
    cen = conv - mu
    var = jnp.mean(cen * cen, axis=-1, keepdims=True)
    uu = _silu(cen * lax.rsqrt(var + EPS) * lng_ref[...] + lnb_ref[...])
    m = (jnp.dot(uu.astype(BF16), wo_ref[0:cw, :], preferred_element_type=F32)
         + jnp.dot(o_ref[...], wo_ref[cw:, :], preferred_element_type=F32))
    x1 = x_ref[...] + mod_ref[0, 2:3, :] * m
    x1_ref[...] = x1
    _moe_prologue(x1, mod_ref, g2_ref, wr_ref, h2_ref, lg_ref)


def _pre_odd_kernel(x_ref, y_ref, modp_ref, mod_ref, g_ref, w_ref, x2_ref, xc_ref, p_ref, bg_ref, *, cw):
    x2 = x_ref[...] + modp_ref[0, 5:6, :] * y_ref[...]
    x2_ref[...] = x2
    h = _rms_mod(x2, g_ref[...], mod_ref[0, 1:2, :], mod_ref[0, 0:1, :])
    proj = jnp.dot(h.astype(BF16), w_ref[...], preferred_element_type=F32)
    xc_ref[...] = proj[:, :cw]
    hd = proj[:, cw:2 * cw]
    bg_ref[...] = proj[:, 2 * cw:3 * cw]
    p_ref[...] = proj[:, 3 * cw:] * hd


def _post_odd_kernel(x_ref, xc_ref, xcp_ref, xcn_ref, p_ref, pp_ref, pn_ref, bg_ref, mod_ref,
                     pw_ref, ps_ref, dw_ref, wo_ref, g2_ref, wr_ref,
                     x3_ref, h2_ref, lg_ref, extc_sc, extp_sc, *, npt, tps):
    i = pl.program_id(0)
    first, last = _seq_edges(i, npt, tps)
    _fill_ext(extc_sc, xc_ref, xcp_ref, xcn_ref, first, last)
    _fill_ext(extp_sc, p_ref, pp_ref, pn_ref, first, last)
    cw = xc_ref.shape[1]
    gc = cw // len(POOL_WINDOWS)
    is_prompt = i < npt
    t_len = jnp.where(is_prompt, TM, TM * tps)
    pos = lax.broadcasted_iota(jnp.int32, (TM, 1), 0) + jnp.where(is_prompt, 0, ((i - npt) % tps) * TM)
    ds = []
    for g, w in enumerate(POOL_WINDOWS):
        sl = slice(g * gc, (g + 1) * gc)
        acc = None
        for o in range(-(w // 2), w // 2):
            term = extc_sc[HALO + o:HALO + o + TM, sl]
            acc = term if acc is None else acc + term
        lo = jnp.maximum(pos - w // 2, 0)
        hi = jnp.minimum(pos + w // 2 - 1, t_len - 1)
        cnt = (hi - lo + 1).astype(F32)
        ds.append(acc / cnt - extc_sc[HALO:HALO + TM, sl])
    d = jnp.concatenate(ds, axis=1).astype(BF16)
    yc = jnp.dot(d, pw_ref[...], preferred_element_type=F32) * ps_ref[...]
    conv = None
    for k in range(SCONV_K):
        start = HALO - SCONV_K // 2 + k
        term = extp_sc[start:start + TM, :] * dw_ref[k:k + 1, :]
        conv = term if conv is None else conv + term
    yd = bg_ref[...] * conv
    m = (jnp.dot(yc.astype(BF16), wo_ref[0:cw, :], preferred_element_type=F32)
         + jnp.dot(yd.astype(BF16), wo_ref[cw:, :], preferred_element_type=F32))
    x3 = x_ref[...] + mod_ref[0, 2:3, :] * m
    x3_ref[...] = x3
    _moe_prologue(x3, mod_ref, g2_ref, wr_ref, h2_ref, lg_ref)


def _final_kernel(x_ref, y_ref, modp_ref, g_ref, op_ref, os_ref, *, npt):
    i = pl.program_id(0)
    x = x_ref[...] + modp_ref[0, 5:6, :] * y_ref[...]
    ms = jnp.mean(x * x, axis=-1, keepdims=True)
    out = x * lax.rsqrt(ms + EPS) * g_ref[...]

    @pl.when(i < npt)
    def _():
        op_ref[...] = out

    @pl.when(i >= npt)
    def _():
        os_ref[...] = out


def _expert_kernel(be_ref, nv_ref, x_ref, w_ref, wg_ref, wu_ref, wd_ref, y_ref, wg_sc, wu_sc, wd_sc):
    i = pl.program_id(0)

    @pl.when(jnp.logical_or(i == 0, be_ref[i] != be_ref[jnp.maximum(i - 1, 0)]))
    def _():
        wg_sc[...] = wg_ref[0, 0].astype(BF16)
        wu_sc[...] = wu_ref[0, 0].astype(BF16)
        wd_sc[...] = wd_ref[0, 0].astype(BF16)

    @pl.when(i < nv_ref[0])
    def _():
        x = x_ref[...]
        hg = jnp.dot(x, wg_sc[...], preferred_element_type=F32)
        hu = jnp.dot(x, wu_sc[...], preferred_element_type=F32)
        hid = (_silu(hg) * hu).astype(BF16)
        y_ref[...] = jnp.dot(hid, wd_sc[...], preferred_element_type=F32) * w_ref[...]

    @pl.when(i >= nv_ref[0])
    def _():
        y_ref[...] = jnp.zeros(y_ref.shape, F32)


def _experts(blk_e, nvalid, xb, row_w, layer, wg, wu, wd):
    p_len, d = xb.shape
    hid = wg.shape[3]
    nblk = p_len // MOE_BLK
    grid_spec = pltpu.PrefetchScalarGridSpec(
        num_scalar_prefetch=2,
        grid=(nblk,),
        in_specs=[pl.BlockSpec((MOE_BLK, d), lambda i, be, nv: (i, 0)),
                  pl.BlockSpec((MOE_BLK, 1), lambda i, be, nv: (i, 0)),
                  pl.BlockSpec((1, 1, d, hid), lambda i, be, nv: (layer, be[i], 0, 0)),
                  pl.BlockSpec((1, 1, d, hid), lambda i, be, nv: (layer, be[i], 0, 0)),
                  pl.BlockSpec((1, 1, hid, d), lambda i, be, nv: (layer, be[i], 0, 0))],
        out_specs=pl.BlockSpec((MOE_BLK, d), lambda i, be, nv: (i, 0)),
        scratch_shapes=[pltpu.VMEM((d, hid), BF16), pltpu.VMEM((d, hid), BF16), pltpu.VMEM((hid, d), BF16)],
    )
    return pl.pallas_call(
        _expert_kernel,
        grid_spec=grid_spec,
        out_shape=jax.ShapeDtypeStruct((p_len, d), F32),
        compiler_params=_cparams(("arbitrary",)),
        name="moe_experts",
    )(blk_e, nvalid, xb, row_w, wg, wu, wd)


def _moe(h2, logits, b_rg, b_re, layer, wg, wu, wd):
    n, d = h2.shape
    gp = jax.nn.softmax(logits[:, :N_GROUPS] + b_rg, axis=-1)
    g_idx = jnp.argmax(gp, axis=-1).astype(jnp.int32)
    g_p = jnp.take_along_axis(gp, g_idx[:, None], axis=-1)[:, 0]
    el_all = logits[:, N_GROUPS:N_GROUPS + N_EXPERTS].reshape(n, N_GROUPS, EXPERTS_PER_GROUP) + b_re
    el = jnp.take_along_axis(el_all, g_idx[:, None, None], axis=1)[:, 0]
    top_v, top_i = lax.top_k(el, TOP_K)
    wts = jax.nn.softmax(top_v, axis=-1) * g_p[:, None]
    eid = g_idx[:, None] * EXPERTS_PER_GROUP + top_i.astype(jnp.int32)

    a_len = n * TOP_K
    flat_e = eid.reshape(-1)
    flat_w = wts.reshape(-1)
    order = jnp.argsort(flat_e)
    counts = jnp.zeros((N_EXPERTS,), jnp.int32).at[flat_e].add(1)
    pc = ((counts + MOE_BLK - 1) // MOE_BLK) * MOE_BLK
    start = jnp.cumsum(counts) - counts
    pend = jnp.cumsum(pc)
    pstart = pend - pc
    se = flat_e[order]
    dest = pstart[se] + (jnp.arange(a_len, dtype=jnp.int32) - start[se])
    p_len = a_len + N_EXPERTS * MOE_BLK
    nblk = p_len // MOE_BLK
    row_tok = jnp.zeros((p_len,), jnp.int32).at[dest].set((order // TOP_K).astype(jnp.int32))
    row_w = jnp.zeros((p_len,), F32).at[dest].set(flat_w[order])
    pos = jnp.zeros((a_len,), jnp.int32).at[order].set(dest)
    blk_e = jnp.minimum(jnp.searchsorted(pend, jnp.arange(nblk, dtype=jnp.int32) * MOE_BLK, side='right'),
                        N_EXPERTS - 1).astype(jnp.int32)
    nvalid = (pend[-1:] // MOE_BLK).astype(jnp.int32)

    xb = h2[row_tok]
    yb = _experts(blk_e, nvalid, xb, row_w[:, None], layer, wg, wu, wd)
    pos2 = pos.reshape(n, TOP_K)
    return yb[pos2[:, 0]] + yb[pos2[:, 1]]


def _rope_tables(t_len, qw):
    rows = t_len // GRID_W
    ax = qw // (DIFF_HEADS * 2) // 2
    row = jnp.repeat(jnp.arange(rows, dtype=F32), GRID_W)
    col = jnp.tile(jnp.arange(GRID_W, dtype=F32), rows)
    inv = ROPE_BASE ** (-jnp.arange(0, ax, 2, dtype=F32) / ax)
    ar = row[:, None] * inv[None, :]
    ac = col[:, None] * inv[None, :]
    cr, sr, cc, sc = jnp.cos(ar), jnp.sin(ar), jnp.cos(ac), jnp.sin(ac)
    cos = jnp.concatenate([cr, cr, cc, cc], axis=1)
    sins = jnp.concatenate([-sr, sr, -sc, sc], axis=1)
    reps = qw // cos.shape[1]
    cos = jnp.concatenate([jnp.ones((TM, cos.shape[1]), F32), cos], axis=0)
    sins = jnp.concatenate([jnp.zeros((TM, sins.shape[1]), F32), sins], axis=0)
    return jnp.tile(cos, (1, reps)), jnp.tile(sins, (1, reps))


def kernel(x_prompt, x_sample, cache_k, cache_v, c, c_ctx, w_ada, b_ada, g_norm1, g_norm2, g_final, w_in_e, conv_a_w, conv_a_b, ln_a_g, ln_a_b, lam_q1, lam_k1, lam_q2, lam_k2, subln_g, w_out_e, w_in_o, pool_w, pool_scale, conv_d_w, w_out_o, w_router_g, b_router_g, w_router_e, b_router_e, w_gate, w_up, w_down):
    bp, tp, d = x_prompt.shape
    bs, ts, _ = x_sample.shape
    past = cache_k.shape[2]
    assert tp == TM and ts % TM == 0 and d % LANES == 0
    npt = bp
    tps = ts // TM
    n_p, n_s = bp * tp, bs * ts
    n = n_p + n_s
    nt = n // TM
    cw = conv_a_w.shape[2]
    qw = (w_in_e.shape[2] - 2 * cw) // 3
    dh = qw // (DIFF_HEADS * 2)
    hpt = TM // HALO

    def cond_of(i):
        return jnp.where(i < npt, 0, 1 + (i - npt) // tps)

    def tile(w):
        return pl.BlockSpec((TM, w), lambda i: (i, 0))

    def halo_prev(w):
        return pl.BlockSpec((HALO, w), lambda i: (jnp.maximum(i * hpt - 1, 0), 0))

    def halo_next(w):
        return pl.BlockSpec((HALO, w), lambda i: (jnp.minimum((i + 1) * hpt, nt * hpt - 1), 0))

    def whole(shape):
        return pl.BlockSpec(shape, lambda i: (0,) * len(shape))

    mod_spec = pl.BlockSpec((1, 6, d), lambda i: (cond_of(i), 0, 0))

    rows = 8 * ((1 + bs + 7) // 8)
    cond = jnp.concatenate([c_ctx[None, :], c, jnp.zeros((rows - 1 - bs, d), F32)], axis=0)
    mod = _ada_table(cond, w_ada, b_ada)

    x = jnp.concatenate([x_prompt.reshape(n_p, d), x_sample.reshape(n_s, d)], axis=0)

    def router_w(i):
        wre = jnp.transpose(w_router_e[i], (1, 0, 2)).reshape(d, N_EXPERTS)
        wr = jnp.concatenate([w_router_g[i], wre], axis=1)
        return jnp.pad(wr, ((0, 0), (0, ROUTER_PAD - wr.shape[1]))).astype(BF16)

    cos, sins = _rope_tables(ts, qw)
    tab_spec = pl.BlockSpec((TM, qw), lambda i: (jnp.where(i < npt, 0, 1 + (i - npt) % tps), 0))
    cache_spec = pl.BlockSpec((TM, qw), lambda i: (jnp.minimum(i, npt - 1), 0))
    u, q, k, vt, kc, vc = pl.pallas_call(
        functools.partial(_pre_even_kernel, npt=npt, cw=cw, qw=qw, qscale=dh ** -0.5 * math.log2(math.e)),
        grid=(nt,),
        in_specs=[tile(d), mod_spec, whole((1, d)), whole(w_in_e.shape[1:]), tab_spec, tab_spec],
        out_specs=[tile(cw), tile(qw), tile(qw), pl.BlockSpec((qw, TM), lambda i: (0, i)),
                   cache_spec, cache_spec],
        out_shape=[jax.ShapeDtypeStruct((n, cw), F32), jax.ShapeDtypeStruct((n, qw), BF16),
                   jax.ShapeDtypeStruct((n, qw), BF16), jax.ShapeDtypeStruct((qw, n), BF16),
                   jax.ShapeDtypeStruct((n_p, qw), F32), jax.ShapeDtypeStruct((n_p, qw), F32)],
        compiler_params=_cparams(("arbitrary",)),
        name="pre_even",
    )(x, mod[0], g_norm1[0][None, :], w_in_e[0].astype(BF16), cos, sins)

    lam_init = 0.8 - 0.6 * math.exp(-0.3 * 0)
    lam = (jnp.exp(jnp.sum(lam_q1[0] * lam_k1[0])) - jnp.exp(jnp.sum(lam_q2[0] * lam_k2[0])) + lam_init)
    lam = lam.reshape(1).astype(F32)
    sub_g = subln_g[0][:, None]
    o_p = _attention(lam, q, k, vt, sub_g, None, batch=bp, t_len=tp, row0=0, tq=TM, tk=TM,
                     post_scale=1.0 - lam_init)
    k_ctx = cache_k[:, 0].reshape(bs * past, qw).astype(BF16)
    vt_ctx = jnp.transpose(cache_v[:, 0].reshape(bs, past, qw), (0, 2, 1)).astype(BF16)
    o_s = _attention(lam, q, k, vt, sub_g, (k_ctx, vt_ctx), batch=bs, t_len=ts, row0=n_p, tq=512, tk=512,
                     post_scale=1.0 - lam_init)
    o = jnp.concatenate([o_p, o_s], axis=0)

    moe_outs = [jax.ShapeDtypeStruct((n, d), F32), jax.ShapeDtypeStruct((n, d), BF16),
                jax.ShapeDtypeStruct((n, ROUTER_PAD), F32)]
    moe_out_specs = [tile(d), tile(d), tile(ROUTER_PAD)]
    x1, h2, logits = pl.pallas_call(
        functools.partial(_post_even_kernel, npt=npt, tps=tps),
        grid=(nt,),
        in_specs=[tile(d), tile(cw), halo_prev(cw), halo_next(cw), tile(qw), mod_spec,
                  whole((CONV_K, cw)), whole((1, cw)), whole((1, cw)), whole((1, cw)),
                  whole(w_out_e.shape[1:]), whole((1, d)), whole((d, ROUTER_PAD))],
        out_specs=moe_out_specs,
        out_shape=moe_outs,
        scratch_shapes=[pltpu.VMEM((TM + 2 * HALO, cw), F32)],
        compiler_params=_cparams(("parallel",)),
        name="post_even",
    )(x, u, u, u, o, mod[0], conv_a_w[0], conv_a_b[0][None, :], ln_a_g[0][None, :], ln_a_b[0][None, :],
      w_out_e[0].astype(BF16), g_norm2[0][None, :], router_w(0))
    y = _moe(h2, logits, b_router_g[0], b_router_e[0],
             0, w_gate, w_up, w_down)

    pw = pool_w.shape[2]
    x2, xc, p, bg = pl.pallas_call(
        functools.partial(_pre_odd_kernel, cw=cw),
        grid=(nt,),
        in_specs=[tile(d), tile(d), mod_spec, mod_spec, whole((1, d)), whole(w_in_o.shape[1:])],
        out_specs=[tile(d), tile(cw), tile(cw), tile(cw)],
        out_shape=[jax.ShapeDtypeStruct((n, d), F32)] + [jax.ShapeDtypeStruct((n, cw), F32)] * 3,
        compiler_params=_cparams(("parallel",)),
        name="pre_odd",
    )(x1, y, mod[0], mod[1], g_norm1[1][None, :], w_in_o[0].astype(BF16))
    pool_bd = jnp.zeros((cw, cw), F32)
    for g in range(len(POOL_WINDOWS)):
        pool_bd = pool_bd.at[g * pw:(g + 1) * pw, g * pw:(g + 1) * pw].set(pool_w[0, g])
    x3, h2, logits = pl.pallas_call(
        functools.partial(_post_odd_kernel, npt=npt, tps=tps),
        grid=(nt,),
        in_specs=[tile(d), tile(cw), halo_prev(cw), halo_next(cw), tile(cw), halo_prev(cw), halo_next(cw),
                  tile(cw), mod_spec, whole((cw, cw)), whole((1, cw)), whole((SCONV_K, cw)),
                  whole(w_out_o.shape[1:]), whole((1, d)), whole((d, ROUTER_PAD))],
        out_specs=moe_out_specs,
        out_shape=moe_outs,
        scratch_shapes=[pltpu.VMEM((TM + 2 * HALO, cw), F32), pltpu.VMEM((TM + 2 * HALO, cw), F32)],
        compiler_params=_cparams(("parallel",)),
        name="post_odd",
    )(x2, xc, xc, xc, p, p, p, bg, mod[1], pool_bd.astype(BF16), pool_scale[0][None, :], conv_d_w[0],
      w_out_o[0].astype(BF16), g_norm2[1][None, :], router_w(1))
    y = _moe(h2, logits, b_router_g[1], b_router_e[1],
             1, w_gate, w_up, w_down)

    out_p, out_s = pl.pallas_call(
        functools.partial(_final_kernel, npt=npt),
        grid=(nt,),
        in_specs=[tile(d), tile(d), mod_spec, whole((1, d))],
        out_specs=[pl.BlockSpec((TM, d), lambda i: (jnp.minimum(i, npt - 1), 0)),
                   pl.BlockSpec((TM, d), lambda i: (jnp.maximum(i - npt, 0), 0))],
        out_shape=[jax.ShapeDtypeStruct((n_p, d), F32), jax.ShapeDtypeStruct((n_s, d), F32)],
        compiler_params=_cparams(("arbitrary",)),
        name="final_norm",
    )(x3, y, mod[1], g_final[None, :])

    new_k = kc.reshape(bp, 1, tp, DIFF_HEADS, 2, dh)
    new_v = vc.reshape(bp, 1, tp, DIFF_HEADS, 2 * dh)
    return (out_p.reshape(bp, tp, d), out_s.reshape(bs, ts, d), new_k, new_v)
```

```python
import functools
import math

import jax
import jax.numpy as jnp
from jax import lax
from jax.experimental import pallas as pl
from jax.experimental.pallas import tpu as pltpu

F32 = jnp.float32
BF16 = jnp.bfloat16

EPS = 1e-6
GRID_W = 64
ROPE_BASE = 10000.0
CONV_K = 31
SCONV_K = 3
POOL_WINDOWS = (2, 4, 8, 16)
DIFF_HEADS = 4
N_GROUPS = 4
EXPERTS_PER_GROUP = 8
N_EXPERTS = N_GROUPS * EXPERTS_PER_GROUP
TOP_K = 2

LANES = 128
TM = 256
HALO = 16
MOE_BLK = 256
ROUTER_PAD = 128
ONES_ROWS = 16
VMEM_LIMIT = 56 * 1024 * 1024


def _cparams(sem):
    return pltpu.CompilerParams(dimension_semantics=sem, vmem_limit_bytes=VMEM_LIMIT)


def _rms_mod(x, g, sc, sh):
    ms = jnp.mean(x * x, axis=-1, keepdims=True)
    return (x * lax.rsqrt(ms + EPS)) * g * (1.0 + sc) + sh


def _silu(x):
    return x * jax.nn.sigmoid(x)


def _ada_kernel(c_ref, w_ref, b_ref, o_ref):
    cs = _silu(c_ref[...])
    o_ref[0] = jnp.dot(cs, w_ref[0], precision=lax.Precision.HIGHEST,
                       preferred_element_type=F32) + b_ref[0]


def _ada_table(cond, w_ada, b_ada):
    depth, d, six_d = w_ada.shape
    rows = cond.shape[0]
    nj = six_d // d
    out = pl.pallas_call(
        _ada_kernel,
        grid=(depth, nj),
        in_specs=[pl.BlockSpec((rows, d), lambda l, j: (0, 0)),
                  pl.BlockSpec((1, d, d), lambda l, j: (l, 0, j)),
                  pl.BlockSpec((1, 1, d), lambda l, j: (l, 0, j))],
        out_specs=pl.BlockSpec((1, rows, d), lambda l, j: (l, 0, j)),
        out_shape=jax.ShapeDtypeStruct((depth, rows, six_d), F32),
        compiler_params=_cparams(("parallel", "parallel")),
        name="ada_table",
    )(cond, w_ada, b_ada.reshape(depth, 1, six_d))
    return out.reshape(depth, rows, nj, d)


def _rope(x, cos, sins, first_half):
    outs = []
    for j in range(x.shape[1] // LANES):
        sl = slice(j * LANES, (j + 1) * LANES)
        xs = x[:, sl]
        nxt = pltpu.roll(xs, LANES - 16, 1)
        prv = pltpu.roll(xs, 16, 1)
        rot = jnp.where(first_half, nxt, prv)
        outs.append(xs * cos[:, sl] + rot * sins[:, sl])
    return jnp.concatenate(outs, axis=1)


def _pre_even_kernel(x_ref, mod_ref, g_ref, w_ref, cos_ref, sin_ref,
                     u_ref, q_ref, k_ref, vt_ref, kc_ref, vc_ref, *, npt, cw, qw, qscale):
    i = pl.program_id(0)
    h = _rms_mod(x_ref[...], g_ref[...], mod_ref[0, 1:2, :], mod_ref[0, 0:1, :])
    proj = jnp.dot(h.astype(BF16), w_ref[...], preferred_element_type=F32)
    ga = proj[:, :cw]
    gb = proj[:, cw:2 * cw]
    u_ref[...] = ga * jax.nn.sigmoid(gb)
    q = proj[:, 2 * cw:2 * cw + qw]
    k = proj[:, 2 * cw + qw:2 * cw + 2 * qw]
    v = proj[:, 2 * cw + 2 * qw:]

    @pl.when(i < npt)
    def _():
        kc_ref[...] = k
        vc_ref[...] = v

    lane = lax.broadcasted_iota(jnp.int32, (1, LANES), 1)
    first_half = (lane % 32) < 16
    cos = cos_ref[...]
    sins = sin_ref[...]
    q_ref[...] = (_rope(q, cos, sins, first_half) * qscale).astype(BF16)
    k_ref[...] = _rope(k, cos, sins, first_half).astype(BF16)
    vt_ref[...] = v.T.astype(BF16)


def _attn_kernel(lam_ref, q_ref, k_ref, vt_ref, *rest, nk, tk, nk_ctx, post_scale):
    if nk_ctx:
        kc_ref, vtc_ref, g_ref, o_ref, s_sc, mc_sc, m_sc, acc_sc = rest
    else:
        g_ref, o_ref, s_sc, mc_sc, m_sc, acc_sc = rest
    q = q_ref[...]
    hw = q.shape[1]
    dh = hw // 2
    lane = lax.broadcasted_iota(jnp.int32, (1, hw), 1)
    zero = jnp.zeros_like(q)
    qs = (jnp.where(lane < dh, q, zero), jnp.where(lane >= dh, q, zero))
    m_sc[...] = jnp.full(m_sc.shape, -jnp.inf, F32)
    acc_sc[...] = jnp.zeros(acc_sc.shape, F32)
    ones_rows = (lax.broadcasted_iota(jnp.int32, (ONES_ROWS, tk), 0) == 0).astype(BF16)

    def scores(kk, slot):
        for c in range(2):
            s = lax.dot_general(kk, qs[c], (((1,), (1,)), ((), ())), preferred_element_type=F32)
            s_sc[slot, c] = s
            mc_sc[slot, c] = jnp.max(s, axis=0, keepdims=True)

    def consume(vt, slot):
        vta = jnp.concatenate([vt, ones_rows], axis=0)
        for c in range(2):
            m_prev = m_sc[c]
            m_new = jnp.maximum(m_prev, mc_sc[slot, c])
            alpha = jnp.exp2(m_prev - m_new)
            p = jnp.exp2(s_sc[slot, c] - m_new).astype(BF16)
            acc_sc[c] = alpha * acc_sc[c] + jnp.dot(vta, p, preferred_element_type=F32)
            m_sc[c] = m_new

    def k_at(i):
        if i < nk_ctx:
            return kc_ref[i * tk:(i + 1) * tk, :]
        return k_ref[(i - nk_ctx) * tk:(i - nk_ctx + 1) * tk, :]

    def vt_at(i):
        if i < nk_ctx:
            return vtc_ref[0, :, i * tk:(i + 1) * tk]
        return vt_ref[:, (i - nk_ctx) * tk:(i - nk_ctx + 1) * tk]

    n_total = nk_ctx + nk
    peel = nk_ctx + (n_total - 1 - nk_ctx) % 2
    scores(k_at(0), 0)
    for i in range(peel):
        scores(k_at(i + 1), (i + 1) % 2)
        consume(vt_at(i), i % 2)

    def body(t, carry):
        for r in range(2):
            j = 2 * t + (peel + r - nk_ctx)
            nxt = pl.multiple_of((j + 1) * tk, tk)
            cur = pl.multiple_of(j * tk, tk)
            scores(k_ref[pl.ds(nxt, tk), :], (peel + r + 1) % 2)
            consume(vt_ref[:, pl.ds(cur, tk)], (peel + r) % 2)
        return carry

    lax.fori_loop(0, (n_total - 1 - peel) // 2, body, 0)
    consume(vt_at(n_total - 1), (n_total - 1) % 2)
    o = (acc_sc[0, :hw, :] / acc_sc[0, hw:hw + 1, :]
         - lam_ref[0] * (acc_sc[1, :hw, :] / acc_sc[1, hw:hw + 1, :]))
    ms = jnp.mean(o * o, axis=0, keepdims=True)
    o = o * lax.rsqrt(ms + EPS) * g_ref[...] * post_scale
    o_ref[...] = o.T.astype(BF16)


def _attention(lam, q, k, vt, g_col, ctx, *, batch, t_len, row0, tq, tk, post_scale):
    hw = q.shape[1] // DIFF_HEADS
    nq = t_len // tq
    assert row0 % t_len == 0 and t_len % tk == 0
    qb0 = row0 // tq
    sb0 = row0 // t_len
    in_specs = [pl.BlockSpec(memory_space=pltpu.SMEM),
                pl.BlockSpec((tq, hw), lambda b, h, i: (qb0 + b * nq + i, h)),
                pl.BlockSpec((t_len, hw), lambda b, h, i: (sb0 + b, h)),
                pl.BlockSpec((hw, t_len), lambda b, h, i: (h, sb0 + b))]
    args = [lam, q, k, vt]
    nk_ctx = 0
    if ctx is not None:
        k_ctx, vt_ctx = ctx
        past = vt_ctx.shape[2]
        assert past % tk == 0
        nk_ctx = past // tk
        in_specs += [pl.BlockSpec((past, hw), lambda b, h, i: (b, h)),
                     pl.BlockSpec((1, hw, past), lambda b, h, i: (b, h, 0))]
        args += [k_ctx, vt_ctx]
    in_specs.append(pl.BlockSpec((hw, 1), lambda b, h, i: (0, 0)))
    args.append(g_col)
    return pl.pallas_call(
        functools.partial(_attn_kernel, nk=t_len // tk, tk=tk, nk_ctx=nk_ctx, post_scale=post_scale),
        grid=(batch, DIFF_HEADS, nq),
        in_specs=in_specs,
        out_specs=pl.BlockSpec((tq, hw), lambda b, h, i: (b * nq + i, h)),
        out_shape=jax.ShapeDtypeStruct((batch * t_len, q.shape[1]), BF16),
        scratch_shapes=[pltpu.VMEM((2, 2, tk, tq), F32), pltpu.VMEM((2, 2, 1, tq), F32),
                        pltpu.VMEM((2, 1, tq), F32), pltpu.VMEM((2, hw + ONES_ROWS, tq), F32)],
        compiler_params=_cparams(("parallel", "parallel", "parallel")),
        name="diff_attention",
    )(*args)


def _seq_edges(i, npt, tps):
    r = (i - npt) % tps
    is_prompt = i < npt
    first = jnp.logical_or(is_prompt, r == 0)
    last = jnp.logical_or(is_prompt, r == tps - 1)
    return first, last


def _fill_ext(ext_sc, cur_ref, prev_ref, next_ref, first, last):
    prev = prev_ref[...]
    nxt = next_ref[...]
    ext_sc[0:HALO, :] = jnp.where(first, jnp.zeros_like(prev), prev)
    ext_sc[HALO:HALO + TM, :] = cur_ref[...]
    ext_sc[HALO + TM:, :] = jnp.where(last, jnp.zeros_like(nxt), nxt)


def _moe_prologue(x_new, mod_ref, g2_ref, wr_ref, h2_ref, lg_ref):
    h2 = _rms_mod(x_new, g2_ref[...], mod_ref[0, 4:5, :], mod_ref[0, 3:4, :]).astype(BF16)
    h2_ref[...] = h2
    lg_ref[...] = jnp.dot(h2, wr_ref[...], preferred_element_type=F32)


def _post_even_kernel(x_ref, u_ref, up_ref, un_ref, o_ref, mod_ref, cw_ref, cb_ref, lng_ref, lnb_ref,
                      wo_ref, g2_ref, wr_ref, x1_ref, h2_ref, lg_ref, ext_sc, *, npt, tps):
    i = pl.program_id(0)
    first, last = _seq_edges(i, npt, tps)
    _fill_ext(ext_sc, u_ref, up_ref, un_ref, first, last)
    cw = u_ref.shape[1]
    rows = 32
    chunks = []
    for rc in range(TM // rows):
        acc = None
        for k in range(CONV_K):
            start = rc * rows + HALO - CONV_K // 2 + k
            term = ext_sc[start:start + rows, :] * cw_ref[k:k + 1, :]
            acc = term if acc is None else acc + term
        chunks.append(acc)
    conv = jnp.concatenate(chunks, axis=0) + cb_ref[...]
    mu = jnp.mean(conv, axis=-1, keepdims=True)
    cen = conv - mu
    var = jnp.mean(cen * cen, axis=-1, keepdims=True)
    uu = _silu(cen * lax.rsqrt(var + EPS) * lng_ref[...] + lnb_ref[...])
    m = (jnp.dot(uu.astype(BF16), wo_ref[0:cw, :], preferred_element_type=F32)
         + jnp.dot(o_ref[...], wo_ref[cw:, :], preferred_element_type=F32))
    x1 = x_ref[...] + mod_ref[0, 2:3, :] * m
    x1_ref[...] = x1
    _moe_prologue(x1, mod_ref, g2_ref, wr_ref, h2_ref, lg_ref)


def _pre_odd_kernel(x_ref, y_ref, modp_ref, mod_ref, g_ref, w_ref, x2_ref, xc_ref, p_ref, bg_ref, *, cw):
    x2 = x_ref[...] + modp_ref[0, 5:6, :] * y_ref[...]
    x2_ref[...] = x2
    h = _rms_mod(x2, g_ref[...], mod_ref[0, 1:2, :], mod_ref[0, 0:1, :])
    proj = jnp.dot(h.astype(BF16), w_ref[...], preferred_element_type=F32)
    xc_ref[...] = proj[:, :cw]
    hd = proj[:, cw:2 * cw]
    bg_ref[...] = proj[:, 2 * cw:3 * cw]
    p_ref[...] = proj[:, 3 * cw:] * hd


def _post_odd_kernel(x_ref, xc_ref, xcp_ref, xcn_ref, p_ref, pp_ref, pn_ref, bg_ref, mod_ref,
                     pw_ref, ps_ref, dw_ref, wo_ref, g2_ref, wr_ref,
                     x3_ref, h2_ref, lg_ref, extc_sc, extp_sc, *, npt, tps):
    i = pl.program_id(0)
    first, last = _seq_edges(i, npt, tps)
    _fill_ext(extc_sc, xc_ref, xcp_ref, xcn_ref, first, last)
    _fill_ext(extp_sc, p_ref, pp_ref, pn_ref, first, last)
    cw = xc_ref.shape[1]
    gc = cw // len(POOL_WINDOWS)
    is_prompt = i < npt
    t_len = jnp.where(is_prompt, TM, TM * tps)
    pos = lax.broadcasted_iota(jnp.int32, (TM, 1), 0) + jnp.where(is_prompt, 0, ((i - npt) % tps) * TM)
    ds = []
    for g, w in enumerate(POOL_WINDOWS):
        sl = slice(g * gc, (g + 1) * gc)
        acc = None
        for o in range(-(w // 2), w // 2):
            term = extc_sc[HALO + o:HALO + o + TM, sl]
            acc = term if acc is None else acc + term
        lo = jnp.maximum(pos - w // 2, 0)
        hi = jnp.minimum(pos + w // 2 - 1, t_len - 1)
        cnt = (hi - lo + 1).astype(F32)
        ds.append(acc / cnt - extc_sc[HALO:HALO + TM, sl])
    d = jnp.concatenate(ds, axis=1).astype(BF16)
    yc = jnp.dot(d, pw_ref[...], preferred_element_type=F32) * ps_ref[...]
    conv = None
    for k in range(SCONV_K):
        start = HALO - SCONV_K // 2 + k
        term = extp_sc[start:start + TM, :] * dw_ref[k:k + 1, :]
        conv = term if conv is None else conv + term
    yd = bg_ref[...] * conv
    m = (jnp.dot(yc.astype(BF16), wo_ref[0:cw, :], preferred_element_type=F32)
         + jnp.dot(yd.astype(BF16), wo_ref[cw:, :], preferred_element_type=F32))
    x3 = x_ref[...] + mod_ref[0, 2:3, :] * m
    x3_ref[...] = x3
    _moe_prologue(x3, mod_ref, g2_ref, wr_ref, h2_ref, lg_ref)


def _final_kernel(x_ref, y_ref, modp_ref, g_ref, op_ref, os_ref, *, npt):
    i = pl.program_id(0)
    x = x_ref[...] + modp_ref[0, 5:6, :] * y_ref[...]
    ms = jnp.mean(x * x, axis=-1, keepdims=True)
    out = x * lax.rsqrt(ms + EPS) * g_ref[...]

    @pl.when(i < npt)
    def _():
        op_ref[...] = out

    @pl.when(i >= npt)
    def _():
        os_ref[...] = out


def _expert_kernel(be_ref, nv_ref, x_ref, w_ref, wg_ref, wu_ref, wd_ref, y_ref, wg_sc, wu_sc, wd_sc):
    i = pl.program_id(0)

    @pl.when(jnp.logical_or(i == 0, be_ref[i] != be_ref[jnp.maximum(i - 1, 0)]))
    def _():
        wg_sc[...] = wg_ref[0, 0].astype(BF16)
        wu_sc[...] = wu_ref[0, 0].astype(BF16)
        wd_sc[...] = wd_ref[0, 0].astype(BF16)

    @pl.when(i < nv_ref[0])
    def _():
        x = x_ref[...]
        hg = jnp.dot(x, wg_sc[...], preferred_element_type=F32)
        hu = jnp.dot(x, wu_sc[...], preferred_element_type=F32)
        hid = (_silu(hg) * hu).astype(BF16)
        y_ref[...] = jnp.dot(hid, wd_sc[...], preferred_element_type=F32) * w_ref[...]

    @pl.when(i >= nv_ref[0])
    def _():
        y_ref[...] = jnp.zeros(y_ref.shape, F32)


def _experts(blk_e, nvalid, xb, row_w, layer, wg, wu, wd):
    p_len, d = xb.shape
    hid = wg.shape[3]
    nblk = p_len // MOE_BLK
    grid_spec = pltpu.PrefetchScalarGridSpec(
        num_scalar_prefetch=2,
        grid=(nblk,),
        in_specs=[pl.BlockSpec((MOE_BLK, d), lambda i, be, nv: (i, 0)),
                  pl.BlockSpec((MOE_BLK, 1), lambda i, be, nv: (i, 0)),
                  pl.BlockSpec((1, 1, d, hid), lambda i, be, nv: (layer, be[i], 0, 0)),
                  pl.BlockSpec((1, 1, d, hid), lambda i, be, nv: (layer, be[i], 0, 0)),
                  pl.BlockSpec((1, 1, hid, d), lambda i, be, nv: (layer, be[i], 0, 0))],
        out_specs=pl.BlockSpec((MOE_BLK, d), lambda i, be, nv: (i, 0)),
        scratch_shapes=[pltpu.VMEM((d, hid), BF16), pltpu.VMEM((d, hid), BF16), pltpu.VMEM((hid, d), BF16)],
    )
    return pl.pallas_call(
        _expert_kernel,
        grid_spec=grid_spec,
        out_shape=jax.ShapeDtypeStruct((p_len, d), F32),
        compiler_params=_cparams(("arbitrary",)),
        name="moe_experts",
    )(blk_e, nvalid, xb, row_w, wg, wu, wd)


def _moe(h2, logits, b_rg, b_re, layer, wg, wu, wd):
    n, d = h2.shape
    gp = jax.nn.softmax(logits[:, :N_GROUPS] + b_rg, axis=-1)
    g_idx = jnp.argmax(gp, axis=-1).astype(jnp.int32)
    g_p = jnp.take_along_axis(gp, g_idx[:, None], axis=-1)[:, 0]
    el_all = logits[:, N_GROUPS:N_GROUPS + N_EXPERTS].reshape(n, N_GROUPS, EXPERTS_PER_GROUP) + b_re
    el = jnp.take_along_axis(el_all, g_idx[:, None, None], axis=1)[:, 0]
    top_v, top_i = lax.top_k(el, TOP_K)
    wts = jax.nn.softmax(top_v, axis=-1) * g_p[:, None]
    eid = g_idx[:, None] * EXPERTS_PER_GROUP + top_i.astype(jnp.int32)

    a_len = n * TOP_K
    flat_e = eid.reshape(-1)
    flat_w = wts.reshape(-1)
    order = jnp.argsort(flat_e)
    counts = jnp.zeros((N_EXPERTS,), jnp.int32).at[flat_e].add(1)
    pc = ((counts + MOE_BLK - 1) // MOE_BLK) * MOE_BLK
    start = jnp.cumsum(counts) - counts
    pend = jnp.cumsum(pc)
    pstart = pend - pc
    se = flat_e[order]
    dest = pstart[se] + (jnp.arange(a_len, dtype=jnp.int32) - start[se])
    p_len = a_len + N_EXPERTS * MOE_BLK
    nblk = p_len // MOE_BLK
    row_tok = jnp.zeros((p_len,), jnp.int32).at[dest].set((order // TOP_K).astype(jnp.int32))
    row_w = jnp.zeros((p_len,), F32).at[dest].set(flat_w[order])
    pos = jnp.zeros((a_len,), jnp.int32).at[order].set(dest)
    blk_e = jnp.minimum(jnp.searchsorted(pend, jnp.arange(nblk, dtype=jnp.int32) * MOE_BLK, side='right'),
                        N_EXPERTS - 1).astype(jnp.int32)
    nvalid = (pend[-1:] // MOE_BLK).astype(jnp.int32)

    xb = h2[row_tok]
    yb = _experts(blk_e, nvalid, xb, row_w[:, None], layer, wg, wu, wd)
    pos2 = pos.reshape(n, TOP_K)
    return yb[pos2[:, 0]] + yb[pos2[:, 1]]


def _rope_tables(t_len, qw):
    rows = t_len // GRID_W
    ax = qw // (DIFF_HEADS * 2) // 2
    row = jnp.repeat(jnp.arange(rows, dtype=F32), GRID_W)
    col = jnp.tile(jnp.arange(GRID_W, dtype=F32), rows)
    inv = ROPE_BASE ** (-jnp.arange(0, ax, 2, dtype=F32) / ax)
    ar = row[:, None] * inv[None, :]
    ac = col[:, None] * inv[None, :]
    cr, sr, cc, sc = jnp.cos(ar), jnp.sin(ar), jnp.cos(ac), jnp.sin(ac)
    cos = jnp.concatenate([cr, cr, cc, cc], axis=1)
    sins = jnp.concatenate([-sr, sr, -sc, sc], axis=1)
    reps = qw // cos.shape[1]
    cos = jnp.concatenate([jnp.ones((TM, cos.shape[1]), F32), cos], axis=0)
    sins = jnp.concatenate([jnp.zeros((TM, sins.shape[1]), F32), sins], axis=0)
    return jnp.tile(cos, (1, reps)), jnp.tile(sins, (1, reps))


def kernel(x_prompt, x_sample, cache_k, cache_v, c, c_ctx, w_ada, b_ada, g_norm1, g_norm2, g_final, w_in_e, conv_a_w, conv_a_b, ln_a_g, ln_a_b, lam_q1, lam_k1, lam_q2, lam_k2, subln_g, w_out_e, w_in_o, pool_w, pool_scale, conv_d_w, w_out_o, w_router_g, b_router_g, w_router_e, b_router_e, w_gate, w_up, w_down):
    bp, tp, d = x_prompt.shape
    bs, ts, _ = x_sample.shape
    past = cache_k.shape[2]
    assert tp == TM and ts % TM == 0 and d % LANES == 0
    npt = bp
    tps = ts // TM
    n_p, n_s = bp * tp, bs * ts
    n = n_p + n_s
    nt = n // TM
    cw = conv_a_w.shape[2]
    qw = (w_in_e.shape[2] - 2 * cw) // 3
    dh = qw // (DIFF_HEADS * 2)
    hpt = TM // HALO

    def cond_of(i):
        return jnp.where(i < npt, 0, 1 + (i - npt) // tps)

    def tile(w):
        return pl.BlockSpec((TM, w), lambda i: (i, 0))

    def halo_prev(w):
        return pl.BlockSpec((HALO, w), lambda i: (jnp.maximum(i * hpt - 1, 0), 0))

    def halo_next(w):
        return pl.BlockSpec((HALO, w), lambda i: (jnp.minimum((i + 1) * hpt, nt * hpt - 1), 0))

    def whole(shape):
        return pl.BlockSpec(shape, lambda i: (0,) * len(shape))

    mod_spec = pl.BlockSpec((1, 6, d), lambda i: (cond_of(i), 0, 0))

    rows = 8 * ((1 + bs + 7) // 8)
    cond = jnp.concatenate([c_ctx[None, :], c, jnp.zeros((rows - 1 - bs, d), F32)], axis=0)
    mod = _ada_table(cond, w_ada, b_ada)

    x = jnp.concatenate([x_prompt.reshape(n_p, d), x_sample.reshape(n_s, d)], axis=0)

    def router_w(i):
        wre = jnp.transpose(w_router_e[i], (1, 0, 2)).reshape(d, N_EXPERTS)
        wr = jnp.concatenate([w_router_g[i], wre], axis=1)
        return jnp.pad(wr, ((0, 0), (0, ROUTER_PAD - wr.shape[1]))).astype(BF16)

    cos, sins = _rope_tables(ts, qw)
    tab_spec = pl.BlockSpec((TM, qw), lambda i: (jnp.where(i < npt, 0, 1 + (i - npt) % tps), 0))
    cache_spec = pl.BlockSpec((TM, qw), lambda i: (jnp.minimum(i, npt - 1), 0))
    u, q, k, vt, kc, vc = pl.pallas_call(
        functools.partial(_pre_even_kernel, npt=npt, cw=cw, qw=qw, qscale=dh ** -0.5 * math.log2(math.e)),
        grid=(nt,),
        in_specs=[tile(d), mod_spec, whole((1, d)), whole(w_in_e.shape[1:]), tab_spec, tab_spec],
        out_specs=[tile(cw), tile(qw), tile(qw), pl.BlockSpec((qw, TM), lambda i: (0, i)),
                   cache_spec, cache_spec],
        out_shape=[jax.ShapeDtypeStruct((n, cw), F32), jax.ShapeDtypeStruct((n, qw), BF16),
                   jax.ShapeDtypeStruct((n, qw), BF16), jax.ShapeDtypeStruct((qw, n), BF16),
                   jax.ShapeDtypeStruct((n_p, qw), F32), jax.ShapeDtypeStruct((n_p, qw), F32)],
        compiler_params=_cparams(("arbitrary",)),
        name="pre_even",
    )(x, mod[0], g_norm1[0][None, :], w_in_e[0].astype(BF16), cos, sins)

    lam_init = 0.8 - 0.6 * math.exp(-0.3 * 0)
    lam = (jnp.exp(jnp.sum(lam_q1[0] * lam_k1[0])) - jnp.exp(jnp.sum(lam_q2[0] * lam_k2[0])) + lam_init)
    lam = lam.reshape(1).astype(F32)
    sub_g = subln_g[0][:, None]
    o_p = _attention(lam, q, k, vt, sub_g, None, batch=bp, t_len=tp, row0=0, tq=TM, tk=TM,
                     post_scale=1.0 - lam_init)
    k_ctx = cache_k[:, 0].reshape(bs * past, qw).astype(BF16)
    vt_ctx = jnp.transpose(cache_v[:, 0].reshape(bs, past, qw), (0, 2, 1)).astype(BF16)
    o_s = _attention(lam, q, k, vt, sub_g, (k_ctx, vt_ctx), batch=bs, t_len=ts, row0=n_p, tq=512, tk=512,
                     post_scale=1.0 - lam_init)
    o = jnp.concatenate([o_p, o_s], axis=0)

    moe_outs = [jax.ShapeDtypeStruct((n, d), F32), jax.ShapeDtypeStruct((n, d), BF16),
                jax.ShapeDtypeStruct((n, ROUTER_PAD), F32)]
    moe_out_specs = [tile(d), tile(d), tile(ROUTER_PAD)]
    x1, h2, logits = pl.pallas_call(
        functools.partial(_post_even_kernel, npt=npt, tps=tps),
        grid=(nt,),
        in_specs=[tile(d), tile(cw), halo_prev(cw), halo_next(cw), tile(qw), mod_spec,
                  whole((CONV_K, cw)), whole((1, cw)), whole((1, cw)), whole((1, cw)),
                  whole(w_out_e.shape[1:]), whole((1, d)), whole((d, ROUTER_PAD))],
        out_specs=moe_out_specs,
        out_shape=moe_outs,
        scratch_shapes=[pltpu.VMEM((TM + 2 * HALO, cw), F32)],
        compiler_params=_cparams(("parallel",)),
        name="post_even",
    )(x, u, u, u, o, mod[0], conv_a_w[0], conv_a_b[0][None, :], ln_a_g[0][None, :], ln_a_b[0][None, :],
      w_out_e[0].astype(BF16), g_norm2[0][None, :], router_w(0))
    y = _moe(h2, logits, b_router_g[0], b_router_e[0],
             0, w_gate, w_up, w_down)

    pw = pool_w.shape[2]
    x2, xc, p, bg = pl.pallas_call(
        functools.partial(_pre_odd_kernel, cw=cw),
        grid=(nt,),
        in_specs=[tile(d), tile(d), mod_spec, mod_spec, whole((1, d)), whole(w_in_o.shape[1:])],
        out_specs=[tile(d), tile(cw), tile(cw), tile(cw)],
        out_shape=[jax.ShapeDtypeStruct((n, d), F32)] + [jax.ShapeDtypeStruct((n, cw), F32)] * 3,
        compiler_params=_cparams(("parallel",)),
        name="pre_odd",
    )(x1, y, mod[0], mod[1], g_norm1[1][None, :], w_in_o[0].astype(BF16))
    pool_bd = jnp.zeros((cw, cw), F32)
    for g in range(len(POOL_WINDOWS)):
        pool_bd = pool_bd.at[g * pw:(g + 1) * pw, g * pw:(g + 1) * pw].set(pool_w[0, g])
    x3, h2, logits = pl.pallas_call(
        functools.partial(_post_odd_kernel, npt=npt, tps=tps),
        grid=(nt,),
        in_specs=[tile(d), tile(cw), halo_prev(cw), halo_next(cw), tile(cw), halo_prev(cw), halo_next(cw),
                  tile(cw), mod_spec, whole((cw, cw)), whole((1, cw)), whole((SCONV_K, cw)),
                  whole(w_out_o.shape[1:]), whole((1, d)), whole((d, ROUTER_PAD))],
        out_specs=moe_out_specs,
        out_shape=moe_outs,
        scratch_shapes=[pltpu.VMEM((TM + 2 * HALO, cw), F32), pltpu.VMEM((TM + 2 * HALO, cw), F32)],
        compiler_params=_cparams(("parallel",)),
        name="post_odd",
    )(x2, xc, xc, xc, p, p, p, bg, mod[1], pool_bd.astype(BF16), pool_scale[0][None, :], conv_d_w[0],
      w_out_o[0].astype(BF16), g_norm2[1][None, :], router_w(1))
    y = _moe(h2, logits, b_router_g[1], b_router_e[1],
             1, w_gate, w_up, w_down)

    out_p, out_s = pl.pallas_call(
        functools.partial(_final_kernel, npt=npt),
        grid=(nt,),
        in_specs=[tile(d), tile(d), mod_spec, whole((1, d))],
        out_specs=[pl.BlockSpec((TM, d), lambda i: (jnp.minimum(i, npt - 1), 0)),
                   pl.BlockSpec((TM, d), lambda i: (jnp.maximum(i - npt, 0), 0))],
        out_shape=[jax.ShapeDtypeStruct((n_p, d), F32), jax.ShapeDtypeStruct((n_s, d), F32)],
        compiler_params=_cparams(("arbitrary",)),
        name="final_norm",
    )(x3, y, mod[1], g_final[None, :])

    new_k = kc.reshape(bp, 1, tp, DIFF_HEADS, 2, dh)
    new_v = vc.reshape(bp, 1, tp, DIFF_HEADS, 2 * dh)
    return (out_p.reshape(bp, tp, d), out_s.reshape(bs, ts, d), new_k, new_v)
```

```python
import functools
import math

import jax
import jax.numpy as jnp
from jax import lax
from jax.experimental import pallas as pl
from jax.experimental.pallas import tpu as pltpu

F32 = jnp.float32
BF16 = jnp.bfloat16

EPS = 1e-6
GRID_W = 64
ROPE_BASE = 10000.0
CONV_K = 31
SCONV_K = 3
POOL_WINDOWS = (2, 4, 8, 16)
DIFF_HEADS = 4
N_GROUPS = 4
EXPERTS_PER_GROUP = 8
N_EXPERTS = N_GROUPS * EXPERTS_PER_GROUP
TOP_K = 2

LANES = 128
TM = 256
HALO = 16
MOE_BLK = 256
ROUTER_PAD = 128
ONES_ROWS = 16
VMEM_LIMIT = 56 * 1024 * 1024


def _cparams(sem):
    return pltpu.CompilerParams(dimension_semantics=sem, vmem_limit_bytes=VMEM_LIMIT)


def _rms_mod(x, g, sc, sh):
    ms = jnp.mean(x * x, axis=-1, keepdims=True)
    return (x * lax.rsqrt(ms + EPS)) * g * (1.0 + sc) + sh


def _silu(x):
    return x * jax.nn.sigmoid(x)


def _ada_kernel(c_ref, w_ref, b_ref, o_ref):
    cs = _silu(c_ref[...])
    o_ref[0] = jnp.dot(cs, w_ref[0], precision=lax.Precision.HIGHEST,
                       preferred_element_type=F32) + b_ref[0]


def _ada_table(cond, w_ada, b_ada):
    depth, d, six_d = w_ada.shape
    rows = cond.shape[0]
    nj = six_d // d
    out = pl.pallas_call(
        _ada_kernel,
        grid=(depth, nj),
        in_specs=[pl.BlockSpec((rows, d), lambda l, j: (0, 0)),
                  pl.BlockSpec((1, d, d), lambda l, j: (l, 0, j)),
                  pl.BlockSpec((1, 1, d), lambda l, j: (l, 0, j))],
        out_specs=pl.BlockSpec((1, rows, d), lambda l, j: (l, 0, j)),
        out_shape=jax.ShapeDtypeStruct((depth, rows, six_d), F32),
        compiler_params=_cparams(("parallel", "parallel")),
        name="ada_table",
    )(cond, w_ada, b_ada.reshape(depth, 1, six_d))
    return out.reshape(depth, rows, nj, d)


def _rope(x, cos, sins, first_half):
    outs = []
    for j in range(x.shape[1] // LANES):
        sl = slice(j * LANES, (j + 1) * LANES)
        xs = x[:, sl]
        nxt = pltpu.roll(xs, LANES - 16, 1)
        prv = pltpu.roll(xs, 16, 1)
        rot = jnp.where(first_half, nxt, prv)
        outs.append(xs * cos[:, sl] + rot * sins[:, sl])
    return jnp.concatenate(outs, axis=1)


def _pre_even_kernel(x_ref, mod_ref, g_ref, w_ref, cos_ref, sin_ref,
                     u_ref, q_ref, k_ref, vt_ref, kc_ref, vc_ref, *, npt, cw, qw, qscale):
    i = pl.program_id(0)
    h = _rms_mod(x_ref[...], g_ref[...], mod_ref[0, 1:2, :], mod_ref[0, 0:1, :])
    proj = jnp.dot(h.astype(BF16), w_ref[...], preferred_element_type=F32)
    ga = proj[:, :cw]
    gb = proj[:, cw:2 * cw]
    u_ref[...] = ga * jax.nn.sigmoid(gb)
    q = proj[:, 2 * cw:2 * cw + qw]
    k = proj[:, 2 * cw + qw:2 * cw + 2 * qw]
    v = proj[:, 2 * cw + 2 * qw:]

    @pl.when(i < npt)
    def _():
        kc_ref[...] = k
        vc_ref[...] = v

    lane = lax.broadcasted_iota(jnp.int32, (1, LANES), 1)
    first_half = (lane % 32) < 16
    cos = cos_ref[...]
    sins = sin_ref[...]
    q_ref[...] = (_rope(q, cos, sins, first_half) * qscale).astype(BF16)
    k_ref[...] = _rope(k, cos, sins, first_half).astype(BF16)
    vt_ref[...] = v.T.astype(BF16)


def _attn_kernel(lam_ref, q_ref, k_ref, vt_ref, *rest, nk, tk, nk_ctx, post_scale):
    if nk_ctx:
        kc_ref, vtc_ref, g_ref, o_ref, s_sc, mc_sc, m_sc, acc_sc = rest
    else:
        g_ref, o_ref, s_sc, mc_sc, m_sc, acc_sc = rest
    q = q_ref[...]
    hw = q.shape[1]
    dh = hw // 2
    lane = lax.broadcasted_iota(jnp.int32, (1, hw), 1)
    zero = jnp.zeros_like(q)
    qs = (jnp.where(lane < dh, q, zero), jnp.where(lane >= dh, q, zero))
    m_sc[...] = jnp.full(m_sc.shape, -jnp.inf, F32)
    acc_sc[...] = jnp.zeros(acc_sc.shape, F32)
    ones_rows = (lax.broadcasted_iota(jnp.int32, (ONES_ROWS, tk), 0) == 0).astype(BF16)

    def scores(kk, slot):
        for c in range(2):
            s = lax.dot_general(kk, qs[c], (((1,), (1,)), ((), ())), preferred_element_type=F32)
            s_sc[slot, c] = s
            mc_sc[slot, c] = jnp.max(s, axis=0, keepdims=True)

    def consume(vt, slot):
        vta = jnp.concatenate([vt, ones_rows], axis=0)
        for c in range(2):
            m_prev = m_sc[c]
            m_new = jnp.maximum(m_prev, mc_sc[slot, c])
            alpha = jnp.exp2(m_prev - m_new)
            p = jnp.exp2(s_sc[slot, c] - m_new).astype(BF16)
            acc_sc[c] = alpha * acc_sc[c] + jnp.dot(vta, p, preferred_element_type=F32)
            m_sc[c] = m_new

    def k_at(i):
        if i < nk_ctx:
            return kc_ref[i * tk:(i + 1) * tk, :]
        return k_ref[(i - nk_ctx) * tk:(i - nk_ctx + 1) * tk, :]

    def vt_at(i):
        if i < nk_ctx:
            return vtc_ref[0, :, i * tk:(i + 1) * tk]
        return vt_ref[:, (i - nk_ctx) * tk:(i - nk_ctx + 1) * tk]

    n_total = nk_ctx + nk
    peel = nk_ctx + (n_total - 1 - nk_ctx) % 2
    scores(k_at(0), 0)
    for i in range(peel):
        scores(k_at(i + 1), (i + 1) % 2)
        consume(vt_at(i), i % 2)

    def body(t, carry):
        for r in range(2):
            j = 2 * t + (peel + r - nk_ctx)
            nxt = pl.multiple_of((j + 1) * tk, tk)
            cur = pl.multiple_of(j * tk, tk)
            scores(k_ref[pl.ds(nxt, tk), :], (peel + r + 1) % 2)
            consume(vt_ref[:, pl.ds(cur, tk)], (peel + r) % 2)
        return carry

    lax.fori_loop(0, (n_total - 1 - peel) // 2, body, 0)
    consume(vt_at(n_total - 1), (n_total - 1) % 2)
    o = (acc_sc[0, :hw, :] / acc_sc[0, hw:hw + 1, :]
         - lam_ref[0] * (acc_sc[1, :hw, :] / acc_sc[1, hw:hw + 1, :]))
    ms = jnp.mean(o * o, axis=0, keepdims=True)
    o = o * lax.rsqrt(ms + EPS) * g_ref[...] * post_scale
    o_ref[...] = o.T.astype(BF16)


def _attention(lam, q, k, vt, g_col, ctx, *, batch, t_len, row0, tq, tk, post_scale):
    hw = q.shape[1] // DIFF_HEADS
    nq = t_len // tq
    assert row0 % t_len == 0 and t_len % tk == 0
    qb0 = row0 // tq
    sb0 = row0 // t_len
    in_specs = [pl.BlockSpec(memory_space=pltpu.SMEM),
                pl.BlockSpec((tq, hw), lambda b, h, i: (qb0 + b * nq + i, h)),
                pl.BlockSpec((t_len, hw), lambda b, h, i: (sb0 + b, h)),
                pl.BlockSpec((hw, t_len), lambda b, h, i: (h, sb0 + b))]
    args = [lam, q, k, vt]
    nk_ctx = 0
    if ctx is not None:
        k_ctx, vt_ctx = ctx
        past = vt_ctx.shape[2]
        assert past % tk == 0
        nk_ctx = past // tk
        in_specs += [pl.BlockSpec((past, hw), lambda b, h, i: (b, h)),
                     pl.BlockSpec((1, hw, past), lambda b, h, i: (b, h, 0))]
        args += [k_ctx, vt_ctx]
    in_specs.append(pl.BlockSpec((hw, 1), lambda b, h, i: (0, 0)))
    args.append(g_col)
    return pl.pallas_call(
        functools.partial(_attn_kernel, nk=t_len // tk, tk=tk, nk_ctx=nk_ctx, post_scale=post_scale),
        grid=(batch, DIFF_HEADS, nq),
        in_specs=in_specs,
        out_specs=pl.BlockSpec((tq, hw), lambda b, h, i: (b * nq + i, h)),
        out_shape=jax.ShapeDtypeStruct((batch * t_len, q.shape[1]), BF16),
        scratch_shapes=[pltpu.VMEM((2, 2, tk, tq), F32), pltpu.VMEM((2, 2, 1, tq), F32),
                        pltpu.VMEM((2, 1, tq), F32), pltpu.VMEM((2, hw + ONES_ROWS, tq), F32)],
        compiler_params=_cparams(("parallel", "parallel", "parallel")),
        name="diff_attention",
    )(*args)


def _seq_edges(i, npt, tps):
    r = (i - npt) % tps
    is_prompt = i < npt
    first = jnp.logical_or(is_prompt, r == 0)
    last = jnp.logical_or(is_prompt, r == tps - 1)
    return first, last


def _fill_ext(ext_sc, cur_ref, prev_ref, next_ref, first, last):
    prev = prev_ref[...]
    nxt = next_ref[...]
    ext_sc[0:HALO, :] = jnp.where(first, jnp.zeros_like(prev), prev)
    ext_sc[HALO:HALO + TM, :] = cur_ref[...]
    ext_sc[HALO + TM:, :] = jnp.where(last, jnp.zeros_like(nxt), nxt)


def _moe_prologue(x_new, mod_ref, g2_ref, wr_ref, h2_ref, lg_ref):
    h2 = _rms_mod(x_new, g2_ref[...], mod_ref[0, 4:5, :], mod_ref[0, 3:4, :]).astype(BF16)
    h2_ref[...] = h2
    lg_ref[...] = jnp.dot(h2, wr_ref[...], preferred_element_type=F32)


def _post_even_kernel(x_ref, u_ref, up_ref, un_ref, o_ref, mod_ref, cw_ref, cb_ref, lng_ref, lnb_ref,
                      wo_ref, g2_ref, wr_ref, x1_ref, h2_ref, lg_ref, ext_sc, *, npt, tps):
    i = pl.program_id(0)
    first, last = _seq_edges(i, npt, tps)
    _fill_ext(ext_sc, u_ref, up_ref, un_ref, first, last)
    cw = u_ref.shape[1]
    rows = 32
    chunks = []
    for rc in range(TM // rows):
        acc = None
        for k in range(CONV_K):
            start = rc * rows + HALO - CONV_K // 2 + k
            term = ext_sc[start:start + rows, :] * cw_ref[k:k + 1, :]
            acc = term if acc is None else acc + term
        chunks.append(acc)
    conv = jnp.concatenate(chunks, axis=0) + cb_ref[...]
    mu = jnp.mean(conv, axis=-1, keepdims=True)
    cen = conv - mu
    var = jnp.mean(cen * cen, axis=-1, keepdims=True)
    uu = _silu(cen * lax.rsqrt(var + EPS) * lng_ref[...] + lnb_ref[...])
    m = (jnp.dot(uu.astype(BF16), wo_ref[0:cw, :], preferred_element_type=F32)
         + jnp.dot(o_ref[...], wo_ref[cw:, :], preferred_element_type=F32))
    x1 = x_ref[...] + mod_ref[0, 2:3, :] * m
    x1_ref[...] = x1
    _moe_prologue(x1, mod_ref, g2_ref, wr_ref, h2_ref, lg_ref)


def _pre_odd_kernel(x_ref, y_ref, modp_ref, mod_ref, g_ref, w_ref, x2_ref, xc_ref, p_ref, bg_ref, *, cw):
    x2 = x_ref[...] + modp_ref[0, 5:6, :] * y_ref[...]
    x2_ref[...] = x2
    h = _rms_mod(x2, g_ref[...], mod_ref[0, 1:2, :], mod_ref[0, 0:1, :])
    proj = jnp.dot(h.astype(BF16), w_ref[...], preferred_element_type=F32)
    xc_ref[...] = proj[:, :cw]
    hd = proj[:, cw:2 * cw]
    bg_ref[...] = proj[:, 2 * cw:3 * cw]
    p_ref[...] = proj[:, 3 * cw:] * hd


def _post_odd_kernel(x_ref, xc_ref, xcp_ref, xcn_ref, p_ref, pp_ref, pn_ref, bg_ref, mod_ref,
                     pw_ref, ps_ref, dw_ref, wo_ref, g2_ref, wr_ref,
                     x3_ref, h2_ref, lg_ref, extc_sc, extp_sc, *, npt, tps):
    i = pl.program_id(0)
    first, last = _seq_edges(i, npt, tps)
    _fill_ext(extc_sc, xc_ref, xcp_ref, xcn_ref, first, last)
    _fill_ext(extp_sc, p_ref, pp_ref, pn_ref, first, last)
    cw = xc_ref.shape[1]
    gc = cw // len(POOL_WINDOWS)
    is_prompt = i < npt
    t_len = jnp.where(is_prompt, TM, TM * tps)
    pos = lax.broadcasted_iota(jnp.int32, (TM, 1), 0) + jnp.where(is_prompt, 0, ((i - npt) % tps) * TM)
    ds = []
    for g, w in enumerate(POOL_WINDOWS):
        sl = slice(g * gc, (g + 1) * gc)
        acc = None
        for o in range(-(w // 2), w // 2):
            term = extc_sc[HALO + o:HALO + o + TM, sl]
            acc = term if acc is None else acc + term
        lo = jnp.maximum(pos - w // 2, 0)
        hi = jnp.minimum(pos + w // 2 - 1, t_len - 1)
        cnt = (hi - lo + 1).astype(F32)
        ds.append(acc / cnt - extc_sc[HALO:HALO + TM, sl])
    d = jnp.concatenate(ds, axis=1).astype(BF16)
    yc = jnp.dot(d, pw_ref[...], preferred_element_type=F32) * ps_ref[...]
    conv = None
    for k in range(SCONV_K):
        start = HALO - SCONV_K // 2 + k
        term = extp_sc[start:start + TM, :] * dw_ref[k:k + 1, :]
        conv = term if conv is None else conv + term
    yd = bg_ref[...] * conv
    m = (jnp.dot(yc.astype(BF16), wo_ref[0:cw, :], preferred_element_type=F32)
         + jnp.dot(yd.astype(BF16), wo_ref[cw:, :], preferred_element_type=F32))
    x3 = x_ref[...] + mod_ref[0, 2:3, :] * m
    x3_ref[...] = x3
    _moe_prologue(x3, mod_ref, g2_ref, wr_ref, h2_ref, lg_ref)


def _final_kernel(x_ref, y_ref, modp_ref, g_ref, op_ref, os_ref, *, npt):
    i = pl.program_id(0)
    x = x_ref[...] + modp_ref[0, 5:6, :] * y_ref[...]
    ms = jnp.mean(x * x, axis=-1, keepdims=True)
    out = x * lax.rsqrt(ms + EPS) * g_ref[...]

    @pl.when(i < npt)
    def _():
        op_ref[...] = out

    @pl.when(i >= npt)
    def _():
        os_ref[...] = out


def _expert_kernel(be_ref, nv_ref, x_ref, wg_ref, wu_ref, wd_ref, y_ref, wg_sc, wu_sc, wd_sc):
    i = pl.program_id(0)

    @pl.when(jnp.logical_or(i == 0, be_ref[i] != be_ref[jnp.maximum(i - 1, 0)]))
    def _():
        wg_sc[...] = wg_ref[0, 0].astype(BF16)
        wu_sc[...] = wu_ref[0, 0].astype(BF16)
        wd_sc[...] = wd_ref[0, 0].astype(BF16)

    @pl.when(i < nv_ref[0])
    def _():
        x = x_ref[...]
        hg = jnp.dot(x, wg_sc[...], preferred_element_type=F32)
        hu = jnp.dot(x, wu_sc[...], preferred_element_type=F32)
        hid = (_silu(hg) * hu).astype(BF16)
        y_ref[...] = jnp.dot(hid, wd_sc[...], preferred_element_type=F32)

    @pl.when(i >= nv_ref[0])
    def _():
        y_ref[...] = jnp.zeros(y_ref.shape, F32)


def _experts(blk_e, nvalid, xb, layer, wg, wu, wd):
    p_len, d = xb.shape
    hid = wg.shape[3]
    nblk = p_len // MOE_BLK
    grid_spec = pltpu.PrefetchScalarGridSpec(
        num_scalar_prefetch=2,
        grid=(nblk,),
        in_specs=[pl.BlockSpec((MOE_BLK, d), lambda i, be, nv: (i, 0)),
                  pl.BlockSpec((1, 1, d, hid), lambda i, be, nv: (layer, be[i], 0, 0)),
                  pl.BlockSpec((1, 1, d, hid), lambda i, be, nv: (layer, be[i], 0, 0)),
                  pl.BlockSpec((1, 1, hid, d), lambda i, be, nv: (layer, be[i], 0, 0))],
        out_specs=pl.BlockSpec((MOE_BLK, d), lambda i, be, nv: (i, 0)),
        scratch_shapes=[pltpu.VMEM((d, hid), BF16), pltpu.VMEM((d, hid), BF16), pltpu.VMEM((hid, d), BF16)],
    )
    return pl.pallas_call(
        _expert_kernel,
        grid_spec=grid_spec,
        out_shape=jax.ShapeDtypeStruct((p_len, d), F32),
        compiler_params=_cparams(("arbitrary",)),
        name="moe_experts",
    )(blk_e, nvalid, xb, wg, wu, wd)


def _route_kernel(lg_ref, b_ref, info_ref, cnt_ref):
    lt = lg_ref[...].T + b_ref[...]
    gl = lt[N_EXPERTS:N_EXPERTS + N_GROUPS]
    ge = jnp.exp(gl - jnp.max(gl, axis=0, keepdims=True))
    gp = ge / jnp.sum(ge, axis=0, keepdims=True)
    best = gp[0:1]
    gidx = jnp.zeros(best.shape, jnp.int32)
    el = lt[0:EXPERTS_PER_GROUP]
    for g in range(1, N_GROUPS):
        better = gp[g:g + 1] > best
        gidx = jnp.where(better, g, gidx)
        best = jnp.where(better, gp[g:g + 1], best)
    for g in range(1, N_GROUPS):
        el = jnp.where(gidx == g, lt[g * EXPERTS_PER_GROUP:(g + 1) * EXPERTS_PER_GROUP], el)
    rio = lax.broadcasted_iota(jnp.int32, el.shape, 0)
    v0 = jnp.max(el, axis=0, keepdims=True)
    i0 = jnp.min(jnp.where(el == v0, rio, EXPERTS_PER_GROUP), axis=0, keepdims=True)
    el2 = jnp.where(rio == i0, -jnp.inf, el)
    v1 = jnp.max(el2, axis=0, keepdims=True)
    i1 = jnp.min(jnp.where(el2 == v1, rio, EXPERTS_PER_GROUP), axis=0, keepdims=True)
    e1 = jnp.exp(v1 - v0)
    den = 1.0 + e1
    w0 = (1.0 / den) * best
    w1 = (e1 / den) * best
    eid0 = gidx * EXPERTS_PER_GROUP + i0
    eid1 = gidx * EXPERTS_PER_GROUP + i1
    tm = lt.shape[1]
    eio = lax.broadcasted_iota(jnp.int32, (N_EXPERTS, tm), 0)
    oh0 = eio == eid0
    oh1 = eio == eid1
    ohs = jnp.where(oh0, 1.0, 0.0) + jnp.where(oh1, 1.0, 0.0)
    upper = (lax.broadcasted_iota(jnp.int32, (tm, tm), 0)
             < lax.broadcasted_iota(jnp.int32, (tm, tm), 1)).astype(BF16)
    before = jnp.dot(ohs.astype(BF16), upper, preferred_element_type=F32)
    rank0 = jnp.sum(jnp.where(oh0, before, 0.0), axis=0, keepdims=True)
    rank1 = jnp.sum(jnp.where(oh1, before, 0.0), axis=0, keepdims=True)
    info_ref[0] = jnp.concatenate([eid0.astype(F32), eid1.astype(F32), w0, w1, rank0, rank1,
                                   jnp.zeros((2, tm), F32)], axis=0)
    cnt_ref[0] = jnp.sum(ohs, axis=1, keepdims=True)


def _route(logits, bias_col):
    n = logits.shape[0]
    nt = n // TM
    return pl.pallas_call(
        _route_kernel,
        grid=(nt,),
        in_specs=[pl.BlockSpec((TM, ROUTER_PAD), lambda i: (i, 0)),
                  pl.BlockSpec((ROUTER_PAD, 1), lambda i: (0, 0))],
        out_specs=[pl.BlockSpec((1, 8, TM), lambda i: (i, 0, 0)),
                   pl.BlockSpec((1, N_EXPERTS, 1), lambda i: (i, 0, 0))],
        out_shape=[jax.ShapeDtypeStruct((nt, 8, TM), F32), jax.ShapeDtypeStruct((nt, N_EXPERTS, 1), F32)],
        compiler_params=_cparams(("parallel",)),
        name="moe_route",
    )(logits, bias_col)


def _moe(h2, logits, bias_col, layer, wg, wu, wd):
    n, d = h2.shape
    info, cnt = _route(logits, bias_col)
    cnt = cnt[:, :, 0].astype(jnp.int32)
    counts = jnp.sum(cnt, axis=0)
    pc = ((counts + MOE_BLK - 1) // MOE_BLK) * MOE_BLK
    pend = jnp.cumsum(pc)
    base = (pend - pc)[None, :] + jnp.cumsum(cnt, axis=0) - cnt
    eid = info[:, 0:2, :].astype(jnp.int32)
    rank = info[:, 4:6, :].astype(jnp.int32)
    sel = eid[..., None] == jnp.arange(N_EXPERTS, dtype=jnp.int32)
    dest = jnp.sum(jnp.where(sel, base[:, None, None, :], 0), axis=-1) + rank
    d0 = dest[:, 0, :].reshape(n)
    d1 = dest[:, 1, :].reshape(n)
    w0 = info[:, 2, :].reshape(n, 1)
    w1 = info[:, 3, :].reshape(n, 1)
    p_len = n * TOP_K + N_EXPERTS * MOE_BLK
    nblk = p_len // MOE_BLK
    blk_e = jnp.minimum(jnp.searchsorted(pend, jnp.arange(nblk, dtype=jnp.int32) * MOE_BLK, side='right'),
                        N_EXPERTS - 1).astype(jnp.int32)
    nvalid = (pend[-1:] // MOE_BLK).astype(jnp.int32)

    xb = jnp.zeros((p_len, d), h2.dtype)
    xb = xb.at[d0].set(h2, unique_indices=True)
    xb = xb.at[d1].set(h2, unique_indices=True)
    yb = _experts(blk_e, nvalid, xb, layer, wg, wu, wd)
    return w0 * yb[d0] + w1 * yb[d1]


def _rope_tables(t_len, qw):
    rows = t_len // GRID_W
    ax = qw // (DIFF_HEADS * 2) // 2
    row = jnp.repeat(jnp.arange(rows, dtype=F32), GRID_W)
    col = jnp.tile(jnp.arange(GRID_W, dtype=F32), rows)
    inv = ROPE_BASE ** (-jnp.arange(0, ax, 2, dtype=F32) / ax)
    ar = row[:, None] * inv[None, :]
    ac = col[:, None] * inv[None, :]
    cr, sr, cc, sc = jnp.cos(ar), jnp.sin(ar), jnp.cos(ac), jnp.sin(ac)
    cos = jnp.concatenate([cr, cr, cc, cc], axis=1)
    sins = jnp.concatenate([-sr, sr, -sc, sc], axis=1)
    reps = qw // cos.shape[1]
    cos = jnp.concatenate([jnp.ones((TM, cos.shape[1]), F32), cos], axis=0)
    sins = jnp.concatenate([jnp.zeros((TM, sins.shape[1]), F32), sins], axis=0)
    return jnp.tile(cos, (1, reps)), jnp.tile(sins, (1, reps))


def kernel(x_prompt, x_sample, cache_k, cache_v, c, c_ctx, w_ada, b_ada, g_norm1, g_norm2, g_final, w_in_e, conv_a_w, conv_a_b, ln_a_g, ln_a_b, lam_q1, lam_k1, lam_q2, lam_k2, subln_g, w_out_e, w_in_o, pool_w, pool_scale, conv_d_w, w_out_o, w_router_g, b_router_g, w_router_e, b_router_e, w_gate, w_up, w_down):
    bp, tp, d = x_prompt.shape
    bs, ts, _ = x_sample.shape
    past = cache_k.shape[2]
    assert tp == TM and ts % TM == 0 and d % LANES == 0
    npt = bp
    tps = ts // TM
    n_p, n_s = bp * tp, bs * ts
    n = n_p + n_s
    nt = n // TM
    cw = conv_a_w.shape[2]
    qw = (w_in_e.shape[2] - 2 * cw) // 3
    dh = qw // (DIFF_HEADS * 2)
    hpt = TM // HALO

    def cond_of(i):
        return jnp.where(i < npt, 0, 1 + (i - npt) // tps)

    def tile(w):
        return pl.BlockSpec((TM, w), lambda i: (i, 0))

    def halo_prev(w):
        return pl.BlockSpec((HALO, w), lambda i: (jnp.maximum(i * hpt - 1, 0), 0))

    def halo_next(w):
        return pl.BlockSpec((HALO, w), lambda i: (jnp.minimum((i + 1) * hpt, nt * hpt - 1), 0))

    def whole(shape):
        return pl.BlockSpec(shape, lambda i: (0,) * len(shape))

    mod_spec = pl.BlockSpec((1, 6, d), lambda i: (cond_of(i), 0, 0))

    rows = 8 * ((1 + bs + 7) // 8)
    cond = jnp.concatenate([c_ctx[None, :], c, jnp.zeros((rows - 1 - bs, d), F32)], axis=0)
    mod = _ada_table(cond, w_ada, b_ada)

    x = jnp.concatenate([x_prompt.reshape(n_p, d), x_sample.reshape(n_s, d)], axis=0)

    def router_w(i):
        wre = jnp.transpose(w_router_e[i], (1, 0, 2)).reshape(d, N_EXPERTS)
        wr = jnp.concatenate([wre, w_router_g[i]], axis=1)
        return jnp.pad(wr, ((0, 0), (0, ROUTER_PAD - wr.shape[1]))).astype(BF16)

    def router_b(i):
        b = jnp.concatenate([b_router_e[i].reshape(N_EXPERTS), b_router_g[i]])
        return jnp.pad(b, (0, ROUTER_PAD - b.shape[0]))[:, None]

    cos, sins = _rope_tables(ts, qw)
    tab_spec = pl.BlockSpec((TM, qw), lambda i: (jnp.where(i < npt, 0, 1 + (i - npt) % tps), 0))
    cache_spec = pl.BlockSpec((TM, qw), lambda i: (jnp.minimum(i, npt - 1), 0))
    u, q, k, vt, kc, vc = pl.pallas_call(
        functools.partial(_pre_even_kernel, npt=npt, cw=cw, qw=qw, qscale=dh ** -0.5 * math.log2(math.e)),
        grid=(nt,),
        in_specs=[tile(d), mod_spec, whole((1, d)), whole(w_in_e.shape[1:]), tab_spec, tab_spec],
        out_specs=[tile(cw), tile(qw), tile(qw), pl.BlockSpec((qw, TM), lambda i: (0, i)),
                   cache_spec, cache_spec],
        out_shape=[jax.ShapeDtypeStruct((n, cw), F32), jax.ShapeDtypeStruct((n, qw), BF16),
                   jax.ShapeDtypeStruct((n, qw), BF16), jax.ShapeDtypeStruct((qw, n), BF16),
                   jax.ShapeDtypeStruct((n_p, qw), F32), jax.ShapeDtypeStruct((n_p, qw), F32)],
        compiler_params=_cparams(("arbitrary",)),
        name="pre_even",
    )(x, mod[0], g_norm1[0][None, :], w_in_e[0].astype(BF16), cos, sins)

    lam_init = 0.8 - 0.6 * math.exp(-0.3 * 0)
    lam = (jnp.exp(jnp.sum(lam_q1[0] * lam_k1[0])) - jnp.exp(jnp.sum(lam_q2[0] * lam_k2[0])) + lam_init)
    lam = lam.reshape(1).astype(F32)
    sub_g = subln_g[0][:, None]
    o_p = _attention(lam, q, k, vt, sub_g, None, batch=bp, t_len=tp, row0=0, tq=TM, tk=TM,
                     post_scale=1.0 - lam_init)
    k_ctx = cache_k[:, 0].reshape(bs * past, qw).astype(BF16)
    vt_ctx = jnp.transpose(cache_v[:, 0].reshape(bs, past, qw), (0, 2, 1)).astype(BF16)
    o_s = _attention(lam, q, k, vt, sub_g, (k_ctx, vt_ctx), batch=bs, t_len=ts, row0=n_p, tq=512, tk=512,
                     post_scale=1.0 - lam_init)
    o = jnp.concatenate([o_p, o_s], axis=0)

    moe_outs = [jax.ShapeDtypeStruct((n, d), F32), jax.ShapeDtypeStruct((n, d), BF16),
                jax.ShapeDtypeStruct((n, ROUTER_PAD), F32)]
    moe_out_specs = [tile(d), tile(d), tile(ROUTER_PAD)]
    x1, h2, logits = pl.pallas_call(
        functools.partial(_post_even_kernel, npt=npt, tps=tps),
        grid=(nt,),
        in_specs=[tile(d), tile(cw), halo_prev(cw), halo_next(cw), tile(qw), mod_spec,
                  whole((CONV_K, cw)), whole((1, cw)), whole((1, cw)), whole((1, cw)),
                  whole(w_out_e.shape[1:]), whole((1, d)), whole((d, ROUTER_PAD))],
        out_specs=moe_out_specs,
        out_shape=moe_outs,
        scratch_shapes=[pltpu.VMEM((TM + 2 * HALO, cw), F32)],
        compiler_params=_cparams(("parallel",)),
        name="post_even",
    )(x, u, u, u, o, mod[0], conv_a_w[0], conv_a_b[0][None, :], ln_a_g[0][None, :], ln_a_b[0][None, :],
      w_out_e[0].astype(BF16), g_norm2[0][None, :], router_w(0))
    y = _moe(h2, logits, router_b(0), 0, w_gate, w_up, w_down)

    pw = pool_w.shape[2]
    x2, xc, p, bg = pl.pallas_call(
        functools.partial(_pre_odd_kernel, cw=cw),
        grid=(nt,),
        in_specs=[tile(d), tile(d), mod_spec, mod_spec, whole((1, d)), whole(w_in_o.shape[1:])],
        out_specs=[tile(d), tile(cw), tile(cw), tile(cw)],
        out_shape=[jax.ShapeDtypeStruct((n, d), F32)] + [jax.ShapeDtypeStruct((n, cw), F32)] * 3,
        compiler_params=_cparams(("parallel",)),
        name="pre_odd",
    )(x1, y, mod[0], mod[1], g_norm1[1][None, :], w_in_o[0].astype(BF16))
    pool_bd = jnp.zeros((cw, cw), F32)
    for g in range(len(POOL_WINDOWS)):
        pool_bd = pool_bd.at[g * pw:(g + 1) * pw, g * pw:(g + 1) * pw].set(pool_w[0, g])
    x3, h2, logits = pl.pallas_call(
        functools.partial(_post_odd_kernel, npt=npt, tps=tps),
        grid=(nt,),
        in_specs=[tile(d), tile(cw), halo_prev(cw), halo_next(cw), tile(cw), halo_prev(cw), halo_next(cw),
                  tile(cw), mod_spec, whole((cw, cw)), whole((1, cw)), whole((SCONV_K, cw)),
                  whole(w_out_o.shape[1:]), whole((1, d)), whole((d, ROUTER_PAD))],
        out_specs=moe_out_specs,
        out_shape=moe_outs,
        scratch_shapes=[pltpu.VMEM((TM + 2 * HALO, cw), F32), pltpu.VMEM((TM + 2 * HALO, cw), F32)],
        compiler_params=_cparams(("parallel",)),
        name="post_odd",
    )(x2, xc, xc, xc, p, p, p, bg, mod[1], pool_bd.astype(BF16), pool_scale[0][None, :], conv_d_w[0],
      w_out_o[0].astype(BF16), g_norm2[1][None, :], router_w(1))
    y = _moe(h2, logits, router_b(1), 1, w_gate, w_up, w_down)

    out_p, out_s = pl.pallas_call(
        functools.partial(_final_kernel, npt=npt),
        grid=(nt,),
        in_specs=[tile(d), tile(d), mod_spec, whole((1, d))],
        out_specs=[pl.BlockSpec((TM, d), lambda i: (jnp.minimum(i, npt - 1), 0)),
                   pl.BlockSpec((TM, d), lambda i: (jnp.maximum(i - npt, 0), 0))],
        out_shape=[jax.ShapeDtypeStruct((n_p, d), F32), jax.ShapeDtypeStruct((n_s, d), F32)],
        compiler_params=_cparams(("arbitrary",)),
        name="final_norm",
    )(x3, y, mod[1], g_final[None, :])

    new_k = kc.reshape(bp, 1, tp, DIFF_HEADS, 2, dh)
    new_v = vc.reshape(bp, 1, tp, DIFF_HEADS, 2 * dh)
    return (out_p.reshape(bp, tp, d), out_s.reshape(bs, ts, d), new_k, new_v)
```

```python
import functools
import math

import jax
import jax.numpy as jnp
from jax import lax
from jax.experimental import pallas as pl
from jax.experimental.pallas import tpu as pltpu

F32 = jnp.float32
BF16 = jnp.bfloat16

EPS = 1e-6
GRID_W = 64
ROPE_BASE = 10000.0
CONV_K = 31
SCONV_K = 3
POOL_WINDOWS = (2, 4, 8, 16)
DIFF_HEADS = 4
N_GROUPS = 4
EXPERTS_PER_GROUP = 8
N_EXPERTS = N_GROUPS * EXPERTS_PER_GROUP
TOP_K = 2

LANES = 128
TM = 256
HALO = 16
MOE_BLK = 256
ROUTER_PAD = 128
ONES_ROWS = 16
VMEM_LIMIT = 56 * 1024 * 1024


def _cparams(sem):
    return pltpu.CompilerParams(dimension_semantics=sem, vmem_limit_bytes=VMEM_LIMIT)


def _rms_mod(x, g, sc, sh):
    ms = jnp.mean(x * x, axis=-1, keepdims=True)
    return (x * lax.rsqrt(ms + EPS)) * g * (1.0 + sc) + sh


def _silu(x):
    return x * jax.nn.sigmoid(x)


def _ada_kernel(c_ref, w_ref, b_ref, o_ref):
    cs = _silu(c_ref[...])
    o_ref[0] = jnp.dot(cs, w_ref[0], precision=lax.Precision.HIGHEST,
                       preferred_element_type=F32) + b_ref[0]


def _ada_table(cond, w_ada, b_ada):
    depth, d, six_d = w_ada.shape
    rows = cond.shape[0]
    nj = six_d // d
    out = pl.pallas_call(
        _ada_kernel,
        grid=(depth, nj),
        in_specs=[pl.BlockSpec((rows, d), lambda l, j: (0, 0)),
                  pl.BlockSpec((1, d, d), lambda l, j: (l, 0, j)),
                  pl.BlockSpec((1, 1, d), lambda l, j: (l, 0, j))],
        out_specs=pl.BlockSpec((1, rows, d), lambda l, j: (l, 0, j)),
        out_shape=jax.ShapeDtypeStruct((depth, rows, six_d), F32),
        compiler_params=_cparams(("parallel", "parallel")),
        name="ada_table",
    )(cond, w_ada, b_ada.reshape(depth, 1, six_d))
    return out.reshape(depth, rows, nj, d)


def _rope(x, cos, sins, first_half):
    outs = []
    for j in range(x.shape[1] // LANES):
        sl = slice(j * LANES, (j + 1) * LANES)
        xs = x[:, sl]
        nxt = pltpu.roll(xs, LANES - 16, 1)
        prv = pltpu.roll(xs, 16, 1)
        rot = jnp.where(first_half, nxt, prv)
        outs.append(xs * cos[:, sl] + rot * sins[:, sl])
    return jnp.concatenate(outs, axis=1)


def _pre_even_kernel(x_ref, mod_ref, g_ref, w_ref, cos_ref, sin_ref,
                     u_ref, q_ref, k_ref, vt_ref, kc_ref, vc_ref, *, npt, cw, qw, qscale):
    i = pl.program_id(0)
    h = _rms_mod(x_ref[...], g_ref[...], mod_ref[0, 1:2, :], mod_ref[0, 0:1, :])
    proj = jnp.dot(h.astype(BF16), w_ref[...], preferred_element_type=F32)
    ga = proj[:, :cw]
    gb = proj[:, cw:2 * cw]
    u_ref[...] = ga * jax.nn.sigmoid(gb)
    q = proj[:, 2 * cw:2 * cw + qw]
    k = proj[:, 2 * cw + qw:2 * cw + 2 * qw]
    v = proj[:, 2 * cw + 2 * qw:]

    @pl.when(i < npt)
    def _():
        kc_ref[...] = k
        vc_ref[...] = v

    lane = lax.broadcasted_iota(jnp.int32, (1, LANES), 1)
    first_half = (lane % 32) < 16
    cos = cos_ref[...]
    sins = sin_ref[...]
    q_ref[...] = (_rope(q, cos, sins, first_half) * qscale).astype(BF16)
    k_ref[...] = _rope(k, cos, sins, first_half).astype(BF16)
    vt_ref[...] = v.T.astype(BF16)


def _attn_kernel(lam_ref, q_ref, k_ref, vt_ref, *rest, nk, tk, nk_ctx, post_scale):
    if nk_ctx:
        kc_ref, vtc_ref, g_ref, o_ref, s_sc, mc_sc, m_sc, acc_sc = rest
    else:
        g_ref, o_ref, s_sc, mc_sc, m_sc, acc_sc = rest
    q = q_ref[...]
    hw = q.shape[1]
    dh = hw // 2
    lane = lax.broadcasted_iota(jnp.int32, (1, hw), 1)
    zero = jnp.zeros_like(q)
    qs = (jnp.where(lane < dh, q, zero), jnp.where(lane >= dh, q, zero))
    m_sc[...] = jnp.full(m_sc.shape, -jnp.inf, F32)
    acc_sc[...] = jnp.zeros(acc_sc.shape, F32)
    ones_rows = (lax.broadcasted_iota(jnp.int32, (ONES_ROWS, tk), 0) == 0).astype(BF16)

    def scores(kk, slot):
        for c in range(2):
            s = lax.dot_general(kk, qs[c], (((1,), (1,)), ((), ())), preferred_element_type=F32)
            s_sc[slot, c] = s
            mc_sc[slot, c] = jnp.max(s, axis=0, keepdims=True)

    def consume(vt, slot):
        vta = jnp.concatenate([vt, ones_rows], axis=0)
        for c in range(2):
            m_prev = m_sc[c]
            m_new = jnp.maximum(m_prev, mc_sc[slot, c])
            alpha = jnp.exp2(m_prev - m_new)
            p = jnp.exp2(s_sc[slot, c] - m_new).astype(BF16)
            acc_sc[c] = alpha * acc_sc[c] + jnp.dot(vta, p, preferred_element_type=F32)
            m_sc[c] = m_new

    def k_at(i):
        if i < nk_ctx:
            return kc_ref[i * tk:(i + 1) * tk, :]
        return k_ref[(i - nk_ctx) * tk:(i - nk_ctx + 1) * tk, :]

    def vt_at(i):
        if i < nk_ctx:
            return vtc_ref[0, :, i * tk:(i + 1) * tk]
        return vt_ref[:, (i - nk_ctx) * tk:(i - nk_ctx + 1) * tk]

    n_total = nk_ctx + nk
    peel = nk_ctx + (n_total - 1 - nk_ctx) % 2
    scores(k_at(0), 0)
    for i in range(peel):
        scores(k_at(i + 1), (i + 1) % 2)
        consume(vt_at(i), i % 2)

    def body(t, carry):
        for r in range(2):
            j = 2 * t + (peel + r - nk_ctx)
            nxt = pl.multiple_of((j + 1) * tk, tk)
            cur = pl.multiple_of(j * tk, tk)
            scores(k_ref[pl.ds(nxt, tk), :], (peel + r + 1) % 2)
            consume(vt_ref[:, pl.ds(cur, tk)], (peel + r) % 2)
        return carry

    lax.fori_loop(0, (n_total - 1 - peel) // 2, body, 0)
    consume(vt_at(n_total - 1), (n_total - 1) % 2)
    o = (acc_sc[0, :hw, :] / acc_sc[0, hw:hw + 1, :]
         - lam_ref[0] * (acc_sc[1, :hw, :] / acc_sc[1, hw:hw + 1, :]))
    ms = jnp.mean(o * o, axis=0, keepdims=True)
    o = o * lax.rsqrt(ms + EPS) * g_ref[...] * post_scale
    o_ref[...] = o.T.astype(BF16)


def _attention(lam, q, k, vt, g_col, ctx, *, batch, t_len, row0, tq, tk, post_scale):
    hw = q.shape[1] // DIFF_HEADS
    nq = t_len // tq
    assert row0 % t_len == 0 and t_len % tk == 0
    qb0 = row0 // tq
    sb0 = row0 // t_len
    in_specs = [pl.BlockSpec(memory_space=pltpu.SMEM),
                pl.BlockSpec((tq, hw), lambda b, h, i: (qb0 + b * nq + i, h)),
                pl.BlockSpec((t_len, hw), lambda b, h, i: (sb0 + b, h)),
                pl.BlockSpec((hw, t_len), lambda b, h, i: (h, sb0 + b))]
    args = [lam, q, k, vt]
    nk_ctx = 0
    if ctx is not None:
        k_ctx, vt_ctx = ctx
        past = vt_ctx.shape[2]
        assert past % tk == 0
        nk_ctx = past // tk
        in_specs += [pl.BlockSpec((past, hw), lambda b, h, i: (b, h)),
                     pl.BlockSpec((1, hw, past), lambda b, h, i: (b, h, 0))]
        args += [k_ctx, vt_ctx]
    in_specs.append(pl.BlockSpec((hw, 1), lambda b, h, i: (0, 0)))
    args.append(g_col)
    return pl.pallas_call(
        functools.partial(_attn_kernel, nk=t_len // tk, tk=tk, nk_ctx=nk_ctx, post_scale=post_scale),
        grid=(batch, DIFF_HEADS, nq),
        in_specs=in_specs,
        out_specs=pl.BlockSpec((tq, hw), lambda b, h, i: (b * nq + i, h)),
        out_shape=jax.ShapeDtypeStruct((batch * t_len, q.shape[1]), BF16),
        scratch_shapes=[pltpu.VMEM((2, 2, tk, tq), F32), pltpu.VMEM((2, 2, 1, tq), F32),
                        pltpu.VMEM((2, 1, tq), F32), pltpu.VMEM((2, hw + ONES_ROWS, tq), F32)],
        compiler_params=_cparams(("parallel", "parallel", "parallel")),
        name="diff_attention",
    )(*args)


def _seq_edges(i, npt, tps):
    r = (i - npt) % tps
    is_prompt = i < npt
    first = jnp.logical_or(is_prompt, r == 0)
    last = jnp.logical_or(is_prompt, r == tps - 1)
    return first, last


def _fill_ext(ext_sc, cur_ref, prev_ref, next_ref, first, last):
    prev = prev_ref[...]
    nxt = next_ref[...]
    ext_sc[0:HALO, :] = jnp.where(first, jnp.zeros_like(prev), prev)
    ext_sc[HALO:HALO + TM, :] = cur_ref[...]
    ext_sc[HALO + TM:, :] = jnp.where(last, jnp.zeros_like(nxt), nxt)


def _moe_prologue(x_new, mod_ref, g2_ref, wr_ref, h2_ref, lg_ref):
    h2 = _rms_mod(x_new, g2_ref[...], mod_ref[0, 4:5, :], mod_ref[0, 3:4, :])
    h2_ref[...] = h2
    lg_ref[...] = jnp.dot(h2.astype(BF16), wr_ref[...], preferred_element_type=F32)


def _post_even_kernel(x_ref, u_ref, up_ref, un_ref, o_ref, mod_ref, cw_ref, cb_ref, lng_ref, lnb_ref,
                      wo_ref, g2_ref, wr_ref, x1_ref, h2_ref, lg_ref, ext_sc, *, npt, tps):
    i = pl.program_id(0)
    first, last = _seq_edges(i, npt, tps)
    _fill_ext(ext_sc, u_ref, up_ref, un_ref, first, last)
    cw = u_ref.shape[1]
    rows = 32
    chunks = []
    for rc in range(TM // rows):
        acc = None
        for k in range(CONV_K):
            start = rc * rows + HALO - CONV_K // 2 + k
            term = ext_sc[start:start + rows, :] * cw_ref[k:k + 1, :]
            acc = term if acc is None else acc + term
        chunks.append(acc)
    conv = jnp.concatenate(chunks, axis=0) + cb_ref[...]
    mu = jnp.mean(conv, axis=-1, keepdims=True)
    cen = conv - mu
    var = jnp.mean(cen * cen, axis=-1, keepdims=True)
    uu = _silu(cen * lax.rsqrt(var + EPS) * lng_ref[...] + lnb_ref[...])
    m = (jnp.dot(uu.astype(BF16), wo_ref[0:cw, :], preferred_element_type=F32)
         + jnp.dot(o_ref[...], wo_ref[cw:, :], preferred_element_type=F32))
    x1 = x_ref[...] + mod_ref[0, 2:3, :] * m
    x1_ref[...] = x1
    _moe_prologue(x1, mod_ref, g2_ref, wr_ref, h2_ref, lg_ref)


def _row_copy(src_hbm, src_row, dst, dst_row, sem):
    return pltpu.make_async_copy(src_hbm.at[pl.ds(src_row, 1)], dst.at[pl.ds(dst_row, 1)], sem)


def _gather_start(dest_ref, yb_hbm, ybuf, sem, slot):
    def body(n, carry):
        for k in range(TOP_K):
            _row_copy(yb_hbm, dest_ref[0, 0, k * TM + n], ybuf.at[slot, k], n, sem.at[slot]).start()
        return carry

    lax.fori_loop(0, TM, body, 0, unroll=8)


def _moe_combine(dcur_ref, dnxt_ref, cw_ref, yb_hbm, ybuf, sem):
    t = pl.program_id(0)
    slot = t % 2

    @pl.when(t == 0)
    def _():
        _gather_start(dcur_ref, yb_hbm, ybuf, sem, 0)

    @pl.when(t + 1 < pl.num_programs(0))
    def _():
        _gather_start(dnxt_ref, yb_hbm, ybuf, sem, 1 - slot)

    for k in range(TOP_K):
        pltpu.make_async_copy(yb_hbm.at[pl.ds(0, TM)], ybuf.at[slot, k], sem.at[slot]).wait()
    w = cw_ref[...]
    y = w[:, 0:1] * ybuf[slot, 0]
    for k in range(1, TOP_K):
        y = y + w[:, k:k + 1] * ybuf[slot, k]
    return y


def _dispatch_kernel(dest_ref, h2_hbm, xb0_hbm, xb_hbm, sem):
    del xb0_hbm
    t = pl.program_id(0)

    def body(n, carry):
        for k in range(TOP_K):
            _row_copy(h2_hbm, t * TM + n, xb_hbm, dest_ref[0, 0, k * TM + n], sem).start()
        return carry

    lax.fori_loop(0, TM, body, 0, unroll=8)

    def drain_one_tile():
        pltpu.make_async_copy(h2_hbm.at[pl.ds(0, TOP_K * TM)], xb_hbm.at[pl.ds(0, TOP_K * TM)], sem).wait()

    @pl.when(t > 0)
    def _():
        drain_one_tile()

    @pl.when(t == pl.num_programs(0) - 1)
    def _():
        drain_one_tile()


def _dispatch(dest3, h2, p_len):
    n, d = h2.shape
    return pl.pallas_call(
        _dispatch_kernel,
        grid=(n // TM,),
        in_specs=[pl.BlockSpec((1, 1, TOP_K * TM), lambda t: (t, 0, 0), memory_space=pltpu.SMEM),
                  pl.BlockSpec(memory_space=pl.ANY), pl.BlockSpec(memory_space=pl.ANY)],
        out_specs=pl.BlockSpec(memory_space=pl.ANY),
        out_shape=jax.ShapeDtypeStruct((p_len, d), h2.dtype),
        scratch_shapes=[pltpu.SemaphoreType.DMA(())],
        input_output_aliases={2: 0},
        compiler_params=_cparams(("arbitrary",)),
        name="moe_dispatch",
    )(dest3, h2, jnp.zeros((p_len, d), h2.dtype))


def _pre_odd_kernel(x_ref, dcur_ref, dnxt_ref, cw_ref, yb_hbm, modp_ref, mod_ref, g_ref, w_ref,
                    x2_ref, xc_ref, p_ref, bg_ref, ybuf, sem, *, cw):
    y = _moe_combine(dcur_ref, dnxt_ref, cw_ref, yb_hbm, ybuf, sem)
    x2 = x_ref[...] + modp_ref[0, 5:6, :] * y
    x2_ref[...] = x2
    h = _rms_mod(x2, g_ref[...], mod_ref[0, 1:2, :], mod_ref[0, 0:1, :])
    proj = jnp.dot(h.astype(BF16), w_ref[...], preferred_element_type=F32)
    xc_ref[...] = proj[:, :cw]
    hd = proj[:, cw:2 * cw]
    bg_ref[...] = proj[:, 2 * cw:3 * cw]
    p_ref[...] = proj[:, 3 * cw:] * hd


def _post_odd_kernel(x_ref, xc_ref, xcp_ref, xcn_ref, p_ref, pp_ref, pn_ref, bg_ref, mod_ref,
                     pw_ref, ps_ref, dw_ref, wo_ref, g2_ref, wr_ref,
                     x3_ref, h2_ref, lg_ref, extc_sc, extp_sc, *, npt, tps):
    i = pl.program_id(0)
    first, last = _seq_edges(i, npt, tps)
    _fill_ext(extc_sc, xc_ref, xcp_ref, xcn_ref, first, last)
    _fill_ext(extp_sc, p_ref, pp_ref, pn_ref, first, last)
    cw = xc_ref.shape[1]
    gc = cw // len(POOL_WINDOWS)
    is_prompt = i < npt
    t_len = jnp.where(is_prompt, TM, TM * tps)
    pos = lax.broadcasted_iota(jnp.int32, (TM, 1), 0) + jnp.where(is_prompt, 0, ((i - npt) % tps) * TM)
    ds = []
    for g, w in enumerate(POOL_WINDOWS):
        sl = slice(g * gc, (g + 1) * gc)
        acc = None
        for o in range(-(w // 2), w // 2):
            term = extc_sc[HALO + o:HALO + o + TM, sl]
            acc = term if acc is None else acc + term
        lo = jnp.maximum(pos - w // 2, 0)
        hi = jnp.minimum(pos + w // 2 - 1, t_len - 1)
        cnt = (hi - lo + 1).astype(F32)
        ds.append(acc / cnt - extc_sc[HALO:HALO + TM, sl])
    d = jnp.concatenate(ds, axis=1).astype(BF16)
    yc = jnp.dot(d, pw_ref[...], preferred_element_type=F32) * ps_ref[...]
    conv = None
    for k in range(SCONV_K):
        start = HALO - SCONV_K // 2 + k
        term = extp_sc[start:start + TM, :] * dw_ref[k:k + 1, :]
        conv = term if conv is None else conv + term
    yd = bg_ref[...] * conv
    m = (jnp.dot(yc.astype(BF16), wo_ref[0:cw, :], preferred_element_type=F32)
         + jnp.dot(yd.astype(BF16), wo_ref[cw:, :], preferred_element_type=F32))
    x3 = x_ref[...] + mod_ref[0, 2:3, :] * m
    x3_ref[...] = x3
    _moe_prologue(x3, mod_ref, g2_ref, wr_ref, h2_ref, lg_ref)


def _final_kernel(x_ref, dcur_ref, dnxt_ref, cw_ref, yb_hbm, modp_ref, g_ref, op_ref, os_ref, ybuf, sem, *, npt):
    i = pl.program_id(0)
    y = _moe_combine(dcur_ref, dnxt_ref, cw_ref, yb_hbm, ybuf, sem)
    x = x_ref[...] + modp_ref[0, 5:6, :] * y
    ms = jnp.mean(x * x, axis=-1, keepdims=True)
    out = x * lax.rsqrt(ms + EPS) * g_ref[...]

    @pl.when(i < npt)
    def _():
        op_ref[...] = out

    @pl.when(i >= npt)
    def _():
        os_ref[...] = out


def _expert_kernel(be_ref, nv_ref, x_ref, wg_ref, wu_ref, wd_ref, y_ref, wg_sc, wu_sc, wd_sc):
    i = pl.program_id(0)

    @pl.when(jnp.logical_or(i == 0, be_ref[i] != be_ref[jnp.maximum(i - 1, 0)]))
    def _():
        wg_sc[...] = wg_ref[0, 0].astype(BF16)
        wu_sc[...] = wu_ref[0, 0].astype(BF16)
        wd_sc[...] = wd_ref[0, 0].astype(BF16)

    @pl.when(i < nv_ref[0])
    def _():
        x = x_ref[...].astype(BF16)
        hg = jnp.dot(x, wg_sc[...], preferred_element_type=F32)
        hu = jnp.dot(x, wu_sc[...], preferred_element_type=F32)
        hid = (_silu(hg) * hu).astype(BF16)
        y_ref[...] = jnp.dot(hid, wd_sc[...], preferred_element_type=F32)

    @pl.when(i >= nv_ref[0])
    def _():
        y_ref[...] = jnp.zeros(y_ref.shape, F32)


def _experts(blk_e, nvalid, xb, layer, wg, wu, wd):
    p_len, d = xb.shape
    hid = wg.shape[3]
    nblk = p_len // MOE_BLK
    grid_spec = pltpu.PrefetchScalarGridSpec(
        num_scalar_prefetch=2,
        grid=(nblk,),
        in_specs=[pl.BlockSpec((MOE_BLK, d), lambda i, be, nv: (i, 0)),
                  pl.BlockSpec((1, 1, d, hid), lambda i, be, nv: (layer, be[i], 0, 0)),
                  pl.BlockSpec((1, 1, d, hid), lambda i, be, nv: (layer, be[i], 0, 0)),
                  pl.BlockSpec((1, 1, hid, d), lambda i, be, nv: (layer, be[i], 0, 0))],
        out_specs=pl.BlockSpec((MOE_BLK, d), lambda i, be, nv: (i, 0)),
        scratch_shapes=[pltpu.VMEM((d, hid), BF16), pltpu.VMEM((d, hid), BF16), pltpu.VMEM((hid, d), BF16)],
    )
    return pl.pallas_call(
        _expert_kernel,
        grid_spec=grid_spec,
        out_shape=jax.ShapeDtypeStruct((p_len, d), F32),
        compiler_params=_cparams(("arbitrary",)),
        name="moe_experts",
    )(blk_e, nvalid, xb, wg, wu, wd)


def _route_kernel(lg_ref, b_ref, info_ref, cnt_ref):
    lt = lg_ref[...].T + b_ref[...]
    gl = lt[N_EXPERTS:N_EXPERTS + N_GROUPS]
    ge = jnp.exp(gl - jnp.max(gl, axis=0, keepdims=True))
    gp = ge / jnp.sum(ge, axis=0, keepdims=True)
    best = gp[0:1]
    gidx = jnp.zeros(best.shape, jnp.int32)
    el = lt[0:EXPERTS_PER_GROUP]
    for g in range(1, N_GROUPS):
        better = gp[g:g + 1] > best
        gidx = jnp.where(better, g, gidx)
        best = jnp.where(better, gp[g:g + 1], best)
    for g in range(1, N_GROUPS):
        el = jnp.where(gidx == g, lt[g * EXPERTS_PER_GROUP:(g + 1) * EXPERTS_PER_GROUP], el)
    rio = lax.broadcasted_iota(jnp.int32, el.shape, 0)
    v0 = jnp.max(el, axis=0, keepdims=True)
    i0 = jnp.min(jnp.where(el == v0, rio, EXPERTS_PER_GROUP), axis=0, keepdims=True)
    el2 = jnp.where(rio == i0, -jnp.inf, el)
    v1 = jnp.max(el2, axis=0, keepdims=True)
    i1 = jnp.min(jnp.where(el2 == v1, rio, EXPERTS_PER_GROUP), axis=0, keepdims=True)
    e1 = jnp.exp(v1 - v0)
    den = 1.0 + e1
    w0 = (1.0 / den) * best
    w1 = (e1 / den) * best
    eid0 = gidx * EXPERTS_PER_GROUP + i0
    eid1 = gidx * EXPERTS_PER_GROUP + i1
    tm = lt.shape[1]
    eio = lax.broadcasted_iota(jnp.int32, (N_EXPERTS, tm), 0)
    oh0 = eio == eid0
    oh1 = eio == eid1
    ohs = jnp.where(oh0, 1.0, 0.0) + jnp.where(oh1, 1.0, 0.0)
    upper = (lax.broadcasted_iota(jnp.int32, (tm, tm), 0)
             < lax.broadcasted_iota(jnp.int32, (tm, tm), 1)).astype(BF16)
    before = jnp.dot(ohs.astype(BF16), upper, preferred_element_type=F32)
    rank0 = jnp.sum(jnp.where(oh0, before, 0.0), axis=0, keepdims=True)
    rank1 = jnp.sum(jnp.where(oh1, before, 0.0), axis=0, keepdims=True)
    info_ref[0] = jnp.concatenate([eid0.astype(F32), eid1.astype(F32), w0, w1, rank0, rank1,
                                   jnp.zeros((2, tm), F32)], axis=0)
    cnt_ref[0] = jnp.sum(ohs, axis=1, keepdims=True)


def _route(logits, bias_col):
    n = logits.shape[0]
    nt = n // TM
    return pl.pallas_call(
        _route_kernel,
        grid=(nt,),
        in_specs=[pl.BlockSpec((TM, ROUTER_PAD), lambda i: (i, 0)),
                  pl.BlockSpec((ROUTER_PAD, 1), lambda i: (0, 0))],
        out_specs=[pl.BlockSpec((1, 8, TM), lambda i: (i, 0, 0)),
                   pl.BlockSpec((1, N_EXPERTS, 1), lambda i: (i, 0, 0))],
        out_shape=[jax.ShapeDtypeStruct((nt, 8, TM), F32), jax.ShapeDtypeStruct((nt, N_EXPERTS, 1), F32)],
        compiler_params=_cparams(("parallel",)),
        name="moe_route",
    )(logits, bias_col)


def _moe(h2, logits, bias_col, layer, wg, wu, wd):
    n, d = h2.shape
    info, cnt = _route(logits, bias_col)
    cnt = cnt[:, :, 0].astype(jnp.int32)
    counts = jnp.sum(cnt, axis=0)
    pc = ((counts + MOE_BLK - 1) // MOE_BLK) * MOE_BLK
    pend = jnp.cumsum(pc)
    base = (pend - pc)[None, :] + jnp.cumsum(cnt, axis=0) - cnt
    eid = info[:, 0:2, :].astype(jnp.int32)
    rank = info[:, 4:6, :].astype(jnp.int32)
    sel = eid[..., None] == jnp.arange(N_EXPERTS, dtype=jnp.int32)
    dest = jnp.sum(jnp.where(sel, base[:, None, None, :], 0), axis=-1) + rank
    p_len = n * TOP_K + N_EXPERTS * MOE_BLK
    nblk = p_len // MOE_BLK
    blk_start = jnp.arange(nblk, dtype=jnp.int32) * MOE_BLK
    blk_e = jnp.minimum(jnp.sum(pend[None, :] <= blk_start[:, None], axis=1), N_EXPERTS - 1).astype(jnp.int32)
    nvalid = (pend[-1:] // MOE_BLK).astype(jnp.int32)
    dest3 = dest.reshape(dest.shape[0], 1, TOP_K * TM)
    wcol = jnp.transpose(info[:, 2:2 + TOP_K, :], (0, 2, 1)).reshape(n, TOP_K)
    xb = _dispatch(dest3, h2, p_len)
    yb = _experts(blk_e, nvalid, xb, layer, wg, wu, wd)
    return dest3, wcol, yb


def _rope_tables(t_len, qw):
    rows = t_len // GRID_W
    ax = qw // (DIFF_HEADS * 2) // 2
    row = jnp.repeat(jnp.arange(rows, dtype=F32), GRID_W)
    col = jnp.tile(jnp.arange(GRID_W, dtype=F32), rows)
    inv = ROPE_BASE ** (-jnp.arange(0, ax, 2, dtype=F32) / ax)
    ar = row[:, None] * inv[None, :]
    ac = col[:, None] * inv[None, :]
    cr, sr, cc, sc = jnp.cos(ar), jnp.sin(ar), jnp.cos(ac), jnp.sin(ac)
    cos = jnp.concatenate([cr, cr, cc, cc], axis=1)
    sins = jnp.concatenate([-sr, sr, -sc, sc], axis=1)
    reps = qw // cos.shape[1]
    cos = jnp.concatenate([jnp.ones((TM, cos.shape[1]), F32), cos], axis=0)
    sins = jnp.concatenate([jnp.zeros((TM, sins.shape[1]), F32), sins], axis=0)
    return jnp.tile(cos, (1, reps)), jnp.tile(sins, (1, reps))


def kernel(x_prompt, x_sample, cache_k, cache_v, c, c_ctx, w_ada, b_ada, g_norm1, g_norm2, g_final, w_in_e, conv_a_w, conv_a_b, ln_a_g, ln_a_b, lam_q1, lam_k1, lam_q2, lam_k2, subln_g, w_out_e, w_in_o, pool_w, pool_scale, conv_d_w, w_out_o, w_router_g, b_router_g, w_router_e, b_router_e, w_gate, w_up, w_down):
    bp, tp, d = x_prompt.shape
    bs, ts, _ = x_sample.shape
    past = cache_k.shape[2]
    assert tp == TM and ts % TM == 0 and d % LANES == 0
    npt = bp
    tps = ts // TM
    n_p, n_s = bp * tp, bs * ts
    n = n_p + n_s
    nt = n // TM
    cw = conv_a_w.shape[2]
    qw = (w_in_e.shape[2] - 2 * cw) // 3
    dh = qw // (DIFF_HEADS * 2)
    hpt = TM // HALO

    def cond_of(i):
        return jnp.where(i < npt, 0, 1 + (i - npt) // tps)

    def tile(w):
        return pl.BlockSpec((TM, w), lambda i: (i, 0))

    def halo_prev(w):
        return pl.BlockSpec((HALO, w), lambda i: (jnp.maximum(i * hpt - 1, 0), 0))

    def halo_next(w):
        return pl.BlockSpec((HALO, w), lambda i: (jnp.minimum((i + 1) * hpt, nt * hpt - 1), 0))

    def whole(shape):
        return pl.BlockSpec(shape, lambda i: (0,) * len(shape))

    mod_spec = pl.BlockSpec((1, 6, d), lambda i: (cond_of(i), 0, 0))
    combine_specs = [pl.BlockSpec((1, 1, TOP_K * TM), lambda i: (i, 0, 0), memory_space=pltpu.SMEM),
                     pl.BlockSpec((1, 1, TOP_K * TM), lambda i: (jnp.minimum(i + 1, nt - 1), 0, 0),
                                  memory_space=pltpu.SMEM),
                     tile(TOP_K), pl.BlockSpec(memory_space=pl.ANY)]
    combine_scratch = [pltpu.VMEM((2, TOP_K, TM, d), F32), pltpu.SemaphoreType.DMA((2,))]

    rows = 8 * ((1 + bs + 7) // 8)
    cond = jnp.concatenate([c_ctx[None, :], c, jnp.zeros((rows - 1 - bs, d), F32)], axis=0)
    mod = _ada_table(cond, w_ada, b_ada)

    x = jnp.concatenate([x_prompt.reshape(n_p, d), x_sample.reshape(n_s, d)], axis=0)

    def router_w(i):
        wre = jnp.transpose(w_router_e[i], (1, 0, 2)).reshape(d, N_EXPERTS)
        wr = jnp.concatenate([wre, w_router_g[i]], axis=1)
        return jnp.pad(wr, ((0, 0), (0, ROUTER_PAD - wr.shape[1]))).astype(BF16)

    def router_b(i):
        b = jnp.concatenate([b_router_e[i].reshape(N_EXPERTS), b_router_g[i]])
        return jnp.pad(b, (0, ROUTER_PAD - b.shape[0]))[:, None]

    cos, sins = _rope_tables(ts, qw)
    tab_spec = pl.BlockSpec((TM, qw), lambda i: (jnp.where(i < npt, 0, 1 + (i - npt) % tps), 0))
    cache_spec = pl.BlockSpec((TM, qw), lambda i: (jnp.minimum(i, npt - 1), 0))
    u, q, k, vt, kc, vc = pl.pallas_call(
        functools.partial(_pre_even_kernel, npt=npt, cw=cw, qw=qw, qscale=dh ** -0.5 * math.log2(math.e)),
        grid=(nt,),
        in_specs=[tile(d), mod_spec, whole((1, d)), whole(w_in_e.shape[1:]), tab_spec, tab_spec],
        out_specs=[tile(cw), tile(qw), tile(qw), pl.BlockSpec((qw, TM), lambda i: (0, i)),
                   cache_spec, cache_spec],
        out_shape=[jax.ShapeDtypeStruct((n, cw), F32), jax.ShapeDtypeStruct((n, qw), BF16),
                   jax.ShapeDtypeStruct((n, qw), BF16), jax.ShapeDtypeStruct((qw, n), BF16),
                   jax.ShapeDtypeStruct((n_p, qw), F32), jax.ShapeDtypeStruct((n_p, qw), F32)],
        compiler_params=_cparams(("arbitrary",)),
        name="pre_even",
    )(x, mod[0], g_norm1[0][None, :], w_in_e[0].astype(BF16), cos, sins)

    lam_init = 0.8 - 0.6 * math.exp(-0.3 * 0)
    lam = (jnp.exp(jnp.sum(lam_q1[0] * lam_k1[0])) - jnp.exp(jnp.sum(lam_q2[0] * lam_k2[0])) + lam_init)
    lam = lam.reshape(1).astype(F32)
    sub_g = subln_g[0][:, None]
    o_p = _attention(lam, q, k, vt, sub_g, None, batch=bp, t_len=tp, row0=0, tq=TM, tk=TM,
                     post_scale=1.0 - lam_init)
    k_ctx = cache_k[:, 0].reshape(bs * past, qw).astype(BF16)
    vt_ctx = jnp.transpose(cache_v[:, 0].reshape(bs, past, qw), (0, 2, 1)).astype(BF16)
    o_s = _attention(lam, q, k, vt, sub_g, (k_ctx, vt_ctx), batch=bs, t_len=ts, row0=n_p, tq=512, tk=512,
                     post_scale=1.0 - lam_init)
    o = jnp.concatenate([o_p, o_s], axis=0)

    moe_outs = [jax.ShapeDtypeStruct((n, d), F32), jax.ShapeDtypeStruct((n, d), F32),
                jax.ShapeDtypeStruct((n, ROUTER_PAD), F32)]
    moe_out_specs = [tile(d), tile(d), tile(ROUTER_PAD)]
    x1, h2, logits = pl.pallas_call(
        functools.partial(_post_even_kernel, npt=npt, tps=tps),
        grid=(nt,),
        in_specs=[tile(d), tile(cw), halo_prev(cw), halo_next(cw), tile(qw), mod_spec,
                  whole((CONV_K, cw)), whole((1, cw)), whole((1, cw)), whole((1, cw)),
                  whole(w_out_e.shape[1:]), whole((1, d)), whole((d, ROUTER_PAD))],
        out_specs=moe_out_specs,
        out_shape=moe_outs,
        scratch_shapes=[pltpu.VMEM((TM + 2 * HALO, cw), F32)],
        compiler_params=_cparams(("parallel",)),
        name="post_even",
    )(x, u, u, u, o, mod[0], conv_a_w[0], conv_a_b[0][None, :], ln_a_g[0][None, :], ln_a_b[0][None, :],
      w_out_e[0].astype(BF16), g_norm2[0][None, :], router_w(0))
    dest3, wcol, yb = _moe(h2, logits, router_b(0), 0, w_gate, w_up, w_down)

    pw = pool_w.shape[2]
    x2, xc, p, bg = pl.pallas_call(
        functools.partial(_pre_odd_kernel, cw=cw),
        grid=(nt,),
        in_specs=[tile(d)] + combine_specs + [mod_spec, mod_spec, whole((1, d)), whole(w_in_o.shape[1:])],
        out_specs=[tile(d), tile(cw), tile(cw), tile(cw)],
        out_shape=[jax.ShapeDtypeStruct((n, d), F32)] + [jax.ShapeDtypeStruct((n, cw), F32)] * 3,
        scratch_shapes=combine_scratch,
        compiler_params=_cparams(("arbitrary",)),
        name="pre_odd",
    )(x1, dest3, dest3, wcol, yb, mod[0], mod[1], g_norm1[1][None, :], w_in_o[0].astype(BF16))
    pool_bd = jnp.zeros((cw, cw), F32)
    for g in range(len(POOL_WINDOWS)):
        pool_bd = pool_bd.at[g * pw:(g + 1) * pw, g * pw:(g + 1) * pw].set(pool_w[0, g])
    x3, h2, logits = pl.pallas_call(
        functools.partial(_post_odd_kernel, npt=npt, tps=tps),
        grid=(nt,),
        in_specs=[tile(d), tile(cw), halo_prev(cw), halo_next(cw), tile(cw), halo_prev(cw), halo_next(cw),
                  tile(cw), mod_spec, whole((cw, cw)), whole((1, cw)), whole((SCONV_K, cw)),
                  whole(w_out_o.shape[1:]), whole((1, d)), whole((d, ROUTER_PAD))],
        out_specs=moe_out_specs,
        out_shape=moe_outs,
        scratch_shapes=[pltpu.VMEM((TM + 2 * HALO, cw), F32), pltpu.VMEM((TM + 2 * HALO, cw), F32)],
        compiler_params=_cparams(("parallel",)),
        name="post_odd",
    )(x2, xc, xc, xc, p, p, p, bg, mod[1], pool_bd.astype(BF16), pool_scale[0][None, :], conv_d_w[0],
      w_out_o[0].astype(BF16), g_norm2[1][None, :], router_w(1))
    dest3, wcol, yb = _moe(h2, logits, router_b(1), 1, w_gate, w_up, w_down)

    out_p, out_s = pl.pallas_call(
        functools.partial(_final_kernel, npt=npt),
        grid=(nt,),
        in_specs=[tile(d)] + combine_specs + [mod_spec, whole((1, d))],
        out_specs=[pl.BlockSpec((TM, d), lambda i: (jnp.minimum(i, npt - 1), 0)),
                   pl.BlockSpec((TM, d), lambda i: (jnp.maximum(i - npt, 0), 0))],
        out_shape=[jax.ShapeDtypeStruct((n_p, d), F32), jax.ShapeDtypeStruct((n_s, d), F32)],
        scratch_shapes=combine_scratch,
        compiler_params=_cparams(("arbitrary",)),
        name="final_norm",
    )(x3, dest3, dest3, wcol, yb, mod[1], g_final[None, :])

    new_k = kc.reshape(bp, 1, tp, DIFF_HEADS, 2, dh)
    new_v = vc.reshape(bp, 1, tp, DIFF_HEADS, 2 * dh)
    return (out_p.reshape(bp, tp, d), out_s.reshape(bs, ts, d), new_k, new_v)
```

```python
import functools
import math

import jax
import jax.numpy as jnp
from jax import lax
from jax.experimental import pallas as pl
from jax.experimental.pallas import tpu as pltpu

F32 = jnp.float32
BF16 = jnp.bfloat16

EPS = 1e-6
GRID_W = 64
ROPE_BASE = 10000.0
CONV_K = 31
SCONV_K = 3
POOL_WINDOWS = (2, 4, 8, 16)
DIFF_HEADS = 4
N_GROUPS = 4
EXPERTS_PER_GROUP = 8
N_EXPERTS = N_GROUPS * EXPERTS_PER_GROUP
TOP_K = 2

LANES = 128
TM = 256
HALO = 16
MOE_BLK = 256
ROUTER_PAD = 128
ONES_ROWS = 16
DISPATCH_SLOTS = 3
VMEM_LIMIT = 56 * 1024 * 1024


def _cparams(sem):
    return pltpu.CompilerParams(dimension_semantics=sem, vmem_limit_bytes=VMEM_LIMIT)


def _rms_mod(x, g, sc, sh):
    ms = jnp.mean(x * x, axis=-1, keepdims=True)
    return (x * lax.rsqrt(ms + EPS)) * g * (1.0 + sc) + sh


def _silu(x):
    return x * jax.nn.sigmoid(x)


def _ada_kernel(c_ref, w_ref, b_ref, o_ref):
    cs = _silu(c_ref[...])
    o_ref[0] = jnp.dot(cs, w_ref[0], precision=lax.Precision.HIGHEST,
                       preferred_element_type=F32) + b_ref[0]


def _ada_table(cond, w_ada, b_ada):
    depth, d, six_d = w_ada.shape
    rows = cond.shape[0]
    nj = six_d // d
    out = pl.pallas_call(
        _ada_kernel,
        grid=(depth, nj),
        in_specs=[pl.BlockSpec((rows, d), lambda l, j: (0, 0)),
                  pl.BlockSpec((1, d, d), lambda l, j: (l, 0, j)),
                  pl.BlockSpec((1, 1, d), lambda l, j: (l, 0, j))],
        out_specs=pl.BlockSpec((1, rows, d), lambda l, j: (l, 0, j)),
        out_shape=jax.ShapeDtypeStruct((depth, rows, six_d), F32),
        compiler_params=_cparams(("parallel", "parallel")),
        name="ada_table",
    )(cond, w_ada, b_ada.reshape(depth, 1, six_d))
    return out.reshape(depth, rows, nj, d)


def _rope(x, cos, sins, first_half):
    outs = []
    for j in range(x.shape[1] // LANES):
        sl = slice(j * LANES, (j + 1) * LANES)
        xs = x[:, sl]
        nxt = pltpu.roll(xs, LANES - 16, 1)
        prv = pltpu.roll(xs, 16, 1)
        rot = jnp.where(first_half, nxt, prv)
        outs.append(xs * cos[:, sl] + rot * sins[:, sl])
    return jnp.concatenate(outs, axis=1)


def _pre_even_kernel(x_ref, mod_ref, g_ref, w_ref, cos_ref, sin_ref,
                     u_ref, q_ref, k_ref, vt_ref, kc_ref, vc_ref, *, npt, cw, qw, qscale):
    i = pl.program_id(0)
    h = _rms_mod(x_ref[...], g_ref[...], mod_ref[0, 1:2, :], mod_ref[0, 0:1, :])
    proj = jnp.dot(h.astype(BF16), w_ref[...], preferred_element_type=F32)
    ga = proj[:, :cw]
    gb = proj[:, cw:2 * cw]
    u_ref[...] = ga * jax.nn.sigmoid(gb)
    q = proj[:, 2 * cw:2 * cw + qw]
    k = proj[:, 2 * cw + qw:2 * cw + 2 * qw]
    v = proj[:, 2 * cw + 2 * qw:]

    @pl.when(i < npt)
    def _():
        kc_ref[...] = k
        vc_ref[...] = v

    lane = lax.broadcasted_iota(jnp.int32, (1, LANES), 1)
    first_half = (lane % 32) < 16
    cos = cos_ref[...]
    sins = sin_ref[...]
    q_ref[...] = (_rope(q, cos, sins, first_half) * qscale).astype(BF16)
    k_ref[...] = _rope(k, cos, sins, first_half).astype(BF16)
    vt_ref[...] = v.T.astype(BF16)


def _attn_kernel(lam_ref, q_ref, k_ref, vt_ref, *rest, nk, tk, nk_ctx, post_scale):
    if nk_ctx:
        kc_ref, vtc_ref, g_ref, o_ref, s_sc, mc_sc, m_sc, acc_sc = rest
    else:
        g_ref, o_ref, s_sc, mc_sc, m_sc, acc_sc = rest
    q = q_ref[...]
    hw = q.shape[1]
    dh = hw // 2
    lane = lax.broadcasted_iota(jnp.int32, (1, hw), 1)
    zero = jnp.zeros_like(q)
    qs = (jnp.where(lane < dh, q, zero), jnp.where(lane >= dh, q, zero))
    m_sc[...] = jnp.full(m_sc.shape, -jnp.inf, F32)
    acc_sc[...] = jnp.zeros(acc_sc.shape, F32)
    ones_rows = (lax.broadcasted_iota(jnp.int32, (ONES_ROWS, tk), 0) == 0).astype(BF16)

    def scores(kk, slot):
        for c in range(2):
            s = lax.dot_general(kk, qs[c], (((1,), (1,)), ((), ())), preferred_element_type=F32)
            s_sc[slot, c] = s
            mc_sc[slot, c] = jnp.max(s, axis=0, keepdims=True)

    def consume(vt, slot):
        vta = jnp.concatenate([vt, ones_rows], axis=0)
        for c in range(2):
            m_prev = m_sc[c]
            m_new = jnp.maximum(m_prev, mc_sc[slot, c])
            alpha = jnp.exp2(m_prev - m_new)
            p = jnp.exp2(s_sc[slot, c] - m_new).astype(BF16)
            acc_sc[c] = alpha * acc_sc[c] + jnp.dot(vta, p, preferred_element_type=F32)
            m_sc[c] = m_new

    def k_at(i):
        if i < nk_ctx:
            return kc_ref[i * tk:(i + 1) * tk, :]
        return k_ref[(i - nk_ctx) * tk:(i - nk_ctx + 1) * tk, :]

    def vt_at(i):
        if i < nk_ctx:
            return vtc_ref[0, :, i * tk:(i + 1) * tk]
        return vt_ref[:, (i - nk_ctx) * tk:(i - nk_ctx + 1) * tk]

    n_total = nk_ctx + nk
    peel = nk_ctx + (n_total - 1 - nk_ctx) % 2
    scores(k_at(0), 0)
    for i in range(peel):
        scores(k_at(i + 1), (i + 1) % 2)
        consume(vt_at(i), i % 2)

    def body(t, carry):
        for r in range(2):
            j = 2 * t + (peel + r - nk_ctx)
            nxt = pl.multiple_of((j + 1) * tk, tk)
            cur = pl.multiple_of(j * tk, tk)
            scores(k_ref[pl.ds(nxt, tk), :], (peel + r + 1) % 2)
            consume(vt_ref[:, pl.ds(cur, tk)], (peel + r) % 2)
        return carry

    lax.fori_loop(0, (n_total - 1 - peel) // 2, body, 0)
    consume(vt_at(n_total - 1), (n_total - 1) % 2)
    o = (acc_sc[0, :hw, :] / acc_sc[0, hw:hw + 1, :]
         - lam_ref[0] * (acc_sc[1, :hw, :] / acc_sc[1, hw:hw + 1, :]))
    ms = jnp.mean(o * o, axis=0, keepdims=True)
    o = o * lax.rsqrt(ms + EPS) * g_ref[...] * post_scale
    o_ref[...] = o.T.astype(BF16)


def _attention(lam, q, k, vt, g_col, ctx, *, batch, t_len, row0, tq, tk, post_scale):
    hw = q.shape[1] // DIFF_HEADS
    nq = t_len // tq
    assert row0 % t_len == 0 and t_len % tk == 0
    qb0 = row0 // tq
    sb0 = row0 // t_len
    in_specs = [pl.BlockSpec(memory_space=pltpu.SMEM),
                pl.BlockSpec((tq, hw), lambda b, h, i: (qb0 + b * nq + i, h)),
                pl.BlockSpec((t_len, hw), lambda b, h, i: (sb0 + b, h)),
                pl.BlockSpec((hw, t_len), lambda b, h, i: (h, sb0 + b))]
    args = [lam, q, k, vt]
    nk_ctx = 0
    if ctx is not None:
        k_ctx, vt_ctx = ctx
        past = vt_ctx.shape[2]
        assert past % tk == 0
        nk_ctx = past // tk
        in_specs += [pl.BlockSpec((past, hw), lambda b, h, i: (b, h)),
                     pl.BlockSpec((1, hw, past), lambda b, h, i: (b, h, 0))]
        args += [k_ctx, vt_ctx]
    in_specs.append(pl.BlockSpec((hw, 1), lambda b, h, i: (0, 0)))
    args.append(g_col)
    return pl.pallas_call(
        functools.partial(_attn_kernel, nk=t_len // tk, tk=tk, nk_ctx=nk_ctx, post_scale=post_scale),
        grid=(batch, DIFF_HEADS, nq),
        in_specs=in_specs,
        out_specs=pl.BlockSpec((tq, hw), lambda b, h, i: (b * nq + i, h)),
        out_shape=jax.ShapeDtypeStruct((batch * t_len, q.shape[1]), BF16),
        scratch_shapes=[pltpu.VMEM((2, 2, tk, tq), F32), pltpu.VMEM((2, 2, 1, tq), F32),
                        pltpu.VMEM((2, 1, tq), F32), pltpu.VMEM((2, hw + ONES_ROWS, tq), F32)],
        compiler_params=_cparams(("parallel", "parallel", "parallel")),
        name="diff_attention",
    )(*args)


def _seq_edges(i, npt, tps):
    r = (i - npt) % tps
    is_prompt = i < npt
    first = jnp.logical_or(is_prompt, r == 0)
    last = jnp.logical_or(is_prompt, r == tps - 1)
    return first, last


def _fill_ext(ext_sc, cur_ref, prev_ref, next_ref, first, last):
    prev = prev_ref[...]
    nxt = next_ref[...]
    ext_sc[0:HALO, :] = jnp.where(first, jnp.zeros_like(prev), prev)
    ext_sc[HALO:HALO + TM, :] = cur_ref[...]
    ext_sc[HALO + TM:, :] = jnp.where(last, jnp.zeros_like(nxt), nxt)


def _moe_prologue(x_new, mod_ref, g2_ref, wr_ref, h2_ref, lg_ref):
    h2 = _rms_mod(x_new, g2_ref[...], mod_ref[0, 4:5, :], mod_ref[0, 3:4, :])
    h2_ref[...] = h2
    lg_ref[...] = jnp.dot(h2.astype(BF16), wr_ref[...], preferred_element_type=F32)


def _post_even_kernel(x_ref, u_ref, up_ref, un_ref, o_ref, mod_ref, cw_ref, cb_ref, lng_ref, lnb_ref,
                      wo_ref, g2_ref, wr_ref, x1_ref, h2_ref, lg_ref, ext_sc, *, npt, tps):
    i = pl.program_id(0)
    first, last = _seq_edges(i, npt, tps)
    _fill_ext(ext_sc, u_ref, up_ref, un_ref, first, last)
    cw = u_ref.shape[1]
    rows = 32
    chunks = []
    for rc in range(TM // rows):
        acc = None
        for k in range(CONV_K):
            start = rc * rows + HALO - CONV_K // 2 + k
            term = ext_sc[start:start + rows, :] * cw_ref[k:k + 1, :]
            acc = term if acc is None else acc + term
        chunks.append(acc)
    conv = jnp.concatenate(chunks, axis=0) + cb_ref[...]
    mu = jnp.mean(conv, axis=-1, keepdims=True)
    cen = conv - mu
    var = jnp.mean(cen * cen, axis=-1, keepdims=True)
    uu = _silu(cen * lax.rsqrt(var + EPS) * lng_ref[...] + lnb_ref[...])
    m = (jnp.dot(uu.astype(BF16), wo_ref[0:cw, :], preferred_element_type=F32)
         + jnp.dot(o_ref[...], wo_ref[cw:, :], preferred_element_type=F32))
    x1 = x_ref[...] + mod_ref[0, 2:3, :] * m
    x1_ref[...] = x1
    _moe_prologue(x1, mod_ref, g2_ref, wr_ref, h2_ref, lg_ref)


def _row_copy(src_hbm, src_row, dst, dst_row, sem):
    return pltpu.make_async_copy(src_hbm.at[pl.ds(src_row, 1)], dst.at[pl.ds(dst_row, 1)], sem)


def _gather_start(dest_ref, yb_hbm, ybuf, sem, slot):
    def body(n, carry):
        for k in range(TOP_K):
            _row_copy(yb_hbm, dest_ref[0, 0, k * TM + n], ybuf.at[slot, k], n, sem.at[slot]).start()
        return carry

    lax.fori_loop(0, TM, body, 0, unroll=8)


def _moe_combine(dcur_ref, dnxt_ref, cw_ref, yb_hbm, ybuf, sem):
    t = pl.program_id(0)
    slot = t % 2

    @pl.when(t == 0)
    def _():
        _gather_start(dcur_ref, yb_hbm, ybuf, sem, 0)

    @pl.when(t + 1 < pl.num_programs(0))
    def _():
        _gather_start(dnxt_ref, yb_hbm, ybuf, sem, 1 - slot)

    for k in range(TOP_K):
        pltpu.make_async_copy(yb_hbm.at[pl.ds(0, TM)], ybuf.at[slot, k], sem.at[slot]).wait()
    w = cw_ref[...]
    y = w[:, 0:1] * ybuf[slot, 0]
    for k in range(1, TOP_K):
        y = y + w[:, k:k + 1] * ybuf[slot, k]
    return y


def _dispatch_kernel(dest_ref, h2_hbm, xb0_hbm, xb_hbm, stage, in_sem, row_sem):
    del xb0_hbm
    t = pl.program_id(0)
    nt = pl.num_programs(0)
    slot = t % DISPATCH_SLOTS

    def stage_in(tile_idx, s):
        return pltpu.make_async_copy(h2_hbm.at[pl.ds(tile_idx * TM, TM)], stage.at[s], in_sem.at[s])

    def drain_rows(s):
        for _ in range(TOP_K):
            pltpu.make_async_copy(stage.at[s], xb_hbm.at[pl.ds(0, TM)], row_sem.at[s]).wait()

    @pl.when(t == 0)
    def _():
        stage_in(0, 0).start()

    @pl.when(jnp.logical_and(t == 0, nt > 1))
    def _():
        stage_in(1, 1).start()

    stage_in(t, slot).wait()

    def body(n, carry):
        for k in range(TOP_K):
            _row_copy(stage.at[slot], n, xb_hbm, dest_ref[0, 0, k * TM + n], row_sem.at[slot]).start()
        return carry

    lax.fori_loop(0, TM, body, 0, unroll=8)

    @pl.when(t > 0)
    def _():
        drain_rows((t + DISPATCH_SLOTS - 1) % DISPATCH_SLOTS)

    @pl.when(t + 2 < nt)
    def _():
        stage_in(t + 2, (t + 2) % DISPATCH_SLOTS).start()

    @pl.when(t == nt - 1)
    def _():
        drain_rows(slot)


def _dispatch(dest3, h2, p_len):
    n, d = h2.shape
    return pl.pallas_call(
        _dispatch_kernel,
        grid=(n // TM,),
        in_specs=[pl.BlockSpec((1, 1, TOP_K * TM), lambda t: (t, 0, 0), memory_space=pltpu.SMEM),
                  pl.BlockSpec(memory_space=pl.ANY), pl.BlockSpec(memory_space=pl.ANY)],
        out_specs=pl.BlockSpec(memory_space=pl.ANY),
        out_shape=jax.ShapeDtypeStruct((p_len, d), h2.dtype),
        scratch_shapes=[pltpu.VMEM((DISPATCH_SLOTS, TM, d), h2.dtype),
                        pltpu.SemaphoreType.DMA((DISPATCH_SLOTS,)), pltpu.SemaphoreType.DMA((DISPATCH_SLOTS,))],
        input_output_aliases={2: 0},
        compiler_params=_cparams(("arbitrary",)),
        name="moe_dispatch",
    )(dest3, h2, jnp.zeros((p_len, d), h2.dtype))


def _pre_odd_kernel(x_ref, dcur_ref, dnxt_ref, cw_ref, yb_hbm, modp_ref, mod_ref, g_ref, w_ref,
                    x2_ref, xc_ref, p_ref, bg_ref, ybuf, sem, *, cw):
    y = _moe_combine(dcur_ref, dnxt_ref, cw_ref, yb_hbm, ybuf, sem)
    x2 = x_ref[...] + modp_ref[0, 5:6, :] * y
    x2_ref[...] = x2
    h = _rms_mod(x2, g_ref[...], mod_ref[0, 1:2, :], mod_ref[0, 0:1, :])
    proj = jnp.dot(h.astype(BF16), w_ref[...], preferred_element_type=F32)
    xc_ref[...] = proj[:, :cw]
    hd = proj[:, cw:2 * cw]
    bg_ref[...] = proj[:, 2 * cw:3 * cw]
    p_ref[...] = proj[:, 3 * cw:] * hd


def _post_odd_kernel(x_ref, xc_ref, xcp_ref, xcn_ref, p_ref, pp_ref, pn_ref, bg_ref, mod_ref,
                     pw_ref, ps_ref, dw_ref, wo_ref, g2_ref, wr_ref,
                     x3_ref, h2_ref, lg_ref, extc_sc, extp_sc, *, npt, tps):
    i = pl.program_id(0)
    first, last = _seq_edges(i, npt, tps)
    _fill_ext(extc_sc, xc_ref, xcp_ref, xcn_ref, first, last)
    _fill_ext(extp_sc, p_ref, pp_ref, pn_ref, first, last)
    cw = xc_ref.shape[1]
    gc = cw // len(POOL_WINDOWS)
    is_prompt = i < npt
    t_len = jnp.where(is_prompt, TM, TM * tps)
    pos = lax.broadcasted_iota(jnp.int32, (TM, 1), 0) + jnp.where(is_prompt, 0, ((i - npt) % tps) * TM)
    ds = []
    for g, w in enumerate(POOL_WINDOWS):
        sl = slice(g * gc, (g + 1) * gc)
        acc = None
        for o in range(-(w // 2), w // 2):
            term = extc_sc[HALO + o:HALO + o + TM, sl]
            acc = term if acc is None else acc + term
        lo = jnp.maximum(pos - w // 2, 0)
        hi = jnp.minimum(pos + w // 2 - 1, t_len - 1)
        cnt = (hi - lo + 1).astype(F32)
        ds.append(acc / cnt - extc_sc[HALO:HALO + TM, sl])
    d = jnp.concatenate(ds, axis=1).astype(BF16)
    yc = jnp.dot(d, pw_ref[...], preferred_element_type=F32) * ps_ref[...]
    conv = None
    for k in range(SCONV_K):
        start = HALO - SCONV_K // 2 + k
        term = extp_sc[start:start + TM, :] * dw_ref[k:k + 1, :]
        conv = term if conv is None else conv + term
    yd = bg_ref[...] * conv
    m = (jnp.dot(yc.astype(BF16), wo_ref[0:cw, :], preferred_element_type=F32)
         + jnp.dot(yd.astype(BF16), wo_ref[cw:, :], preferred_element_type=F32))
    x3 = x_ref[...] + mod_ref[0, 2:3, :] * m
    x3_ref[...] = x3
    _moe_prologue(x3, mod_ref, g2_ref, wr_ref, h2_ref, lg_ref)


def _final_kernel(x_ref, dcur_ref, dnxt_ref, cw_ref, yb_hbm, modp_ref, g_ref, op_ref, os_ref, ybuf, sem, *, npt):
    i = pl.program_id(0)
    y = _moe_combine(dcur_ref, dnxt_ref, cw_ref, yb_hbm, ybuf, sem)
    x = x_ref[...] + modp_ref[0, 5:6, :] * y
    ms = jnp.mean(x * x, axis=-1, keepdims=True)
    out = x * lax.rsqrt(ms + EPS) * g_ref[...]

    @pl.when(i < npt)
    def _():
        op_ref[...] = out

    @pl.when(i >= npt)
    def _():
        os_ref[...] = out


def _expert_kernel(be_ref, nv_ref, x_ref, wg_ref, wu_ref, wd_ref, y_ref, wg_sc, wu_sc, wd_sc):
    i = pl.program_id(0)

    @pl.when(jnp.logical_or(i == 0, be_ref[i] != be_ref[jnp.maximum(i - 1, 0)]))
    def _():
        wg_sc[...] = wg_ref[0, 0].astype(BF16)
        wu_sc[...] = wu_ref[0, 0].astype(BF16)
        wd_sc[...] = wd_ref[0, 0].astype(BF16)

    @pl.when(i < nv_ref[0])
    def _():
        x = x_ref[...].astype(BF16)
        hg = jnp.dot(x, wg_sc[...], preferred_element_type=F32)
        hu = jnp.dot(x, wu_sc[...], preferred_element_type=F32)
        hid = (_silu(hg) * hu).astype(BF16)
        y_ref[...] = jnp.dot(hid, wd_sc[...], preferred_element_type=F32)

    @pl.when(i >= nv_ref[0])
    def _():
        y_ref[...] = jnp.zeros(y_ref.shape, F32)


def _experts(blk_e, nvalid, xb, layer, wg, wu, wd):
    p_len, d = xb.shape
    hid = wg.shape[3]
    nblk = p_len // MOE_BLK
    grid_spec = pltpu.PrefetchScalarGridSpec(
        num_scalar_prefetch=2,
        grid=(nblk,),
        in_specs=[pl.BlockSpec((MOE_BLK, d), lambda i, be, nv: (i, 0)),
                  pl.BlockSpec((1, 1, d, hid), lambda i, be, nv: (layer, be[i], 0, 0)),
                  pl.BlockSpec((1, 1, d, hid), lambda i, be, nv: (layer, be[i], 0, 0)),
                  pl.BlockSpec((1, 1, hid, d), lambda i, be, nv: (layer, be[i], 0, 0))],
        out_specs=pl.BlockSpec((MOE_BLK, d), lambda i, be, nv: (i, 0)),
        scratch_shapes=[pltpu.VMEM((d, hid), BF16), pltpu.VMEM((d, hid), BF16), pltpu.VMEM((hid, d), BF16)],
    )
    return pl.pallas_call(
        _expert_kernel,
        grid_spec=grid_spec,
        out_shape=jax.ShapeDtypeStruct((p_len, d), F32),
        compiler_params=_cparams(("arbitrary",)),
        name="moe_experts",
    )(blk_e, nvalid, xb, wg, wu, wd)


def _route_kernel(lg_ref, b_ref, info_ref, cnt_ref):
    lt = lg_ref[...].T + b_ref[...]
    gl = lt[N_EXPERTS:N_EXPERTS + N_GROUPS]
    ge = jnp.exp(gl - jnp.max(gl, axis=0, keepdims=True))
    gp = ge / jnp.sum(ge, axis=0, keepdims=True)
    best = gp[0:1]
    gidx = jnp.zeros(best.shape, jnp.int32)
    el = lt[0:EXPERTS_PER_GROUP]
    for g in range(1, N_GROUPS):
        better = gp[g:g + 1] > best
        gidx = jnp.where(better, g, gidx)
        best = jnp.where(better, gp[g:g + 1], best)
    for g in range(1, N_GROUPS):
        el = jnp.where(gidx == g, lt[g * EXPERTS_PER_GROUP:(g + 1) * EXPERTS_PER_GROUP], el)
    rio = lax.broadcasted_iota(jnp.int32, el.shape, 0)
    v0 = jnp.max(el, axis=0, keepdims=True)
    i0 = jnp.min(jnp.where(el == v0, rio, EXPERTS_PER_GROUP), axis=0, keepdims=True)
    el2 = jnp.where(rio == i0, -jnp.inf, el)
    v1 = jnp.max(el2, axis=0, keepdims=True)
    i1 = jnp.min(jnp.where(el2 == v1, rio, EXPERTS_PER_GROUP), axis=0, keepdims=True)
    e1 = jnp.exp(v1 - v0)
    den = 1.0 + e1
    w0 = (1.0 / den) * best
    w1 = (e1 / den) * best
    eid0 = gidx * EXPERTS_PER_GROUP + i0
    eid1 = gidx * EXPERTS_PER_GROUP + i1
    tm = lt.shape[1]
    eio = lax.broadcasted_iota(jnp.int32, (N_EXPERTS, tm), 0)
    oh0 = eio == eid0
    oh1 = eio == eid1
    ohs = jnp.where(oh0, 1.0, 0.0) + jnp.where(oh1, 1.0, 0.0)
    upper = (lax.broadcasted_iota(jnp.int32, (tm, tm), 0)
             < lax.broadcasted_iota(jnp.int32, (tm, tm), 1)).astype(BF16)
    before = jnp.dot(ohs.astype(BF16), upper, preferred_element_type=F32)
    rank0 = jnp.sum(jnp.where(oh0, before, 0.0), axis=0, keepdims=True)
    rank1 = jnp.sum(jnp.where(oh1, before, 0.0), axis=0, keepdims=True)
    info_ref[0] = jnp.concatenate([eid0.astype(F32), eid1.astype(F32), w0, w1, rank0, rank1,
                                   jnp.zeros((2, tm), F32)], axis=0)
    cnt_ref[0] = jnp.sum(ohs, axis=1, keepdims=True)


def _route(logits, bias_col):
    n = logits.shape[0]
    nt = n // TM
    return pl.pallas_call(
        _route_kernel,
        grid=(nt,),
        in_specs=[pl.BlockSpec((TM, ROUTER_PAD), lambda i: (i, 0)),
                  pl.BlockSpec((ROUTER_PAD, 1), lambda i: (0, 0))],
        out_specs=[pl.BlockSpec((1, 8, TM), lambda i: (i, 0, 0)),
                   pl.BlockSpec((1, N_EXPERTS, 1), lambda i: (i, 0, 0))],
        out_shape=[jax.ShapeDtypeStruct((nt, 8, TM), F32), jax.ShapeDtypeStruct((nt, N_EXPERTS, 1), F32)],
        compiler_params=_cparams(("parallel",)),
        name="moe_route",
    )(logits, bias_col)


def _moe(h2, logits, bias_col, layer, wg, wu, wd):
    n, d = h2.shape
    info, cnt = _route(logits, bias_col)
    cnt = cnt[:, :, 0].astype(jnp.int32)
    counts = jnp.sum(cnt, axis=0)
    pc = ((counts + MOE_BLK - 1) // MOE_BLK) * MOE_BLK
    pend = jnp.cumsum(pc)
    base = (pend - pc)[None, :] + jnp.cumsum(cnt, axis=0) - cnt
    eid = info[:, 0:2, :].astype(jnp.int32)
    rank = info[:, 4:6, :].astype(jnp.int32)
    sel = eid[..., None] == jnp.arange(N_EXPERTS, dtype=jnp.int32)
    dest = jnp.sum(jnp.where(sel, base[:, None, None, :], 0), axis=-1) + rank
    p_len = n * TOP_K + N_EXPERTS * MOE_BLK
    nblk = p_len // MOE_BLK
    blk_start = jnp.arange(nblk, dtype=jnp.int32) * MOE_BLK
    blk_e = jnp.minimum(jnp.sum(pend[None, :] <= blk_start[:, None], axis=1), N_EXPERTS - 1).astype(jnp.int32)
    nvalid = (pend[-1:] // MOE_BLK).astype(jnp.int32)
    dest3 = dest.reshape(dest.shape[0], 1, TOP_K * TM)
    wcol = jnp.transpose(info[:, 2:2 + TOP_K, :], (0, 2, 1)).reshape(n, TOP_K)
    xb = _dispatch(dest3, h2, p_len)
    yb = _experts(blk_e, nvalid, xb, layer, wg, wu, wd)
    return dest3, wcol, yb


def _rope_tables(t_len, qw):
    rows = t_len // GRID_W
    ax = qw // (DIFF_HEADS * 2) // 2
    row = jnp.repeat(jnp.arange(rows, dtype=F32), GRID_W)
    col = jnp.tile(jnp.arange(GRID_W, dtype=F32), rows)
    inv = ROPE_BASE ** (-jnp.arange(0, ax, 2, dtype=F32) / ax)
    ar = row[:, None] * inv[None, :]
    ac = col[:, None] * inv[None, :]
    cr, sr, cc, sc = jnp.cos(ar), jnp.sin(ar), jnp.cos(ac), jnp.sin(ac)
    cos = jnp.concatenate([cr, cr, cc, cc], axis=1)
    sins = jnp.concatenate([-sr, sr, -sc, sc], axis=1)
    reps = qw // cos.shape[1]
    cos = jnp.concatenate([jnp.ones((TM, cos.shape[1]), F32), cos], axis=0)
    sins = jnp.concatenate([jnp.zeros((TM, sins.shape[1]), F32), sins], axis=0)
    return jnp.tile(cos, (1, reps)), jnp.tile(sins, (1, reps))


def kernel(x_prompt, x_sample, cache_k, cache_v, c, c_ctx, w_ada, b_ada, g_norm1, g_norm2, g_final, w_in_e, conv_a_w, conv_a_b, ln_a_g, ln_a_b, lam_q1, lam_k1, lam_q2, lam_k2, subln_g, w_out_e, w_in_o, pool_w, pool_scale, conv_d_w, w_out_o, w_router_g, b_router_g, w_router_e, b_router_e, w_gate, w_up, w_down):
    bp, tp, d = x_prompt.shape
    bs, ts, _ = x_sample.shape
    past = cache_k.shape[2]
    assert tp == TM and ts % TM == 0 and d % LANES == 0
    npt = bp
    tps = ts // TM
    n_p, n_s = bp * tp, bs * ts
    n = n_p + n_s
    nt = n // TM
    cw = conv_a_w.shape[2]
    qw = (w_in_e.shape[2] - 2 * cw) // 3
    dh = qw // (DIFF_HEADS * 2)
    hpt = TM // HALO

    def cond_of(i):
        return jnp.where(i < npt, 0, 1 + (i - npt) // tps)

    def tile(w):
        return pl.BlockSpec((TM, w), lambda i: (i, 0))

    def halo_prev(w):
        return pl.BlockSpec((HALO, w), lambda i: (jnp.maximum(i * hpt - 1, 0), 0))

    def halo_next(w):
        return pl.BlockSpec((HALO, w), lambda i: (jnp.minimum((i + 1) * hpt, nt * hpt - 1), 0))

    def whole(shape):
        return pl.BlockSpec(shape, lambda i: (0,) * len(shape))

    mod_spec = pl.BlockSpec((1, 6, d), lambda i: (cond_of(i), 0, 0))
    combine_specs = [pl.BlockSpec((1, 1, TOP_K * TM), lambda i: (i, 0, 0), memory_space=pltpu.SMEM),
                     pl.BlockSpec((1, 1, TOP_K * TM), lambda i: (jnp.minimum(i + 1, nt - 1), 0, 0),
                                  memory_space=pltpu.SMEM),
                     tile(TOP_K), pl.BlockSpec(memory_space=pl.ANY)]
    combine_scratch = [pltpu.VMEM((2, TOP_K, TM, d), F32), pltpu.SemaphoreType.DMA((2,))]

    rows = 8 * ((1 + bs + 7) // 8)
    cond = jnp.concatenate([c_ctx[None, :], c, jnp.zeros((rows - 1 - bs, d), F32)], axis=0)
    mod = _ada_table(cond, w_ada, b_ada)

    x = jnp.concatenate([x_prompt.reshape(n_p, d), x_sample.reshape(n_s, d)], axis=0)

    def router_w(i):
        wre = jnp.transpose(w_router_e[i], (1, 0, 2)).reshape(d, N_EXPERTS)
        wr = jnp.concatenate([wre, w_router_g[i]], axis=1)
        return jnp.pad(wr, ((0, 0), (0, ROUTER_PAD - wr.shape[1]))).astype(BF16)

    def router_b(i):
        b = jnp.concatenate([b_router_e[i].reshape(N_EXPERTS), b_router_g[i]])
        return jnp.pad(b, (0, ROUTER_PAD - b.shape[0]))[:, None]

    cos, sins = _rope_tables(ts, qw)
    tab_spec = pl.BlockSpec((TM, qw), lambda i: (jnp.where(i < npt, 0, 1 + (i - npt) % tps), 0))
    cache_spec = pl.BlockSpec((TM, qw), lambda i: (jnp.minimum(i, npt - 1), 0))
    u, q, k, vt, kc, vc = pl.pallas_call(
        functools.partial(_pre_even_kernel, npt=npt, cw=cw, qw=qw, qscale=dh ** -0.5 * math.log2(math.e)),
        grid=(nt,),
        in_specs=[tile(d), mod_spec, whole((1, d)), whole(w_in_e.shape[1:]), tab_spec, tab_spec],
        out_specs=[tile(cw), tile(qw), tile(qw), pl.BlockSpec((qw, TM), lambda i: (0, i)),
                   cache_spec, cache_spec],
        out_shape=[jax.ShapeDtypeStruct((n, cw), F32), jax.ShapeDtypeStruct((n, qw), BF16),
                   jax.ShapeDtypeStruct((n, qw), BF16), jax.ShapeDtypeStruct((qw, n), BF16),
                   jax.ShapeDtypeStruct((n_p, qw), F32), jax.ShapeDtypeStruct((n_p, qw), F32)],
        compiler_params=_cparams(("arbitrary",)),
        name="pre_even",
    )(x, mod[0], g_norm1[0][None, :], w_in_e[0].astype(BF16), cos, sins)

    lam_init = 0.8 - 0.6 * math.exp(-0.3 * 0)
    lam = (jnp.exp(jnp.sum(lam_q1[0] * lam_k1[0])) - jnp.exp(jnp.sum(lam_q2[0] * lam_k2[0])) + lam_init)
    lam = lam.reshape(1).astype(F32)
    sub_g = subln_g[0][:, None]
    o_p = _attention(lam, q, k, vt, sub_g, None, batch=bp, t_len=tp, row0=0, tq=TM, tk=TM,
                     post_scale=1.0 - lam_init)
    k_ctx = cache_k[:, 0].reshape(bs * past, qw).astype(BF16)
    vt_ctx = jnp.transpose(cache_v[:, 0].reshape(bs, past, qw), (0, 2, 1)).astype(BF16)
    o_s = _attention(lam, q, k, vt, sub_g, (k_ctx, vt_ctx), batch=bs, t_len=ts, row0=n_p, tq=512, tk=512,
                     post_scale=1.0 - lam_init)
    o = jnp.concatenate([o_p, o_s], axis=0)

    moe_outs = [jax.ShapeDtypeStruct((n, d), F32), jax.ShapeDtypeStruct((n, d), F32),
                jax.ShapeDtypeStruct((n, ROUTER_PAD), F32)]
    moe_out_specs = [tile(d), tile(d), tile(ROUTER_PAD)]
    x1, h2, logits = pl.pallas_call(
        functools.partial(_post_even_kernel, npt=npt, tps=tps),
        grid=(nt,),
        in_specs=[tile(d), tile(cw), halo_prev(cw), halo_next(cw), tile(qw), mod_spec,
                  whole((CONV_K, cw)), whole((1, cw)), whole((1, cw)), whole((1, cw)),
                  whole(w_out_e.shape[1:]), whole((1, d)), whole((d, ROUTER_PAD))],
        out_specs=moe_out_specs,
        out_shape=moe_outs,
        scratch_shapes=[pltpu.VMEM((TM + 2 * HALO, cw), F32)],
        compiler_params=_cparams(("parallel",)),
        name="post_even",
    )(x, u, u, u, o, mod[0], conv_a_w[0], conv_a_b[0][None, :], ln_a_g[0][None, :], ln_a_b[0][None, :],
      w_out_e[0].astype(BF16), g_norm2[0][None, :], router_w(0))
    dest3, wcol, yb = _moe(h2, logits, router_b(0), 0, w_gate, w_up, w_down)

    pw = pool_w.shape[2]
    x2, xc, p, bg = pl.pallas_call(
        functools.partial(_pre_odd_kernel, cw=cw),
        grid=(nt,),
        in_specs=[tile(d)] + combine_specs + [mod_spec, mod_spec, whole((1, d)), whole(w_in_o.shape[1:])],
        out_specs=[tile(d), tile(cw), tile(cw), tile(cw)],
        out_shape=[jax.ShapeDtypeStruct((n, d), F32)] + [jax.ShapeDtypeStruct((n, cw), F32)] * 3,
        scratch_shapes=combine_scratch,
        compiler_params=_cparams(("arbitrary",)),
        name="pre_odd",
    )(x1, dest3, dest3, wcol, yb, mod[0], mod[1], g_norm1[1][None, :], w_in_o[0].astype(BF16))
    pool_bd = jnp.zeros((cw, cw), F32)
    for g in range(len(POOL_WINDOWS)):
        pool_bd = pool_bd.at[g * pw:(g + 1) * pw, g * pw:(g + 1) * pw].set(pool_w[0, g])
    x3, h2, logits = pl.pallas_call(
        functools.partial(_post_odd_kernel, npt=npt, tps=tps),
        grid=(nt,),
        in_specs=[tile(d), tile(cw), halo_prev(cw), halo_next(cw), tile(cw), halo_prev(cw), halo_next(cw),
                  tile(cw), mod_spec, whole((cw, cw)), whole((1, cw)), whole((SCONV_K, cw)),
                  whole(w_out_o.shape[1:]), whole((1, d)), whole((d, ROUTER_PAD))],
        out_specs=moe_out_specs,
        out_shape=moe_outs,
        scratch_shapes=[pltpu.VMEM((TM + 2 * HALO, cw), F32), pltpu.VMEM((TM + 2 * HALO, cw), F32)],
        compiler_params=_cparams(("parallel",)),
        name="post_odd",
    )(x2, xc, xc, xc, p, p, p, bg, mod[1], pool_bd.astype(BF16), pool_scale[0][None, :], conv_d_w[0],
      w_out_o[0].astype(BF16), g_norm2[1][None, :], router_w(1))
    dest3, wcol, yb = _moe(h2, logits, router_b(1), 1, w_gate, w_up, w_down)

    out_p, out_s = pl.pallas_call(
        functools.partial(_final_kernel, npt=npt),
        grid=(nt,),
        in_specs=[tile(d)] + combine_specs + [mod_spec, whole((1, d))],
        out_specs=[pl.BlockSpec((TM, d), lambda i: (jnp.minimum(i, npt - 1), 0)),
                   pl.BlockSpec((TM, d), lambda i: (jnp.maximum(i - npt, 0), 0))],
        out_shape=[jax.ShapeDtypeStruct((n_p, d), F32), jax.ShapeDtypeStruct((n_s, d), F32)],
        scratch_shapes=combine_scratch,
        compiler_params=_cparams(("arbitrary",)),
        name="final_norm",
    )(x3, dest3, dest3, wcol, yb, mod[1], g_final[None, :])

    new_k = kc.reshape(bp, 1, tp, DIFF_HEADS, 2, dh)
    new_v = vc.reshape(bp, 1, tp, DIFF_HEADS, 2 * dh)
    return (out_p.reshape(bp, tp, d), out_s.reshape(bs, ts, d), new_k, new_v)
```

```python
import functools
import math

import jax
import jax.numpy as jnp
from jax import lax
from jax.experimental import pallas as pl
from jax.experimental.pallas import tpu as pltpu

F32 = jnp.float32
BF16 = jnp.bfloat16

EPS = 1e-6
GRID_W = 64
ROPE_BASE = 10000.0
CONV_K = 31
SCONV_K = 3
POOL_WINDOWS = (2, 4, 8, 16)
DIFF_HEADS = 4
N_GROUPS = 4
EXPERTS_PER_GROUP = 8
N_EXPERTS = N_GROUPS * EXPERTS_PER_GROUP
TOP_K = 2

LANES = 128
TM = 256
HALO = 16
MOE_BLK = 256
ROUTER_PAD = 128
ONES_ROWS = 16
DISPATCH_SLOTS = 3
VMEM_LIMIT = 56 * 1024 * 1024


def _cparams(sem):
    return pltpu.CompilerParams(dimension_semantics=sem, vmem_limit_bytes=VMEM_LIMIT)


def _rms_mod(x, g, sc, sh):
    ms = jnp.mean(x * x, axis=-1, keepdims=True)
    return (x * lax.rsqrt(ms + EPS)) * g * (1.0 + sc) + sh


def _silu(x):
    return x * jax.nn.sigmoid(x)


def _ada_kernel(c_ref, w_ref, b_ref, o_ref):
    cs = _silu(c_ref[...])
    o_ref[0] = jnp.dot(cs, w_ref[0], precision=lax.Precision.HIGHEST,
                       preferred_element_type=F32) + b_ref[0]


def _ada_table(cond, w_ada, b_ada):
    depth, d, six_d = w_ada.shape
    rows = cond.shape[0]
    nj = six_d // d
    out = pl.pallas_call(
        _ada_kernel,
        grid=(depth, nj),
        in_specs=[pl.BlockSpec((rows, d), lambda l, j: (0, 0)),
                  pl.BlockSpec((1, d, d), lambda l, j: (l, 0, j)),
                  pl.BlockSpec((1, 1, d), lambda l, j: (l, 0, j))],
        out_specs=pl.BlockSpec((1, rows, d), lambda l, j: (l, 0, j)),
        out_shape=jax.ShapeDtypeStruct((depth, rows, six_d), F32),
        compiler_params=_cparams(("parallel", "parallel")),
        name="ada_table",
    )(cond, w_ada, b_ada.reshape(depth, 1, six_d))
    return out.reshape(depth, rows, nj, d)


def _rope(x, cos, sins, first_half):
    outs = []
    for j in range(x.shape[1] // LANES):
        sl = slice(j * LANES, (j + 1) * LANES)
        xs = x[:, sl]
        nxt = pltpu.roll(xs, LANES - 16, 1)
        prv = pltpu.roll(xs, 16, 1)
        rot = jnp.where(first_half, nxt, prv)
        outs.append(xs * cos[:, sl] + rot * sins[:, sl])
    return jnp.concatenate(outs, axis=1)


def _pre_even_kernel(x_ref, mod_ref, g_ref, w_ref, cos_ref, sin_ref,
                     u_ref, q_ref, k_ref, vt_ref, kc_ref, vc_ref, *, npt, cw, qw, qscale):
    i = pl.program_id(0)
    h = _rms_mod(x_ref[...], g_ref[...], mod_ref[0, 1:2, :], mod_ref[0, 0:1, :])
    proj = jnp.dot(h.astype(BF16), w_ref[...], preferred_element_type=F32)
    ga = proj[:, :cw]
    gb = proj[:, cw:2 * cw]
    u_ref[...] = ga * jax.nn.sigmoid(gb)
    q = proj[:, 2 * cw:2 * cw + qw]
    k = proj[:, 2 * cw + qw:2 * cw + 2 * qw]
    v = proj[:, 2 * cw + 2 * qw:]

    @pl.when(i < npt)
    def _():
        kc_ref[...] = k
        vc_ref[...] = v

    lane = lax.broadcasted_iota(jnp.int32, (1, LANES), 1)
    first_half = (lane % 32) < 16
    cos = cos_ref[...]
    sins = sin_ref[...]
    q_ref[...] = (_rope(q, cos, sins, first_half) * qscale).astype(BF16)
    k_ref[...] = _rope(k, cos, sins, first_half).astype(BF16)
    vt_ref[...] = v.T.astype(BF16)


def _attn_kernel(lam_ref, q_ref, k_ref, vt_ref, *rest, nk, tk, nk_ctx, post_scale):
    if nk_ctx:
        kc_ref, vtc_ref, g_ref, o_ref, s_sc, mc_sc, m_sc, acc_sc = rest
    else:
        g_ref, o_ref, s_sc, mc_sc, m_sc, acc_sc = rest
    q = q_ref[...]
    hw = q.shape[1]
    dh = hw // 2
    lane = lax.broadcasted_iota(jnp.int32, (1, hw), 1)
    zero = jnp.zeros_like(q)
    qs = (jnp.where(lane < dh, q, zero), jnp.where(lane >= dh, q, zero))
    m_sc[...] = jnp.full(m_sc.shape, -jnp.inf, F32)
    acc_sc[...] = jnp.zeros(acc_sc.shape, F32)
    ones_rows = (lax.broadcasted_iota(jnp.int32, (ONES_ROWS, tk), 0) == 0).astype(BF16)

    def scores(kk, slot):
        for c in range(2):
            s = lax.dot_general(kk, qs[c], (((1,), (1,)), ((), ())), preferred_element_type=F32)
            s_sc[slot, c] = s
            mc_sc[slot, c] = jnp.max(s, axis=0, keepdims=True)

    def consume(vt, slot):
        vta = jnp.concatenate([vt, ones_rows], axis=0)
        for c in range(2):
            m_prev = m_sc[c]
            m_new = jnp.maximum(m_prev, mc_sc[slot, c])
            alpha = jnp.exp2(m_prev - m_new)
            p = jnp.exp2(s_sc[slot, c] - m_new).astype(BF16)
            acc_sc[c] = alpha * acc_sc[c] + jnp.dot(vta, p, preferred_element_type=F32)
            m_sc[c] = m_new

    def k_at(i):
        if i < nk_ctx:
            return kc_ref[i * tk:(i + 1) * tk, :]
        return k_ref[(i - nk_ctx) * tk:(i - nk_ctx + 1) * tk, :]

    def vt_at(i):
        if i < nk_ctx:
            return vtc_ref[0, :, i * tk:(i + 1) * tk]
        return vt_ref[:, (i - nk_ctx) * tk:(i - nk_ctx + 1) * tk]

    n_total = nk_ctx + nk
    peel = nk_ctx + (n_total - 1 - nk_ctx) % 2
    scores(k_at(0), 0)
    for i in range(peel):
        scores(k_at(i + 1), (i + 1) % 2)
        consume(vt_at(i), i % 2)

    def body(t, carry):
        for r in range(2):
            j = 2 * t + (peel + r - nk_ctx)
            nxt = pl.multiple_of((j + 1) * tk, tk)
            cur = pl.multiple_of(j * tk, tk)
            scores(k_ref[pl.ds(nxt, tk), :], (peel + r + 1) % 2)
            consume(vt_ref[:, pl.ds(cur, tk)], (peel + r) % 2)
        return carry

    lax.fori_loop(0, (n_total - 1 - peel) // 2, body, 0)
    consume(vt_at(n_total - 1), (n_total - 1) % 2)
    o = (acc_sc[0, :hw, :] / acc_sc[0, hw:hw + 1, :]
         - lam_ref[0] * (acc_sc[1, :hw, :] / acc_sc[1, hw:hw + 1, :]))
    ms = jnp.mean(o * o, axis=0, keepdims=True)
    o = o * lax.rsqrt(ms + EPS) * g_ref[...] * post_scale
    o_ref[...] = o.T.astype(BF16)


def _attention(lam, q, k, vt, g_col, ctx, *, batch, t_len, row0, tq, tk, post_scale):
    hw = q.shape[1] // DIFF_HEADS
    nq = t_len // tq
    assert row0 % t_len == 0 and t_len % tk == 0
    qb0 = row0 // tq
    sb0 = row0 // t_len
    in_specs = [pl.BlockSpec(memory_space=pltpu.SMEM),
                pl.BlockSpec((tq, hw), lambda b, h, i: (qb0 + b * nq + i, h)),
                pl.BlockSpec((t_len, hw), lambda b, h, i: (sb0 + b, h)),
                pl.BlockSpec((hw, t_len), lambda b, h, i: (h, sb0 + b))]
    args = [lam, q, k, vt]
    nk_ctx = 0
    if ctx is not None:
        k_ctx, vt_ctx = ctx
        past = vt_ctx.shape[2]
        assert past % tk == 0
        nk_ctx = past // tk
        in_specs += [pl.BlockSpec((past, hw), lambda b, h, i: (b, h)),
                     pl.BlockSpec((1, hw, past), lambda b, h, i: (b, h, 0))]
        args += [k_ctx, vt_ctx]
    in_specs.append(pl.BlockSpec((hw, 1), lambda b, h, i: (0, 0)))
    args.append(g_col)
    return pl.pallas_call(
        functools.partial(_attn_kernel, nk=t_len // tk, tk=tk, nk_ctx=nk_ctx, post_scale=post_scale),
        grid=(batch, DIFF_HEADS, nq),
        in_specs=in_specs,
        out_specs=pl.BlockSpec((tq, hw), lambda b, h, i: (b * nq + i, h)),
        out_shape=jax.ShapeDtypeStruct((batch * t_len, q.shape[1]), BF16),
        scratch_shapes=[pltpu.VMEM((2, 2, tk, tq), F32), pltpu.VMEM((2, 2, 1, tq), F32),
                        pltpu.VMEM((2, 1, tq), F32), pltpu.VMEM((2, hw + ONES_ROWS, tq), F32)],
        compiler_params=_cparams(("parallel", "parallel", "parallel")),
        name="diff_attention",
    )(*args)


def _seq_edges(i, npt, tps):
    r = (i - npt) % tps
    is_prompt = i < npt
    first = jnp.logical_or(is_prompt, r == 0)
    last = jnp.logical_or(is_prompt, r == tps - 1)
    return first, last


def _fill_ext(ext_sc, cur_ref, prev_ref, next_ref, first, last):
    prev = prev_ref[...]
    nxt = next_ref[...]
    ext_sc[0:HALO, :] = jnp.where(first, jnp.zeros_like(prev), prev)
    ext_sc[HALO:HALO + TM, :] = cur_ref[...]
    ext_sc[HALO + TM:, :] = jnp.where(last, jnp.zeros_like(nxt), nxt)


def _moe_prologue(x_new, mod_ref, g2_ref, wr_ref, h2_ref, lg_ref):
    h2 = _rms_mod(x_new, g2_ref[...], mod_ref[0, 4:5, :], mod_ref[0, 3:4, :])
    h2_ref[...] = h2
    lg_ref[...] = jnp.dot(h2.astype(BF16), wr_ref[...], preferred_element_type=F32)


def _post_even_kernel(x_ref, u_ref, up_ref, un_ref, o_ref, mod_ref, cw_ref, cb_ref, lng_ref, lnb_ref,
                      wo_ref, g2_ref, wr_ref, x1_ref, h2_ref, lg_ref, ext_sc, *, npt, tps):
    i = pl.program_id(0)
    first, last = _seq_edges(i, npt, tps)
    _fill_ext(ext_sc, u_ref, up_ref, un_ref, first, last)
    cw = u_ref.shape[1]
    rows = 32
    chunks = []
    for rc in range(TM // rows):
        acc = None
        for k in range(CONV_K):
            start = rc * rows + HALO - CONV_K // 2 + k
            term = ext_sc[start:start + rows, :] * cw_ref[k:k + 1, :]
            acc = term if acc is None else acc + term
        chunks.append(acc)
    conv = jnp.concatenate(chunks, axis=0) + cb_ref[...]
    mu = jnp.mean(conv, axis=-1, keepdims=True)
    cen = conv - mu
    var = jnp.mean(cen * cen, axis=-1, keepdims=True)
    uu = _silu(cen * lax.rsqrt(var + EPS) * lng_ref[...] + lnb_ref[...])
    m = (jnp.dot(uu.astype(BF16), wo_ref[0:cw, :], preferred_element_type=F32)
         + jnp.dot(o_ref[...], wo_ref[cw:, :], preferred_element_type=F32))
    x1 = x_ref[...] + mod_ref[0, 2:3, :] * m
    x1_ref[...] = x1
    _moe_prologue(x1, mod_ref, g2_ref, wr_ref, h2_ref, lg_ref)


def _row_copy(src_hbm, src_row, dst, dst_row, sem):
    return pltpu.make_async_copy(src_hbm.at[pl.ds(src_row, 1)], dst.at[pl.ds(dst_row, 1)], sem)


def _gather_start(dest_ref, yb_hbm, ybuf, sem, slot):
    def body(n, carry):
        for k in range(TOP_K):
            _row_copy(yb_hbm, dest_ref[0, 0, k * TM + n], ybuf.at[slot, k], n, sem.at[slot]).start(priority=k % 2)
        return carry

    lax.fori_loop(0, TM, body, 0, unroll=8)


def _moe_combine(dcur_ref, dnxt_ref, cw_ref, yb_hbm, ybuf, sem):
    t = pl.program_id(0)
    slot = t % 2

    @pl.when(t == 0)
    def _():
        _gather_start(dcur_ref, yb_hbm, ybuf, sem, 0)

    @pl.when(t + 1 < pl.num_programs(0))
    def _():
        _gather_start(dnxt_ref, yb_hbm, ybuf, sem, 1 - slot)

    for k in range(TOP_K):
        pltpu.make_async_copy(yb_hbm.at[pl.ds(0, TM)], ybuf.at[slot, k], sem.at[slot]).wait()
    w = cw_ref[...]
    y = w[:, 0:1] * ybuf[slot, 0]
    for k in range(1, TOP_K):
        y = y + w[:, k:k + 1] * ybuf[slot, k]
    return y


def _dispatch_kernel(dest_ref, pend_ref, h2_hbm, xb_hbm, stage, zeros, in_sem, row_sem, zero_sem):
    t = pl.program_id(0)
    nt = pl.num_programs(0)
    slot = t % DISPATCH_SLOTS

    def stage_in(tile_idx, s):
        return pltpu.make_async_copy(h2_hbm.at[pl.ds(tile_idx * TM, TM)], stage.at[s], in_sem.at[s])

    def drain_rows(s):
        for _ in range(TOP_K):
            pltpu.make_async_copy(stage.at[s], xb_hbm.at[pl.ds(0, TM)], row_sem.at[s]).wait()

    def zero_block(start):
        return pltpu.make_async_copy(zeros, xb_hbm.at[pl.ds(pl.multiple_of(start, MOE_BLK), MOE_BLK)], zero_sem)

    def zero_fill(op):
        for e in range(N_EXPERTS):
            op(zero_block(jnp.maximum(pend_ref[e] - MOE_BLK, 0)))
        for j in range(N_EXPERTS):
            start = pend_ref[N_EXPERTS - 1] + j * MOE_BLK

            @pl.when(start < xb_hbm.shape[0])
            def _():
                op(zero_block(start))

    @pl.when(t == 0)
    def _():
        stage_in(0, 0).start()
        zeros[...] = jnp.zeros(zeros.shape, zeros.dtype)
        zero_fill(lambda cp: cp.start())
        zero_fill(lambda cp: cp.wait())

    @pl.when(jnp.logical_and(t == 0, nt > 1))
    def _():
        stage_in(1, 1).start()

    stage_in(t, slot).wait()

    def body(n, carry):
        for k in range(TOP_K):
            _row_copy(stage.at[slot], n, xb_hbm, dest_ref[0, 0, k * TM + n], row_sem.at[slot]).start(priority=k % 2)
        return carry

    lax.fori_loop(0, TM, body, 0, unroll=8)

    @pl.when(t > 0)
    def _():
        drain_rows((t + DISPATCH_SLOTS - 1) % DISPATCH_SLOTS)

    @pl.when(t + 2 < nt)
    def _():
        stage_in(t + 2, (t + 2) % DISPATCH_SLOTS).start()

    @pl.when(t == nt - 1)
    def _():
        drain_rows(slot)


def _dispatch(dest3, pend, h2, p_len):
    n, d = h2.shape
    return pl.pallas_call(
        _dispatch_kernel,
        grid=(n // TM,),
        in_specs=[pl.BlockSpec((1, 1, TOP_K * TM), lambda t: (t, 0, 0), memory_space=pltpu.SMEM),
                  pl.BlockSpec(memory_space=pltpu.SMEM), pl.BlockSpec(memory_space=pl.ANY)],
        out_specs=pl.BlockSpec(memory_space=pl.ANY),
        out_shape=jax.ShapeDtypeStruct((p_len, d), h2.dtype),
        scratch_shapes=[pltpu.VMEM((DISPATCH_SLOTS, TM, d), h2.dtype), pltpu.VMEM((MOE_BLK, d), h2.dtype),
                        pltpu.SemaphoreType.DMA((DISPATCH_SLOTS,)), pltpu.SemaphoreType.DMA((DISPATCH_SLOTS,)),
                        pltpu.SemaphoreType.DMA(())],
        compiler_params=_cparams(("arbitrary",)),
        name="moe_dispatch",
    )(dest3, pend, h2)


def _pre_odd_kernel(x_ref, dcur_ref, dnxt_ref, cw_ref, yb_hbm, modp_ref, mod_ref, g_ref, w_ref,
                    x2_ref, xc_ref, p_ref, bg_ref, ybuf, sem, *, cw):
    y = _moe_combine(dcur_ref, dnxt_ref, cw_ref, yb_hbm, ybuf, sem)
    x2 = x_ref[...] + modp_ref[0, 5:6, :] * y
    x2_ref[...] = x2
    h = _rms_mod(x2, g_ref[...], mod_ref[0, 1:2, :], mod_ref[0, 0:1, :])
    proj = jnp.dot(h.astype(BF16), w_ref[...], preferred_element_type=F32)
    xc_ref[...] = proj[:, :cw]
    hd = proj[:, cw:2 * cw]
    bg_ref[...] = proj[:, 2 * cw:3 * cw]
    p_ref[...] = proj[:, 3 * cw:] * hd


def _post_odd_kernel(x_ref, xc_ref, xcp_ref, xcn_ref, p_ref, pp_ref, pn_ref, bg_ref, mod_ref,
                     pw_ref, ps_ref, dw_ref, wo_ref, g2_ref, wr_ref,
                     x3_ref, h2_ref, lg_ref, extc_sc, extp_sc, *, npt, tps):
    i = pl.program_id(0)
    first, last = _seq_edges(i, npt, tps)
    _fill_ext(extc_sc, xc_ref, xcp_ref, xcn_ref, first, last)
    _fill_ext(extp_sc, p_ref, pp_ref, pn_ref, first, last)
    cw = xc_ref.shape[1]
    gc = cw // len(POOL_WINDOWS)
    is_prompt = i < npt
    t_len = jnp.where(is_prompt, TM, TM * tps)
    pos = lax.broadcasted_iota(jnp.int32, (TM, 1), 0) + jnp.where(is_prompt, 0, ((i - npt) % tps) * TM)
    ds = []
    for g, w in enumerate(POOL_WINDOWS):
        sl = slice(g * gc, (g + 1) * gc)
        acc = None
        for o in range(-(w // 2), w // 2):
            term = extc_sc[HALO + o:HALO + o + TM, sl]
            acc = term if acc is None else acc + term
        lo = jnp.maximum(pos - w // 2, 0)
        hi = jnp.minimum(pos + w // 2 - 1, t_len - 1)
        cnt = (hi - lo + 1).astype(F32)
        ds.append(acc / cnt - extc_sc[HALO:HALO + TM, sl])
    d = jnp.concatenate(ds, axis=1).astype(BF16)
    yc = jnp.dot(d, pw_ref[...], preferred_element_type=F32) * ps_ref[...]
    conv = None
    for k in range(SCONV_K):
        start = HALO - SCONV_K // 2 + k
        term = extp_sc[start:start + TM, :] * dw_ref[k:k + 1, :]
        conv = term if conv is None else conv + term
    yd = bg_ref[...] * conv
    m = (jnp.dot(yc.astype(BF16), wo_ref[0:cw, :], preferred_element_type=F32)
         + jnp.dot(yd.astype(BF16), wo_ref[cw:, :], preferred_element_type=F32))
    x3 = x_ref[...] + mod_ref[0, 2:3, :] * m
    x3_ref[...] = x3
    _moe_prologue(x3, mod_ref, g2_ref, wr_ref, h2_ref, lg_ref)


def _final_kernel(x_ref, dcur_ref, dnxt_ref, cw_ref, yb_hbm, modp_ref, g_ref, op_ref, os_ref, ybuf, sem, *, npt):
    i = pl.program_id(0)
    y = _moe_combine(dcur_ref, dnxt_ref, cw_ref, yb_hbm, ybuf, sem)
    x = x_ref[...] + modp_ref[0, 5:6, :] * y
    ms = jnp.mean(x * x, axis=-1, keepdims=True)
    out = x * lax.rsqrt(ms + EPS) * g_ref[...]

    @pl.when(i < npt)
    def _():
        op_ref[...] = out

    @pl.when(i >= npt)
    def _():
        os_ref[...] = out


def _expert_kernel(be_ref, nv_ref, x_ref, wg_ref, wu_ref, wd_ref, y_ref, wg_sc, wu_sc, wd_sc):
    i = pl.program_id(0)

    @pl.when(jnp.logical_or(i == 0, be_ref[i] != be_ref[jnp.maximum(i - 1, 0)]))
    def _():
        wg_sc[...] = wg_ref[0, 0].astype(BF16)
        wu_sc[...] = wu_ref[0, 0].astype(BF16)
        wd_sc[...] = wd_ref[0, 0].astype(BF16)

    @pl.when(i < nv_ref[0])
    def _():
        x = x_ref[...].astype(BF16)
        hg = jnp.dot(x, wg_sc[...], preferred_element_type=F32)
        hu = jnp.dot(x, wu_sc[...], preferred_element_type=F32)
        hid = (_silu(hg) * hu).astype(BF16)
        y_ref[...] = jnp.dot(hid, wd_sc[...], preferred_element_type=F32)

    @pl.when(i >= nv_ref[0])
    def _():
        y_ref[...] = jnp.zeros(y_ref.shape, F32)


def _experts(blk_e, nvalid, xb, layer, wg, wu, wd):
    p_len, d = xb.shape
    hid = wg.shape[3]
    nblk = p_len // MOE_BLK
    grid_spec = pltpu.PrefetchScalarGridSpec(
        num_scalar_prefetch=2,
        grid=(nblk,),
        in_specs=[pl.BlockSpec((MOE_BLK, d), lambda i, be, nv: (jnp.minimum(i, nv[0] - 1), 0)),
                  pl.BlockSpec((1, 1, d, hid), lambda i, be, nv: (layer, be[i], 0, 0)),
                  pl.BlockSpec((1, 1, d, hid), lambda i, be, nv: (layer, be[i], 0, 0)),
                  pl.BlockSpec((1, 1, hid, d), lambda i, be, nv: (layer, be[i], 0, 0))],
        out_specs=pl.BlockSpec((MOE_BLK, d), lambda i, be, nv: (i, 0)),
        scratch_shapes=[pltpu.VMEM((d, hid), BF16), pltpu.VMEM((d, hid), BF16), pltpu.VMEM((hid, d), BF16)],
    )
    return pl.pallas_call(
        _expert_kernel,
        grid_spec=grid_spec,
        out_shape=jax.ShapeDtypeStruct((p_len, d), F32),
        compiler_params=_cparams(("arbitrary",)),
        name="moe_experts",
    )(blk_e, nvalid, xb, wg, wu, wd)


def _route_kernel(lg_ref, b_ref, info_ref, cnt_ref):
    lt = lg_ref[...].T + b_ref[...]
    gl = lt[N_EXPERTS:N_EXPERTS + N_GROUPS]
    ge = jnp.exp(gl - jnp.max(gl, axis=0, keepdims=True))
    gp = ge / jnp.sum(ge, axis=0, keepdims=True)
    best = gp[0:1]
    gidx = jnp.zeros(best.shape, jnp.int32)
    el = lt[0:EXPERTS_PER_GROUP]
    for g in range(1, N_GROUPS):
        better = gp[g:g + 1] > best
        gidx = jnp.where(better, g, gidx)
        best = jnp.where(better, gp[g:g + 1], best)
    for g in range(1, N_GROUPS):
        el = jnp.where(gidx == g, lt[g * EXPERTS_PER_GROUP:(g + 1) * EXPERTS_PER_GROUP], el)
    rio = lax.broadcasted_iota(jnp.int32, el.shape, 0)
    v0 = jnp.max(el, axis=0, keepdims=True)
    i0 = jnp.min(jnp.where(el == v0, rio, EXPERTS_PER_GROUP), axis=0, keepdims=True)
    el2 = jnp.where(rio == i0, -jnp.inf, el)
    v1 = jnp.max(el2, axis=0, keepdims=True)
    i1 = jnp.min(jnp.where(el2 == v1, rio, EXPERTS_PER_GROUP), axis=0, keepdims=True)
    e1 = jnp.exp(v1 - v0)
    den = 1.0 + e1
    w0 = (1.0 / den) * best
    w1 = (e1 / den) * best
    eid0 = gidx * EXPERTS_PER_GROUP + i0
    eid1 = gidx * EXPERTS_PER_GROUP + i1
    tm = lt.shape[1]
    eio = lax.broadcasted_iota(jnp.int32, (N_EXPERTS, tm), 0)
    oh0 = eio == eid0
    oh1 = eio == eid1
    ohs = jnp.where(oh0, 1.0, 0.0) + jnp.where(oh1, 1.0, 0.0)
    upper = (lax.broadcasted_iota(jnp.int32, (tm, tm), 0)
             < lax.broadcasted_iota(jnp.int32, (tm, tm), 1)).astype(BF16)
    before = jnp.dot(ohs.astype(BF16), upper, preferred_element_type=F32)
    rank0 = jnp.sum(jnp.where(oh0, before, 0.0), axis=0, keepdims=True)
    rank1 = jnp.sum(jnp.where(oh1, before, 0.0), axis=0, keepdims=True)
    info_ref[0] = jnp.concatenate([eid0.astype(F32), eid1.astype(F32), w0, w1, rank0, rank1,
                                   jnp.zeros((2, tm), F32)], axis=0)
    cnt_ref[0] = jnp.sum(ohs, axis=1, keepdims=True)


def _route(logits, bias_col):
    n = logits.shape[0]
    nt = n // TM
    return pl.pallas_call(
        _route_kernel,
        grid=(nt,),
        in_specs=[pl.BlockSpec((TM, ROUTER_PAD), lambda i: (i, 0)),
                  pl.BlockSpec((ROUTER_PAD, 1), lambda i: (0, 0))],
        out_specs=[pl.BlockSpec((1, 8, TM), lambda i: (i, 0, 0)),
                   pl.BlockSpec((1, N_EXPERTS, 1), lambda i: (i, 0, 0))],
        out_shape=[jax.ShapeDtypeStruct((nt, 8, TM), F32), jax.ShapeDtypeStruct((nt, N_EXPERTS, 1), F32)],
        compiler_params=_cparams(("parallel",)),
        name="moe_route",
    )(logits, bias_col)


def _moe(h2, logits, bias_col, layer, wg, wu, wd):
    n, d = h2.shape
    info, cnt = _route(logits, bias_col)
    cnt = cnt[:, :, 0].astype(jnp.int32)
    counts = jnp.sum(cnt, axis=0)
    pc = ((counts + MOE_BLK - 1) // MOE_BLK) * MOE_BLK
    pend = jnp.cumsum(pc)
    base = (pend - pc)[None, :] + jnp.cumsum(cnt, axis=0) - cnt
    eid = info[:, 0:2, :].astype(jnp.int32)
    rank = info[:, 4:6, :].astype(jnp.int32)
    sel = eid[..., None] == jnp.arange(N_EXPERTS, dtype=jnp.int32)
    dest = jnp.sum(jnp.where(sel, base[:, None, None, :], 0), axis=-1) + rank
    p_len = n * TOP_K + N_EXPERTS * MOE_BLK
    nblk = p_len // MOE_BLK
    blk_start = jnp.arange(nblk, dtype=jnp.int32) * MOE_BLK
    blk_e = jnp.minimum(jnp.sum(pend[None, :] <= blk_start[:, None], axis=1), N_EXPERTS - 1).astype(jnp.int32)
    nvalid = (pend[-1:] // MOE_BLK).astype(jnp.int32)
    dest3 = dest.reshape(dest.shape[0], 1, TOP_K * TM)
    wcol = jnp.transpose(info[:, 2:2 + TOP_K, :], (0, 2, 1)).reshape(n, TOP_K)
    xb = _dispatch(dest3, pend.astype(jnp.int32), h2, p_len)
    yb = _experts(blk_e, nvalid, xb, layer, wg, wu, wd)
    return dest3, wcol, yb


def _rope_tables(t_len, qw):
    rows = t_len // GRID_W
    ax = qw // (DIFF_HEADS * 2) // 2
    row = jnp.repeat(jnp.arange(rows, dtype=F32), GRID_W)
    col = jnp.tile(jnp.arange(GRID_W, dtype=F32), rows)
    inv = ROPE_BASE ** (-jnp.arange(0, ax, 2, dtype=F32) / ax)
    ar = row[:, None] * inv[None, :]
    ac = col[:, None] * inv[None, :]
    cr, sr, cc, sc = jnp.cos(ar), jnp.sin(ar), jnp.cos(ac), jnp.sin(ac)
    cos = jnp.concatenate([cr, cr, cc, cc], axis=1)
    sins = jnp.concatenate([-sr, sr, -sc, sc], axis=1)
    reps = qw // cos.shape[1]
    cos = jnp.concatenate([jnp.ones((TM, cos.shape[1]), F32), cos], axis=0)
    sins = jnp.concatenate([jnp.zeros((TM, sins.shape[1]), F32), sins], axis=0)
    return jnp.tile(cos, (1, reps)), jnp.tile(sins, (1, reps))


def kernel(x_prompt, x_sample, cache_k, cache_v, c, c_ctx, w_ada, b_ada, g_norm1, g_norm2, g_final, w_in_e, conv_a_w, conv_a_b, ln_a_g, ln_a_b, lam_q1, lam_k1, lam_q2, lam_k2, subln_g, w_out_e, w_in_o, pool_w, pool_scale, conv_d_w, w_out_o, w_router_g, b_router_g, w_router_e, b_router_e, w_gate, w_up, w_down):
    bp, tp, d = x_prompt.shape
    bs, ts, _ = x_sample.shape
    past = cache_k.shape[2]
    assert tp == TM and ts % TM == 0 and d % LANES == 0
    npt = bp
    tps = ts // TM
    n_p, n_s = bp * tp, bs * ts
    n = n_p + n_s
    nt = n // TM
    cw = conv_a_w.shape[2]
    qw = (w_in_e.shape[2] - 2 * cw) // 3
    dh = qw // (DIFF_HEADS * 2)
    hpt = TM // HALO

    def cond_of(i):
        return jnp.where(i < npt, 0, 1 + (i - npt) // tps)

    def tile(w):
        return pl.BlockSpec((TM, w), lambda i: (i, 0))

    def halo_prev(w):
        return pl.BlockSpec((HALO, w), lambda i: (jnp.maximum(i * hpt - 1, 0), 0))

    def halo_next(w):
        return pl.BlockSpec((HALO, w), lambda i: (jnp.minimum((i + 1) * hpt, nt * hpt - 1), 0))

    def whole(shape):
        return pl.BlockSpec(shape, lambda i: (0,) * len(shape))

    mod_spec = pl.BlockSpec((1, 6, d), lambda i: (cond_of(i), 0, 0))
    combine_specs = [pl.BlockSpec((1, 1, TOP_K * TM), lambda i: (i, 0, 0), memory_space=pltpu.SMEM),
                     pl.BlockSpec((1, 1, TOP_K * TM), lambda i: (jnp.minimum(i + 1, nt - 1), 0, 0),
                                  memory_space=pltpu.SMEM),
                     tile(TOP_K), pl.BlockSpec(memory_space=pl.ANY)]
    combine_scratch = [pltpu.VMEM((2, TOP_K, TM, d), F32), pltpu.SemaphoreType.DMA((2,))]

    rows = 8 * ((1 + bs + 7) // 8)
    cond = jnp.concatenate([c_ctx[None, :], c, jnp.zeros((rows - 1 - bs, d), F32)], axis=0)
    mod = _ada_table(cond, w_ada, b_ada)

    x = jnp.concatenate([x_prompt.reshape(n_p, d), x_sample.reshape(n_s, d)], axis=0)

    def router_w(i):
        wre = jnp.transpose(w_router_e[i], (1, 0, 2)).reshape(d, N_EXPERTS)
        wr = jnp.concatenate([wre, w_router_g[i]], axis=1)
        return jnp.pad(wr, ((0, 0), (0, ROUTER_PAD - wr.shape[1]))).astype(BF16)

    def router_b(i):
        b = jnp.concatenate([b_router_e[i].reshape(N_EXPERTS), b_router_g[i]])
        return jnp.pad(b, (0, ROUTER_PAD - b.shape[0]))[:, None]

    cos, sins = _rope_tables(ts, qw)
    tab_spec = pl.BlockSpec((TM, qw), lambda i: (jnp.where(i < npt, 0, 1 + (i - npt) % tps), 0))
    cache_spec = pl.BlockSpec((TM, qw), lambda i: (jnp.minimum(i, npt - 1), 0))
    u, q, k, vt, kc, vc = pl.pallas_call(
        functools.partial(_pre_even_kernel, npt=npt, cw=cw, qw=qw, qscale=dh ** -0.5 * math.log2(math.e)),
        grid=(nt,),
        in_specs=[tile(d), mod_spec, whole((1, d)), whole(w_in_e.shape[1:]), tab_spec, tab_spec],
        out_specs=[tile(cw), tile(qw), tile(qw), pl.BlockSpec((qw, TM), lambda i: (0, i)),
                   cache_spec, cache_spec],
        out_shape=[jax.ShapeDtypeStruct((n, cw), F32), jax.ShapeDtypeStruct((n, qw), BF16),
                   jax.ShapeDtypeStruct((n, qw), BF16), jax.ShapeDtypeStruct((qw, n), BF16),
                   jax.ShapeDtypeStruct((n_p, qw), F32), jax.ShapeDtypeStruct((n_p, qw), F32)],
        compiler_params=_cparams(("arbitrary",)),
        name="pre_even",
    )(x, mod[0], g_norm1[0][None, :], w_in_e[0].astype(BF16), cos, sins)

    lam_init = 0.8 - 0.6 * math.exp(-0.3 * 0)
    lam = (jnp.exp(jnp.sum(lam_q1[0] * lam_k1[0])) - jnp.exp(jnp.sum(lam_q2[0] * lam_k2[0])) + lam_init)
    lam = lam.reshape(1).astype(F32)
    sub_g = subln_g[0][:, None]
    o_p = _attention(lam, q, k, vt, sub_g, None, batch=bp, t_len=tp, row0=0, tq=TM, tk=TM,
                     post_scale=1.0 - lam_init)
    k_ctx = cache_k[:, 0].reshape(bs * past, qw).astype(BF16)
    vt_ctx = jnp.transpose(cache_v[:, 0].reshape(bs, past, qw), (0, 2, 1)).astype(BF16)
    o_s = _attention(lam, q, k, vt, sub_g, (k_ctx, vt_ctx), batch=bs, t_len=ts, row0=n_p, tq=512, tk=512,
                     post_scale=1.0 - lam_init)
    o = jnp.concatenate([o_p, o_s], axis=0)

    moe_outs = [jax.ShapeDtypeStruct((n, d), F32), jax.ShapeDtypeStruct((n, d), F32),
                jax.ShapeDtypeStruct((n, ROUTER_PAD), F32)]
    moe_out_specs = [tile(d), tile(d), tile(ROUTER_PAD)]
    x1, h2, logits = pl.pallas_call(
        functools.partial(_post_even_kernel, npt=npt, tps=tps),
        grid=(nt,),
        in_specs=[tile(d), tile(cw), halo_prev(cw), halo_next(cw), tile(qw), mod_spec,
                  whole((CONV_K, cw)), whole((1, cw)), whole((1, cw)), whole((1, cw)),
                  whole(w_out_e.shape[1:]), whole((1, d)), whole((d, ROUTER_PAD))],
        out_specs=moe_out_specs,
        out_shape=moe_outs,
        scratch_shapes=[pltpu.VMEM((TM + 2 * HALO, cw), F32)],
        compiler_params=_cparams(("parallel",)),
        name="post_even",
    )(x, u, u, u, o, mod[0], conv_a_w[0], conv_a_b[0][None, :], ln_a_g[0][None, :], ln_a_b[0][None, :],
      w_out_e[0].astype(BF16), g_norm2[0][None, :], router_w(0))
    dest3, wcol, yb = _moe(h2, logits, router_b(0), 0, w_gate, w_up, w_down)

    pw = pool_w.shape[2]
    x2, xc, p, bg = pl.pallas_call(
        functools.partial(_pre_odd_kernel, cw=cw),
        grid=(nt,),
        in_specs=[tile(d)] + combine_specs + [mod_spec, mod_spec, whole((1, d)), whole(w_in_o.shape[1:])],
        out_specs=[tile(d), tile(cw), tile(cw), tile(cw)],
        out_shape=[jax.ShapeDtypeStruct((n, d), F32)] + [jax.ShapeDtypeStruct((n, cw), F32)] * 3,
        scratch_shapes=combine_scratch,
        compiler_params=_cparams(("arbitrary",)),
        name="pre_odd",
    )(x1, dest3, dest3, wcol, yb, mod[0], mod[1], g_norm1[1][None, :], w_in_o[0].astype(BF16))
    pool_bd = jnp.zeros((cw, cw), F32)
    for g in range(len(POOL_WINDOWS)):
        pool_bd = pool_bd.at[g * pw:(g + 1) * pw, g * pw:(g + 1) * pw].set(pool_w[0, g])
    x3, h2, logits = pl.pallas_call(
        functools.partial(_post_odd_kernel, npt=npt, tps=tps),
        grid=(nt,),
        in_specs=[tile(d), tile(cw), halo_prev(cw), halo_next(cw), tile(cw), halo_prev(cw), halo_next(cw),
                  tile(cw), mod_spec, whole((cw, cw)), whole((1, cw)), whole((SCONV_K, cw)),
                  whole(w_out_o.shape[1:]), whole((1, d)), whole((d, ROUTER_PAD))],
        out_specs=moe_out_specs,
        out_shape=moe_outs,
        scratch_shapes=[pltpu.VMEM((TM + 2 * HALO, cw), F32), pltpu.VMEM((TM + 2 * HALO, cw), F32)],
        compiler_params=_cparams(("parallel",)),
        name="post_odd",
    )(x2, xc, xc, xc, p, p, p, bg, mod[1], pool_bd.astype(BF16), pool_scale[0][None, :], conv_d_w[0],
      w_out_o[0].astype(BF16), g_norm2[1][None, :], router_w(1))
    dest3, wcol, yb = _moe(h2, logits, router_b(1), 1, w_gate, w_up, w_down)

    out_p, out_s = pl.pallas_call(
        functools.partial(_final_kernel, npt=npt),
        grid=(nt,),
        in_specs=[tile(d)] + combine_specs + [mod_spec, whole((1, d))],
        out_specs=[pl.BlockSpec((TM, d), lambda i: (jnp.minimum(i, npt - 1), 0)),
                   pl.BlockSpec((TM, d), lambda i: (jnp.maximum(i - npt, 0), 0))],
        out_shape=[jax.ShapeDtypeStruct((n_p, d), F32), jax.ShapeDtypeStruct((n_s, d), F32)],
        scratch_shapes=combine_scratch,
        compiler_params=_cparams(("arbitrary",)),
        name="final_norm",
    )(x3, dest3, dest3, wcol, yb, mod[1], g_final[None, :])

    new_k = kc.reshape(bp, 1, tp, DIFF_HEADS, 2, dh)
    new_v = vc.reshape(bp, 1, tp, DIFF_HEADS, 2 * dh)
    return (out_p.reshape(bp, tp, d), out_s.reshape(bs, ts, d), new_k, new_v)
```

```python
import functools
import math

import jax
import jax.numpy as jnp
from jax import lax
from jax.experimental import pallas as pl
from jax.experimental.pallas import tpu as pltpu

F32 = jnp.float32
BF16 = jnp.bfloat16

EPS = 1e-6
GRID_W = 64
ROPE_BASE = 10000.0
CONV_K = 31
SCONV_K = 3
POOL_WINDOWS = (2, 4, 8, 16)
DIFF_HEADS = 4
N_GROUPS = 4
EXPERTS_PER_GROUP = 8
N_EXPERTS = N_GROUPS * EXPERTS_PER_GROUP
TOP_K = 2

LANES = 128
SUBLANES = 8
TM = 256
HALO = 16
MOE_BLK = 256
ROUTER_PAD = 128
ONES_ROWS = 16
DISPATCH_SLOTS = 3
VMEM_LIMIT = 56 * 1024 * 1024


def _cparams(sem):
    return pltpu.CompilerParams(dimension_semantics=sem, vmem_limit_bytes=VMEM_LIMIT)


def _rms_mod(x, g, sc, sh):
    ms = jnp.mean(x * x, axis=-1, keepdims=True)
    return (x * lax.rsqrt(ms + EPS)) * g * (1.0 + sc) + sh


def _silu(x):
    return x * jax.nn.sigmoid(x)


def _ada_kernel(c_ref, w_ref, b_ref, o_ref):
    cs = _silu(c_ref[...])
    o_ref[0] = jnp.dot(cs, w_ref[0], precision=lax.Precision.HIGHEST,
                       preferred_element_type=F32) + b_ref[0]


def _ada_table(cond, w_ada, b_ada):
    depth, d, six_d = w_ada.shape
    rows = cond.shape[0]
    nj = six_d // d
    out = pl.pallas_call(
        _ada_kernel,
        grid=(depth, nj),
        in_specs=[pl.BlockSpec((rows, d), lambda l, j: (0, 0)),
                  pl.BlockSpec((1, d, d), lambda l, j: (l, 0, j)),
                  pl.BlockSpec((1, 1, d), lambda l, j: (l, 0, j))],
        out_specs=pl.BlockSpec((1, rows, d), lambda l, j: (l, 0, j)),
        out_shape=jax.ShapeDtypeStruct((depth, rows, six_d), F32),
        compiler_params=_cparams(("parallel", "parallel")),
        name="ada_table",
    )(cond, w_ada, b_ada.reshape(depth, 1, six_d))
    return out.reshape(depth, rows, nj, d)


def _rope(x, cos, sins, first_half):
    outs = []
    for j in range(x.shape[1] // LANES):
        sl = slice(j * LANES, (j + 1) * LANES)
        xs = x[:, sl]
        nxt = pltpu.roll(xs, LANES - 16, 1)
        prv = pltpu.roll(xs, 16, 1)
        rot = jnp.where(first_half, nxt, prv)
        outs.append(xs * cos[:, sl] + rot * sins[:, sl])
    return jnp.concatenate(outs, axis=1)


def _pre_even_kernel(xp_ref, xs_ref, mod_ref, g_ref, w_ref, cos_ref, sin_ref,
                     u_ref, q_ref, k_ref, vt_ref, kc_ref, vc_ref, *, npt, cw, qw, qscale):
    i = pl.program_id(0)
    x = jnp.where(i < npt, xp_ref[...], xs_ref[...])
    h = _rms_mod(x, g_ref[...], mod_ref[0, 1:2, :], mod_ref[0, 0:1, :])
    proj = jnp.dot(h.astype(BF16), w_ref[...], preferred_element_type=F32)
    ga = proj[:, :cw]
    gb = proj[:, cw:2 * cw]
    u_ref[...] = ga * jax.nn.sigmoid(gb)
    q = proj[:, 2 * cw:2 * cw + qw]
    k = proj[:, 2 * cw + qw:2 * cw + 2 * qw]
    v = proj[:, 2 * cw + 2 * qw:]

    @pl.when(i < npt)
    def _():
        kc_ref[...] = k
        vc_ref[...] = v

    lane = lax.broadcasted_iota(jnp.int32, (1, LANES), 1)
    first_half = (lane % 32) < 16
    cos = cos_ref[...]
    sins = sin_ref[...]
    q_ref[...] = (_rope(q, cos, sins, first_half) * qscale).astype(BF16)
    k_ref[...] = _rope(k, cos, sins, first_half).astype(BF16)
    vt_ref[...] = v.T.astype(BF16)


def _attn_kernel(lam_ref, q_ref, k_ref, vt_ref, *rest, nk, tk, nk_ctx, post_scale):
    if nk_ctx:
        kc_ref, vtc_ref, g_ref, o_ref, s_sc, mc_sc, m_sc, acc_sc = rest
    else:
        g_ref, o_ref, s_sc, mc_sc, m_sc, acc_sc = rest
    q = q_ref[...]
    hw = q.shape[1]
    dh = hw // 2
    lane = lax.broadcasted_iota(jnp.int32, (1, hw), 1)
    zero = jnp.zeros_like(q)
    qs = (jnp.where(lane < dh, q, zero), jnp.where(lane >= dh, q, zero))
    m_sc[...] = jnp.full(m_sc.shape, -jnp.inf, F32)
    acc_sc[...] = jnp.zeros(acc_sc.shape, F32)
    ones_rows = (lax.broadcasted_iota(jnp.int32, (ONES_ROWS, tk), 0) == 0).astype(BF16)

    def scores(kk, slot):
        for c in range(2):
            s = lax.dot_general(kk, qs[c], (((1,), (1,)), ((), ())), preferred_element_type=F32)
            s_sc[slot, c] = s
            mc_sc[slot, c] = jnp.max(s, axis=0, keepdims=True)

    def consume(vt, slot):
        vta = jnp.concatenate([vt, ones_rows], axis=0)
        for c in range(2):
            m_prev = m_sc[c]
            m_new = jnp.maximum(m_prev, mc_sc[slot, c])
            alpha = jnp.exp2(m_prev - m_new)
            p = jnp.exp2(s_sc[slot, c] - m_new).astype(BF16)
            acc_sc[c] = alpha * acc_sc[c] + jnp.dot(vta, p, preferred_element_type=F32)
            m_sc[c] = m_new

    def k_at(i):
        if i < nk_ctx:
            return kc_ref[i * tk:(i + 1) * tk, :]
        return k_ref[(i - nk_ctx) * tk:(i - nk_ctx + 1) * tk, :]

    def vt_at(i):
        if i < nk_ctx:
            return vtc_ref[0, :, i * tk:(i + 1) * tk]
        return vt_ref[:, (i - nk_ctx) * tk:(i - nk_ctx + 1) * tk]

    n_total = nk_ctx + nk
    peel = nk_ctx + (n_total - 1 - nk_ctx) % 2
    scores(k_at(0), 0)
    for i in range(peel):
        scores(k_at(i + 1), (i + 1) % 2)
        consume(vt_at(i), i % 2)

    def body(t, carry):
        for r in range(2):
            j = 2 * t + (peel + r - nk_ctx)
            nxt = pl.multiple_of((j + 1) * tk, tk)
            cur = pl.multiple_of(j * tk, tk)
            scores(k_ref[pl.ds(nxt, tk), :], (peel + r + 1) % 2)
            consume(vt_ref[:, pl.ds(cur, tk)], (peel + r) % 2)
        return carry

    lax.fori_loop(0, (n_total - 1 - peel) // 2, body, 0)
    consume(vt_at(n_total - 1), (n_total - 1) % 2)
    o = (acc_sc[0, :hw, :] / acc_sc[0, hw:hw + 1, :]
         - lam_ref[0] * (acc_sc[1, :hw, :] / acc_sc[1, hw:hw + 1, :]))
    ms = jnp.mean(o * o, axis=0, keepdims=True)
    o = o * lax.rsqrt(ms + EPS) * g_ref[...] * post_scale
    o_ref[...] = o.T.astype(BF16)


def _attention(lam, q, k, vt, g_col, ctx, *, batch, t_len, row0, tq, tk, post_scale):
    hw = q.shape[1] // DIFF_HEADS
    nq = t_len // tq
    assert row0 % t_len == 0 and t_len % tk == 0
    qb0 = row0 // tq
    sb0 = row0 // t_len
    in_specs = [pl.BlockSpec(memory_space=pltpu.SMEM),
                pl.BlockSpec((tq, hw), lambda b, h, i: (qb0 + b * nq + i, h)),
                pl.BlockSpec((t_len, hw), lambda b, h, i: (sb0 + b, h)),
                pl.BlockSpec((hw, t_len), lambda b, h, i: (h, sb0 + b))]
    args = [lam, q, k, vt]
    nk_ctx = 0
    if ctx is not None:
        k_ctx, vt_ctx = ctx
        past = vt_ctx.shape[2]
        assert past % tk == 0
        nk_ctx = past // tk
        in_specs += [pl.BlockSpec((past, hw), lambda b, h, i: (b, h)),
                     pl.BlockSpec((1, hw, past), lambda b, h, i: (b, h, 0))]
        args += [k_ctx, vt_ctx]
    in_specs.append(pl.BlockSpec((hw, 1), lambda b, h, i: (0, 0)))
    args.append(g_col)
    return pl.pallas_call(
        functools.partial(_attn_kernel, nk=t_len // tk, tk=tk, nk_ctx=nk_ctx, post_scale=post_scale),
        grid=(batch, DIFF_HEADS, nq),
        in_specs=in_specs,
        out_specs=pl.BlockSpec((tq, hw), lambda b, h, i: (b * nq + i, h)),
        out_shape=jax.ShapeDtypeStruct((batch * t_len, q.shape[1]), BF16),
        scratch_shapes=[pltpu.VMEM((2, 2, tk, tq), F32), pltpu.VMEM((2, 2, 1, tq), F32),
                        pltpu.VMEM((2, 1, tq), F32), pltpu.VMEM((2, hw + ONES_ROWS, tq), F32)],
        compiler_params=_cparams(("parallel", "parallel", "parallel")),
        name="diff_attention",
    )(*args)


def _seq_edges(i, npt, tps):
    r = (i - npt) % tps
    is_prompt = i < npt
    first = jnp.logical_or(is_prompt, r == 0)
    last = jnp.logical_or(is_prompt, r == tps - 1)
    return first, last


def _fill_ext(ext_sc, cur_ref, prev_ref, next_ref, first, last):
    prev = prev_ref[...]
    nxt = next_ref[...]
    ext_sc[0:HALO, :] = jnp.where(first, jnp.zeros_like(prev), prev)
    ext_sc[HALO:HALO + TM, :] = cur_ref[...]
    ext_sc[HALO + TM:, :] = jnp.where(last, jnp.zeros_like(nxt), nxt)


def _moe_prologue(x_new, mod_ref, g2_ref, wr_ref, h2_ref, lg_ref):
    h2 = _rms_mod(x_new, g2_ref[...], mod_ref[0, 4:5, :], mod_ref[0, 3:4, :])
    h2_ref[...] = h2
    lg_ref[...] = jnp.dot(h2.astype(BF16), wr_ref[...], preferred_element_type=F32)


def _post_even_kernel(xp_ref, xs_ref, u_ref, up_ref, un_ref, op_ref, os_ref, mod_ref, cw_ref, cb_ref,
                      lng_ref, lnb_ref, wo_ref, g2_ref, wr_ref, x1_ref, h2_ref, lg_ref, ext_sc, sh_sc, *, npt, tps):
    i = pl.program_id(0)
    first, last = _seq_edges(i, npt, tps)
    _fill_ext(ext_sc, u_ref, up_ref, un_ref, first, last)
    cw = u_ref.shape[1]
    span = sh_sc.shape[1]
    for r in range(SUBLANES):
        sh_sc[r] = ext_sc[r:r + span, :]
    rows = 32
    chunks = []
    for rc in range(TM // rows):
        acc = None
        for k in range(CONV_K):
            off = HALO - CONV_K // 2 + k
            start = rc * rows + off - off % SUBLANES
            term = sh_sc[off % SUBLANES, start:start + rows, :] * cw_ref[k:k + 1, :]
            acc = term if acc is None else acc + term
        chunks.append(acc)
    conv = jnp.concatenate(chunks, axis=0) + cb_ref[...]
    mu = jnp.mean(conv, axis=-1, keepdims=True)
    cen = conv - mu
    var = jnp.mean(cen * cen, axis=-1, keepdims=True)
    uu = _silu(cen * lax.rsqrt(var + EPS) * lng_ref[...] + lnb_ref[...])
    is_prompt = i < npt
    o = jnp.where(is_prompt, op_ref[...], os_ref[...])
    m = (jnp.dot(uu.astype(BF16), wo_ref[0:cw, :], preferred_element_type=F32)
         + jnp.dot(o, wo_ref[cw:, :], preferred_element_type=F32))
    x1 = jnp.where(is_prompt, xp_ref[...], xs_ref[...]) + mod_ref[0, 2:3, :] * m
    x1_ref[...] = x1
    _moe_prologue(x1, mod_ref, g2_ref, wr_ref, h2_ref, lg_ref)


def _row_copy(src_hbm, src_row, dst, dst_row, sem):
    return pltpu.make_async_copy(src_hbm.at[pl.ds(src_row, 1)], dst.at[pl.ds(dst_row, 1)], sem)


def _gather_start(dest_ref, yb_hbm, ybuf, sem, slot):
    def body(n, carry):
        for k in range(TOP_K):
            _row_copy(yb_hbm, dest_ref[0, 0, k * TM + n], ybuf.at[slot, k], n, sem.at[slot]).start(priority=k % 2)
        return carry

    lax.fori_loop(0, TM, body, 0, unroll=8)


def _moe_combine(dcur_ref, dnxt_ref, cw_ref, yb_hbm, ybuf, sem):
    t = pl.program_id(0)
    slot = t % 2

    @pl.when(t == 0)
    def _():
        _gather_start(dcur_ref, yb_hbm, ybuf, sem, 0)

    @pl.when(t + 1 < pl.num_programs(0))
    def _():
        _gather_start(dnxt_ref, yb_hbm, ybuf, sem, 1 - slot)

    for k in range(TOP_K):
        pltpu.make_async_copy(yb_hbm.at[pl.ds(0, TM)], ybuf.at[slot, k], sem.at[slot]).wait()
    w = cw_ref[...]
    y = w[:, 0:1] * ybuf[slot, 0]
    for k in range(1, TOP_K):
        y = y + w[:, k:k + 1] * ybuf[slot, k]
    return y


def _dispatch_kernel(dest_ref, pend_ref, h2_hbm, xb_hbm, stage, zeros, in_sem, row_sem, zero_sem):
    t = pl.program_id(0)
    nt = pl.num_programs(0)
    slot = t % DISPATCH_SLOTS

    def stage_in(tile_idx, s):
        return pltpu.make_async_copy(h2_hbm.at[pl.ds(tile_idx * TM, TM)], stage.at[s], in_sem.at[s])

    def drain_rows(s):
        for _ in range(TOP_K):
            pltpu.make_async_copy(stage.at[s], xb_hbm.at[pl.ds(0, TM)], row_sem.at[s]).wait()

    def zero_block(start):
        return pltpu.make_async_copy(zeros, xb_hbm.at[pl.ds(pl.multiple_of(start, MOE_BLK), MOE_BLK)], zero_sem)

    def zero_fill(op):
        for e in range(N_EXPERTS):
            op(zero_block(jnp.maximum(pend_ref[e] - MOE_BLK, 0)))
        for j in range(N_EXPERTS):
            start = pend_ref[N_EXPERTS - 1] + j * MOE_BLK

            @pl.when(start < xb_hbm.shape[0])
            def _():
                op(zero_block(start))

    @pl.when(t == 0)
    def _():
        stage_in(0, 0).start()
        zeros[...] = jnp.zeros(zeros.shape, zeros.dtype)
        zero_fill(lambda cp: cp.start())
        zero_fill(lambda cp: cp.wait())

    @pl.when(jnp.logical_and(t == 0, nt > 1))
    def _():
        stage_in(1, 1).start()

    stage_in(t, slot).wait()

    def body(n, carry):
        for k in range(TOP_K):
            _row_copy(stage.at[slot], n, xb_hbm, dest_ref[0, 0, k * TM + n], row_sem.at[slot]).start(priority=k % 2)
        return carry

    lax.fori_loop(0, TM, body, 0, unroll=8)

    @pl.when(t > 0)
    def _():
        drain_rows((t + DISPATCH_SLOTS - 1) % DISPATCH_SLOTS)

    @pl.when(t + 2 < nt)
    def _():
        stage_in(t + 2, (t + 2) % DISPATCH_SLOTS).start()

    @pl.when(t == nt - 1)
    def _():
        drain_rows(slot)


def _dispatch(dest3, pend, h2, p_len):
    n, d = h2.shape
    return pl.pallas_call(
        _dispatch_kernel,
        grid=(n // TM,),
        in_specs=[pl.BlockSpec((1, 1, TOP_K * TM), lambda t: (t, 0, 0), memory_space=pltpu.SMEM),
                  pl.BlockSpec(memory_space=pltpu.SMEM), pl.BlockSpec(memory_space=pl.ANY)],
        out_specs=pl.BlockSpec(memory_space=pl.ANY),
        out_shape=jax.ShapeDtypeStruct((p_len, d), h2.dtype),
        scratch_shapes=[pltpu.VMEM((DISPATCH_SLOTS, TM, d), h2.dtype), pltpu.VMEM((MOE_BLK, d), h2.dtype),
                        pltpu.SemaphoreType.DMA((DISPATCH_SLOTS,)), pltpu.SemaphoreType.DMA((DISPATCH_SLOTS,)),
                        pltpu.SemaphoreType.DMA(())],
        compiler_params=_cparams(("arbitrary",)),
        name="moe_dispatch",
    )(dest3, pend, h2)


def _pre_odd_kernel(x_ref, dcur_ref, dnxt_ref, cw_ref, yb_hbm, modp_ref, mod_ref, g_ref, w_ref,
                    x2_ref, xc_ref, p_ref, bg_ref, ybuf, sem, *, cw):
    y = _moe_combine(dcur_ref, dnxt_ref, cw_ref, yb_hbm, ybuf, sem)
    x2 = x_ref[...] + modp_ref[0, 5:6, :] * y
    x2_ref[...] = x2
    h = _rms_mod(x2, g_ref[...], mod_ref[0, 1:2, :], mod_ref[0, 0:1, :])
    proj = jnp.dot(h.astype(BF16), w_ref[...], preferred_element_type=F32)
    xc_ref[...] = proj[:, :cw]
    hd = proj[:, cw:2 * cw]
    bg_ref[...] = proj[:, 2 * cw:3 * cw]
    p_ref[...] = proj[:, 3 * cw:] * hd


def _post_odd_kernel(x_ref, xc_ref, xcp_ref, xcn_ref, p_ref, pp_ref, pn_ref, bg_ref, mod_ref,
                     pw_ref, ps_ref, dw_ref, wo_ref, g2_ref, wr_ref,
                     x3_ref, h2_ref, lg_ref, extc_sc, extp_sc, *, npt, tps):
    i = pl.program_id(0)
    first, last = _seq_edges(i, npt, tps)
    _fill_ext(extc_sc, xc_ref, xcp_ref, xcn_ref, first, last)
    _fill_ext(extp_sc, p_ref, pp_ref, pn_ref, first, last)
    cw = xc_ref.shape[1]
    gc = cw // len(POOL_WINDOWS)
    is_prompt = i < npt
    t_len = jnp.where(is_prompt, TM, TM * tps)
    pos = lax.broadcasted_iota(jnp.int32, (TM, 1), 0) + jnp.where(is_prompt, 0, ((i - npt) % tps) * TM)
    ds = []
    for g, w in enumerate(POOL_WINDOWS):
        sl = slice(g * gc, (g + 1) * gc)
        acc = None
        for o in range(-(w // 2), w // 2):
            term = extc_sc[HALO + o:HALO + o + TM, sl]
            acc = term if acc is None else acc + term
        lo = jnp.maximum(pos - w // 2, 0)
        hi = jnp.minimum(pos + w // 2 - 1, t_len - 1)
        cnt = (hi - lo + 1).astype(F32)
        ds.append(acc / cnt - extc_sc[HALO:HALO + TM, sl])
    d = jnp.concatenate(ds, axis=1).astype(BF16)
    yc = jnp.dot(d, pw_ref[...], preferred_element_type=F32) * ps_ref[...]
    conv = None
    for k in range(SCONV_K):
        start = HALO - SCONV_K // 2 + k
        term = extp_sc[start:start + TM, :] * dw_ref[k:k + 1, :]
        conv = term if conv is None else conv + term
    yd = bg_ref[...] * conv
    m = (jnp.dot(yc.astype(BF16), wo_ref[0:cw, :], preferred_element_type=F32)
         + jnp.dot(yd.astype(BF16), wo_ref[cw:, :], preferred_element_type=F32))
    x3 = x_ref[...] + mod_ref[0, 2:3, :] * m
    x3_ref[...] = x3
    _moe_prologue(x3, mod_ref, g2_ref, wr_ref, h2_ref, lg_ref)


def _final_kernel(x_ref, dcur_ref, dnxt_ref, cw_ref, yb_hbm, modp_ref, g_ref, op_ref, os_ref, ybuf, sem, *, npt):
    i = pl.program_id(0)
    y = _moe_combine(dcur_ref, dnxt_ref, cw_ref, yb_hbm, ybuf, sem)
    x = x_ref[...] + modp_ref[0, 5:6, :] * y
    ms = jnp.mean(x * x, axis=-1, keepdims=True)
    out = x * lax.rsqrt(ms + EPS) * g_ref[...]

    @pl.when(i < npt)
    def _():
        op_ref[...] = out

    @pl.when(i >= npt)
    def _():
        os_ref[...] = out


def _expert_kernel(be_ref, nv_ref, x_ref, wg_ref, wu_ref, wd_ref, y_ref, wg_sc, wu_sc, wd_sc):
    i = pl.program_id(0)

    @pl.when(jnp.logical_or(i == 0, be_ref[i] != be_ref[jnp.maximum(i - 1, 0)]))
    def _():
        wg_sc[...] = wg_ref[0, 0].astype(BF16)
        wu_sc[...] = wu_ref[0, 0].astype(BF16)
        wd_sc[...] = wd_ref[0, 0].astype(BF16)

    @pl.when(i < nv_ref[0])
    def _():
        x = x_ref[...].astype(BF16)
        hg = jnp.dot(x, wg_sc[...], preferred_element_type=F32)
        hu = jnp.dot(x, wu_sc[...], preferred_element_type=F32)
        hid = (_silu(hg) * hu).astype(BF16)
        y_ref[...] = jnp.dot(hid, wd_sc[...], preferred_element_type=F32)

    @pl.when(i >= nv_ref[0])
    def _():
        y_ref[...] = jnp.zeros(y_ref.shape, F32)


def _experts(blk_e, nvalid, xb, layer, wg, wu, wd):
    p_len, d = xb.shape
    hid = wg.shape[3]
    nblk = p_len // MOE_BLK
    grid_spec = pltpu.PrefetchScalarGridSpec(
        num_scalar_prefetch=2,
        grid=(nblk,),
        in_specs=[pl.BlockSpec((MOE_BLK, d), lambda i, be, nv: (jnp.minimum(i, nv[0] - 1), 0)),
                  pl.BlockSpec((1, 1, d, hid), lambda i, be, nv: (layer, be[i], 0, 0)),
                  pl.BlockSpec((1, 1, d, hid), lambda i, be, nv: (layer, be[i], 0, 0)),
                  pl.BlockSpec((1, 1, hid, d), lambda i, be, nv: (layer, be[i], 0, 0))],
        out_specs=pl.BlockSpec((MOE_BLK, d), lambda i, be, nv: (i, 0)),
        scratch_shapes=[pltpu.VMEM((d, hid), BF16), pltpu.VMEM((d, hid), BF16), pltpu.VMEM((hid, d), BF16)],
    )
    return pl.pallas_call(
        _expert_kernel,
        grid_spec=grid_spec,
        out_shape=jax.ShapeDtypeStruct((p_len, d), F32),
        compiler_params=_cparams(("arbitrary",)),
        name="moe_experts",
    )(blk_e, nvalid, xb, wg, wu, wd)


def _route_kernel(lg_ref, b_ref, info_ref, cnt_ref):
    lt = lg_ref[...].T + b_ref[...]
    gl = lt[N_EXPERTS:N_EXPERTS + N_GROUPS]
    ge = jnp.exp(gl - jnp.max(gl, axis=0, keepdims=True))
    gp = ge / jnp.sum(ge, axis=0, keepdims=True)
    best = gp[0:1]
    gidx = jnp.zeros(best.shape, jnp.int32)
    el = lt[0:EXPERTS_PER_GROUP]
    for g in range(1, N_GROUPS):
        better = gp[g:g + 1] > best
        gidx = jnp.where(better, g, gidx)
        best = jnp.where(better, gp[g:g + 1], best)
    for g in range(1, N_GROUPS):
        el = jnp.where(gidx == g, lt[g * EXPERTS_PER_GROUP:(g + 1) * EXPERTS_PER_GROUP], el)
    rio = lax.broadcasted_iota(jnp.int32, el.shape, 0)
    v0 = jnp.max(el, axis=0, keepdims=True)
    i0 = jnp.min(jnp.where(el == v0, rio, EXPERTS_PER_GROUP), axis=0, keepdims=True)
    el2 = jnp.where(rio == i0, -jnp.inf, el)
    v1 = jnp.max(el2, axis=0, keepdims=True)
    i1 = jnp.min(jnp.where(el2 == v1, rio, EXPERTS_PER_GROUP), axis=0, keepdims=True)
    e1 = jnp.exp(v1 - v0)
    den = 1.0 + e1
    w0 = (1.0 / den) * best
    w1 = (e1 / den) * best
    eid0 = gidx * EXPERTS_PER_GROUP + i0
    eid1 = gidx * EXPERTS_PER_GROUP + i1
    tm = lt.shape[1]
    eio = lax.broadcasted_iota(jnp.int32, (N_EXPERTS, tm), 0)
    oh0 = eio == eid0
    oh1 = eio == eid1
    ohs = jnp.where(oh0, 1.0, 0.0) + jnp.where(oh1, 1.0, 0.0)
    upper = (lax.broadcasted_iota(jnp.int32, (tm, tm), 0)
             < lax.broadcasted_iota(jnp.int32, (tm, tm), 1)).astype(BF16)
    before = jnp.dot(ohs.astype(BF16), upper, preferred_element_type=F32)
    rank0 = jnp.sum(jnp.where(oh0, before, 0.0), axis=0, keepdims=True)
    rank1 = jnp.sum(jnp.where(oh1, before, 0.0), axis=0, keepdims=True)
    info_ref[0] = jnp.concatenate([eid0.astype(F32), eid1.astype(F32), w0, w1, rank0, rank1,
                                   jnp.zeros((2, tm), F32)], axis=0)
    cnt_ref[0] = jnp.sum(ohs, axis=1, keepdims=True)


def _route(logits, bias_col):
    n = logits.shape[0]
    nt = n // TM
    return pl.pallas_call(
        _route_kernel,
        grid=(nt,),
        in_specs=[pl.BlockSpec((TM, ROUTER_PAD), lambda i: (i, 0)),
                  pl.BlockSpec((ROUTER_PAD, 1), lambda i: (0, 0))],
        out_specs=[pl.BlockSpec((1, 8, TM), lambda i: (i, 0, 0)),
                   pl.BlockSpec((1, N_EXPERTS, 1), lambda i: (i, 0, 0))],
        out_shape=[jax.ShapeDtypeStruct((nt, 8, TM), F32), jax.ShapeDtypeStruct((nt, N_EXPERTS, 1), F32)],
        compiler_params=_cparams(("parallel",)),
        name="moe_route",
    )(logits, bias_col)


def _moe(h2, logits, bias_col, layer, wg, wu, wd):
    n, d = h2.shape
    info, cnt = _route(logits, bias_col)
    cnt = cnt[:, :, 0].astype(jnp.int32)
    counts = jnp.sum(cnt, axis=0)
    pc = ((counts + MOE_BLK - 1) // MOE_BLK) * MOE_BLK
    pend = jnp.cumsum(pc)
    base = (pend - pc)[None, :] + jnp.cumsum(cnt, axis=0) - cnt
    eid = info[:, 0:2, :].astype(jnp.int32)
    rank = info[:, 4:6, :].astype(jnp.int32)
    sel = eid[..., None] == jnp.arange(N_EXPERTS, dtype=jnp.int32)
    dest = jnp.sum(jnp.where(sel, base[:, None, None, :], 0), axis=-1) + rank
    p_len = n * TOP_K + N_EXPERTS * MOE_BLK
    nblk = p_len // MOE_BLK
    blk_start = jnp.arange(nblk, dtype=jnp.int32) * MOE_BLK
    blk_e = jnp.minimum(jnp.sum(pend[None, :] <= blk_start[:, None], axis=1), N_EXPERTS - 1).astype(jnp.int32)
    nvalid = (pend[-1:] // MOE_BLK).astype(jnp.int32)
    dest3 = dest.reshape(dest.shape[0], 1, TOP_K * TM)
    wcol = jnp.transpose(info[:, 2:2 + TOP_K, :], (0, 2, 1)).reshape(n, TOP_K)
    xb = _dispatch(dest3, pend.astype(jnp.int32), h2, p_len)
    yb = _experts(blk_e, nvalid, xb, layer, wg, wu, wd)
    return dest3, wcol, yb


def _rope_tables(t_len, qw):
    rows = t_len // GRID_W
    ax = qw // (DIFF_HEADS * 2) // 2
    row = jnp.repeat(jnp.arange(rows, dtype=F32), GRID_W)
    col = jnp.tile(jnp.arange(GRID_W, dtype=F32), rows)
    inv = ROPE_BASE ** (-jnp.arange(0, ax, 2, dtype=F32) / ax)
    ar = row[:, None] * inv[None, :]
    ac = col[:, None] * inv[None, :]
    cr, sr, cc, sc = jnp.cos(ar), jnp.sin(ar), jnp.cos(ac), jnp.sin(ac)
    cos = jnp.concatenate([cr, cr, cc, cc], axis=1)
    sins = jnp.concatenate([-sr, sr, -sc, sc], axis=1)
    reps = qw // cos.shape[1]
    cos = jnp.concatenate([jnp.ones((TM, cos.shape[1]), F32), cos], axis=0)
    sins = jnp.concatenate([jnp.zeros((TM, sins.shape[1]), F32), sins], axis=0)
    return jnp.tile(cos, (1, reps)), jnp.tile(sins, (1, reps))


def kernel(x_prompt, x_sample, cache_k, cache_v, c, c_ctx, w_ada, b_ada, g_norm1, g_norm2, g_final, w_in_e, conv_a_w, conv_a_b, ln_a_g, ln_a_b, lam_q1, lam_k1, lam_q2, lam_k2, subln_g, w_out_e, w_in_o, pool_w, pool_scale, conv_d_w, w_out_o, w_router_g, b_router_g, w_router_e, b_router_e, w_gate, w_up, w_down):
    bp, tp, d = x_prompt.shape
    bs, ts, _ = x_sample.shape
    past = cache_k.shape[2]
    assert tp == TM and ts % TM == 0 and d % LANES == 0
    npt = bp
    tps = ts // TM
    n_p, n_s = bp * tp, bs * ts
    n = n_p + n_s
    nt = n // TM
    cw = conv_a_w.shape[2]
    qw = (w_in_e.shape[2] - 2 * cw) // 3
    dh = qw // (DIFF_HEADS * 2)
    hpt = TM // HALO

    def cond_of(i):
        return jnp.where(i < npt, 0, 1 + (i - npt) // tps)

    def tile(w):
        return pl.BlockSpec((TM, w), lambda i: (i, 0))

    def halo_prev(w):
        return pl.BlockSpec((HALO, w), lambda i: (jnp.maximum(i * hpt - 1, 0), 0))

    def halo_next(w):
        return pl.BlockSpec((HALO, w), lambda i: (jnp.minimum((i + 1) * hpt, nt * hpt - 1), 0))

    def whole(shape):
        return pl.BlockSpec(shape, lambda i: (0,) * len(shape))

    mod_spec = pl.BlockSpec((1, 6, d), lambda i: (cond_of(i), 0, 0))
    combine_specs = [pl.BlockSpec((1, 1, TOP_K * TM), lambda i: (i, 0, 0), memory_space=pltpu.SMEM),
                     pl.BlockSpec((1, 1, TOP_K * TM), lambda i: (jnp.minimum(i + 1, nt - 1), 0, 0),
                                  memory_space=pltpu.SMEM),
                     tile(TOP_K), pl.BlockSpec(memory_space=pl.ANY)]
    combine_scratch = [pltpu.VMEM((2, TOP_K, TM, d), F32), pltpu.SemaphoreType.DMA((2,))]

    rows = 8 * ((1 + bs + 7) // 8)
    cond = jnp.concatenate([c_ctx[None, :], c, jnp.zeros((rows - 1 - bs, d), F32)], axis=0)
    mod = _ada_table(cond, w_ada, b_ada)

    xp2, xs2 = x_prompt.reshape(n_p, d), x_sample.reshape(n_s, d)

    def prompt_tile(w):
        return pl.BlockSpec((TM, w), lambda i: (jnp.minimum(i, npt - 1), 0))

    def sample_tile(w):
        return pl.BlockSpec((TM, w), lambda i: (jnp.maximum(i - npt, 0), 0))

    def router_w(i):
        wre = jnp.transpose(w_router_e[i], (1, 0, 2)).reshape(d, N_EXPERTS)
        wr = jnp.concatenate([wre, w_router_g[i]], axis=1)
        return jnp.pad(wr, ((0, 0), (0, ROUTER_PAD - wr.shape[1]))).astype(BF16)

    def router_b(i):
        b = jnp.concatenate([b_router_e[i].reshape(N_EXPERTS), b_router_g[i]])
        return jnp.pad(b, (0, ROUTER_PAD - b.shape[0]))[:, None]

    cos, sins = _rope_tables(ts, qw)
    tab_spec = pl.BlockSpec((TM, qw), lambda i: (jnp.where(i < npt, 0, 1 + (i - npt) % tps), 0))
    cache_spec = pl.BlockSpec((TM, qw), lambda i: (jnp.minimum(i, npt - 1), 0))
    u, q, k, vt, kc, vc = pl.pallas_call(
        functools.partial(_pre_even_kernel, npt=npt, cw=cw, qw=qw, qscale=dh ** -0.5 * math.log2(math.e)),
        grid=(nt,),
        in_specs=[prompt_tile(d), sample_tile(d), mod_spec, whole((1, d)), whole(w_in_e.shape[1:]),
                  tab_spec, tab_spec],
        out_specs=[tile(cw), tile(qw), tile(qw), pl.BlockSpec((qw, TM), lambda i: (0, i)),
                   cache_spec, cache_spec],
        out_shape=[jax.ShapeDtypeStruct((n, cw), F32), jax.ShapeDtypeStruct((n, qw), BF16),
                   jax.ShapeDtypeStruct((n, qw), BF16), jax.ShapeDtypeStruct((qw, n), BF16),
                   jax.ShapeDtypeStruct((n_p, qw), F32), jax.ShapeDtypeStruct((n_p, qw), F32)],
        compiler_params=_cparams(("arbitrary",)),
        name="pre_even",
    )(xp2, xs2, mod[0], g_norm1[0][None, :], w_in_e[0].astype(BF16), cos, sins)

    lam_init = 0.8 - 0.6 * math.exp(-0.3 * 0)
    lam = (jnp.exp(jnp.sum(lam_q1[0] * lam_k1[0])) - jnp.exp(jnp.sum(lam_q2[0] * lam_k2[0])) + lam_init)
    lam = lam.reshape(1).astype(F32)
    sub_g = subln_g[0][:, None]
    o_p = _attention(lam, q, k, vt, sub_g, None, batch=bp, t_len=tp, row0=0, tq=TM, tk=TM,
                     post_scale=1.0 - lam_init)
    k_ctx = cache_k[:, 0].reshape(bs * past, qw).astype(BF16)
    vt_ctx = jnp.transpose(cache_v[:, 0].reshape(bs, past, qw), (0, 2, 1)).astype(BF16)
    o_s = _attention(lam, q, k, vt, sub_g, (k_ctx, vt_ctx), batch=bs, t_len=ts, row0=n_p, tq=512, tk=512,
                     post_scale=1.0 - lam_init)

    moe_outs = [jax.ShapeDtypeStruct((n, d), F32), jax.ShapeDtypeStruct((n, d), F32),
                jax.ShapeDtypeStruct((n, ROUTER_PAD), F32)]
    moe_out_specs = [tile(d), tile(d), tile(ROUTER_PAD)]
    x1, h2, logits = pl.pallas_call(
        functools.partial(_post_even_kernel, npt=npt, tps=tps),
        grid=(nt,),
        in_specs=[prompt_tile(d), sample_tile(d), tile(cw), halo_prev(cw), halo_next(cw),
                  prompt_tile(qw), sample_tile(qw), mod_spec,
                  whole((CONV_K, cw)), whole((1, cw)), whole((1, cw)), whole((1, cw)),
                  whole(w_out_e.shape[1:]), whole((1, d)), whole((d, ROUTER_PAD))],
        out_specs=moe_out_specs,
        out_shape=moe_outs,
        scratch_shapes=[pltpu.VMEM((TM + 2 * HALO, cw), F32),
                        pltpu.VMEM((SUBLANES, TM + 2 * HALO - SUBLANES, cw), F32)],
        compiler_params=_cparams(("parallel",)),
        name="post_even",
    )(xp2, xs2, u, u, u, o_p, o_s, mod[0], conv_a_w[0], conv_a_b[0][None, :], ln_a_g[0][None, :], ln_a_b[0][None, :],
      w_out_e[0].astype(BF16), g_norm2[0][None, :], router_w(0))
    dest3, wcol, yb = _moe(h2, logits, router_b(0), 0, w_gate, w_up, w_down)

    pw = pool_w.shape[2]
    x2, xc, p, bg = pl.pallas_call(
        functools.partial(_pre_odd_kernel, cw=cw),
        grid=(nt,),
        in_specs=[tile(d)] + combine_specs + [mod_spec, mod_spec, whole((1, d)), whole(w_in_o.shape[1:])],
        out_specs=[tile(d), tile(cw), tile(cw), tile(cw)],
        out_shape=[jax.ShapeDtypeStruct((n, d), F32)] + [jax.ShapeDtypeStruct((n, cw), F32)] * 3,
        scratch_shapes=combine_scratch,
        compiler_params=_cparams(("arbitrary",)),
        name="pre_odd",
    )(x1, dest3, dest3, wcol, yb, mod[0], mod[1], g_norm1[1][None, :], w_in_o[0].astype(BF16))
    pool_bd = jnp.zeros((cw, cw), F32)
    for g in range(len(POOL_WINDOWS)):
        pool_bd = pool_bd.at[g * pw:(g + 1) * pw, g * pw:(g + 1) * pw].set(pool_w[0, g])
    x3, h2, logits = pl.pallas_call(
        functools.partial(_post_odd_kernel, npt=npt, tps=tps),
        grid=(nt,),
        in_specs=[tile(d), tile(cw), halo_prev(cw), halo_next(cw), tile(cw), halo_prev(cw), halo_next(cw),
                  tile(cw), mod_spec, whole((cw, cw)), whole((1, cw)), whole((SCONV_K, cw)),
                  whole(w_out_o.shape[1:]), whole((1, d)), whole((d, ROUTER_PAD))],
        out_specs=moe_out_specs,
        out_shape=moe_outs,
        scratch_shapes=[pltpu.VMEM((TM + 2 * HALO, cw), F32), pltpu.VMEM((TM + 2 * HALO, cw), F32)],
        compiler_params=_cparams(("parallel",)),
        name="post_odd",
    )(x2, xc, xc, xc, p, p, p, bg, mod[1], pool_bd.astype(BF16), pool_scale[0][None, :], conv_d_w[0],
      w_out_o[0].astype(BF16), g_norm2[1][None, :], router_w(1))
    dest3, wcol, yb = _moe(h2, logits, router_b(1), 1, w_gate, w_up, w_down)

    out_p, out_s = pl.pallas_call(
        functools.partial(_final_kernel, npt=npt),
        grid=(nt,),
        in_specs=[tile(d)] + combine_specs + [mod_spec, whole((1, d))],
        out_specs=[pl.BlockSpec((TM, d), lambda i: (jnp.minimum(i, npt - 1), 0)),
                   pl.BlockSpec((TM, d), lambda i: (jnp.maximum(i - npt, 0), 0))],
        out_shape=[jax.ShapeDtypeStruct((n_p, d), F32), jax.ShapeDtypeStruct((n_s, d), F32)],
        scratch_shapes=combine_scratch,
        compiler_params=_cparams(("arbitrary",)),
        name="final_norm",
    )(x3, dest3, dest3, wcol, yb, mod[1], g_final[None, :])

    new_k = kc.reshape(bp, 1, tp, DIFF_HEADS, 2, dh)
    new_v = vc.reshape(bp, 1, tp, DIFF_HEADS, 2 * dh)
    return (out_p.reshape(bp, tp, d), out_s.reshape(bs, ts, d), new_k, new_v)
```

```python
import functools
import math

import jax
import jax.numpy as jnp
from jax import lax
from jax.experimental import pallas as pl
from jax.experimental.pallas import tpu as pltpu

F32 = jnp.float32
BF16 = jnp.bfloat16

EPS = 1e-6
GRID_W = 64
ROPE_BASE = 10000.0
CONV_K = 31
SCONV_K = 3
POOL_WINDOWS = (2, 4, 8, 16)
DIFF_HEADS = 4
N_GROUPS = 4
EXPERTS_PER_GROUP = 8
N_EXPERTS = N_GROUPS * EXPERTS_PER_GROUP
TOP_K = 2

LANES = 128
SUBLANES = 8
TM = 256
HALO = 16
MOE_BLK = 512
ROUTER_PAD = 128
ONES_ROWS = 16
DISPATCH_SLOTS = 3
VMEM_LIMIT = 56 * 1024 * 1024


def _cparams(sem):
    return pltpu.CompilerParams(dimension_semantics=sem, vmem_limit_bytes=VMEM_LIMIT)


def _rms_mod(x, g, sc, sh):
    ms = jnp.mean(x * x, axis=-1, keepdims=True)
    return (x * lax.rsqrt(ms + EPS)) * g * (1.0 + sc) + sh


def _silu(x):
    return x * jax.nn.sigmoid(x)


def _ada_kernel(c_ref, w_ref, b_ref, o_ref):
    cs = _silu(c_ref[...])
    o_ref[0] = jnp.dot(cs, w_ref[0], precision=lax.Precision.HIGHEST,
                       preferred_element_type=F32) + b_ref[0]


def _ada_table(cond, w_ada, b_ada):
    depth, d, six_d = w_ada.shape
    rows = cond.shape[0]
    nj = six_d // d
    out = pl.pallas_call(
        _ada_kernel,
        grid=(depth, nj),
        in_specs=[pl.BlockSpec((rows, d), lambda l, j: (0, 0)),
                  pl.BlockSpec((1, d, d), lambda l, j: (l, 0, j)),
                  pl.BlockSpec((1, 1, d), lambda l, j: (l, 0, j))],
        out_specs=pl.BlockSpec((1, rows, d), lambda l, j: (l, 0, j)),
        out_shape=jax.ShapeDtypeStruct((depth, rows, six_d), F32),
        compiler_params=_cparams(("parallel", "parallel")),
        name="ada_table",
    )(cond, w_ada, b_ada.reshape(depth, 1, six_d))
    return out.reshape(depth, rows, nj, d)


def _rope(x, cos, sins, first_half):
    outs = []
    for j in range(x.shape[1] // LANES):
        sl = slice(j * LANES, (j + 1) * LANES)
        xs = x[:, sl]
        nxt = pltpu.roll(xs, LANES - 16, 1)
        prv = pltpu.roll(xs, 16, 1)
        rot = jnp.where(first_half, nxt, prv)
        outs.append(xs * cos[:, sl] + rot * sins[:, sl])
    return jnp.concatenate(outs, axis=1)


def _pre_even_kernel(xp_ref, xs_ref, mod_ref, g_ref, w_ref, cos_ref, sin_ref,
                     u_ref, q_ref, k_ref, vt_ref, kc_ref, vc_ref, *, npt, cw, qw, qscale):
    i = pl.program_id(0)
    x = jnp.where(i < npt, xp_ref[...], xs_ref[...])
    h = _rms_mod(x, g_ref[...], mod_ref[0, 1:2, :], mod_ref[0, 0:1, :])
    proj = jnp.dot(h.astype(BF16), w_ref[...], preferred_element_type=F32)
    ga = proj[:, :cw]
    gb = proj[:, cw:2 * cw]
    u_ref[...] = ga * jax.nn.sigmoid(gb)
    q = proj[:, 2 * cw:2 * cw + qw]
    k = proj[:, 2 * cw + qw:2 * cw + 2 * qw]
    v = proj[:, 2 * cw + 2 * qw:]

    @pl.when(i < npt)
    def _():
        kc_ref[...] = k
        vc_ref[...] = v

    lane = lax.broadcasted_iota(jnp.int32, (1, LANES), 1)
    first_half = (lane % 32) < 16
    cos = cos_ref[...]
    sins = sin_ref[...]
    q_ref[...] = (_rope(q, cos, sins, first_half) * qscale).astype(BF16)
    k_ref[...] = _rope(k, cos, sins, first_half).astype(BF16)
    vt_ref[...] = v.T.astype(BF16)


def _attn_kernel(lam_ref, q_ref, k_ref, vt_ref, *rest, nk, tk, nk_ctx, post_scale):
    if nk_ctx:
        kc_ref, vtc_ref, g_ref, o_ref, s_sc, mc_sc, m_sc, acc_sc = rest
    else:
        g_ref, o_ref, s_sc, mc_sc, m_sc, acc_sc = rest
    q = q_ref[...]
    hw = q.shape[1]
    dh = hw // 2
    lane = lax.broadcasted_iota(jnp.int32, (1, hw), 1)
    zero = jnp.zeros_like(q)
    qs = (jnp.where(lane < dh, q, zero), jnp.where(lane >= dh, q, zero))
    m_sc[...] = jnp.full(m_sc.shape, -jnp.inf, F32)
    acc_sc[...] = jnp.zeros(acc_sc.shape, F32)
    ones_rows = (lax.broadcasted_iota(jnp.int32, (ONES_ROWS, tk), 0) == 0).astype(BF16)

    def scores(kk, slot):
        for c in range(2):
            s = lax.dot_general(kk, qs[c], (((1,), (1,)), ((), ())), preferred_element_type=F32)
            s_sc[slot, c] = s
            mc_sc[slot, c] = jnp.max(s, axis=0, keepdims=True)

    def consume(vt, slot):
        vta = jnp.concatenate([vt, ones_rows], axis=0)
        for c in range(2):
            m_prev = m_sc[c]
            m_new = jnp.maximum(m_prev, mc_sc[slot, c])
            alpha = jnp.exp2(m_prev - m_new)
            p = jnp.exp2(s_sc[slot, c] - m_new).astype(BF16)
            acc_sc[c] = alpha * acc_sc[c] + jnp.dot(vta, p, preferred_element_type=F32)
            m_sc[c] = m_new

    def k_at(i):
        if i < nk_ctx:
            return kc_ref[i * tk:(i + 1) * tk, :]
        return k_ref[(i - nk_ctx) * tk:(i - nk_ctx + 1) * tk, :]

    def vt_at(i):
        if i < nk_ctx:
            return vtc_ref[0, :, i * tk:(i + 1) * tk]
        return vt_ref[:, (i - nk_ctx) * tk:(i - nk_ctx + 1) * tk]

    n_total = nk_ctx + nk
    peel = nk_ctx + (n_total - 1 - nk_ctx) % 2
    scores(k_at(0), 0)
    for i in range(peel):
        scores(k_at(i + 1), (i + 1) % 2)
        consume(vt_at(i), i % 2)

    def body(t, carry):
        for r in range(2):
            j = 2 * t + (peel + r - nk_ctx)
            nxt = pl.multiple_of((j + 1) * tk, tk)
            cur = pl.multiple_of(j * tk, tk)
            scores(k_ref[pl.ds(nxt, tk), :], (peel + r + 1) % 2)
            consume(vt_ref[:, pl.ds(cur, tk)], (peel + r) % 2)
        return carry

    lax.fori_loop(0, (n_total - 1 - peel) // 2, body, 0)
    consume(vt_at(n_total - 1), (n_total - 1) % 2)
    o = (acc_sc[0, :hw, :] / acc_sc[0, hw:hw + 1, :]
         - lam_ref[0] * (acc_sc[1, :hw, :] / acc_sc[1, hw:hw + 1, :]))
    ms = jnp.mean(o * o, axis=0, keepdims=True)
    o = o * lax.rsqrt(ms + EPS) * g_ref[...] * post_scale
    o_ref[...] = o.T.astype(BF16)


def _attention(lam, q, k, vt, g_col, ctx, *, batch, t_len, row0, tq, tk, post_scale):
    hw = q.shape[1] // DIFF_HEADS
    nq = t_len // tq
    assert row0 % t_len == 0 and t_len % tk == 0
    qb0 = row0 // tq
    sb0 = row0 // t_len
    in_specs = [pl.BlockSpec(memory_space=pltpu.SMEM),
                pl.BlockSpec((tq, hw), lambda b, h, i: (qb0 + b * nq + i, h)),
                pl.BlockSpec((t_len, hw), lambda b, h, i: (sb0 + b, h)),
                pl.BlockSpec((hw, t_len), lambda b, h, i: (h, sb0 + b))]
    args = [lam, q, k, vt]
    nk_ctx = 0
    if ctx is not None:
        k_ctx, vt_ctx = ctx
        past = vt_ctx.shape[2]
        assert past % tk == 0
        nk_ctx = past // tk
        in_specs += [pl.BlockSpec((past, hw), lambda b, h, i: (b, h)),
                     pl.BlockSpec((1, hw, past), lambda b, h, i: (b, h, 0))]
        args += [k_ctx, vt_ctx]
    in_specs.append(pl.BlockSpec((hw, 1), lambda b, h, i: (0, 0)))
    args.append(g_col)
    return pl.pallas_call(
        functools.partial(_attn_kernel, nk=t_len // tk, tk=tk, nk_ctx=nk_ctx, post_scale=post_scale),
        grid=(batch, DIFF_HEADS, nq),
        in_specs=in_specs,
        out_specs=pl.BlockSpec((tq, hw), lambda b, h, i: (b * nq + i, h)),
        out_shape=jax.ShapeDtypeStruct((batch * t_len, q.shape[1]), BF16),
        scratch_shapes=[pltpu.VMEM((2, 2, tk, tq), F32), pltpu.VMEM((2, 2, 1, tq), F32),
                        pltpu.VMEM((2, 1, tq), F32), pltpu.VMEM((2, hw + ONES_ROWS, tq), F32)],
        compiler_params=_cparams(("parallel", "parallel", "parallel")),
        name="diff_attention",
    )(*args)


def _seq_edges(i, npt, tps):
    r = (i - npt) % tps
    is_prompt = i < npt
    first = jnp.logical_or(is_prompt, r == 0)
    last = jnp.logical_or(is_prompt, r == tps - 1)
    return first, last


def _fill_ext(ext_sc, cur_ref, prev_ref, next_ref, first, last):
    prev = prev_ref[...]
    nxt = next_ref[...]
    ext_sc[0:HALO, :] = jnp.where(first, jnp.zeros_like(prev), prev)
    ext_sc[HALO:HALO + TM, :] = cur_ref[...]
    ext_sc[HALO + TM:, :] = jnp.where(last, jnp.zeros_like(nxt), nxt)


def _moe_prologue(x_new, mod_ref, g2_ref, wr_ref, h2_ref, lg_ref):
    h2 = _rms_mod(x_new, g2_ref[...], mod_ref[0, 4:5, :], mod_ref[0, 3:4, :])
    h2_ref[...] = h2
    lg_ref[...] = jnp.dot(h2.astype(BF16), wr_ref[...], preferred_element_type=F32)


def _post_even_kernel(xp_ref, xs_ref, u_ref, up_ref, un_ref, op_ref, os_ref, mod_ref, cw_ref, cb_ref,
                      lng_ref, lnb_ref, wo_ref, g2_ref, wr_ref, x1_ref, h2_ref, lg_ref, ext_sc, sh_sc, *, npt, tps):
    i = pl.program_id(0)
    first, last = _seq_edges(i, npt, tps)
    _fill_ext(ext_sc, u_ref, up_ref, un_ref, first, last)
    cw = u_ref.shape[1]
    span = sh_sc.shape[1]
    for r in range(SUBLANES):
        sh_sc[r] = ext_sc[r:r + span, :]
    rows = 32
    chunks = []
    for rc in range(TM // rows):
        acc = None
        for k in range(CONV_K):
            off = HALO - CONV_K // 2 + k
            start = rc * rows + off - off % SUBLANES
            term = sh_sc[off % SUBLANES, start:start + rows, :] * cw_ref[k:k + 1, :]
            acc = term if acc is None else acc + term
        chunks.append(acc)
    conv = jnp.concatenate(chunks, axis=0) + cb_ref[...]
    mu = jnp.mean(conv, axis=-1, keepdims=True)
    cen = conv - mu
    var = jnp.mean(cen * cen, axis=-1, keepdims=True)
    uu = _silu(cen * lax.rsqrt(var + EPS) * lng_ref[...] + lnb_ref[...])
    is_prompt = i < npt
    o = jnp.where(is_prompt, op_ref[...], os_ref[...])
    m = (jnp.dot(uu.astype(BF16), wo_ref[0:cw, :], preferred_element_type=F32)
         + jnp.dot(o, wo_ref[cw:, :], preferred_element_type=F32))
    x1 = jnp.where(is_prompt, xp_ref[...], xs_ref[...]) + mod_ref[0, 2:3, :] * m
    x1_ref[...] = x1
    _moe_prologue(x1, mod_ref, g2_ref, wr_ref, h2_ref, lg_ref)


def _row_copy(src_hbm, src_row, dst, dst_row, sem):
    return pltpu.make_async_copy(src_hbm.at[pl.ds(src_row, 1)], dst.at[pl.ds(dst_row, 1)], sem)


def _gather_start(dest_ref, yb_hbm, ybuf, sem, slot):
    def body(n, carry):
        for k in range(TOP_K):
            _row_copy(yb_hbm, dest_ref[0, 0, k * TM + n], ybuf.at[slot, k], n, sem.at[slot]).start(priority=k % 2)
        return carry

    lax.fori_loop(0, TM, body, 0, unroll=8)


def _moe_combine(dcur_ref, dnxt_ref, cw_ref, yb_hbm, ybuf, sem):
    t = pl.program_id(0)
    slot = t % 2

    @pl.when(t == 0)
    def _():
        _gather_start(dcur_ref, yb_hbm, ybuf, sem, 0)

    @pl.when(t + 1 < pl.num_programs(0))
    def _():
        _gather_start(dnxt_ref, yb_hbm, ybuf, sem, 1 - slot)

    for k in range(TOP_K):
        pltpu.make_async_copy(yb_hbm.at[pl.ds(0, TM)], ybuf.at[slot, k], sem.at[slot]).wait()
    w = cw_ref[...]
    y = w[:, 0:1] * ybuf[slot, 0]
    for k in range(1, TOP_K):
        y = y + w[:, k:k + 1] * ybuf[slot, k]
    return y


def _dispatch_kernel(dest_ref, pend_ref, h2_hbm, xb_hbm, stage, zeros, in_sem, row_sem, zero_sem):
    t = pl.program_id(0)
    nt = pl.num_programs(0)
    slot = t % DISPATCH_SLOTS

    def stage_in(tile_idx, s):
        return pltpu.make_async_copy(h2_hbm.at[pl.ds(tile_idx * TM, TM)], stage.at[s], in_sem.at[s])

    def drain_rows(s):
        for _ in range(TOP_K):
            pltpu.make_async_copy(stage.at[s], xb_hbm.at[pl.ds(0, TM)], row_sem.at[s]).wait()

    def zero_block(start):
        return pltpu.make_async_copy(zeros, xb_hbm.at[pl.ds(pl.multiple_of(start, MOE_BLK), MOE_BLK)], zero_sem)

    def zero_fill(op):
        for e in range(N_EXPERTS):
            op(zero_block(jnp.maximum(pend_ref[e] - MOE_BLK, 0)))
        for j in range(N_EXPERTS):
            start = pend_ref[N_EXPERTS - 1] + j * MOE_BLK

            @pl.when(start < xb_hbm.shape[0])
            def _():
                op(zero_block(start))

    @pl.when(t == 0)
    def _():
        stage_in(0, 0).start()
        zeros[...] = jnp.zeros(zeros.shape, zeros.dtype)
        zero_fill(lambda cp: cp.start())
        zero_fill(lambda cp: cp.wait())

    @pl.when(jnp.logical_and(t == 0, nt > 1))
    def _():
        stage_in(1, 1).start()

    stage_in(t, slot).wait()

    def body(n, carry):
        for k in range(TOP_K):
            _row_copy(stage.at[slot], n, xb_hbm, dest_ref[0, 0, k * TM + n], row_sem.at[slot]).start(priority=k % 2)
        return carry

    lax.fori_loop(0, TM, body, 0, unroll=8)

    @pl.when(t > 0)
    def _():
        drain_rows((t + DISPATCH_SLOTS - 1) % DISPATCH_SLOTS)

    @pl.when(t + 2 < nt)
    def _():
        stage_in(t + 2, (t + 2) % DISPATCH_SLOTS).start()

    @pl.when(t == nt - 1)
    def _():
        drain_rows(slot)


def _dispatch(dest3, pend, h2, p_len):
    n, d = h2.shape
    return pl.pallas_call(
        _dispatch_kernel,
        grid=(n // TM,),
        in_specs=[pl.BlockSpec((1, 1, TOP_K * TM), lambda t: (t, 0, 0), memory_space=pltpu.SMEM),
                  pl.BlockSpec(memory_space=pltpu.SMEM), pl.BlockSpec(memory_space=pl.ANY)],
        out_specs=pl.BlockSpec(memory_space=pl.ANY),
        out_shape=jax.ShapeDtypeStruct((p_len, d), h2.dtype),
        scratch_shapes=[pltpu.VMEM((DISPATCH_SLOTS, TM, d), h2.dtype), pltpu.VMEM((MOE_BLK, d), h2.dtype),
                        pltpu.SemaphoreType.DMA((DISPATCH_SLOTS,)), pltpu.SemaphoreType.DMA((DISPATCH_SLOTS,)),
                        pltpu.SemaphoreType.DMA(())],
        compiler_params=_cparams(("arbitrary",)),
        name="moe_dispatch",
    )(dest3, pend, h2)


def _pre_odd_kernel(x_ref, dcur_ref, dnxt_ref, cw_ref, yb_hbm, modp_ref, mod_ref, g_ref, w_ref,
                    x2_ref, xc_ref, p_ref, bg_ref, ybuf, sem, *, cw):
    y = _moe_combine(dcur_ref, dnxt_ref, cw_ref, yb_hbm, ybuf, sem)
    x2 = x_ref[...] + modp_ref[0, 5:6, :] * y
    x2_ref[...] = x2
    h = _rms_mod(x2, g_ref[...], mod_ref[0, 1:2, :], mod_ref[0, 0:1, :])
    proj = jnp.dot(h.astype(BF16), w_ref[...], preferred_element_type=F32)
    xc_ref[...] = proj[:, :cw]
    hd = proj[:, cw:2 * cw]
    bg_ref[...] = proj[:, 2 * cw:3 * cw]
    p_ref[...] = proj[:, 3 * cw:] * hd


def _post_odd_kernel(x_ref, xc_ref, xcp_ref, xcn_ref, p_ref, pp_ref, pn_ref, bg_ref, mod_ref,
                     pw_ref, ps_ref, dw_ref, wo_ref, g2_ref, wr_ref,
                     x3_ref, h2_ref, lg_ref, extc_sc, extp_sc, *, npt, tps):
    i = pl.program_id(0)
    first, last = _seq_edges(i, npt, tps)
    _fill_ext(extc_sc, xc_ref, xcp_ref, xcn_ref, first, last)
    _fill_ext(extp_sc, p_ref, pp_ref, pn_ref, first, last)
    cw = xc_ref.shape[1]
    gc = cw // len(POOL_WINDOWS)
    is_prompt = i < npt
    t_len = jnp.where(is_prompt, TM, TM * tps)
    pos = lax.broadcasted_iota(jnp.int32, (TM, 1), 0) + jnp.where(is_prompt, 0, ((i - npt) % tps) * TM)
    ds = []
    for g, w in enumerate(POOL_WINDOWS):
        sl = slice(g * gc, (g + 1) * gc)
        acc = None
        for o in range(-(w // 2), w // 2):
            term = extc_sc[HALO + o:HALO + o + TM, sl]
            acc = term if acc is None else acc + term
        lo = jnp.maximum(pos - w // 2, 0)
        hi = jnp.minimum(pos + w // 2 - 1, t_len - 1)
        cnt = (hi - lo + 1).astype(F32)
        ds.append(acc / cnt - extc_sc[HALO:HALO + TM, sl])
    d = jnp.concatenate(ds, axis=1).astype(BF16)
    yc = jnp.dot(d, pw_ref[...], preferred_element_type=F32) * ps_ref[...]
    conv = None
    for k in range(SCONV_K):
        start = HALO - SCONV_K // 2 + k
        term = extp_sc[start:start + TM, :] * dw_ref[k:k + 1, :]
        conv = term if conv is None else conv + term
    yd = bg_ref[...] * conv
    m = (jnp.dot(yc.astype(BF16), wo_ref[0:cw, :], preferred_element_type=F32)
         + jnp.dot(yd.astype(BF16), wo_ref[cw:, :], preferred_element_type=F32))
    x3 = x_ref[...] + mod_ref[0, 2:3, :] * m
    x3_ref[...] = x3
    _moe_prologue(x3, mod_ref, g2_ref, wr_ref, h2_ref, lg_ref)


def _final_kernel(x_ref, dcur_ref, dnxt_ref, cw_ref, yb_hbm, modp_ref, g_ref, op_ref, os_ref, ybuf, sem, *, npt):
    i = pl.program_id(0)
    y = _moe_combine(dcur_ref, dnxt_ref, cw_ref, yb_hbm, ybuf, sem)
    x = x_ref[...] + modp_ref[0, 5:6, :] * y
    ms = jnp.mean(x * x, axis=-1, keepdims=True)
    out = x * lax.rsqrt(ms + EPS) * g_ref[...]

    @pl.when(i < npt)
    def _():
        op_ref[...] = out

    @pl.when(i >= npt)
    def _():
        os_ref[...] = out


def _expert_kernel(be_ref, nv_ref, x_ref, wg_ref, wu_ref, wd_ref, y_ref, wg_sc, wu_sc, wd_sc):
    i = pl.program_id(0)

    @pl.when(jnp.logical_or(i == 0, be_ref[i] != be_ref[jnp.maximum(i - 1, 0)]))
    def _():
        wg_sc[...] = wg_ref[0, 0].astype(BF16)
        wu_sc[...] = wu_ref[0, 0].astype(BF16)
        wd_sc[...] = wd_ref[0, 0].astype(BF16)

    @pl.when(i < nv_ref[0])
    def _():
        x = x_ref[...].astype(BF16)
        hg = jnp.dot(x, wg_sc[...], preferred_element_type=F32)
        hu = jnp.dot(x, wu_sc[...], preferred_element_type=F32)
        hid = (_silu(hg) * hu).astype(BF16)
        y_ref[...] = jnp.dot(hid, wd_sc[...], preferred_element_type=F32)

    @pl.when(i >= nv_ref[0])
    def _():
        y_ref[...] = jnp.zeros(y_ref.shape, F32)


def _experts(blk_e, nvalid, xb, layer, wg, wu, wd):
    p_len, d = xb.shape
    hid = wg.shape[3]
    nblk = p_len // MOE_BLK
    grid_spec = pltpu.PrefetchScalarGridSpec(
        num_scalar_prefetch=2,
        grid=(nblk,),
        in_specs=[pl.BlockSpec((MOE_BLK, d), lambda i, be, nv: (jnp.minimum(i, nv[0] - 1), 0)),
                  pl.BlockSpec((1, 1, d, hid), lambda i, be, nv: (layer, be[i], 0, 0)),
                  pl.BlockSpec((1, 1, d, hid), lambda i, be, nv: (layer, be[i], 0, 0)),
                  pl.BlockSpec((1, 1, hid, d), lambda i, be, nv: (layer, be[i], 0, 0))],
        out_specs=pl.BlockSpec((MOE_BLK, d), lambda i, be, nv: (i, 0)),
        scratch_shapes=[pltpu.VMEM((d, hid), BF16), pltpu.VMEM((d, hid), BF16), pltpu.VMEM((hid, d), BF16)],
    )
    return pl.pallas_call(
        _expert_kernel,
        grid_spec=grid_spec,
        out_shape=jax.ShapeDtypeStruct((p_len, d), F32),
        compiler_params=_cparams(("arbitrary",)),
        name="moe_experts",
    )(blk_e, nvalid, xb, wg, wu, wd)


def _route_kernel(lg_ref, b_ref, info_ref, cnt_ref):
    lt = lg_ref[...].T + b_ref[...]
    gl = lt[N_EXPERTS:N_EXPERTS + N_GROUPS]
    ge = jnp.exp(gl - jnp.max(gl, axis=0, keepdims=True))
    gp = ge / jnp.sum(ge, axis=0, keepdims=True)
    best = gp[0:1]
    gidx = jnp.zeros(best.shape, jnp.int32)
    el = lt[0:EXPERTS_PER_GROUP]
    for g in range(1, N_GROUPS):
        better = gp[g:g + 1] > best
        gidx = jnp.where(better, g, gidx)
        best = jnp.where(better, gp[g:g + 1], best)
    for g in range(1, N_GROUPS):
        el = jnp.where(gidx == g, lt[g * EXPERTS_PER_GROUP:(g + 1) * EXPERTS_PER_GROUP], el)
    rio = lax.broadcasted_iota(jnp.int32, el.shape, 0)
    v0 = jnp.max(el, axis=0, keepdims=True)
    i0 = jnp.min(jnp.where(el == v0, rio, EXPERTS_PER_GROUP), axis=0, keepdims=True)
    el2 = jnp.where(rio == i0, -jnp.inf, el)
    v1 = jnp.max(el2, axis=0, keepdims=True)
    i1 = jnp.min(jnp.where(el2 == v1, rio, EXPERTS_PER_GROUP), axis=0, keepdims=True)
    e1 = jnp.exp(v1 - v0)
    den = 1.0 + e1
    w0 = (1.0 / den) * best
    w1 = (e1 / den) * best
    eid0 = gidx * EXPERTS_PER_GROUP + i0
    eid1 = gidx * EXPERTS_PER_GROUP + i1
    tm = lt.shape[1]
    eio = lax.broadcasted_iota(jnp.int32, (N_EXPERTS, tm), 0)
    oh0 = eio == eid0
    oh1 = eio == eid1
    ohs = jnp.where(oh0, 1.0, 0.0) + jnp.where(oh1, 1.0, 0.0)
    upper = (lax.broadcasted_iota(jnp.int32, (tm, tm), 0)
             < lax.broadcasted_iota(jnp.int32, (tm, tm), 1)).astype(BF16)
    before = jnp.dot(ohs.astype(BF16), upper, preferred_element_type=F32)
    rank0 = jnp.sum(jnp.where(oh0, before, 0.0), axis=0, keepdims=True)
    rank1 = jnp.sum(jnp.where(oh1, before, 0.0), axis=0, keepdims=True)
    info_ref[0] = jnp.concatenate([eid0.astype(F32), eid1.astype(F32), w0, w1, rank0, rank1,
                                   jnp.zeros((2, tm), F32)], axis=0)
    cnt_ref[0] = jnp.sum(ohs, axis=1, keepdims=True)


def _route(logits, bias_col):
    n = logits.shape[0]
    nt = n // TM
    return pl.pallas_call(
        _route_kernel,
        grid=(nt,),
        in_specs=[pl.BlockSpec((TM, ROUTER_PAD), lambda i: (i, 0)),
                  pl.BlockSpec((ROUTER_PAD, 1), lambda i: (0, 0))],
        out_specs=[pl.BlockSpec((1, 8, TM), lambda i: (i, 0, 0)),
                   pl.BlockSpec((1, N_EXPERTS, 1), lambda i: (i, 0, 0))],
        out_shape=[jax.ShapeDtypeStruct((nt, 8, TM), F32), jax.ShapeDtypeStruct((nt, N_EXPERTS, 1), F32)],
        compiler_params=_cparams(("parallel",)),
        name="moe_route",
    )(logits, bias_col)


def _moe(h2, logits, bias_col, layer, wg, wu, wd):
    n, d = h2.shape
    info, cnt = _route(logits, bias_col)
    cnt = cnt[:, :, 0].astype(jnp.int32)
    counts = jnp.sum(cnt, axis=0)
    pc = ((counts + MOE_BLK - 1) // MOE_BLK) * MOE_BLK
    pend = jnp.cumsum(pc)
    base = (pend - pc)[None, :] + jnp.cumsum(cnt, axis=0) - cnt
    eid = info[:, 0:2, :].astype(jnp.int32)
    rank = info[:, 4:6, :].astype(jnp.int32)
    sel = eid[..., None] == jnp.arange(N_EXPERTS, dtype=jnp.int32)
    dest = jnp.sum(jnp.where(sel, base[:, None, None, :], 0), axis=-1) + rank
    p_len = n * TOP_K + N_EXPERTS * MOE_BLK
    nblk = p_len // MOE_BLK
    blk_start = jnp.arange(nblk, dtype=jnp.int32) * MOE_BLK
    blk_e = jnp.minimum(jnp.sum(pend[None, :] <= blk_start[:, None], axis=1), N_EXPERTS - 1).astype(jnp.int32)
    nvalid = (pend[-1:] // MOE_BLK).astype(jnp.int32)
    dest3 = dest.reshape(dest.shape[0], 1, TOP_K * TM)
    wcol = jnp.transpose(info[:, 2:2 + TOP_K, :], (0, 2, 1)).reshape(n, TOP_K)
    xb = _dispatch(dest3, pend.astype(jnp.int32), h2, p_len)
    yb = _experts(blk_e, nvalid, xb, layer, wg, wu, wd)
    return dest3, wcol, yb


def _rope_tables(t_len, qw):
    rows = t_len // GRID_W
    ax = qw // (DIFF_HEADS * 2) // 2
    row = jnp.repeat(jnp.arange(rows, dtype=F32), GRID_W)
    col = jnp.tile(jnp.arange(GRID_W, dtype=F32), rows)
    inv = ROPE_BASE ** (-jnp.arange(0, ax, 2, dtype=F32) / ax)
    ar = row[:, None] * inv[None, :]
    ac = col[:, None] * inv[None, :]
    cr, sr, cc, sc = jnp.cos(ar), jnp.sin(ar), jnp.cos(ac), jnp.sin(ac)
    cos = jnp.concatenate([cr, cr, cc, cc], axis=1)
    sins = jnp.concatenate([-sr, sr, -sc, sc], axis=1)
    reps = qw // cos.shape[1]
    cos = jnp.concatenate([jnp.ones((TM, cos.shape[1]), F32), cos], axis=0)
    sins = jnp.concatenate([jnp.zeros((TM, sins.shape[1]), F32), sins], axis=0)
    return jnp.tile(cos, (1, reps)), jnp.tile(sins, (1, reps))


def kernel(x_prompt, x_sample, cache_k, cache_v, c, c_ctx, w_ada, b_ada, g_norm1, g_norm2, g_final, w_in_e, conv_a_w, conv_a_b, ln_a_g, ln_a_b, lam_q1, lam_k1, lam_q2, lam_k2, subln_g, w_out_e, w_in_o, pool_w, pool_scale, conv_d_w, w_out_o, w_router_g, b_router_g, w_router_e, b_router_e, w_gate, w_up, w_down):
    bp, tp, d = x_prompt.shape
    bs, ts, _ = x_sample.shape
    past = cache_k.shape[2]
    assert tp == TM and ts % TM == 0 and d % LANES == 0
    npt = bp
    tps = ts // TM
    n_p, n_s = bp * tp, bs * ts
    n = n_p + n_s
    nt = n // TM
    cw = conv_a_w.shape[2]
    qw = (w_in_e.shape[2] - 2 * cw) // 3
    dh = qw // (DIFF_HEADS * 2)
    hpt = TM // HALO

    def cond_of(i):
        return jnp.where(i < npt, 0, 1 + (i - npt) // tps)

    def tile(w):
        return pl.BlockSpec((TM, w), lambda i: (i, 0))

    def halo_prev(w):
        return pl.BlockSpec((HALO, w), lambda i: (jnp.maximum(i * hpt - 1, 0), 0))

    def halo_next(w):
        return pl.BlockSpec((HALO, w), lambda i: (jnp.minimum((i + 1) * hpt, nt * hpt - 1), 0))

    def whole(shape):
        return pl.BlockSpec(shape, lambda i: (0,) * len(shape))

    mod_spec = pl.BlockSpec((1, 6, d), lambda i: (cond_of(i), 0, 0))
    combine_specs = [pl.BlockSpec((1, 1, TOP_K * TM), lambda i: (i, 0, 0), memory_space=pltpu.SMEM),
                     pl.BlockSpec((1, 1, TOP_K * TM), lambda i: (jnp.minimum(i + 1, nt - 1), 0, 0),
                                  memory_space=pltpu.SMEM),
                     tile(TOP_K), pl.BlockSpec(memory_space=pl.ANY)]
    combine_scratch = [pltpu.VMEM((2, TOP_K, TM, d), F32), pltpu.SemaphoreType.DMA((2,))]

    rows = 8 * ((1 + bs + 7) // 8)
    cond = jnp.concatenate([c_ctx[None, :], c, jnp.zeros((rows - 1 - bs, d), F32)], axis=0)
    mod = _ada_table(cond, w_ada, b_ada)

    xp2, xs2 = x_prompt.reshape(n_p, d), x_sample.reshape(n_s, d)

    def prompt_tile(w):
        return pl.BlockSpec((TM, w), lambda i: (jnp.minimum(i, npt - 1), 0))

    def sample_tile(w):
        return pl.BlockSpec((TM, w), lambda i: (jnp.maximum(i - npt, 0), 0))

    def router_w(i):
        wre = jnp.transpose(w_router_e[i], (1, 0, 2)).reshape(d, N_EXPERTS)
        wr = jnp.concatenate([wre, w_router_g[i]], axis=1)
        return jnp.pad(wr, ((0, 0), (0, ROUTER_PAD - wr.shape[1]))).astype(BF16)

    def router_b(i):
        b = jnp.concatenate([b_router_e[i].reshape(N_EXPERTS), b_router_g[i]])
        return jnp.pad(b, (0, ROUTER_PAD - b.shape[0]))[:, None]

    cos, sins = _rope_tables(ts, qw)
    tab_spec = pl.BlockSpec((TM, qw), lambda i: (jnp.where(i < npt, 0, 1 + (i - npt) % tps), 0))
    cache_spec = pl.BlockSpec((TM, qw), lambda i: (jnp.minimum(i, npt - 1), 0))
    u, q, k, vt, kc, vc = pl.pallas_call(
        functools.partial(_pre_even_kernel, npt=npt, cw=cw, qw=qw, qscale=dh ** -0.5 * math.log2(math.e)),
        grid=(nt,),
        in_specs=[prompt_tile(d), sample_tile(d), mod_spec, whole((1, d)), whole(w_in_e.shape[1:]),
                  tab_spec, tab_spec],
        out_specs=[tile(cw), tile(qw), tile(qw), pl.BlockSpec((qw, TM), lambda i: (0, i)),
                   cache_spec, cache_spec],
        out_shape=[jax.ShapeDtypeStruct((n, cw), F32), jax.ShapeDtypeStruct((n, qw), BF16),
                   jax.ShapeDtypeStruct((n, qw), BF16), jax.ShapeDtypeStruct((qw, n), BF16),
                   jax.ShapeDtypeStruct((n_p, qw), F32), jax.ShapeDtypeStruct((n_p, qw), F32)],
        compiler_params=_cparams(("arbitrary",)),
        name="pre_even",
    )(xp2, xs2, mod[0], g_norm1[0][None, :], w_in_e[0].astype(BF16), cos, sins)

    lam_init = 0.8 - 0.6 * math.exp(-0.3 * 0)
    lam = (jnp.exp(jnp.sum(lam_q1[0] * lam_k1[0])) - jnp.exp(jnp.sum(lam_q2[0] * lam_k2[0])) + lam_init)
    lam = lam.reshape(1).astype(F32)
    sub_g = subln_g[0][:, None]
    o_p = _attention(lam, q, k, vt, sub_g, None, batch=bp, t_len=tp, row0=0, tq=TM, tk=TM,
                     post_scale=1.0 - lam_init)
    k_ctx = cache_k[:, 0].reshape(bs * past, qw).astype(BF16)
    vt_ctx = jnp.transpose(cache_v[:, 0].reshape(bs, past, qw), (0, 2, 1)).astype(BF16)
    o_s = _attention(lam, q, k, vt, sub_g, (k_ctx, vt_ctx), batch=bs, t_len=ts, row0=n_p, tq=1024, tk=512,
                     post_scale=1.0 - lam_init)

    moe_outs = [jax.ShapeDtypeStruct((n, d), F32), jax.ShapeDtypeStruct((n, d), F32),
                jax.ShapeDtypeStruct((n, ROUTER_PAD), F32)]
    moe_out_specs = [tile(d), tile(d), tile(ROUTER_PAD)]
    x1, h2, logits = pl.pallas_call(
        functools.partial(_post_even_kernel, npt=npt, tps=tps),
        grid=(nt,),
        in_specs=[prompt_tile(d), sample_tile(d), tile(cw), halo_prev(cw), halo_next(cw),
                  prompt_tile(qw), sample_tile(qw), mod_spec,
                  whole((CONV_K, cw)), whole((1, cw)), whole((1, cw)), whole((1, cw)),
                  whole(w_out_e.shape[1:]), whole((1, d)), whole((d, ROUTER_PAD))],
        out_specs=moe_out_specs,
        out_shape=moe_outs,
        scratch_shapes=[pltpu.VMEM((TM + 2 * HALO, cw), F32),
                        pltpu.VMEM((SUBLANES, TM + 2 * HALO - SUBLANES, cw), F32)],
        compiler_params=_cparams(("parallel",)),
        name="post_even",
    )(xp2, xs2, u, u, u, o_p, o_s, mod[0], conv_a_w[0], conv_a_b[0][None, :], ln_a_g[0][None, :], ln_a_b[0][None, :],
      w_out_e[0].astype(BF16), g_norm2[0][None, :], router_w(0))
    dest3, wcol, yb = _moe(h2, logits, router_b(0), 0, w_gate, w_up, w_down)

    pw = pool_w.shape[2]
    x2, xc, p, bg = pl.pallas_call(
        functools.partial(_pre_odd_kernel, cw=cw),
        grid=(nt,),
        in_specs=[tile(d)] + combine_specs + [mod_spec, mod_spec, whole((1, d)), whole(w_in_o.shape[1:])],
        out_specs=[tile(d), tile(cw), tile(cw), tile(cw)],
        out_shape=[jax.ShapeDtypeStruct((n, d), F32)] + [jax.ShapeDtypeStruct((n, cw), F32)] * 3,
        scratch_shapes=combine_scratch,
        compiler_params=_cparams(("arbitrary",)),
        name="pre_odd",
    )(x1, dest3, dest3, wcol, yb, mod[0], mod[1], g_norm1[1][None, :], w_in_o[0].astype(BF16))
    pool_bd = jnp.zeros((cw, cw), F32)
    for g in range(len(POOL_WINDOWS)):
        pool_bd = pool_bd.at[g * pw:(g + 1) * pw, g * pw:(g + 1) * pw].set(pool_w[0, g])
    x3, h2, logits = pl.pallas_call(
        functools.partial(_post_odd_kernel, npt=npt, tps=tps),
        grid=(nt,),
        in_specs=[tile(d), tile(cw), halo_prev(cw), halo_next(cw), tile(cw), halo_prev(cw), halo_next(cw),
                  tile(cw), mod_spec, whole((cw, cw)), whole((1, cw)), whole((SCONV_K, cw)),
                  whole(w_out_o.shape[1:]), whole((1, d)), whole((d, ROUTER_PAD))],
        out_specs=moe_out_specs,
        out_shape=moe_outs,
        scratch_shapes=[pltpu.VMEM((TM + 2 * HALO, cw), F32), pltpu.VMEM((TM + 2 * HALO, cw), F32)],
        compiler_params=_cparams(("parallel",)),
        name="post_odd",
    )(x2, xc, xc, xc, p, p, p, bg, mod[1], pool_bd.astype(BF16), pool_scale[0][None, :], conv_d_w[0],
      w_out_o[0].astype(BF16), g_norm2[1][None, :], router_w(1))
    dest3, wcol, yb = _moe(h2, logits, router_b(1), 1, w_gate, w_up, w_down)

    out_p, out_s = pl.pallas_call(
        functools.partial(_final_kernel, npt=npt),
        grid=(nt,),
        in_specs=[tile(d)] + combine_specs + [mod_spec, whole((1, d))],
        out_specs=[pl.BlockSpec((TM, d), lambda i: (jnp.minimum(i, npt - 1), 0)),
                   pl.BlockSpec((TM, d), lambda i: (jnp.maximum(i - npt, 0), 0))],
        out_shape=[jax.ShapeDtypeStruct((n_p, d), F32), jax.ShapeDtypeStruct((n_s, d), F32)],
        scratch_shapes=combine_scratch,
        compiler_params=_cparams(("arbitrary",)),
        name="final_norm",
    )(x3, dest3, dest3, wcol, yb, mod[1], g_final[None, :])

    new_k = kc.reshape(bp, 1, tp, DIFF_HEADS, 2, dh)
    new_v = vc.reshape(bp, 1, tp, DIFF_HEADS, 2 * dh)
    return (out_p.reshape(bp, tp, d), out_s.reshape(bs, ts, d), new_k, new_v)
```

```python
import functools
import math

import jax
import jax.numpy as jnp
from jax import lax
from jax.experimental import pallas as pl
from jax.experimental.pallas import tpu as pltpu

F32 = jnp.float32
BF16 = jnp.bfloat16

EPS = 1e-6
GRID_W = 64
ROPE_BASE = 10000.0
CONV_K = 31
SCONV_K = 3
POOL_WINDOWS = (2, 4, 8, 16)
DIFF_HEADS = 4
N_GROUPS = 4
EXPERTS_PER_GROUP = 8
N_EXPERTS = N_GROUPS * EXPERTS_PER_GROUP
TOP_K = 2

LANES = 128
SUBLANES = 8
TM = 256
HALO = 16
MOE_BLK = 512
ROUTER_PAD = 128
ONES_ROWS = 16
DISPATCH_SLOTS = 3
VMEM_LIMIT = 56 * 1024 * 1024


def _cparams(sem):
    return pltpu.CompilerParams(dimension_semantics=sem, vmem_limit_bytes=VMEM_LIMIT)


def _rms_mod(x, g, sc, sh):
    ms = jnp.mean(x * x, axis=-1, keepdims=True)
    return (x * lax.rsqrt(ms + EPS)) * g * (1.0 + sc) + sh


def _silu(x):
    return x * jax.nn.sigmoid(x)


def _ada_kernel(c_ref, w_ref, b_ref, o_ref):
    cs = _silu(c_ref[...])
    o_ref[0] = jnp.dot(cs, w_ref[0], precision=lax.Precision.HIGHEST,
                       preferred_element_type=F32) + b_ref[0]


def _ada_table(cond, w_ada, b_ada):
    depth, d, six_d = w_ada.shape
    rows = cond.shape[0]
    nj = six_d // d
    out = pl.pallas_call(
        _ada_kernel,
        grid=(depth, nj),
        in_specs=[pl.BlockSpec((rows, d), lambda l, j: (0, 0)),
                  pl.BlockSpec((1, d, d), lambda l, j: (l, 0, j)),
                  pl.BlockSpec((1, 1, d), lambda l, j: (l, 0, j))],
        out_specs=pl.BlockSpec((1, rows, d), lambda l, j: (l, 0, j)),
        out_shape=jax.ShapeDtypeStruct((depth, rows, six_d), F32),
        compiler_params=_cparams(("parallel", "parallel")),
        name="ada_table",
    )(cond, w_ada, b_ada.reshape(depth, 1, six_d))
    return out.reshape(depth, rows, nj, d)


def _rope(x, cos, sins, first_half):
    outs = []
    for j in range(x.shape[1] // LANES):
        sl = slice(j * LANES, (j + 1) * LANES)
        xs = x[:, sl]
        nxt = pltpu.roll(xs, LANES - 16, 1)
        prv = pltpu.roll(xs, 16, 1)
        rot = jnp.where(first_half, nxt, prv)
        outs.append(xs * cos[:, sl] + rot * sins[:, sl])
    return jnp.concatenate(outs, axis=1)


def _pre_even_kernel(xp_ref, xs_ref, mod_ref, g_ref, w_ref, cos_ref, sin_ref,
                     u_ref, q_ref, k_ref, vt_ref, kc_ref, vc_ref, *, npt, cw, qw, qscale):
    i = pl.program_id(0)
    x = jnp.where(i < npt, xp_ref[...], xs_ref[...])
    h = _rms_mod(x, g_ref[...], mod_ref[0, 1:2, :], mod_ref[0, 0:1, :])
    proj = jnp.dot(h.astype(BF16), w_ref[...], preferred_element_type=F32)
    ga = proj[:, :cw]
    gb = proj[:, cw:2 * cw]
    u_ref[...] = ga * jax.nn.sigmoid(gb)
    q = proj[:, 2 * cw:2 * cw + qw]
    k = proj[:, 2 * cw + qw:2 * cw + 2 * qw]
    v = proj[:, 2 * cw + 2 * qw:]

    @pl.when(i < npt)
    def _():
        kc_ref[...] = k
        vc_ref[...] = v

    lane = lax.broadcasted_iota(jnp.int32, (1, LANES), 1)
    first_half = (lane % 32) < 16
    cos = cos_ref[...]
    sins = sin_ref[...]
    q_ref[...] = (_rope(q, cos, sins, first_half) * qscale).astype(BF16)
    k_ref[...] = _rope(k, cos, sins, first_half).astype(BF16)
    vt_ref[...] = v.T.astype(BF16)


def _attn_kernel(lam_ref, q_ref, k_ref, vt_ref, *rest, nk, tk, nk_ctx, post_scale):
    if nk_ctx:
        kc_ref, vtc_ref, g_ref, o_ref, s_sc, mc_sc, m_sc, acc_sc = rest
    else:
        g_ref, o_ref, s_sc, mc_sc, m_sc, acc_sc = rest
    q = q_ref[...]
    hw = q.shape[1]
    dh = hw // 2
    lane = lax.broadcasted_iota(jnp.int32, (1, hw), 1)
    zero = jnp.zeros_like(q)
    qs = (jnp.where(lane < dh, q, zero), jnp.where(lane >= dh, q, zero))
    m_sc[...] = jnp.full(m_sc.shape, -jnp.inf, F32)
    acc_sc[...] = jnp.zeros(acc_sc.shape, F32)
    ones_rows = (lax.broadcasted_iota(jnp.int32, (ONES_ROWS, tk), 0) == 0).astype(BF16)

    def scores(kk, slot):
        for c in range(2):
            s = lax.dot_general(kk, qs[c], (((1,), (1,)), ((), ())), preferred_element_type=F32)
            s_sc[slot, c] = s
            mc_sc[slot, c] = jnp.max(s, axis=0, keepdims=True)

    def consume(vt, slot):
        vta = jnp.concatenate([vt, ones_rows], axis=0)
        for c in range(2):
            m_prev = m_sc[c]
            m_new = jnp.maximum(m_prev, mc_sc[slot, c])
            alpha = jnp.exp2(m_prev - m_new)
            p = jnp.exp2(s_sc[slot, c] - m_new).astype(BF16)
            acc_sc[c] = alpha * acc_sc[c] + jnp.dot(vta, p, preferred_element_type=F32)
            m_sc[c] = m_new

    def k_at(i):
        if i < nk_ctx:
            return kc_ref[i * tk:(i + 1) * tk, :]
        return k_ref[(i - nk_ctx) * tk:(i - nk_ctx + 1) * tk, :]

    def vt_at(i):
        if i < nk_ctx:
            return vtc_ref[0, :, i * tk:(i + 1) * tk]
        return vt_ref[:, (i - nk_ctx) * tk:(i - nk_ctx + 1) * tk]

    n_total = nk_ctx + nk
    peel = nk_ctx + (n_total - 1 - nk_ctx) % 2
    scores(k_at(0), 0)
    for i in range(peel):
        scores(k_at(i + 1), (i + 1) % 2)
        consume(vt_at(i), i % 2)

    def body(t, carry):
        for r in range(2):
            j = 2 * t + (peel + r - nk_ctx)
            nxt = pl.multiple_of((j + 1) * tk, tk)
            cur = pl.multiple_of(j * tk, tk)
            scores(k_ref[pl.ds(nxt, tk), :], (peel + r + 1) % 2)
            consume(vt_ref[:, pl.ds(cur, tk)], (peel + r) % 2)
        return carry

    lax.fori_loop(0, (n_total - 1 - peel) // 2, body, 0)
    consume(vt_at(n_total - 1), (n_total - 1) % 2)
    o = (acc_sc[0, :hw, :] / acc_sc[0, hw:hw + 1, :]
         - lam_ref[0] * (acc_sc[1, :hw, :] / acc_sc[1, hw:hw + 1, :]))
    ms = jnp.mean(o * o, axis=0, keepdims=True)
    o = o * lax.rsqrt(ms + EPS) * g_ref[...] * post_scale
    o_ref[...] = o.T.astype(BF16)


def _attention(lam, q, k, vt, g_col, ctx, *, batch, t_len, row0, tq, tk, post_scale):
    hw = q.shape[1] // DIFF_HEADS
    nq = t_len // tq
    assert row0 % t_len == 0 and t_len % tk == 0
    qb0 = row0 // tq
    sb0 = row0 // t_len
    in_specs = [pl.BlockSpec(memory_space=pltpu.SMEM),
                pl.BlockSpec((tq, hw), lambda b, h, i: (qb0 + b * nq + i, h)),
                pl.BlockSpec((t_len, hw), lambda b, h, i: (sb0 + b, h)),
                pl.BlockSpec((hw, t_len), lambda b, h, i: (h, sb0 + b))]
    args = [lam, q, k, vt]
    nk_ctx = 0
    if ctx is not None:
        k_ctx, vt_ctx = ctx
        past = vt_ctx.shape[2]
        assert past % tk == 0
        nk_ctx = past // tk
        in_specs += [pl.BlockSpec((past, hw), lambda b, h, i: (b, h)),
                     pl.BlockSpec((1, hw, past), lambda b, h, i: (b, h, 0))]
        args += [k_ctx, vt_ctx]
    in_specs.append(pl.BlockSpec((hw, 1), lambda b, h, i: (0, 0)))
    args.append(g_col)
    return pl.pallas_call(
        functools.partial(_attn_kernel, nk=t_len // tk, tk=tk, nk_ctx=nk_ctx, post_scale=post_scale),
        grid=(batch, DIFF_HEADS, nq),
        in_specs=in_specs,
        out_specs=pl.BlockSpec((tq, hw), lambda b, h, i: (b * nq + i, h)),
        out_shape=jax.ShapeDtypeStruct((batch * t_len, q.shape[1]), BF16),
        scratch_shapes=[pltpu.VMEM((2, 2, tk, tq), F32), pltpu.VMEM((2, 2, 1, tq), F32),
                        pltpu.VMEM((2, 1, tq), F32), pltpu.VMEM((2, hw + ONES_ROWS, tq), F32)],
        compiler_params=_cparams(("parallel", "parallel", "parallel")),
        name="diff_attention",
    )(*args)


def _seq_edges(i, npt, tps):
    r = (i - npt) % tps
    is_prompt = i < npt
    first = jnp.logical_or(is_prompt, r == 0)
    last = jnp.logical_or(is_prompt, r == tps - 1)
    return first, last


def _fill_ext(ext_sc, cur_ref, prev_ref, next_ref, first, last):
    prev = prev_ref[...]
    nxt = next_ref[...]
    ext_sc[0:HALO, :] = jnp.where(first, jnp.zeros_like(prev), prev)
    ext_sc[HALO:HALO + TM, :] = cur_ref[...]
    ext_sc[HALO + TM:, :] = jnp.where(last, jnp.zeros_like(nxt), nxt)


def _moe_prologue(x_new, mod_ref, g2_ref, wr_ref, h2_ref, lg_ref):
    h2 = _rms_mod(x_new, g2_ref[...], mod_ref[0, 4:5, :], mod_ref[0, 3:4, :])
    h2_ref[...] = h2
    lg_ref[...] = jnp.dot(h2.astype(BF16), wr_ref[...], preferred_element_type=F32)


def _post_even_kernel(xp_ref, xs_ref, u_ref, up_ref, un_ref, op_ref, os_ref, mod_ref, cw_ref, cb_ref,
                      lng_ref, lnb_ref, wo_ref, g2_ref, wr_ref, x1_ref, h2_ref, lg_ref, ext_sc, sh_sc, *, npt, tps):
    i = pl.program_id(0)
    first, last = _seq_edges(i, npt, tps)
    _fill_ext(ext_sc, u_ref, up_ref, un_ref, first, last)
    cw = u_ref.shape[1]
    span = sh_sc.shape[1]
    for r in range(SUBLANES):
        sh_sc[r] = ext_sc[r:r + span, :]
    rows = 32
    chunks = []
    for rc in range(TM // rows):
        acc = None
        for k in range(CONV_K):
            off = HALO - CONV_K // 2 + k
            start = rc * rows + off - off % SUBLANES
            term = sh_sc[off % SUBLANES, start:start + rows, :] * cw_ref[k:k + 1, :]
            acc = term if acc is None else acc + term
        chunks.append(acc)
    conv = jnp.concatenate(chunks, axis=0) + cb_ref[...]
    mu = jnp.mean(conv, axis=-1, keepdims=True)
    cen = conv - mu
    var = jnp.mean(cen * cen, axis=-1, keepdims=True)
    uu = _silu(cen * lax.rsqrt(var + EPS) * lng_ref[...] + lnb_ref[...])
    is_prompt = i < npt
    o = jnp.where(is_prompt, op_ref[...], os_ref[...])
    m = (jnp.dot(uu.astype(BF16), wo_ref[0:cw, :], preferred_element_type=F32)
         + jnp.dot(o, wo_ref[cw:, :], preferred_element_type=F32))
    x1 = jnp.where(is_prompt, xp_ref[...], xs_ref[...]) + mod_ref[0, 2:3, :] * m
    x1_ref[...] = x1
    _moe_prologue(x1, mod_ref, g2_ref, wr_ref, h2_ref, lg_ref)


def _gather_start(dest_ref, yb_hbm, ybuf, sem, slot):
    def body(i, carry):
        for j in range(SUBLANES):
            for k in range(TOP_K):
                row = dest_ref[0, 0, k * TM + i * SUBLANES + j]
                pltpu.make_async_copy(yb_hbm.at[pl.ds(row, 1)], ybuf.at[slot, k, i, pl.ds(j, 1)],
                                      sem.at[slot]).start(priority=k % 2)
        return carry

    lax.fori_loop(0, TM // SUBLANES, body, 0)


def _moe_combine(dcur_ref, dnxt_ref, cw_ref, yb_hbm, ybuf, sem):
    t = pl.program_id(0)
    slot = t % 2

    @pl.when(t == 0)
    def _():
        _gather_start(dcur_ref, yb_hbm, ybuf, sem, 0)

    @pl.when(t + 1 < pl.num_programs(0))
    def _():
        _gather_start(dnxt_ref, yb_hbm, ybuf, sem, 1 - slot)

    for k in range(TOP_K):
        pltpu.make_async_copy(ybuf.at[slot, k], ybuf.at[slot, k], sem.at[slot]).wait()
    w = cw_ref[...]
    d = ybuf.shape[-1]
    y = w[:, 0:1] * ybuf[slot, 0].reshape(TM, d)
    for k in range(1, TOP_K):
        y = y + w[:, k:k + 1] * ybuf[slot, k].reshape(TM, d)
    return y


def _dispatch_kernel(dest_ref, pend_ref, h2_hbm, xb_hbm, stage, zeros, in_sem, row_sem, zero_sem):
    t = pl.program_id(0)
    nt = pl.num_programs(0)
    slot = t % DISPATCH_SLOTS

    tile_rows = TM // SUBLANES

    def stage_in(tile_idx, s):
        return pltpu.make_async_copy(h2_hbm.at[pl.ds(tile_idx * tile_rows, tile_rows)], stage.at[s], in_sem.at[s])

    def drain_rows(s):
        for _ in range(TOP_K):
            pltpu.make_async_copy(stage.at[s], stage.at[s], row_sem.at[s]).wait()

    def zero_block(start):
        return pltpu.make_async_copy(zeros, xb_hbm.at[pl.ds(pl.multiple_of(start, MOE_BLK), MOE_BLK)], zero_sem)

    def zero_fill(op):
        for e in range(N_EXPERTS):
            op(zero_block(jnp.maximum(pend_ref[e] - MOE_BLK, 0)))
        for j in range(N_EXPERTS):
            start = pend_ref[N_EXPERTS - 1] + j * MOE_BLK

            @pl.when(start < xb_hbm.shape[0])
            def _():
                op(zero_block(start))

    @pl.when(t == 0)
    def _():
        stage_in(0, 0).start()
        zeros[...] = jnp.zeros(zeros.shape, zeros.dtype)
        zero_fill(lambda cp: cp.start())
        zero_fill(lambda cp: cp.wait())

    @pl.when(jnp.logical_and(t == 0, nt > 1))
    def _():
        stage_in(1, 1).start()

    stage_in(t, slot).wait()

    def body(i, carry):
        for j in range(SUBLANES):
            for k in range(TOP_K):
                row = dest_ref[0, 0, k * TM + i * SUBLANES + j]
                pltpu.make_async_copy(stage.at[slot, i, pl.ds(j, 1)], xb_hbm.at[pl.ds(row, 1)],
                                      row_sem.at[slot]).start(priority=k % 2)
        return carry

    lax.fori_loop(0, tile_rows, body, 0)

    @pl.when(t > 0)
    def _():
        drain_rows((t + DISPATCH_SLOTS - 1) % DISPATCH_SLOTS)

    @pl.when(t + 2 < nt)
    def _():
        stage_in(t + 2, (t + 2) % DISPATCH_SLOTS).start()

    @pl.when(t == nt - 1)
    def _():
        drain_rows(slot)


def _dispatch(dest3, pend, h2, p_len):
    n, d = h2.shape
    return pl.pallas_call(
        _dispatch_kernel,
        grid=(n // TM,),
        in_specs=[pl.BlockSpec((1, 1, TOP_K * TM), lambda t: (t, 0, 0), memory_space=pltpu.SMEM),
                  pl.BlockSpec(memory_space=pltpu.SMEM), pl.BlockSpec(memory_space=pl.ANY)],
        out_specs=pl.BlockSpec(memory_space=pl.ANY),
        out_shape=jax.ShapeDtypeStruct((p_len, d), h2.dtype),
        scratch_shapes=[pltpu.VMEM((DISPATCH_SLOTS, TM // SUBLANES, SUBLANES, d), h2.dtype),
                        pltpu.VMEM((MOE_BLK, d), h2.dtype),
                        pltpu.SemaphoreType.DMA((DISPATCH_SLOTS,)), pltpu.SemaphoreType.DMA((DISPATCH_SLOTS,)),
                        pltpu.SemaphoreType.DMA(())],
        compiler_params=_cparams(("arbitrary",)),
        name="moe_dispatch",
    )(dest3, pend, h2.reshape(n // SUBLANES, SUBLANES, d))


def _pre_odd_kernel(x_ref, dcur_ref, dnxt_ref, cw_ref, yb_hbm, modp_ref, mod_ref, g_ref, w_ref,
                    x2_ref, xc_ref, p_ref, bg_ref, ybuf, sem, *, cw):
    y = _moe_combine(dcur_ref, dnxt_ref, cw_ref, yb_hbm, ybuf, sem)
    x2 = x_ref[...] + modp_ref[0, 5:6, :] * y
    x2_ref[...] = x2
    h = _rms_mod(x2, g_ref[...], mod_ref[0, 1:2, :], mod_ref[0, 0:1, :])
    proj = jnp.dot(h.astype(BF16), w_ref[...], preferred_element_type=F32)
    xc_ref[...] = proj[:, :cw]
    hd = proj[:, cw:2 * cw]
    bg_ref[...] = proj[:, 2 * cw:3 * cw]
    p_ref[...] = proj[:, 3 * cw:] * hd


def _post_odd_kernel(x_ref, xc_ref, xcp_ref, xcn_ref, p_ref, pp_ref, pn_ref, bg_ref, mod_ref,
                     pw_ref, ps_ref, dw_ref, wo_ref, g2_ref, wr_ref,
                     x3_ref, h2_ref, lg_ref, extc_sc, extp_sc, *, npt, tps):
    i = pl.program_id(0)
    first, last = _seq_edges(i, npt, tps)
    _fill_ext(extc_sc, xc_ref, xcp_ref, xcn_ref, first, last)
    _fill_ext(extp_sc, p_ref, pp_ref, pn_ref, first, last)
    cw = xc_ref.shape[1]
    gc = cw // len(POOL_WINDOWS)
    is_prompt = i < npt
    t_len = jnp.where(is_prompt, TM, TM * tps)
    pos = lax.broadcasted_iota(jnp.int32, (TM, 1), 0) + jnp.where(is_prompt, 0, ((i - npt) % tps) * TM)
    ds = []
    for g, w in enumerate(POOL_WINDOWS):
        sl = slice(g * gc, (g + 1) * gc)
        acc = None
        for o in range(-(w // 2), w // 2):
            term = extc_sc[HALO + o:HALO + o + TM, sl]
            acc = term if acc is None else acc + term
        lo = jnp.maximum(pos - w // 2, 0)
        hi = jnp.minimum(pos + w // 2 - 1, t_len - 1)
        cnt = (hi - lo + 1).astype(F32)
        ds.append(acc / cnt - extc_sc[HALO:HALO + TM, sl])
    d = jnp.concatenate(ds, axis=1).astype(BF16)
    yc = jnp.dot(d, pw_ref[...], preferred_element_type=F32) * ps_ref[...]
    conv = None
    for k in range(SCONV_K):
        start = HALO - SCONV_K // 2 + k
        term = extp_sc[start:start + TM, :] * dw_ref[k:k + 1, :]
        conv = term if conv is None else conv + term
    yd = bg_ref[...] * conv
    m = (jnp.dot(yc.astype(BF16), wo_ref[0:cw, :], preferred_element_type=F32)
         + jnp.dot(yd.astype(BF16), wo_ref[cw:, :], preferred_element_type=F32))
    x3 = x_ref[...] + mod_ref[0, 2:3, :] * m
    x3_ref[...] = x3
    _moe_prologue(x3, mod_ref, g2_ref, wr_ref, h2_ref, lg_ref)


def _final_kernel(x_ref, dcur_ref, dnxt_ref, cw_ref, yb_hbm, modp_ref, g_ref, op_ref, os_ref, ybuf, sem, *, npt):
    i = pl.program_id(0)
    y = _moe_combine(dcur_ref, dnxt_ref, cw_ref, yb_hbm, ybuf, sem)
    x = x_ref[...] + modp_ref[0, 5:6, :] * y
    ms = jnp.mean(x * x, axis=-1, keepdims=True)
    out = x * lax.rsqrt(ms + EPS) * g_ref[...]

    @pl.when(i < npt)
    def _():
        op_ref[...] = out

    @pl.when(i >= npt)
    def _():
        os_ref[...] = out


def _expert_kernel(be_ref, nv_ref, x_ref, wg_ref, wu_ref, wd_ref, y_ref, wg_sc, wu_sc, wd_sc):
    i = pl.program_id(0)

    @pl.when(jnp.logical_or(i == 0, be_ref[i] != be_ref[jnp.maximum(i - 1, 0)]))
    def _():
        wg_sc[...] = wg_ref[0, 0].astype(BF16)
        wu_sc[...] = wu_ref[0, 0].astype(BF16)
        wd_sc[...] = wd_ref[0, 0].astype(BF16)

    @pl.when(i < nv_ref[0])
    def _():
        x = x_ref[...].astype(BF16)
        hg = jnp.dot(x, wg_sc[...], preferred_element_type=F32)
        hu = jnp.dot(x, wu_sc[...], preferred_element_type=F32)
        hid = (_silu(hg) * hu).astype(BF16)
        y_ref[...] = jnp.dot(hid, wd_sc[...], preferred_element_type=F32)

    @pl.when(i >= nv_ref[0])
    def _():
        y_ref[...] = jnp.zeros(y_ref.shape, F32)


def _experts(blk_e, nvalid, xb, layer, wg, wu, wd):
    p_len, d = xb.shape
    hid = wg.shape[3]
    nblk = p_len // MOE_BLK
    grid_spec = pltpu.PrefetchScalarGridSpec(
        num_scalar_prefetch=2,
        grid=(nblk,),
        in_specs=[pl.BlockSpec((MOE_BLK, d), lambda i, be, nv: (jnp.minimum(i, nv[0] - 1), 0)),
                  pl.BlockSpec((1, 1, d, hid), lambda i, be, nv: (layer, be[i], 0, 0)),
                  pl.BlockSpec((1, 1, d, hid), lambda i, be, nv: (layer, be[i], 0, 0)),
                  pl.BlockSpec((1, 1, hid, d), lambda i, be, nv: (layer, be[i], 0, 0))],
        out_specs=pl.BlockSpec((MOE_BLK, d), lambda i, be, nv: (i, 0)),
        scratch_shapes=[pltpu.VMEM((d, hid), BF16), pltpu.VMEM((d, hid), BF16), pltpu.VMEM((hid, d), BF16)],
    )
    return pl.pallas_call(
        _expert_kernel,
        grid_spec=grid_spec,
        out_shape=jax.ShapeDtypeStruct((p_len, d), F32),
        compiler_params=_cparams(("arbitrary",)),
        name="moe_experts",
    )(blk_e, nvalid, xb, wg, wu, wd)


def _route_kernel(lg_ref, b_ref, info_ref, cnt_ref):
    lt = lg_ref[...].T + b_ref[...]
    gl = lt[N_EXPERTS:N_EXPERTS + N_GROUPS]
    ge = jnp.exp(gl - jnp.max(gl, axis=0, keepdims=True))
    gp = ge / jnp.sum(ge, axis=0, keepdims=True)
    best = gp[0:1]
    gidx = jnp.zeros(best.shape, jnp.int32)
    el = lt[0:EXPERTS_PER_GROUP]
    for g in range(1, N_GROUPS):
        better = gp[g:g + 1] > best
        gidx = jnp.where(better, g, gidx)
        best = jnp.where(better, gp[g:g + 1], best)
    for g in range(1, N_GROUPS):
        el = jnp.where(gidx == g, lt[g * EXPERTS_PER_GROUP:(g + 1) * EXPERTS_PER_GROUP], el)
    rio = lax.broadcasted_iota(jnp.int32, el.shape, 0)
    v0 = jnp.max(el, axis=0, keepdims=True)
    i0 = jnp.min(jnp.where(el == v0, rio, EXPERTS_PER_GROUP), axis=0, keepdims=True)
    el2 = jnp.where(rio == i0, -jnp.inf, el)
    v1 = jnp.max(el2, axis=0, keepdims=True)
    i1 = jnp.min(jnp.where(el2 == v1, rio, EXPERTS_PER_GROUP), axis=0, keepdims=True)
    e1 = jnp.exp(v1 - v0)
    den = 1.0 + e1
    w0 = (1.0 / den) * best
    w1 = (e1 / den) * best
    eid0 = gidx * EXPERTS_PER_GROUP + i0
    eid1 = gidx * EXPERTS_PER_GROUP + i1
    tm = lt.shape[1]
    eio = lax.broadcasted_iota(jnp.int32, (N_EXPERTS, tm), 0)
    oh0 = eio == eid0
    oh1 = eio == eid1
    ohs = jnp.where(oh0, 1.0, 0.0) + jnp.where(oh1, 1.0, 0.0)
    upper = (lax.broadcasted_iota(jnp.int32, (tm, tm), 0)
             < lax.broadcasted_iota(jnp.int32, (tm, tm), 1)).astype(BF16)
    before = jnp.dot(ohs.astype(BF16), upper, preferred_element_type=F32)
    rank0 = jnp.sum(jnp.where(oh0, before, 0.0), axis=0, keepdims=True)
    rank1 = jnp.sum(jnp.where(oh1, before, 0.0), axis=0, keepdims=True)
    info_ref[0] = jnp.concatenate([eid0.astype(F32), eid1.astype(F32), w0, w1, rank0, rank1,
                                   jnp.zeros((2, tm), F32)], axis=0)
    cnt_ref[0] = jnp.sum(ohs, axis=1, keepdims=True)


def _route(logits, bias_col):
    n = logits.shape[0]
    nt = n // TM
    return pl.pallas_call(
        _route_kernel,
        grid=(nt,),
        in_specs=[pl.BlockSpec((TM, ROUTER_PAD), lambda i: (i, 0)),
                  pl.BlockSpec((ROUTER_PAD, 1), lambda i: (0, 0))],
        out_specs=[pl.BlockSpec((1, 8, TM), lambda i: (i, 0, 0)),
                   pl.BlockSpec((1, N_EXPERTS, 1), lambda i: (i, 0, 0))],
        out_shape=[jax.ShapeDtypeStruct((nt, 8, TM), F32), jax.ShapeDtypeStruct((nt, N_EXPERTS, 1), F32)],
        compiler_params=_cparams(("parallel",)),
        name="moe_route",
    )(logits, bias_col)


def _moe(h2, logits, bias_col, layer, wg, wu, wd):
    n, d = h2.shape
    info, cnt = _route(logits, bias_col)
    cnt = cnt[:, :, 0].astype(jnp.int32)
    counts = jnp.sum(cnt, axis=0)
    pc = ((counts + MOE_BLK - 1) // MOE_BLK) * MOE_BLK
    pend = jnp.cumsum(pc)
    base = (pend - pc)[None, :] + jnp.cumsum(cnt, axis=0) - cnt
    eid = info[:, 0:2, :].astype(jnp.int32)
    rank = info[:, 4:6, :].astype(jnp.int32)
    sel = eid[..., None] == jnp.arange(N_EXPERTS, dtype=jnp.int32)
    dest = jnp.sum(jnp.where(sel, base[:, None, None, :], 0), axis=-1) + rank
    p_len = n * TOP_K + N_EXPERTS * MOE_BLK
    nblk = p_len // MOE_BLK
    blk_start = jnp.arange(nblk, dtype=jnp.int32) * MOE_BLK
    blk_e = jnp.minimum(jnp.sum(pend[None, :] <= blk_start[:, None], axis=1), N_EXPERTS - 1).astype(jnp.int32)
    nvalid = (pend[-1:] // MOE_BLK).astype(jnp.int32)
    dest3 = dest.reshape(dest.shape[0], 1, TOP_K * TM)
    wcol = jnp.transpose(info[:, 2:2 + TOP_K, :], (0, 2, 1)).reshape(n, TOP_K)
    xb = _dispatch(dest3, pend.astype(jnp.int32), h2, p_len)
    yb = _experts(blk_e, nvalid, xb, layer, wg, wu, wd)
    return dest3, wcol, yb


def _rope_tables(t_len, qw):
    rows = t_len // GRID_W
    ax = qw // (DIFF_HEADS * 2) // 2
    row = jnp.repeat(jnp.arange(rows, dtype=F32), GRID_W)
    col = jnp.tile(jnp.arange(GRID_W, dtype=F32), rows)
    inv = ROPE_BASE ** (-jnp.arange(0, ax, 2, dtype=F32) / ax)
    ar = row[:, None] * inv[None, :]
    ac = col[:, None] * inv[None, :]
    cr, sr, cc, sc = jnp.cos(ar), jnp.sin(ar), jnp.cos(ac), jnp.sin(ac)
    cos = jnp.concatenate([cr, cr, cc, cc], axis=1)
    sins = jnp.concatenate([-sr, sr, -sc, sc], axis=1)
    reps = qw // cos.shape[1]
    cos = jnp.concatenate([jnp.ones((TM, cos.shape[1]), F32), cos], axis=0)
    sins = jnp.concatenate([jnp.zeros((TM, sins.shape[1]), F32), sins], axis=0)
    return jnp.tile(cos, (1, reps)), jnp.tile(sins, (1, reps))


def kernel(x_prompt, x_sample, cache_k, cache_v, c, c_ctx, w_ada, b_ada, g_norm1, g_norm2, g_final, w_in_e, conv_a_w, conv_a_b, ln_a_g, ln_a_b, lam_q1, lam_k1, lam_q2, lam_k2, subln_g, w_out_e, w_in_o, pool_w, pool_scale, conv_d_w, w_out_o, w_router_g, b_router_g, w_router_e, b_router_e, w_gate, w_up, w_down):
    bp, tp, d = x_prompt.shape
    bs, ts, _ = x_sample.shape
    past = cache_k.shape[2]
    assert tp == TM and ts % TM == 0 and d % LANES == 0
    npt = bp
    tps = ts // TM
    n_p, n_s = bp * tp, bs * ts
    n = n_p + n_s
    nt = n // TM
    cw = conv_a_w.shape[2]
    qw = (w_in_e.shape[2] - 2 * cw) // 3
    dh = qw // (DIFF_HEADS * 2)
    hpt = TM // HALO

    def cond_of(i):
        return jnp.where(i < npt, 0, 1 + (i - npt) // tps)

    def tile(w):
        return pl.BlockSpec((TM, w), lambda i: (i, 0))

    def halo_prev(w):
        return pl.BlockSpec((HALO, w), lambda i: (jnp.maximum(i * hpt - 1, 0), 0))

    def halo_next(w):
        return pl.BlockSpec((HALO, w), lambda i: (jnp.minimum((i + 1) * hpt, nt * hpt - 1), 0))

    def whole(shape):
        return pl.BlockSpec(shape, lambda i: (0,) * len(shape))

    mod_spec = pl.BlockSpec((1, 6, d), lambda i: (cond_of(i), 0, 0))
    combine_specs = [pl.BlockSpec((1, 1, TOP_K * TM), lambda i: (i, 0, 0), memory_space=pltpu.SMEM),
                     pl.BlockSpec((1, 1, TOP_K * TM), lambda i: (jnp.minimum(i + 1, nt - 1), 0, 0),
                                  memory_space=pltpu.SMEM),
                     tile(TOP_K), pl.BlockSpec(memory_space=pl.ANY)]
    combine_scratch = [pltpu.VMEM((2, TOP_K, TM // SUBLANES, SUBLANES, d), F32), pltpu.SemaphoreType.DMA((2,))]

    rows = 8 * ((1 + bs + 7) // 8)
    cond = jnp.concatenate([c_ctx[None, :], c, jnp.zeros((rows - 1 - bs, d), F32)], axis=0)
    mod = _ada_table(cond, w_ada, b_ada)

    xp2, xs2 = x_prompt.reshape(n_p, d), x_sample.reshape(n_s, d)

    def prompt_tile(w):
        return pl.BlockSpec((TM, w), lambda i: (jnp.minimum(i, npt - 1), 0))

    def sample_tile(w):
        return pl.BlockSpec((TM, w), lambda i: (jnp.maximum(i - npt, 0), 0))

    def router_w(i):
        wre = jnp.transpose(w_router_e[i], (1, 0, 2)).reshape(d, N_EXPERTS)
        wr = jnp.concatenate([wre, w_router_g[i]], axis=1)
        return jnp.pad(wr, ((0, 0), (0, ROUTER_PAD - wr.shape[1]))).astype(BF16)

    def router_b(i):
        b = jnp.concatenate([b_router_e[i].reshape(N_EXPERTS), b_router_g[i]])
        return jnp.pad(b, (0, ROUTER_PAD - b.shape[0]))[:, None]

    cos, sins = _rope_tables(ts, qw)
    tab_spec = pl.BlockSpec((TM, qw), lambda i: (jnp.where(i < npt, 0, 1 + (i - npt) % tps), 0))
    cache_spec = pl.BlockSpec((TM, qw), lambda i: (jnp.minimum(i, npt - 1), 0))
    u, q, k, vt, kc, vc = pl.pallas_call(
        functools.partial(_pre_even_kernel, npt=npt, cw=cw, qw=qw, qscale=dh ** -0.5 * math.log2(math.e)),
        grid=(nt,),
        in_specs=[prompt_tile(d), sample_tile(d), mod_spec, whole((1, d)), whole(w_in_e.shape[1:]),
                  tab_spec, tab_spec],
        out_specs=[tile(cw), tile(qw), tile(qw), pl.BlockSpec((qw, TM), lambda i: (0, i)),
                   cache_spec, cache_spec],
        out_shape=[jax.ShapeDtypeStruct((n, cw), F32), jax.ShapeDtypeStruct((n, qw), BF16),
                   jax.ShapeDtypeStruct((n, qw), BF16), jax.ShapeDtypeStruct((qw, n), BF16),
                   jax.ShapeDtypeStruct((n_p, qw), F32), jax.ShapeDtypeStruct((n_p, qw), F32)],
        compiler_params=_cparams(("arbitrary",)),
        name="pre_even",
    )(xp2, xs2, mod[0], g_norm1[0][None, :], w_in_e[0].astype(BF16), cos, sins)

    lam_init = 0.8 - 0.6 * math.exp(-0.3 * 0)
    lam = (jnp.exp(jnp.sum(lam_q1[0] * lam_k1[0])) - jnp.exp(jnp.sum(lam_q2[0] * lam_k2[0])) + lam_init)
    lam = lam.reshape(1).astype(F32)
    sub_g = subln_g[0][:, None]
    o_p = _attention(lam, q, k, vt, sub_g, None, batch=bp, t_len=tp, row0=0, tq=TM, tk=TM,
                     post_scale=1.0 - lam_init)
    k_ctx = cache_k[:, 0].reshape(bs * past, qw).astype(BF16)
    vt_ctx = jnp.transpose(cache_v[:, 0].reshape(bs, past, qw), (0, 2, 1)).astype(BF16)
    o_s = _attention(lam, q, k, vt, sub_g, (k_ctx, vt_ctx), batch=bs, t_len=ts, row0=n_p, tq=1024, tk=512,
                     post_scale=1.0 - lam_init)

    moe_outs = [jax.ShapeDtypeStruct((n, d), F32), jax.ShapeDtypeStruct((n, d), F32),
                jax.ShapeDtypeStruct((n, ROUTER_PAD), F32)]
    moe_out_specs = [tile(d), tile(d), tile(ROUTER_PAD)]
    x1, h2, logits = pl.pallas_call(
        functools.partial(_post_even_kernel, npt=npt, tps=tps),
        grid=(nt,),
        in_specs=[prompt_tile(d), sample_tile(d), tile(cw), halo_prev(cw), halo_next(cw),
                  prompt_tile(qw), sample_tile(qw), mod_spec,
                  whole((CONV_K, cw)), whole((1, cw)), whole((1, cw)), whole((1, cw)),
                  whole(w_out_e.shape[1:]), whole((1, d)), whole((d, ROUTER_PAD))],
        out_specs=moe_out_specs,
        out_shape=moe_outs,
        scratch_shapes=[pltpu.VMEM((TM + 2 * HALO, cw), F32),
                        pltpu.VMEM((SUBLANES, TM + 2 * HALO - SUBLANES, cw), F32)],
        compiler_params=_cparams(("parallel",)),
        name="post_even",
    )(xp2, xs2, u, u, u, o_p, o_s, mod[0], conv_a_w[0], conv_a_b[0][None, :], ln_a_g[0][None, :], ln_a_b[0][None, :],
      w_out_e[0].astype(BF16), g_norm2[0][None, :], router_w(0))
    dest3, wcol, yb = _moe(h2, logits, router_b(0), 0, w_gate, w_up, w_down)

    pw = pool_w.shape[2]
    x2, xc, p, bg = pl.pallas_call(
        functools.partial(_pre_odd_kernel, cw=cw),
        grid=(nt,),
        in_specs=[tile(d)] + combine_specs + [mod_spec, mod_spec, whole((1, d)), whole(w_in_o.shape[1:])],
        out_specs=[tile(d), tile(cw), tile(cw), tile(cw)],
        out_shape=[jax.ShapeDtypeStruct((n, d), F32)] + [jax.ShapeDtypeStruct((n, cw), F32)] * 3,
        scratch_shapes=combine_scratch,
        compiler_params=_cparams(("arbitrary",)),
        name="pre_odd",
    )(x1, dest3, dest3, wcol, yb, mod[0], mod[1], g_norm1[1][None, :], w_in_o[0].astype(BF16))
    pool_bd = jnp.zeros((cw, cw), F32)
    for g in range(len(POOL_WINDOWS)):
        pool_bd = pool_bd.at[g * pw:(g + 1) * pw, g * pw:(g + 1) * pw].set(pool_w[0, g])
    x3, h2, logits = pl.pallas_call(
        functools.partial(_post_odd_kernel, npt=npt, tps=tps),
        grid=(nt,),
        in_specs=[tile(d), tile(cw), halo_prev(cw), halo_next(cw), tile(cw), halo_prev(cw), halo_next(cw),
                  tile(cw), mod_spec, whole((cw, cw)), whole((1, cw)), whole((SCONV_K, cw)),
                  whole(w_out_o.shape[1:]), whole((1, d)), whole((d, ROUTER_PAD))],
        out_specs=moe_out_specs,
        out_shape=moe_outs,
        scratch_shapes=[pltpu.VMEM((TM + 2 * HALO, cw), F32), pltpu.VMEM((TM + 2 * HALO, cw), F32)],
        compiler_params=_cparams(("parallel",)),
        name="post_odd",
    )(x2, xc, xc, xc, p, p, p, bg, mod[1], pool_bd.astype(BF16), pool_scale[0][None, :], conv_d_w[0],
      w_out_o[0].astype(BF16), g_norm2[1][None, :], router_w(1))
    dest3, wcol, yb = _moe(h2, logits, router_b(1), 1, w_gate, w_up, w_down)

    out_p, out_s = pl.pallas_call(
        functools.partial(_final_kernel, npt=npt),
        grid=(nt,),
        in_specs=[tile(d)] + combine_specs + [mod_spec, whole((1, d))],
        out_specs=[pl.BlockSpec((TM, d), lambda i: (jnp.minimum(i, npt - 1), 0)),
                   pl.BlockSpec((TM, d), lambda i: (jnp.maximum(i - npt, 0), 0))],
        out_shape=[jax.ShapeDtypeStruct((n_p, d), F32), jax.ShapeDtypeStruct((n_s, d), F32)],
        scratch_shapes=combine_scratch,
        compiler_params=_cparams(("arbitrary",)),
        name="final_norm",
    )(x3, dest3, dest3, wcol, yb, mod[1], g_final[None, :])

    new_k = kc.reshape(bp, 1, tp, DIFF_HEADS, 2, dh)
    new_v = vc.reshape(bp, 1, tp, DIFF_HEADS, 2 * dh)
    return (out_p.reshape(bp, tp, d), out_s.reshape(bs, ts, d), new_k, new_v)
```

```python
import functools
import math

import jax
import jax.numpy as jnp
from jax import lax
from jax.experimental import pallas as pl
from jax.experimental.pallas import tpu as pltpu

F32 = jnp.float32
BF16 = jnp.bfloat16

EPS = 1e-6
GRID_W = 64
ROPE_BASE = 10000.0
CONV_K = 31
SCONV_K = 3
POOL_WINDOWS = (2, 4, 8, 16)
DIFF_HEADS = 4
N_GROUPS = 4
EXPERTS_PER_GROUP = 8
N_EXPERTS = N_GROUPS * EXPERTS_PER_GROUP
TOP_K = 2

LANES = 128
SUBLANES = 8
TM = 256
HALO = 16
PROJ_ROW_PARTS = 2
MIX_ROW_PARTS = 1
MOE_BLK = 512
ROUTER_PAD = 128
ONES_ROWS = 16
DISPATCH_SLOTS = 3
VMEM_LIMIT = 56 * 1024 * 1024


def _cparams(sem):
    return pltpu.CompilerParams(dimension_semantics=sem, vmem_limit_bytes=VMEM_LIMIT)


def _rms_mod(x, g, sc, sh):
    ms = jnp.mean(x * x, axis=-1, keepdims=True)
    return (x * lax.rsqrt(ms + EPS)) * g * (1.0 + sc) + sh


def _silu(x):
    return x * jax.nn.sigmoid(x)


def _ada_kernel(c_ref, w_ref, b_ref, o_ref):
    cs = _silu(c_ref[...])
    o_ref[0] = jnp.dot(cs, w_ref[0], precision=lax.Precision.HIGHEST,
                       preferred_element_type=F32) + b_ref[0]


def _ada_table(cond, w_ada, b_ada):
    depth, d, six_d = w_ada.shape
    rows = cond.shape[0]
    nj = six_d // d
    out = pl.pallas_call(
        _ada_kernel,
        grid=(depth, nj),
        in_specs=[pl.BlockSpec((rows, d), lambda l, j: (0, 0)),
                  pl.BlockSpec((1, d, d), lambda l, j: (l, 0, j)),
                  pl.BlockSpec((1, 1, d), lambda l, j: (l, 0, j))],
        out_specs=pl.BlockSpec((1, rows, d), lambda l, j: (l, 0, j)),
        out_shape=jax.ShapeDtypeStruct((depth, rows, six_d), F32),
        compiler_params=_cparams(("parallel", "parallel")),
        name="ada_table",
    )(cond, w_ada, b_ada.reshape(depth, 1, six_d))
    return out.reshape(depth, rows, nj, d)


def _row_parts(parts):
    rows = TM // parts
    return [slice(p * rows, (p + 1) * rows) for p in range(parts)]


def _rope(x, cos, sins, first_half):
    outs = []
    for j in range(x.shape[1] // LANES):
        sl = slice(j * LANES, (j + 1) * LANES)
        xs = x[:, sl]
        nxt = pltpu.roll(xs, LANES - 16, 1)
        prv = pltpu.roll(xs, 16, 1)
        rot = jnp.where(first_half, nxt, prv)
        outs.append(xs * cos[:, sl] + rot * sins[:, sl])
    return jnp.concatenate(outs, axis=1)


def _pre_even_kernel(xp_ref, xs_ref, mod_ref, g_ref, w_ref, cos_ref, sin_ref,
                     u_ref, q_ref, k_ref, vt_ref, kc_ref, vc_ref, kv_sc, *, npt, cw, qw, qscale):
    i = pl.program_id(0)
    is_prompt = i < npt
    lane = lax.broadcasted_iota(jnp.int32, (1, LANES), 1)
    first_half = (lane % 32) < 16
    for sl in _row_parts(PROJ_ROW_PARTS):
        x = jnp.where(is_prompt, xp_ref[sl, :], xs_ref[sl, :])
        h = _rms_mod(x, g_ref[...], mod_ref[0, 1:2, :], mod_ref[0, 0:1, :])
        proj = jnp.dot(h.astype(BF16), w_ref[...], preferred_element_type=F32)
        ga = proj[:, :cw]
        gb = proj[:, cw:2 * cw]
        u_ref[sl, :] = ga * jax.nn.sigmoid(gb)
        q = proj[:, 2 * cw:2 * cw + qw]
        k = proj[:, 2 * cw + qw:2 * cw + 2 * qw]
        v = proj[:, 2 * cw + 2 * qw:]
        kv_sc[0, sl, :] = k
        kv_sc[1, sl, :] = v
        cos = cos_ref[sl, :]
        sins = sin_ref[sl, :]
        q_ref[sl, :] = (_rope(q, cos, sins, first_half) * qscale).astype(BF16)
        k_ref[sl, :] = _rope(k, cos, sins, first_half).astype(BF16)
        vt_ref[:, sl] = v.T.astype(BF16)

    @pl.when(is_prompt)
    def _():
        kc_ref[...] = kv_sc[0]
        vc_ref[...] = kv_sc[1]


def _attn_kernel(lam_ref, q_ref, k_ref, vt_ref, *rest, nk, tk, nk_ctx, post_scale):
    if nk_ctx:
        kc_ref, vtc_ref, g_ref, o_ref, s_sc, mc_sc, m_sc, acc_sc = rest
    else:
        g_ref, o_ref, s_sc, mc_sc, m_sc, acc_sc = rest
    q = q_ref[...]
    hw = q.shape[1]
    dh = hw // 2
    lane = lax.broadcasted_iota(jnp.int32, (1, hw), 1)
    zero = jnp.zeros_like(q)
    qs = (jnp.where(lane < dh, q, zero), jnp.where(lane >= dh, q, zero))
    m_sc[...] = jnp.full(m_sc.shape, -jnp.inf, F32)
    acc_sc[...] = jnp.zeros(acc_sc.shape, F32)
    ones_rows = (lax.broadcasted_iota(jnp.int32, (ONES_ROWS, tk), 0) == 0).astype(BF16)

    def scores(kk, slot):
        for c in range(2):
            s = lax.dot_general(kk, qs[c], (((1,), (1,)), ((), ())), preferred_element_type=F32)
            s_sc[slot, c] = s
            mc_sc[slot, c] = jnp.max(s, axis=0, keepdims=True)

    def consume(vt, slot):
        vta = jnp.concatenate([vt, ones_rows], axis=0)
        for c in range(2):
            m_prev = m_sc[c]
            m_new = jnp.maximum(m_prev, mc_sc[slot, c])
            alpha = jnp.exp2(m_prev - m_new)
            p = jnp.exp2(s_sc[slot, c] - m_new).astype(BF16)
            acc_sc[c] = alpha * acc_sc[c] + jnp.dot(vta, p, preferred_element_type=F32)
            m_sc[c] = m_new

    def k_at(i):
        if i < nk_ctx:
            return kc_ref[i * tk:(i + 1) * tk, :]
        return k_ref[(i - nk_ctx) * tk:(i - nk_ctx + 1) * tk, :]

    def vt_at(i):
        if i < nk_ctx:
            return vtc_ref[0, :, i * tk:(i + 1) * tk]
        return vt_ref[:, (i - nk_ctx) * tk:(i - nk_ctx + 1) * tk]

    n_total = nk_ctx + nk
    peel = nk_ctx + (n_total - 1 - nk_ctx) % 2
    scores(k_at(0), 0)
    for i in range(peel):
        scores(k_at(i + 1), (i + 1) % 2)
        consume(vt_at(i), i % 2)

    def body(t, carry):
        for r in range(2):
            j = 2 * t + (peel + r - nk_ctx)
            nxt = pl.multiple_of((j + 1) * tk, tk)
            cur = pl.multiple_of(j * tk, tk)
            scores(k_ref[pl.ds(nxt, tk), :], (peel + r + 1) % 2)
            consume(vt_ref[:, pl.ds(cur, tk)], (peel + r) % 2)
        return carry

    lax.fori_loop(0, (n_total - 1 - peel) // 2, body, 0)
    consume(vt_at(n_total - 1), (n_total - 1) % 2)
    o = (acc_sc[0, :hw, :] / acc_sc[0, hw:hw + 1, :]
         - lam_ref[0] * (acc_sc[1, :hw, :] / acc_sc[1, hw:hw + 1, :]))
    ms = jnp.mean(o * o, axis=0, keepdims=True)
    o = o * lax.rsqrt(ms + EPS) * g_ref[...] * post_scale
    o_ref[...] = o.T.astype(BF16)


def _attention(lam, q, k, vt, g_col, ctx, *, batch, t_len, row0, tq, tk, post_scale):
    hw = q.shape[1] // DIFF_HEADS
    nq = t_len // tq
    assert row0 % t_len == 0 and t_len % tk == 0
    qb0 = row0 // tq
    sb0 = row0 // t_len
    in_specs = [pl.BlockSpec(memory_space=pltpu.SMEM),
                pl.BlockSpec((tq, hw), lambda b, h, i: (qb0 + b * nq + i, h)),
                pl.BlockSpec((t_len, hw), lambda b, h, i: (sb0 + b, h)),
                pl.BlockSpec((hw, t_len), lambda b, h, i: (h, sb0 + b))]
    args = [lam, q, k, vt]
    nk_ctx = 0
    if ctx is not None:
        k_ctx, vt_ctx = ctx
        past = vt_ctx.shape[2]
        assert past % tk == 0
        nk_ctx = past // tk
        in_specs += [pl.BlockSpec((past, hw), lambda b, h, i: (b, h)),
                     pl.BlockSpec((1, hw, past), lambda b, h, i: (b, h, 0))]
        args += [k_ctx, vt_ctx]
    in_specs.append(pl.BlockSpec((hw, 1), lambda b, h, i: (0, 0)))
    args.append(g_col)
    return pl.pallas_call(
        functools.partial(_attn_kernel, nk=t_len // tk, tk=tk, nk_ctx=nk_ctx, post_scale=post_scale),
        grid=(batch, DIFF_HEADS, nq),
        in_specs=in_specs,
        out_specs=pl.BlockSpec((tq, hw), lambda b, h, i: (b * nq + i, h)),
        out_shape=jax.ShapeDtypeStruct((batch * t_len, q.shape[1]), BF16),
        scratch_shapes=[pltpu.VMEM((2, 2, tk, tq), F32), pltpu.VMEM((2, 2, 1, tq), F32),
                        pltpu.VMEM((2, 1, tq), F32), pltpu.VMEM((2, hw + ONES_ROWS, tq), F32)],
        compiler_params=_cparams(("parallel", "parallel", "parallel")),
        name="diff_attention",
    )(*args)


def _seq_edges(i, npt, tps):
    r = (i - npt) % tps
    is_prompt = i < npt
    first = jnp.logical_or(is_prompt, r == 0)
    last = jnp.logical_or(is_prompt, r == tps - 1)
    return first, last


def _fill_ext(ext_sc, cur_ref, prev_ref, next_ref, first, last):
    prev = prev_ref[...]
    nxt = next_ref[...]
    ext_sc[0:HALO, :] = jnp.where(first, jnp.zeros_like(prev), prev)
    ext_sc[HALO:HALO + TM, :] = cur_ref[...]
    ext_sc[HALO + TM:, :] = jnp.where(last, jnp.zeros_like(nxt), nxt)


def _moe_prologue(x_new, sl, mod_ref, g2_ref, wr_ref, h2_ref, lg_ref):
    h2 = _rms_mod(x_new, g2_ref[...], mod_ref[0, 4:5, :], mod_ref[0, 3:4, :])
    h2_ref[sl, :] = h2
    lg_ref[sl, :] = jnp.dot(h2.astype(BF16), wr_ref[...], preferred_element_type=F32)


def _post_even_kernel(xp_ref, xs_ref, u_ref, up_ref, un_ref, op_ref, os_ref, mod_ref, cw_ref, cb_ref,
                      lng_ref, lnb_ref, wo_ref, g2_ref, wr_ref, x1_ref, h2_ref, lg_ref, ext_sc, sh_sc, *, npt, tps):
    i = pl.program_id(0)
    first, last = _seq_edges(i, npt, tps)
    _fill_ext(ext_sc, u_ref, up_ref, un_ref, first, last)
    cw = u_ref.shape[1]
    span = sh_sc.shape[1]
    for r in range(SUBLANES):
        sh_sc[r] = ext_sc[r:r + span, :]
    rows = 32
    is_prompt = i < npt
    for sl in _row_parts(MIX_ROW_PARTS):
        chunks = []
        for r0 in range(sl.start, sl.stop, rows):
            acc = None
            for k in range(CONV_K):
                off = HALO - CONV_K // 2 + k
                start = r0 + off - off % SUBLANES
                term = sh_sc[off % SUBLANES, start:start + rows, :] * cw_ref[k:k + 1, :]
                acc = term if acc is None else acc + term
            chunks.append(acc)
        conv = jnp.concatenate(chunks, axis=0) + cb_ref[...]
        mu = jnp.mean(conv, axis=-1, keepdims=True)
        cen = conv - mu
        var = jnp.mean(cen * cen, axis=-1, keepdims=True)
        uu = _silu(cen * lax.rsqrt(var + EPS) * lng_ref[...] + lnb_ref[...])
        o = jnp.where(is_prompt, op_ref[sl, :], os_ref[sl, :])
        m = (jnp.dot(uu.astype(BF16), wo_ref[0:cw, :], preferred_element_type=F32)
             + jnp.dot(o, wo_ref[cw:, :], preferred_element_type=F32))
        x1 = jnp.where(is_prompt, xp_ref[sl, :], xs_ref[sl, :]) + mod_ref[0, 2:3, :] * m
        x1_ref[sl, :] = x1
        _moe_prologue(x1, sl, mod_ref, g2_ref, wr_ref, h2_ref, lg_ref)


def _gather_start(dest_ref, yb_hbm, ybuf, sem, slot):
    def body(i, carry):
        for j in range(SUBLANES):
            for k in range(TOP_K):
                row = dest_ref[0, 0, k * TM + i * SUBLANES + j]
                pltpu.make_async_copy(yb_hbm.at[pl.ds(row, 1)], ybuf.at[slot, k, i, pl.ds(j, 1)],
                                      sem.at[slot]).start(priority=k % 2)
        return carry

    lax.fori_loop(0, TM // SUBLANES, body, 0)


def _moe_gather(dcur_ref, dnxt_ref, yb_hbm, ybuf, sem):
    t = pl.program_id(0)
    slot = t % 2

    @pl.when(t == 0)
    def _():
        _gather_start(dcur_ref, yb_hbm, ybuf, sem, 0)

    @pl.when(t + 1 < pl.num_programs(0))
    def _():
        _gather_start(dnxt_ref, yb_hbm, ybuf, sem, 1 - slot)

    for k in range(TOP_K):
        pltpu.make_async_copy(ybuf.at[slot, k], ybuf.at[slot, k], sem.at[slot]).wait()
    return slot


def _combined_rows(cw_ref, ybuf, slot, sl):
    d = ybuf.shape[-1]
    g = slice(sl.start // SUBLANES, sl.stop // SUBLANES)
    w = cw_ref[sl, :]
    y = w[:, 0:1] * ybuf[slot, 0, g].reshape(sl.stop - sl.start, d)
    for k in range(1, TOP_K):
        y = y + w[:, k:k + 1] * ybuf[slot, k, g].reshape(sl.stop - sl.start, d)
    return y


def _dispatch_kernel(dest_ref, pend_ref, h2_hbm, xb_hbm, stage, zeros, in_sem, row_sem, zero_sem):
    t = pl.program_id(0)
    nt = pl.num_programs(0)
    slot = t % DISPATCH_SLOTS

    tile_rows = TM // SUBLANES

    def stage_in(tile_idx, s):
        return pltpu.make_async_copy(h2_hbm.at[pl.ds(tile_idx * tile_rows, tile_rows)], stage.at[s], in_sem.at[s])

    def drain_rows(s):
        for _ in range(TOP_K):
            pltpu.make_async_copy(stage.at[s], stage.at[s], row_sem.at[s]).wait()

    def zero_block(start):
        return pltpu.make_async_copy(zeros, xb_hbm.at[pl.ds(pl.multiple_of(start, MOE_BLK), MOE_BLK)], zero_sem)

    def zero_fill(op):
        for e in range(N_EXPERTS):
            op(zero_block(jnp.maximum(pend_ref[e] - MOE_BLK, 0)))
        for j in range(N_EXPERTS):
            start = pend_ref[N_EXPERTS - 1] + j * MOE_BLK

            @pl.when(start < xb_hbm.shape[0])
            def _():
                op(zero_block(start))

    @pl.when(t == 0)
    def _():
        stage_in(0, 0).start()
        zeros[...] = jnp.zeros(zeros.shape, zeros.dtype)
        zero_fill(lambda cp: cp.start())
        zero_fill(lambda cp: cp.wait())

    @pl.when(jnp.logical_and(t == 0, nt > 1))
    def _():
        stage_in(1, 1).start()

    stage_in(t, slot).wait()

    def body(i, carry):
        for j in range(SUBLANES):
            for k in range(TOP_K):
                row = dest_ref[0, 0, k * TM + i * SUBLANES + j]
                pltpu.make_async_copy(stage.at[slot, i, pl.ds(j, 1)], xb_hbm.at[pl.ds(row, 1)],
                                      row_sem.at[slot]).start(priority=k % 2)
        return carry

    lax.fori_loop(0, tile_rows, body, 0)

    @pl.when(t > 0)
    def _():
        drain_rows((t + DISPATCH_SLOTS - 1) % DISPATCH_SLOTS)

    @pl.when(t + 2 < nt)
    def _():
        stage_in(t + 2, (t + 2) % DISPATCH_SLOTS).start()

    @pl.when(t == nt - 1)
    def _():
        drain_rows(slot)


def _dispatch(dest3, pend, h2, p_len):
    n, d = h2.shape
    return pl.pallas_call(
        _dispatch_kernel,
        grid=(n // TM,),
        in_specs=[pl.BlockSpec((1, 1, TOP_K * TM), lambda t: (t, 0, 0), memory_space=pltpu.SMEM),
                  pl.BlockSpec(memory_space=pltpu.SMEM), pl.BlockSpec(memory_space=pl.ANY)],
        out_specs=pl.BlockSpec(memory_space=pl.ANY),
        out_shape=jax.ShapeDtypeStruct((p_len, d), h2.dtype),
        scratch_shapes=[pltpu.VMEM((DISPATCH_SLOTS, TM // SUBLANES, SUBLANES, d), h2.dtype),
                        pltpu.VMEM((MOE_BLK, d), h2.dtype),
                        pltpu.SemaphoreType.DMA((DISPATCH_SLOTS,)), pltpu.SemaphoreType.DMA((DISPATCH_SLOTS,)),
                        pltpu.SemaphoreType.DMA(())],
        compiler_params=_cparams(("arbitrary",)),
        name="moe_dispatch",
    )(dest3, pend, h2.reshape(n // SUBLANES, SUBLANES, d))


def _pre_odd_kernel(x_ref, dcur_ref, dnxt_ref, cw_ref, yb_hbm, modp_ref, mod_ref, g_ref, w_ref,
                    x2_ref, xc_ref, p_ref, bg_ref, ybuf, sem, *, cw):
    slot = _moe_gather(dcur_ref, dnxt_ref, yb_hbm, ybuf, sem)
    for sl in _row_parts(PROJ_ROW_PARTS):
        x2 = x_ref[sl, :] + modp_ref[0, 5:6, :] * _combined_rows(cw_ref, ybuf, slot, sl)
        x2_ref[sl, :] = x2
        h = _rms_mod(x2, g_ref[...], mod_ref[0, 1:2, :], mod_ref[0, 0:1, :])
        proj = jnp.dot(h.astype(BF16), w_ref[...], preferred_element_type=F32)
        xc_ref[sl, :] = proj[:, :cw]
        hd = proj[:, cw:2 * cw]
        bg_ref[sl, :] = proj[:, 2 * cw:3 * cw]
        p_ref[sl, :] = proj[:, 3 * cw:] * hd


def _post_odd_kernel(x_ref, xc_ref, xcp_ref, xcn_ref, p_ref, pp_ref, pn_ref, bg_ref, mod_ref,
                     pw_ref, ps_ref, dw_ref, wo_ref, g2_ref, wr_ref,
                     x3_ref, h2_ref, lg_ref, extc_sc, extp_sc, *, npt, tps):
    i = pl.program_id(0)
    first, last = _seq_edges(i, npt, tps)
    _fill_ext(extc_sc, xc_ref, xcp_ref, xcn_ref, first, last)
    _fill_ext(extp_sc, p_ref, pp_ref, pn_ref, first, last)
    cw = xc_ref.shape[1]
    gc = cw // len(POOL_WINDOWS)
    is_prompt = i < npt
    t_len = jnp.where(is_prompt, TM, TM * tps)
    tile_pos = jnp.where(is_prompt, 0, ((i - npt) % tps) * TM)
    for rs in _row_parts(MIX_ROW_PARTS):
        nrows = rs.stop - rs.start
        base = HALO + rs.start
        pos = lax.broadcasted_iota(jnp.int32, (nrows, 1), 0) + (tile_pos + rs.start)
        ds = []
        for g, w in enumerate(POOL_WINDOWS):
            sl = slice(g * gc, (g + 1) * gc)
            acc = None
            for o in range(-(w // 2), w // 2):
                term = extc_sc[base + o:base + o + nrows, sl]
                acc = term if acc is None else acc + term
            lo = jnp.maximum(pos - w // 2, 0)
            hi = jnp.minimum(pos + w // 2 - 1, t_len - 1)
            cnt = (hi - lo + 1).astype(F32)
            ds.append(acc / cnt - extc_sc[base:base + nrows, sl])
        d = jnp.concatenate(ds, axis=1).astype(BF16)
        yc = jnp.dot(d, pw_ref[...], preferred_element_type=F32) * ps_ref[...]
        conv = None
        for k in range(SCONV_K):
            start = base - SCONV_K // 2 + k
            term = extp_sc[start:start + nrows, :] * dw_ref[k:k + 1, :]
            conv = term if conv is None else conv + term
        yd = bg_ref[rs, :] * conv
        m = (jnp.dot(yc.astype(BF16), wo_ref[0:cw, :], preferred_element_type=F32)
             + jnp.dot(yd.astype(BF16), wo_ref[cw:, :], preferred_element_type=F32))
        x3 = x_ref[rs, :] + mod_ref[0, 2:3, :] * m
        x3_ref[rs, :] = x3
        _moe_prologue(x3, rs, mod_ref, g2_ref, wr_ref, h2_ref, lg_ref)


def _final_kernel(x_ref, dcur_ref, dnxt_ref, cw_ref, yb_hbm, modp_ref, g_ref, op_ref, os_ref, ybuf, sem, *, npt):
    i = pl.program_id(0)
    slot = _moe_gather(dcur_ref, dnxt_ref, yb_hbm, ybuf, sem)
    x = x_ref[...] + modp_ref[0, 5:6, :] * _combined_rows(cw_ref, ybuf, slot, slice(0, TM))
    ms = jnp.mean(x * x, axis=-1, keepdims=True)
    out = x * lax.rsqrt(ms + EPS) * g_ref[...]

    @pl.when(i < npt)
    def _():
        op_ref[...] = out

    @pl.when(i >= npt)
    def _():
        os_ref[...] = out


def _expert_kernel(be_ref, nv_ref, x_ref, wg_ref, wu_ref, wd_ref, y_ref, wg_sc, wu_sc, wd_sc):
    i = pl.program_id(0)

    @pl.when(jnp.logical_or(i == 0, be_ref[i] != be_ref[jnp.maximum(i - 1, 0)]))
    def _():
        wg_sc[...] = wg_ref[0, 0].astype(BF16)
        wu_sc[...] = wu_ref[0, 0].astype(BF16)
        wd_sc[...] = wd_ref[0, 0].astype(BF16)

    @pl.when(i < nv_ref[0])
    def _():
        x = x_ref[...].astype(BF16)
        hg = jnp.dot(x, wg_sc[...], preferred_element_type=F32)
        hu = jnp.dot(x, wu_sc[...], preferred_element_type=F32)
        hid = (_silu(hg) * hu).astype(BF16)
        y_ref[...] = jnp.dot(hid, wd_sc[...], preferred_element_type=F32)

    @pl.when(i >= nv_ref[0])
    def _():
        y_ref[...] = jnp.zeros(y_ref.shape, F32)


def _experts(blk_e, nvalid, xb, layer, wg, wu, wd):
    p_len, d = xb.shape
    hid = wg.shape[3]
    nblk = p_len // MOE_BLK
    grid_spec = pltpu.PrefetchScalarGridSpec(
        num_scalar_prefetch=2,
        grid=(nblk,),
        in_specs=[pl.BlockSpec((MOE_BLK, d), lambda i, be, nv: (jnp.minimum(i, nv[0] - 1), 0)),
                  pl.BlockSpec((1, 1, d, hid), lambda i, be, nv: (layer, be[i], 0, 0)),
                  pl.BlockSpec((1, 1, d, hid), lambda i, be, nv: (layer, be[i], 0, 0)),
                  pl.BlockSpec((1, 1, hid, d), lambda i, be, nv: (layer, be[i], 0, 0))],
        out_specs=pl.BlockSpec((MOE_BLK, d), lambda i, be, nv: (i, 0)),
        scratch_shapes=[pltpu.VMEM((d, hid), BF16), pltpu.VMEM((d, hid), BF16), pltpu.VMEM((hid, d), BF16)],
    )
    return pl.pallas_call(
        _expert_kernel,
        grid_spec=grid_spec,
        out_shape=jax.ShapeDtypeStruct((p_len, d), F32),
        compiler_params=_cparams(("arbitrary",)),
        name="moe_experts",
    )(blk_e, nvalid, xb, wg, wu, wd)


def _route_kernel(lg_ref, b_ref, info_ref, cnt_ref):
    lt = lg_ref[...].T + b_ref[...]
    gl = lt[N_EXPERTS:N_EXPERTS + N_GROUPS]
    ge = jnp.exp(gl - jnp.max(gl, axis=0, keepdims=True))
    gp = ge / jnp.sum(ge, axis=0, keepdims=True)
    best = gp[0:1]
    gidx = jnp.zeros(best.shape, jnp.int32)
    el = lt[0:EXPERTS_PER_GROUP]
    for g in range(1, N_GROUPS):
        better = gp[g:g + 1] > best
        gidx = jnp.where(better, g, gidx)
        best = jnp.where(better, gp[g:g + 1], best)
    for g in range(1, N_GROUPS):
        el = jnp.where(gidx == g, lt[g * EXPERTS_PER_GROUP:(g + 1) * EXPERTS_PER_GROUP], el)
    rio = lax.broadcasted_iota(jnp.int32, el.shape, 0)
    v0 = jnp.max(el, axis=0, keepdims=True)
    i0 = jnp.min(jnp.where(el == v0, rio, EXPERTS_PER_GROUP), axis=0, keepdims=True)
    el2 = jnp.where(rio == i0, -jnp.inf, el)
    v1 = jnp.max(el2, axis=0, keepdims=True)
    i1 = jnp.min(jnp.where(el2 == v1, rio, EXPERTS_PER_GROUP), axis=0, keepdims=True)
    e1 = jnp.exp(v1 - v0)
    den = 1.0 + e1
    w0 = (1.0 / den) * best
    w1 = (e1 / den) * best
    eid0 = gidx * EXPERTS_PER_GROUP + i0
    eid1 = gidx * EXPERTS_PER_GROUP + i1
    tm = lt.shape[1]
    eio = lax.broadcasted_iota(jnp.int32, (N_EXPERTS, tm), 0)
    oh0 = eio == eid0
    oh1 = eio == eid1
    ohs = jnp.where(oh0, 1.0, 0.0) + jnp.where(oh1, 1.0, 0.0)
    upper = (lax.broadcasted_iota(jnp.int32, (tm, tm), 0)
             < lax.broadcasted_iota(jnp.int32, (tm, tm), 1)).astype(BF16)
    before = jnp.dot(ohs.astype(BF16), upper, preferred_element_type=F32)
    rank0 = jnp.sum(jnp.where(oh0, before, 0.0), axis=0, keepdims=True)
    rank1 = jnp.sum(jnp.where(oh1, before, 0.0), axis=0, keepdims=True)
    info_ref[0] = jnp.concatenate([eid0.astype(F32), eid1.astype(F32), w0, w1, rank0, rank1,
                                   jnp.zeros((2, tm), F32)], axis=0)
    cnt_ref[0] = jnp.sum(ohs, axis=1, keepdims=True)


def _route(logits, bias_col):
    n = logits.shape[0]
    nt = n // TM
    return pl.pallas_call(
        _route_kernel,
        grid=(nt,),
        in_specs=[pl.BlockSpec((TM, ROUTER_PAD), lambda i: (i, 0)),
                  pl.BlockSpec((ROUTER_PAD, 1), lambda i: (0, 0))],
        out_specs=[pl.BlockSpec((1, 8, TM), lambda i: (i, 0, 0)),
                   pl.BlockSpec((1, N_EXPERTS, 1), lambda i: (i, 0, 0))],
        out_shape=[jax.ShapeDtypeStruct((nt, 8, TM), F32), jax.ShapeDtypeStruct((nt, N_EXPERTS, 1), F32)],
        compiler_params=_cparams(("parallel",)),
        name="moe_route",
    )(logits, bias_col)


def _moe(h2, logits, bias_col, layer, wg, wu, wd):
    n, d = h2.shape
    info, cnt = _route(logits, bias_col)
    cnt = cnt[:, :, 0].astype(jnp.int32)
    counts = jnp.sum(cnt, axis=0)
    pc = ((counts + MOE_BLK - 1) // MOE_BLK) * MOE_BLK
    pend = jnp.cumsum(pc)
    base = (pend - pc)[None, :] + jnp.cumsum(cnt, axis=0) - cnt
    eid = info[:, 0:2, :].astype(jnp.int32)
    rank = info[:, 4:6, :].astype(jnp.int32)
    sel = eid[..., None] == jnp.arange(N_EXPERTS, dtype=jnp.int32)
    dest = jnp.sum(jnp.where(sel, base[:, None, None, :], 0), axis=-1) + rank
    p_len = n * TOP_K + N_EXPERTS * MOE_BLK
    nblk = p_len // MOE_BLK
    blk_start = jnp.arange(nblk, dtype=jnp.int32) * MOE_BLK
    blk_e = jnp.minimum(jnp.sum(pend[None, :] <= blk_start[:, None], axis=1), N_EXPERTS - 1).astype(jnp.int32)
    nvalid = (pend[-1:] // MOE_BLK).astype(jnp.int32)
    dest3 = dest.reshape(dest.shape[0], 1, TOP_K * TM)
    wcol = jnp.transpose(info[:, 2:2 + TOP_K, :], (0, 2, 1)).reshape(n, TOP_K)
    xb = _dispatch(dest3, pend.astype(jnp.int32), h2, p_len)
    yb = _experts(blk_e, nvalid, xb, layer, wg, wu, wd)
    return dest3, wcol, yb


def _rope_tables(t_len, qw):
    rows = t_len // GRID_W
    ax = qw // (DIFF_HEADS * 2) // 2
    row = jnp.repeat(jnp.arange(rows, dtype=F32), GRID_W)
    col = jnp.tile(jnp.arange(GRID_W, dtype=F32), rows)
    inv = ROPE_BASE ** (-jnp.arange(0, ax, 2, dtype=F32) / ax)
    ar = row[:, None] * inv[None, :]
    ac = col[:, None] * inv[None, :]
    cr, sr, cc, sc = jnp.cos(ar), jnp.sin(ar), jnp.cos(ac), jnp.sin(ac)
    cos = jnp.concatenate([cr, cr, cc, cc], axis=1)
    sins = jnp.concatenate([-sr, sr, -sc, sc], axis=1)
    reps = qw // cos.shape[1]
    cos = jnp.concatenate([jnp.ones((TM, cos.shape[1]), F32), cos], axis=0)
    sins = jnp.concatenate([jnp.zeros((TM, sins.shape[1]), F32), sins], axis=0)
    return jnp.tile(cos, (1, reps)), jnp.tile(sins, (1, reps))


def kernel(x_prompt, x_sample, cache_k, cache_v, c, c_ctx, w_ada, b_ada, g_norm1, g_norm2, g_final, w_in_e, conv_a_w, conv_a_b, ln_a_g, ln_a_b, lam_q1, lam_k1, lam_q2, lam_k2, subln_g, w_out_e, w_in_o, pool_w, pool_scale, conv_d_w, w_out_o, w_router_g, b_router_g, w_router_e, b_router_e, w_gate, w_up, w_down):
    bp, tp, d = x_prompt.shape
    bs, ts, _ = x_sample.shape
    past = cache_k.shape[2]
    assert tp == TM and ts % TM == 0 and d % LANES == 0
    npt = bp
    tps = ts // TM
    n_p, n_s = bp * tp, bs * ts
    n = n_p + n_s
    nt = n // TM
    cw = conv_a_w.shape[2]
    qw = (w_in_e.shape[2] - 2 * cw) // 3
    dh = qw // (DIFF_HEADS * 2)
    hpt = TM // HALO

    def cond_of(i):
        return jnp.where(i < npt, 0, 1 + (i - npt) // tps)

    def tile(w):
        return pl.BlockSpec((TM, w), lambda i: (i, 0))

    def halo_prev(w):
        return pl.BlockSpec((HALO, w), lambda i: (jnp.maximum(i * hpt - 1, 0), 0))

    def halo_next(w):
        return pl.BlockSpec((HALO, w), lambda i: (jnp.minimum((i + 1) * hpt, nt * hpt - 1), 0))

    def whole(shape):
        return pl.BlockSpec(shape, lambda i: (0,) * len(shape))

    mod_spec = pl.BlockSpec((1, 6, d), lambda i: (cond_of(i), 0, 0))
    combine_specs = [pl.BlockSpec((1, 1, TOP_K * TM), lambda i: (i, 0, 0), memory_space=pltpu.SMEM),
                     pl.BlockSpec((1, 1, TOP_K * TM), lambda i: (jnp.minimum(i + 1, nt - 1), 0, 0),
                                  memory_space=pltpu.SMEM),
                     tile(TOP_K), pl.BlockSpec(memory_space=pl.ANY)]
    combine_scratch = [pltpu.VMEM((2, TOP_K, TM // SUBLANES, SUBLANES, d), F32), pltpu.SemaphoreType.DMA((2,))]

    rows = 8 * ((1 + bs + 7) // 8)
    cond = jnp.concatenate([c_ctx[None, :], c, jnp.zeros((rows - 1 - bs, d), F32)], axis=0)
    mod = _ada_table(cond, w_ada, b_ada)

    xp2, xs2 = x_prompt.reshape(n_p, d), x_sample.reshape(n_s, d)

    def prompt_tile(w):
        return pl.BlockSpec((TM, w), lambda i: (jnp.minimum(i, npt - 1), 0))

    def sample_tile(w):
        return pl.BlockSpec((TM, w), lambda i: (jnp.maximum(i - npt, 0), 0))

    def router_w(i):
        wre = jnp.transpose(w_router_e[i], (1, 0, 2)).reshape(d, N_EXPERTS)
        wr = jnp.concatenate([wre, w_router_g[i]], axis=1)
        return jnp.pad(wr, ((0, 0), (0, ROUTER_PAD - wr.shape[1]))).astype(BF16)

    def router_b(i):
        b = jnp.concatenate([b_router_e[i].reshape(N_EXPERTS), b_router_g[i]])
        return jnp.pad(b, (0, ROUTER_PAD - b.shape[0]))[:, None]

    cos, sins = _rope_tables(ts, qw)
    tab_spec = pl.BlockSpec((TM, qw), lambda i: (jnp.where(i < npt, 0, 1 + (i - npt) % tps), 0))
    cache_spec = pl.BlockSpec((TM, qw), lambda i: (jnp.minimum(i, npt - 1), 0))
    u, q, k, vt, kc, vc = pl.pallas_call(
        functools.partial(_pre_even_kernel, npt=npt, cw=cw, qw=qw, qscale=dh ** -0.5 * math.log2(math.e)),
        grid=(nt,),
        in_specs=[prompt_tile(d), sample_tile(d), mod_spec, whole((1, d)), whole(w_in_e.shape[1:]),
                  tab_spec, tab_spec],
        out_specs=[tile(cw), tile(qw), tile(qw), pl.BlockSpec((qw, TM), lambda i: (0, i)),
                   cache_spec, cache_spec],
        out_shape=[jax.ShapeDtypeStruct((n, cw), F32), jax.ShapeDtypeStruct((n, qw), BF16),
                   jax.ShapeDtypeStruct((n, qw), BF16), jax.ShapeDtypeStruct((qw, n), BF16),
                   jax.ShapeDtypeStruct((n_p, qw), F32), jax.ShapeDtypeStruct((n_p, qw), F32)],
        scratch_shapes=[pltpu.VMEM((2, TM, qw), F32)],
        compiler_params=_cparams(("arbitrary",)),
        name="pre_even",
    )(xp2, xs2, mod[0], g_norm1[0][None, :], w_in_e[0].astype(BF16), cos, sins)

    lam_init = 0.8 - 0.6 * math.exp(-0.3 * 0)
    lam = (jnp.exp(jnp.sum(lam_q1[0] * lam_k1[0])) - jnp.exp(jnp.sum(lam_q2[0] * lam_k2[0])) + lam_init)
    lam = lam.reshape(1).astype(F32)
    sub_g = subln_g[0][:, None]
    o_p = _attention(lam, q, k, vt, sub_g, None, batch=bp, t_len=tp, row0=0, tq=TM, tk=TM,
                     post_scale=1.0 - lam_init)
    k_ctx = cache_k[:, 0].reshape(bs * past, qw).astype(BF16)
    vt_ctx = jnp.transpose(cache_v[:, 0].reshape(bs, past, qw), (0, 2, 1)).astype(BF16)
    o_s = _attention(lam, q, k, vt, sub_g, (k_ctx, vt_ctx), batch=bs, t_len=ts, row0=n_p, tq=1024, tk=512,
                     post_scale=1.0 - lam_init)

    moe_outs = [jax.ShapeDtypeStruct((n, d), F32), jax.ShapeDtypeStruct((n, d), F32),
                jax.ShapeDtypeStruct((n, ROUTER_PAD), F32)]
    moe_out_specs = [tile(d), tile(d), tile(ROUTER_PAD)]
    x1, h2, logits = pl.pallas_call(
        functools.partial(_post_even_kernel, npt=npt, tps=tps),
        grid=(nt,),
        in_specs=[prompt_tile(d), sample_tile(d), tile(cw), halo_prev(cw), halo_next(cw),
                  prompt_tile(qw), sample_tile(qw), mod_spec,
                  whole((CONV_K, cw)), whole((1, cw)), whole((1, cw)), whole((1, cw)),
                  whole(w_out_e.shape[1:]), whole((1, d)), whole((d, ROUTER_PAD))],
        out_specs=moe_out_specs,
        out_shape=moe_outs,
        scratch_shapes=[pltpu.VMEM((TM + 2 * HALO, cw), F32),
                        pltpu.VMEM((SUBLANES, TM + 2 * HALO - SUBLANES, cw), F32)],
        compiler_params=_cparams(("parallel",)),
        name="post_even",
    )(xp2, xs2, u, u, u, o_p, o_s, mod[0], conv_a_w[0], conv_a_b[0][None, :], ln_a_g[0][None, :], ln_a_b[0][None, :],
      w_out_e[0].astype(BF16), g_norm2[0][None, :], router_w(0))
    dest3, wcol, yb = _moe(h2, logits, router_b(0), 0, w_gate, w_up, w_down)

    pw = pool_w.shape[2]
    x2, xc, p, bg = pl.pallas_call(
        functools.partial(_pre_odd_kernel, cw=cw),
        grid=(nt,),
        in_specs=[tile(d)] + combine_specs + [mod_spec, mod_spec, whole((1, d)), whole(w_in_o.shape[1:])],
        out_specs=[tile(d), tile(cw), tile(cw), tile(cw)],
        out_shape=[jax.ShapeDtypeStruct((n, d), F32)] + [jax.ShapeDtypeStruct((n, cw), F32)] * 3,
        scratch_shapes=combine_scratch,
        compiler_params=_cparams(("arbitrary",)),
        name="pre_odd",
    )(x1, dest3, dest3, wcol, yb, mod[0], mod[1], g_norm1[1][None, :], w_in_o[0].astype(BF16))
    pool_bd = jnp.zeros((cw, cw), F32)
    for g in range(len(POOL_WINDOWS)):
        pool_bd = pool_bd.at[g * pw:(g + 1) * pw, g * pw:(g + 1) * pw].set(pool_w[0, g])
    x3, h2, logits = pl.pallas_call(
        functools.partial(_post_odd_kernel, npt=npt, tps=tps),
        grid=(nt,),
        in_specs=[tile(d), tile(cw), halo_prev(cw), halo_next(cw), tile(cw), halo_prev(cw), halo_next(cw),
                  tile(cw), mod_spec, whole((cw, cw)), whole((1, cw)), whole((SCONV_K, cw)),
                  whole(w_out_o.shape[1:]), whole((1, d)), whole((d, ROUTER_PAD))],
        out_specs=moe_out_specs,
        out_shape=moe_outs,
        scratch_shapes=[pltpu.VMEM((TM + 2 * HALO, cw), F32), pltpu.VMEM((TM + 2 * HALO, cw), F32)],
        compiler_params=_cparams(("parallel",)),
        name="post_odd",
    )(x2, xc, xc, xc, p, p, p, bg, mod[1], pool_bd.astype(BF16), pool_scale[0][None, :], conv_d_w[0],
      w_out_o[0].astype(BF16), g_norm2[1][None, :], router_w(1))
    dest3, wcol, yb = _moe(h2, logits, router_b(1), 1, w_gate, w_up, w_down)

    out_p, out_s = pl.pallas_call(
        functools.partial(_final_kernel, npt=npt),
        grid=(nt,),
        in_specs=[tile(d)] + combine_specs + [mod_spec, whole((1, d))],
        out_specs=[pl.BlockSpec((TM, d), lambda i: (jnp.minimum(i, npt - 1), 0)),
                   pl.BlockSpec((TM, d), lambda i: (jnp.maximum(i - npt, 0), 0))],
        out_shape=[jax.ShapeDtypeStruct((n_p, d), F32), jax.ShapeDtypeStruct((n_s, d), F32)],
        scratch_shapes=combine_scratch,
        compiler_params=_cparams(("arbitrary",)),
        name="final_norm",
    )(x3, dest3, dest3, wcol, yb, mod[1], g_final[None, :])

    new_k = kc.reshape(bp, 1, tp, DIFF_HEADS, 2, dh)
    new_v = vc.reshape(bp, 1, tp, DIFF_HEADS, 2 * dh)
    return (out_p.reshape(bp, tp, d), out_s.reshape(bs, ts, d), new_k, new_v)
```

```python
import functools
import math

import jax
import jax.numpy as jnp
from jax import lax
from jax.experimental import pallas as pl
from jax.experimental.pallas import tpu as pltpu

F32 = jnp.float32
BF16 = jnp.bfloat16

EPS = 1e-6
GRID_W = 64
ROPE_BASE = 10000.0
CONV_K = 31
SCONV_K = 3
POOL_WINDOWS = (2, 4, 8, 16)
DIFF_HEADS = 4
N_GROUPS = 4
EXPERTS_PER_GROUP = 8
N_EXPERTS = N_GROUPS * EXPERTS_PER_GROUP
TOP_K = 2

LANES = 128
SUBLANES = 8
TM = 256
HALO = 16
PROJ_ROW_PARTS = 2
MIX_ROW_PARTS = 1
MOE_BLK = 512
ROUTER_PAD = 128
ONES_ROWS = 16
DISPATCH_SLOTS = 3
VMEM_LIMIT = 56 * 1024 * 1024


def _cparams(sem):
    return pltpu.CompilerParams(dimension_semantics=sem, vmem_limit_bytes=VMEM_LIMIT)


def _rms_mod(x, g, sc, sh):
    ms = jnp.mean(x * x, axis=-1, keepdims=True)
    return (x * lax.rsqrt(ms + EPS)) * g * (1.0 + sc) + sh


def _silu(x):
    return x * jax.nn.sigmoid(x)


def _ada_kernel(c_ref, w_ref, b_ref, o_ref):
    cs = _silu(c_ref[...])
    o_ref[0] = jnp.dot(cs, w_ref[0], precision=lax.Precision.HIGHEST,
                       preferred_element_type=F32) + b_ref[0]


def _ada_table(cond, w_ada, b_ada):
    depth, d, six_d = w_ada.shape
    rows = cond.shape[0]
    nj = six_d // d
    out = pl.pallas_call(
        _ada_kernel,
        grid=(depth, nj),
        in_specs=[pl.BlockSpec((rows, d), lambda l, j: (0, 0)),
                  pl.BlockSpec((1, d, d), lambda l, j: (l, 0, j)),
                  pl.BlockSpec((1, 1, d), lambda l, j: (l, 0, j))],
        out_specs=pl.BlockSpec((1, rows, d), lambda l, j: (l, 0, j)),
        out_shape=jax.ShapeDtypeStruct((depth, rows, six_d), F32),
        compiler_params=_cparams(("parallel", "parallel")),
        name="ada_table",
    )(cond, w_ada, b_ada.reshape(depth, 1, six_d))
    return out.reshape(depth, rows, nj, d)


def _row_parts(parts):
    rows = TM // parts
    return [slice(p * rows, (p + 1) * rows) for p in range(parts)]


def _rope(x, cos, sins, first_half):
    outs = []
    for j in range(x.shape[1] // LANES):
        sl = slice(j * LANES, (j + 1) * LANES)
        xs = x[:, sl]
        nxt = pltpu.roll(xs, LANES - 16, 1)
        prv = pltpu.roll(xs, 16, 1)
        rot = jnp.where(first_half, nxt, prv)
        outs.append(xs * cos[:, sl] + rot * sins[:, sl])
    return jnp.concatenate(outs, axis=1)


def _pre_even_kernel(xp_ref, xs_ref, mod_ref, g_ref, w_ref, cos_ref, sin_ref,
                     u_ref, q_ref, k_ref, vt_ref, kc_ref, vc_ref, kv_sc, *, npt, cw, qw, qscale):
    i = pl.program_id(0)
    is_prompt = i < npt
    lane = lax.broadcasted_iota(jnp.int32, (1, LANES), 1)
    first_half = (lane % 32) < 16
    for sl in _row_parts(PROJ_ROW_PARTS):
        x = jnp.where(is_prompt, xp_ref[sl, :], xs_ref[sl, :])
        h = _rms_mod(x, g_ref[...], mod_ref[0, 1:2, :], mod_ref[0, 0:1, :])
        proj = jnp.dot(h.astype(BF16), w_ref[...], preferred_element_type=F32)
        ga = proj[:, :cw]
        gb = proj[:, cw:2 * cw]
        u_ref[sl, :] = ga * jax.nn.sigmoid(gb)
        q = proj[:, 2 * cw:2 * cw + qw]
        k = proj[:, 2 * cw + qw:2 * cw + 2 * qw]
        v = proj[:, 2 * cw + 2 * qw:]
        kv_sc[0, sl, :] = k
        kv_sc[1, sl, :] = v
        cos = cos_ref[sl, :]
        sins = sin_ref[sl, :]
        q_ref[sl, :] = (_rope(q, cos, sins, first_half) * qscale).astype(BF16)
        k_ref[sl, :] = _rope(k, cos, sins, first_half).astype(BF16)
        vt_ref[:, sl] = v.T.astype(BF16)

    @pl.when(is_prompt)
    def _():
        kc_ref[...] = kv_sc[0]
        vc_ref[...] = kv_sc[1]


def _attn_kernel(lam_ref, q_ref, k_ref, vt_ref, *rest, nk, tk, nk_ctx, post_scale):
    if nk_ctx:
        kc_ref, vtc_ref, g_ref, o_ref, s_sc, mc_sc, m_sc, acc_sc = rest
    else:
        g_ref, o_ref, s_sc, mc_sc, m_sc, acc_sc = rest
    q = q_ref[...]
    hw = q.shape[1]
    dh = hw // 2
    lane = lax.broadcasted_iota(jnp.int32, (1, hw), 1)
    zero = jnp.zeros_like(q)
    qs = (jnp.where(lane < dh, q, zero), jnp.where(lane >= dh, q, zero))
    m_sc[...] = jnp.full(m_sc.shape, -jnp.inf, F32)
    acc_sc[...] = jnp.zeros(acc_sc.shape, F32)
    ones_rows = (lax.broadcasted_iota(jnp.int32, (ONES_ROWS, tk), 0) == 0).astype(BF16)

    def scores(kk, slot):
        for c in range(2):
            s = lax.dot_general(kk, qs[c], (((1,), (1,)), ((), ())), preferred_element_type=F32)
            s_sc[slot, c] = s
            mc_sc[slot, c] = jnp.max(s, axis=0, keepdims=True)

    def consume(vt, slot):
        vta = jnp.concatenate([vt, ones_rows], axis=0)
        for c in range(2):
            m_prev = m_sc[c]
            m_new = jnp.maximum(m_prev, mc_sc[slot, c])
            alpha = jnp.exp2(m_prev - m_new)
            p = jnp.exp2(s_sc[slot, c] - m_new).astype(BF16)
            acc_sc[c] = alpha * acc_sc[c] + jnp.dot(vta, p, preferred_element_type=F32)
            m_sc[c] = m_new

    def k_at(i):
        if i < nk_ctx:
            return kc_ref[i * tk:(i + 1) * tk, :]
        return k_ref[(i - nk_ctx) * tk:(i - nk_ctx + 1) * tk, :]

    def vt_at(i):
        if i < nk_ctx:
            return vtc_ref[0, :, i * tk:(i + 1) * tk]
        return vt_ref[:, (i - nk_ctx) * tk:(i - nk_ctx + 1) * tk]

    n_total = nk_ctx + nk
    peel = nk_ctx + (n_total - 1 - nk_ctx) % 2
    scores(k_at(0), 0)
    for i in range(peel):
        scores(k_at(i + 1), (i + 1) % 2)
        consume(vt_at(i), i % 2)

    def body(t, carry):
        for r in range(2):
            j = 2 * t + (peel + r - nk_ctx)
            nxt = pl.multiple_of((j + 1) * tk, tk)
            cur = pl.multiple_of(j * tk, tk)
            scores(k_ref[pl.ds(nxt, tk), :], (peel + r + 1) % 2)
            consume(vt_ref[:, pl.ds(cur, tk)], (peel + r) % 2)
        return carry

    lax.fori_loop(0, (n_total - 1 - peel) // 2, body, 0)
    consume(vt_at(n_total - 1), (n_total - 1) % 2)
    o = (acc_sc[0, :hw, :] / acc_sc[0, hw:hw + 1, :]
         - lam_ref[0] * (acc_sc[1, :hw, :] / acc_sc[1, hw:hw + 1, :]))
    ms = jnp.mean(o * o, axis=0, keepdims=True)
    o = o * lax.rsqrt(ms + EPS) * g_ref[...] * post_scale
    o_ref[...] = o.T.astype(BF16)


def _attention(lam, q, k, vt, g_col, ctx, *, batch, t_len, row0, tq, tk, post_scale):
    hw = q.shape[1] // DIFF_HEADS
    nq = t_len // tq
    assert row0 % t_len == 0 and t_len % tk == 0
    qb0 = row0 // tq
    sb0 = row0 // t_len
    in_specs = [pl.BlockSpec(memory_space=pltpu.SMEM),
                pl.BlockSpec((tq, hw), lambda b, h, i: (qb0 + b * nq + i, h)),
                pl.BlockSpec((t_len, hw), lambda b, h, i: (sb0 + b, h)),
                pl.BlockSpec((hw, t_len), lambda b, h, i: (h, sb0 + b))]
    args = [lam, q, k, vt]
    nk_ctx = 0
    if ctx is not None:
        k_ctx, vt_ctx = ctx
        past = vt_ctx.shape[2]
        assert past % tk == 0
        nk_ctx = past // tk
        in_specs += [pl.BlockSpec((past, hw), lambda b, h, i: (b, h)),
                     pl.BlockSpec((1, hw, past), lambda b, h, i: (b, h, 0))]
        args += [k_ctx, vt_ctx]
    in_specs.append(pl.BlockSpec((hw, 1), lambda b, h, i: (0, 0)))
    args.append(g_col)
    return pl.pallas_call(
        functools.partial(_attn_kernel, nk=t_len // tk, tk=tk, nk_ctx=nk_ctx, post_scale=post_scale),
        grid=(batch, DIFF_HEADS, nq),
        in_specs=in_specs,
        out_specs=pl.BlockSpec((tq, hw), lambda b, h, i: (b * nq + i, h)),
        out_shape=jax.ShapeDtypeStruct((batch * t_len, q.shape[1]), BF16),
        scratch_shapes=[pltpu.VMEM((2, 2, tk, tq), F32), pltpu.VMEM((2, 2, 1, tq), F32),
                        pltpu.VMEM((2, 1, tq), F32), pltpu.VMEM((2, hw + ONES_ROWS, tq), F32)],
        compiler_params=_cparams(("parallel", "parallel", "parallel")),
        name="diff_attention",
    )(*args)


def _seq_edges(i, npt, tps):
    r = (i - npt) % tps
    is_prompt = i < npt
    first = jnp.logical_or(is_prompt, r == 0)
    last = jnp.logical_or(is_prompt, r == tps - 1)
    return first, last


def _fill_ext(ext_sc, cur_ref, prev_ref, next_ref, first, last):
    prev = prev_ref[...]
    nxt = next_ref[...]
    ext_sc[0:HALO, :] = jnp.where(first, jnp.zeros_like(prev), prev)
    ext_sc[HALO:HALO + TM, :] = cur_ref[...]
    ext_sc[HALO + TM:, :] = jnp.where(last, jnp.zeros_like(nxt), nxt)


def _route_tile(logits, b_ref, info_ref, cnt_ref):
    lt = logits.T + b_ref[...]
    gl = lt[N_EXPERTS:N_EXPERTS + N_GROUPS]
    ge = jnp.exp(gl - jnp.max(gl, axis=0, keepdims=True))
    gp = ge / jnp.sum(ge, axis=0, keepdims=True)
    best = gp[0:1]
    gidx = jnp.zeros(best.shape, jnp.int32)
    el = lt[0:EXPERTS_PER_GROUP]
    for g in range(1, N_GROUPS):
        better = gp[g:g + 1] > best
        gidx = jnp.where(better, g, gidx)
        best = jnp.where(better, gp[g:g + 1], best)
    for g in range(1, N_GROUPS):
        el = jnp.where(gidx == g, lt[g * EXPERTS_PER_GROUP:(g + 1) * EXPERTS_PER_GROUP], el)
    rio = lax.broadcasted_iota(jnp.int32, el.shape, 0)
    v0 = jnp.max(el, axis=0, keepdims=True)
    i0 = jnp.min(jnp.where(el == v0, rio, EXPERTS_PER_GROUP), axis=0, keepdims=True)
    el2 = jnp.where(rio == i0, -jnp.inf, el)
    v1 = jnp.max(el2, axis=0, keepdims=True)
    i1 = jnp.min(jnp.where(el2 == v1, rio, EXPERTS_PER_GROUP), axis=0, keepdims=True)
    e1 = jnp.exp(v1 - v0)
    den = 1.0 + e1
    w0 = (1.0 / den) * best
    w1 = (e1 / den) * best
    eid0 = gidx * EXPERTS_PER_GROUP + i0
    eid1 = gidx * EXPERTS_PER_GROUP + i1
    tm = lt.shape[1]
    eio = lax.broadcasted_iota(jnp.int32, (N_EXPERTS, tm), 0)
    oh0 = eio == eid0
    oh1 = eio == eid1
    ohs = jnp.where(oh0, 1.0, 0.0) + jnp.where(oh1, 1.0, 0.0)
    upper = (lax.broadcasted_iota(jnp.int32, (tm, tm), 0)
             < lax.broadcasted_iota(jnp.int32, (tm, tm), 1)).astype(BF16)
    before = jnp.dot(ohs.astype(BF16), upper, preferred_element_type=F32)
    rank0 = jnp.sum(jnp.where(oh0, before, 0.0), axis=0, keepdims=True)
    rank1 = jnp.sum(jnp.where(oh1, before, 0.0), axis=0, keepdims=True)
    info_ref[0] = jnp.concatenate([eid0.astype(F32), eid1.astype(F32), w0, w1, rank0, rank1,
                                   jnp.zeros((2, tm), F32)], axis=0)
    cnt_ref[0] = jnp.sum(ohs, axis=1, keepdims=True)


def _moe_prologue(x_new, sl, mod_ref, g2_ref, wr_ref, rb_ref, h2_ref, info_ref, cnt_ref):
    h2 = _rms_mod(x_new, g2_ref[...], mod_ref[0, 4:5, :], mod_ref[0, 3:4, :])
    h2_ref[sl, :] = h2
    assert sl.stop - sl.start == TM
    _route_tile(jnp.dot(h2.astype(BF16), wr_ref[...], preferred_element_type=F32), rb_ref, info_ref, cnt_ref)


def _post_even_kernel(xp_ref, xs_ref, u_ref, up_ref, un_ref, op_ref, os_ref, mod_ref, cw_ref, cb_ref,
                      lng_ref, lnb_ref, wo_ref, g2_ref, wr_ref, rb_ref, x1_ref, h2_ref, info_ref, cnt_ref,
                      ext_sc, sh_sc, *, npt, tps):
    i = pl.program_id(0)
    first, last = _seq_edges(i, npt, tps)
    _fill_ext(ext_sc, u_ref, up_ref, un_ref, first, last)
    cw = u_ref.shape[1]
    span = sh_sc.shape[1]
    for r in range(SUBLANES):
        sh_sc[r] = ext_sc[r:r + span, :]
    rows = 32
    is_prompt = i < npt
    for sl in _row_parts(MIX_ROW_PARTS):
        chunks = []
        for r0 in range(sl.start, sl.stop, rows):
            acc = None
            for k in range(CONV_K):
                off = HALO - CONV_K // 2 + k
                start = r0 + off - off % SUBLANES
                term = sh_sc[off % SUBLANES, start:start + rows, :] * cw_ref[k:k + 1, :]
                acc = term if acc is None else acc + term
            chunks.append(acc)
        conv = jnp.concatenate(chunks, axis=0) + cb_ref[...]
        mu = jnp.mean(conv, axis=-1, keepdims=True)
        cen = conv - mu
        var = jnp.mean(cen * cen, axis=-1, keepdims=True)
        uu = _silu(cen * lax.rsqrt(var + EPS) * lng_ref[...] + lnb_ref[...])
        o = jnp.where(is_prompt, op_ref[sl, :], os_ref[sl, :])
        m = (jnp.dot(uu.astype(BF16), wo_ref[0:cw, :], preferred_element_type=F32)
             + jnp.dot(o, wo_ref[cw:, :], preferred_element_type=F32))
        x1 = jnp.where(is_prompt, xp_ref[sl, :], xs_ref[sl, :]) + mod_ref[0, 2:3, :] * m
        x1_ref[sl, :] = x1
        _moe_prologue(x1, sl, mod_ref, g2_ref, wr_ref, rb_ref, h2_ref, info_ref, cnt_ref)


def _gather_start(dest_ref, yb_hbm, ybuf, sem, slot):
    def body(i, carry):
        for j in range(SUBLANES):
            for k in range(TOP_K):
                row = dest_ref[0, 0, k * TM + i * SUBLANES + j]
                pltpu.make_async_copy(yb_hbm.at[pl.ds(row, 1)], ybuf.at[slot, k, i, pl.ds(j, 1)],
                                      sem.at[slot]).start(priority=k % 2)
        return carry

    lax.fori_loop(0, TM // SUBLANES, body, 0)


def _moe_gather(dcur_ref, dnxt_ref, yb_hbm, ybuf, sem):
    t = pl.program_id(0)
    slot = t % 2

    @pl.when(t == 0)
    def _():
        _gather_start(dcur_ref, yb_hbm, ybuf, sem, 0)

    @pl.when(t + 1 < pl.num_programs(0))
    def _():
        _gather_start(dnxt_ref, yb_hbm, ybuf, sem, 1 - slot)

    for k in range(TOP_K):
        pltpu.make_async_copy(ybuf.at[slot, k], ybuf.at[slot, k], sem.at[slot]).wait()
    return slot


def _combined_rows(cw_ref, ybuf, slot, sl):
    d = ybuf.shape[-1]
    g = slice(sl.start // SUBLANES, sl.stop // SUBLANES)
    w = cw_ref[sl, :]
    y = w[:, 0:1] * ybuf[slot, 0, g].reshape(sl.stop - sl.start, d)
    for k in range(1, TOP_K):
        y = y + w[:, k:k + 1] * ybuf[slot, k, g].reshape(sl.stop - sl.start, d)
    return y


def _dispatch_kernel(dest_ref, pend_ref, h2_hbm, xb_hbm, stage, zeros, in_sem, row_sem, zero_sem):
    t = pl.program_id(0)
    nt = pl.num_programs(0)
    slot = t % DISPATCH_SLOTS

    tile_rows = TM // SUBLANES

    def stage_in(tile_idx, s):
        return pltpu.make_async_copy(h2_hbm.at[pl.ds(tile_idx * tile_rows, tile_rows)], stage.at[s], in_sem.at[s])

    def drain_rows(s):
        for _ in range(TOP_K):
            pltpu.make_async_copy(stage.at[s], stage.at[s], row_sem.at[s]).wait()

    def zero_block(start):
        return pltpu.make_async_copy(zeros, xb_hbm.at[pl.ds(pl.multiple_of(start, MOE_BLK), MOE_BLK)], zero_sem)

    def zero_fill(op):
        for e in range(N_EXPERTS):
            op(zero_block(jnp.maximum(pend_ref[e] - MOE_BLK, 0)))
        for j in range(N_EXPERTS):
            start = pend_ref[N_EXPERTS - 1] + j * MOE_BLK

            @pl.when(start < xb_hbm.shape[0])
            def _():
                op(zero_block(start))

    @pl.when(t == 0)
    def _():
        stage_in(0, 0).start()
        zeros[...] = jnp.zeros(zeros.shape, zeros.dtype)
        zero_fill(lambda cp: cp.start())
        zero_fill(lambda cp: cp.wait())

    @pl.when(jnp.logical_and(t == 0, nt > 1))
    def _():
        stage_in(1, 1).start()

    stage_in(t, slot).wait()

    def body(i, carry):
        for j in range(SUBLANES):
            for k in range(TOP_K):
                row = dest_ref[0, 0, k * TM + i * SUBLANES + j]
                pltpu.make_async_copy(stage.at[slot, i, pl.ds(j, 1)], xb_hbm.at[pl.ds(row, 1)],
                                      row_sem.at[slot]).start(priority=k % 2)
        return carry

    lax.fori_loop(0, tile_rows, body, 0)

    @pl.when(t > 0)
    def _():
        drain_rows((t + DISPATCH_SLOTS - 1) % DISPATCH_SLOTS)

    @pl.when(t + 2 < nt)
    def _():
        stage_in(t + 2, (t + 2) % DISPATCH_SLOTS).start()

    @pl.when(t == nt - 1)
    def _():
        drain_rows(slot)


def _dispatch(dest3, pend, h2, p_len):
    n, d = h2.shape
    return pl.pallas_call(
        _dispatch_kernel,
        grid=(n // TM,),
        in_specs=[pl.BlockSpec((1, 1, TOP_K * TM), lambda t: (t, 0, 0), memory_space=pltpu.SMEM),
                  pl.BlockSpec(memory_space=pltpu.SMEM), pl.BlockSpec(memory_space=pl.ANY)],
        out_specs=pl.BlockSpec(memory_space=pl.ANY),
        out_shape=jax.ShapeDtypeStruct((p_len, d), h2.dtype),
        scratch_shapes=[pltpu.VMEM((DISPATCH_SLOTS, TM // SUBLANES, SUBLANES, d), h2.dtype),
                        pltpu.VMEM((MOE_BLK, d), h2.dtype),
                        pltpu.SemaphoreType.DMA((DISPATCH_SLOTS,)), pltpu.SemaphoreType.DMA((DISPATCH_SLOTS,)),
                        pltpu.SemaphoreType.DMA(())],
        compiler_params=_cparams(("arbitrary",)),
        name="moe_dispatch",
    )(dest3, pend, h2.reshape(n // SUBLANES, SUBLANES, d))


def _pre_odd_kernel(x_ref, dcur_ref, dnxt_ref, cw_ref, yb_hbm, modp_ref, mod_ref, g_ref, w_ref,
                    x2_ref, xc_ref, p_ref, bg_ref, ybuf, sem, *, cw):
    slot = _moe_gather(dcur_ref, dnxt_ref, yb_hbm, ybuf, sem)
    for sl in _row_parts(PROJ_ROW_PARTS):
        x2 = x_ref[sl, :] + modp_ref[0, 5:6, :] * _combined_rows(cw_ref, ybuf, slot, sl)
        x2_ref[sl, :] = x2
        h = _rms_mod(x2, g_ref[...], mod_ref[0, 1:2, :], mod_ref[0, 0:1, :])
        proj = jnp.dot(h.astype(BF16), w_ref[...], preferred_element_type=F32)
        xc_ref[sl, :] = proj[:, :cw]
        hd = proj[:, cw:2 * cw]
        bg_ref[sl, :] = proj[:, 2 * cw:3 * cw]
        p_ref[sl, :] = proj[:, 3 * cw:] * hd


def _post_odd_kernel(x_ref, xc_ref, xcp_ref, xcn_ref, p_ref, pp_ref, pn_ref, bg_ref, mod_ref,
                     pw_ref, ps_ref, dw_ref, wo_ref, g2_ref, wr_ref, rb_ref,
                     x3_ref, h2_ref, info_ref, cnt_ref, extc_sc, extp_sc, *, npt, tps):
    i = pl.program_id(0)
    first, last = _seq_edges(i, npt, tps)
    _fill_ext(extc_sc, xc_ref, xcp_ref, xcn_ref, first, last)
    _fill_ext(extp_sc, p_ref, pp_ref, pn_ref, first, last)
    cw = xc_ref.shape[1]
    gc = cw // len(POOL_WINDOWS)
    is_prompt = i < npt
    t_len = jnp.where(is_prompt, TM, TM * tps)
    tile_pos = jnp.where(is_prompt, 0, ((i - npt) % tps) * TM)
    for rs in _row_parts(MIX_ROW_PARTS):
        nrows = rs.stop - rs.start
        base = HALO + rs.start
        pos = lax.broadcasted_iota(jnp.int32, (nrows, 1), 0) + (tile_pos + rs.start)
        ds = []
        for g, w in enumerate(POOL_WINDOWS):
            sl = slice(g * gc, (g + 1) * gc)
            acc = None
            for o in range(-(w // 2), w // 2):
                term = extc_sc[base + o:base + o + nrows, sl]
                acc = term if acc is None else acc + term
            lo = jnp.maximum(pos - w // 2, 0)
            hi = jnp.minimum(pos + w // 2 - 1, t_len - 1)
            cnt = (hi - lo + 1).astype(F32)
            ds.append(acc / cnt - extc_sc[base:base + nrows, sl])
        d = jnp.concatenate(ds, axis=1).astype(BF16)
        yc = jnp.dot(d, pw_ref[...], preferred_element_type=F32) * ps_ref[...]
        conv = None
        for k in range(SCONV_K):
            start = base - SCONV_K // 2 + k
            term = extp_sc[start:start + nrows, :] * dw_ref[k:k + 1, :]
            conv = term if conv is None else conv + term
        yd = bg_ref[rs, :] * conv
        m = (jnp.dot(yc.astype(BF16), wo_ref[0:cw, :], preferred_element_type=F32)
             + jnp.dot(yd.astype(BF16), wo_ref[cw:, :], preferred_element_type=F32))
        x3 = x_ref[rs, :] + mod_ref[0, 2:3, :] * m
        x3_ref[rs, :] = x3
        _moe_prologue(x3, rs, mod_ref, g2_ref, wr_ref, rb_ref, h2_ref, info_ref, cnt_ref)


def _final_kernel(x_ref, dcur_ref, dnxt_ref, cw_ref, yb_hbm, modp_ref, g_ref, op_ref, os_ref, ybuf, sem, *, npt):
    i = pl.program_id(0)
    slot = _moe_gather(dcur_ref, dnxt_ref, yb_hbm, ybuf, sem)
    x = x_ref[...] + modp_ref[0, 5:6, :] * _combined_rows(cw_ref, ybuf, slot, slice(0, TM))
    ms = jnp.mean(x * x, axis=-1, keepdims=True)
    out = x * lax.rsqrt(ms + EPS) * g_ref[...]

    @pl.when(i < npt)
    def _():
        op_ref[...] = out

    @pl.when(i >= npt)
    def _():
        os_ref[...] = out


def _expert_kernel(be_ref, nv_ref, x_ref, wg_ref, wu_ref, wd_ref, y_ref, wg_sc, wu_sc, wd_sc):
    i = pl.program_id(0)

    @pl.when(jnp.logical_or(i == 0, be_ref[i] != be_ref[jnp.maximum(i - 1, 0)]))
    def _():
        wg_sc[...] = wg_ref[0, 0].astype(BF16)
        wu_sc[...] = wu_ref[0, 0].astype(BF16)
        wd_sc[...] = wd_ref[0, 0].astype(BF16)

    @pl.when(i < nv_ref[0])
    def _():
        x = x_ref[...].astype(BF16)
        hg = jnp.dot(x, wg_sc[...], preferred_element_type=F32)
        hu = jnp.dot(x, wu_sc[...], preferred_element_type=F32)
        hid = (_silu(hg) * hu).astype(BF16)
        y_ref[...] = jnp.dot(hid, wd_sc[...], preferred_element_type=F32)

    @pl.when(i >= nv_ref[0])
    def _():
        y_ref[...] = jnp.zeros(y_ref.shape, F32)


def _experts(blk_e, nvalid, xb, layer, wg, wu, wd):
    p_len, d = xb.shape
    hid = wg.shape[3]
    nblk = p_len // MOE_BLK
    grid_spec = pltpu.PrefetchScalarGridSpec(
        num_scalar_prefetch=2,
        grid=(nblk,),
        in_specs=[pl.BlockSpec((MOE_BLK, d), lambda i, be, nv: (jnp.minimum(i, nv[0] - 1), 0)),
                  pl.BlockSpec((1, 1, d, hid), lambda i, be, nv: (layer, be[i], 0, 0)),
                  pl.BlockSpec((1, 1, d, hid), lambda i, be, nv: (layer, be[i], 0, 0)),
                  pl.BlockSpec((1, 1, hid, d), lambda i, be, nv: (layer, be[i], 0, 0))],
        out_specs=pl.BlockSpec((MOE_BLK, d), lambda i, be, nv: (i, 0)),
        scratch_shapes=[pltpu.VMEM((d, hid), BF16), pltpu.VMEM((d, hid), BF16), pltpu.VMEM((hid, d), BF16)],
    )
    return pl.pallas_call(
        _expert_kernel,
        grid_spec=grid_spec,
        out_shape=jax.ShapeDtypeStruct((p_len, d), F32),
        compiler_params=_cparams(("arbitrary",)),
        name="moe_experts",
    )(blk_e, nvalid, xb, wg, wu, wd)


def _moe(h2, info, cnt, layer, wg, wu, wd):
    n, d = h2.shape
    cnt = cnt[:, :, 0].astype(jnp.int32)
    counts = jnp.sum(cnt, axis=0)
    pc = ((counts + MOE_BLK - 1) // MOE_BLK) * MOE_BLK
    pend = jnp.cumsum(pc)
    base = (pend - pc)[None, :] + jnp.cumsum(cnt, axis=0) - cnt
    eid = info[:, 0:2, :].astype(jnp.int32)
    rank = info[:, 4:6, :].astype(jnp.int32)
    sel = eid[..., None] == jnp.arange(N_EXPERTS, dtype=jnp.int32)
    dest = jnp.sum(jnp.where(sel, base[:, None, None, :], 0), axis=-1) + rank
    p_len = n * TOP_K + N_EXPERTS * MOE_BLK
    nblk = p_len // MOE_BLK
    blk_start = jnp.arange(nblk, dtype=jnp.int32) * MOE_BLK
    blk_e = jnp.minimum(jnp.sum(pend[None, :] <= blk_start[:, None], axis=1), N_EXPERTS - 1).astype(jnp.int32)
    nvalid = (pend[-1:] // MOE_BLK).astype(jnp.int32)
    dest3 = dest.reshape(dest.shape[0], 1, TOP_K * TM)
    wcol = jnp.transpose(info[:, 2:2 + TOP_K, :], (0, 2, 1)).reshape(n, TOP_K)
    xb = _dispatch(dest3, pend.astype(jnp.int32), h2, p_len)
    yb = _experts(blk_e, nvalid, xb, layer, wg, wu, wd)
    return dest3, wcol, yb


def _rope_tables(t_len, qw):
    rows = t_len // GRID_W
    ax = qw // (DIFF_HEADS * 2) // 2
    row = jnp.repeat(jnp.arange(rows, dtype=F32), GRID_W)
    col = jnp.tile(jnp.arange(GRID_W, dtype=F32), rows)
    inv = ROPE_BASE ** (-jnp.arange(0, ax, 2, dtype=F32) / ax)
    ar = row[:, None] * inv[None, :]
    ac = col[:, None] * inv[None, :]
    cr, sr, cc, sc = jnp.cos(ar), jnp.sin(ar), jnp.cos(ac), jnp.sin(ac)
    cos = jnp.concatenate([cr, cr, cc, cc], axis=1)
    sins = jnp.concatenate([-sr, sr, -sc, sc], axis=1)
    reps = qw // cos.shape[1]
    cos = jnp.concatenate([jnp.ones((TM, cos.shape[1]), F32), cos], axis=0)
    sins = jnp.concatenate([jnp.zeros((TM, sins.shape[1]), F32), sins], axis=0)
    return jnp.tile(cos, (1, reps)), jnp.tile(sins, (1, reps))


def kernel(x_prompt, x_sample, cache_k, cache_v, c, c_ctx, w_ada, b_ada, g_norm1, g_norm2, g_final, w_in_e, conv_a_w, conv_a_b, ln_a_g, ln_a_b, lam_q1, lam_k1, lam_q2, lam_k2, subln_g, w_out_e, w_in_o, pool_w, pool_scale, conv_d_w, w_out_o, w_router_g, b_router_g, w_router_e, b_router_e, w_gate, w_up, w_down):
    bp, tp, d = x_prompt.shape
    bs, ts, _ = x_sample.shape
    past = cache_k.shape[2]
    assert tp == TM and ts % TM == 0 and d % LANES == 0
    npt = bp
    tps = ts // TM
    n_p, n_s = bp * tp, bs * ts
    n = n_p + n_s
    nt = n // TM
    cw = conv_a_w.shape[2]
    qw = (w_in_e.shape[2] - 2 * cw) // 3
    dh = qw // (DIFF_HEADS * 2)
    hpt = TM // HALO

    def cond_of(i):
        return jnp.where(i < npt, 0, 1 + (i - npt) // tps)

    def tile(w):
        return pl.BlockSpec((TM, w), lambda i: (i, 0))

    def halo_prev(w):
        return pl.BlockSpec((HALO, w), lambda i: (jnp.maximum(i * hpt - 1, 0), 0))

    def halo_next(w):
        return pl.BlockSpec((HALO, w), lambda i: (jnp.minimum((i + 1) * hpt, nt * hpt - 1), 0))

    def whole(shape):
        return pl.BlockSpec(shape, lambda i: (0,) * len(shape))

    mod_spec = pl.BlockSpec((1, 6, d), lambda i: (cond_of(i), 0, 0))
    combine_specs = [pl.BlockSpec((1, 1, TOP_K * TM), lambda i: (i, 0, 0), memory_space=pltpu.SMEM),
                     pl.BlockSpec((1, 1, TOP_K * TM), lambda i: (jnp.minimum(i + 1, nt - 1), 0, 0),
                                  memory_space=pltpu.SMEM),
                     tile(TOP_K), pl.BlockSpec(memory_space=pl.ANY)]
    combine_scratch = [pltpu.VMEM((2, TOP_K, TM // SUBLANES, SUBLANES, d), F32), pltpu.SemaphoreType.DMA((2,))]

    rows = 8 * ((1 + bs + 7) // 8)
    cond = jnp.concatenate([c_ctx[None, :], c, jnp.zeros((rows - 1 - bs, d), F32)], axis=0)
    mod = _ada_table(cond, w_ada, b_ada)

    xp2, xs2 = x_prompt.reshape(n_p, d), x_sample.reshape(n_s, d)

    def prompt_tile(w):
        return pl.BlockSpec((TM, w), lambda i: (jnp.minimum(i, npt - 1), 0))

    def sample_tile(w):
        return pl.BlockSpec((TM, w), lambda i: (jnp.maximum(i - npt, 0), 0))

    def router_w(i):
        wre = jnp.transpose(w_router_e[i], (1, 0, 2)).reshape(d, N_EXPERTS)
        wr = jnp.concatenate([wre, w_router_g[i]], axis=1)
        return jnp.pad(wr, ((0, 0), (0, ROUTER_PAD - wr.shape[1]))).astype(BF16)

    def router_b(i):
        b = jnp.concatenate([b_router_e[i].reshape(N_EXPERTS), b_router_g[i]])
        return jnp.pad(b, (0, ROUTER_PAD - b.shape[0]))[:, None]

    cos, sins = _rope_tables(ts, qw)
    tab_spec = pl.BlockSpec((TM, qw), lambda i: (jnp.where(i < npt, 0, 1 + (i - npt) % tps), 0))
    cache_spec = pl.BlockSpec((TM, qw), lambda i: (jnp.minimum(i, npt - 1), 0))
    u, q, k, vt, kc, vc = pl.pallas_call(
        functools.partial(_pre_even_kernel, npt=npt, cw=cw, qw=qw, qscale=dh ** -0.5 * math.log2(math.e)),
        grid=(nt,),
        in_specs=[prompt_tile(d), sample_tile(d), mod_spec, whole((1, d)), whole(w_in_e.shape[1:]),
                  tab_spec, tab_spec],
        out_specs=[tile(cw), tile(qw), tile(qw), pl.BlockSpec((qw, TM), lambda i: (0, i)),
                   cache_spec, cache_spec],
        out_shape=[jax.ShapeDtypeStruct((n, cw), F32), jax.ShapeDtypeStruct((n, qw), BF16),
                   jax.ShapeDtypeStruct((n, qw), BF16), jax.ShapeDtypeStruct((qw, n), BF16),
                   jax.ShapeDtypeStruct((n_p, qw), F32), jax.ShapeDtypeStruct((n_p, qw), F32)],
        scratch_shapes=[pltpu.VMEM((2, TM, qw), F32)],
        compiler_params=_cparams(("arbitrary",)),
        name="pre_even",
    )(xp2, xs2, mod[0], g_norm1[0][None, :], w_in_e[0].astype(BF16), cos, sins)

    lam_init = 0.8 - 0.6 * math.exp(-0.3 * 0)
    lam = (jnp.exp(jnp.sum(lam_q1[0] * lam_k1[0])) - jnp.exp(jnp.sum(lam_q2[0] * lam_k2[0])) + lam_init)
    lam = lam.reshape(1).astype(F32)
    sub_g = subln_g[0][:, None]
    o_p = _attention(lam, q, k, vt, sub_g, None, batch=bp, t_len=tp, row0=0, tq=TM, tk=TM,
                     post_scale=1.0 - lam_init)
    k_ctx = cache_k[:, 0].reshape(bs * past, qw).astype(BF16)
    vt_ctx = jnp.transpose(cache_v[:, 0].reshape(bs, past, qw), (0, 2, 1)).astype(BF16)
    o_s = _attention(lam, q, k, vt, sub_g, (k_ctx, vt_ctx), batch=bs, t_len=ts, row0=n_p, tq=1024, tk=512,
                     post_scale=1.0 - lam_init)

    moe_outs = [jax.ShapeDtypeStruct((n, d), F32), jax.ShapeDtypeStruct((n, d), F32),
                jax.ShapeDtypeStruct((nt, 8, TM), F32), jax.ShapeDtypeStruct((nt, N_EXPERTS, 1), F32)]
    moe_out_specs = [tile(d), tile(d), pl.BlockSpec((1, 8, TM), lambda i: (i, 0, 0)),
                     pl.BlockSpec((1, N_EXPERTS, 1), lambda i: (i, 0, 0))]
    x1, h2, info, cnt = pl.pallas_call(
        functools.partial(_post_even_kernel, npt=npt, tps=tps),
        grid=(nt,),
        in_specs=[prompt_tile(d), sample_tile(d), tile(cw), halo_prev(cw), halo_next(cw),
                  prompt_tile(qw), sample_tile(qw), mod_spec,
                  whole((CONV_K, cw)), whole((1, cw)), whole((1, cw)), whole((1, cw)),
                  whole(w_out_e.shape[1:]), whole((1, d)), whole((d, ROUTER_PAD)), whole((ROUTER_PAD, 1))],
        out_specs=moe_out_specs,
        out_shape=moe_outs,
        scratch_shapes=[pltpu.VMEM((TM + 2 * HALO, cw), F32),
                        pltpu.VMEM((SUBLANES, TM + 2 * HALO - SUBLANES, cw), F32)],
        compiler_params=_cparams(("parallel",)),
        name="post_even",
    )(xp2, xs2, u, u, u, o_p, o_s, mod[0], conv_a_w[0], conv_a_b[0][None, :], ln_a_g[0][None, :], ln_a_b[0][None, :],
      w_out_e[0].astype(BF16), g_norm2[0][None, :], router_w(0), router_b(0))
    dest3, wcol, yb = _moe(h2, info, cnt, 0, w_gate, w_up, w_down)

    pw = pool_w.shape[2]
    x2, xc, p, bg = pl.pallas_call(
        functools.partial(_pre_odd_kernel, cw=cw),
        grid=(nt,),
        in_specs=[tile(d)] + combine_specs + [mod_spec, mod_spec, whole((1, d)), whole(w_in_o.shape[1:])],
        out_specs=[tile(d), tile(cw), tile(cw), tile(cw)],
        out_shape=[jax.ShapeDtypeStruct((n, d), F32)] + [jax.ShapeDtypeStruct((n, cw), F32)] * 3,
        scratch_shapes=combine_scratch,
        compiler_params=_cparams(("arbitrary",)),
        name="pre_odd",
    )(x1, dest3, dest3, wcol, yb, mod[0], mod[1], g_norm1[1][None, :], w_in_o[0].astype(BF16))
    pool_bd = jnp.zeros((cw, cw), F32)
    for g in range(len(POOL_WINDOWS)):
        pool_bd = pool_bd.at[g * pw:(g + 1) * pw, g * pw:(g + 1) * pw].set(pool_w[0, g])
    x3, h2, info, cnt = pl.pallas_call(
        functools.partial(_post_odd_kernel, npt=npt, tps=tps),
        grid=(nt,),
        in_specs=[tile(d), tile(cw), halo_prev(cw), halo_next(cw), tile(cw), halo_prev(cw), halo_next(cw),
                  tile(cw), mod_spec, whole((cw, cw)), whole((1, cw)), whole((SCONV_K, cw)),
                  whole(w_out_o.shape[1:]), whole((1, d)), whole((d, ROUTER_PAD)), whole((ROUTER_PAD, 1))],
        out_specs=moe_out_specs,
        out_shape=moe_outs,
        scratch_shapes=[pltpu.VMEM((TM + 2 * HALO, cw), F32), pltpu.VMEM((TM + 2 * HALO, cw), F32)],
        compiler_params=_cparams(("parallel",)),
        name="post_odd",
    )(x2, xc, xc, xc, p, p, p, bg, mod[1], pool_bd.astype(BF16), pool_scale[0][None, :], conv_d_w[0],
      w_out_o[0].astype(BF16), g_norm2[1][None, :], router_w(1), router_b(1))
    dest3, wcol, yb = _moe(h2, info, cnt, 1, w_gate, w_up, w_down)

    out_p, out_s = pl.pallas_call(
        functools.partial(_final_kernel, npt=npt),
        grid=(nt,),
        in_specs=[tile(d)] + combine_specs + [mod_spec, whole((1, d))],
        out_specs=[pl.BlockSpec((TM, d), lambda i: (jnp.minimum(i, npt - 1), 0)),
                   pl.BlockSpec((TM, d), lambda i: (jnp.maximum(i - npt, 0), 0))],
        out_shape=[jax.ShapeDtypeStruct((n_p, d), F32), jax.ShapeDtypeStruct((n_s, d), F32)],
        scratch_shapes=combine_scratch,
        compiler_params=_cparams(("arbitrary",)),
        name="final_norm",
    )(x3, dest3, dest3, wcol, yb, mod[1], g_final[None, :])

    new_k = kc.reshape(bp, 1, tp, DIFF_HEADS, 2, dh)
    new_v = vc.reshape(bp, 1, tp, DIFF_HEADS, 2 * dh)
    return (out_p.reshape(bp, tp, d), out_s.reshape(bs, ts, d), new_k, new_v)
```

```python
import functools
import math

import jax
import jax.numpy as jnp
from jax import lax
from jax.experimental import pallas as pl
from jax.experimental.pallas import tpu as pltpu

F32 = jnp.float32
BF16 = jnp.bfloat16

EPS = 1e-6
GRID_W = 64
ROPE_BASE = 10000.0
CONV_K = 31
SCONV_K = 3
POOL_WINDOWS = (2, 4, 8, 16)
DIFF_HEADS = 4
N_GROUPS = 4
EXPERTS_PER_GROUP = 8
N_EXPERTS = N_GROUPS * EXPERTS_PER_GROUP
TOP_K = 2

LANES = 128
SUBLANES = 8
TM = 256
HALO = 16
PROJ_ROW_PARTS = 2
MIX_ROW_PARTS = 1
MOE_BLK = 512
ROUTER_PAD = 128
ONES_ROWS = 16
DISPATCH_SLOTS = 3
VMEM_LIMIT = 56 * 1024 * 1024


def _cparams(sem):
    return pltpu.CompilerParams(dimension_semantics=sem, vmem_limit_bytes=VMEM_LIMIT)


def _rms_mod(x, g, sc, sh):
    ms = jnp.mean(x * x, axis=-1, keepdims=True)
    return (x * lax.rsqrt(ms + EPS)) * g * (1.0 + sc) + sh


def _silu(x):
    return x * jax.nn.sigmoid(x)


def _ada_kernel(c_ref, w_ref, b_ref, o_ref):
    cs = _silu(c_ref[...])
    o_ref[0] = jnp.dot(cs, w_ref[0], precision=lax.Precision.HIGHEST,
                       preferred_element_type=F32) + b_ref[0]


def _ada_table(cond, w_ada, b_ada):
    depth, d, six_d = w_ada.shape
    rows = cond.shape[0]
    nj = six_d // d
    out = pl.pallas_call(
        _ada_kernel,
        grid=(depth, nj),
        in_specs=[pl.BlockSpec((rows, d), lambda l, j: (0, 0)),
                  pl.BlockSpec((1, d, d), lambda l, j: (l, 0, j)),
                  pl.BlockSpec((1, 1, d), lambda l, j: (l, 0, j))],
        out_specs=pl.BlockSpec((1, rows, d), lambda l, j: (l, 0, j)),
        out_shape=jax.ShapeDtypeStruct((depth, rows, six_d), F32),
        compiler_params=_cparams(("parallel", "parallel")),
        name="ada_table",
    )(cond, w_ada, b_ada.reshape(depth, 1, six_d))
    return out.reshape(depth, rows, nj, d)


def _row_parts(parts):
    rows = TM // parts
    return [slice(p * rows, (p + 1) * rows) for p in range(parts)]


def _rope(x, cos, sins, first_half):
    outs = []
    for j in range(x.shape[1] // LANES):
        sl = slice(j * LANES, (j + 1) * LANES)
        xs = x[:, sl]
        nxt = pltpu.roll(xs, LANES - 16, 1)
        prv = pltpu.roll(xs, 16, 1)
        rot = jnp.where(first_half, nxt, prv)
        outs.append(xs * cos[:, sl] + rot * sins[:, sl])
    return jnp.concatenate(outs, axis=1)


def _pre_even_kernel(xp_ref, xs_ref, mod_ref, g_ref, w_ref, cos_ref, sin_ref,
                     u_ref, q_ref, k_ref, vt_ref, kc_ref, vc_ref, kv_sc, *, npt, cw, qw, qscale):
    i = pl.program_id(0)
    is_prompt = i < npt
    lane = lax.broadcasted_iota(jnp.int32, (1, LANES), 1)
    first_half = (lane % 32) < 16
    for sl in _row_parts(PROJ_ROW_PARTS):
        x = jnp.where(is_prompt, xp_ref[sl, :], xs_ref[sl, :])
        h = _rms_mod(x, g_ref[...], mod_ref[0, 1:2, :], mod_ref[0, 0:1, :])
        proj = jnp.dot(h.astype(BF16), w_ref[...], preferred_element_type=F32)
        ga = proj[:, :cw]
        gb = proj[:, cw:2 * cw]
        u_ref[sl, :] = ga * jax.nn.sigmoid(gb)
        q = proj[:, 2 * cw:2 * cw + qw]
        k = proj[:, 2 * cw + qw:2 * cw + 2 * qw]
        v = proj[:, 2 * cw + 2 * qw:]
        kv_sc[0, sl, :] = k
        kv_sc[1, sl, :] = v
        cos = cos_ref[sl, :]
        sins = sin_ref[sl, :]
        q_ref[sl, :] = (_rope(q, cos, sins, first_half) * qscale).astype(BF16)
        k_ref[sl, :] = _rope(k, cos, sins, first_half).astype(BF16)
        vt_ref[:, sl] = v.T.astype(BF16)

    @pl.when(is_prompt)
    def _():
        heads, halves, dh = kc_ref.shape[3:]
        for h in range(heads):
            vc_ref[0, 0, :, h, :] = kv_sc[1, :, h * halves * dh:(h + 1) * halves * dh]
            for c in range(halves):
                lo = (h * halves + c) * dh
                kc_ref[0, 0, :, h, c, :] = kv_sc[0, :, lo:lo + dh]


def _attn_kernel(lam_ref, q_ref, k_ref, vt_ref, *rest, nk, tk, nk_ctx, post_scale):
    if nk_ctx:
        kc_ref, vtc_ref, g_ref, o_ref, s_sc, mc_sc, m_sc, acc_sc = rest
    else:
        g_ref, o_ref, s_sc, mc_sc, m_sc, acc_sc = rest
    q = q_ref[...]
    hw = q.shape[1]
    dh = hw // 2
    lane = lax.broadcasted_iota(jnp.int32, (1, hw), 1)
    zero = jnp.zeros_like(q)
    qs = (jnp.where(lane < dh, q, zero), jnp.where(lane >= dh, q, zero))
    m_sc[...] = jnp.full(m_sc.shape, -jnp.inf, F32)
    acc_sc[...] = jnp.zeros(acc_sc.shape, F32)
    ones_rows = (lax.broadcasted_iota(jnp.int32, (ONES_ROWS, tk), 0) == 0).astype(BF16)

    def scores(kk, slot):
        for c in range(2):
            s = lax.dot_general(kk, qs[c], (((1,), (1,)), ((), ())), preferred_element_type=F32)
            s_sc[slot, c] = s
            mc_sc[slot, c] = jnp.max(s, axis=0, keepdims=True)

    def consume(vt, slot):
        vta = jnp.concatenate([vt, ones_rows], axis=0)
        for c in range(2):
            m_prev = m_sc[c]
            m_new = jnp.maximum(m_prev, mc_sc[slot, c])
            alpha = jnp.exp2(m_prev - m_new)
            p = jnp.exp2(s_sc[slot, c] - m_new).astype(BF16)
            acc_sc[c] = alpha * acc_sc[c] + jnp.dot(vta, p, preferred_element_type=F32)
            m_sc[c] = m_new

    def k_at(i):
        if i < nk_ctx:
            return kc_ref[i * tk:(i + 1) * tk, :]
        return k_ref[(i - nk_ctx) * tk:(i - nk_ctx + 1) * tk, :]

    def vt_at(i):
        if i < nk_ctx:
            return vtc_ref[0, :, i * tk:(i + 1) * tk]
        return vt_ref[:, (i - nk_ctx) * tk:(i - nk_ctx + 1) * tk]

    n_total = nk_ctx + nk
    peel = nk_ctx + (n_total - 1 - nk_ctx) % 2
    scores(k_at(0), 0)
    for i in range(peel):
        scores(k_at(i + 1), (i + 1) % 2)
        consume(vt_at(i), i % 2)

    def body(t, carry):
        for r in range(2):
            j = 2 * t + (peel + r - nk_ctx)
            nxt = pl.multiple_of((j + 1) * tk, tk)
            cur = pl.multiple_of(j * tk, tk)
            scores(k_ref[pl.ds(nxt, tk), :], (peel + r + 1) % 2)
            consume(vt_ref[:, pl.ds(cur, tk)], (peel + r) % 2)
        return carry

    lax.fori_loop(0, (n_total - 1 - peel) // 2, body, 0)
    consume(vt_at(n_total - 1), (n_total - 1) % 2)
    o = (acc_sc[0, :hw, :] / acc_sc[0, hw:hw + 1, :]
         - lam_ref[0] * (acc_sc[1, :hw, :] / acc_sc[1, hw:hw + 1, :]))
    ms = jnp.mean(o * o, axis=0, keepdims=True)
    o = o * lax.rsqrt(ms + EPS) * g_ref[...] * post_scale
    o_ref[...] = o.T.astype(BF16)


def _attention(lam, q, k, vt, g_col, ctx, *, batch, t_len, row0, tq, tk, post_scale):
    hw = q.shape[1] // DIFF_HEADS
    nq = t_len // tq
    assert row0 % t_len == 0 and t_len % tk == 0
    qb0 = row0 // tq
    sb0 = row0 // t_len
    in_specs = [pl.BlockSpec(memory_space=pltpu.SMEM),
                pl.BlockSpec((tq, hw), lambda b, h, i: (qb0 + b * nq + i, h)),
                pl.BlockSpec((t_len, hw), lambda b, h, i: (sb0 + b, h)),
                pl.BlockSpec((hw, t_len), lambda b, h, i: (h, sb0 + b))]
    args = [lam, q, k, vt]
    nk_ctx = 0
    if ctx is not None:
        k_ctx, vt_ctx = ctx
        past = vt_ctx.shape[2]
        assert past % tk == 0
        nk_ctx = past // tk
        in_specs += [pl.BlockSpec((past, hw), lambda b, h, i: (b, h)),
                     pl.BlockSpec((1, hw, past), lambda b, h, i: (b, h, 0))]
        args += [k_ctx, vt_ctx]
    in_specs.append(pl.BlockSpec((hw, 1), lambda b, h, i: (0, 0)))
    args.append(g_col)
    return pl.pallas_call(
        functools.partial(_attn_kernel, nk=t_len // tk, tk=tk, nk_ctx=nk_ctx, post_scale=post_scale),
        grid=(batch, DIFF_HEADS, nq),
        in_specs=in_specs,
        out_specs=pl.BlockSpec((tq, hw), lambda b, h, i: (b * nq + i, h)),
        out_shape=jax.ShapeDtypeStruct((batch * t_len, q.shape[1]), BF16),
        scratch_shapes=[pltpu.VMEM((2, 2, tk, tq), F32), pltpu.VMEM((2, 2, 1, tq), F32),
                        pltpu.VMEM((2, 1, tq), F32), pltpu.VMEM((2, hw + ONES_ROWS, tq), F32)],
        compiler_params=_cparams(("parallel", "parallel", "parallel")),
        name="diff_attention",
    )(*args)


def _seq_edges(i, npt, tps):
    r = (i - npt) % tps
    is_prompt = i < npt
    first = jnp.logical_or(is_prompt, r == 0)
    last = jnp.logical_or(is_prompt, r == tps - 1)
    return first, last


def _fill_ext(ext_sc, cur_ref, prev_ref, next_ref, first, last):
    prev = prev_ref[...]
    nxt = next_ref[...]
    ext_sc[0:HALO, :] = jnp.where(first, jnp.zeros_like(prev), prev)
    ext_sc[HALO:HALO + TM, :] = cur_ref[...]
    ext_sc[HALO + TM:, :] = jnp.where(last, jnp.zeros_like(nxt), nxt)


def _route_tile(logits, b_ref, info_ref, cnt_ref):
    lt = logits.T + b_ref[...]
    gl = lt[N_EXPERTS:N_EXPERTS + N_GROUPS]
    ge = jnp.exp(gl - jnp.max(gl, axis=0, keepdims=True))
    gp = ge / jnp.sum(ge, axis=0, keepdims=True)
    best = gp[0:1]
    gidx = jnp.zeros(best.shape, jnp.int32)
    el = lt[0:EXPERTS_PER_GROUP]
    for g in range(1, N_GROUPS):
        better = gp[g:g + 1] > best
        gidx = jnp.where(better, g, gidx)
        best = jnp.where(better, gp[g:g + 1], best)
    for g in range(1, N_GROUPS):
        el = jnp.where(gidx == g, lt[g * EXPERTS_PER_GROUP:(g + 1) * EXPERTS_PER_GROUP], el)
    rio = lax.broadcasted_iota(jnp.int32, el.shape, 0)
    v0 = jnp.max(el, axis=0, keepdims=True)
    i0 = jnp.min(jnp.where(el == v0, rio, EXPERTS_PER_GROUP), axis=0, keepdims=True)
    el2 = jnp.where(rio == i0, -jnp.inf, el)
    v1 = jnp.max(el2, axis=0, keepdims=True)
    i1 = jnp.min(jnp.where(el2 == v1, rio, EXPERTS_PER_GROUP), axis=0, keepdims=True)
    e1 = jnp.exp(v1 - v0)
    den = 1.0 + e1
    w0 = (1.0 / den) * best
    w1 = (e1 / den) * best
    eid0 = gidx * EXPERTS_PER_GROUP + i0
    eid1 = gidx * EXPERTS_PER_GROUP + i1
    tm = lt.shape[1]
    eio = lax.broadcasted_iota(jnp.int32, (N_EXPERTS, tm), 0)
    oh0 = eio == eid0
    oh1 = eio == eid1
    ohs = jnp.where(oh0, 1.0, 0.0) + jnp.where(oh1, 1.0, 0.0)
    upper = (lax.broadcasted_iota(jnp.int32, (tm, tm), 0)
             < lax.broadcasted_iota(jnp.int32, (tm, tm), 1)).astype(BF16)
    before = jnp.dot(ohs.astype(BF16), upper, preferred_element_type=F32)
    rank0 = jnp.sum(jnp.where(oh0, before, 0.0), axis=0, keepdims=True)
    rank1 = jnp.sum(jnp.where(oh1, before, 0.0), axis=0, keepdims=True)
    info_ref[0] = jnp.concatenate([eid0.astype(F32), eid1.astype(F32), w0, w1, rank0, rank1,
                                   jnp.zeros((2, tm), F32)], axis=0)
    cnt_ref[0] = jnp.sum(ohs, axis=1, keepdims=True)


def _moe_prologue(x_new, sl, mod_ref, g2_ref, wr_ref, rb_ref, h2_ref, info_ref, cnt_ref):
    h2 = _rms_mod(x_new, g2_ref[...], mod_ref[0, 4:5, :], mod_ref[0, 3:4, :])
    h2_ref[sl, :] = h2
    assert sl.stop - sl.start == TM
    _route_tile(jnp.dot(h2.astype(BF16), wr_ref[...], preferred_element_type=F32), rb_ref, info_ref, cnt_ref)


def _post_even_kernel(xp_ref, xs_ref, u_ref, up_ref, un_ref, op_ref, os_ref, mod_ref, cw_ref, cb_ref,
                      lng_ref, lnb_ref, wo_ref, g2_ref, wr_ref, rb_ref, x1_ref, h2_ref, info_ref, cnt_ref,
                      ext_sc, sh_sc, *, npt, tps):
    i = pl.program_id(0)
    first, last = _seq_edges(i, npt, tps)
    _fill_ext(ext_sc, u_ref, up_ref, un_ref, first, last)
    cw = u_ref.shape[1]
    span = sh_sc.shape[1]
    for r in range(SUBLANES):
        sh_sc[r] = ext_sc[r:r + span, :]
    rows = 32
    is_prompt = i < npt
    for sl in _row_parts(MIX_ROW_PARTS):
        chunks = []
        for r0 in range(sl.start, sl.stop, rows):
            acc = None
            for k in range(CONV_K):
                off = HALO - CONV_K // 2 + k
                start = r0 + off - off % SUBLANES
                term = sh_sc[off % SUBLANES, start:start + rows, :] * cw_ref[k:k + 1, :]
                acc = term if acc is None else acc + term
            chunks.append(acc)
        conv = jnp.concatenate(chunks, axis=0) + cb_ref[...]
        mu = jnp.mean(conv, axis=-1, keepdims=True)
        cen = conv - mu
        var = jnp.mean(cen * cen, axis=-1, keepdims=True)
        uu = _silu(cen * lax.rsqrt(var + EPS) * lng_ref[...] + lnb_ref[...])
        o = jnp.where(is_prompt, op_ref[sl, :], os_ref[sl, :])
        m = (jnp.dot(uu.astype(BF16), wo_ref[0:cw, :], preferred_element_type=F32)
             + jnp.dot(o, wo_ref[cw:, :], preferred_element_type=F32))
        x1 = jnp.where(is_prompt, xp_ref[sl, :], xs_ref[sl, :]) + mod_ref[0, 2:3, :] * m
        x1_ref[sl, :] = x1
        _moe_prologue(x1, sl, mod_ref, g2_ref, wr_ref, rb_ref, h2_ref, info_ref, cnt_ref)


def _gather_start(dest_ref, yb_hbm, ybuf, sem, slot):
    def body(i, carry):
        for j in range(SUBLANES):
            for k in range(TOP_K):
                row = dest_ref[0, 0, k * TM + i * SUBLANES + j]
                pltpu.make_async_copy(yb_hbm.at[pl.ds(row, 1)], ybuf.at[slot, k, i, pl.ds(j, 1)],
                                      sem.at[slot]).start(priority=k % 2)
        return carry

    lax.fori_loop(0, TM // SUBLANES, body, 0)


def _moe_gather(dcur_ref, dnxt_ref, yb_hbm, ybuf, sem):
    t = pl.program_id(0)
    slot = t % 2

    @pl.when(t == 0)
    def _():
        _gather_start(dcur_ref, yb_hbm, ybuf, sem, 0)

    @pl.when(t + 1 < pl.num_programs(0))
    def _():
        _gather_start(dnxt_ref, yb_hbm, ybuf, sem, 1 - slot)

    for k in range(TOP_K):
        pltpu.make_async_copy(ybuf.at[slot, k], ybuf.at[slot, k], sem.at[slot]).wait()
    return slot


def _combined_rows(cw_ref, ybuf, slot, sl):
    d = ybuf.shape[-1]
    g = slice(sl.start // SUBLANES, sl.stop // SUBLANES)
    w = cw_ref[sl, :]
    y = w[:, 0:1] * ybuf[slot, 0, g].reshape(sl.stop - sl.start, d)
    for k in range(1, TOP_K):
        y = y + w[:, k:k + 1] * ybuf[slot, k, g].reshape(sl.stop - sl.start, d)
    return y


def _dispatch_kernel(dest_ref, pend_ref, h2_hbm, xb_hbm, stage, zeros, in_sem, row_sem, zero_sem):
    t = pl.program_id(0)
    nt = pl.num_programs(0)
    slot = t % DISPATCH_SLOTS

    tile_rows = TM // SUBLANES

    def stage_in(tile_idx, s):
        return pltpu.make_async_copy(h2_hbm.at[pl.ds(tile_idx * tile_rows, tile_rows)], stage.at[s], in_sem.at[s])

    def drain_rows(s):
        for _ in range(TOP_K):
            pltpu.make_async_copy(stage.at[s], stage.at[s], row_sem.at[s]).wait()

    def zero_block(start):
        return pltpu.make_async_copy(zeros, xb_hbm.at[pl.ds(pl.multiple_of(start, MOE_BLK), MOE_BLK)], zero_sem)

    def zero_fill(op):
        for e in range(N_EXPERTS):
            op(zero_block(jnp.maximum(pend_ref[e] - MOE_BLK, 0)))
        for j in range(N_EXPERTS):
            start = pend_ref[N_EXPERTS - 1] + j * MOE_BLK

            @pl.when(start < xb_hbm.shape[0])
            def _():
                op(zero_block(start))

    @pl.when(t == 0)
    def _():
        stage_in(0, 0).start()
        zeros[...] = jnp.zeros(zeros.shape, zeros.dtype)
        zero_fill(lambda cp: cp.start())
        zero_fill(lambda cp: cp.wait())

    @pl.when(jnp.logical_and(t == 0, nt > 1))
    def _():
        stage_in(1, 1).start()

    stage_in(t, slot).wait()

    def body(i, carry):
        for j in range(SUBLANES):
            for k in range(TOP_K):
                row = dest_ref[0, 0, k * TM + i * SUBLANES + j]
                pltpu.make_async_copy(stage.at[slot, i, pl.ds(j, 1)], xb_hbm.at[pl.ds(row, 1)],
                                      row_sem.at[slot]).start(priority=k % 2)
        return carry

    lax.fori_loop(0, tile_rows, body, 0)

    @pl.when(t > 0)
    def _():
        drain_rows((t + DISPATCH_SLOTS - 1) % DISPATCH_SLOTS)

    @pl.when(t + 2 < nt)
    def _():
        stage_in(t + 2, (t + 2) % DISPATCH_SLOTS).start()

    @pl.when(t == nt - 1)
    def _():
        drain_rows(slot)


def _dispatch(dest3, pend, h2, p_len):
    n, d = h2.shape
    return pl.pallas_call(
        _dispatch_kernel,
        grid=(n // TM,),
        in_specs=[pl.BlockSpec((1, 1, TOP_K * TM), lambda t: (t, 0, 0), memory_space=pltpu.SMEM),
                  pl.BlockSpec(memory_space=pltpu.SMEM), pl.BlockSpec(memory_space=pl.ANY)],
        out_specs=pl.BlockSpec(memory_space=pl.ANY),
        out_shape=jax.ShapeDtypeStruct((p_len, d), h2.dtype),
        scratch_shapes=[pltpu.VMEM((DISPATCH_SLOTS, TM // SUBLANES, SUBLANES, d), h2.dtype),
                        pltpu.VMEM((MOE_BLK, d), h2.dtype),
                        pltpu.SemaphoreType.DMA((DISPATCH_SLOTS,)), pltpu.SemaphoreType.DMA((DISPATCH_SLOTS,)),
                        pltpu.SemaphoreType.DMA(())],
        compiler_params=_cparams(("arbitrary",)),
        name="moe_dispatch",
    )(dest3, pend, h2.reshape(n // SUBLANES, SUBLANES, d))


def _pre_odd_kernel(x_ref, dcur_ref, dnxt_ref, cw_ref, yb_hbm, modp_ref, mod_ref, g_ref, w_ref,
                    x2_ref, xc_ref, p_ref, bg_ref, ybuf, sem, *, cw):
    slot = _moe_gather(dcur_ref, dnxt_ref, yb_hbm, ybuf, sem)
    for sl in _row_parts(PROJ_ROW_PARTS):
        x2 = x_ref[sl, :] + modp_ref[0, 5:6, :] * _combined_rows(cw_ref, ybuf, slot, sl)
        x2_ref[sl, :] = x2
        h = _rms_mod(x2, g_ref[...], mod_ref[0, 1:2, :], mod_ref[0, 0:1, :])
        proj = jnp.dot(h.astype(BF16), w_ref[...], preferred_element_type=F32)
        xc_ref[sl, :] = proj[:, :cw]
        hd = proj[:, cw:2 * cw]
        bg_ref[sl, :] = proj[:, 2 * cw:3 * cw]
        p_ref[sl, :] = proj[:, 3 * cw:] * hd


def _post_odd_kernel(x_ref, xc_ref, xcp_ref, xcn_ref, p_ref, pp_ref, pn_ref, bg_ref, mod_ref,
                     pw_ref, ps_ref, dw_ref, wo_ref, g2_ref, wr_ref, rb_ref,
                     x3_ref, h2_ref, info_ref, cnt_ref, extc_sc, extp_sc, *, npt, tps):
    i = pl.program_id(0)
    first, last = _seq_edges(i, npt, tps)
    _fill_ext(extc_sc, xc_ref, xcp_ref, xcn_ref, first, last)
    _fill_ext(extp_sc, p_ref, pp_ref, pn_ref, first, last)
    cw = xc_ref.shape[1]
    gc = cw // len(POOL_WINDOWS)
    is_prompt = i < npt
    t_len = jnp.where(is_prompt, TM, TM * tps)
    tile_pos = jnp.where(is_prompt, 0, ((i - npt) % tps) * TM)
    for rs in _row_parts(MIX_ROW_PARTS):
        nrows = rs.stop - rs.start
        base = HALO + rs.start
        pos = lax.broadcasted_iota(jnp.int32, (nrows, 1), 0) + (tile_pos + rs.start)
        ds = []
        for g, w in enumerate(POOL_WINDOWS):
            sl = slice(g * gc, (g + 1) * gc)
            acc = None
            for o in range(-(w // 2), w // 2):
                term = extc_sc[base + o:base + o + nrows, sl]
                acc = term if acc is None else acc + term
            lo = jnp.maximum(pos - w // 2, 0)
            hi = jnp.minimum(pos + w // 2 - 1, t_len - 1)
            cnt = (hi - lo + 1).astype(F32)
            ds.append(acc / cnt - extc_sc[base:base + nrows, sl])
        d = jnp.concatenate(ds, axis=1).astype(BF16)
        yc = jnp.dot(d, pw_ref[...], preferred_element_type=F32) * ps_ref[...]
        conv = None
        for k in range(SCONV_K):
            start = base - SCONV_K // 2 + k
            term = extp_sc[start:start + nrows, :] * dw_ref[k:k + 1, :]
            conv = term if conv is None else conv + term
        yd = bg_ref[rs, :] * conv
        m = (jnp.dot(yc.astype(BF16), wo_ref[0:cw, :], preferred_element_type=F32)
             + jnp.dot(yd.astype(BF16), wo_ref[cw:, :], preferred_element_type=F32))
        x3 = x_ref[rs, :] + mod_ref[0, 2:3, :] * m
        x3_ref[rs, :] = x3
        _moe_prologue(x3, rs, mod_ref, g2_ref, wr_ref, rb_ref, h2_ref, info_ref, cnt_ref)


def _final_kernel(x_ref, dcur_ref, dnxt_ref, cw_ref, yb_hbm, modp_ref, g_ref, op_ref, os_ref, ybuf, sem, *, npt):
    i = pl.program_id(0)
    slot = _moe_gather(dcur_ref, dnxt_ref, yb_hbm, ybuf, sem)
    x = x_ref[...] + modp_ref[0, 5:6, :] * _combined_rows(cw_ref, ybuf, slot, slice(0, TM))
    ms = jnp.mean(x * x, axis=-1, keepdims=True)
    out = x * lax.rsqrt(ms + EPS) * g_ref[...]

    @pl.when(i < npt)
    def _():
        op_ref[...] = out

    @pl.when(i >= npt)
    def _():
        os_ref[...] = out


def _expert_kernel(be_ref, nv_ref, x_ref, wg_ref, wu_ref, wd_ref, y_ref, wg_sc, wu_sc, wd_sc):
    i = pl.program_id(0)

    @pl.when(jnp.logical_or(i == 0, be_ref[i] != be_ref[jnp.maximum(i - 1, 0)]))
    def _():
        wg_sc[...] = wg_ref[0, 0].astype(BF16)
        wu_sc[...] = wu_ref[0, 0].astype(BF16)
        wd_sc[...] = wd_ref[0, 0].astype(BF16)

    @pl.when(i < nv_ref[0])
    def _():
        x = x_ref[...].astype(BF16)
        hg = jnp.dot(x, wg_sc[...], preferred_element_type=F32)
        hu = jnp.dot(x, wu_sc[...], preferred_element_type=F32)
        hid = (_silu(hg) * hu).astype(BF16)
        y_ref[...] = jnp.dot(hid, wd_sc[...], preferred_element_type=F32)

    @pl.when(i >= nv_ref[0])
    def _():
        y_ref[...] = jnp.zeros(y_ref.shape, F32)


def _experts(blk_e, nvalid, xb, layer, wg, wu, wd):
    p_len, d = xb.shape
    hid = wg.shape[3]
    nblk = p_len // MOE_BLK
    grid_spec = pltpu.PrefetchScalarGridSpec(
        num_scalar_prefetch=2,
        grid=(nblk,),
        in_specs=[pl.BlockSpec((MOE_BLK, d), lambda i, be, nv: (jnp.minimum(i, nv[0] - 1), 0)),
                  pl.BlockSpec((1, 1, d, hid), lambda i, be, nv: (layer, be[i], 0, 0)),
                  pl.BlockSpec((1, 1, d, hid), lambda i, be, nv: (layer, be[i], 0, 0)),
                  pl.BlockSpec((1, 1, hid, d), lambda i, be, nv: (layer, be[i], 0, 0))],
        out_specs=pl.BlockSpec((MOE_BLK, d), lambda i, be, nv: (i, 0)),
        scratch_shapes=[pltpu.VMEM((d, hid), BF16), pltpu.VMEM((d, hid), BF16), pltpu.VMEM((hid, d), BF16)],
    )
    return pl.pallas_call(
        _expert_kernel,
        grid_spec=grid_spec,
        out_shape=jax.ShapeDtypeStruct((p_len, d), F32),
        compiler_params=_cparams(("arbitrary",)),
        name="moe_experts",
    )(blk_e, nvalid, xb, wg, wu, wd)


def _moe(h2, info, cnt, layer, wg, wu, wd):
    n, d = h2.shape
    cnt = cnt[:, :, 0].astype(jnp.int32)
    counts = jnp.sum(cnt, axis=0)
    pc = ((counts + MOE_BLK - 1) // MOE_BLK) * MOE_BLK
    pend = jnp.cumsum(pc)
    base = (pend - pc)[None, :] + jnp.cumsum(cnt, axis=0) - cnt
    eid = info[:, 0:2, :].astype(jnp.int32)
    rank = info[:, 4:6, :].astype(jnp.int32)
    sel = eid[..., None] == jnp.arange(N_EXPERTS, dtype=jnp.int32)
    dest = jnp.sum(jnp.where(sel, base[:, None, None, :], 0), axis=-1) + rank
    p_len = n * TOP_K + N_EXPERTS * MOE_BLK
    nblk = p_len // MOE_BLK
    blk_start = jnp.arange(nblk, dtype=jnp.int32) * MOE_BLK
    blk_e = jnp.minimum(jnp.sum(pend[None, :] <= blk_start[:, None], axis=1), N_EXPERTS - 1).astype(jnp.int32)
    nvalid = (pend[-1:] // MOE_BLK).astype(jnp.int32)
    dest3 = dest.reshape(dest.shape[0], 1, TOP_K * TM)
    wcol = jnp.transpose(info[:, 2:2 + TOP_K, :], (0, 2, 1)).reshape(n, TOP_K)
    xb = _dispatch(dest3, pend.astype(jnp.int32), h2, p_len)
    yb = _experts(blk_e, nvalid, xb, layer, wg, wu, wd)
    return dest3, wcol, yb


def _rope_tables(t_len, qw):
    rows = t_len // GRID_W
    ax = qw // (DIFF_HEADS * 2) // 2
    row = jnp.repeat(jnp.arange(rows, dtype=F32), GRID_W)
    col = jnp.tile(jnp.arange(GRID_W, dtype=F32), rows)
    inv = ROPE_BASE ** (-jnp.arange(0, ax, 2, dtype=F32) / ax)
    ar = row[:, None] * inv[None, :]
    ac = col[:, None] * inv[None, :]
    cr, sr, cc, sc = jnp.cos(ar), jnp.sin(ar), jnp.cos(ac), jnp.sin(ac)
    cos = jnp.concatenate([cr, cr, cc, cc], axis=1)
    sins = jnp.concatenate([-sr, sr, -sc, sc], axis=1)
    reps = qw // cos.shape[1]
    cos = jnp.concatenate([jnp.ones((TM, cos.shape[1]), F32), cos], axis=0)
    sins = jnp.concatenate([jnp.zeros((TM, sins.shape[1]), F32), sins], axis=0)
    return jnp.tile(cos, (1, reps)), jnp.tile(sins, (1, reps))


def kernel(x_prompt, x_sample, cache_k, cache_v, c, c_ctx, w_ada, b_ada, g_norm1, g_norm2, g_final, w_in_e, conv_a_w, conv_a_b, ln_a_g, ln_a_b, lam_q1, lam_k1, lam_q2, lam_k2, subln_g, w_out_e, w_in_o, pool_w, pool_scale, conv_d_w, w_out_o, w_router_g, b_router_g, w_router_e, b_router_e, w_gate, w_up, w_down):
    bp, tp, d = x_prompt.shape
    bs, ts, _ = x_sample.shape
    past = cache_k.shape[2]
    assert tp == TM and ts % TM == 0 and d % LANES == 0
    npt = bp
    tps = ts // TM
    n_p, n_s = bp * tp, bs * ts
    n = n_p + n_s
    nt = n // TM
    cw = conv_a_w.shape[2]
    qw = (w_in_e.shape[2] - 2 * cw) // 3
    dh = qw // (DIFF_HEADS * 2)
    hpt = TM // HALO

    def cond_of(i):
        return jnp.where(i < npt, 0, 1 + (i - npt) // tps)

    def tile(w):
        return pl.BlockSpec((TM, w), lambda i: (i, 0))

    def halo_prev(w):
        return pl.BlockSpec((HALO, w), lambda i: (jnp.maximum(i * hpt - 1, 0), 0))

    def halo_next(w):
        return pl.BlockSpec((HALO, w), lambda i: (jnp.minimum((i + 1) * hpt, nt * hpt - 1), 0))

    def whole(shape):
        return pl.BlockSpec(shape, lambda i: (0,) * len(shape))

    mod_spec = pl.BlockSpec((1, 6, d), lambda i: (cond_of(i), 0, 0))
    combine_specs = [pl.BlockSpec((1, 1, TOP_K * TM), lambda i: (i, 0, 0), memory_space=pltpu.SMEM),
                     pl.BlockSpec((1, 1, TOP_K * TM), lambda i: (jnp.minimum(i + 1, nt - 1), 0, 0),
                                  memory_space=pltpu.SMEM),
                     tile(TOP_K), pl.BlockSpec(memory_space=pl.ANY)]
    combine_scratch = [pltpu.VMEM((2, TOP_K, TM // SUBLANES, SUBLANES, d), F32), pltpu.SemaphoreType.DMA((2,))]

    rows = 8 * ((1 + bs + 7) // 8)
    cond = jnp.concatenate([c_ctx[None, :], c, jnp.zeros((rows - 1 - bs, d), F32)], axis=0)
    mod = _ada_table(cond, w_ada, b_ada)

    xp2, xs2 = x_prompt.reshape(n_p, d), x_sample.reshape(n_s, d)

    def prompt_tile(w):
        return pl.BlockSpec((TM, w), lambda i: (jnp.minimum(i, npt - 1), 0))

    def sample_tile(w):
        return pl.BlockSpec((TM, w), lambda i: (jnp.maximum(i - npt, 0), 0))

    def router_w(i):
        wre = jnp.transpose(w_router_e[i], (1, 0, 2)).reshape(d, N_EXPERTS)
        wr = jnp.concatenate([wre, w_router_g[i]], axis=1)
        return jnp.pad(wr, ((0, 0), (0, ROUTER_PAD - wr.shape[1]))).astype(BF16)

    def router_b(i):
        b = jnp.concatenate([b_router_e[i].reshape(N_EXPERTS), b_router_g[i]])
        return jnp.pad(b, (0, ROUTER_PAD - b.shape[0]))[:, None]

    cos, sins = _rope_tables(ts, qw)
    tab_spec = pl.BlockSpec((TM, qw), lambda i: (jnp.where(i < npt, 0, 1 + (i - npt) % tps), 0))
    kc_spec = pl.BlockSpec((1, 1, TM, DIFF_HEADS, 2, dh), lambda i: (jnp.minimum(i, npt - 1), 0, 0, 0, 0, 0))
    vc_spec = pl.BlockSpec((1, 1, TM, DIFF_HEADS, 2 * dh), lambda i: (jnp.minimum(i, npt - 1), 0, 0, 0, 0))
    u, q, k, vt, new_k, new_v = pl.pallas_call(
        functools.partial(_pre_even_kernel, npt=npt, cw=cw, qw=qw, qscale=dh ** -0.5 * math.log2(math.e)),
        grid=(nt,),
        in_specs=[prompt_tile(d), sample_tile(d), mod_spec, whole((1, d)), whole(w_in_e.shape[1:]),
                  tab_spec, tab_spec],
        out_specs=[tile(cw), tile(qw), tile(qw), pl.BlockSpec((qw, TM), lambda i: (0, i)),
                   kc_spec, vc_spec],
        out_shape=[jax.ShapeDtypeStruct((n, cw), F32), jax.ShapeDtypeStruct((n, qw), BF16),
                   jax.ShapeDtypeStruct((n, qw), BF16), jax.ShapeDtypeStruct((qw, n), BF16),
                   jax.ShapeDtypeStruct((bp, 1, tp, DIFF_HEADS, 2, dh), F32),
                   jax.ShapeDtypeStruct((bp, 1, tp, DIFF_HEADS, 2 * dh), F32)],
        scratch_shapes=[pltpu.VMEM((2, TM, qw), F32)],
        compiler_params=_cparams(("arbitrary",)),
        name="pre_even",
    )(xp2, xs2, mod[0], g_norm1[0][None, :], w_in_e[0].astype(BF16), cos, sins)

    lam_init = 0.8 - 0.6 * math.exp(-0.3 * 0)
    lam = (jnp.exp(jnp.sum(lam_q1[0] * lam_k1[0])) - jnp.exp(jnp.sum(lam_q2[0] * lam_k2[0])) + lam_init)
    lam = lam.reshape(1).astype(F32)
    sub_g = subln_g[0][:, None]
    o_p = _attention(lam, q, k, vt, sub_g, None, batch=bp, t_len=tp, row0=0, tq=TM, tk=TM,
                     post_scale=1.0 - lam_init)
    k_ctx = cache_k[:, 0].reshape(bs * past, qw).astype(BF16)
    vt_ctx = jnp.transpose(cache_v[:, 0].reshape(bs, past, qw), (0, 2, 1)).astype(BF16)
    o_s = _attention(lam, q, k, vt, sub_g, (k_ctx, vt_ctx), batch=bs, t_len=ts, row0=n_p, tq=1024, tk=512,
                     post_scale=1.0 - lam_init)

    moe_outs = [jax.ShapeDtypeStruct((n, d), F32), jax.ShapeDtypeStruct((n, d), F32),
                jax.ShapeDtypeStruct((nt, 8, TM), F32), jax.ShapeDtypeStruct((nt, N_EXPERTS, 1), F32)]
    moe_out_specs = [tile(d), tile(d), pl.BlockSpec((1, 8, TM), lambda i: (i, 0, 0)),
                     pl.BlockSpec((1, N_EXPERTS, 1), lambda i: (i, 0, 0))]
    x1, h2, info, cnt = pl.pallas_call(
        functools.partial(_post_even_kernel, npt=npt, tps=tps),
        grid=(nt,),
        in_specs=[prompt_tile(d), sample_tile(d), tile(cw), halo_prev(cw), halo_next(cw),
                  prompt_tile(qw), sample_tile(qw), mod_spec,
                  whole((CONV_K, cw)), whole((1, cw)), whole((1, cw)), whole((1, cw)),
                  whole(w_out_e.shape[1:]), whole((1, d)), whole((d, ROUTER_PAD)), whole((ROUTER_PAD, 1))],
        out_specs=moe_out_specs,
        out_shape=moe_outs,
        scratch_shapes=[pltpu.VMEM((TM + 2 * HALO, cw), F32),
                        pltpu.VMEM((SUBLANES, TM + 2 * HALO - SUBLANES, cw), F32)],
        compiler_params=_cparams(("parallel",)),
        name="post_even",
    )(xp2, xs2, u, u, u, o_p, o_s, mod[0], conv_a_w[0], conv_a_b[0][None, :], ln_a_g[0][None, :], ln_a_b[0][None, :],
      w_out_e[0].astype(BF16), g_norm2[0][None, :], router_w(0), router_b(0))
    dest3, wcol, yb = _moe(h2, info, cnt, 0, w_gate, w_up, w_down)

    pw = pool_w.shape[2]
    x2, xc, p, bg = pl.pallas_call(
        functools.partial(_pre_odd_kernel, cw=cw),
        grid=(nt,),
        in_specs=[tile(d)] + combine_specs + [mod_spec, mod_spec, whole((1, d)), whole(w_in_o.shape[1:])],
        out_specs=[tile(d), tile(cw), tile(cw), tile(cw)],
        out_shape=[jax.ShapeDtypeStruct((n, d), F32)] + [jax.ShapeDtypeStruct((n, cw), F32)] * 3,
        scratch_shapes=combine_scratch,
        compiler_params=_cparams(("arbitrary",)),
        name="pre_odd",
    )(x1, dest3, dest3, wcol, yb, mod[0], mod[1], g_norm1[1][None, :], w_in_o[0].astype(BF16))
    pool_bd = jnp.zeros((cw, cw), F32)
    for g in range(len(POOL_WINDOWS)):
        pool_bd = pool_bd.at[g * pw:(g + 1) * pw, g * pw:(g + 1) * pw].set(pool_w[0, g])
    x3, h2, info, cnt = pl.pallas_call(
        functools.partial(_post_odd_kernel, npt=npt, tps=tps),
        grid=(nt,),
        in_specs=[tile(d), tile(cw), halo_prev(cw), halo_next(cw), tile(cw), halo_prev(cw), halo_next(cw),
                  tile(cw), mod_spec, whole((cw, cw)), whole((1, cw)), whole((SCONV_K, cw)),
                  whole(w_out_o.shape[1:]), whole((1, d)), whole((d, ROUTER_PAD)), whole((ROUTER_PAD, 1))],
        out_specs=moe_out_specs,
        out_shape=moe_outs,
        scratch_shapes=[pltpu.VMEM((TM + 2 * HALO, cw), F32), pltpu.VMEM((TM + 2 * HALO, cw), F32)],
        compiler_params=_cparams(("parallel",)),
        name="post_odd",
    )(x2, xc, xc, xc, p, p, p, bg, mod[1], pool_bd.astype(BF16), pool_scale[0][None, :], conv_d_w[0],
      w_out_o[0].astype(BF16), g_norm2[1][None, :], router_w(1), router_b(1))
    dest3, wcol, yb = _moe(h2, info, cnt, 1, w_gate, w_up, w_down)

    out_p, out_s = pl.pallas_call(
        functools.partial(_final_kernel, npt=npt),
        grid=(nt,),
        in_specs=[tile(d)] + combine_specs + [mod_spec, whole((1, d))],
        out_specs=[pl.BlockSpec((TM, d), lambda i: (jnp.minimum(i, npt - 1), 0)),
                   pl.BlockSpec((TM, d), lambda i: (jnp.maximum(i - npt, 0), 0))],
        out_shape=[jax.ShapeDtypeStruct((n_p, d), F32), jax.ShapeDtypeStruct((n_s, d), F32)],
        scratch_shapes=combine_scratch,
        compiler_params=_cparams(("arbitrary",)),
        name="final_norm",
    )(x3, dest3, dest3, wcol, yb, mod[1], g_final[None, :])

    return (out_p.reshape(bp, tp, d), out_s.reshape(bs, ts, d), new_k, new_v)
```

```python
import functools
import math

import jax
import jax.numpy as jnp
from jax import lax
from jax.experimental import pallas as pl
from jax.experimental.pallas import tpu as pltpu

F32 = jnp.float32
BF16 = jnp.bfloat16

EPS = 1e-6
GRID_W = 64
ROPE_BASE = 10000.0
CONV_K = 31
SCONV_K = 3
POOL_WINDOWS = (2, 4, 8, 16)
DIFF_HEADS = 4
N_GROUPS = 4
EXPERTS_PER_GROUP = 8
N_EXPERTS = N_GROUPS * EXPERTS_PER_GROUP
TOP_K = 2

LANES = 128
SUBLANES = 8
TM = 256
HALO = 16
PROJ_ROW_PARTS = 2
MIX_ROW_PARTS = 1
MOE_BLK = 512
ATTN_TQ = 2048
ATTN_TK = 512
CONV_ROWS = 32
INFO_ROWS = 8
ROUTER_PAD = 128
ONES_ROWS = 16
DISPATCH_SLOTS = 3
VMEM_LIMIT = 56 * 1024 * 1024


def _cparams(sem):
    return pltpu.CompilerParams(dimension_semantics=sem, vmem_limit_bytes=VMEM_LIMIT)


def _rms_mod(x, g, sc, sh):
    ms = jnp.mean(x * x, axis=-1, keepdims=True)
    return (x * lax.rsqrt(ms + EPS)) * g * (1.0 + sc) + sh


def _silu(x):
    return x * jax.nn.sigmoid(x)


def _ada_kernel(c_ref, w_ref, b_ref, o_ref):
    cs = _silu(c_ref[...])
    o_ref[0] = jnp.dot(cs, w_ref[0], precision=lax.Precision.HIGHEST,
                       preferred_element_type=F32) + b_ref[0]


def _ada_table(cond, w_ada, b_ada):
    depth, d, six_d = w_ada.shape
    rows = cond.shape[0]
    nj = six_d // d
    out = pl.pallas_call(
        _ada_kernel,
        grid=(depth, nj),
        in_specs=[pl.BlockSpec((rows, d), lambda l, j: (0, 0)),
                  pl.BlockSpec((1, d, d), lambda l, j: (l, 0, j)),
                  pl.BlockSpec((1, 1, d), lambda l, j: (l, 0, j))],
        out_specs=pl.BlockSpec((1, rows, d), lambda l, j: (l, 0, j)),
        out_shape=jax.ShapeDtypeStruct((depth, rows, six_d), F32),
        compiler_params=_cparams(("parallel", "parallel")),
        name="ada_table",
    )(cond, w_ada, b_ada.reshape(depth, 1, six_d))
    return out.reshape(depth, rows, nj, d)


def _row_parts(parts):
    rows = TM // parts
    return [slice(p * rows, (p + 1) * rows) for p in range(parts)]


def _rope(x, cos, sins, half):
    lane = lax.broadcasted_iota(jnp.int32, (1, LANES), 1)
    first_half = (lane % (2 * half)) < half
    outs = []
    for j in range(x.shape[1] // LANES):
        sl = slice(j * LANES, (j + 1) * LANES)
        xs = x[:, sl]
        nxt = pltpu.roll(xs, LANES - half, 1)
        prv = pltpu.roll(xs, half, 1)
        rot = jnp.where(first_half, nxt, prv)
        outs.append(xs * cos[:, sl] + rot * sins[:, sl])
    return jnp.concatenate(outs, axis=1)


def _pre_even_kernel(xp_ref, xs_ref, mod_ref, g_ref, w_ref, cos_ref, sin_ref,
                     u_ref, q_ref, k_ref, vt_ref, kc_ref, vc_ref, kv_sc, *, npt, cw, qw, qscale):
    i = pl.program_id(0)
    is_prompt = i < npt
    half = kc_ref.shape[-1] // 4
    for sl in _row_parts(PROJ_ROW_PARTS):
        x = jnp.where(is_prompt, xp_ref[sl, :], xs_ref[sl, :])
        h = _rms_mod(x, g_ref[...], mod_ref[0, 1:2, :], mod_ref[0, 0:1, :])
        proj = jnp.dot(h.astype(BF16), w_ref[...], preferred_element_type=F32)
        ga = proj[:, :cw]
        gb = proj[:, cw:2 * cw]
        u_ref[sl, :] = ga * jax.nn.sigmoid(gb)
        q = proj[:, 2 * cw:2 * cw + qw]
        k = proj[:, 2 * cw + qw:2 * cw + 2 * qw]
        v = proj[:, 2 * cw + 2 * qw:]
        kv_sc[0, sl, :] = k
        kv_sc[1, sl, :] = v
        cos = cos_ref[sl, :]
        sins = sin_ref[sl, :]
        q_ref[sl, :] = (_rope(q, cos, sins, half) * qscale).astype(BF16)
        k_ref[sl, :] = _rope(k, cos, sins, half).astype(BF16)
        vt_ref[:, sl] = v.T.astype(BF16)

    @pl.when(is_prompt)
    def _():
        heads, halves, dh = kc_ref.shape[3:]
        for h in range(heads):
            vc_ref[0, 0, :, h, :] = kv_sc[1, :, h * halves * dh:(h + 1) * halves * dh]
            for c in range(halves):
                lo = (h * halves + c) * dh
                kc_ref[0, 0, :, h, c, :] = kv_sc[0, :, lo:lo + dh]


def _attn_kernel(lam_ref, q_ref, k_ref, vt_ref, *rest, nk, tk, nk_ctx, post_scale):
    if nk_ctx:
        kc_ref, vtc_ref, g_ref, o_ref, s_sc, mc_sc, m_sc, acc_sc = rest
    else:
        g_ref, o_ref, s_sc, mc_sc, m_sc, acc_sc = rest
    q = q_ref[...]
    hw = q.shape[1]
    dh = hw // 2
    lane = lax.broadcasted_iota(jnp.int32, (1, hw), 1)
    zero = jnp.zeros_like(q)
    qs = (jnp.where(lane < dh, q, zero), jnp.where(lane >= dh, q, zero))
    m_sc[...] = jnp.full(m_sc.shape, -jnp.inf, F32)
    acc_sc[...] = jnp.zeros(acc_sc.shape, F32)
    ones_rows = (lax.broadcasted_iota(jnp.int32, (ONES_ROWS, tk), 0) == 0).astype(BF16)

    def scores(kk, slot):
        for c in range(2):
            s = lax.dot_general(kk, qs[c], (((1,), (1,)), ((), ())), preferred_element_type=F32)
            s_sc[slot, c] = s
            mc_sc[slot, c] = jnp.max(s, axis=0, keepdims=True)

    def consume(vt, slot):
        vta = jnp.concatenate([vt, ones_rows], axis=0)
        for c in range(2):
            m_prev = m_sc[c]
            m_new = jnp.maximum(m_prev, mc_sc[slot, c])
            alpha = jnp.exp2(m_prev - m_new)
            p = jnp.exp2(s_sc[slot, c] - m_new).astype(BF16)
            acc_sc[c] = alpha * acc_sc[c] + jnp.dot(vta, p, preferred_element_type=F32)
            m_sc[c] = m_new

    def k_at(i):
        if i < nk_ctx:
            return kc_ref[i * tk:(i + 1) * tk, :]
        return k_ref[(i - nk_ctx) * tk:(i - nk_ctx + 1) * tk, :]

    def vt_at(i):
        if i < nk_ctx:
            return vtc_ref[0, :, i * tk:(i + 1) * tk]
        return vt_ref[:, (i - nk_ctx) * tk:(i - nk_ctx + 1) * tk]

    n_total = nk_ctx + nk
    peel = nk_ctx + (n_total - 1 - nk_ctx) % 2
    scores(k_at(0), 0)
    for i in range(peel):
        scores(k_at(i + 1), (i + 1) % 2)
        consume(vt_at(i), i % 2)

    def body(t, carry):
        for r in range(2):
            j = 2 * t + (peel + r - nk_ctx)
            nxt = pl.multiple_of((j + 1) * tk, tk)
            cur = pl.multiple_of(j * tk, tk)
            scores(k_ref[pl.ds(nxt, tk), :], (peel + r + 1) % 2)
            consume(vt_ref[:, pl.ds(cur, tk)], (peel + r) % 2)
        return carry

    lax.fori_loop(0, (n_total - 1 - peel) // 2, body, 0)
    consume(vt_at(n_total - 1), (n_total - 1) % 2)
    o = (acc_sc[0, :hw, :] / acc_sc[0, hw:hw + 1, :]
         - lam_ref[0] * (acc_sc[1, :hw, :] / acc_sc[1, hw:hw + 1, :]))
    ms = jnp.mean(o * o, axis=0, keepdims=True)
    o = o * lax.rsqrt(ms + EPS) * g_ref[...] * post_scale
    o_ref[...] = o.T.astype(BF16)


def _attention(lam, q, k, vt, g_col, ctx, *, batch, t_len, row0, tq, tk, post_scale):
    hw = q.shape[1] // DIFF_HEADS
    nq = t_len // tq
    assert row0 % t_len == 0 and t_len % tk == 0
    qb0 = row0 // tq
    sb0 = row0 // t_len
    in_specs = [pl.BlockSpec(memory_space=pltpu.SMEM),
                pl.BlockSpec((tq, hw), lambda b, h, i: (qb0 + b * nq + i, h)),
                pl.BlockSpec((t_len, hw), lambda b, h, i: (sb0 + b, h)),
                pl.BlockSpec((hw, t_len), lambda b, h, i: (h, sb0 + b))]
    args = [lam, q, k, vt]
    nk_ctx = 0
    if ctx is not None:
        k_ctx, vt_ctx = ctx
        past = vt_ctx.shape[2]
        assert past % tk == 0
        nk_ctx = past // tk
        in_specs += [pl.BlockSpec((past, hw), lambda b, h, i: (b, h)),
                     pl.BlockSpec((1, hw, past), lambda b, h, i: (b, h, 0))]
        args += [k_ctx, vt_ctx]
    in_specs.append(pl.BlockSpec((hw, 1), lambda b, h, i: (0, 0)))
    args.append(g_col)
    return pl.pallas_call(
        functools.partial(_attn_kernel, nk=t_len // tk, tk=tk, nk_ctx=nk_ctx, post_scale=post_scale),
        grid=(batch, DIFF_HEADS, nq),
        in_specs=in_specs,
        out_specs=pl.BlockSpec((tq, hw), lambda b, h, i: (b * nq + i, h)),
        out_shape=jax.ShapeDtypeStruct((batch * t_len, q.shape[1]), BF16),
        scratch_shapes=[pltpu.VMEM((2, 2, tk, tq), F32), pltpu.VMEM((2, 2, 1, tq), F32),
                        pltpu.VMEM((2, 1, tq), F32), pltpu.VMEM((2, hw + ONES_ROWS, tq), F32)],
        compiler_params=_cparams(("parallel", "parallel", "parallel")),
        name="diff_attention",
    )(*args)


def _seq_edges(i, npt, tps):
    r = (i - npt) % tps
    is_prompt = i < npt
    first = jnp.logical_or(is_prompt, r == 0)
    last = jnp.logical_or(is_prompt, r == tps - 1)
    return first, last


def _fill_ext(ext_sc, cur_ref, prev_ref, next_ref, first, last):
    prev = prev_ref[...]
    nxt = next_ref[...]
    ext_sc[0:HALO, :] = jnp.where(first, jnp.zeros_like(prev), prev)
    ext_sc[HALO:HALO + TM, :] = cur_ref[...]
    ext_sc[HALO + TM:, :] = jnp.where(last, jnp.zeros_like(nxt), nxt)


def _route_tile(logits, b_ref, info_ref, cnt_ref):
    lt = logits.T + b_ref[...]
    gl = lt[N_EXPERTS:N_EXPERTS + N_GROUPS]
    ge = jnp.exp(gl - jnp.max(gl, axis=0, keepdims=True))
    gp = ge / jnp.sum(ge, axis=0, keepdims=True)
    best = gp[0:1]
    gidx = jnp.zeros(best.shape, jnp.int32)
    el = lt[0:EXPERTS_PER_GROUP]
    for g in range(1, N_GROUPS):
        better = gp[g:g + 1] > best
        gidx = jnp.where(better, g, gidx)
        best = jnp.where(better, gp[g:g + 1], best)
    for g in range(1, N_GROUPS):
        el = jnp.where(gidx == g, lt[g * EXPERTS_PER_GROUP:(g + 1) * EXPERTS_PER_GROUP], el)
    rio = lax.broadcasted_iota(jnp.int32, el.shape, 0)
    v0 = jnp.max(el, axis=0, keepdims=True)
    i0 = jnp.min(jnp.where(el == v0, rio, EXPERTS_PER_GROUP), axis=0, keepdims=True)
    el2 = jnp.where(rio == i0, -jnp.inf, el)
    v1 = jnp.max(el2, axis=0, keepdims=True)
    i1 = jnp.min(jnp.where(el2 == v1, rio, EXPERTS_PER_GROUP), axis=0, keepdims=True)
    e1 = jnp.exp(v1 - v0)
    den = 1.0 + e1
    w0 = (1.0 / den) * best
    w1 = (e1 / den) * best
    eid0 = gidx * EXPERTS_PER_GROUP + i0
    eid1 = gidx * EXPERTS_PER_GROUP + i1
    tm = lt.shape[1]
    eio = lax.broadcasted_iota(jnp.int32, (N_EXPERTS, tm), 0)
    oh0 = eio == eid0
    oh1 = eio == eid1
    ohs = jnp.where(oh0, 1.0, 0.0) + jnp.where(oh1, 1.0, 0.0)
    upper = (lax.broadcasted_iota(jnp.int32, (tm, tm), 0)
             < lax.broadcasted_iota(jnp.int32, (tm, tm), 1)).astype(BF16)
    before = jnp.dot(ohs.astype(BF16), upper, preferred_element_type=F32)
    rank0 = jnp.sum(jnp.where(oh0, before, 0.0), axis=0, keepdims=True)
    rank1 = jnp.sum(jnp.where(oh1, before, 0.0), axis=0, keepdims=True)
    info_ref[0] = jnp.concatenate([eid0.astype(F32), eid1.astype(F32), w0, w1, rank0, rank1,
                                   jnp.zeros((INFO_ROWS - 6, tm), F32)], axis=0)
    cnt_ref[0] = jnp.sum(ohs, axis=1, keepdims=True)


def _moe_prologue(x_new, sl, mod_ref, g2_ref, wr_ref, rb_ref, h2_ref, info_ref, cnt_ref):
    h2 = _rms_mod(x_new, g2_ref[...], mod_ref[0, 4:5, :], mod_ref[0, 3:4, :])
    h2_ref[sl, :] = h2
    assert sl.stop - sl.start == TM
    _route_tile(jnp.dot(h2.astype(BF16), wr_ref[...], preferred_element_type=F32), rb_ref, info_ref, cnt_ref)


def _post_even_kernel(xp_ref, xs_ref, u_ref, up_ref, un_ref, op_ref, os_ref, mod_ref, cw_ref, cb_ref,
                      lng_ref, lnb_ref, wo_ref, g2_ref, wr_ref, rb_ref, x1_ref, h2_ref, info_ref, cnt_ref,
                      ext_sc, sh_sc, *, npt, tps):
    i = pl.program_id(0)
    first, last = _seq_edges(i, npt, tps)
    _fill_ext(ext_sc, u_ref, up_ref, un_ref, first, last)
    cw = u_ref.shape[1]
    span = sh_sc.shape[1]
    for r in range(SUBLANES):
        sh_sc[r] = ext_sc[r:r + span, :]
    rows = CONV_ROWS
    is_prompt = i < npt
    for sl in _row_parts(MIX_ROW_PARTS):
        chunks = []
        for r0 in range(sl.start, sl.stop, rows):
            acc = None
            for k in range(CONV_K):
                off = HALO - CONV_K // 2 + k
                start = r0 + off - off % SUBLANES
                term = sh_sc[off % SUBLANES, start:start + rows, :] * cw_ref[k:k + 1, :]
                acc = term if acc is None else acc + term
            chunks.append(acc)
        conv = jnp.concatenate(chunks, axis=0) + cb_ref[...]
        mu = jnp.mean(conv, axis=-1, keepdims=True)
        cen = conv - mu
        var = jnp.mean(cen * cen, axis=-1, keepdims=True)
        uu = _silu(cen * lax.rsqrt(var + EPS) * lng_ref[...] + lnb_ref[...])
        o = jnp.where(is_prompt, op_ref[sl, :], os_ref[sl, :])
        m = (jnp.dot(uu.astype(BF16), wo_ref[0:cw, :], preferred_element_type=F32)
             + jnp.dot(o, wo_ref[cw:, :], preferred_element_type=F32))
        x1 = jnp.where(is_prompt, xp_ref[sl, :], xs_ref[sl, :]) + mod_ref[0, 2:3, :] * m
        x1_ref[sl, :] = x1
        _moe_prologue(x1, sl, mod_ref, g2_ref, wr_ref, rb_ref, h2_ref, info_ref, cnt_ref)


def _gather_start(dest_ref, yb_hbm, ybuf, sem, slot):
    def body(i, carry):
        for j in range(SUBLANES):
            for k in range(TOP_K):
                row = dest_ref[0, 0, k * TM + i * SUBLANES + j]
                pltpu.make_async_copy(yb_hbm.at[pl.ds(row, 1)], ybuf.at[slot, k, i, pl.ds(j, 1)],
                                      sem.at[slot]).start(priority=k % 2)
        return carry

    lax.fori_loop(0, TM // SUBLANES, body, 0)


def _moe_gather(dcur_ref, dnxt_ref, yb_hbm, ybuf, sem):
    t = pl.program_id(0)
    slot = t % 2

    @pl.when(t == 0)
    def _():
        _gather_start(dcur_ref, yb_hbm, ybuf, sem, 0)

    @pl.when(t + 1 < pl.num_programs(0))
    def _():
        _gather_start(dnxt_ref, yb_hbm, ybuf, sem, 1 - slot)

    for k in range(TOP_K):
        pltpu.make_async_copy(ybuf.at[slot, k], ybuf.at[slot, k], sem.at[slot]).wait()
    return slot


def _combined_rows(cw_ref, ybuf, slot, sl):
    d = ybuf.shape[-1]
    g = slice(sl.start // SUBLANES, sl.stop // SUBLANES)
    w = cw_ref[sl, :]
    y = w[:, 0:1] * ybuf[slot, 0, g].reshape(sl.stop - sl.start, d)
    for k in range(1, TOP_K):
        y = y + w[:, k:k + 1] * ybuf[slot, k, g].reshape(sl.stop - sl.start, d)
    return y


def _dispatch_kernel(dest_ref, pend_ref, h2_hbm, xb_hbm, stage, zeros, in_sem, row_sem, zero_sem):
    t = pl.program_id(0)
    nt = pl.num_programs(0)
    slot = t % DISPATCH_SLOTS

    tile_rows = TM // SUBLANES

    def stage_in(tile_idx, s):
        return pltpu.make_async_copy(h2_hbm.at[pl.ds(tile_idx * tile_rows, tile_rows)], stage.at[s], in_sem.at[s])

    def drain_rows(s):
        for _ in range(TOP_K):
            pltpu.make_async_copy(stage.at[s], stage.at[s], row_sem.at[s]).wait()

    def zero_block(start):
        return pltpu.make_async_copy(zeros, xb_hbm.at[pl.ds(pl.multiple_of(start, MOE_BLK), MOE_BLK)], zero_sem)

    def zero_fill(op):
        for e in range(N_EXPERTS):
            op(zero_block(jnp.maximum(pend_ref[e] - MOE_BLK, 0)))
        for j in range(N_EXPERTS):
            start = pend_ref[N_EXPERTS - 1] + j * MOE_BLK

            @pl.when(start < xb_hbm.shape[0])
            def _():
                op(zero_block(start))

    @pl.when(t == 0)
    def _():
        stage_in(0, 0).start()
        zeros[...] = jnp.zeros(zeros.shape, zeros.dtype)
        zero_fill(lambda cp: cp.start())
        zero_fill(lambda cp: cp.wait())

    @pl.when(jnp.logical_and(t == 0, nt > 1))
    def _():
        stage_in(1, 1).start()

    stage_in(t, slot).wait()

    def body(i, carry):
        for j in range(SUBLANES):
            for k in range(TOP_K):
                row = dest_ref[0, 0, k * TM + i * SUBLANES + j]
                pltpu.make_async_copy(stage.at[slot, i, pl.ds(j, 1)], xb_hbm.at[pl.ds(row, 1)],
                                      row_sem.at[slot]).start(priority=k % 2)
        return carry

    lax.fori_loop(0, tile_rows, body, 0)

    @pl.when(t > 0)
    def _():
        drain_rows((t + DISPATCH_SLOTS - 1) % DISPATCH_SLOTS)

    @pl.when(t + 2 < nt)
    def _():
        stage_in(t + 2, (t + 2) % DISPATCH_SLOTS).start()

    @pl.when(t == nt - 1)
    def _():
        drain_rows(slot)


def _dispatch(dest3, pend, h2, p_len):
    n, d = h2.shape
    return pl.pallas_call(
        _dispatch_kernel,
        grid=(n // TM,),
        in_specs=[pl.BlockSpec((1, 1, TOP_K * TM), lambda t: (t, 0, 0), memory_space=pltpu.SMEM),
                  pl.BlockSpec(memory_space=pltpu.SMEM), pl.BlockSpec(memory_space=pl.ANY)],
        out_specs=pl.BlockSpec(memory_space=pl.ANY),
        out_shape=jax.ShapeDtypeStruct((p_len, d), h2.dtype),
        scratch_shapes=[pltpu.VMEM((DISPATCH_SLOTS, TM // SUBLANES, SUBLANES, d), h2.dtype),
                        pltpu.VMEM((MOE_BLK, d), h2.dtype),
                        pltpu.SemaphoreType.DMA((DISPATCH_SLOTS,)), pltpu.SemaphoreType.DMA((DISPATCH_SLOTS,)),
                        pltpu.SemaphoreType.DMA(())],
        compiler_params=_cparams(("arbitrary",)),
        name="moe_dispatch",
    )(dest3, pend, h2.reshape(n // SUBLANES, SUBLANES, d))


def _pre_odd_kernel(x_ref, dcur_ref, dnxt_ref, cw_ref, yb_hbm, modp_ref, mod_ref, g_ref, w_ref,
                    x2_ref, xc_ref, p_ref, bg_ref, ybuf, sem, *, cw):
    slot = _moe_gather(dcur_ref, dnxt_ref, yb_hbm, ybuf, sem)
    for sl in _row_parts(PROJ_ROW_PARTS):
        x2 = x_ref[sl, :] + modp_ref[0, 5:6, :] * _combined_rows(cw_ref, ybuf, slot, sl)
        x2_ref[sl, :] = x2
        h = _rms_mod(x2, g_ref[...], mod_ref[0, 1:2, :], mod_ref[0, 0:1, :])
        proj = jnp.dot(h.astype(BF16), w_ref[...], preferred_element_type=F32)
        xc_ref[sl, :] = proj[:, :cw]
        hd = proj[:, cw:2 * cw]
        bg_ref[sl, :] = proj[:, 2 * cw:3 * cw]
        p_ref[sl, :] = proj[:, 3 * cw:] * hd


def _post_odd_kernel(x_ref, xc_ref, xcp_ref, xcn_ref, p_ref, pp_ref, pn_ref, bg_ref, mod_ref,
                     pw_ref, ps_ref, dw_ref, wo_ref, g2_ref, wr_ref, rb_ref,
                     x3_ref, h2_ref, info_ref, cnt_ref, extc_sc, extp_sc, *, npt, tps):
    i = pl.program_id(0)
    first, last = _seq_edges(i, npt, tps)
    _fill_ext(extc_sc, xc_ref, xcp_ref, xcn_ref, first, last)
    _fill_ext(extp_sc, p_ref, pp_ref, pn_ref, first, last)
    cw = xc_ref.shape[1]
    gc = cw // len(POOL_WINDOWS)
    is_prompt = i < npt
    t_len = jnp.where(is_prompt, TM, TM * tps)
    tile_pos = jnp.where(is_prompt, 0, ((i - npt) % tps) * TM)
    for rs in _row_parts(MIX_ROW_PARTS):
        nrows = rs.stop - rs.start
        base = HALO + rs.start
        pos = lax.broadcasted_iota(jnp.int32, (nrows, 1), 0) + (tile_pos + rs.start)
        ds = []
        for g, w in enumerate(POOL_WINDOWS):
            sl = slice(g * gc, (g + 1) * gc)
            acc = None
            for o in range(-(w // 2), w // 2):
                term = extc_sc[base + o:base + o + nrows, sl]
                acc = term if acc is None else acc + term
            lo = jnp.maximum(pos - w // 2, 0)
            hi = jnp.minimum(pos + w // 2 - 1, t_len - 1)
            cnt = (hi - lo + 1).astype(F32)
            ds.append(acc / cnt - extc_sc[base:base + nrows, sl])
        d = jnp.concatenate(ds, axis=1).astype(BF16)
        yc = jnp.dot(d, pw_ref[...], preferred_element_type=F32) * ps_ref[...]
        conv = None
        for k in range(SCONV_K):
            start = base - SCONV_K // 2 + k
            term = extp_sc[start:start + nrows, :] * dw_ref[k:k + 1, :]
            conv = term if conv is None else conv + term
        yd = bg_ref[rs, :] * conv
        m = (jnp.dot(yc.astype(BF16), wo_ref[0:cw, :], preferred_element_type=F32)
             + jnp.dot(yd.astype(BF16), wo_ref[cw:, :], preferred_element_type=F32))
        x3 = x_ref[rs, :] + mod_ref[0, 2:3, :] * m
        x3_ref[rs, :] = x3
        _moe_prologue(x3, rs, mod_ref, g2_ref, wr_ref, rb_ref, h2_ref, info_ref, cnt_ref)


def _final_kernel(x_ref, dcur_ref, dnxt_ref, cw_ref, yb_hbm, modp_ref, g_ref, op_ref, os_ref, ybuf, sem, *, npt):
    i = pl.program_id(0)
    slot = _moe_gather(dcur_ref, dnxt_ref, yb_hbm, ybuf, sem)
    x = x_ref[...] + modp_ref[0, 5:6, :] * _combined_rows(cw_ref, ybuf, slot, slice(0, TM))
    ms = jnp.mean(x * x, axis=-1, keepdims=True)
    out = x * lax.rsqrt(ms + EPS) * g_ref[...]

    @pl.when(i < npt)
    def _():
        op_ref[...] = out

    @pl.when(i >= npt)
    def _():
        os_ref[...] = out


def _expert_kernel(be_ref, nv_ref, x_ref, wg_ref, wu_ref, wd_ref, y_ref, wg_sc, wu_sc, wd_sc):
    i = pl.program_id(0)

    @pl.when(jnp.logical_or(i == 0, be_ref[i] != be_ref[jnp.maximum(i - 1, 0)]))
    def _():
        wg_sc[...] = wg_ref[0, 0].astype(BF16)
        wu_sc[...] = wu_ref[0, 0].astype(BF16)
        wd_sc[...] = wd_ref[0, 0].astype(BF16)

    @pl.when(i < nv_ref[0])
    def _():
        x = x_ref[...].astype(BF16)
        hg = jnp.dot(x, wg_sc[...], preferred_element_type=F32)
        hu = jnp.dot(x, wu_sc[...], preferred_element_type=F32)
        hid = (_silu(hg) * hu).astype(BF16)
        y_ref[...] = jnp.dot(hid, wd_sc[...], preferred_element_type=F32)

    @pl.when(i >= nv_ref[0])
    def _():
        y_ref[...] = jnp.zeros(y_ref.shape, F32)


def _experts(blk_e, nvalid, xb, layer, wg, wu, wd):
    p_len, d = xb.shape
    hid = wg.shape[3]
    nblk = p_len // MOE_BLK
    grid_spec = pltpu.PrefetchScalarGridSpec(
        num_scalar_prefetch=2,
        grid=(nblk,),
        in_specs=[pl.BlockSpec((MOE_BLK, d), lambda i, be, nv: (jnp.minimum(i, nv[0] - 1), 0)),
                  pl.BlockSpec((1, 1, d, hid), lambda i, be, nv: (layer, be[i], 0, 0)),
                  pl.BlockSpec((1, 1, d, hid), lambda i, be, nv: (layer, be[i], 0, 0)),
                  pl.BlockSpec((1, 1, hid, d), lambda i, be, nv: (layer, be[i], 0, 0))],
        out_specs=pl.BlockSpec((MOE_BLK, d), lambda i, be, nv: (i, 0)),
        scratch_shapes=[pltpu.VMEM((d, hid), BF16), pltpu.VMEM((d, hid), BF16), pltpu.VMEM((hid, d), BF16)],
    )
    return pl.pallas_call(
        _expert_kernel,
        grid_spec=grid_spec,
        out_shape=jax.ShapeDtypeStruct((p_len, d), F32),
        compiler_params=_cparams(("arbitrary",)),
        name="moe_experts",
    )(blk_e, nvalid, xb, wg, wu, wd)


def _moe(h2, info, cnt, layer, wg, wu, wd):
    n, d = h2.shape
    cnt = cnt[:, :, 0].astype(jnp.int32)
    counts = jnp.sum(cnt, axis=0)
    pc = ((counts + MOE_BLK - 1) // MOE_BLK) * MOE_BLK
    pend = jnp.cumsum(pc)
    base = (pend - pc)[None, :] + jnp.cumsum(cnt, axis=0) - cnt
    eid = info[:, 0:2, :].astype(jnp.int32)
    rank = info[:, 4:6, :].astype(jnp.int32)
    sel = eid[..., None] == jnp.arange(N_EXPERTS, dtype=jnp.int32)
    dest = jnp.sum(jnp.where(sel, base[:, None, None, :], 0), axis=-1) + rank
    p_len = n * TOP_K + N_EXPERTS * MOE_BLK
    nblk = p_len // MOE_BLK
    blk_start = jnp.arange(nblk, dtype=jnp.int32) * MOE_BLK
    blk_e = jnp.minimum(jnp.sum(pend[None, :] <= blk_start[:, None], axis=1), N_EXPERTS - 1).astype(jnp.int32)
    nvalid = (pend[-1:] // MOE_BLK).astype(jnp.int32)
    dest3 = dest.reshape(dest.shape[0], 1, TOP_K * TM)
    wcol = jnp.transpose(info[:, 2:2 + TOP_K, :], (0, 2, 1)).reshape(n, TOP_K)
    xb = _dispatch(dest3, pend.astype(jnp.int32), h2, p_len)
    yb = _experts(blk_e, nvalid, xb, layer, wg, wu, wd)
    return dest3, wcol, yb


def _rope_tables(t_len, qw):
    rows = t_len // GRID_W
    ax = qw // (DIFF_HEADS * 2) // 2
    row = jnp.repeat(jnp.arange(rows, dtype=F32), GRID_W)
    col = jnp.tile(jnp.arange(GRID_W, dtype=F32), rows)
    inv = ROPE_BASE ** (-jnp.arange(0, ax, 2, dtype=F32) / ax)
    ar = row[:, None] * inv[None, :]
    ac = col[:, None] * inv[None, :]
    cr, sr, cc, sc = jnp.cos(ar), jnp.sin(ar), jnp.cos(ac), jnp.sin(ac)
    cos = jnp.concatenate([cr, cr, cc, cc], axis=1)
    sins = jnp.concatenate([-sr, sr, -sc, sc], axis=1)
    reps = qw // cos.shape[1]
    cos = jnp.concatenate([jnp.ones((TM, cos.shape[1]), F32), cos], axis=0)
    sins = jnp.concatenate([jnp.zeros((TM, sins.shape[1]), F32), sins], axis=0)
    return jnp.tile(cos, (1, reps)), jnp.tile(sins, (1, reps))


def kernel(x_prompt, x_sample, cache_k, cache_v, c, c_ctx, w_ada, b_ada, g_norm1, g_norm2, g_final, w_in_e, conv_a_w, conv_a_b, ln_a_g, ln_a_b, lam_q1, lam_k1, lam_q2, lam_k2, subln_g, w_out_e, w_in_o, pool_w, pool_scale, conv_d_w, w_out_o, w_router_g, b_router_g, w_router_e, b_router_e, w_gate, w_up, w_down):
    bp, tp, d = x_prompt.shape
    bs, ts, _ = x_sample.shape
    past = cache_k.shape[2]
    assert tp == TM and ts % TM == 0 and d % LANES == 0
    npt = bp
    tps = ts // TM
    n_p, n_s = bp * tp, bs * ts
    n = n_p + n_s
    nt = n // TM
    cw = conv_a_w.shape[2]
    qw = (w_in_e.shape[2] - 2 * cw) // 3
    dh = qw // (DIFF_HEADS * 2)
    hpt = TM // HALO

    def cond_of(i):
        return jnp.where(i < npt, 0, 1 + (i - npt) // tps)

    def tile(w):
        return pl.BlockSpec((TM, w), lambda i: (i, 0))

    def halo_prev(w):
        return pl.BlockSpec((HALO, w), lambda i: (jnp.maximum(i * hpt - 1, 0), 0))

    def halo_next(w):
        return pl.BlockSpec((HALO, w), lambda i: (jnp.minimum((i + 1) * hpt, nt * hpt - 1), 0))

    def whole(shape):
        return pl.BlockSpec(shape, lambda i: (0,) * len(shape))

    mod_spec = pl.BlockSpec((1, 6, d), lambda i: (cond_of(i), 0, 0))
    combine_specs = [pl.BlockSpec((1, 1, TOP_K * TM), lambda i: (i, 0, 0), memory_space=pltpu.SMEM),
                     pl.BlockSpec((1, 1, TOP_K * TM), lambda i: (jnp.minimum(i + 1, nt - 1), 0, 0),
                                  memory_space=pltpu.SMEM),
                     tile(TOP_K), pl.BlockSpec(memory_space=pl.ANY)]
    combine_scratch = [pltpu.VMEM((2, TOP_K, TM // SUBLANES, SUBLANES, d), F32), pltpu.SemaphoreType.DMA((2,))]

    rows = 8 * ((1 + bs + 7) // 8)
    cond = jnp.concatenate([c_ctx[None, :], c, jnp.zeros((rows - 1 - bs, d), F32)], axis=0)
    mod = _ada_table(cond, w_ada, b_ada)

    xp2, xs2 = x_prompt.reshape(n_p, d), x_sample.reshape(n_s, d)

    def prompt_tile(w):
        return pl.BlockSpec((TM, w), lambda i: (jnp.minimum(i, npt - 1), 0))

    def sample_tile(w):
        return pl.BlockSpec((TM, w), lambda i: (jnp.maximum(i - npt, 0), 0))

    def router_w(i):
        wre = jnp.transpose(w_router_e[i], (1, 0, 2)).reshape(d, N_EXPERTS)
        wr = jnp.concatenate([wre, w_router_g[i]], axis=1)
        return jnp.pad(wr, ((0, 0), (0, ROUTER_PAD - wr.shape[1]))).astype(BF16)

    def router_b(i):
        b = jnp.concatenate([b_router_e[i].reshape(N_EXPERTS), b_router_g[i]])
        return jnp.pad(b, (0, ROUTER_PAD - b.shape[0]))[:, None]

    cos, sins = _rope_tables(ts, qw)
    tab_spec = pl.BlockSpec((TM, qw), lambda i: (jnp.where(i < npt, 0, 1 + (i - npt) % tps), 0))
    kc_spec = pl.BlockSpec((1, 1, TM, DIFF_HEADS, 2, dh), lambda i: (jnp.minimum(i, npt - 1), 0, 0, 0, 0, 0))
    vc_spec = pl.BlockSpec((1, 1, TM, DIFF_HEADS, 2 * dh), lambda i: (jnp.minimum(i, npt - 1), 0, 0, 0, 0))
    u, q, k, vt, new_k, new_v = pl.pallas_call(
        functools.partial(_pre_even_kernel, npt=npt, cw=cw, qw=qw, qscale=dh ** -0.5 * math.log2(math.e)),
        grid=(nt,),
        in_specs=[prompt_tile(d), sample_tile(d), mod_spec, whole((1, d)), whole(w_in_e.shape[1:]),
                  tab_spec, tab_spec],
        out_specs=[tile(cw), tile(qw), tile(qw), pl.BlockSpec((qw, TM), lambda i: (0, i)),
                   kc_spec, vc_spec],
        out_shape=[jax.ShapeDtypeStruct((n, cw), F32), jax.ShapeDtypeStruct((n, qw), BF16),
                   jax.ShapeDtypeStruct((n, qw), BF16), jax.ShapeDtypeStruct((qw, n), BF16),
                   jax.ShapeDtypeStruct((bp, 1, tp, DIFF_HEADS, 2, dh), F32),
                   jax.ShapeDtypeStruct((bp, 1, tp, DIFF_HEADS, 2 * dh), F32)],
        scratch_shapes=[pltpu.VMEM((2, TM, qw), F32)],
        compiler_params=_cparams(("arbitrary",)),
        name="pre_even",
    )(xp2, xs2, mod[0], g_norm1[0][None, :], w_in_e[0].astype(BF16), cos, sins)

    lam_init = 0.8 - 0.6 * math.exp(-0.3 * 0)
    lam = (jnp.exp(jnp.sum(lam_q1[0] * lam_k1[0])) - jnp.exp(jnp.sum(lam_q2[0] * lam_k2[0])) + lam_init)
    lam = lam.reshape(1).astype(F32)
    sub_g = subln_g[0][:, None]
    o_p = _attention(lam, q, k, vt, sub_g, None, batch=bp, t_len=tp, row0=0, tq=TM, tk=TM,
                     post_scale=1.0 - lam_init)
    k_ctx = cache_k[:, 0].reshape(bs * past, qw).astype(BF16)
    vt_ctx = jnp.transpose(cache_v[:, 0].reshape(bs, past, qw), (0, 2, 1)).astype(BF16)
    o_s = _attention(lam, q, k, vt, sub_g, (k_ctx, vt_ctx), batch=bs, t_len=ts, row0=n_p, tq=ATTN_TQ, tk=ATTN_TK,
                     post_scale=1.0 - lam_init)

    moe_outs = [jax.ShapeDtypeStruct((n, d), F32), jax.ShapeDtypeStruct((n, d), F32),
                jax.ShapeDtypeStruct((nt, INFO_ROWS, TM), F32), jax.ShapeDtypeStruct((nt, N_EXPERTS, 1), F32)]
    moe_out_specs = [tile(d), tile(d), pl.BlockSpec((1, INFO_ROWS, TM), lambda i: (i, 0, 0)),
                     pl.BlockSpec((1, N_EXPERTS, 1), lambda i: (i, 0, 0))]
    x1, h2, info, cnt = pl.pallas_call(
        functools.partial(_post_even_kernel, npt=npt, tps=tps),
        grid=(nt,),
        in_specs=[prompt_tile(d), sample_tile(d), tile(cw), halo_prev(cw), halo_next(cw),
                  prompt_tile(qw), sample_tile(qw), mod_spec,
                  whole((CONV_K, cw)), whole((1, cw)), whole((1, cw)), whole((1, cw)),
                  whole(w_out_e.shape[1:]), whole((1, d)), whole((d, ROUTER_PAD)), whole((ROUTER_PAD, 1))],
        out_specs=moe_out_specs,
        out_shape=moe_outs,
        scratch_shapes=[pltpu.VMEM((TM + 2 * HALO, cw), F32),
                        pltpu.VMEM((SUBLANES, TM + 2 * HALO - SUBLANES, cw), F32)],
        compiler_params=_cparams(("parallel",)),
        name="post_even",
    )(xp2, xs2, u, u, u, o_p, o_s, mod[0], conv_a_w[0], conv_a_b[0][None, :], ln_a_g[0][None, :], ln_a_b[0][None, :],
      w_out_e[0].astype(BF16), g_norm2[0][None, :], router_w(0), router_b(0))
    dest3, wcol, yb = _moe(h2, info, cnt, 0, w_gate, w_up, w_down)

    pw = pool_w.shape[2]
    x2, xc, p, bg = pl.pallas_call(
        functools.partial(_pre_odd_kernel, cw=cw),
        grid=(nt,),
        in_specs=[tile(d)] + combine_specs + [mod_spec, mod_spec, whole((1, d)), whole(w_in_o.shape[1:])],
        out_specs=[tile(d), tile(cw), tile(cw), tile(cw)],
        out_shape=[jax.ShapeDtypeStruct((n, d), F32)] + [jax.ShapeDtypeStruct((n, cw), F32)] * 3,
        scratch_shapes=combine_scratch,
        compiler_params=_cparams(("arbitrary",)),
        name="pre_odd",
    )(x1, dest3, dest3, wcol, yb, mod[0], mod[1], g_norm1[1][None, :], w_in_o[0].astype(BF16))
    pool_bd = jnp.zeros((cw, cw), F32)
    for g in range(len(POOL_WINDOWS)):
        pool_bd = pool_bd.at[g * pw:(g + 1) * pw, g * pw:(g + 1) * pw].set(pool_w[0, g])
    x3, h2, info, cnt = pl.pallas_call(
        functools.partial(_post_odd_kernel, npt=npt, tps=tps),
        grid=(nt,),
        in_specs=[tile(d), tile(cw), halo_prev(cw), halo_next(cw), tile(cw), halo_prev(cw), halo_next(cw),
                  tile(cw), mod_spec, whole((cw, cw)), whole((1, cw)), whole((SCONV_K, cw)),
                  whole(w_out_o.shape[1:]), whole((1, d)), whole((d, ROUTER_PAD)), whole((ROUTER_PAD, 1))],
        out_specs=moe_out_specs,
        out_shape=moe_outs,
        scratch_shapes=[pltpu.VMEM((TM + 2 * HALO, cw), F32), pltpu.VMEM((TM + 2 * HALO, cw), F32)],
        compiler_params=_cparams(("parallel",)),
        name="post_odd",
    )(x2, xc, xc, xc, p, p, p, bg, mod[1], pool_bd.astype(BF16), pool_scale[0][None, :], conv_d_w[0],
      w_out_o[0].astype(BF16), g_norm2[1][None, :], router_w(1), router_b(1))
    dest3, wcol, yb = _moe(h2, info, cnt, 1, w_gate, w_up, w_down)

    out_p, out_s = pl.pallas_call(
        functools.partial(_final_kernel, npt=npt),
        grid=(nt,),
        in_specs=[tile(d)] + combine_specs + [mod_spec, whole((1, d))],
        out_specs=[pl.BlockSpec((TM, d), lambda i: (jnp.minimum(i, npt - 1), 0)),
                   pl.BlockSpec((TM, d), lambda i: (jnp.maximum(i - npt, 0), 0))],
        out_shape=[jax.ShapeDtypeStruct((n_p, d), F32), jax.ShapeDtypeStruct((n_s, d), F32)],
        scratch_shapes=combine_scratch,
        compiler_params=_cparams(("arbitrary",)),
        name="final_norm",
    )(x3, dest3, dest3, wcol, yb, mod[1], g_final[None, :])

    return (out_p.reshape(bp, tp, d), out_s.reshape(bs, ts, d), new_k, new_v)
```

```python
import functools
import math

import jax
import jax.numpy as jnp
from jax import lax
from jax.experimental import pallas as pl
from jax.experimental.pallas import tpu as pltpu

F32 = jnp.float32
BF16 = jnp.bfloat16

EPS = 1e-6
GRID_W = 64
ROPE_BASE = 10000.0
CONV_K = 31
SCONV_K = 3
POOL_WINDOWS = (2, 4, 8, 16)
DIFF_HEADS = 4
N_GROUPS = 4
EXPERTS_PER_GROUP = 8
N_EXPERTS = N_GROUPS * EXPERTS_PER_GROUP
TOP_K = 2

LANES = 128
SUBLANES = 8
TM = 256
HALO = 16
PROJ_ROW_PARTS = 2
MIX_ROW_PARTS = 1
MOE_BLK = 512
EXPERT_ROW_PARTS = 2
ATTN_TQ = 2048
ATTN_TK = 512
CONV_ROWS = 32
INFO_ROWS = 8
ROUTER_PAD = 128
ONES_ROWS = 16
DISPATCH_SLOTS = 3
VMEM_LIMIT = 56 * 1024 * 1024


def _cparams(sem):
    return pltpu.CompilerParams(dimension_semantics=sem, vmem_limit_bytes=VMEM_LIMIT)


def _rms_mod(x, g, sc, sh):
    ms = jnp.mean(x * x, axis=-1, keepdims=True)
    return (x * lax.rsqrt(ms + EPS)) * g * (1.0 + sc) + sh


def _silu(x):
    return x * jax.nn.sigmoid(x)


def _ada_kernel(c_ref, w_ref, b_ref, o_ref):
    cs = _silu(c_ref[...])
    o_ref[0] = jnp.dot(cs, w_ref[0], precision=lax.Precision.HIGHEST,
                       preferred_element_type=F32) + b_ref[0]


def _ada_table(cond, w_ada, b_ada):
    depth, d, six_d = w_ada.shape
    rows = cond.shape[0]
    nj = six_d // d
    out = pl.pallas_call(
        _ada_kernel,
        grid=(depth, nj),
        in_specs=[pl.BlockSpec((rows, d), lambda l, j: (0, 0)),
                  pl.BlockSpec((1, d, d), lambda l, j: (l, 0, j)),
                  pl.BlockSpec((1, 1, d), lambda l, j: (l, 0, j))],
        out_specs=pl.BlockSpec((1, rows, d), lambda l, j: (l, 0, j)),
        out_shape=jax.ShapeDtypeStruct((depth, rows, six_d), F32),
        compiler_params=_cparams(("parallel", "parallel")),
        name="ada_table",
    )(cond, w_ada, b_ada.reshape(depth, 1, six_d))
    return out.reshape(depth, rows, nj, d)


def _row_parts(parts):
    rows = TM // parts
    return [slice(p * rows, (p + 1) * rows) for p in range(parts)]


def _rope(x, cos, sins, half):
    lane = lax.broadcasted_iota(jnp.int32, (1, LANES), 1)
    first_half = (lane % (2 * half)) < half
    outs = []
    for j in range(x.shape[1] // LANES):
        sl = slice(j * LANES, (j + 1) * LANES)
        xs = x[:, sl]
        nxt = pltpu.roll(xs, LANES - half, 1)
        prv = pltpu.roll(xs, half, 1)
        rot = jnp.where(first_half, nxt, prv)
        outs.append(xs * cos[:, sl] + rot * sins[:, sl])
    return jnp.concatenate(outs, axis=1)


def _pre_even_kernel(xp_ref, xs_ref, mod_ref, g_ref, w_ref, cos_ref, sin_ref,
                     u_ref, q_ref, k_ref, vt_ref, kc_ref, vc_ref, kv_sc, *, npt, cw, qw, qscale):
    i = pl.program_id(0)
    is_prompt = i < npt
    half = kc_ref.shape[-1] // 4
    for sl in _row_parts(PROJ_ROW_PARTS):
        x = jnp.where(is_prompt, xp_ref[sl, :], xs_ref[sl, :])
        h = _rms_mod(x, g_ref[...], mod_ref[0, 1:2, :], mod_ref[0, 0:1, :])
        proj = jnp.dot(h.astype(BF16), w_ref[...], preferred_element_type=F32)
        ga = proj[:, :cw]
        gb = proj[:, cw:2 * cw]
        u_ref[sl, :] = ga * jax.nn.sigmoid(gb)
        q = proj[:, 2 * cw:2 * cw + qw]
        k = proj[:, 2 * cw + qw:2 * cw + 2 * qw]
        v = proj[:, 2 * cw + 2 * qw:]
        kv_sc[0, sl, :] = k
        kv_sc[1, sl, :] = v
        cos = cos_ref[sl, :]
        sins = sin_ref[sl, :]
        q_ref[sl, :] = (_rope(q, cos, sins, half) * qscale).astype(BF16)
        k_ref[sl, :] = _rope(k, cos, sins, half).astype(BF16)
        vt_ref[:, sl] = v.T.astype(BF16)

    @pl.when(is_prompt)
    def _():
        heads, halves, dh = kc_ref.shape[3:]
        for h in range(heads):
            vc_ref[0, 0, :, h, :] = kv_sc[1, :, h * halves * dh:(h + 1) * halves * dh]
            for c in range(halves):
                lo = (h * halves + c) * dh
                kc_ref[0, 0, :, h, c, :] = kv_sc[0, :, lo:lo + dh]


def _attn_kernel(lam_ref, q_ref, k_ref, vt_ref, *rest, nk, tk, nk_ctx, post_scale):
    if nk_ctx:
        kc_ref, vtc_ref, g_ref, o_ref, s_sc, mc_sc, m_sc, acc_sc = rest
    else:
        g_ref, o_ref, s_sc, mc_sc, m_sc, acc_sc = rest
    q = q_ref[...]
    hw = q.shape[1]
    dh = hw // 2
    lane = lax.broadcasted_iota(jnp.int32, (1, hw), 1)
    zero = jnp.zeros_like(q)
    qs = (jnp.where(lane < dh, q, zero), jnp.where(lane >= dh, q, zero))
    m_sc[...] = jnp.full(m_sc.shape, -jnp.inf, F32)
    acc_sc[...] = jnp.zeros(acc_sc.shape, F32)
    ones_rows = (lax.broadcasted_iota(jnp.int32, (ONES_ROWS, tk), 0) == 0).astype(BF16)

    def scores(kk, slot):
        for c in range(2):
            s = lax.dot_general(kk, qs[c], (((1,), (1,)), ((), ())), preferred_element_type=F32)
            s_sc[slot, c] = s
            mc_sc[slot, c] = jnp.max(s, axis=0, keepdims=True)

    def consume(vt, slot):
        vta = jnp.concatenate([vt, ones_rows], axis=0)
        for c in range(2):
            m_prev = m_sc[c]
            m_new = jnp.maximum(m_prev, mc_sc[slot, c])
            alpha = jnp.exp2(m_prev - m_new)
            p = jnp.exp2(s_sc[slot, c] - m_new).astype(BF16)
            acc_sc[c] = alpha * acc_sc[c] + jnp.dot(vta, p, preferred_element_type=F32)
            m_sc[c] = m_new

    def k_at(i):
        if i < nk_ctx:
            return kc_ref[i * tk:(i + 1) * tk, :]
        return k_ref[(i - nk_ctx) * tk:(i - nk_ctx + 1) * tk, :]

    def vt_at(i):
        if i < nk_ctx:
            return vtc_ref[0, :, i * tk:(i + 1) * tk]
        return vt_ref[:, (i - nk_ctx) * tk:(i - nk_ctx + 1) * tk]

    n_total = nk_ctx + nk
    peel = nk_ctx + (n_total - 1 - nk_ctx) % 2
    scores(k_at(0), 0)
    for i in range(peel):
        scores(k_at(i + 1), (i + 1) % 2)
        consume(vt_at(i), i % 2)

    def body(t, carry):
        for r in range(2):
            j = 2 * t + (peel + r - nk_ctx)
            nxt = pl.multiple_of((j + 1) * tk, tk)
            cur = pl.multiple_of(j * tk, tk)
            scores(k_ref[pl.ds(nxt, tk), :], (peel + r + 1) % 2)
            consume(vt_ref[:, pl.ds(cur, tk)], (peel + r) % 2)
        return carry

    lax.fori_loop(0, (n_total - 1 - peel) // 2, body, 0)
    consume(vt_at(n_total - 1), (n_total - 1) % 2)
    o = (acc_sc[0, :hw, :] / acc_sc[0, hw:hw + 1, :]
         - lam_ref[0] * (acc_sc[1, :hw, :] / acc_sc[1, hw:hw + 1, :]))
    ms = jnp.mean(o * o, axis=0, keepdims=True)
    o = o * lax.rsqrt(ms + EPS) * g_ref[...] * post_scale
    o_ref[...] = o.T.astype(BF16)


def _attn_heads_kernel(lam_ref, q_ref, k_ref, vt_ref, g_ref, o_ref, *scratch, heads, **kw):
    hw = q_ref.shape[1] // heads
    for h in range(heads):
        cols = pl.ds(h * hw, hw)
        _attn_kernel(lam_ref, q_ref.at[:, cols], k_ref.at[:, cols], vt_ref.at[cols, :], g_ref, o_ref.at[:, cols],
                     *scratch, **kw)


def _attention(lam, q, k, vt, g_col, ctx, *, batch, t_len, row0, tq, tk, post_scale, heads_per_step=1):
    hw = q.shape[1] // DIFF_HEADS
    bw = hw * heads_per_step
    nq = t_len // tq
    assert row0 % t_len == 0 and t_len % tk == 0 and (ctx is None or heads_per_step == 1)
    qb0 = row0 // tq
    sb0 = row0 // t_len
    in_specs = [pl.BlockSpec(memory_space=pltpu.SMEM),
                pl.BlockSpec((tq, bw), lambda b, h, i: (qb0 + b * nq + i, h)),
                pl.BlockSpec((t_len, bw), lambda b, h, i: (sb0 + b, h)),
                pl.BlockSpec((bw, t_len), lambda b, h, i: (h, sb0 + b))]
    args = [lam, q, k, vt]
    nk_ctx = 0
    if ctx is not None:
        k_ctx, vt_ctx = ctx
        past = vt_ctx.shape[2]
        assert past % tk == 0
        nk_ctx = past // tk
        in_specs += [pl.BlockSpec((past, hw), lambda b, h, i: (b, h)),
                     pl.BlockSpec((1, hw, past), lambda b, h, i: (b, h, 0))]
        args += [k_ctx, vt_ctx]
    in_specs.append(pl.BlockSpec((hw, 1), lambda b, h, i: (0, 0)))
    args.append(g_col)
    body = functools.partial(_attn_kernel, nk=t_len // tk, tk=tk, nk_ctx=nk_ctx, post_scale=post_scale)
    if heads_per_step > 1:
        body = functools.partial(_attn_heads_kernel, heads=heads_per_step, nk=t_len // tk, tk=tk, nk_ctx=nk_ctx,
                                 post_scale=post_scale)
    return pl.pallas_call(
        body,
        grid=(batch, DIFF_HEADS // heads_per_step, nq),
        in_specs=in_specs,
        out_specs=pl.BlockSpec((tq, bw), lambda b, h, i: (b * nq + i, h)),
        out_shape=jax.ShapeDtypeStruct((batch * t_len, q.shape[1]), BF16),
        scratch_shapes=[pltpu.VMEM((2, 2, tk, tq), F32), pltpu.VMEM((2, 2, 1, tq), F32),
                        pltpu.VMEM((2, 1, tq), F32), pltpu.VMEM((2, hw + ONES_ROWS, tq), F32)],
        compiler_params=_cparams(("parallel", "parallel", "parallel")),
        name="diff_attention",
    )(*args)


def _seq_edges(i, npt, tps):
    r = (i - npt) % tps
    is_prompt = i < npt
    first = jnp.logical_or(is_prompt, r == 0)
    last = jnp.logical_or(is_prompt, r == tps - 1)
    return first, last


def _fill_ext(ext_sc, cur_ref, prev_ref, next_ref, first, last):
    prev = prev_ref[...]
    nxt = next_ref[...]
    ext_sc[0:HALO, :] = jnp.where(first, jnp.zeros_like(prev), prev)
    ext_sc[HALO:HALO + TM, :] = cur_ref[...]
    ext_sc[HALO + TM:, :] = jnp.where(last, jnp.zeros_like(nxt), nxt)


def _route_tile(logits, b_ref, info_ref, cnt_ref):
    lt = logits.T + b_ref[...]
    gl = lt[N_EXPERTS:N_EXPERTS + N_GROUPS]
    ge = jnp.exp(gl - jnp.max(gl, axis=0, keepdims=True))
    gp = ge / jnp.sum(ge, axis=0, keepdims=True)
    best = gp[0:1]
    gidx = jnp.zeros(best.shape, jnp.int32)
    el = lt[0:EXPERTS_PER_GROUP]
    for g in range(1, N_GROUPS):
        better = gp[g:g + 1] > best
        gidx = jnp.where(better, g, gidx)
        best = jnp.where(better, gp[g:g + 1], best)
    for g in range(1, N_GROUPS):
        el = jnp.where(gidx == g, lt[g * EXPERTS_PER_GROUP:(g + 1) * EXPERTS_PER_GROUP], el)
    rio = lax.broadcasted_iota(jnp.int32, el.shape, 0)
    v0 = jnp.max(el, axis=0, keepdims=True)
    i0 = jnp.min(jnp.where(el == v0, rio, EXPERTS_PER_GROUP), axis=0, keepdims=True)
    el2 = jnp.where(rio == i0, -jnp.inf, el)
    v1 = jnp.max(el2, axis=0, keepdims=True)
    i1 = jnp.min(jnp.where(el2 == v1, rio, EXPERTS_PER_GROUP), axis=0, keepdims=True)
    e1 = jnp.exp(v1 - v0)
    den = 1.0 + e1
    w0 = (1.0 / den) * best
    w1 = (e1 / den) * best
    eid0 = gidx * EXPERTS_PER_GROUP + i0
    eid1 = gidx * EXPERTS_PER_GROUP + i1
    tm = lt.shape[1]
    eio = lax.broadcasted_iota(jnp.int32, (N_EXPERTS, tm), 0)
    oh0 = eio == eid0
    oh1 = eio == eid1
    ohs = jnp.where(oh0, 1.0, 0.0) + jnp.where(oh1, 1.0, 0.0)
    upper = (lax.broadcasted_iota(jnp.int32, (tm, tm), 0)
             < lax.broadcasted_iota(jnp.int32, (tm, tm), 1)).astype(BF16)
    before = jnp.dot(ohs.astype(BF16), upper, preferred_element_type=F32)
    rank0 = jnp.sum(jnp.where(oh0, before, 0.0), axis=0, keepdims=True)
    rank1 = jnp.sum(jnp.where(oh1, before, 0.0), axis=0, keepdims=True)
    info_ref[0] = jnp.concatenate([eid0.astype(F32), eid1.astype(F32), w0, w1, rank0, rank1,
                                   jnp.zeros((INFO_ROWS - 6, tm), F32)], axis=0)
    cnt_ref[0] = jnp.sum(ohs, axis=1, keepdims=True)


def _moe_prologue(x_new, sl, mod_ref, g2_ref, wr_ref, rb_ref, h2_ref, info_ref, cnt_ref):
    h2 = _rms_mod(x_new, g2_ref[...], mod_ref[0, 4:5, :], mod_ref[0, 3:4, :])
    h2_ref[sl, :] = h2
    assert sl.stop - sl.start == TM
    _route_tile(jnp.dot(h2.astype(BF16), wr_ref[...], preferred_element_type=F32), rb_ref, info_ref, cnt_ref)


def _post_even_kernel(xp_ref, xs_ref, u_ref, up_ref, un_ref, op_ref, os_ref, mod_ref, cw_ref, cb_ref,
                      lng_ref, lnb_ref, wo_ref, g2_ref, wr_ref, rb_ref, x1_ref, h2_ref, info_ref, cnt_ref,
                      ext_sc, sh_sc, *, npt, tps):
    i = pl.program_id(0)
    first, last = _seq_edges(i, npt, tps)
    _fill_ext(ext_sc, u_ref, up_ref, un_ref, first, last)
    cw = u_ref.shape[1]
    span = sh_sc.shape[1]
    for r in range(SUBLANES):
        sh_sc[r] = ext_sc[r:r + span, :]
    rows = CONV_ROWS
    is_prompt = i < npt
    for sl in _row_parts(MIX_ROW_PARTS):
        chunks = []
        for r0 in range(sl.start, sl.stop, rows):
            acc = None
            for k in range(CONV_K):
                off = HALO - CONV_K // 2 + k
                start = r0 + off - off % SUBLANES
                term = sh_sc[off % SUBLANES, start:start + rows, :] * cw_ref[k:k + 1, :]
                acc = term if acc is None else acc + term
            chunks.append(acc)
        conv = jnp.concatenate(chunks, axis=0) + cb_ref[...]
        mu = jnp.mean(conv, axis=-1, keepdims=True)
        cen = conv - mu
        var = jnp.mean(cen * cen, axis=-1, keepdims=True)
        uu = _silu(cen * lax.rsqrt(var + EPS) * lng_ref[...] + lnb_ref[...])
        o = jnp.where(is_prompt, op_ref[sl, :], os_ref[sl, :])
        m = (jnp.dot(uu.astype(BF16), wo_ref[0:cw, :], preferred_element_type=F32)
             + jnp.dot(o, wo_ref[cw:, :], preferred_element_type=F32))
        x1 = jnp.where(is_prompt, xp_ref[sl, :], xs_ref[sl, :]) + mod_ref[0, 2:3, :] * m
        x1_ref[sl, :] = x1
        _moe_prologue(x1, sl, mod_ref, g2_ref, wr_ref, rb_ref, h2_ref, info_ref, cnt_ref)


def _gather_start(dest_ref, yb_hbm, ybuf, sem, slot):
    def body(i, carry):
        for j in range(SUBLANES):
            for k in range(TOP_K):
                row = dest_ref[0, 0, k * TM + i * SUBLANES + j]
                pltpu.make_async_copy(yb_hbm.at[pl.ds(row, 1)], ybuf.at[slot, k, i, pl.ds(j, 1)],
                                      sem.at[slot]).start(priority=k % 2)
        return carry

    lax.fori_loop(0, TM // SUBLANES, body, 0)


def _moe_gather(dcur_ref, dnxt_ref, yb_hbm, ybuf, sem):
    t = pl.program_id(0)
    slot = t % 2

    @pl.when(t == 0)
    def _():
        _gather_start(dcur_ref, yb_hbm, ybuf, sem, 0)

    @pl.when(t + 1 < pl.num_programs(0))
    def _():
        _gather_start(dnxt_ref, yb_hbm, ybuf, sem, 1 - slot)

    for k in range(TOP_K):
        pltpu.make_async_copy(ybuf.at[slot, k], ybuf.at[slot, k], sem.at[slot]).wait()
    return slot


def _combined_rows(cw_ref, ybuf, slot, sl):
    d = ybuf.shape[-1]
    g = slice(sl.start // SUBLANES, sl.stop // SUBLANES)
    w = cw_ref[sl, :]
    y = w[:, 0:1] * ybuf[slot, 0, g].reshape(sl.stop - sl.start, d)
    for k in range(1, TOP_K):
        y = y + w[:, k:k + 1] * ybuf[slot, k, g].reshape(sl.stop - sl.start, d)
    return y


def _dispatch_kernel(dest_ref, pend_ref, h2_hbm, xb_hbm, stage, zeros, in_sem, row_sem, zero_sem):
    t = pl.program_id(0)
    nt = pl.num_programs(0)
    slot = t % DISPATCH_SLOTS

    tile_rows = TM // SUBLANES

    def stage_in(tile_idx, s):
        return pltpu.make_async_copy(h2_hbm.at[pl.ds(tile_idx * tile_rows, tile_rows)], stage.at[s], in_sem.at[s])

    def drain_rows(s):
        for _ in range(TOP_K):
            pltpu.make_async_copy(stage.at[s], stage.at[s], row_sem.at[s]).wait()

    def zero_block(start):
        return pltpu.make_async_copy(zeros, xb_hbm.at[pl.ds(pl.multiple_of(start, MOE_BLK), MOE_BLK)], zero_sem)

    def zero_fill(op):
        for e in range(N_EXPERTS):
            op(zero_block(jnp.maximum(pend_ref[e] - MOE_BLK, 0)))
        for j in range(N_EXPERTS):
            start = pend_ref[N_EXPERTS - 1] + j * MOE_BLK

            @pl.when(start < xb_hbm.shape[0])
            def _():
                op(zero_block(start))

    @pl.when(t == 0)
    def _():
        stage_in(0, 0).start()
        zeros[...] = jnp.zeros(zeros.shape, zeros.dtype)
        zero_fill(lambda cp: cp.start())
        zero_fill(lambda cp: cp.wait())

    @pl.when(jnp.logical_and(t == 0, nt > 1))
    def _():
        stage_in(1, 1).start()

    stage_in(t, slot).wait()

    def body(i, carry):
        for j in range(SUBLANES):
            for k in range(TOP_K):
                row = dest_ref[0, 0, k * TM + i * SUBLANES + j]
                pltpu.make_async_copy(stage.at[slot, i, pl.ds(j, 1)], xb_hbm.at[pl.ds(row, 1)],
                                      row_sem.at[slot]).start(priority=k % 2)
        return carry

    lax.fori_loop(0, tile_rows, body, 0)

    @pl.when(t > 0)
    def _():
        drain_rows((t + DISPATCH_SLOTS - 1) % DISPATCH_SLOTS)

    @pl.when(t + 2 < nt)
    def _():
        stage_in(t + 2, (t + 2) % DISPATCH_SLOTS).start()

    @pl.when(t == nt - 1)
    def _():
        drain_rows(slot)


def _dispatch(dest3, pend, h2, p_len):
    n, d = h2.shape
    return pl.pallas_call(
        _dispatch_kernel,
        grid=(n // TM,),
        in_specs=[pl.BlockSpec((1, 1, TOP_K * TM), lambda t: (t, 0, 0), memory_space=pltpu.SMEM),
                  pl.BlockSpec(memory_space=pltpu.SMEM), pl.BlockSpec(memory_space=pl.ANY)],
        out_specs=pl.BlockSpec(memory_space=pl.ANY),
        out_shape=jax.ShapeDtypeStruct((p_len, d), h2.dtype),
        scratch_shapes=[pltpu.VMEM((DISPATCH_SLOTS, TM // SUBLANES, SUBLANES, d), h2.dtype),
                        pltpu.VMEM((MOE_BLK, d), h2.dtype),
                        pltpu.SemaphoreType.DMA((DISPATCH_SLOTS,)), pltpu.SemaphoreType.DMA((DISPATCH_SLOTS,)),
                        pltpu.SemaphoreType.DMA(())],
        compiler_params=_cparams(("arbitrary",)),
        name="moe_dispatch",
    )(dest3, pend, h2.reshape(n // SUBLANES, SUBLANES, d))


def _pre_odd_kernel(x_ref, dcur_ref, dnxt_ref, cw_ref, yb_hbm, modp_ref, mod_ref, g_ref, w_ref,
                    x2_ref, xc_ref, p_ref, bg_ref, ybuf, sem, *, cw):
    slot = _moe_gather(dcur_ref, dnxt_ref, yb_hbm, ybuf, sem)
    for sl in _row_parts(PROJ_ROW_PARTS):
        x2 = x_ref[sl, :] + modp_ref[0, 5:6, :] * _combined_rows(cw_ref, ybuf, slot, sl)
        x2_ref[sl, :] = x2
        h = _rms_mod(x2, g_ref[...], mod_ref[0, 1:2, :], mod_ref[0, 0:1, :])
        proj = jnp.dot(h.astype(BF16), w_ref[...], preferred_element_type=F32)
        xc_ref[sl, :] = proj[:, :cw]
        hd = proj[:, cw:2 * cw]
        bg_ref[sl, :] = proj[:, 2 * cw:3 * cw]
        p_ref[sl, :] = proj[:, 3 * cw:] * hd


def _post_odd_kernel(x_ref, xc_ref, xcp_ref, xcn_ref, p_ref, pp_ref, pn_ref, bg_ref, mod_ref,
                     pw_ref, ps_ref, dw_ref, wo_ref, g2_ref, wr_ref, rb_ref,
                     x3_ref, h2_ref, info_ref, cnt_ref, extc_sc, extp_sc, *, npt, tps):
    i = pl.program_id(0)
    first, last = _seq_edges(i, npt, tps)
    _fill_ext(extc_sc, xc_ref, xcp_ref, xcn_ref, first, last)
    _fill_ext(extp_sc, p_ref, pp_ref, pn_ref, first, last)
    cw = xc_ref.shape[1]
    gc = cw // len(POOL_WINDOWS)
    is_prompt = i < npt
    t_len = jnp.where(is_prompt, TM, TM * tps)
    tile_pos = jnp.where(is_prompt, 0, ((i - npt) % tps) * TM)
    for rs in _row_parts(MIX_ROW_PARTS):
        nrows = rs.stop - rs.start
        base = HALO + rs.start
        pos = lax.broadcasted_iota(jnp.int32, (nrows, 1), 0) + (tile_pos + rs.start)
        ds = []
        for g, w in enumerate(POOL_WINDOWS):
            sl = slice(g * gc, (g + 1) * gc)
            acc = None
            for o in range(-(w // 2), w // 2):
                term = extc_sc[base + o:base + o + nrows, sl]
                acc = term if acc is None else acc + term
            lo = jnp.maximum(pos - w // 2, 0)
            hi = jnp.minimum(pos + w // 2 - 1, t_len - 1)
            cnt = (hi - lo + 1).astype(F32)
            ds.append(acc / cnt - extc_sc[base:base + nrows, sl])
        d = jnp.concatenate(ds, axis=1).astype(BF16)
        yc = jnp.dot(d, pw_ref[...], preferred_element_type=F32) * ps_ref[...]
        conv = None
        for k in range(SCONV_K):
            start = base - SCONV_K // 2 + k
            term = extp_sc[start:start + nrows, :] * dw_ref[k:k + 1, :]
            conv = term if conv is None else conv + term
        yd = bg_ref[rs, :] * conv
        m = (jnp.dot(yc.astype(BF16), wo_ref[0:cw, :], preferred_element_type=F32)
             + jnp.dot(yd.astype(BF16), wo_ref[cw:, :], preferred_element_type=F32))
        x3 = x_ref[rs, :] + mod_ref[0, 2:3, :] * m
        x3_ref[rs, :] = x3
        _moe_prologue(x3, rs, mod_ref, g2_ref, wr_ref, rb_ref, h2_ref, info_ref, cnt_ref)


def _final_kernel(x_ref, dcur_ref, dnxt_ref, cw_ref, yb_hbm, modp_ref, g_ref, op_ref, os_ref, ybuf, sem, *, npt):
    i = pl.program_id(0)
    slot = _moe_gather(dcur_ref, dnxt_ref, yb_hbm, ybuf, sem)
    x = x_ref[...] + modp_ref[0, 5:6, :] * _combined_rows(cw_ref, ybuf, slot, slice(0, TM))
    ms = jnp.mean(x * x, axis=-1, keepdims=True)
    out = x * lax.rsqrt(ms + EPS) * g_ref[...]

    @pl.when(i < npt)
    def _():
        op_ref[...] = out

    @pl.when(i >= npt)
    def _():
        os_ref[...] = out


def _expert_kernel(be_ref, nv_ref, x_ref, wg_ref, wu_ref, wd_ref, y_ref, wg_sc, wu_sc, wd_sc):
    i = pl.program_id(0)

    @pl.when(jnp.logical_or(i == 0, be_ref[i] != be_ref[jnp.maximum(i - 1, 0)]))
    def _():
        wg_sc[...] = wg_ref[0, 0].astype(BF16)
        wu_sc[...] = wu_ref[0, 0].astype(BF16)
        wd_sc[...] = wd_ref[0, 0].astype(BF16)

    @pl.when(i < nv_ref[0])
    def _():
        rows = MOE_BLK // EXPERT_ROW_PARTS
        for p in range(EXPERT_ROW_PARTS):
            sl = slice(p * rows, (p + 1) * rows)
            x = x_ref[sl, :].astype(BF16)
            hg = jnp.dot(x, wg_sc[...], preferred_element_type=F32)
            hu = jnp.dot(x, wu_sc[...], preferred_element_type=F32)
            hid = (_silu(hg) * hu).astype(BF16)
            y_ref[sl, :] = jnp.dot(hid, wd_sc[...], preferred_element_type=F32)

    @pl.when(i >= nv_ref[0])
    def _():
        y_ref[...] = jnp.zeros(y_ref.shape, F32)


def _experts(blk_e, nvalid, xb, layer, wg, wu, wd):
    p_len, d = xb.shape
    hid = wg.shape[3]
    nblk = p_len // MOE_BLK
    grid_spec = pltpu.PrefetchScalarGridSpec(
        num_scalar_prefetch=2,
        grid=(nblk,),
        in_specs=[pl.BlockSpec((MOE_BLK, d), lambda i, be, nv: (jnp.minimum(i, nv[0] - 1), 0)),
                  pl.BlockSpec((1, 1, d, hid), lambda i, be, nv: (layer, be[i], 0, 0)),
                  pl.BlockSpec((1, 1, d, hid), lambda i, be, nv: (layer, be[i], 0, 0)),
                  pl.BlockSpec((1, 1, hid, d), lambda i, be, nv: (layer, be[i], 0, 0))],
        out_specs=pl.BlockSpec((MOE_BLK, d), lambda i, be, nv: (i, 0)),
        scratch_shapes=[pltpu.VMEM((d, hid), BF16), pltpu.VMEM((d, hid), BF16), pltpu.VMEM((hid, d), BF16)],
    )
    return pl.pallas_call(
        _expert_kernel,
        grid_spec=grid_spec,
        out_shape=jax.ShapeDtypeStruct((p_len, d), F32),
        compiler_params=_cparams(("arbitrary",)),
        name="moe_experts",
    )(blk_e, nvalid, xb, wg, wu, wd)


def _moe(h2, info, cnt, layer, wg, wu, wd):
    n, d = h2.shape
    cnt = cnt[:, :, 0].astype(jnp.int32)
    counts = jnp.sum(cnt, axis=0)
    pc = ((counts + MOE_BLK - 1) // MOE_BLK) * MOE_BLK
    pend = jnp.cumsum(pc)
    base = (pend - pc)[None, :] + jnp.cumsum(cnt, axis=0) - cnt
    eid = info[:, 0:2, :].astype(jnp.int32)
    rank = info[:, 4:6, :].astype(jnp.int32)
    sel = eid[..., None] == jnp.arange(N_EXPERTS, dtype=jnp.int32)
    dest = jnp.sum(jnp.where(sel, base[:, None, None, :], 0), axis=-1) + rank
    p_len = n * TOP_K + N_EXPERTS * MOE_BLK
    nblk = p_len // MOE_BLK
    blk_start = jnp.arange(nblk, dtype=jnp.int32) * MOE_BLK
    blk_e = jnp.minimum(jnp.sum(pend[None, :] <= blk_start[:, None], axis=1), N_EXPERTS - 1).astype(jnp.int32)
    nvalid = (pend[-1:] // MOE_BLK).astype(jnp.int32)
    dest3 = dest.reshape(dest.shape[0], 1, TOP_K * TM)
    wcol = jnp.transpose(info[:, 2:2 + TOP_K, :], (0, 2, 1)).reshape(n, TOP_K)
    xb = _dispatch(dest3, pend.astype(jnp.int32), h2, p_len)
    yb = _experts(blk_e, nvalid, xb, layer, wg, wu, wd)
    return dest3, wcol, yb


def _rope_tables(t_len, qw):
    rows = t_len // GRID_W
    ax = qw // (DIFF_HEADS * 2) // 2
    row = jnp.repeat(jnp.arange(rows, dtype=F32), GRID_W)
    col = jnp.tile(jnp.arange(GRID_W, dtype=F32), rows)
    inv = ROPE_BASE ** (-jnp.arange(0, ax, 2, dtype=F32) / ax)
    ar = row[:, None] * inv[None, :]
    ac = col[:, None] * inv[None, :]
    cr, sr, cc, sc = jnp.cos(ar), jnp.sin(ar), jnp.cos(ac), jnp.sin(ac)
    cos = jnp.concatenate([cr, cr, cc, cc], axis=1)
    sins = jnp.concatenate([-sr, sr, -sc, sc], axis=1)
    reps = qw // cos.shape[1]
    cos = jnp.concatenate([jnp.ones((TM, cos.shape[1]), F32), cos], axis=0)
    sins = jnp.concatenate([jnp.zeros((TM, sins.shape[1]), F32), sins], axis=0)
    return jnp.tile(cos, (1, reps)), jnp.tile(sins, (1, reps))


def kernel(x_prompt, x_sample, cache_k, cache_v, c, c_ctx, w_ada, b_ada, g_norm1, g_norm2, g_final, w_in_e, conv_a_w, conv_a_b, ln_a_g, ln_a_b, lam_q1, lam_k1, lam_q2, lam_k2, subln_g, w_out_e, w_in_o, pool_w, pool_scale, conv_d_w, w_out_o, w_router_g, b_router_g, w_router_e, b_router_e, w_gate, w_up, w_down):
    bp, tp, d = x_prompt.shape
    bs, ts, _ = x_sample.shape
    past = cache_k.shape[2]
    assert tp == TM and ts % TM == 0 and d % LANES == 0
    npt = bp
    tps = ts // TM
    n_p, n_s = bp * tp, bs * ts
    n = n_p + n_s
    nt = n // TM
    cw = conv_a_w.shape[2]
    qw = (w_in_e.shape[2] - 2 * cw) // 3
    dh = qw // (DIFF_HEADS * 2)
    hpt = TM // HALO

    def cond_of(i):
        return jnp.where(i < npt, 0, 1 + (i - npt) // tps)

    def tile(w):
        return pl.BlockSpec((TM, w), lambda i: (i, 0))

    def halo_prev(w):
        return pl.BlockSpec((HALO, w), lambda i: (jnp.maximum(i * hpt - 1, 0), 0))

    def halo_next(w):
        return pl.BlockSpec((HALO, w), lambda i: (jnp.minimum((i + 1) * hpt, nt * hpt - 1), 0))

    def whole(shape):
        return pl.BlockSpec(shape, lambda i: (0,) * len(shape))

    mod_spec = pl.BlockSpec((1, 6, d), lambda i: (cond_of(i), 0, 0))
    combine_specs = [pl.BlockSpec((1, 1, TOP_K * TM), lambda i: (i, 0, 0), memory_space=pltpu.SMEM),
                     pl.BlockSpec((1, 1, TOP_K * TM), lambda i: (jnp.minimum(i + 1, nt - 1), 0, 0),
                                  memory_space=pltpu.SMEM),
                     tile(TOP_K), pl.BlockSpec(memory_space=pl.ANY)]
    combine_scratch = [pltpu.VMEM((2, TOP_K, TM // SUBLANES, SUBLANES, d), F32), pltpu.SemaphoreType.DMA((2,))]

    rows = 8 * ((1 + bs + 7) // 8)
    cond = jnp.concatenate([c_ctx[None, :], c, jnp.zeros((rows - 1 - bs, d), F32)], axis=0)
    mod = _ada_table(cond, w_ada, b_ada)

    xp2, xs2 = x_prompt.reshape(n_p, d), x_sample.reshape(n_s, d)

    def prompt_tile(w):
        return pl.BlockSpec((TM, w), lambda i: (jnp.minimum(i, npt - 1), 0))

    def sample_tile(w):
        return pl.BlockSpec((TM, w), lambda i: (jnp.maximum(i - npt, 0), 0))

    def router_w(i):
        wre = jnp.transpose(w_router_e[i], (1, 0, 2)).reshape(d, N_EXPERTS)
        wr = jnp.concatenate([wre, w_router_g[i]], axis=1)
        return jnp.pad(wr, ((0, 0), (0, ROUTER_PAD - wr.shape[1]))).astype(BF16)

    def router_b(i):
        b = jnp.concatenate([b_router_e[i].reshape(N_EXPERTS), b_router_g[i]])
        return jnp.pad(b, (0, ROUTER_PAD - b.shape[0]))[:, None]

    cos, sins = _rope_tables(ts, qw)
    tab_spec = pl.BlockSpec((TM, qw), lambda i: (jnp.where(i < npt, 0, 1 + (i - npt) % tps), 0))
    kc_spec = pl.BlockSpec((1, 1, TM, DIFF_HEADS, 2, dh), lambda i: (jnp.minimum(i, npt - 1), 0, 0, 0, 0, 0))
    vc_spec = pl.BlockSpec((1, 1, TM, DIFF_HEADS, 2 * dh), lambda i: (jnp.minimum(i, npt - 1), 0, 0, 0, 0))
    u, q, k, vt, new_k, new_v = pl.pallas_call(
        functools.partial(_pre_even_kernel, npt=npt, cw=cw, qw=qw, qscale=dh ** -0.5 * math.log2(math.e)),
        grid=(nt,),
        in_specs=[prompt_tile(d), sample_tile(d), mod_spec, whole((1, d)), whole(w_in_e.shape[1:]),
                  tab_spec, tab_spec],
        out_specs=[tile(cw), tile(qw), tile(qw), pl.BlockSpec((qw, TM), lambda i: (0, i)),
                   kc_spec, vc_spec],
        out_shape=[jax.ShapeDtypeStruct((n, cw), F32), jax.ShapeDtypeStruct((n, qw), BF16),
                   jax.ShapeDtypeStruct((n, qw), BF16), jax.ShapeDtypeStruct((qw, n), BF16),
                   jax.ShapeDtypeStruct((bp, 1, tp, DIFF_HEADS, 2, dh), F32),
                   jax.ShapeDtypeStruct((bp, 1, tp, DIFF_HEADS, 2 * dh), F32)],
        scratch_shapes=[pltpu.VMEM((2, TM, qw), F32)],
        compiler_params=_cparams(("arbitrary",)),
        name="pre_even",
    )(xp2, xs2, mod[0], g_norm1[0][None, :], w_in_e[0].astype(BF16), cos, sins)

    lam_init = 0.8 - 0.6 * math.exp(-0.3 * 0)
    lam = (jnp.exp(jnp.sum(lam_q1[0] * lam_k1[0])) - jnp.exp(jnp.sum(lam_q2[0] * lam_k2[0])) + lam_init)
    lam = lam.reshape(1).astype(F32)
    sub_g = subln_g[0][:, None]
    o_p = _attention(lam, q, k, vt, sub_g, None, batch=bp, t_len=tp, row0=0, tq=TM, tk=TM,
                     post_scale=1.0 - lam_init, heads_per_step=DIFF_HEADS)
    k_ctx = cache_k[:, 0].reshape(bs * past, qw).astype(BF16)
    vt_ctx = jnp.transpose(cache_v[:, 0].reshape(bs, past, qw), (0, 2, 1)).astype(BF16)
    o_s = _attention(lam, q, k, vt, sub_g, (k_ctx, vt_ctx), batch=bs, t_len=ts, row0=n_p, tq=ATTN_TQ, tk=ATTN_TK,
                     post_scale=1.0 - lam_init)

    moe_outs = [jax.ShapeDtypeStruct((n, d), F32), jax.ShapeDtypeStruct((n, d), F32),
                jax.ShapeDtypeStruct((nt, INFO_ROWS, TM), F32), jax.ShapeDtypeStruct((nt, N_EXPERTS, 1), F32)]
    moe_out_specs = [tile(d), tile(d), pl.BlockSpec((1, INFO_ROWS, TM), lambda i: (i, 0, 0)),
                     pl.BlockSpec((1, N_EXPERTS, 1), lambda i: (i, 0, 0))]
    x1, h2, info, cnt = pl.pallas_call(
        functools.partial(_post_even_kernel, npt=npt, tps=tps),
        grid=(nt,),
        in_specs=[prompt_tile(d), sample_tile(d), tile(cw), halo_prev(cw), halo_next(cw),
                  prompt_tile(qw), sample_tile(qw), mod_spec,
                  whole((CONV_K, cw)), whole((1, cw)), whole((1, cw)), whole((1, cw)),
                  whole(w_out_e.shape[1:]), whole((1, d)), whole((d, ROUTER_PAD)), whole((ROUTER_PAD, 1))],
        out_specs=moe_out_specs,
        out_shape=moe_outs,
        scratch_shapes=[pltpu.VMEM((TM + 2 * HALO, cw), F32),
                        pltpu.VMEM((SUBLANES, TM + 2 * HALO - SUBLANES, cw), F32)],
        compiler_params=_cparams(("parallel",)),
        name="post_even",
    )(xp2, xs2, u, u, u, o_p, o_s, mod[0], conv_a_w[0], conv_a_b[0][None, :], ln_a_g[0][None, :], ln_a_b[0][None, :],
      w_out_e[0].astype(BF16), g_norm2[0][None, :], router_w(0), router_b(0))
    dest3, wcol, yb = _moe(h2, info, cnt, 0, w_gate, w_up, w_down)

    pw = pool_w.shape[2]
    x2, xc, p, bg = pl.pallas_call(
        functools.partial(_pre_odd_kernel, cw=cw),
        grid=(nt,),
        in_specs=[tile(d)] + combine_specs + [mod_spec, mod_spec, whole((1, d)), whole(w_in_o.shape[1:])],
        out_specs=[tile(d), tile(cw), tile(cw), tile(cw)],
        out_shape=[jax.ShapeDtypeStruct((n, d), F32)] + [jax.ShapeDtypeStruct((n, cw), F32)] * 3,
        scratch_shapes=combine_scratch,
        compiler_params=_cparams(("arbitrary",)),
        name="pre_odd",
    )(x1, dest3, dest3, wcol, yb, mod[0], mod[1], g_norm1[1][None, :], w_in_o[0].astype(BF16))
    pool_bd = jnp.zeros((cw, cw), F32)
    for g in range(len(POOL_WINDOWS)):
        pool_bd = pool_bd.at[g * pw:(g + 1) * pw, g * pw:(g + 1) * pw].set(pool_w[0, g])
    x3, h2, info, cnt = pl.pallas_call(
        functools.partial(_post_odd_kernel, npt=npt, tps=tps),
        grid=(nt,),
        in_specs=[tile(d), tile(cw), halo_prev(cw), halo_next(cw), tile(cw), halo_prev(cw), halo_next(cw),
                  tile(cw), mod_spec, whole((cw, cw)), whole((1, cw)), whole((SCONV_K, cw)),
                  whole(w_out_o.shape[1:]), whole((1, d)), whole((d, ROUTER_PAD)), whole((ROUTER_PAD, 1))],
        out_specs=moe_out_specs,
        out_shape=moe_outs,
        scratch_shapes=[pltpu.VMEM((TM + 2 * HALO, cw), F32), pltpu.VMEM((TM + 2 * HALO, cw), F32)],
        compiler_params=_cparams(("parallel",)),
        name="post_odd",
    )(x2, xc, xc, xc, p, p, p, bg, mod[1], pool_bd.astype(BF16), pool_scale[0][None, :], conv_d_w[0],
      w_out_o[0].astype(BF16), g_norm2[1][None, :], router_w(1), router_b(1))
    dest3, wcol, yb = _moe(h2, info, cnt, 1, w_gate, w_up, w_down)

    out_p, out_s = pl.pallas_call(
        functools.partial(_final_kernel, npt=npt),
        grid=(nt,),
        in_specs=[tile(d)] + combine_specs + [mod_spec, whole((1, d))],
        out_specs=[pl.BlockSpec((TM, d), lambda i: (jnp.minimum(i, npt - 1), 0)),
                   pl.BlockSpec((TM, d), lambda i: (jnp.maximum(i - npt, 0), 0))],
        out_shape=[jax.ShapeDtypeStruct((n_p, d), F32), jax.ShapeDtypeStruct((n_s, d), F32)],
        scratch_shapes=combine_scratch,
        compiler_params=_cparams(("arbitrary",)),
        name="final_norm",
    )(x3, dest3, dest3, wcol, yb, mod[1], g_final[None, :])

    return (out_p.reshape(bp, tp, d), out_s.reshape(bs, ts, d), new_k, new_v)
```

```python
import functools
import math

import jax
import jax.numpy as jnp
from jax import lax
from jax.experimental import pallas as pl
from jax.experimental.pallas import tpu as pltpu

F32 = jnp.float32
BF16 = jnp.bfloat16

EPS = 1e-6
GRID_W = 64
ROPE_BASE = 10000.0
CONV_K = 31
SCONV_K = 3
POOL_WINDOWS = (2, 4, 8, 16)
DIFF_HEADS = 4
N_GROUPS = 4
EXPERTS_PER_GROUP = 8
N_EXPERTS = N_GROUPS * EXPERTS_PER_GROUP
TOP_K = 2

LANES = 128
SUBLANES = 8
TM = 256
HALO = 16
PROJ_ROW_PARTS = 2
MIX_ROW_PARTS = 1
MOE_BLK = 512
ATTN_TQ = 2048
ATTN_TK = 512
CONV_ROWS = 32
INFO_ROWS = 8
ROUTER_PAD = LANES
ONES_ROWS = 2 * SUBLANES
DISPATCH_SLOTS = 3
VMEM_LIMIT = 56 * 1024 * 1024


def _cparams(sem):
    return pltpu.CompilerParams(dimension_semantics=sem, vmem_limit_bytes=VMEM_LIMIT)


def _rms_mod(x, g, sc, sh):
    ms = jnp.mean(x * x, axis=-1, keepdims=True)
    return (x * lax.rsqrt(ms + EPS)) * g * (1.0 + sc) + sh


def _silu(x):
    return x * jax.nn.sigmoid(x)


def _ada_kernel(c_ref, w_ref, b_ref, o_ref):
    cs = _silu(c_ref[...])
    o_ref[0] = jnp.dot(cs, w_ref[0], precision=lax.Precision.HIGHEST,
                       preferred_element_type=F32) + b_ref[0]


def _ada_table(cond, w_ada, b_ada):
    depth, d, six_d = w_ada.shape
    rows = cond.shape[0]
    nj = six_d // d
    out = pl.pallas_call(
        _ada_kernel,
        grid=(depth, nj),
        in_specs=[pl.BlockSpec((rows, d), lambda l, j: (0, 0)),
                  pl.BlockSpec((1, d, d), lambda l, j: (l, 0, j)),
                  pl.BlockSpec((1, 1, d), lambda l, j: (l, 0, j))],
        out_specs=pl.BlockSpec((1, rows, d), lambda l, j: (l, 0, j)),
        out_shape=jax.ShapeDtypeStruct((depth, rows, six_d), F32),
        compiler_params=_cparams(("parallel", "parallel")),
        name="ada_table",
    )(cond, w_ada, b_ada.reshape(depth, 1, six_d))
    return out.reshape(depth, rows, nj, d)


def _row_parts(parts):
    rows = TM // parts
    return [slice(p * rows, (p + 1) * rows) for p in range(parts)]


def _rope(x, cos, sins, half):
    lane = lax.broadcasted_iota(jnp.int32, (1, LANES), 1)
    first_half = (lane % (2 * half)) < half
    outs = []
    for j in range(x.shape[1] // LANES):
        sl = slice(j * LANES, (j + 1) * LANES)
        xs = x[:, sl]
        nxt = pltpu.roll(xs, LANES - half, 1)
        prv = pltpu.roll(xs, half, 1)
        rot = jnp.where(first_half, nxt, prv)
        outs.append(xs * cos[:, sl] + rot * sins[:, sl])
    return jnp.concatenate(outs, axis=1)


def _pre_even_kernel(xp_ref, xs_ref, mod_ref, g_ref, w_ref, cos_ref, sin_ref,
                     u_ref, q_ref, k_ref, vt_ref, kc_ref, vc_ref, kv_sc, *, npt, cw, qw, qscale):
    i = pl.program_id(0)
    is_prompt = i < npt
    half = kc_ref.shape[-1] // 4
    for sl in _row_parts(PROJ_ROW_PARTS):
        x = jnp.where(is_prompt, xp_ref[sl, :], xs_ref[sl, :])
        h = _rms_mod(x, g_ref[...], mod_ref[0, 1:2, :], mod_ref[0, 0:1, :])
        proj = jnp.dot(h.astype(BF16), w_ref[...], preferred_element_type=F32)
        ga = proj[:, :cw]
        gb = proj[:, cw:2 * cw]
        u_ref[sl, :] = ga * jax.nn.sigmoid(gb)
        q = proj[:, 2 * cw:2 * cw + qw]
        k = proj[:, 2 * cw + qw:2 * cw + 2 * qw]
        v = proj[:, 2 * cw + 2 * qw:]
        kv_sc[0, sl, :] = k
        kv_sc[1, sl, :] = v
        cos = cos_ref[sl, :]
        sins = sin_ref[sl, :]
        q_ref[sl, :] = (_rope(q, cos, sins, half) * qscale).astype(BF16)
        k_ref[sl, :] = _rope(k, cos, sins, half).astype(BF16)
        vt_ref[:, sl] = v.T.astype(BF16)

    @pl.when(is_prompt)
    def _():
        heads, halves, dh = kc_ref.shape[3:]
        for h in range(heads):
            vc_ref[0, 0, :, h, :] = kv_sc[1, :, h * halves * dh:(h + 1) * halves * dh]
            for c in range(halves):
                lo = (h * halves + c) * dh
                kc_ref[0, 0, :, h, c, :] = kv_sc[0, :, lo:lo + dh]


def _attn_kernel(lam_ref, q_ref, k_ref, vt_ref, *rest, nk, tk, nk_ctx, post_scale):
    if nk_ctx:
        kc_ref, vtc_ref, g_ref, o_ref, s_sc, mc_sc, m_sc, acc_sc = rest
    else:
        g_ref, o_ref, s_sc, mc_sc, m_sc, acc_sc = rest
    q = q_ref[...]
    hw = q.shape[1]
    dh = hw // 2
    lane = lax.broadcasted_iota(jnp.int32, (1, hw), 1)
    zero = jnp.zeros_like(q)
    qs = (jnp.where(lane < dh, q, zero), jnp.where(lane >= dh, q, zero))
    m_sc[...] = jnp.full(m_sc.shape, -jnp.inf, F32)
    acc_sc[...] = jnp.zeros(acc_sc.shape, F32)
    ones_rows = (lax.broadcasted_iota(jnp.int32, (ONES_ROWS, tk), 0) == 0).astype(BF16)

    def scores(kk, slot):
        for c in range(2):
            s = lax.dot_general(kk, qs[c], (((1,), (1,)), ((), ())), preferred_element_type=F32)
            s_sc[slot, c] = s
            mc_sc[slot, c] = jnp.max(s, axis=0, keepdims=True)

    def consume(vt, slot):
        vta = jnp.concatenate([vt, ones_rows], axis=0)
        for c in range(2):
            m_prev = m_sc[c]
            m_new = jnp.maximum(m_prev, mc_sc[slot, c])
            alpha = jnp.exp2(m_prev - m_new)
            p = jnp.exp2(s_sc[slot, c] - m_new).astype(BF16)
            acc_sc[c] = alpha * acc_sc[c] + jnp.dot(vta, p, preferred_element_type=F32)
            m_sc[c] = m_new

    def k_at(i):
        if i < nk_ctx:
            return kc_ref[i * tk:(i + 1) * tk, :]
        return k_ref[(i - nk_ctx) * tk:(i - nk_ctx + 1) * tk, :]

    def vt_at(i):
        if i < nk_ctx:
            return vtc_ref[0, :, i * tk:(i + 1) * tk]
        return vt_ref[:, (i - nk_ctx) * tk:(i - nk_ctx + 1) * tk]

    n_total = nk_ctx + nk
    peel = nk_ctx + (n_total - 1 - nk_ctx) % 2
    scores(k_at(0), 0)
    for i in range(peel):
        scores(k_at(i + 1), (i + 1) % 2)
        consume(vt_at(i), i % 2)

    def body(t, carry):
        for r in range(2):
            j = 2 * t + (peel + r - nk_ctx)
            nxt = pl.multiple_of((j + 1) * tk, tk)
            cur = pl.multiple_of(j * tk, tk)
            scores(k_ref[pl.ds(nxt, tk), :], (peel + r + 1) % 2)
            consume(vt_ref[:, pl.ds(cur, tk)], (peel + r) % 2)
        return carry

    lax.fori_loop(0, (n_total - 1 - peel) // 2, body, 0)
    consume(vt_at(n_total - 1), (n_total - 1) % 2)
    o = (acc_sc[0, :hw, :] / acc_sc[0, hw:hw + 1, :]
         - lam_ref[0] * (acc_sc[1, :hw, :] / acc_sc[1, hw:hw + 1, :]))
    ms = jnp.mean(o * o, axis=0, keepdims=True)
    o = o * lax.rsqrt(ms + EPS) * g_ref[...] * post_scale
    o_ref[...] = o.T.astype(BF16)


def _attn_heads_kernel(lam_ref, q_ref, k_ref, vt_ref, g_ref, o_ref, *scratch, heads, **kw):
    hw = q_ref.shape[1] // heads
    for h in range(heads):
        cols = pl.ds(h * hw, hw)
        _attn_kernel(lam_ref, q_ref.at[:, cols], k_ref.at[:, cols], vt_ref.at[cols, :], g_ref, o_ref.at[:, cols],
                     *scratch, **kw)


def _attention(lam, q, k, vt, g_col, ctx, *, batch, t_len, row0, tq, tk, post_scale, heads_per_step=1):
    hw = q.shape[1] // DIFF_HEADS
    bw = hw * heads_per_step
    nq = t_len // tq
    assert row0 % t_len == 0 and t_len % tk == 0 and (ctx is None or heads_per_step == 1)
    qb0 = row0 // tq
    sb0 = row0 // t_len
    in_specs = [pl.BlockSpec(memory_space=pltpu.SMEM),
                pl.BlockSpec((tq, bw), lambda b, h, i: (qb0 + b * nq + i, h)),
                pl.BlockSpec((t_len, bw), lambda b, h, i: (sb0 + b, h)),
                pl.BlockSpec((bw, t_len), lambda b, h, i: (h, sb0 + b))]
    args = [lam, q, k, vt]
    nk_ctx = 0
    if ctx is not None:
        k_ctx, vt_ctx = ctx
        past = vt_ctx.shape[2]
        assert past % tk == 0
        nk_ctx = past // tk
        in_specs += [pl.BlockSpec((past, hw), lambda b, h, i: (b, h)),
                     pl.BlockSpec((1, hw, past), lambda b, h, i: (b, h, 0))]
        args += [k_ctx, vt_ctx]
    in_specs.append(pl.BlockSpec((hw, 1), lambda b, h, i: (0, 0)))
    args.append(g_col)
    body = functools.partial(_attn_kernel, nk=t_len // tk, tk=tk, nk_ctx=nk_ctx, post_scale=post_scale)
    if heads_per_step > 1:
        body = functools.partial(_attn_heads_kernel, heads=heads_per_step, nk=t_len // tk, tk=tk, nk_ctx=nk_ctx,
                                 post_scale=post_scale)
    return pl.pallas_call(
        body,
        grid=(batch, DIFF_HEADS // heads_per_step, nq),
        in_specs=in_specs,
        out_specs=pl.BlockSpec((tq, bw), lambda b, h, i: (b * nq + i, h)),
        out_shape=jax.ShapeDtypeStruct((batch * t_len, q.shape[1]), BF16),
        scratch_shapes=[pltpu.VMEM((2, 2, tk, tq), F32), pltpu.VMEM((2, 2, 1, tq), F32),
                        pltpu.VMEM((2, 1, tq), F32), pltpu.VMEM((2, hw + ONES_ROWS, tq), F32)],
        compiler_params=_cparams(("parallel", "parallel", "parallel")),
        name="diff_attention",
    )(*args)


def _seq_edges(i, npt, tps):
    r = (i - npt) % tps
    is_prompt = i < npt
    first = jnp.logical_or(is_prompt, r == 0)
    last = jnp.logical_or(is_prompt, r == tps - 1)
    return first, last


def _fill_ext(ext_sc, cur_ref, prev_ref, next_ref, first, last):
    prev = prev_ref[...]
    nxt = next_ref[...]
    ext_sc[0:HALO, :] = jnp.where(first, jnp.zeros_like(prev), prev)
    ext_sc[HALO:HALO + TM, :] = cur_ref[...]
    ext_sc[HALO + TM:, :] = jnp.where(last, jnp.zeros_like(nxt), nxt)


def _route_tile(logits, b_ref, info_ref, cnt_ref):
    lt = logits.T + b_ref[...]
    gl = lt[N_EXPERTS:N_EXPERTS + N_GROUPS]
    ge = jnp.exp(gl - jnp.max(gl, axis=0, keepdims=True))
    gp = ge / jnp.sum(ge, axis=0, keepdims=True)
    best = gp[0:1]
    gidx = jnp.zeros(best.shape, jnp.int32)
    el = lt[0:EXPERTS_PER_GROUP]
    for g in range(1, N_GROUPS):
        better = gp[g:g + 1] > best
        gidx = jnp.where(better, g, gidx)
        best = jnp.where(better, gp[g:g + 1], best)
    for g in range(1, N_GROUPS):
        el = jnp.where(gidx == g, lt[g * EXPERTS_PER_GROUP:(g + 1) * EXPERTS_PER_GROUP], el)
    rio = lax.broadcasted_iota(jnp.int32, el.shape, 0)
    v0 = jnp.max(el, axis=0, keepdims=True)
    i0 = jnp.min(jnp.where(el == v0, rio, EXPERTS_PER_GROUP), axis=0, keepdims=True)
    el2 = jnp.where(rio == i0, -jnp.inf, el)
    v1 = jnp.max(el2, axis=0, keepdims=True)
    i1 = jnp.min(jnp.where(el2 == v1, rio, EXPERTS_PER_GROUP), axis=0, keepdims=True)
    e1 = jnp.exp(v1 - v0)
    den = 1.0 + e1
    w0 = (1.0 / den) * best
    w1 = (e1 / den) * best
    eid0 = gidx * EXPERTS_PER_GROUP + i0
    eid1 = gidx * EXPERTS_PER_GROUP + i1
    tm = lt.shape[1]
    eio = lax.broadcasted_iota(jnp.int32, (N_EXPERTS, tm), 0)
    oh0 = eio == eid0
    oh1 = eio == eid1
    ohs = jnp.where(oh0, 1.0, 0.0) + jnp.where(oh1, 1.0, 0.0)
    upper = (lax.broadcasted_iota(jnp.int32, (tm, tm), 0)
             < lax.broadcasted_iota(jnp.int32, (tm, tm), 1)).astype(BF16)
    before = jnp.dot(ohs.astype(BF16), upper, preferred_element_type=F32)
    rank0 = jnp.sum(jnp.where(oh0, before, 0.0), axis=0, keepdims=True)
    rank1 = jnp.sum(jnp.where(oh1, before, 0.0), axis=0, keepdims=True)
    info_ref[0] = jnp.concatenate([eid0.astype(F32), eid1.astype(F32), w0, w1, rank0, rank1,
                                   jnp.zeros((INFO_ROWS - 6, tm), F32)], axis=0)
    cnt_ref[0] = jnp.sum(ohs, axis=1, keepdims=True)


def _moe_prologue(x_new, sl, mod_ref, g2_ref, wr_ref, rb_ref, h2_ref, info_ref, cnt_ref):
    h2 = _rms_mod(x_new, g2_ref[...], mod_ref[0, 4:5, :], mod_ref[0, 3:4, :])
    h2_ref[sl, :] = h2
    assert sl.stop - sl.start == TM
    _route_tile(jnp.dot(h2.astype(BF16), wr_ref[...], preferred_element_type=F32), rb_ref, info_ref, cnt_ref)


def _post_even_kernel(xp_ref, xs_ref, u_ref, up_ref, un_ref, op_ref, os_ref, mod_ref, cw_ref, cb_ref,
                      lng_ref, lnb_ref, wo_ref, g2_ref, wr_ref, rb_ref, x1_ref, h2_ref, info_ref, cnt_ref,
                      ext_sc, sh_sc, *, npt, tps):
    i = pl.program_id(0)
    first, last = _seq_edges(i, npt, tps)
    _fill_ext(ext_sc, u_ref, up_ref, un_ref, first, last)
    cw = u_ref.shape[1]
    span = sh_sc.shape[1]
    for r in range(SUBLANES):
        sh_sc[r] = ext_sc[r:r + span, :]
    rows = CONV_ROWS
    is_prompt = i < npt
    for sl in _row_parts(MIX_ROW_PARTS):
        chunks = []
        for r0 in range(sl.start, sl.stop, rows):
            acc = None
            for k in range(CONV_K):
                off = HALO - CONV_K // 2 + k
                start = r0 + off - off % SUBLANES
                term = sh_sc[off % SUBLANES, start:start + rows, :] * cw_ref[k:k + 1, :]
                acc = term if acc is None else acc + term
            chunks.append(acc)
        conv = jnp.concatenate(chunks, axis=0) + cb_ref[...]
        mu = jnp.mean(conv, axis=-1, keepdims=True)
        cen = conv - mu
        var = jnp.mean(cen * cen, axis=-1, keepdims=True)
        uu = _silu(cen * lax.rsqrt(var + EPS) * lng_ref[...] + lnb_ref[...])
        o = jnp.where(is_prompt, op_ref[sl, :], os_ref[sl, :])
        m = (jnp.dot(uu.astype(BF16), wo_ref[0:cw, :], preferred_element_type=F32)
             + jnp.dot(o, wo_ref[cw:, :], preferred_element_type=F32))
        x1 = jnp.where(is_prompt, xp_ref[sl, :], xs_ref[sl, :]) + mod_ref[0, 2:3, :] * m
        x1_ref[sl, :] = x1
        _moe_prologue(x1, sl, mod_ref, g2_ref, wr_ref, rb_ref, h2_ref, info_ref, cnt_ref)


def _gather_start(dest_ref, yb_hbm, ybuf, sem, slot):
    def body(i, carry):
        for j in range(SUBLANES):
            for k in range(TOP_K):
                row = dest_ref[0, 0, k * TM + i * SUBLANES + j]
                pltpu.make_async_copy(yb_hbm.at[pl.ds(row, 1)], ybuf.at[slot, k, i, pl.ds(j, 1)],
                                      sem.at[slot]).start(priority=k % 2)
        return carry

    lax.fori_loop(0, TM // SUBLANES, body, 0)


def _moe_gather(dcur_ref, dnxt_ref, yb_hbm, ybuf, sem):
    t = pl.program_id(0)
    slot = t % 2

    @pl.when(t == 0)
    def _():
        _gather_start(dcur_ref, yb_hbm, ybuf, sem, 0)

    _gather_wait(ybuf, sem, slot)
    return slot


def _gather_wait(ybuf, sem, slot):
    for k in range(TOP_K):
        pltpu.make_async_copy(ybuf.at[slot, k], ybuf.at[slot, k], sem.at[slot]).wait()


def _moe_prefetch(dnxt_ref, yb_hbm, ybuf, sem, slot):
    for i in range(TM // SUBLANES):
        for j in range(SUBLANES):
            for k in range(TOP_K):
                row = dnxt_ref[0, 0, k * TM + i * SUBLANES + j]
                pltpu.make_async_copy(yb_hbm.at[pl.ds(row, 1)], ybuf.at[1 - slot, k, i, pl.ds(j, 1)],
                                      sem.at[1 - slot]).start(priority=k % 2)

    @pl.when(pl.program_id(0) == pl.num_programs(0) - 1)
    def _():
        _gather_wait(ybuf, sem, 1 - slot)


def _combined_rows(cw_ref, ybuf, slot, sl):
    d = ybuf.shape[-1]
    g = slice(sl.start // SUBLANES, sl.stop // SUBLANES)
    w = cw_ref[sl, :]
    y = w[:, 0:1] * ybuf[slot, 0, g].reshape(sl.stop - sl.start, d)
    for k in range(1, TOP_K):
        y = y + w[:, k:k + 1] * ybuf[slot, k, g].reshape(sl.stop - sl.start, d)
    return y


def _dispatch_kernel(dest_ref, pend_ref, h2_hbm, xb_hbm, stage, zeros, in_sem, row_sem, zero_sem):
    t = pl.program_id(0)
    nt = pl.num_programs(0)
    slot = t % DISPATCH_SLOTS

    tile_rows = TM // SUBLANES

    def stage_in(tile_idx, s):
        return pltpu.make_async_copy(h2_hbm.at[pl.ds(tile_idx * tile_rows, tile_rows)], stage.at[s], in_sem.at[s])

    def drain_rows(s):
        for _ in range(TOP_K):
            pltpu.make_async_copy(stage.at[s], stage.at[s], row_sem.at[s]).wait()

    def zero_block(start):
        return pltpu.make_async_copy(zeros, xb_hbm.at[pl.ds(pl.multiple_of(start, MOE_BLK), MOE_BLK)], zero_sem)

    def zero_fill(op):
        for e in range(N_EXPERTS):
            op(zero_block(jnp.maximum(pend_ref[e] - MOE_BLK, 0)))
        for j in range(N_EXPERTS):
            start = pend_ref[N_EXPERTS - 1] + j * MOE_BLK

            @pl.when(start < xb_hbm.shape[0])
            def _():
                op(zero_block(start))

    @pl.when(t == 0)
    def _():
        stage_in(0, 0).start()
        zeros[...] = jnp.zeros(zeros.shape, zeros.dtype)
        zero_fill(lambda cp: cp.start())
        zero_fill(lambda cp: cp.wait())

    @pl.when(jnp.logical_and(t == 0, nt > 1))
    def _():
        stage_in(1, 1).start()

    stage_in(t, slot).wait()

    def body(i, carry):
        for j in range(SUBLANES):
            for k in range(TOP_K):
                row = dest_ref[0, 0, k * TM + i * SUBLANES + j]
                pltpu.make_async_copy(stage.at[slot, i, pl.ds(j, 1)], xb_hbm.at[pl.ds(row, 1)],
                                      row_sem.at[slot]).start(priority=k % 2)
        return carry

    lax.fori_loop(0, tile_rows, body, 0)

    @pl.when(t > 0)
    def _():
        drain_rows((t + DISPATCH_SLOTS - 1) % DISPATCH_SLOTS)

    @pl.when(t + 2 < nt)
    def _():
        stage_in(t + 2, (t + 2) % DISPATCH_SLOTS).start()

    @pl.when(t == nt - 1)
    def _():
        drain_rows(slot)


def _dispatch(dest3, pend, h2, p_len):
    n, d = h2.shape
    return pl.pallas_call(
        _dispatch_kernel,
        grid=(n // TM,),
        in_specs=[pl.BlockSpec((1, 1, TOP_K * TM), lambda t: (t, 0, 0), memory_space=pltpu.SMEM),
                  pl.BlockSpec(memory_space=pltpu.SMEM), pl.BlockSpec(memory_space=pl.ANY)],
        out_specs=pl.BlockSpec(memory_space=pl.ANY),
        out_shape=jax.ShapeDtypeStruct((p_len, d), h2.dtype),
        scratch_shapes=[pltpu.VMEM((DISPATCH_SLOTS, TM // SUBLANES, SUBLANES, d), h2.dtype),
                        pltpu.VMEM((MOE_BLK, d), h2.dtype),
                        pltpu.SemaphoreType.DMA((DISPATCH_SLOTS,)), pltpu.SemaphoreType.DMA((DISPATCH_SLOTS,)),
                        pltpu.SemaphoreType.DMA(())],
        compiler_params=_cparams(("arbitrary",)),
        name="moe_dispatch",
    )(dest3, pend, h2.reshape(n // SUBLANES, SUBLANES, d))


def _pre_odd_kernel(x_ref, dcur_ref, dnxt_ref, cw_ref, yb_hbm, modp_ref, mod_ref, g_ref, w_ref,
                    x2_ref, xc_ref, p_ref, bg_ref, ybuf, sem, *, cw):
    slot = _moe_gather(dcur_ref, dnxt_ref, yb_hbm, ybuf, sem)
    for sl in _row_parts(PROJ_ROW_PARTS):
        x2 = x_ref[sl, :] + modp_ref[0, 5:6, :] * _combined_rows(cw_ref, ybuf, slot, sl)
        x2_ref[sl, :] = x2
        h = _rms_mod(x2, g_ref[...], mod_ref[0, 1:2, :], mod_ref[0, 0:1, :])
        proj = jnp.dot(h.astype(BF16), w_ref[...], preferred_element_type=F32)
        xc_ref[sl, :] = proj[:, :cw]
        hd = proj[:, cw:2 * cw]
        bg_ref[sl, :] = proj[:, 2 * cw:3 * cw]
        p_ref[sl, :] = proj[:, 3 * cw:] * hd
    _moe_prefetch(dnxt_ref, yb_hbm, ybuf, sem, slot)


def _post_odd_kernel(x_ref, xc_ref, xcp_ref, xcn_ref, p_ref, pp_ref, pn_ref, bg_ref, mod_ref,
                     pw_ref, ps_ref, dw_ref, wo_ref, g2_ref, wr_ref, rb_ref,
                     x3_ref, h2_ref, info_ref, cnt_ref, extc_sc, extp_sc, *, npt, tps):
    i = pl.program_id(0)
    first, last = _seq_edges(i, npt, tps)
    _fill_ext(extc_sc, xc_ref, xcp_ref, xcn_ref, first, last)
    _fill_ext(extp_sc, p_ref, pp_ref, pn_ref, first, last)
    cw = xc_ref.shape[1]
    gc = cw // len(POOL_WINDOWS)
    is_prompt = i < npt
    t_len = jnp.where(is_prompt, TM, TM * tps)
    tile_pos = jnp.where(is_prompt, 0, ((i - npt) % tps) * TM)
    for rs in _row_parts(MIX_ROW_PARTS):
        nrows = rs.stop - rs.start
        base = HALO + rs.start
        pos = lax.broadcasted_iota(jnp.int32, (nrows, 1), 0) + (tile_pos + rs.start)
        ds = []
        for g, w in enumerate(POOL_WINDOWS):
            sl = slice(g * gc, (g + 1) * gc)
            acc = None
            for o in range(-(w // 2), w // 2):
                term = extc_sc[base + o:base + o + nrows, sl]
                acc = term if acc is None else acc + term
            lo = jnp.maximum(pos - w // 2, 0)
            hi = jnp.minimum(pos + w // 2 - 1, t_len - 1)
            cnt = (hi - lo + 1).astype(F32)
            ds.append(acc / cnt - extc_sc[base:base + nrows, sl])
        d = jnp.concatenate(ds, axis=1).astype(BF16)
        yc = jnp.dot(d, pw_ref[...], preferred_element_type=F32) * ps_ref[...]
        conv = None
        for k in range(SCONV_K):
            start = base - SCONV_K // 2 + k
            term = extp_sc[start:start + nrows, :] * dw_ref[k:k + 1, :]
            conv = term if conv is None else conv + term
        yd = bg_ref[rs, :] * conv
        m = (jnp.dot(yc.astype(BF16), wo_ref[0:cw, :], preferred_element_type=F32)
             + jnp.dot(yd.astype(BF16), wo_ref[cw:, :], preferred_element_type=F32))
        x3 = x_ref[rs, :] + mod_ref[0, 2:3, :] * m
        x3_ref[rs, :] = x3
        _moe_prologue(x3, rs, mod_ref, g2_ref, wr_ref, rb_ref, h2_ref, info_ref, cnt_ref)


def _final_kernel(x_ref, dcur_ref, dnxt_ref, cw_ref, yb_hbm, modp_ref, g_ref, op_ref, os_ref, ybuf, sem, *, npt):
    i = pl.program_id(0)
    slot = _moe_gather(dcur_ref, dnxt_ref, yb_hbm, ybuf, sem)
    x = x_ref[...] + modp_ref[0, 5:6, :] * _combined_rows(cw_ref, ybuf, slot, slice(0, TM))
    ms = jnp.mean(x * x, axis=-1, keepdims=True)
    out = x * lax.rsqrt(ms + EPS) * g_ref[...]
    _moe_prefetch(dnxt_ref, yb_hbm, ybuf, sem, slot)

    @pl.when(i < npt)
    def _():
        op_ref[...] = out

    @pl.when(i >= npt)
    def _():
        os_ref[...] = out


def _expert_kernel(be_ref, nv_ref, x_ref, wg_ref, wu_ref, wd_ref, y_ref, wg_sc, wu_sc, wd_sc):
    i = pl.program_id(0)

    @pl.when(jnp.logical_or(i == 0, be_ref[i] != be_ref[jnp.maximum(i - 1, 0)]))
    def _():
        wg_sc[...] = wg_ref[0, 0].astype(BF16)
        wu_sc[...] = wu_ref[0, 0].astype(BF16)
        wd_sc[...] = wd_ref[0, 0].astype(BF16)

    @pl.when(i < nv_ref[0])
    def _():
        x = x_ref[...].astype(BF16)
        hg = jnp.dot(x, wg_sc[...], preferred_element_type=F32)
        hu = jnp.dot(x, wu_sc[...], preferred_element_type=F32)
        hid = (_silu(hg) * hu).astype(BF16)
        y_ref[...] = jnp.dot(hid, wd_sc[...], preferred_element_type=F32)

    @pl.when(i >= nv_ref[0])
    def _():
        y_ref[...] = jnp.zeros(y_ref.shape, F32)


def _experts(blk_e, nvalid, xb, layer, wg, wu, wd):
    p_len, d = xb.shape
    hid = wg.shape[3]
    nblk = p_len // MOE_BLK
    grid_spec = pltpu.PrefetchScalarGridSpec(
        num_scalar_prefetch=2,
        grid=(nblk,),
        in_specs=[pl.BlockSpec((MOE_BLK, d), lambda i, be, nv: (jnp.minimum(i, nv[0] - 1), 0)),
                  pl.BlockSpec((1, 1, d, hid), lambda i, be, nv: (layer, be[i], 0, 0)),
                  pl.BlockSpec((1, 1, d, hid), lambda i, be, nv: (layer, be[i], 0, 0)),
                  pl.BlockSpec((1, 1, hid, d), lambda i, be, nv: (layer, be[i], 0, 0))],
        out_specs=pl.BlockSpec((MOE_BLK, d), lambda i, be, nv: (i, 0)),
        scratch_shapes=[pltpu.VMEM((d, hid), BF16), pltpu.VMEM((d, hid), BF16), pltpu.VMEM((hid, d), BF16)],
    )
    return pl.pallas_call(
        _expert_kernel,
        grid_spec=grid_spec,
        out_shape=jax.ShapeDtypeStruct((p_len, d), F32),
        compiler_params=_cparams(("arbitrary",)),
        name="moe_experts",
    )(blk_e, nvalid, xb, wg, wu, wd)


def _moe(h2, info, cnt, layer, wg, wu, wd):
    n, d = h2.shape
    cnt = cnt[:, :, 0].astype(jnp.int32)
    counts = jnp.sum(cnt, axis=0)
    pc = ((counts + MOE_BLK - 1) // MOE_BLK) * MOE_BLK
    pend = jnp.cumsum(pc)
    base = (pend - pc)[None, :] + jnp.cumsum(cnt, axis=0) - cnt
    eid = info[:, 0:2, :].astype(jnp.int32)
    rank = info[:, 4:6, :].astype(jnp.int32)
    sel = eid[..., None] == jnp.arange(N_EXPERTS, dtype=jnp.int32)
    dest = jnp.sum(jnp.where(sel, base[:, None, None, :], 0), axis=-1) + rank
    p_len = n * TOP_K + N_EXPERTS * MOE_BLK
    nblk = p_len // MOE_BLK
    blk_start = jnp.arange(nblk, dtype=jnp.int32) * MOE_BLK
    blk_e = jnp.minimum(jnp.sum(pend[None, :] <= blk_start[:, None], axis=1), N_EXPERTS - 1).astype(jnp.int32)
    nvalid = (pend[-1:] // MOE_BLK).astype(jnp.int32)
    dest3 = dest.reshape(dest.shape[0], 1, TOP_K * TM)
    wcol = jnp.transpose(info[:, 2:2 + TOP_K, :], (0, 2, 1)).reshape(n, TOP_K)
    xb = _dispatch(dest3, pend.astype(jnp.int32), h2, p_len)
    yb = _experts(blk_e, nvalid, xb, layer, wg, wu, wd)
    return dest3, wcol, yb


def _rope_tables(t_len, qw):
    rows = t_len // GRID_W
    ax = qw // (DIFF_HEADS * 2) // 2
    row = jnp.repeat(jnp.arange(rows, dtype=F32), GRID_W)
    col = jnp.tile(jnp.arange(GRID_W, dtype=F32), rows)
    inv = ROPE_BASE ** (-jnp.arange(0, ax, 2, dtype=F32) / ax)
    ar = row[:, None] * inv[None, :]
    ac = col[:, None] * inv[None, :]
    cr, sr, cc, sc = jnp.cos(ar), jnp.sin(ar), jnp.cos(ac), jnp.sin(ac)
    cos = jnp.concatenate([cr, cr, cc, cc], axis=1)
    sins = jnp.concatenate([-sr, sr, -sc, sc], axis=1)
    reps = qw // cos.shape[1]
    cos = jnp.concatenate([jnp.ones((TM, cos.shape[1]), F32), cos], axis=0)
    sins = jnp.concatenate([jnp.zeros((TM, sins.shape[1]), F32), sins], axis=0)
    return jnp.tile(cos, (1, reps)), jnp.tile(sins, (1, reps))


def kernel(x_prompt, x_sample, cache_k, cache_v, c, c_ctx, w_ada, b_ada, g_norm1, g_norm2, g_final, w_in_e, conv_a_w, conv_a_b, ln_a_g, ln_a_b, lam_q1, lam_k1, lam_q2, lam_k2, subln_g, w_out_e, w_in_o, pool_w, pool_scale, conv_d_w, w_out_o, w_router_g, b_router_g, w_router_e, b_router_e, w_gate, w_up, w_down):
    bp, tp, d = x_prompt.shape
    bs, ts, _ = x_sample.shape
    past = cache_k.shape[2]
    assert tp == TM and ts % TM == 0 and d % LANES == 0
    npt = bp
    tps = ts // TM
    n_p, n_s = bp * tp, bs * ts
    n = n_p + n_s
    nt = n // TM
    cw = conv_a_w.shape[2]
    qw = (w_in_e.shape[2] - 2 * cw) // 3
    dh = qw // (DIFF_HEADS * 2)
    hpt = TM // HALO

    def cond_of(i):
        return jnp.where(i < npt, 0, 1 + (i - npt) // tps)

    def tile(w):
        return pl.BlockSpec((TM, w), lambda i: (i, 0))

    def halo_prev(w):
        return pl.BlockSpec((HALO, w), lambda i: (jnp.maximum(i * hpt - 1, 0), 0))

    def halo_next(w):
        return pl.BlockSpec((HALO, w), lambda i: (jnp.minimum((i + 1) * hpt, nt * hpt - 1), 0))

    def whole(shape):
        return pl.BlockSpec(shape, lambda i: (0,) * len(shape))

    mod_spec = pl.BlockSpec((1, 6, d), lambda i: (cond_of(i), 0, 0))
    combine_specs = [pl.BlockSpec((1, 1, TOP_K * TM), lambda i: (i, 0, 0), memory_space=pltpu.SMEM),
                     pl.BlockSpec((1, 1, TOP_K * TM), lambda i: (jnp.minimum(i + 1, nt - 1), 0, 0),
                                  memory_space=pltpu.SMEM),
                     tile(TOP_K), pl.BlockSpec(memory_space=pl.ANY)]
    combine_scratch = [pltpu.VMEM((2, TOP_K, TM // SUBLANES, SUBLANES, d), F32), pltpu.SemaphoreType.DMA((2,))]

    rows = SUBLANES * pl.cdiv(1 + bs, SUBLANES)
    cond = jnp.concatenate([c_ctx[None, :], c, jnp.zeros((rows - 1 - bs, d), F32)], axis=0)
    mod = _ada_table(cond, w_ada, b_ada)

    xp2, xs2 = x_prompt.reshape(n_p, d), x_sample.reshape(n_s, d)

    def prompt_tile(w):
        return pl.BlockSpec((TM, w), lambda i: (jnp.minimum(i, npt - 1), 0))

    def sample_tile(w):
        return pl.BlockSpec((TM, w), lambda i: (jnp.maximum(i - npt, 0), 0))

    def router_w(i):
        wre = jnp.transpose(w_router_e[i], (1, 0, 2)).reshape(d, N_EXPERTS)
        wr = jnp.concatenate([wre, w_router_g[i]], axis=1)
        return jnp.pad(wr, ((0, 0), (0, ROUTER_PAD - wr.shape[1]))).astype(BF16)

    def router_b(i):
        b = jnp.concatenate([b_router_e[i].reshape(N_EXPERTS), b_router_g[i]])
        return jnp.pad(b, (0, ROUTER_PAD - b.shape[0]))[:, None]

    cos, sins = _rope_tables(ts, qw)
    tab_spec = pl.BlockSpec((TM, qw), lambda i: (jnp.where(i < npt, 0, 1 + (i - npt) % tps), 0))
    kc_spec = pl.BlockSpec((1, 1, TM, DIFF_HEADS, 2, dh), lambda i: (jnp.minimum(i, npt - 1), 0, 0, 0, 0, 0))
    vc_spec = pl.BlockSpec((1, 1, TM, DIFF_HEADS, 2 * dh), lambda i: (jnp.minimum(i, npt - 1), 0, 0, 0, 0))
    u, q, k, vt, new_k, new_v = pl.pallas_call(
        functools.partial(_pre_even_kernel, npt=npt, cw=cw, qw=qw, qscale=dh ** -0.5 * math.log2(math.e)),
        grid=(nt,),
        in_specs=[prompt_tile(d), sample_tile(d), mod_spec, whole((1, d)), whole(w_in_e.shape[1:]),
                  tab_spec, tab_spec],
        out_specs=[tile(cw), tile(qw), tile(qw), pl.BlockSpec((qw, TM), lambda i: (0, i)),
                   kc_spec, vc_spec],
        out_shape=[jax.ShapeDtypeStruct((n, cw), F32), jax.ShapeDtypeStruct((n, qw), BF16),
                   jax.ShapeDtypeStruct((n, qw), BF16), jax.ShapeDtypeStruct((qw, n), BF16),
                   jax.ShapeDtypeStruct((bp, 1, tp, DIFF_HEADS, 2, dh), F32),
                   jax.ShapeDtypeStruct((bp, 1, tp, DIFF_HEADS, 2 * dh), F32)],
        scratch_shapes=[pltpu.VMEM((2, TM, qw), F32)],
        compiler_params=_cparams(("arbitrary",)),
        name="pre_even",
    )(xp2, xs2, mod[0], g_norm1[0][None, :], w_in_e[0].astype(BF16), cos, sins)

    lam_init = 0.8 - 0.6 * math.exp(-0.3 * 0)
    lam = (jnp.exp(jnp.sum(lam_q1[0] * lam_k1[0])) - jnp.exp(jnp.sum(lam_q2[0] * lam_k2[0])) + lam_init)
    lam = lam.reshape(1).astype(F32)
    sub_g = subln_g[0][:, None]
    o_p = _attention(lam, q, k, vt, sub_g, None, batch=bp, t_len=tp, row0=0, tq=TM, tk=TM,
                     post_scale=1.0 - lam_init, heads_per_step=DIFF_HEADS)
    k_ctx = cache_k[:, 0].reshape(bs * past, qw).astype(BF16)
    vt_ctx = jnp.transpose(cache_v[:, 0].reshape(bs, past, qw), (0, 2, 1)).astype(BF16)
    o_s = _attention(lam, q, k, vt, sub_g, (k_ctx, vt_ctx), batch=bs, t_len=ts, row0=n_p, tq=ATTN_TQ, tk=ATTN_TK,
                     post_scale=1.0 - lam_init)

    moe_outs = [jax.ShapeDtypeStruct((n, d), F32), jax.ShapeDtypeStruct((n, d), F32),
                jax.ShapeDtypeStruct((nt, INFO_ROWS, TM), F32), jax.ShapeDtypeStruct((nt, N_EXPERTS, 1), F32)]
    moe_out_specs = [tile(d), tile(d), pl.BlockSpec((1, INFO_ROWS, TM), lambda i: (i, 0, 0)),
                     pl.BlockSpec((1, N_EXPERTS, 1), lambda i: (i, 0, 0))]
    x1, h2, info, cnt = pl.pallas_call(
        functools.partial(_post_even_kernel, npt=npt, tps=tps),
        grid=(nt,),
        in_specs=[prompt_tile(d), sample_tile(d), tile(cw), halo_prev(cw), halo_next(cw),
                  prompt_tile(qw), sample_tile(qw), mod_spec,
                  whole((CONV_K, cw)), whole((1, cw)), whole((1, cw)), whole((1, cw)),
                  whole(w_out_e.shape[1:]), whole((1, d)), whole((d, ROUTER_PAD)), whole((ROUTER_PAD, 1))],
        out_specs=moe_out_specs,
        out_shape=moe_outs,
        scratch_shapes=[pltpu.VMEM((TM + 2 * HALO, cw), F32),
                        pltpu.VMEM((SUBLANES, TM + 2 * HALO - SUBLANES, cw), F32)],
        compiler_params=_cparams(("parallel",)),
        name="post_even",
    )(xp2, xs2, u, u, u, o_p, o_s, mod[0], conv_a_w[0], conv_a_b[0][None, :], ln_a_g[0][None, :], ln_a_b[0][None, :],
      w_out_e[0].astype(BF16), g_norm2[0][None, :], router_w(0), router_b(0))
    dest3, wcol, yb = _moe(h2, info, cnt, 0, w_gate, w_up, w_down)

    pw = pool_w.shape[2]
    x2, xc, p, bg = pl.pallas_call(
        functools.partial(_pre_odd_kernel, cw=cw),
        grid=(nt,),
        in_specs=[tile(d)] + combine_specs + [mod_spec, mod_spec, whole((1, d)), whole(w_in_o.shape[1:])],
        out_specs=[tile(d), tile(cw), tile(cw), tile(cw)],
        out_shape=[jax.ShapeDtypeStruct((n, d), F32)] + [jax.ShapeDtypeStruct((n, cw), F32)] * 3,
        scratch_shapes=combine_scratch,
        compiler_params=_cparams(("arbitrary",)),
        name="pre_odd",
    )(x1, dest3, dest3, wcol, yb, mod[0], mod[1], g_norm1[1][None, :], w_in_o[0].astype(BF16))
    pool_bd = jnp.zeros((cw, cw), F32)
    for g in range(len(POOL_WINDOWS)):
        pool_bd = pool_bd.at[g * pw:(g + 1) * pw, g * pw:(g + 1) * pw].set(pool_w[0, g])
    x3, h2, info, cnt = pl.pallas_call(
        functools.partial(_post_odd_kernel, npt=npt, tps=tps),
        grid=(nt,),
        in_specs=[tile(d), tile(cw), halo_prev(cw), halo_next(cw), tile(cw), halo_prev(cw), halo_next(cw),
                  tile(cw), mod_spec, whole((cw, cw)), whole((1, cw)), whole((SCONV_K, cw)),
                  whole(w_out_o.shape[1:]), whole((1, d)), whole((d, ROUTER_PAD)), whole((ROUTER_PAD, 1))],
        out_specs=moe_out_specs,
        out_shape=moe_outs,
        scratch_shapes=[pltpu.VMEM((TM + 2 * HALO, cw), F32), pltpu.VMEM((TM + 2 * HALO, cw), F32)],
        compiler_params=_cparams(("parallel",)),
        name="post_odd",
    )(x2, xc, xc, xc, p, p, p, bg, mod[1], pool_bd.astype(BF16), pool_scale[0][None, :], conv_d_w[0],
      w_out_o[0].astype(BF16), g_norm2[1][None, :], router_w(1), router_b(1))
    dest3, wcol, yb = _moe(h2, info, cnt, 1, w_gate, w_up, w_down)

    out_p, out_s = pl.pallas_call(
        functools.partial(_final_kernel, npt=npt),
        grid=(nt,),
        in_specs=[tile(d)] + combine_specs + [mod_spec, whole((1, d))],
        out_specs=[pl.BlockSpec((TM, d), lambda i: (jnp.minimum(i, npt - 1), 0)),
                   pl.BlockSpec((TM, d), lambda i: (jnp.maximum(i - npt, 0), 0))],
        out_shape=[jax.ShapeDtypeStruct((n_p, d), F32), jax.ShapeDtypeStruct((n_s, d), F32)],
        scratch_shapes=combine_scratch,
        compiler_params=_cparams(("arbitrary",)),
        name="final_norm",
    )(x3, dest3, dest3, wcol, yb, mod[1], g_final[None, :])

    return (out_p.reshape(bp, tp, d), out_s.reshape(bs, ts, d), new_k, new_v)
```

```python
import functools
import math

import jax
import jax.numpy as jnp
from jax import lax
from jax.experimental import pallas as pl
from jax.experimental.pallas import tpu as pltpu

F32 = jnp.float32
BF16 = jnp.bfloat16

EPS = 1e-6
GRID_W = 64
ROPE_BASE = 10000.0
CONV_K = 31
SCONV_K = 3
POOL_WINDOWS = (2, 4, 8, 16)
DIFF_HEADS = 4
N_GROUPS = 4
EXPERTS_PER_GROUP = 8
N_EXPERTS = N_GROUPS * EXPERTS_PER_GROUP
TOP_K = 2

LANES = 128
SUBLANES = 8
TM = 256
HALO = 16
PROJ_ROW_PARTS = 2
MIX_ROW_PARTS = 1
FINAL_ROW_PARTS = 8
MOE_BLK = 512
ATTN_TQ = 2048
ATTN_TK = 512
CONV_ROWS = 32
INFO_ROWS = 8
ROUTER_PAD = LANES
ONES_ROWS = 2 * SUBLANES
DISPATCH_SLOTS = 3
VMEM_LIMIT = 56 * 1024 * 1024


def _cparams(sem):
    return pltpu.CompilerParams(dimension_semantics=sem, vmem_limit_bytes=VMEM_LIMIT)


def _rms_mod(x, g, sc, sh):
    ms = jnp.mean(x * x, axis=-1, keepdims=True)
    return (x * lax.rsqrt(ms + EPS)) * g * (1.0 + sc) + sh


def _silu(x):
    return x * jax.nn.sigmoid(x)


def _ada_kernel(c_ref, w_ref, b_ref, o_ref):
    cs = _silu(c_ref[...])
    o_ref[0] = jnp.dot(cs, w_ref[0], precision=lax.Precision.HIGHEST,
                       preferred_element_type=F32) + b_ref[0]


def _ada_table(cond, w_ada, b_ada):
    depth, d, six_d = w_ada.shape
    rows = cond.shape[0]
    nj = six_d // d
    out = pl.pallas_call(
        _ada_kernel,
        grid=(depth, nj),
        in_specs=[pl.BlockSpec((rows, d), lambda l, j: (0, 0)),
                  pl.BlockSpec((1, d, d), lambda l, j: (l, 0, j)),
                  pl.BlockSpec((1, 1, d), lambda l, j: (l, 0, j))],
        out_specs=pl.BlockSpec((1, rows, d), lambda l, j: (l, 0, j)),
        out_shape=jax.ShapeDtypeStruct((depth, rows, six_d), F32),
        compiler_params=_cparams(("parallel", "parallel")),
        name="ada_table",
    )(cond, w_ada, b_ada.reshape(depth, 1, six_d))
    return out.reshape(depth, rows, nj, d)


def _row_parts(parts):
    rows = TM // parts
    return [slice(p * rows, (p + 1) * rows) for p in range(parts)]


def _rope(x, cos, sins, half):
    lane = lax.broadcasted_iota(jnp.int32, (1, LANES), 1)
    first_half = (lane % (2 * half)) < half
    outs = []
    for j in range(x.shape[1] // LANES):
        sl = slice(j * LANES, (j + 1) * LANES)
        xs = x[:, sl]
        nxt = pltpu.roll(xs, LANES - half, 1)
        prv = pltpu.roll(xs, half, 1)
        rot = jnp.where(first_half, nxt, prv)
        outs.append(xs * cos[:, sl] + rot * sins[:, sl])
    return jnp.concatenate(outs, axis=1)


def _pre_even_kernel(xp_ref, xs_ref, mod_ref, g_ref, w_ref, cos_ref, sin_ref,
                     u_ref, q_ref, k_ref, vt_ref, kc_ref, vc_ref, kv_sc, *, npt, cw, qw, qscale):
    i = pl.program_id(0)
    is_prompt = i < npt
    half = kc_ref.shape[-1] // 4
    for sl in _row_parts(PROJ_ROW_PARTS):
        x = jnp.where(is_prompt, xp_ref[sl, :], xs_ref[sl, :])
        h = _rms_mod(x, g_ref[...], mod_ref[0, 1:2, :], mod_ref[0, 0:1, :])
        proj = jnp.dot(h.astype(BF16), w_ref[...], preferred_element_type=F32)
        ga = proj[:, :cw]
        gb = proj[:, cw:2 * cw]
        u_ref[sl, :] = ga * jax.nn.sigmoid(gb)
        q = proj[:, 2 * cw:2 * cw + qw]
        k = proj[:, 2 * cw + qw:2 * cw + 2 * qw]
        v = proj[:, 2 * cw + 2 * qw:]
        kv_sc[0, sl, :] = k
        kv_sc[1, sl, :] = v
        cos = cos_ref[sl, :]
        sins = sin_ref[sl, :]
        q_ref[sl, :] = (_rope(q, cos, sins, half) * qscale).astype(BF16)
        k_ref[sl, :] = _rope(k, cos, sins, half).astype(BF16)
        vt_ref[:, sl] = v.T.astype(BF16)

    @pl.when(is_prompt)
    def _():
        heads, halves, dh = kc_ref.shape[3:]
        for h in range(heads):
            vc_ref[0, 0, :, h, :] = kv_sc[1, :, h * halves * dh:(h + 1) * halves * dh]
            for c in range(halves):
                lo = (h * halves + c) * dh
                kc_ref[0, 0, :, h, c, :] = kv_sc[0, :, lo:lo + dh]


def _attn_kernel(lam_ref, q_ref, k_ref, vt_ref, *rest, nk, tk, nk_ctx, post_scale):
    if nk_ctx:
        kc_ref, vtc_ref, g_ref, o_ref, s_sc, mc_sc, m_sc, acc_sc = rest
    else:
        g_ref, o_ref, s_sc, mc_sc, m_sc, acc_sc = rest
    q = q_ref[...]
    hw = q.shape[1]
    dh = hw // 2
    lane = lax.broadcasted_iota(jnp.int32, (1, hw), 1)
    zero = jnp.zeros_like(q)
    qs = (jnp.where(lane < dh, q, zero), jnp.where(lane >= dh, q, zero))
    m_sc[...] = jnp.full(m_sc.shape, -jnp.inf, F32)
    acc_sc[...] = jnp.zeros(acc_sc.shape, F32)
    ones_rows = (lax.broadcasted_iota(jnp.int32, (ONES_ROWS, tk), 0) == 0).astype(BF16)

    def scores(kk, slot):
        for c in range(2):
            s = lax.dot_general(kk, qs[c], (((1,), (1,)), ((), ())), preferred_element_type=F32)
            s_sc[slot, c] = s
            mc_sc[slot, c] = jnp.max(s, axis=0, keepdims=True)

    def consume(vt, slot):
        vta = jnp.concatenate([vt, ones_rows], axis=0)
        for c in range(2):
            m_prev = m_sc[c]
            m_new = jnp.maximum(m_prev, mc_sc[slot, c])
            alpha = jnp.exp2(m_prev - m_new)
            p = jnp.exp2(s_sc[slot, c] - m_new).astype(BF16)
            acc_sc[c] = alpha * acc_sc[c] + jnp.dot(vta, p, preferred_element_type=F32)
            m_sc[c] = m_new

    def k_at(i):
        if i < nk_ctx:
            return kc_ref[i * tk:(i + 1) * tk, :]
        return k_ref[(i - nk_ctx) * tk:(i - nk_ctx + 1) * tk, :]

    def vt_at(i):
        if i < nk_ctx:
            return vtc_ref[0, :, i * tk:(i + 1) * tk]
        return vt_ref[:, (i - nk_ctx) * tk:(i - nk_ctx + 1) * tk]

    n_total = nk_ctx + nk
    peel = nk_ctx + (n_total - 1 - nk_ctx) % 2
    scores(k_at(0), 0)
    for i in range(peel):
        scores(k_at(i + 1), (i + 1) % 2)
        consume(vt_at(i), i % 2)

    def body(t, carry):
        for r in range(2):
            j = 2 * t + (peel + r - nk_ctx)
            nxt = pl.multiple_of((j + 1) * tk, tk)
            cur = pl.multiple_of(j * tk, tk)
            scores(k_ref[pl.ds(nxt, tk), :], (peel + r + 1) % 2)
            consume(vt_ref[:, pl.ds(cur, tk)], (peel + r) % 2)
        return carry

    lax.fori_loop(0, (n_total - 1 - peel) // 2, body, 0)
    consume(vt_at(n_total - 1), (n_total - 1) % 2)
    o = (acc_sc[0, :hw, :] / acc_sc[0, hw:hw + 1, :]
         - lam_ref[0] * (acc_sc[1, :hw, :] / acc_sc[1, hw:hw + 1, :]))
    ms = jnp.mean(o * o, axis=0, keepdims=True)
    o = o * lax.rsqrt(ms + EPS) * g_ref[...] * post_scale
    o_ref[...] = o.T.astype(BF16)


def _attn_heads_kernel(lam_ref, q_ref, k_ref, vt_ref, g_ref, o_ref, *scratch, heads, **kw):
    hw = q_ref.shape[1] // heads
    for h in range(heads):
        cols = pl.ds(h * hw, hw)
        _attn_kernel(lam_ref, q_ref.at[:, cols], k_ref.at[:, cols], vt_ref.at[cols, :], g_ref, o_ref.at[:, cols],
                     *scratch, **kw)


def _attention(lam, q, k, vt, g_col, ctx, *, batch, t_len, row0, tq, tk, post_scale, heads_per_step=1):
    hw = q.shape[1] // DIFF_HEADS
    bw = hw * heads_per_step
    nq = t_len // tq
    assert row0 % t_len == 0 and t_len % tk == 0 and (ctx is None or heads_per_step == 1)
    qb0 = row0 // tq
    sb0 = row0 // t_len
    in_specs = [pl.BlockSpec(memory_space=pltpu.SMEM),
                pl.BlockSpec((tq, bw), lambda b, h, i: (qb0 + b * nq + i, h)),
                pl.BlockSpec((t_len, bw), lambda b, h, i: (sb0 + b, h)),
                pl.BlockSpec((bw, t_len), lambda b, h, i: (h, sb0 + b))]
    args = [lam, q, k, vt]
    nk_ctx = 0
    if ctx is not None:
        k_ctx, vt_ctx = ctx
        past = vt_ctx.shape[2]
        assert past % tk == 0
        nk_ctx = past // tk
        in_specs += [pl.BlockSpec((past, hw), lambda b, h, i: (b, h)),
                     pl.BlockSpec((1, hw, past), lambda b, h, i: (b, h, 0))]
        args += [k_ctx, vt_ctx]
    in_specs.append(pl.BlockSpec((hw, 1), lambda b, h, i: (0, 0)))
    args.append(g_col)
    body = functools.partial(_attn_kernel, nk=t_len // tk, tk=tk, nk_ctx=nk_ctx, post_scale=post_scale)
    if heads_per_step > 1:
        body = functools.partial(_attn_heads_kernel, heads=heads_per_step, nk=t_len // tk, tk=tk, nk_ctx=nk_ctx,
                                 post_scale=post_scale)
    return pl.pallas_call(
        body,
        grid=(batch, DIFF_HEADS // heads_per_step, nq),
        in_specs=in_specs,
        out_specs=pl.BlockSpec((tq, bw), lambda b, h, i: (b * nq + i, h)),
        out_shape=jax.ShapeDtypeStruct((batch * t_len, q.shape[1]), BF16),
        scratch_shapes=[pltpu.VMEM((2, 2, tk, tq), F32), pltpu.VMEM((2, 2, 1, tq), F32),
                        pltpu.VMEM((2, 1, tq), F32), pltpu.VMEM((2, hw + ONES_ROWS, tq), F32)],
        compiler_params=_cparams(("parallel", "parallel", "parallel")),
        name="diff_attention",
    )(*args)


def _seq_edges(i, npt, tps):
    r = (i - npt) % tps
    is_prompt = i < npt
    first = jnp.logical_or(is_prompt, r == 0)
    last = jnp.logical_or(is_prompt, r == tps - 1)
    return first, last


def _fill_ext(ext_sc, cur_ref, prev_ref, next_ref, first, last):
    prev = prev_ref[...]
    nxt = next_ref[...]
    ext_sc[0:HALO, :] = jnp.where(first, jnp.zeros_like(prev), prev)
    ext_sc[HALO:HALO + TM, :] = cur_ref[...]
    ext_sc[HALO + TM:, :] = jnp.where(last, jnp.zeros_like(nxt), nxt)


def _route_tile(logits, b_ref, info_ref, cnt_ref):
    lt = logits.T + b_ref[...]
    gl = lt[N_EXPERTS:N_EXPERTS + N_GROUPS]
    ge = jnp.exp(gl - jnp.max(gl, axis=0, keepdims=True))
    gp = ge / jnp.sum(ge, axis=0, keepdims=True)
    best = gp[0:1]
    gidx = jnp.zeros(best.shape, jnp.int32)
    el = lt[0:EXPERTS_PER_GROUP]
    for g in range(1, N_GROUPS):
        better = gp[g:g + 1] > best
        gidx = jnp.where(better, g, gidx)
        best = jnp.where(better, gp[g:g + 1], best)
    for g in range(1, N_GROUPS):
        el = jnp.where(gidx == g, lt[g * EXPERTS_PER_GROUP:(g + 1) * EXPERTS_PER_GROUP], el)
    rio = lax.broadcasted_iota(jnp.int32, el.shape, 0)
    v0 = jnp.max(el, axis=0, keepdims=True)
    i0 = jnp.min(jnp.where(el == v0, rio, EXPERTS_PER_GROUP), axis=0, keepdims=True)
    el2 = jnp.where(rio == i0, -jnp.inf, el)
    v1 = jnp.max(el2, axis=0, keepdims=True)
    i1 = jnp.min(jnp.where(el2 == v1, rio, EXPERTS_PER_GROUP), axis=0, keepdims=True)
    e1 = jnp.exp(v1 - v0)
    den = 1.0 + e1
    w0 = (1.0 / den) * best
    w1 = (e1 / den) * best
    eid0 = gidx * EXPERTS_PER_GROUP + i0
    eid1 = gidx * EXPERTS_PER_GROUP + i1
    tm = lt.shape[1]
    eio = lax.broadcasted_iota(jnp.int32, (N_EXPERTS, tm), 0)
    oh0 = eio == eid0
    oh1 = eio == eid1
    ohs = jnp.where(oh0, 1.0, 0.0) + jnp.where(oh1, 1.0, 0.0)
    upper = (lax.broadcasted_iota(jnp.int32, (tm, tm), 0)
             < lax.broadcasted_iota(jnp.int32, (tm, tm), 1)).astype(BF16)
    before = jnp.dot(ohs.astype(BF16), upper, preferred_element_type=F32)
    rank0 = jnp.sum(jnp.where(oh0, before, 0.0), axis=0, keepdims=True)
    rank1 = jnp.sum(jnp.where(oh1, before, 0.0), axis=0, keepdims=True)
    info_ref[0] = jnp.concatenate([eid0.astype(F32), eid1.astype(F32), w0, w1, rank0, rank1,
                                   jnp.zeros((INFO_ROWS - 6, tm), F32)], axis=0)
    cnt_ref[0] = jnp.sum(ohs, axis=1, keepdims=True)


def _moe_prologue(x_new, sl, mod_ref, g2_ref, wr_ref, rb_ref, h2_ref, info_ref, cnt_ref):
    h2 = _rms_mod(x_new, g2_ref[...], mod_ref[0, 4:5, :], mod_ref[0, 3:4, :])
    h2_ref[sl, :] = h2
    assert sl.stop - sl.start == TM
    _route_tile(jnp.dot(h2.astype(BF16), wr_ref[...], preferred_element_type=F32), rb_ref, info_ref, cnt_ref)


def _post_even_kernel(xp_ref, xs_ref, u_ref, up_ref, un_ref, op_ref, os_ref, mod_ref, cw_ref, cb_ref,
                      lng_ref, lnb_ref, wo_ref, g2_ref, wr_ref, rb_ref, x1_ref, h2_ref, info_ref, cnt_ref,
                      ext_sc, sh_sc, *, npt, tps):
    i = pl.program_id(0)
    first, last = _seq_edges(i, npt, tps)
    _fill_ext(ext_sc, u_ref, up_ref, un_ref, first, last)
    cw = u_ref.shape[1]
    span = sh_sc.shape[1]
    for r in range(SUBLANES):
        sh_sc[r] = ext_sc[r:r + span, :]
    rows = CONV_ROWS
    is_prompt = i < npt
    for sl in _row_parts(MIX_ROW_PARTS):
        chunks = []
        for r0 in range(sl.start, sl.stop, rows):
            acc = None
            for k in range(CONV_K):
                off = HALO - CONV_K // 2 + k
                start = r0 + off - off % SUBLANES
                term = sh_sc[off % SUBLANES, start:start + rows, :] * cw_ref[k:k + 1, :]
                acc = term if acc is None else acc + term
            chunks.append(acc)
        conv = jnp.concatenate(chunks, axis=0) + cb_ref[...]
        mu = jnp.mean(conv, axis=-1, keepdims=True)
        cen = conv - mu
        var = jnp.mean(cen * cen, axis=-1, keepdims=True)
        uu = _silu(cen * lax.rsqrt(var + EPS) * lng_ref[...] + lnb_ref[...])
        o = jnp.where(is_prompt, op_ref[sl, :], os_ref[sl, :])
        m = (jnp.dot(uu.astype(BF16), wo_ref[0:cw, :], preferred_element_type=F32)
             + jnp.dot(o, wo_ref[cw:, :], preferred_element_type=F32))
        x1 = jnp.where(is_prompt, xp_ref[sl, :], xs_ref[sl, :]) + mod_ref[0, 2:3, :] * m
        x1_ref[sl, :] = x1
        _moe_prologue(x1, sl, mod_ref, g2_ref, wr_ref, rb_ref, h2_ref, info_ref, cnt_ref)


def _gather_start(dest_ref, yb_hbm, ybuf, sem, slot):
    def body(i, carry):
        for j in range(SUBLANES):
            for k in range(TOP_K):
                row = dest_ref[0, 0, k * TM + i * SUBLANES + j]
                pltpu.make_async_copy(yb_hbm.at[pl.ds(row, 1)], ybuf.at[slot, k, i, pl.ds(j, 1)],
                                      sem.at[slot]).start(priority=k % 2)
        return carry

    lax.fori_loop(0, TM // SUBLANES, body, 0)


def _moe_gather(dcur_ref, dnxt_ref, yb_hbm, ybuf, sem):
    t = pl.program_id(0)
    slot = t % 2

    @pl.when(t == 0)
    def _():
        _gather_start(dcur_ref, yb_hbm, ybuf, sem, 0)

    _gather_wait(ybuf, sem, slot)
    return slot


def _gather_wait(ybuf, sem, slot):
    for k in range(TOP_K):
        pltpu.make_async_copy(ybuf.at[slot, k], ybuf.at[slot, k], sem.at[slot]).wait()


def _moe_prefetch_group(dnxt_ref, yb_hbm, ybuf, sem, slot, g, n_groups):
    n_i = TM // SUBLANES
    for i in range(g * n_i // n_groups, (g + 1) * n_i // n_groups):
        for j in range(SUBLANES):
            for k in range(TOP_K):
                row = dnxt_ref[0, 0, k * TM + i * SUBLANES + j]
                pltpu.make_async_copy(yb_hbm.at[pl.ds(row, 1)], ybuf.at[1 - slot, k, i, pl.ds(j, 1)],
                                      sem.at[1 - slot]).start(priority=k % 2)


def _moe_prefetch_finish(ybuf, sem, slot):
    @pl.when(pl.program_id(0) == pl.num_programs(0) - 1)
    def _():
        _gather_wait(ybuf, sem, 1 - slot)


def _combined_rows(cw_ref, ybuf, slot, sl):
    d = ybuf.shape[-1]
    g = slice(sl.start // SUBLANES, sl.stop // SUBLANES)
    w = cw_ref[sl, :]
    y = w[:, 0:1] * ybuf[slot, 0, g].reshape(sl.stop - sl.start, d)
    for k in range(1, TOP_K):
        y = y + w[:, k:k + 1] * ybuf[slot, k, g].reshape(sl.stop - sl.start, d)
    return y


def _dispatch_kernel(dest_ref, pend_ref, h2_hbm, xb_hbm, stage, zeros, in_sem, row_sem, zero_sem):
    t = pl.program_id(0)
    nt = pl.num_programs(0)
    slot = t % DISPATCH_SLOTS

    tile_rows = TM // SUBLANES

    def stage_in(tile_idx, s):
        return pltpu.make_async_copy(h2_hbm.at[pl.ds(tile_idx * tile_rows, tile_rows)], stage.at[s], in_sem.at[s])

    def drain_rows(s):
        for _ in range(TOP_K):
            pltpu.make_async_copy(stage.at[s], stage.at[s], row_sem.at[s]).wait()

    def zero_block(start):
        return pltpu.make_async_copy(zeros, xb_hbm.at[pl.ds(pl.multiple_of(start, MOE_BLK), MOE_BLK)], zero_sem)

    def zero_fill(op):
        for e in range(N_EXPERTS):
            op(zero_block(jnp.maximum(pend_ref[e] - MOE_BLK, 0)))
        for j in range(N_EXPERTS):
            start = pend_ref[N_EXPERTS - 1] + j * MOE_BLK

            @pl.when(start < xb_hbm.shape[0])
            def _():
                op(zero_block(start))

    @pl.when(t == 0)
    def _():
        stage_in(0, 0).start()
        zeros[...] = jnp.zeros(zeros.shape, zeros.dtype)
        zero_fill(lambda cp: cp.start())
        zero_fill(lambda cp: cp.wait())

    @pl.when(jnp.logical_and(t == 0, nt > 1))
    def _():
        stage_in(1, 1).start()

    stage_in(t, slot).wait()

    def body(i, carry):
        for j in range(SUBLANES):
            for k in range(TOP_K):
                row = dest_ref[0, 0, k * TM + i * SUBLANES + j]
                pltpu.make_async_copy(stage.at[slot, i, pl.ds(j, 1)], xb_hbm.at[pl.ds(row, 1)],
                                      row_sem.at[slot]).start(priority=k % 2)
        return carry

    lax.fori_loop(0, tile_rows, body, 0)

    @pl.when(t > 0)
    def _():
        drain_rows((t + DISPATCH_SLOTS - 1) % DISPATCH_SLOTS)

    @pl.when(t + 2 < nt)
    def _():
        stage_in(t + 2, (t + 2) % DISPATCH_SLOTS).start()

    @pl.when(t == nt - 1)
    def _():
        drain_rows(slot)


def _dispatch(dest3, pend, h2, p_len):
    n, d = h2.shape
    return pl.pallas_call(
        _dispatch_kernel,
        grid=(n // TM,),
        in_specs=[pl.BlockSpec((1, 1, TOP_K * TM), lambda t: (t, 0, 0), memory_space=pltpu.SMEM),
                  pl.BlockSpec(memory_space=pltpu.SMEM), pl.BlockSpec(memory_space=pl.ANY)],
        out_specs=pl.BlockSpec(memory_space=pl.ANY),
        out_shape=jax.ShapeDtypeStruct((p_len, d), h2.dtype),
        scratch_shapes=[pltpu.VMEM((DISPATCH_SLOTS, TM // SUBLANES, SUBLANES, d), h2.dtype),
                        pltpu.VMEM((MOE_BLK, d), h2.dtype),
                        pltpu.SemaphoreType.DMA((DISPATCH_SLOTS,)), pltpu.SemaphoreType.DMA((DISPATCH_SLOTS,)),
                        pltpu.SemaphoreType.DMA(())],
        compiler_params=_cparams(("arbitrary",)),
        name="moe_dispatch",
    )(dest3, pend, h2.reshape(n // SUBLANES, SUBLANES, d))


def _pre_odd_kernel(x_ref, dcur_ref, dnxt_ref, cw_ref, yb_hbm, modp_ref, mod_ref, g_ref, w_ref,
                    x2_ref, xc_ref, p_ref, bg_ref, ybuf, sem, *, cw):
    slot = _moe_gather(dcur_ref, dnxt_ref, yb_hbm, ybuf, sem)
    n_groups = PROJ_ROW_PARTS * 5
    group = 0
    for sl in _row_parts(PROJ_ROW_PARTS):
        x2 = x_ref[sl, :] + modp_ref[0, 5:6, :] * _combined_rows(cw_ref, ybuf, slot, sl)
        x2_ref[sl, :] = x2
        _moe_prefetch_group(dnxt_ref, yb_hbm, ybuf, sem, slot, group, n_groups)
        group += 1
        h = _rms_mod(x2, g_ref[...], mod_ref[0, 1:2, :], mod_ref[0, 0:1, :]).astype(BF16)
        hd = None
        for c, out_ref in enumerate((xc_ref, None, bg_ref, p_ref)):
            blk = jnp.dot(h, w_ref[:, c * cw:(c + 1) * cw], preferred_element_type=F32)
            if out_ref is None:
                hd = blk
            elif out_ref is p_ref:
                p_ref[sl, :] = blk * hd
            else:
                out_ref[sl, :] = blk
            _moe_prefetch_group(dnxt_ref, yb_hbm, ybuf, sem, slot, group, n_groups)
            group += 1
    _moe_prefetch_finish(ybuf, sem, slot)


def _post_odd_kernel(x_ref, xc_ref, xcp_ref, xcn_ref, p_ref, pp_ref, pn_ref, bg_ref, mod_ref,
                     pw_ref, ps_ref, dw_ref, wo_ref, g2_ref, wr_ref, rb_ref,
                     x3_ref, h2_ref, info_ref, cnt_ref, extc_sc, extp_sc, *, npt, tps):
    i = pl.program_id(0)
    first, last = _seq_edges(i, npt, tps)
    _fill_ext(extc_sc, xc_ref, xcp_ref, xcn_ref, first, last)
    _fill_ext(extp_sc, p_ref, pp_ref, pn_ref, first, last)
    cw = xc_ref.shape[1]
    gc = cw // len(POOL_WINDOWS)
    is_prompt = i < npt
    t_len = jnp.where(is_prompt, TM, TM * tps)
    tile_pos = jnp.where(is_prompt, 0, ((i - npt) % tps) * TM)
    for rs in _row_parts(MIX_ROW_PARTS):
        nrows = rs.stop - rs.start
        base = HALO + rs.start
        pos = lax.broadcasted_iota(jnp.int32, (nrows, 1), 0) + (tile_pos + rs.start)
        ds = []
        for g, w in enumerate(POOL_WINDOWS):
            sl = slice(g * gc, (g + 1) * gc)
            acc = None
            for o in range(-(w // 2), w // 2):
                term = extc_sc[base + o:base + o + nrows, sl]
                acc = term if acc is None else acc + term
            lo = jnp.maximum(pos - w // 2, 0)
            hi = jnp.minimum(pos + w // 2 - 1, t_len - 1)
            cnt = (hi - lo + 1).astype(F32)
            ds.append(acc / cnt - extc_sc[base:base + nrows, sl])
        d = jnp.concatenate(ds, axis=1).astype(BF16)
        yc = jnp.dot(d, pw_ref[...], preferred_element_type=F32) * ps_ref[...]
        conv = None
        for k in range(SCONV_K):
            start = base - SCONV_K // 2 + k
            term = extp_sc[start:start + nrows, :] * dw_ref[k:k + 1, :]
            conv = term if conv is None else conv + term
        yd = bg_ref[rs, :] * conv
        m = (jnp.dot(yc.astype(BF16), wo_ref[0:cw, :], preferred_element_type=F32)
             + jnp.dot(yd.astype(BF16), wo_ref[cw:, :], preferred_element_type=F32))
        x3 = x_ref[rs, :] + mod_ref[0, 2:3, :] * m
        x3_ref[rs, :] = x3
        _moe_prologue(x3, rs, mod_ref, g2_ref, wr_ref, rb_ref, h2_ref, info_ref, cnt_ref)


def _final_kernel(x_ref, dcur_ref, dnxt_ref, cw_ref, yb_hbm, modp_ref, g_ref, op_ref, os_ref, ybuf, sem, *, npt):
    i = pl.program_id(0)
    slot = _moe_gather(dcur_ref, dnxt_ref, yb_hbm, ybuf, sem)
    outs = []
    parts = _row_parts(FINAL_ROW_PARTS)
    for g, sl in enumerate(parts):
        x = x_ref[sl, :] + modp_ref[0, 5:6, :] * _combined_rows(cw_ref, ybuf, slot, sl)
        ms = jnp.mean(x * x, axis=-1, keepdims=True)
        outs.append(x * lax.rsqrt(ms + EPS) * g_ref[...])
        _moe_prefetch_group(dnxt_ref, yb_hbm, ybuf, sem, slot, g, len(parts))
    out = jnp.concatenate(outs, axis=0)
    _moe_prefetch_finish(ybuf, sem, slot)

    @pl.when(i < npt)
    def _():
        op_ref[...] = out

    @pl.when(i >= npt)
    def _():
        os_ref[...] = out


def _expert_kernel(be_ref, nv_ref, x_ref, wg_ref, wu_ref, wd_ref, y_ref, wg_sc, wu_sc, wd_sc):
    i = pl.program_id(0)

    @pl.when(jnp.logical_or(i == 0, be_ref[i] != be_ref[jnp.maximum(i - 1, 0)]))
    def _():
        wg_sc[...] = wg_ref[0, 0].astype(BF16)
        wu_sc[...] = wu_ref[0, 0].astype(BF16)
        wd_sc[...] = wd_ref[0, 0].astype(BF16)

    @pl.when(i < nv_ref[0])
    def _():
        x = x_ref[...].astype(BF16)
        hg = jnp.dot(x, wg_sc[...], preferred_element_type=F32)
        hu = jnp.dot(x, wu_sc[...], preferred_element_type=F32)
        hid = (_silu(hg) * hu).astype(BF16)
        y_ref[...] = jnp.dot(hid, wd_sc[...], preferred_element_type=F32)

    @pl.when(i >= nv_ref[0])
    def _():
        y_ref[...] = jnp.zeros(y_ref.shape, F32)


def _experts(blk_e, nvalid, xb, layer, wg, wu, wd):
    p_len, d = xb.shape
    hid = wg.shape[3]
    nblk = p_len // MOE_BLK
    grid_spec = pltpu.PrefetchScalarGridSpec(
        num_scalar_prefetch=2,
        grid=(nblk,),
        in_specs=[pl.BlockSpec((MOE_BLK, d), lambda i, be, nv: (jnp.minimum(i, nv[0] - 1), 0)),
                  pl.BlockSpec((1, 1, d, hid), lambda i, be, nv: (layer, be[i], 0, 0)),
                  pl.BlockSpec((1, 1, d, hid), lambda i, be, nv: (layer, be[i], 0, 0)),
                  pl.BlockSpec((1, 1, hid, d), lambda i, be, nv: (layer, be[i], 0, 0))],
        out_specs=pl.BlockSpec((MOE_BLK, d), lambda i, be, nv: (i, 0)),
        scratch_shapes=[pltpu.VMEM((d, hid), BF16), pltpu.VMEM((d, hid), BF16), pltpu.VMEM((hid, d), BF16)],
    )
    return pl.pallas_call(
        _expert_kernel,
        grid_spec=grid_spec,
        out_shape=jax.ShapeDtypeStruct((p_len, d), F32),
        compiler_params=_cparams(("arbitrary",)),
        name="moe_experts",
    )(blk_e, nvalid, xb, wg, wu, wd)


def _moe(h2, info, cnt, layer, wg, wu, wd):
    n, d = h2.shape
    cnt = cnt[:, :, 0].astype(jnp.int32)
    counts = jnp.sum(cnt, axis=0)
    pc = ((counts + MOE_BLK - 1) // MOE_BLK) * MOE_BLK
    pend = jnp.cumsum(pc)
    base = (pend - pc)[None, :] + jnp.cumsum(cnt, axis=0) - cnt
    eid = info[:, 0:2, :].astype(jnp.int32)
    rank = info[:, 4:6, :].astype(jnp.int32)
    sel = eid[..., None] == jnp.arange(N_EXPERTS, dtype=jnp.int32)
    dest = jnp.sum(jnp.where(sel, base[:, None, None, :], 0), axis=-1) + rank
    p_len = n * TOP_K + N_EXPERTS * MOE_BLK
    nblk = p_len // MOE_BLK
    blk_start = jnp.arange(nblk, dtype=jnp.int32) * MOE_BLK
    blk_e = jnp.minimum(jnp.sum(pend[None, :] <= blk_start[:, None], axis=1), N_EXPERTS - 1).astype(jnp.int32)
    nvalid = (pend[-1:] // MOE_BLK).astype(jnp.int32)
    dest3 = dest.reshape(dest.shape[0], 1, TOP_K * TM)
    wcol = jnp.transpose(info[:, 2:2 + TOP_K, :], (0, 2, 1)).reshape(n, TOP_K)
    xb = _dispatch(dest3, pend.astype(jnp.int32), h2, p_len)
    yb = _experts(blk_e, nvalid, xb, layer, wg, wu, wd)
    return dest3, wcol, yb


def _rope_tables(t_len, qw):
    rows = t_len // GRID_W
    ax = qw // (DIFF_HEADS * 2) // 2
    row = jnp.repeat(jnp.arange(rows, dtype=F32), GRID_W)
    col = jnp.tile(jnp.arange(GRID_W, dtype=F32), rows)
    inv = ROPE_BASE ** (-jnp.arange(0, ax, 2, dtype=F32) / ax)
    ar = row[:, None] * inv[None, :]
    ac = col[:, None] * inv[None, :]
    cr, sr, cc, sc = jnp.cos(ar), jnp.sin(ar), jnp.cos(ac), jnp.sin(ac)
    cos = jnp.concatenate([cr, cr, cc, cc], axis=1)
    sins = jnp.concatenate([-sr, sr, -sc, sc], axis=1)
    reps = qw // cos.shape[1]
    cos = jnp.concatenate([jnp.ones((TM, cos.shape[1]), F32), cos], axis=0)
    sins = jnp.concatenate([jnp.zeros((TM, sins.shape[1]), F32), sins], axis=0)
    return jnp.tile(cos, (1, reps)), jnp.tile(sins, (1, reps))


def kernel(x_prompt, x_sample, cache_k, cache_v, c, c_ctx, w_ada, b_ada, g_norm1, g_norm2, g_final, w_in_e, conv_a_w, conv_a_b, ln_a_g, ln_a_b, lam_q1, lam_k1, lam_q2, lam_k2, subln_g, w_out_e, w_in_o, pool_w, pool_scale, conv_d_w, w_out_o, w_router_g, b_router_g, w_router_e, b_router_e, w_gate, w_up, w_down):
    bp, tp, d = x_prompt.shape
    bs, ts, _ = x_sample.shape
    past = cache_k.shape[2]
    assert tp == TM and ts % TM == 0 and d % LANES == 0
    npt = bp
    tps = ts // TM
    n_p, n_s = bp * tp, bs * ts
    n = n_p + n_s
    nt = n // TM
    cw = conv_a_w.shape[2]
    qw = (w_in_e.shape[2] - 2 * cw) // 3
    dh = qw // (DIFF_HEADS * 2)
    hpt = TM // HALO

    def cond_of(i):
        return jnp.where(i < npt, 0, 1 + (i - npt) // tps)

    def tile(w):
        return pl.BlockSpec((TM, w), lambda i: (i, 0))

    def halo_prev(w):
        return pl.BlockSpec((HALO, w), lambda i: (jnp.maximum(i * hpt - 1, 0), 0))

    def halo_next(w):
        return pl.BlockSpec((HALO, w), lambda i: (jnp.minimum((i + 1) * hpt, nt * hpt - 1), 0))

    def whole(shape):
        return pl.BlockSpec(shape, lambda i: (0,) * len(shape))

    mod_spec = pl.BlockSpec((1, 6, d), lambda i: (cond_of(i), 0, 0))
    combine_specs = [pl.BlockSpec((1, 1, TOP_K * TM), lambda i: (i, 0, 0), memory_space=pltpu.SMEM),
                     pl.BlockSpec((1, 1, TOP_K * TM), lambda i: (jnp.minimum(i + 1, nt - 1), 0, 0),
                                  memory_space=pltpu.SMEM),
                     tile(TOP_K), pl.BlockSpec(memory_space=pl.ANY)]
    combine_scratch = [pltpu.VMEM((2, TOP_K, TM // SUBLANES, SUBLANES, d), F32), pltpu.SemaphoreType.DMA((2,))]

    rows = SUBLANES * pl.cdiv(1 + bs, SUBLANES)
    cond = jnp.concatenate([c_ctx[None, :], c, jnp.zeros((rows - 1 - bs, d), F32)], axis=0)
    mod = _ada_table(cond, w_ada, b_ada)

    xp2, xs2 = x_prompt.reshape(n_p, d), x_sample.reshape(n_s, d)

    def prompt_tile(w):
        return pl.BlockSpec((TM, w), lambda i: (jnp.minimum(i, npt - 1), 0))

    def sample_tile(w):
        return pl.BlockSpec((TM, w), lambda i: (jnp.maximum(i - npt, 0), 0))

    def router_w(i):
        wre = jnp.transpose(w_router_e[i], (1, 0, 2)).reshape(d, N_EXPERTS)
        wr = jnp.concatenate([wre, w_router_g[i]], axis=1)
        return jnp.pad(wr, ((0, 0), (0, ROUTER_PAD - wr.shape[1]))).astype(BF16)

    def router_b(i):
        b = jnp.concatenate([b_router_e[i].reshape(N_EXPERTS), b_router_g[i]])
        return jnp.pad(b, (0, ROUTER_PAD - b.shape[0]))[:, None]

    cos, sins = _rope_tables(ts, qw)
    tab_spec = pl.BlockSpec((TM, qw), lambda i: (jnp.where(i < npt, 0, 1 + (i - npt) % tps), 0))
    kc_spec = pl.BlockSpec((1, 1, TM, DIFF_HEADS, 2, dh), lambda i: (jnp.minimum(i, npt - 1), 0, 0, 0, 0, 0))
    vc_spec = pl.BlockSpec((1, 1, TM, DIFF_HEADS, 2 * dh), lambda i: (jnp.minimum(i, npt - 1), 0, 0, 0, 0))
    u, q, k, vt, new_k, new_v = pl.pallas_call(
        functools.partial(_pre_even_kernel, npt=npt, cw=cw, qw=qw, qscale=dh ** -0.5 * math.log2(math.e)),
        grid=(nt,),
        in_specs=[prompt_tile(d), sample_tile(d), mod_spec, whole((1, d)), whole(w_in_e.shape[1:]),
                  tab_spec, tab_spec],
        out_specs=[tile(cw), tile(qw), tile(qw), pl.BlockSpec((qw, TM), lambda i: (0, i)),
                   kc_spec, vc_spec],
        out_shape=[jax.ShapeDtypeStruct((n, cw), F32), jax.ShapeDtypeStruct((n, qw), BF16),
                   jax.ShapeDtypeStruct((n, qw), BF16), jax.ShapeDtypeStruct((qw, n), BF16),
                   jax.ShapeDtypeStruct((bp, 1, tp, DIFF_HEADS, 2, dh), F32),
                   jax.ShapeDtypeStruct((bp, 1, tp, DIFF_HEADS, 2 * dh), F32)],
        scratch_shapes=[pltpu.VMEM((2, TM, qw), F32)],
        compiler_params=_cparams(("arbitrary",)),
        name="pre_even",
    )(xp2, xs2, mod[0], g_norm1[0][None, :], w_in_e[0].astype(BF16), cos, sins)

    lam_init = 0.8 - 0.6 * math.exp(-0.3 * 0)
    lam = (jnp.exp(jnp.sum(lam_q1[0] * lam_k1[0])) - jnp.exp(jnp.sum(lam_q2[0] * lam_k2[0])) + lam_init)
    lam = lam.reshape(1).astype(F32)
    sub_g = subln_g[0][:, None]
    o_p = _attention(lam, q, k, vt, sub_g, None, batch=bp, t_len=tp, row0=0, tq=TM, tk=TM,
                     post_scale=1.0 - lam_init, heads_per_step=DIFF_HEADS)
    k_ctx = cache_k[:, 0].reshape(bs * past, qw).astype(BF16)
    vt_ctx = jnp.transpose(cache_v[:, 0].reshape(bs, past, qw), (0, 2, 1)).astype(BF16)
    o_s = _attention(lam, q, k, vt, sub_g, (k_ctx, vt_ctx), batch=bs, t_len=ts, row0=n_p, tq=ATTN_TQ, tk=ATTN_TK,
                     post_scale=1.0 - lam_init)

    moe_outs = [jax.ShapeDtypeStruct((n, d), F32), jax.ShapeDtypeStruct((n, d), F32),
                jax.ShapeDtypeStruct((nt, INFO_ROWS, TM), F32), jax.ShapeDtypeStruct((nt, N_EXPERTS, 1), F32)]
    moe_out_specs = [tile(d), tile(d), pl.BlockSpec((1, INFO_ROWS, TM), lambda i: (i, 0, 0)),
                     pl.BlockSpec((1, N_EXPERTS, 1), lambda i: (i, 0, 0))]
    x1, h2, info, cnt = pl.pallas_call(
        functools.partial(_post_even_kernel, npt=npt, tps=tps),
        grid=(nt,),
        in_specs=[prompt_tile(d), sample_tile(d), tile(cw), halo_prev(cw), halo_next(cw),
                  prompt_tile(qw), sample_tile(qw), mod_spec,
                  whole((CONV_K, cw)), whole((1, cw)), whole((1, cw)), whole((1, cw)),
                  whole(w_out_e.shape[1:]), whole((1, d)), whole((d, ROUTER_PAD)), whole((ROUTER_PAD, 1))],
        out_specs=moe_out_specs,
        out_shape=moe_outs,
        scratch_shapes=[pltpu.VMEM((TM + 2 * HALO, cw), F32),
                        pltpu.VMEM((SUBLANES, TM + 2 * HALO - SUBLANES, cw), F32)],
        compiler_params=_cparams(("parallel",)),
        name="post_even",
    )(xp2, xs2, u, u, u, o_p, o_s, mod[0], conv_a_w[0], conv_a_b[0][None, :], ln_a_g[0][None, :], ln_a_b[0][None, :],
      w_out_e[0].astype(BF16), g_norm2[0][None, :], router_w(0), router_b(0))
    dest3, wcol, yb = _moe(h2, info, cnt, 0, w_gate, w_up, w_down)

    pw = pool_w.shape[2]
    x2, xc, p, bg = pl.pallas_call(
        functools.partial(_pre_odd_kernel, cw=cw),
        grid=(nt,),
        in_specs=[tile(d)] + combine_specs + [mod_spec, mod_spec, whole((1, d)), whole(w_in_o.shape[1:])],
        out_specs=[tile(d), tile(cw), tile(cw), tile(cw)],
        out_shape=[jax.ShapeDtypeStruct((n, d), F32)] + [jax.ShapeDtypeStruct((n, cw), F32)] * 3,
        scratch_shapes=combine_scratch,
        compiler_params=_cparams(("arbitrary",)),
        name="pre_odd",
    )(x1, dest3, dest3, wcol, yb, mod[0], mod[1], g_norm1[1][None, :], w_in_o[0].astype(BF16))
    pool_bd = jnp.zeros((cw, cw), F32)
    for g in range(len(POOL_WINDOWS)):
        pool_bd = pool_bd.at[g * pw:(g + 1) * pw, g * pw:(g + 1) * pw].set(pool_w[0, g])
    x3, h2, info, cnt = pl.pallas_call(
        functools.partial(_post_odd_kernel, npt=npt, tps=tps),
        grid=(nt,),
        in_specs=[tile(d), tile(cw), halo_prev(cw), halo_next(cw), tile(cw), halo_prev(cw), halo_next(cw),
                  tile(cw), mod_spec, whole((cw, cw)), whole((1, cw)), whole((SCONV_K, cw)),
                  whole(w_out_o.shape[1:]), whole((1, d)), whole((d, ROUTER_PAD)), whole((ROUTER_PAD, 1))],
        out_specs=moe_out_specs,
        out_shape=moe_outs,
        scratch_shapes=[pltpu.VMEM((TM + 2 * HALO, cw), F32), pltpu.VMEM((TM + 2 * HALO, cw), F32)],
        compiler_params=_cparams(("parallel",)),
        name="post_odd",
    )(x2, xc, xc, xc, p, p, p, bg, mod[1], pool_bd.astype(BF16), pool_scale[0][None, :], conv_d_w[0],
      w_out_o[0].astype(BF16), g_norm2[1][None, :], router_w(1), router_b(1))
    dest3, wcol, yb = _moe(h2, info, cnt, 1, w_gate, w_up, w_down)

    out_p, out_s = pl.pallas_call(
        functools.partial(_final_kernel, npt=npt),
        grid=(nt,),
        in_specs=[tile(d)] + combine_specs + [mod_spec, whole((1, d))],
        out_specs=[pl.BlockSpec((TM, d), lambda i: (jnp.minimum(i, npt - 1), 0)),
                   pl.BlockSpec((TM, d), lambda i: (jnp.maximum(i - npt, 0), 0))],
        out_shape=[jax.ShapeDtypeStruct((n_p, d), F32), jax.ShapeDtypeStruct((n_s, d), F32)],
        scratch_shapes=combine_scratch,
        compiler_params=_cparams(("arbitrary",)),
        name="final_norm",
    )(x3, dest3, dest3, wcol, yb, mod[1], g_final[None, :])

    return (out_p.reshape(bp, tp, d), out_s.reshape(bs, ts, d), new_k, new_v)
```

```python
import functools
import math

import jax
import jax.numpy as jnp
from jax import lax
from jax.experimental import pallas as pl
from jax.experimental.pallas import tpu as pltpu

F32 = jnp.float32
BF16 = jnp.bfloat16

EPS = 1e-6
GRID_W = 64
ROPE_BASE = 10000.0
CONV_K = 31
SCONV_K = 3
POOL_WINDOWS = (2, 4, 8, 16)
DIFF_HEADS = 4
N_GROUPS = 4
EXPERTS_PER_GROUP = 8
N_EXPERTS = N_GROUPS * EXPERTS_PER_GROUP
TOP_K = 2

LANES = 128
SUBLANES = 8
TM = 256
HALO = 16
PROJ_ROW_PARTS = 2
MIX_ROW_PARTS = 1
PROJ_COLS = 256
MOE_BLK = 512
ATTN_TQ = 2048
ATTN_TK = 512
CONV_ROWS = 32
INFO_ROWS = 8
ROUTER_PAD = LANES
ONES_ROWS = 2 * SUBLANES
DISPATCH_SLOTS = 3
VMEM_LIMIT = 56 * 1024 * 1024


def _cparams(sem):
    return pltpu.CompilerParams(dimension_semantics=sem, vmem_limit_bytes=VMEM_LIMIT)


def _rms_mod(x, g, sc, sh):
    ms = jnp.mean(x * x, axis=-1, keepdims=True)
    return (x * lax.rsqrt(ms + EPS)) * g * (1.0 + sc) + sh


def _silu(x):
    return x * jax.nn.sigmoid(x)


def _ada_kernel(c_ref, w_ref, b_ref, o_ref):
    cs = _silu(c_ref[...])
    o_ref[0] = jnp.dot(cs, w_ref[0], precision=lax.Precision.HIGHEST,
                       preferred_element_type=F32) + b_ref[0]


def _ada_table(cond, w_ada, b_ada):
    depth, d, six_d = w_ada.shape
    rows = cond.shape[0]
    nj = six_d // d
    out = pl.pallas_call(
        _ada_kernel,
        grid=(depth, nj),
        in_specs=[pl.BlockSpec((rows, d), lambda l, j: (0, 0)),
                  pl.BlockSpec((1, d, d), lambda l, j: (l, 0, j)),
                  pl.BlockSpec((1, 1, d), lambda l, j: (l, 0, j))],
        out_specs=pl.BlockSpec((1, rows, d), lambda l, j: (l, 0, j)),
        out_shape=jax.ShapeDtypeStruct((depth, rows, six_d), F32),
        compiler_params=_cparams(("parallel", "parallel")),
        name="ada_table",
    )(cond, w_ada, b_ada.reshape(depth, 1, six_d))
    return out.reshape(depth, rows, nj, d)


def _row_parts(parts):
    rows = TM // parts
    return [slice(p * rows, (p + 1) * rows) for p in range(parts)]


def _rope(x, cos, sins, half):
    lane = lax.broadcasted_iota(jnp.int32, (1, LANES), 1)
    first_half = (lane % (2 * half)) < half
    outs = []
    for j in range(x.shape[1] // LANES):
        sl = slice(j * LANES, (j + 1) * LANES)
        xs = x[:, sl]
        nxt = pltpu.roll(xs, LANES - half, 1)
        prv = pltpu.roll(xs, half, 1)
        rot = jnp.where(first_half, nxt, prv)
        outs.append(xs * cos[:, sl] + rot * sins[:, sl])
    return jnp.concatenate(outs, axis=1)


def _pre_even_kernel(xp_ref, xs_ref, mod_ref, g_ref, w_ref, cos_ref, sin_ref,
                     u_ref, q_ref, k_ref, vt_ref, kc_ref, vc_ref, kv_sc, *, npt, cw, qw, qscale):
    i = pl.program_id(0)
    is_prompt = i < npt
    half = kc_ref.shape[-1] // 4
    for sl in _row_parts(PROJ_ROW_PARTS):
        x = jnp.where(is_prompt, xp_ref[sl, :], xs_ref[sl, :])
        h = _rms_mod(x, g_ref[...], mod_ref[0, 1:2, :], mod_ref[0, 0:1, :])
        proj = jnp.dot(h.astype(BF16), w_ref[...], preferred_element_type=F32)
        ga = proj[:, :cw]
        gb = proj[:, cw:2 * cw]
        u_ref[sl, :] = ga * jax.nn.sigmoid(gb)
        q = proj[:, 2 * cw:2 * cw + qw]
        k = proj[:, 2 * cw + qw:2 * cw + 2 * qw]
        v = proj[:, 2 * cw + 2 * qw:]
        kv_sc[0, sl, :] = k
        kv_sc[1, sl, :] = v
        cos = cos_ref[sl, :]
        sins = sin_ref[sl, :]
        q_ref[sl, :] = (_rope(q, cos, sins, half) * qscale).astype(BF16)
        k_ref[sl, :] = _rope(k, cos, sins, half).astype(BF16)
        vt_ref[:, sl] = v.T.astype(BF16)

    @pl.when(is_prompt)
    def _():
        heads, halves, dh = kc_ref.shape[3:]
        for h in range(heads):
            vc_ref[0, 0, :, h, :] = kv_sc[1, :, h * halves * dh:(h + 1) * halves * dh]
            for c in range(halves):
                lo = (h * halves + c) * dh
                kc_ref[0, 0, :, h, c, :] = kv_sc[0, :, lo:lo + dh]


def _attn_kernel(lam_ref, q_ref, k_ref, vt_ref, *rest, nk, tk, nk_ctx, post_scale):
    if nk_ctx:
        kc_ref, vtc_ref, g_ref, o_ref, s_sc, mc_sc, m_sc, acc_sc = rest
    else:
        g_ref, o_ref, s_sc, mc_sc, m_sc, acc_sc = rest
    q = q_ref[...]
    hw = q.shape[1]
    dh = hw // 2
    lane = lax.broadcasted_iota(jnp.int32, (1, hw), 1)
    zero = jnp.zeros_like(q)
    qs = (jnp.where(lane < dh, q, zero), jnp.where(lane >= dh, q, zero))
    m_sc[...] = jnp.full(m_sc.shape, -jnp.inf, F32)
    acc_sc[...] = jnp.zeros(acc_sc.shape, F32)
    ones_rows = (lax.broadcasted_iota(jnp.int32, (ONES_ROWS, tk), 0) == 0).astype(BF16)

    def scores(kk, slot):
        for c in range(2):
            s = lax.dot_general(kk, qs[c], (((1,), (1,)), ((), ())), preferred_element_type=F32)
            s_sc[slot, c] = s
            mc_sc[slot, c] = jnp.max(s, axis=0, keepdims=True)

    def consume(vt, slot):
        vta = jnp.concatenate([vt, ones_rows], axis=0)
        for c in range(2):
            m_prev = m_sc[c]
            m_new = jnp.maximum(m_prev, mc_sc[slot, c])
            alpha = jnp.exp2(m_prev - m_new)
            p = jnp.exp2(s_sc[slot, c] - m_new).astype(BF16)
            acc_sc[c] = alpha * acc_sc[c] + jnp.dot(vta, p, preferred_element_type=F32)
            m_sc[c] = m_new

    def k_at(i):
        if i < nk_ctx:
            return kc_ref[i * tk:(i + 1) * tk, :]
        return k_ref[(i - nk_ctx) * tk:(i - nk_ctx + 1) * tk, :]

    def vt_at(i):
        if i < nk_ctx:
            return vtc_ref[0, :, i * tk:(i + 1) * tk]
        return vt_ref[:, (i - nk_ctx) * tk:(i - nk_ctx + 1) * tk]

    n_total = nk_ctx + nk
    peel = nk_ctx + (n_total - 1 - nk_ctx) % 2
    scores(k_at(0), 0)
    for i in range(peel):
        scores(k_at(i + 1), (i + 1) % 2)
        consume(vt_at(i), i % 2)

    def body(t, carry):
        for r in range(2):
            j = 2 * t + (peel + r - nk_ctx)
            nxt = pl.multiple_of((j + 1) * tk, tk)
            cur = pl.multiple_of(j * tk, tk)
            scores(k_ref[pl.ds(nxt, tk), :], (peel + r + 1) % 2)
            consume(vt_ref[:, pl.ds(cur, tk)], (peel + r) % 2)
        return carry

    lax.fori_loop(0, (n_total - 1 - peel) // 2, body, 0)
    consume(vt_at(n_total - 1), (n_total - 1) % 2)
    o = (acc_sc[0, :hw, :] / acc_sc[0, hw:hw + 1, :]
         - lam_ref[0] * (acc_sc[1, :hw, :] / acc_sc[1, hw:hw + 1, :]))
    ms = jnp.mean(o * o, axis=0, keepdims=True)
    o = o * lax.rsqrt(ms + EPS) * g_ref[...] * post_scale
    o_ref[...] = o.T.astype(BF16)


def _attn_heads_kernel(lam_ref, q_ref, k_ref, vt_ref, g_ref, o_ref, *scratch, heads, **kw):
    hw = q_ref.shape[1] // heads
    for h in range(heads):
        cols = pl.ds(h * hw, hw)
        _attn_kernel(lam_ref, q_ref.at[:, cols], k_ref.at[:, cols], vt_ref.at[cols, :], g_ref, o_ref.at[:, cols],
                     *scratch, **kw)


def _attention(lam, q, k, vt, g_col, ctx, *, batch, t_len, row0, tq, tk, post_scale, heads_per_step=1):
    hw = q.shape[1] // DIFF_HEADS
    bw = hw * heads_per_step
    nq = t_len // tq
    assert row0 % t_len == 0 and t_len % tk == 0 and (ctx is None or heads_per_step == 1)
    qb0 = row0 // tq
    sb0 = row0 // t_len
    in_specs = [pl.BlockSpec(memory_space=pltpu.SMEM),
                pl.BlockSpec((tq, bw), lambda b, h, i: (qb0 + b * nq + i, h)),
                pl.BlockSpec((t_len, bw), lambda b, h, i: (sb0 + b, h)),
                pl.BlockSpec((bw, t_len), lambda b, h, i: (h, sb0 + b))]
    args = [lam, q, k, vt]
    nk_ctx = 0
    if ctx is not None:
        k_ctx, vt_ctx = ctx
        past = vt_ctx.shape[2]
        assert past % tk == 0
        nk_ctx = past // tk
        in_specs += [pl.BlockSpec((past, hw), lambda b, h, i: (b, h)),
                     pl.BlockSpec((1, hw, past), lambda b, h, i: (b, h, 0))]
        args += [k_ctx, vt_ctx]
    in_specs.append(pl.BlockSpec((hw, 1), lambda b, h, i: (0, 0)))
    args.append(g_col)
    body = functools.partial(_attn_kernel, nk=t_len // tk, tk=tk, nk_ctx=nk_ctx, post_scale=post_scale)
    if heads_per_step > 1:
        body = functools.partial(_attn_heads_kernel, heads=heads_per_step, nk=t_len // tk, tk=tk, nk_ctx=nk_ctx,
                                 post_scale=post_scale)
    return pl.pallas_call(
        body,
        grid=(batch, DIFF_HEADS // heads_per_step, nq),
        in_specs=in_specs,
        out_specs=pl.BlockSpec((tq, bw), lambda b, h, i: (b * nq + i, h)),
        out_shape=jax.ShapeDtypeStruct((batch * t_len, q.shape[1]), BF16),
        scratch_shapes=[pltpu.VMEM((2, 2, tk, tq), F32), pltpu.VMEM((2, 2, 1, tq), F32),
                        pltpu.VMEM((2, 1, tq), F32), pltpu.VMEM((2, hw + ONES_ROWS, tq), F32)],
        compiler_params=_cparams(("parallel", "parallel", "parallel")),
        name="diff_attention",
    )(*args)


def _seq_edges(i, npt, tps):
    r = (i - npt) % tps
    is_prompt = i < npt
    first = jnp.logical_or(is_prompt, r == 0)
    last = jnp.logical_or(is_prompt, r == tps - 1)
    return first, last


def _fill_ext(ext_sc, cur_ref, prev_ref, next_ref, first, last):
    prev = prev_ref[...]
    nxt = next_ref[...]
    ext_sc[0:HALO, :] = jnp.where(first, jnp.zeros_like(prev), prev)
    ext_sc[HALO:HALO + TM, :] = cur_ref[...]
    ext_sc[HALO + TM:, :] = jnp.where(last, jnp.zeros_like(nxt), nxt)


def _route_tile(logits, b_ref, info_ref, cnt_ref):
    lt = logits.T + b_ref[...]
    gl = lt[N_EXPERTS:N_EXPERTS + N_GROUPS]
    ge = jnp.exp(gl - jnp.max(gl, axis=0, keepdims=True))
    gp = ge / jnp.sum(ge, axis=0, keepdims=True)
    best = gp[0:1]
    gidx = jnp.zeros(best.shape, jnp.int32)
    el = lt[0:EXPERTS_PER_GROUP]
    for g in range(1, N_GROUPS):
        better = gp[g:g + 1] > best
        gidx = jnp.where(better, g, gidx)
        best = jnp.where(better, gp[g:g + 1], best)
    for g in range(1, N_GROUPS):
        el = jnp.where(gidx == g, lt[g * EXPERTS_PER_GROUP:(g + 1) * EXPERTS_PER_GROUP], el)
    rio = lax.broadcasted_iota(jnp.int32, el.shape, 0)
    v0 = jnp.max(el, axis=0, keepdims=True)
    i0 = jnp.min(jnp.where(el == v0, rio, EXPERTS_PER_GROUP), axis=0, keepdims=True)
    el2 = jnp.where(rio == i0, -jnp.inf, el)
    v1 = jnp.max(el2, axis=0, keepdims=True)
    i1 = jnp.min(jnp.where(el2 == v1, rio, EXPERTS_PER_GROUP), axis=0, keepdims=True)
    e1 = jnp.exp(v1 - v0)
    den = 1.0 + e1
    w0 = (1.0 / den) * best
    w1 = (e1 / den) * best
    eid0 = gidx * EXPERTS_PER_GROUP + i0
    eid1 = gidx * EXPERTS_PER_GROUP + i1
    tm = lt.shape[1]
    eio = lax.broadcasted_iota(jnp.int32, (N_EXPERTS, tm), 0)
    oh0 = eio == eid0
    oh1 = eio == eid1
    ohs = jnp.where(oh0, 1.0, 0.0) + jnp.where(oh1, 1.0, 0.0)
    upper = (lax.broadcasted_iota(jnp.int32, (tm, tm), 0)
             < lax.broadcasted_iota(jnp.int32, (tm, tm), 1)).astype(BF16)
    before = jnp.dot(ohs.astype(BF16), upper, preferred_element_type=F32)
    rank0 = jnp.sum(jnp.where(oh0, before, 0.0), axis=0, keepdims=True)
    rank1 = jnp.sum(jnp.where(oh1, before, 0.0), axis=0, keepdims=True)
    info_ref[0] = jnp.concatenate([eid0.astype(F32), eid1.astype(F32), w0, w1, rank0, rank1,
                                   jnp.zeros((INFO_ROWS - 6, tm), F32)], axis=0)
    cnt_ref[0] = jnp.sum(ohs, axis=1, keepdims=True)


def _moe_prologue(x_new, sl, mod_ref, g2_ref, wr_ref, rb_ref, h2_ref, info_ref, cnt_ref):
    h2 = _rms_mod(x_new, g2_ref[...], mod_ref[0, 4:5, :], mod_ref[0, 3:4, :])
    h2_ref[sl, :] = h2
    assert sl.stop - sl.start == TM
    _route_tile(jnp.dot(h2.astype(BF16), wr_ref[...], preferred_element_type=F32), rb_ref, info_ref, cnt_ref)


def _post_even_kernel(xp_ref, xs_ref, u_ref, up_ref, un_ref, op_ref, os_ref, mod_ref, cw_ref, cb_ref,
                      lng_ref, lnb_ref, wo_ref, g2_ref, wr_ref, rb_ref, x1_ref, h2_ref, info_ref, cnt_ref,
                      ext_sc, sh_sc, *, npt, tps):
    i = pl.program_id(0)
    first, last = _seq_edges(i, npt, tps)
    _fill_ext(ext_sc, u_ref, up_ref, un_ref, first, last)
    cw = u_ref.shape[1]
    span = sh_sc.shape[1]
    for r in range(SUBLANES):
        sh_sc[r] = ext_sc[r:r + span, :]
    rows = CONV_ROWS
    is_prompt = i < npt
    for sl in _row_parts(MIX_ROW_PARTS):
        chunks = []
        for r0 in range(sl.start, sl.stop, rows):
            acc = None
            for k in range(CONV_K):
                off = HALO - CONV_K // 2 + k
                start = r0 + off - off % SUBLANES
                term = sh_sc[off % SUBLANES, start:start + rows, :] * cw_ref[k:k + 1, :]
                acc = term if acc is None else acc + term
            chunks.append(acc)
        conv = jnp.concatenate(chunks, axis=0) + cb_ref[...]
        mu = jnp.mean(conv, axis=-1, keepdims=True)
        cen = conv - mu
        var = jnp.mean(cen * cen, axis=-1, keepdims=True)
        uu = _silu(cen * lax.rsqrt(var + EPS) * lng_ref[...] + lnb_ref[...])
        o = jnp.where(is_prompt, op_ref[sl, :], os_ref[sl, :])
        m = (jnp.dot(uu.astype(BF16), wo_ref[0:cw, :], preferred_element_type=F32)
             + jnp.dot(o, wo_ref[cw:, :], preferred_element_type=F32))
        x1 = jnp.where(is_prompt, xp_ref[sl, :], xs_ref[sl, :]) + mod_ref[0, 2:3, :] * m
        x1_ref[sl, :] = x1
        _moe_prologue(x1, sl, mod_ref, g2_ref, wr_ref, rb_ref, h2_ref, info_ref, cnt_ref)


def _gather_start(dest_ref, yb_hbm, ybuf, sem, slot):
    def body(i, carry):
        for j in range(SUBLANES):
            for k in range(TOP_K):
                row = dest_ref[0, 0, k * TM + i * SUBLANES + j]
                pltpu.make_async_copy(yb_hbm.at[pl.ds(row, 1)], ybuf.at[slot, k, i, pl.ds(j, 1)],
                                      sem.at[slot]).start(priority=k % 2)
        return carry

    lax.fori_loop(0, TM // SUBLANES, body, 0)


def _moe_gather(dcur_ref, dnxt_ref, yb_hbm, ybuf, sem, prefetch_next):
    t = pl.program_id(0)
    slot = t % 2

    @pl.when(t == 0)
    def _():
        _gather_start(dcur_ref, yb_hbm, ybuf, sem, 0)

    if prefetch_next:
        @pl.when(t + 1 < pl.num_programs(0))
        def _():
            _gather_start(dnxt_ref, yb_hbm, ybuf, sem, 1 - slot)

    _gather_wait(ybuf, sem, slot)
    return slot


def _gather_wait(ybuf, sem, slot):
    for k in range(TOP_K):
        pltpu.make_async_copy(ybuf.at[slot, k], ybuf.at[slot, k], sem.at[slot]).wait()


def _moe_prefetch_group(dnxt_ref, yb_hbm, ybuf, sem, slot, g, n_groups):
    n_i = TM // SUBLANES
    for i in range(g * n_i // n_groups, (g + 1) * n_i // n_groups):
        for j in range(SUBLANES):
            for k in range(TOP_K):
                row = dnxt_ref[0, 0, k * TM + i * SUBLANES + j]
                pltpu.make_async_copy(yb_hbm.at[pl.ds(row, 1)], ybuf.at[1 - slot, k, i, pl.ds(j, 1)],
                                      sem.at[1 - slot]).start(priority=k % 2)


def _moe_prefetch_finish(ybuf, sem, slot):
    @pl.when(pl.program_id(0) == pl.num_programs(0) - 1)
    def _():
        _gather_wait(ybuf, sem, 1 - slot)


def _combined_rows(cw_ref, ybuf, slot, sl):
    d = ybuf.shape[-1]
    g = slice(sl.start // SUBLANES, sl.stop // SUBLANES)
    w = cw_ref[sl, :]
    y = w[:, 0:1] * ybuf[slot, 0, g].reshape(sl.stop - sl.start, d)
    for k in range(1, TOP_K):
        y = y + w[:, k:k + 1] * ybuf[slot, k, g].reshape(sl.stop - sl.start, d)
    return y


def _dispatch_kernel(dest_ref, pend_ref, h2_hbm, xb_hbm, stage, zeros, in_sem, row_sem, zero_sem):
    t = pl.program_id(0)
    nt = pl.num_programs(0)
    slot = t % DISPATCH_SLOTS

    tile_rows = TM // SUBLANES

    def stage_in(tile_idx, s):
        return pltpu.make_async_copy(h2_hbm.at[pl.ds(tile_idx * tile_rows, tile_rows)], stage.at[s], in_sem.at[s])

    def drain_rows(s):
        for _ in range(TOP_K):
            pltpu.make_async_copy(stage.at[s], stage.at[s], row_sem.at[s]).wait()

    def zero_block(start):
        return pltpu.make_async_copy(zeros, xb_hbm.at[pl.ds(pl.multiple_of(start, MOE_BLK), MOE_BLK)], zero_sem)

    def zero_fill(op):
        for e in range(N_EXPERTS):
            op(zero_block(jnp.maximum(pend_ref[e] - MOE_BLK, 0)))
        for j in range(N_EXPERTS):
            start = pend_ref[N_EXPERTS - 1] + j * MOE_BLK

            @pl.when(start < xb_hbm.shape[0])
            def _():
                op(zero_block(start))

    @pl.when(t == 0)
    def _():
        stage_in(0, 0).start()
        zeros[...] = jnp.zeros(zeros.shape, zeros.dtype)
        zero_fill(lambda cp: cp.start())
        zero_fill(lambda cp: cp.wait())

    @pl.when(jnp.logical_and(t == 0, nt > 1))
    def _():
        stage_in(1, 1).start()

    stage_in(t, slot).wait()

    def body(i, carry):
        for j in range(SUBLANES):
            for k in range(TOP_K):
                row = dest_ref[0, 0, k * TM + i * SUBLANES + j]
                pltpu.make_async_copy(stage.at[slot, i, pl.ds(j, 1)], xb_hbm.at[pl.ds(row, 1)],
                                      row_sem.at[slot]).start(priority=k % 2)
        return carry

    lax.fori_loop(0, tile_rows, body, 0)

    @pl.when(t > 0)
    def _():
        drain_rows((t + DISPATCH_SLOTS - 1) % DISPATCH_SLOTS)

    @pl.when(t + 2 < nt)
    def _():
        stage_in(t + 2, (t + 2) % DISPATCH_SLOTS).start()

    @pl.when(t == nt - 1)
    def _():
        drain_rows(slot)


def _dispatch(dest3, pend, h2, p_len):
    n, d = h2.shape
    return pl.pallas_call(
        _dispatch_kernel,
        grid=(n // TM,),
        in_specs=[pl.BlockSpec((1, 1, TOP_K * TM), lambda t: (t, 0, 0), memory_space=pltpu.SMEM),
                  pl.BlockSpec(memory_space=pltpu.SMEM), pl.BlockSpec(memory_space=pl.ANY)],
        out_specs=pl.BlockSpec(memory_space=pl.ANY),
        out_shape=jax.ShapeDtypeStruct((p_len, d), h2.dtype),
        scratch_shapes=[pltpu.VMEM((DISPATCH_SLOTS, TM // SUBLANES, SUBLANES, d), h2.dtype),
                        pltpu.VMEM((MOE_BLK, d), h2.dtype),
                        pltpu.SemaphoreType.DMA((DISPATCH_SLOTS,)), pltpu.SemaphoreType.DMA((DISPATCH_SLOTS,)),
                        pltpu.SemaphoreType.DMA(())],
        compiler_params=_cparams(("arbitrary",)),
        name="moe_dispatch",
    )(dest3, pend, h2.reshape(n // SUBLANES, SUBLANES, d))


def _pre_odd_kernel(x_ref, dcur_ref, dnxt_ref, cw_ref, yb_hbm, modp_ref, mod_ref, g_ref, w_ref,
                    x2_ref, xc_ref, p_ref, bg_ref, ybuf, sem, *, cw):
    slot = _moe_gather(dcur_ref, dnxt_ref, yb_hbm, ybuf, sem, prefetch_next=False)
    col_blocks = w_ref.shape[1] // PROJ_COLS
    n_groups = PROJ_ROW_PARTS * (1 + col_blocks)
    group = 0
    for sl in _row_parts(PROJ_ROW_PARTS):
        x2 = x_ref[sl, :] + modp_ref[0, 5:6, :] * _combined_rows(cw_ref, ybuf, slot, sl)
        x2_ref[sl, :] = x2
        _moe_prefetch_group(dnxt_ref, yb_hbm, ybuf, sem, slot, group, n_groups)
        group += 1
        h = _rms_mod(x2, g_ref[...], mod_ref[0, 1:2, :], mod_ref[0, 0:1, :]).astype(BF16)
        hd = []
        for c in range(col_blocks):
            cols = slice(c * PROJ_COLS, (c + 1) * PROJ_COLS)
            blk = jnp.dot(h, w_ref[:, cols], preferred_element_type=F32)
            which, off = divmod(c * PROJ_COLS, cw)
            dst = slice(off, off + PROJ_COLS)
            if which == 0:
                xc_ref[sl, dst] = blk
            elif which == 1:
                hd.append(blk)
            elif which == 2:
                bg_ref[sl, dst] = blk
            else:
                p_ref[sl, dst] = blk * hd[off // PROJ_COLS]
            _moe_prefetch_group(dnxt_ref, yb_hbm, ybuf, sem, slot, group, n_groups)
            group += 1
    _moe_prefetch_finish(ybuf, sem, slot)


def _post_odd_kernel(x_ref, xc_ref, xcp_ref, xcn_ref, p_ref, pp_ref, pn_ref, bg_ref, mod_ref,
                     pw_ref, ps_ref, dw_ref, wo_ref, g2_ref, wr_ref, rb_ref,
                     x3_ref, h2_ref, info_ref, cnt_ref, extc_sc, extp_sc, *, npt, tps):
    i = pl.program_id(0)
    first, last = _seq_edges(i, npt, tps)
    _fill_ext(extc_sc, xc_ref, xcp_ref, xcn_ref, first, last)
    _fill_ext(extp_sc, p_ref, pp_ref, pn_ref, first, last)
    cw = xc_ref.shape[1]
    gc = cw // len(POOL_WINDOWS)
    is_prompt = i < npt
    t_len = jnp.where(is_prompt, TM, TM * tps)
    tile_pos = jnp.where(is_prompt, 0, ((i - npt) % tps) * TM)
    for rs in _row_parts(MIX_ROW_PARTS):
        nrows = rs.stop - rs.start
        base = HALO + rs.start
        pos = lax.broadcasted_iota(jnp.int32, (nrows, 1), 0) + (tile_pos + rs.start)
        ds = []
        for g, w in enumerate(POOL_WINDOWS):
            sl = slice(g * gc, (g + 1) * gc)
            acc = None
            for o in range(-(w // 2), w // 2):
                term = extc_sc[base + o:base + o + nrows, sl]
                acc = term if acc is None else acc + term
            lo = jnp.maximum(pos - w // 2, 0)
            hi = jnp.minimum(pos + w // 2 - 1, t_len - 1)
            cnt = (hi - lo + 1).astype(F32)
            ds.append(acc / cnt - extc_sc[base:base + nrows, sl])
        d = jnp.concatenate(ds, axis=1).astype(BF16)
        yc = jnp.dot(d, pw_ref[...], preferred_element_type=F32) * ps_ref[...]
        conv = None
        for k in range(SCONV_K):
            start = base - SCONV_K // 2 + k
            term = extp_sc[start:start + nrows, :] * dw_ref[k:k + 1, :]
            conv = term if conv is None else conv + term
        yd = bg_ref[rs, :] * conv
        m = (jnp.dot(yc.astype(BF16), wo_ref[0:cw, :], preferred_element_type=F32)
             + jnp.dot(yd.astype(BF16), wo_ref[cw:, :], preferred_element_type=F32))
        x3 = x_ref[rs, :] + mod_ref[0, 2:3, :] * m
        x3_ref[rs, :] = x3
        _moe_prologue(x3, rs, mod_ref, g2_ref, wr_ref, rb_ref, h2_ref, info_ref, cnt_ref)


def _final_kernel(x_ref, dcur_ref, dnxt_ref, cw_ref, yb_hbm, modp_ref, g_ref, op_ref, os_ref, ybuf, sem, *, npt):
    i = pl.program_id(0)
    slot = _moe_gather(dcur_ref, dnxt_ref, yb_hbm, ybuf, sem, prefetch_next=True)
    x = x_ref[...] + modp_ref[0, 5:6, :] * _combined_rows(cw_ref, ybuf, slot, slice(0, TM))
    ms = jnp.mean(x * x, axis=-1, keepdims=True)
    out = x * lax.rsqrt(ms + EPS) * g_ref[...]

    @pl.when(i < npt)
    def _():
        op_ref[...] = out

    @pl.when(i >= npt)
    def _():
        os_ref[...] = out


def _expert_kernel(be_ref, nv_ref, x_ref, wg_ref, wu_ref, wd_ref, y_ref, wg_sc, wu_sc, wd_sc):
    i = pl.program_id(0)

    @pl.when(jnp.logical_or(i == 0, be_ref[i] != be_ref[jnp.maximum(i - 1, 0)]))
    def _():
        wg_sc[...] = wg_ref[0, 0].astype(BF16)
        wu_sc[...] = wu_ref[0, 0].astype(BF16)
        wd_sc[...] = wd_ref[0, 0].astype(BF16)

    @pl.when(i < nv_ref[0])
    def _():
        x = x_ref[...].astype(BF16)
        hg = jnp.dot(x, wg_sc[...], preferred_element_type=F32)
        hu = jnp.dot(x, wu_sc[...], preferred_element_type=F32)
        hid = (_silu(hg) * hu).astype(BF16)
        y_ref[...] = jnp.dot(hid, wd_sc[...], preferred_element_type=F32)

    @pl.when(i >= nv_ref[0])
    def _():
        y_ref[...] = jnp.zeros(y_ref.shape, F32)


def _experts(blk_e, nvalid, xb, layer, wg, wu, wd):
    p_len, d = xb.shape
    hid = wg.shape[3]
    nblk = p_len // MOE_BLK
    grid_spec = pltpu.PrefetchScalarGridSpec(
        num_scalar_prefetch=2,
        grid=(nblk,),
        in_specs=[pl.BlockSpec((MOE_BLK, d), lambda i, be, nv: (jnp.minimum(i, nv[0] - 1), 0)),
                  pl.BlockSpec((1, 1, d, hid), lambda i, be, nv: (layer, be[i], 0, 0)),
                  pl.BlockSpec((1, 1, d, hid), lambda i, be, nv: (layer, be[i], 0, 0)),
                  pl.BlockSpec((1, 1, hid, d), lambda i, be, nv: (layer, be[i], 0, 0))],
        out_specs=pl.BlockSpec((MOE_BLK, d), lambda i, be, nv: (i, 0)),
        scratch_shapes=[pltpu.VMEM((d, hid), BF16), pltpu.VMEM((d, hid), BF16), pltpu.VMEM((hid, d), BF16)],
    )
    return pl.pallas_call(
        _expert_kernel,
        grid_spec=grid_spec,
        out_shape=jax.ShapeDtypeStruct((p_len, d), F32),
        compiler_params=_cparams(("arbitrary",)),
        name="moe_experts",
    )(blk_e, nvalid, xb, wg, wu, wd)


def _moe(h2, info, cnt, layer, wg, wu, wd):
    n, d = h2.shape
    cnt = cnt[:, :, 0].astype(jnp.int32)
    counts = jnp.sum(cnt, axis=0)
    pc = ((counts + MOE_BLK - 1) // MOE_BLK) * MOE_BLK
    pend = jnp.cumsum(pc)
    base = (pend - pc)[None, :] + jnp.cumsum(cnt, axis=0) - cnt
    eid = info[:, 0:2, :].astype(jnp.int32)
    rank = info[:, 4:6, :].astype(jnp.int32)
    sel = eid[..., None] == jnp.arange(N_EXPERTS, dtype=jnp.int32)
    dest = jnp.sum(jnp.where(sel, base[:, None, None, :], 0), axis=-1) + rank
    p_len = n * TOP_K + N_EXPERTS * MOE_BLK
    nblk = p_len // MOE_BLK
    blk_start = jnp.arange(nblk, dtype=jnp.int32) * MOE_BLK
    blk_e = jnp.minimum(jnp.sum(pend[None, :] <= blk_start[:, None], axis=1), N_EXPERTS - 1).astype(jnp.int32)
    nvalid = (pend[-1:] // MOE_BLK).astype(jnp.int32)
    dest3 = dest.reshape(dest.shape[0], 1, TOP_K * TM)
    wcol = jnp.transpose(info[:, 2:2 + TOP_K, :], (0, 2, 1)).reshape(n, TOP_K)
    xb = _dispatch(dest3, pend.astype(jnp.int32), h2, p_len)
    yb = _experts(blk_e, nvalid, xb, layer, wg, wu, wd)
    return dest3, wcol, yb


def _rope_tables(t_len, qw):
    rows = t_len // GRID_W
    ax = qw // (DIFF_HEADS * 2) // 2
    row = jnp.repeat(jnp.arange(rows, dtype=F32), GRID_W)
    col = jnp.tile(jnp.arange(GRID_W, dtype=F32), rows)
    inv = ROPE_BASE ** (-jnp.arange(0, ax, 2, dtype=F32) / ax)
    ar = row[:, None] * inv[None, :]
    ac = col[:, None] * inv[None, :]
    cr, sr, cc, sc = jnp.cos(ar), jnp.sin(ar), jnp.cos(ac), jnp.sin(ac)
    cos = jnp.concatenate([cr, cr, cc, cc], axis=1)
    sins = jnp.concatenate([-sr, sr, -sc, sc], axis=1)
    reps = qw // cos.shape[1]
    cos = jnp.concatenate([jnp.ones((TM, cos.shape[1]), F32), cos], axis=0)
    sins = jnp.concatenate([jnp.zeros((TM, sins.shape[1]), F32), sins], axis=0)
    return jnp.tile(cos, (1, reps)), jnp.tile(sins, (1, reps))


def kernel(x_prompt, x_sample, cache_k, cache_v, c, c_ctx, w_ada, b_ada, g_norm1, g_norm2, g_final, w_in_e, conv_a_w, conv_a_b, ln_a_g, ln_a_b, lam_q1, lam_k1, lam_q2, lam_k2, subln_g, w_out_e, w_in_o, pool_w, pool_scale, conv_d_w, w_out_o, w_router_g, b_router_g, w_router_e, b_router_e, w_gate, w_up, w_down):
    bp, tp, d = x_prompt.shape
    bs, ts, _ = x_sample.shape
    past = cache_k.shape[2]
    assert tp == TM and ts % TM == 0 and d % LANES == 0
    npt = bp
    tps = ts // TM
    n_p, n_s = bp * tp, bs * ts
    n = n_p + n_s
    nt = n // TM
    cw = conv_a_w.shape[2]
    qw = (w_in_e.shape[2] - 2 * cw) // 3
    dh = qw // (DIFF_HEADS * 2)
    hpt = TM // HALO

    def cond_of(i):
        return jnp.where(i < npt, 0, 1 + (i - npt) // tps)

    def tile(w):
        return pl.BlockSpec((TM, w), lambda i: (i, 0))

    def halo_prev(w):
        return pl.BlockSpec((HALO, w), lambda i: (jnp.maximum(i * hpt - 1, 0), 0))

    def halo_next(w):
        return pl.BlockSpec((HALO, w), lambda i: (jnp.minimum((i + 1) * hpt, nt * hpt - 1), 0))

    def whole(shape):
        return pl.BlockSpec(shape, lambda i: (0,) * len(shape))

    mod_spec = pl.BlockSpec((1, 6, d), lambda i: (cond_of(i), 0, 0))
    combine_specs = [pl.BlockSpec((1, 1, TOP_K * TM), lambda i: (i, 0, 0), memory_space=pltpu.SMEM),
                     pl.BlockSpec((1, 1, TOP_K * TM), lambda i: (jnp.minimum(i + 1, nt - 1), 0, 0),
                                  memory_space=pltpu.SMEM),
                     tile(TOP_K), pl.BlockSpec(memory_space=pl.ANY)]
    combine_scratch = [pltpu.VMEM((2, TOP_K, TM // SUBLANES, SUBLANES, d), F32), pltpu.SemaphoreType.DMA((2,))]

    rows = SUBLANES * pl.cdiv(1 + bs, SUBLANES)
    cond = jnp.concatenate([c_ctx[None, :], c, jnp.zeros((rows - 1 - bs, d), F32)], axis=0)
    mod = _ada_table(cond, w_ada, b_ada)

    xp2, xs2 = x_prompt.reshape(n_p, d), x_sample.reshape(n_s, d)

    def prompt_tile(w):
        return pl.BlockSpec((TM, w), lambda i: (jnp.minimum(i, npt - 1), 0))

    def sample_tile(w):
        return pl.BlockSpec((TM, w), lambda i: (jnp.maximum(i - npt, 0), 0))

    def router_w(i):
        wre = jnp.transpose(w_router_e[i], (1, 0, 2)).reshape(d, N_EXPERTS)
        wr = jnp.concatenate([wre, w_router_g[i]], axis=1)
        return jnp.pad(wr, ((0, 0), (0, ROUTER_PAD - wr.shape[1]))).astype(BF16)

    def router_b(i):
        b = jnp.concatenate([b_router_e[i].reshape(N_EXPERTS), b_router_g[i]])
        return jnp.pad(b, (0, ROUTER_PAD - b.shape[0]))[:, None]

    cos, sins = _rope_tables(ts, qw)
    tab_spec = pl.BlockSpec((TM, qw), lambda i: (jnp.where(i < npt, 0, 1 + (i - npt) % tps), 0))
    kc_spec = pl.BlockSpec((1, 1, TM, DIFF_HEADS, 2, dh), lambda i: (jnp.minimum(i, npt - 1), 0, 0, 0, 0, 0))
    vc_spec = pl.BlockSpec((1, 1, TM, DIFF_HEADS, 2 * dh), lambda i: (jnp.minimum(i, npt - 1), 0, 0, 0, 0))
    u, q, k, vt, new_k, new_v = pl.pallas_call(
        functools.partial(_pre_even_kernel, npt=npt, cw=cw, qw=qw, qscale=dh ** -0.5 * math.log2(math.e)),
        grid=(nt,),
        in_specs=[prompt_tile(d), sample_tile(d), mod_spec, whole((1, d)), whole(w_in_e.shape[1:]),
                  tab_spec, tab_spec],
        out_specs=[tile(cw), tile(qw), tile(qw), pl.BlockSpec((qw, TM), lambda i: (0, i)),
                   kc_spec, vc_spec],
        out_shape=[jax.ShapeDtypeStruct((n, cw), F32), jax.ShapeDtypeStruct((n, qw), BF16),
                   jax.ShapeDtypeStruct((n, qw), BF16), jax.ShapeDtypeStruct((qw, n), BF16),
                   jax.ShapeDtypeStruct((bp, 1, tp, DIFF_HEADS, 2, dh), F32),
                   jax.ShapeDtypeStruct((bp, 1, tp, DIFF_HEADS, 2 * dh), F32)],
        scratch_shapes=[pltpu.VMEM((2, TM, qw), F32)],
        compiler_params=_cparams(("arbitrary",)),
        name="pre_even",
    )(xp2, xs2, mod[0], g_norm1[0][None, :], w_in_e[0].astype(BF16), cos, sins)

    lam_init = 0.8 - 0.6 * math.exp(-0.3 * 0)
    lam = (jnp.exp(jnp.sum(lam_q1[0] * lam_k1[0])) - jnp.exp(jnp.sum(lam_q2[0] * lam_k2[0])) + lam_init)
    lam = lam.reshape(1).astype(F32)
    sub_g = subln_g[0][:, None]
    o_p = _attention(lam, q, k, vt, sub_g, None, batch=bp, t_len=tp, row0=0, tq=TM, tk=TM,
                     post_scale=1.0 - lam_init, heads_per_step=DIFF_HEADS)
    k_ctx = cache_k[:, 0].reshape(bs * past, qw).astype(BF16)
    vt_ctx = jnp.transpose(cache_v[:, 0].reshape(bs, past, qw), (0, 2, 1)).astype(BF16)
    o_s = _attention(lam, q, k, vt, sub_g, (k_ctx, vt_ctx), batch=bs, t_len=ts, row0=n_p, tq=ATTN_TQ, tk=ATTN_TK,
                     post_scale=1.0 - lam_init)

    moe_outs = [jax.ShapeDtypeStruct((n, d), F32), jax.ShapeDtypeStruct((n, d), F32),
                jax.ShapeDtypeStruct((nt, INFO_ROWS, TM), F32), jax.ShapeDtypeStruct((nt, N_EXPERTS, 1), F32)]
    moe_out_specs = [tile(d), tile(d), pl.BlockSpec((1, INFO_ROWS, TM), lambda i: (i, 0, 0)),
                     pl.BlockSpec((1, N_EXPERTS, 1), lambda i: (i, 0, 0))]
    x1, h2, info, cnt = pl.pallas_call(
        functools.partial(_post_even_kernel, npt=npt, tps=tps),
        grid=(nt,),
        in_specs=[prompt_tile(d), sample_tile(d), tile(cw), halo_prev(cw), halo_next(cw),
                  prompt_tile(qw), sample_tile(qw), mod_spec,
                  whole((CONV_K, cw)), whole((1, cw)), whole((1, cw)), whole((1, cw)),
                  whole(w_out_e.shape[1:]), whole((1, d)), whole((d, ROUTER_PAD)), whole((ROUTER_PAD, 1))],
        out_specs=moe_out_specs,
        out_shape=moe_outs,
        scratch_shapes=[pltpu.VMEM((TM + 2 * HALO, cw), F32),
                        pltpu.VMEM((SUBLANES, TM + 2 * HALO - SUBLANES, cw), F32)],
        compiler_params=_cparams(("parallel",)),
        name="post_even",
    )(xp2, xs2, u, u, u, o_p, o_s, mod[0], conv_a_w[0], conv_a_b[0][None, :], ln_a_g[0][None, :], ln_a_b[0][None, :],
      w_out_e[0].astype(BF16), g_norm2[0][None, :], router_w(0), router_b(0))
    dest3, wcol, yb = _moe(h2, info, cnt, 0, w_gate, w_up, w_down)

    pw = pool_w.shape[2]
    x2, xc, p, bg = pl.pallas_call(
        functools.partial(_pre_odd_kernel, cw=cw),
        grid=(nt,),
        in_specs=[tile(d)] + combine_specs + [mod_spec, mod_spec, whole((1, d)), whole(w_in_o.shape[1:])],
        out_specs=[tile(d), tile(cw), tile(cw), tile(cw)],
        out_shape=[jax.ShapeDtypeStruct((n, d), F32)] + [jax.ShapeDtypeStruct((n, cw), F32)] * 3,
        scratch_shapes=combine_scratch,
        compiler_params=_cparams(("arbitrary",)),
        name="pre_odd",
    )(x1, dest3, dest3, wcol, yb, mod[0], mod[1], g_norm1[1][None, :], w_in_o[0].astype(BF16))
    pool_bd = jnp.zeros((cw, cw), F32)
    for g in range(len(POOL_WINDOWS)):
        pool_bd = pool_bd.at[g * pw:(g + 1) * pw, g * pw:(g + 1) * pw].set(pool_w[0, g])
    x3, h2, info, cnt = pl.pallas_call(
        functools.partial(_post_odd_kernel, npt=npt, tps=tps),
        grid=(nt,),
        in_specs=[tile(d), tile(cw), halo_prev(cw), halo_next(cw), tile(cw), halo_prev(cw), halo_next(cw),
                  tile(cw), mod_spec, whole((cw, cw)), whole((1, cw)), whole((SCONV_K, cw)),
                  whole(w_out_o.shape[1:]), whole((1, d)), whole((d, ROUTER_PAD)), whole((ROUTER_PAD, 1))],
        out_specs=moe_out_specs,
        out_shape=moe_outs,
        scratch_shapes=[pltpu.VMEM((TM + 2 * HALO, cw), F32), pltpu.VMEM((TM + 2 * HALO, cw), F32)],
        compiler_params=_cparams(("parallel",)),
        name="post_odd",
    )(x2, xc, xc, xc, p, p, p, bg, mod[1], pool_bd.astype(BF16), pool_scale[0][None, :], conv_d_w[0],
      w_out_o[0].astype(BF16), g_norm2[1][None, :], router_w(1), router_b(1))
    dest3, wcol, yb = _moe(h2, info, cnt, 1, w_gate, w_up, w_down)

    out_p, out_s = pl.pallas_call(
        functools.partial(_final_kernel, npt=npt),
        grid=(nt,),
        in_specs=[tile(d)] + combine_specs + [mod_spec, whole((1, d))],
        out_specs=[pl.BlockSpec((TM, d), lambda i: (jnp.minimum(i, npt - 1), 0)),
                   pl.BlockSpec((TM, d), lambda i: (jnp.maximum(i - npt, 0), 0))],
        out_shape=[jax.ShapeDtypeStruct((n_p, d), F32), jax.ShapeDtypeStruct((n_s, d), F32)],
        scratch_shapes=combine_scratch,
        compiler_params=_cparams(("arbitrary",)),
        name="final_norm",
    )(x3, dest3, dest3, wcol, yb, mod[1], g_final[None, :])

    return (out_p.reshape(bp, tp, d), out_s.reshape(bs, ts, d), new_k, new_v)
```

```python
import functools
import math

import jax
import jax.numpy as jnp
from jax import lax
from jax.experimental import pallas as pl
from jax.experimental.pallas import tpu as pltpu

F32 = jnp.float32
BF16 = jnp.bfloat16

EPS = 1e-6
GRID_W = 64
ROPE_BASE = 10000.0
CONV_K = 31
SCONV_K = 3
POOL_WINDOWS = (2, 4, 8, 16)
DIFF_HEADS = 4
N_GROUPS = 4
EXPERTS_PER_GROUP = 8
N_EXPERTS = N_GROUPS * EXPERTS_PER_GROUP
TOP_K = 2

LANES = 128
SUBLANES = 8
TM = 256
HALO = 16
PROJ_ROW_PARTS = 2
MIX_ROW_PARTS = 1
PROJ_COLS = 256
MOE_BLK = 512
ATTN_TQ = 2048
ATTN_TK = 512
CONV_ROWS = 32
INFO_ROWS = 8
ROUTER_PAD = LANES
ONES_ROWS = 2 * SUBLANES
DISPATCH_SLOTS = 3
VMEM_LIMIT = 56 * 1024 * 1024


def _cparams(sem):
    return pltpu.CompilerParams(dimension_semantics=sem, vmem_limit_bytes=VMEM_LIMIT)


def _rms_mod(x, g, sc, sh):
    ms = jnp.mean(x * x, axis=-1, keepdims=True)
    return (x * lax.rsqrt(ms + EPS)) * g * (1.0 + sc) + sh


def _silu(x):
    return x * jax.nn.sigmoid(x)


def _pack_bf16_pairs(y):
    half = y.shape[1] // 2
    lo = pltpu.bitcast(y[:, :half].astype(BF16).astype(F32), jnp.uint32)
    hi = pltpu.bitcast(y[:, half:].astype(BF16).astype(F32), jnp.uint32)
    return lax.shift_right_logical(lo, jnp.uint32(16)) | hi


def _unpack_bf16_pairs(w):
    lo = pltpu.bitcast(lax.shift_left(w, jnp.uint32(16)), F32)
    hi = pltpu.bitcast(w & jnp.uint32(0xFFFF0000), F32)
    return jnp.concatenate([lo, hi], axis=1)


def _ada_kernel(c_ref, w_ref, b_ref, o_ref):
    cs = _silu(c_ref[...])
    o_ref[0] = jnp.dot(cs, w_ref[0], precision=lax.Precision.HIGHEST,
                       preferred_element_type=F32) + b_ref[0]


def _ada_table(cond, w_ada, b_ada):
    depth, d, six_d = w_ada.shape
    rows = cond.shape[0]
    nj = six_d // d
    out = pl.pallas_call(
        _ada_kernel,
        grid=(depth, nj),
        in_specs=[pl.BlockSpec((rows, d), lambda l, j: (0, 0)),
                  pl.BlockSpec((1, d, d), lambda l, j: (l, 0, j)),
                  pl.BlockSpec((1, 1, d), lambda l, j: (l, 0, j))],
        out_specs=pl.BlockSpec((1, rows, d), lambda l, j: (l, 0, j)),
        out_shape=jax.ShapeDtypeStruct((depth, rows, six_d), F32),
        compiler_params=_cparams(("parallel", "parallel")),
        name="ada_table",
    )(cond, w_ada, b_ada.reshape(depth, 1, six_d))
    return out.reshape(depth, rows, nj, d)


def _row_parts(parts):
    rows = TM // parts
    return [slice(p * rows, (p + 1) * rows) for p in range(parts)]


def _rope(x, cos, sins, half):
    lane = lax.broadcasted_iota(jnp.int32, (1, LANES), 1)
    first_half = (lane % (2 * half)) < half
    outs = []
    for j in range(x.shape[1] // LANES):
        sl = slice(j * LANES, (j + 1) * LANES)
        xs = x[:, sl]
        nxt = pltpu.roll(xs, LANES - half, 1)
        prv = pltpu.roll(xs, half, 1)
        rot = jnp.where(first_half, nxt, prv)
        outs.append(xs * cos[:, sl] + rot * sins[:, sl])
    return jnp.concatenate(outs, axis=1)


def _pre_even_kernel(xp_ref, xs_ref, mod_ref, g_ref, w_ref, cos_ref, sin_ref,
                     u_ref, q_ref, k_ref, vt_ref, kc_ref, vc_ref, kv_sc, *, npt, cw, qw, qscale):
    i = pl.program_id(0)
    is_prompt = i < npt
    half = kc_ref.shape[-1] // 4
    for sl in _row_parts(PROJ_ROW_PARTS):
        x = jnp.where(is_prompt, xp_ref[sl, :], xs_ref[sl, :])
        h = _rms_mod(x, g_ref[...], mod_ref[0, 1:2, :], mod_ref[0, 0:1, :])
        proj = jnp.dot(h.astype(BF16), w_ref[...], preferred_element_type=F32)
        ga = proj[:, :cw]
        gb = proj[:, cw:2 * cw]
        u_ref[sl, :] = ga * jax.nn.sigmoid(gb)
        q = proj[:, 2 * cw:2 * cw + qw]
        k = proj[:, 2 * cw + qw:2 * cw + 2 * qw]
        v = proj[:, 2 * cw + 2 * qw:]
        kv_sc[0, sl, :] = k
        kv_sc[1, sl, :] = v
        cos = cos_ref[sl, :]
        sins = sin_ref[sl, :]
        q_ref[sl, :] = (_rope(q, cos, sins, half) * qscale).astype(BF16)
        k_ref[sl, :] = _rope(k, cos, sins, half).astype(BF16)
        vt_ref[:, sl] = v.T.astype(BF16)

    @pl.when(is_prompt)
    def _():
        heads, halves, dh = kc_ref.shape[3:]
        for h in range(heads):
            vc_ref[0, 0, :, h, :] = kv_sc[1, :, h * halves * dh:(h + 1) * halves * dh]
            for c in range(halves):
                lo = (h * halves + c) * dh
                kc_ref[0, 0, :, h, c, :] = kv_sc[0, :, lo:lo + dh]


def _attn_kernel(lam_ref, q_ref, k_ref, vt_ref, *rest, nk, tk, nk_ctx, post_scale):
    if nk_ctx:
        kc_ref, vtc_ref, g_ref, o_ref, s_sc, mc_sc, m_sc, acc_sc = rest
    else:
        g_ref, o_ref, s_sc, mc_sc, m_sc, acc_sc = rest
    q = q_ref[...]
    hw = q.shape[1]
    dh = hw // 2
    lane = lax.broadcasted_iota(jnp.int32, (1, hw), 1)
    zero = jnp.zeros_like(q)
    qs = (jnp.where(lane < dh, q, zero), jnp.where(lane >= dh, q, zero))
    m_sc[...] = jnp.full(m_sc.shape, -jnp.inf, F32)
    acc_sc[...] = jnp.zeros(acc_sc.shape, F32)
    ones_rows = (lax.broadcasted_iota(jnp.int32, (ONES_ROWS, tk), 0) == 0).astype(BF16)

    def scores(kk, slot):
        for c in range(2):
            s = lax.dot_general(kk, qs[c], (((1,), (1,)), ((), ())), preferred_element_type=F32)
            s_sc[slot, c] = s
            mc_sc[slot, c] = jnp.max(s, axis=0, keepdims=True)

    def consume(vt, slot):
        vta = jnp.concatenate([vt, ones_rows], axis=0)
        for c in range(2):
            m_prev = m_sc[c]
            m_new = jnp.maximum(m_prev, mc_sc[slot, c])
            alpha = jnp.exp2(m_prev - m_new)
            p = jnp.exp2(s_sc[slot, c] - m_new).astype(BF16)
            acc_sc[c] = alpha * acc_sc[c] + jnp.dot(vta, p, preferred_element_type=F32)
            m_sc[c] = m_new

    def k_at(i):
        if i < nk_ctx:
            return kc_ref[i * tk:(i + 1) * tk, :]
        return k_ref[(i - nk_ctx) * tk:(i - nk_ctx + 1) * tk, :]

    def vt_at(i):
        if i < nk_ctx:
            return vtc_ref[0, :, i * tk:(i + 1) * tk]
        return vt_ref[:, (i - nk_ctx) * tk:(i - nk_ctx + 1) * tk]

    n_total = nk_ctx + nk
    peel = nk_ctx + (n_total - 1 - nk_ctx) % 2
    scores(k_at(0), 0)
    for i in range(peel):
        scores(k_at(i + 1), (i + 1) % 2)
        consume(vt_at(i), i % 2)

    def body(t, carry):
        for r in range(2):
            j = 2 * t + (peel + r - nk_ctx)
            nxt = pl.multiple_of((j + 1) * tk, tk)
            cur = pl.multiple_of(j * tk, tk)
            scores(k_ref[pl.ds(nxt, tk), :], (peel + r + 1) % 2)
            consume(vt_ref[:, pl.ds(cur, tk)], (peel + r) % 2)
        return carry

    lax.fori_loop(0, (n_total - 1 - peel) // 2, body, 0)
    consume(vt_at(n_total - 1), (n_total - 1) % 2)
    o = (acc_sc[0, :hw, :] / acc_sc[0, hw:hw + 1, :]
         - lam_ref[0] * (acc_sc[1, :hw, :] / acc_sc[1, hw:hw + 1, :]))
    ms = jnp.mean(o * o, axis=0, keepdims=True)
    o = o * lax.rsqrt(ms + EPS) * g_ref[...] * post_scale
    o_ref[...] = o.T.astype(BF16)


def _attn_heads_kernel(lam_ref, q_ref, k_ref, vt_ref, g_ref, o_ref, *scratch, heads, **kw):
    hw = q_ref.shape[1] // heads
    for h in range(heads):
        cols = pl.ds(h * hw, hw)
        _attn_kernel(lam_ref, q_ref.at[:, cols], k_ref.at[:, cols], vt_ref.at[cols, :], g_ref, o_ref.at[:, cols],
                     *scratch, **kw)


def _attention(lam, q, k, vt, g_col, ctx, *, batch, t_len, row0, tq, tk, post_scale, heads_per_step=1):
    hw = q.shape[1] // DIFF_HEADS
    bw = hw * heads_per_step
    nq = t_len // tq
    assert row0 % t_len == 0 and t_len % tk == 0 and (ctx is None or heads_per_step == 1)
    qb0 = row0 // tq
    sb0 = row0 // t_len
    in_specs = [pl.BlockSpec(memory_space=pltpu.SMEM),
                pl.BlockSpec((tq, bw), lambda b, h, i: (qb0 + b * nq + i, h)),
                pl.BlockSpec((t_len, bw), lambda b, h, i: (sb0 + b, h)),
                pl.BlockSpec((bw, t_len), lambda b, h, i: (h, sb0 + b))]
    args = [lam, q, k, vt]
    nk_ctx = 0
    if ctx is not None:
        k_ctx, vt_ctx = ctx
        past = vt_ctx.shape[2]
        assert past % tk == 0
        nk_ctx = past // tk
        in_specs += [pl.BlockSpec((past, hw), lambda b, h, i: (b, h)),
                     pl.BlockSpec((1, hw, past), lambda b, h, i: (b, h, 0))]
        args += [k_ctx, vt_ctx]
    in_specs.append(pl.BlockSpec((hw, 1), lambda b, h, i: (0, 0)))
    args.append(g_col)
    body = functools.partial(_attn_kernel, nk=t_len // tk, tk=tk, nk_ctx=nk_ctx, post_scale=post_scale)
    if heads_per_step > 1:
        body = functools.partial(_attn_heads_kernel, heads=heads_per_step, nk=t_len // tk, tk=tk, nk_ctx=nk_ctx,
                                 post_scale=post_scale)
    return pl.pallas_call(
        body,
        grid=(batch, DIFF_HEADS // heads_per_step, nq),
        in_specs=in_specs,
        out_specs=pl.BlockSpec((tq, bw), lambda b, h, i: (b * nq + i, h)),
        out_shape=jax.ShapeDtypeStruct((batch * t_len, q.shape[1]), BF16),
        scratch_shapes=[pltpu.VMEM((2, 2, tk, tq), F32), pltpu.VMEM((2, 2, 1, tq), F32),
                        pltpu.VMEM((2, 1, tq), F32), pltpu.VMEM((2, hw + ONES_ROWS, tq), F32)],
        compiler_params=_cparams(("parallel", "parallel", "parallel")),
        name="diff_attention",
    )(*args)


def _seq_edges(i, npt, tps):
    r = (i - npt) % tps
    is_prompt = i < npt
    first = jnp.logical_or(is_prompt, r == 0)
    last = jnp.logical_or(is_prompt, r == tps - 1)
    return first, last


def _fill_ext(ext_sc, cur_ref, prev_ref, next_ref, first, last):
    prev = prev_ref[...]
    nxt = next_ref[...]
    ext_sc[0:HALO, :] = jnp.where(first, jnp.zeros_like(prev), prev)
    ext_sc[HALO:HALO + TM, :] = cur_ref[...]
    ext_sc[HALO + TM:, :] = jnp.where(last, jnp.zeros_like(nxt), nxt)


def _route_tile(logits, b_ref, info_ref, cnt_ref):
    lt = logits.T + b_ref[...]
    gl = lt[N_EXPERTS:N_EXPERTS + N_GROUPS]
    ge = jnp.exp(gl - jnp.max(gl, axis=0, keepdims=True))
    gp = ge / jnp.sum(ge, axis=0, keepdims=True)
    best = gp[0:1]
    gidx = jnp.zeros(best.shape, jnp.int32)
    el = lt[0:EXPERTS_PER_GROUP]
    for g in range(1, N_GROUPS):
        better = gp[g:g + 1] > best
        gidx = jnp.where(better, g, gidx)
        best = jnp.where(better, gp[g:g + 1], best)
    for g in range(1, N_GROUPS):
        el = jnp.where(gidx == g, lt[g * EXPERTS_PER_GROUP:(g + 1) * EXPERTS_PER_GROUP], el)
    rio = lax.broadcasted_iota(jnp.int32, el.shape, 0)
    v0 = jnp.max(el, axis=0, keepdims=True)
    i0 = jnp.min(jnp.where(el == v0, rio, EXPERTS_PER_GROUP), axis=0, keepdims=True)
    el2 = jnp.where(rio == i0, -jnp.inf, el)
    v1 = jnp.max(el2, axis=0, keepdims=True)
    i1 = jnp.min(jnp.where(el2 == v1, rio, EXPERTS_PER_GROUP), axis=0, keepdims=True)
    e1 = jnp.exp(v1 - v0)
    den = 1.0 + e1
    w0 = (1.0 / den) * best
    w1 = (e1 / den) * best
    eid0 = gidx * EXPERTS_PER_GROUP + i0
    eid1 = gidx * EXPERTS_PER_GROUP + i1
    tm = lt.shape[1]
    eio = lax.broadcasted_iota(jnp.int32, (N_EXPERTS, tm), 0)
    oh0 = eio == eid0
    oh1 = eio == eid1
    ohs = jnp.where(oh0, 1.0, 0.0) + jnp.where(oh1, 1.0, 0.0)
    upper = (lax.broadcasted_iota(jnp.int32, (tm, tm), 0)
             < lax.broadcasted_iota(jnp.int32, (tm, tm), 1)).astype(BF16)
    before = jnp.dot(ohs.astype(BF16), upper, preferred_element_type=F32)
    rank0 = jnp.sum(jnp.where(oh0, before, 0.0), axis=0, keepdims=True)
    rank1 = jnp.sum(jnp.where(oh1, before, 0.0), axis=0, keepdims=True)
    info_ref[0] = jnp.concatenate([eid0.astype(F32), eid1.astype(F32), w0, w1, rank0, rank1,
                                   jnp.zeros((INFO_ROWS - 6, tm), F32)], axis=0)
    cnt_ref[0] = jnp.sum(ohs, axis=1, keepdims=True)


def _moe_prologue(x_new, sl, mod_ref, g2_ref, wr_ref, rb_ref, h2_ref, info_ref, cnt_ref):
    h2 = _rms_mod(x_new, g2_ref[...], mod_ref[0, 4:5, :], mod_ref[0, 3:4, :])
    h2_ref[sl, :] = h2
    assert sl.stop - sl.start == TM
    _route_tile(jnp.dot(h2.astype(BF16), wr_ref[...], preferred_element_type=F32), rb_ref, info_ref, cnt_ref)


def _post_even_kernel(xp_ref, xs_ref, u_ref, up_ref, un_ref, op_ref, os_ref, mod_ref, cw_ref, cb_ref,
                      lng_ref, lnb_ref, wo_ref, g2_ref, wr_ref, rb_ref, x1_ref, h2_ref, info_ref, cnt_ref,
                      ext_sc, sh_sc, *, npt, tps):
    i = pl.program_id(0)
    first, last = _seq_edges(i, npt, tps)
    _fill_ext(ext_sc, u_ref, up_ref, un_ref, first, last)
    cw = u_ref.shape[1]
    span = sh_sc.shape[1]
    for r in range(SUBLANES):
        sh_sc[r] = ext_sc[r:r + span, :]
    rows = CONV_ROWS
    is_prompt = i < npt
    for sl in _row_parts(MIX_ROW_PARTS):
        chunks = []
        for r0 in range(sl.start, sl.stop, rows):
            acc = None
            for k in range(CONV_K):
                off = HALO - CONV_K // 2 + k
                start = r0 + off - off % SUBLANES
                term = sh_sc[off % SUBLANES, start:start + rows, :] * cw_ref[k:k + 1, :]
                acc = term if acc is None else acc + term
            chunks.append(acc)
        conv = jnp.concatenate(chunks, axis=0) + cb_ref[...]
        mu = jnp.mean(conv, axis=-1, keepdims=True)
        cen = conv - mu
        var = jnp.mean(cen * cen, axis=-1, keepdims=True)
        uu = _silu(cen * lax.rsqrt(var + EPS) * lng_ref[...] + lnb_ref[...])
        o = jnp.where(is_prompt, op_ref[sl, :], os_ref[sl, :])
        m = (jnp.dot(uu.astype(BF16), wo_ref[0:cw, :], preferred_element_type=F32)
             + jnp.dot(o, wo_ref[cw:, :], preferred_element_type=F32))
        x1 = jnp.where(is_prompt, xp_ref[sl, :], xs_ref[sl, :]) + mod_ref[0, 2:3, :] * m
        x1_ref[sl, :] = x1
        _moe_prologue(x1, sl, mod_ref, g2_ref, wr_ref, rb_ref, h2_ref, info_ref, cnt_ref)


def _gather_start(dest_ref, yb_hbm, ybuf, sem, slot):
    def body(i, carry):
        for j in range(SUBLANES):
            for k in range(TOP_K):
                row = dest_ref[0, 0, k * TM + i * SUBLANES + j]
                pltpu.make_async_copy(yb_hbm.at[pl.ds(row, 1)], ybuf.at[slot, k, i, pl.ds(j, 1)],
                                      sem.at[slot]).start(priority=k % 2)
        return carry

    lax.fori_loop(0, TM // SUBLANES, body, 0)


def _moe_gather(dcur_ref, dnxt_ref, yb_hbm, ybuf, sem, prefetch_next):
    t = pl.program_id(0)
    slot = t % 2

    @pl.when(t == 0)
    def _():
        _gather_start(dcur_ref, yb_hbm, ybuf, sem, 0)

    if prefetch_next:
        @pl.when(t + 1 < pl.num_programs(0))
        def _():
            _gather_start(dnxt_ref, yb_hbm, ybuf, sem, 1 - slot)

    _gather_wait(ybuf, sem, slot)
    return slot


def _gather_wait(ybuf, sem, slot):
    for k in range(TOP_K):
        pltpu.make_async_copy(ybuf.at[slot, k], ybuf.at[slot, k], sem.at[slot]).wait()


def _moe_prefetch_group(dnxt_ref, yb_hbm, ybuf, sem, slot, g, n_groups):
    n_i = TM // SUBLANES
    for i in range(g * n_i // n_groups, (g + 1) * n_i // n_groups):
        for j in range(SUBLANES):
            for k in range(TOP_K):
                row = dnxt_ref[0, 0, k * TM + i * SUBLANES + j]
                pltpu.make_async_copy(yb_hbm.at[pl.ds(row, 1)], ybuf.at[1 - slot, k, i, pl.ds(j, 1)],
                                      sem.at[1 - slot]).start(priority=k % 2)


def _moe_prefetch_finish(ybuf, sem, slot):
    @pl.when(pl.program_id(0) == pl.num_programs(0) - 1)
    def _():
        _gather_wait(ybuf, sem, 1 - slot)


def _combined_rows(cw_ref, ybuf, slot, sl):
    dw = ybuf.shape[-1]
    g = slice(sl.start // SUBLANES, sl.stop // SUBLANES)
    w = cw_ref[sl, :]
    y = w[:, 0:1] * _unpack_bf16_pairs(ybuf[slot, 0, g].reshape(sl.stop - sl.start, dw))
    for k in range(1, TOP_K):
        y = y + w[:, k:k + 1] * _unpack_bf16_pairs(ybuf[slot, k, g].reshape(sl.stop - sl.start, dw))
    return y


def _dispatch_kernel(dest_ref, pend_ref, h2_hbm, xb_hbm, stage, zeros, in_sem, row_sem, zero_sem):
    t = pl.program_id(0)
    nt = pl.num_programs(0)
    slot = t % DISPATCH_SLOTS

    tile_rows = TM // SUBLANES

    def stage_in(tile_idx, s):
        return pltpu.make_async_copy(h2_hbm.at[pl.ds(tile_idx * tile_rows, tile_rows)], stage.at[s], in_sem.at[s])

    def drain_rows(s):
        for _ in range(TOP_K):
            pltpu.make_async_copy(stage.at[s], stage.at[s], row_sem.at[s]).wait()

    def zero_block(start):
        return pltpu.make_async_copy(zeros, xb_hbm.at[pl.ds(pl.multiple_of(start, MOE_BLK), MOE_BLK)], zero_sem)

    def zero_fill(op):
        for e in range(N_EXPERTS):
            op(zero_block(jnp.maximum(pend_ref[e] - MOE_BLK, 0)))
        for j in range(N_EXPERTS):
            start = pend_ref[N_EXPERTS - 1] + j * MOE_BLK

            @pl.when(start < xb_hbm.shape[0])
            def _():
                op(zero_block(start))

    @pl.when(t == 0)
    def _():
        stage_in(0, 0).start()
        zeros[...] = jnp.zeros(zeros.shape, zeros.dtype)
        zero_fill(lambda cp: cp.start())
        zero_fill(lambda cp: cp.wait())

    @pl.when(jnp.logical_and(t == 0, nt > 1))
    def _():
        stage_in(1, 1).start()

    stage_in(t, slot).wait()

    def body(i, carry):
        for j in range(SUBLANES):
            for k in range(TOP_K):
                row = dest_ref[0, 0, k * TM + i * SUBLANES + j]
                pltpu.make_async_copy(stage.at[slot, i, pl.ds(j, 1)], xb_hbm.at[pl.ds(row, 1)],
                                      row_sem.at[slot]).start(priority=k % 2)
        return carry

    lax.fori_loop(0, tile_rows, body, 0)

    @pl.when(t > 0)
    def _():
        drain_rows((t + DISPATCH_SLOTS - 1) % DISPATCH_SLOTS)

    @pl.when(t + 2 < nt)
    def _():
        stage_in(t + 2, (t + 2) % DISPATCH_SLOTS).start()

    @pl.when(t == nt - 1)
    def _():
        drain_rows(slot)


def _dispatch(dest3, pend, h2, p_len):
    n, d = h2.shape
    return pl.pallas_call(
        _dispatch_kernel,
        grid=(n // TM,),
        in_specs=[pl.BlockSpec((1, 1, TOP_K * TM), lambda t: (t, 0, 0), memory_space=pltpu.SMEM),
                  pl.BlockSpec(memory_space=pltpu.SMEM), pl.BlockSpec(memory_space=pl.ANY)],
        out_specs=pl.BlockSpec(memory_space=pl.ANY),
        out_shape=jax.ShapeDtypeStruct((p_len, d), h2.dtype),
        scratch_shapes=[pltpu.VMEM((DISPATCH_SLOTS, TM // SUBLANES, SUBLANES, d), h2.dtype),
                        pltpu.VMEM((MOE_BLK, d), h2.dtype),
                        pltpu.SemaphoreType.DMA((DISPATCH_SLOTS,)), pltpu.SemaphoreType.DMA((DISPATCH_SLOTS,)),
                        pltpu.SemaphoreType.DMA(())],
        compiler_params=_cparams(("arbitrary",)),
        name="moe_dispatch",
    )(dest3, pend, h2.reshape(n // SUBLANES, SUBLANES, d))


def _pre_odd_kernel(x_ref, dcur_ref, dnxt_ref, cw_ref, yb_hbm, modp_ref, mod_ref, g_ref, w_ref,
                    x2_ref, xc_ref, p_ref, bg_ref, ybuf, sem, *, cw):
    slot = _moe_gather(dcur_ref, dnxt_ref, yb_hbm, ybuf, sem, prefetch_next=False)
    col_blocks = w_ref.shape[1] // PROJ_COLS
    n_groups = PROJ_ROW_PARTS * (1 + col_blocks)
    group = 0
    for sl in _row_parts(PROJ_ROW_PARTS):
        x2 = x_ref[sl, :] + modp_ref[0, 5:6, :] * _combined_rows(cw_ref, ybuf, slot, sl)
        x2_ref[sl, :] = x2
        _moe_prefetch_group(dnxt_ref, yb_hbm, ybuf, sem, slot, group, n_groups)
        group += 1
        h = _rms_mod(x2, g_ref[...], mod_ref[0, 1:2, :], mod_ref[0, 0:1, :]).astype(BF16)
        hd = []
        for c in range(col_blocks):
            cols = slice(c * PROJ_COLS, (c + 1) * PROJ_COLS)
            blk = jnp.dot(h, w_ref[:, cols], preferred_element_type=F32)
            which, off = divmod(c * PROJ_COLS, cw)
            dst = slice(off, off + PROJ_COLS)
            if which == 0:
                xc_ref[sl, dst] = blk
            elif which == 1:
                hd.append(blk)
            elif which == 2:
                bg_ref[sl, dst] = blk
            else:
                p_ref[sl, dst] = blk * hd[off // PROJ_COLS]
            _moe_prefetch_group(dnxt_ref, yb_hbm, ybuf, sem, slot, group, n_groups)
            group += 1
    _moe_prefetch_finish(ybuf, sem, slot)


def _post_odd_kernel(x_ref, xc_ref, xcp_ref, xcn_ref, p_ref, pp_ref, pn_ref, bg_ref, mod_ref,
                     pw_ref, ps_ref, dw_ref, wo_ref, g2_ref, wr_ref, rb_ref,
                     x3_ref, h2_ref, info_ref, cnt_ref, extc_sc, extp_sc, *, npt, tps):
    i = pl.program_id(0)
    first, last = _seq_edges(i, npt, tps)
    _fill_ext(extc_sc, xc_ref, xcp_ref, xcn_ref, first, last)
    _fill_ext(extp_sc, p_ref, pp_ref, pn_ref, first, last)
    cw = xc_ref.shape[1]
    gc = cw // len(POOL_WINDOWS)
    is_prompt = i < npt
    t_len = jnp.where(is_prompt, TM, TM * tps)
    tile_pos = jnp.where(is_prompt, 0, ((i - npt) % tps) * TM)
    for rs in _row_parts(MIX_ROW_PARTS):
        nrows = rs.stop - rs.start
        base = HALO + rs.start
        pos = lax.broadcasted_iota(jnp.int32, (nrows, 1), 0) + (tile_pos + rs.start)
        ds = []
        for g, w in enumerate(POOL_WINDOWS):
            sl = slice(g * gc, (g + 1) * gc)
            acc = None
            for o in range(-(w // 2), w // 2):
                term = extc_sc[base + o:base + o + nrows, sl]
                acc = term if acc is None else acc + term
            lo = jnp.maximum(pos - w // 2, 0)
            hi = jnp.minimum(pos + w // 2 - 1, t_len - 1)
            cnt = (hi - lo + 1).astype(F32)
            ds.append(acc / cnt - extc_sc[base:base + nrows, sl])
        d = jnp.concatenate(ds, axis=1).astype(BF16)
        yc = jnp.dot(d, pw_ref[...], preferred_element_type=F32) * ps_ref[...]
        conv = None
        for k in range(SCONV_K):
            start = base - SCONV_K // 2 + k
            term = extp_sc[start:start + nrows, :] * dw_ref[k:k + 1, :]
            conv = term if conv is None else conv + term
        yd = bg_ref[rs, :] * conv
        m = (jnp.dot(yc.astype(BF16), wo_ref[0:cw, :], preferred_element_type=F32)
             + jnp.dot(yd.astype(BF16), wo_ref[cw:, :], preferred_element_type=F32))
        x3 = x_ref[rs, :] + mod_ref[0, 2:3, :] * m
        x3_ref[rs, :] = x3
        _moe_prologue(x3, rs, mod_ref, g2_ref, wr_ref, rb_ref, h2_ref, info_ref, cnt_ref)


def _final_kernel(x_ref, dcur_ref, dnxt_ref, cw_ref, yb_hbm, modp_ref, g_ref, op_ref, os_ref, ybuf, sem, *, npt):
    i = pl.program_id(0)
    slot = _moe_gather(dcur_ref, dnxt_ref, yb_hbm, ybuf, sem, prefetch_next=True)
    x = x_ref[...] + modp_ref[0, 5:6, :] * _combined_rows(cw_ref, ybuf, slot, slice(0, TM))
    ms = jnp.mean(x * x, axis=-1, keepdims=True)
    out = x * lax.rsqrt(ms + EPS) * g_ref[...]

    @pl.when(i < npt)
    def _():
        op_ref[...] = out

    @pl.when(i >= npt)
    def _():
        os_ref[...] = out


def _expert_kernel(be_ref, nv_ref, x_ref, wg_ref, wu_ref, wd_ref, y_ref, wg_sc, wu_sc, wd_sc):
    i = pl.program_id(0)

    @pl.when(jnp.logical_or(i == 0, be_ref[i] != be_ref[jnp.maximum(i - 1, 0)]))
    def _():
        wg_sc[...] = wg_ref[0, 0].astype(BF16)
        wu_sc[...] = wu_ref[0, 0].astype(BF16)
        wd_sc[...] = wd_ref[0, 0].astype(BF16)

    @pl.when(i < nv_ref[0])
    def _():
        x = x_ref[...].astype(BF16)
        hg = jnp.dot(x, wg_sc[...], preferred_element_type=F32)
        hu = jnp.dot(x, wu_sc[...], preferred_element_type=F32)
        hid = (_silu(hg) * hu).astype(BF16)
        y_ref[...] = _pack_bf16_pairs(jnp.dot(hid, wd_sc[...], preferred_element_type=F32))

    @pl.when(i >= nv_ref[0])
    def _():
        y_ref[...] = jnp.zeros(y_ref.shape, y_ref.dtype)


def _experts(blk_e, nvalid, xb, layer, wg, wu, wd):
    p_len, d = xb.shape
    hid = wg.shape[3]
    nblk = p_len // MOE_BLK
    grid_spec = pltpu.PrefetchScalarGridSpec(
        num_scalar_prefetch=2,
        grid=(nblk,),
        in_specs=[pl.BlockSpec((MOE_BLK, d), lambda i, be, nv: (jnp.minimum(i, nv[0] - 1), 0)),
                  pl.BlockSpec((1, 1, d, hid), lambda i, be, nv: (layer, be[i], 0, 0)),
                  pl.BlockSpec((1, 1, d, hid), lambda i, be, nv: (layer, be[i], 0, 0)),
                  pl.BlockSpec((1, 1, hid, d), lambda i, be, nv: (layer, be[i], 0, 0))],
        out_specs=pl.BlockSpec((MOE_BLK, d // 2), lambda i, be, nv: (i, 0)),
        scratch_shapes=[pltpu.VMEM((d, hid), BF16), pltpu.VMEM((d, hid), BF16), pltpu.VMEM((hid, d), BF16)],
    )
    return pl.pallas_call(
        _expert_kernel,
        grid_spec=grid_spec,
        out_shape=jax.ShapeDtypeStruct((p_len, d // 2), jnp.uint32),
        compiler_params=_cparams(("arbitrary",)),
        name="moe_experts",
    )(blk_e, nvalid, xb, wg, wu, wd)


def _moe(h2, info, cnt, layer, wg, wu, wd):
    n, d = h2.shape
    cnt = cnt[:, :, 0].astype(jnp.int32)
    counts = jnp.sum(cnt, axis=0)
    pc = ((counts + MOE_BLK - 1) // MOE_BLK) * MOE_BLK
    pend = jnp.cumsum(pc)
    base = (pend - pc)[None, :] + jnp.cumsum(cnt, axis=0) - cnt
    eid = info[:, 0:2, :].astype(jnp.int32)
    rank = info[:, 4:6, :].astype(jnp.int32)
    sel = eid[..., None] == jnp.arange(N_EXPERTS, dtype=jnp.int32)
    dest = jnp.sum(jnp.where(sel, base[:, None, None, :], 0), axis=-1) + rank
    p_len = n * TOP_K + N_EXPERTS * MOE_BLK
    nblk = p_len // MOE_BLK
    blk_start = jnp.arange(nblk, dtype=jnp.int32) * MOE_BLK
    blk_e = jnp.minimum(jnp.sum(pend[None, :] <= blk_start[:, None], axis=1), N_EXPERTS - 1).astype(jnp.int32)
    nvalid = (pend[-1:] // MOE_BLK).astype(jnp.int32)
    dest3 = dest.reshape(dest.shape[0], 1, TOP_K * TM)
    wcol = jnp.transpose(info[:, 2:2 + TOP_K, :], (0, 2, 1)).reshape(n, TOP_K)
    xb = _dispatch(dest3, pend.astype(jnp.int32), h2, p_len)
    yb = _experts(blk_e, nvalid, xb, layer, wg, wu, wd)
    return dest3, wcol, yb


def _rope_tables(t_len, qw):
    rows = t_len // GRID_W
    ax = qw // (DIFF_HEADS * 2) // 2
    row = jnp.repeat(jnp.arange(rows, dtype=F32), GRID_W)
    col = jnp.tile(jnp.arange(GRID_W, dtype=F32), rows)
    inv = ROPE_BASE ** (-jnp.arange(0, ax, 2, dtype=F32) / ax)
    ar = row[:, None] * inv[None, :]
    ac = col[:, None] * inv[None, :]
    cr, sr, cc, sc = jnp.cos(ar), jnp.sin(ar), jnp.cos(ac), jnp.sin(ac)
    cos = jnp.concatenate([cr, cr, cc, cc], axis=1)
    sins = jnp.concatenate([-sr, sr, -sc, sc], axis=1)
    reps = qw // cos.shape[1]
    cos = jnp.concatenate([jnp.ones((TM, cos.shape[1]), F32), cos], axis=0)
    sins = jnp.concatenate([jnp.zeros((TM, sins.shape[1]), F32), sins], axis=0)
    return jnp.tile(cos, (1, reps)), jnp.tile(sins, (1, reps))


def kernel(x_prompt, x_sample, cache_k, cache_v, c, c_ctx, w_ada, b_ada, g_norm1, g_norm2, g_final, w_in_e, conv_a_w, conv_a_b, ln_a_g, ln_a_b, lam_q1, lam_k1, lam_q2, lam_k2, subln_g, w_out_e, w_in_o, pool_w, pool_scale, conv_d_w, w_out_o, w_router_g, b_router_g, w_router_e, b_router_e, w_gate, w_up, w_down):
    bp, tp, d = x_prompt.shape
    bs, ts, _ = x_sample.shape
    past = cache_k.shape[2]
    assert tp == TM and ts % TM == 0 and d % LANES == 0
    npt = bp
    tps = ts // TM
    n_p, n_s = bp * tp, bs * ts
    n = n_p + n_s
    nt = n // TM
    cw = conv_a_w.shape[2]
    qw = (w_in_e.shape[2] - 2 * cw) // 3
    dh = qw // (DIFF_HEADS * 2)
    hpt = TM // HALO

    def cond_of(i):
        return jnp.where(i < npt, 0, 1 + (i - npt) // tps)

    def tile(w):
        return pl.BlockSpec((TM, w), lambda i: (i, 0))

    def halo_prev(w):
        return pl.BlockSpec((HALO, w), lambda i: (jnp.maximum(i * hpt - 1, 0), 0))

    def halo_next(w):
        return pl.BlockSpec((HALO, w), lambda i: (jnp.minimum((i + 1) * hpt, nt * hpt - 1), 0))

    def whole(shape):
        return pl.BlockSpec(shape, lambda i: (0,) * len(shape))

    mod_spec = pl.BlockSpec((1, 6, d), lambda i: (cond_of(i), 0, 0))
    combine_specs = [pl.BlockSpec((1, 1, TOP_K * TM), lambda i: (i, 0, 0), memory_space=pltpu.SMEM),
                     pl.BlockSpec((1, 1, TOP_K * TM), lambda i: (jnp.minimum(i + 1, nt - 1), 0, 0),
                                  memory_space=pltpu.SMEM),
                     tile(TOP_K), pl.BlockSpec(memory_space=pl.ANY)]
    combine_scratch = [pltpu.VMEM((2, TOP_K, TM // SUBLANES, SUBLANES, d // 2), jnp.uint32),
                       pltpu.SemaphoreType.DMA((2,))]

    rows = SUBLANES * pl.cdiv(1 + bs, SUBLANES)
    cond = jnp.concatenate([c_ctx[None, :], c, jnp.zeros((rows - 1 - bs, d), F32)], axis=0)
    mod = _ada_table(cond, w_ada, b_ada)

    xp2, xs2 = x_prompt.reshape(n_p, d), x_sample.reshape(n_s, d)

    def prompt_tile(w):
        return pl.BlockSpec((TM, w), lambda i: (jnp.minimum(i, npt - 1), 0))

    def sample_tile(w):
        return pl.BlockSpec((TM, w), lambda i: (jnp.maximum(i - npt, 0), 0))

    def router_w(i):
        wre = jnp.transpose(w_router_e[i], (1, 0, 2)).reshape(d, N_EXPERTS)
        wr = jnp.concatenate([wre, w_router_g[i]], axis=1)
        return jnp.pad(wr, ((0, 0), (0, ROUTER_PAD - wr.shape[1]))).astype(BF16)

    def router_b(i):
        b = jnp.concatenate([b_router_e[i].reshape(N_EXPERTS), b_router_g[i]])
        return jnp.pad(b, (0, ROUTER_PAD - b.shape[0]))[:, None]

    cos, sins = _rope_tables(ts, qw)
    tab_spec = pl.BlockSpec((TM, qw), lambda i: (jnp.where(i < npt, 0, 1 + (i - npt) % tps), 0))
    kc_spec = pl.BlockSpec((1, 1, TM, DIFF_HEADS, 2, dh), lambda i: (jnp.minimum(i, npt - 1), 0, 0, 0, 0, 0))
    vc_spec = pl.BlockSpec((1, 1, TM, DIFF_HEADS, 2 * dh), lambda i: (jnp.minimum(i, npt - 1), 0, 0, 0, 0))
    u, q, k, vt, new_k, new_v = pl.pallas_call(
        functools.partial(_pre_even_kernel, npt=npt, cw=cw, qw=qw, qscale=dh ** -0.5 * math.log2(math.e)),
        grid=(nt,),
        in_specs=[prompt_tile(d), sample_tile(d), mod_spec, whole((1, d)), whole(w_in_e.shape[1:]),
                  tab_spec, tab_spec],
        out_specs=[tile(cw), tile(qw), tile(qw), pl.BlockSpec((qw, TM), lambda i: (0, i)),
                   kc_spec, vc_spec],
        out_shape=[jax.ShapeDtypeStruct((n, cw), F32), jax.ShapeDtypeStruct((n, qw), BF16),
                   jax.ShapeDtypeStruct((n, qw), BF16), jax.ShapeDtypeStruct((qw, n), BF16),
                   jax.ShapeDtypeStruct((bp, 1, tp, DIFF_HEADS, 2, dh), F32),
                   jax.ShapeDtypeStruct((bp, 1, tp, DIFF_HEADS, 2 * dh), F32)],
        scratch_shapes=[pltpu.VMEM((2, TM, qw), F32)],
        compiler_params=_cparams(("arbitrary",)),
        name="pre_even",
    )(xp2, xs2, mod[0], g_norm1[0][None, :], w_in_e[0].astype(BF16), cos, sins)

    lam_init = 0.8 - 0.6 * math.exp(-0.3 * 0)
    lam = (jnp.exp(jnp.sum(lam_q1[0] * lam_k1[0])) - jnp.exp(jnp.sum(lam_q2[0] * lam_k2[0])) + lam_init)
    lam = lam.reshape(1).astype(F32)
    sub_g = subln_g[0][:, None]
    o_p = _attention(lam, q, k, vt, sub_g, None, batch=bp, t_len=tp, row0=0, tq=TM, tk=TM,
                     post_scale=1.0 - lam_init, heads_per_step=DIFF_HEADS)
    k_ctx = cache_k[:, 0].reshape(bs * past, qw).astype(BF16)
    vt_ctx = jnp.transpose(cache_v[:, 0].reshape(bs, past, qw), (0, 2, 1)).astype(BF16)
    o_s = _attention(lam, q, k, vt, sub_g, (k_ctx, vt_ctx), batch=bs, t_len=ts, row0=n_p, tq=ATTN_TQ, tk=ATTN_TK,
                     post_scale=1.0 - lam_init)

    moe_outs = [jax.ShapeDtypeStruct((n, d), F32), jax.ShapeDtypeStruct((n, d), F32),
                jax.ShapeDtypeStruct((nt, INFO_ROWS, TM), F32), jax.ShapeDtypeStruct((nt, N_EXPERTS, 1), F32)]
    moe_out_specs = [tile(d), tile(d), pl.BlockSpec((1, INFO_ROWS, TM), lambda i: (i, 0, 0)),
                     pl.BlockSpec((1, N_EXPERTS, 1), lambda i: (i, 0, 0))]
    x1, h2, info, cnt = pl.pallas_call(
        functools.partial(_post_even_kernel, npt=npt, tps=tps),
        grid=(nt,),
        in_specs=[prompt_tile(d), sample_tile(d), tile(cw), halo_prev(cw), halo_next(cw),
                  prompt_tile(qw), sample_tile(qw), mod_spec,
                  whole((CONV_K, cw)), whole((1, cw)), whole((1, cw)), whole((1, cw)),
                  whole(w_out_e.shape[1:]), whole((1, d)), whole((d, ROUTER_PAD)), whole((ROUTER_PAD, 1))],
        out_specs=moe_out_specs,
        out_shape=moe_outs,
        scratch_shapes=[pltpu.VMEM((TM + 2 * HALO, cw), F32),
                        pltpu.VMEM((SUBLANES, TM + 2 * HALO - SUBLANES, cw), F32)],
        compiler_params=_cparams(("parallel",)),
        name="post_even",
    )(xp2, xs2, u, u, u, o_p, o_s, mod[0], conv_a_w[0], conv_a_b[0][None, :], ln_a_g[0][None, :], ln_a_b[0][None, :],
      w_out_e[0].astype(BF16), g_norm2[0][None, :], router_w(0), router_b(0))
    dest3, wcol, yb = _moe(h2, info, cnt, 0, w_gate, w_up, w_down)

    pw = pool_w.shape[2]
    x2, xc, p, bg = pl.pallas_call(
        functools.partial(_pre_odd_kernel, cw=cw),
        grid=(nt,),
        in_specs=[tile(d)] + combine_specs + [mod_spec, mod_spec, whole((1, d)), whole(w_in_o.shape[1:])],
        out_specs=[tile(d), tile(cw), tile(cw), tile(cw)],
        out_shape=[jax.ShapeDtypeStruct((n, d), F32)] + [jax.ShapeDtypeStruct((n, cw), F32)] * 3,
        scratch_shapes=combine_scratch,
        compiler_params=_cparams(("arbitrary",)),
        name="pre_odd",
    )(x1, dest3, dest3, wcol, yb, mod[0], mod[1], g_norm1[1][None, :], w_in_o[0].astype(BF16))
    pool_bd = jnp.zeros((cw, cw), F32)
    for g in range(len(POOL_WINDOWS)):
        pool_bd = pool_bd.at[g * pw:(g + 1) * pw, g * pw:(g + 1) * pw].set(pool_w[0, g])
    x3, h2, info, cnt = pl.pallas_call(
        functools.partial(_post_odd_kernel, npt=npt, tps=tps),
        grid=(nt,),
        in_specs=[tile(d), tile(cw), halo_prev(cw), halo_next(cw), tile(cw), halo_prev(cw), halo_next(cw),
                  tile(cw), mod_spec, whole((cw, cw)), whole((1, cw)), whole((SCONV_K, cw)),
                  whole(w_out_o.shape[1:]), whole((1, d)), whole((d, ROUTER_PAD)), whole((ROUTER_PAD, 1))],
        out_specs=moe_out_specs,
        out_shape=moe_outs,
        scratch_shapes=[pltpu.VMEM((TM + 2 * HALO, cw), F32), pltpu.VMEM((TM + 2 * HALO, cw), F32)],
        compiler_params=_cparams(("parallel",)),
        name="post_odd",
    )(x2, xc, xc, xc, p, p, p, bg, mod[1], pool_bd.astype(BF16), pool_scale[0][None, :], conv_d_w[0],
      w_out_o[0].astype(BF16), g_norm2[1][None, :], router_w(1), router_b(1))
    dest3, wcol, yb = _moe(h2, info, cnt, 1, w_gate, w_up, w_down)

    out_p, out_s = pl.pallas_call(
        functools.partial(_final_kernel, npt=npt),
        grid=(nt,),
        in_specs=[tile(d)] + combine_specs + [mod_spec, whole((1, d))],
        out_specs=[pl.BlockSpec((TM, d), lambda i: (jnp.minimum(i, npt - 1), 0)),
                   pl.BlockSpec((TM, d), lambda i: (jnp.maximum(i - npt, 0), 0))],
        out_shape=[jax.ShapeDtypeStruct((n_p, d), F32), jax.ShapeDtypeStruct((n_s, d), F32)],
        scratch_shapes=combine_scratch,
        compiler_params=_cparams(("arbitrary",)),
        name="final_norm",
    )(x3, dest3, dest3, wcol, yb, mod[1], g_final[None, :])

    return (out_p.reshape(bp, tp, d), out_s.reshape(bs, ts, d), new_k, new_v)
```

```python
import functools
import math

import jax
import jax.numpy as jnp
from jax import lax
from jax.experimental import pallas as pl
from jax.experimental.pallas import tpu as pltpu

F32 = jnp.float32
BF16 = jnp.bfloat16

EPS = 1e-6
GRID_W = 64
ROPE_BASE = 10000.0
CONV_K = 31
SCONV_K = 3
POOL_WINDOWS = (2, 4, 8, 16)
DIFF_HEADS = 4
N_GROUPS = 4
EXPERTS_PER_GROUP = 8
N_EXPERTS = N_GROUPS * EXPERTS_PER_GROUP
TOP_K = 2

LANES = 128
SUBLANES = 8
TM = 256
HALO = 16
PROJ_ROW_PARTS = 2
MIX_ROW_PARTS = 1
PROJ_COLS = 256
MOE_BLK = 512
ATTN_TQ = 2048
ATTN_TK = 512
CONV_ROWS = 32
INFO_ROWS = 8
ROUTER_PAD = LANES
ONES_ROWS = 2 * SUBLANES
DISPATCH_SLOTS = 3
VMEM_LIMIT = 56 * 1024 * 1024


def _cparams(sem):
    return pltpu.CompilerParams(dimension_semantics=sem, vmem_limit_bytes=VMEM_LIMIT)


def _rms_mod(x, g, sc, sh):
    ms = jnp.mean(x * x, axis=-1, keepdims=True)
    return (x * lax.rsqrt(ms + EPS)) * g * (1.0 + sc) + sh


def _silu(x):
    return x * jax.nn.sigmoid(x)


def _pack_bf16_pairs(y):
    half = y.shape[1] // 2
    lo = pltpu.bitcast(y[:, :half].astype(BF16).astype(F32), jnp.uint32)
    hi = pltpu.bitcast(y[:, half:].astype(BF16).astype(F32), jnp.uint32)
    return lax.shift_right_logical(lo, jnp.uint32(16)) | hi


def _unpack_bf16_pairs(w):
    lo = pltpu.bitcast(lax.shift_left(w, jnp.uint32(16)), F32)
    hi = pltpu.bitcast(w & jnp.uint32(0xFFFF0000), F32)
    return jnp.concatenate([lo, hi], axis=1)


def _ada_kernel(c_ref, w_ref, b_ref, o_ref):
    cs = _silu(c_ref[...])
    o_ref[0] = jnp.dot(cs, w_ref[0], precision=lax.Precision.HIGHEST,
                       preferred_element_type=F32) + b_ref[0]


def _ada_table(cond, w_ada, b_ada):
    depth, d, six_d = w_ada.shape
    rows = cond.shape[0]
    nj = six_d // d
    out = pl.pallas_call(
        _ada_kernel,
        grid=(depth, nj),
        in_specs=[pl.BlockSpec((rows, d), lambda l, j: (0, 0)),
                  pl.BlockSpec((1, d, d), lambda l, j: (l, 0, j)),
                  pl.BlockSpec((1, 1, d), lambda l, j: (l, 0, j))],
        out_specs=pl.BlockSpec((1, rows, d), lambda l, j: (l, 0, j)),
        out_shape=jax.ShapeDtypeStruct((depth, rows, six_d), F32),
        compiler_params=_cparams(("parallel", "parallel")),
        name="ada_table",
    )(cond, w_ada, b_ada.reshape(depth, 1, six_d))
    return out.reshape(depth, rows, nj, d)


def _row_parts(parts):
    rows = TM // parts
    return [slice(p * rows, (p + 1) * rows) for p in range(parts)]


def _rope(x, cos, sins, half):
    lane = lax.broadcasted_iota(jnp.int32, (1, LANES), 1)
    first_half = (lane % (2 * half)) < half
    outs = []
    for j in range(x.shape[1] // LANES):
        sl = slice(j * LANES, (j + 1) * LANES)
        xs = x[:, sl]
        nxt = pltpu.roll(xs, LANES - half, 1)
        prv = pltpu.roll(xs, half, 1)
        rot = jnp.where(first_half, nxt, prv)
        outs.append(xs * cos[:, sl] + rot * sins[:, sl])
    return jnp.concatenate(outs, axis=1)


def _pre_even_kernel(xp_ref, xs_ref, mod_ref, g_ref, w_ref, cos_ref, sin_ref,
                     u_ref, q_ref, k_ref, vt_ref, kc_ref, vc_ref, kv_sc, *, npt, cw, qw, qscale):
    i = pl.program_id(0)
    is_prompt = i < npt
    half = kc_ref.shape[-1] // 4
    for sl in _row_parts(PROJ_ROW_PARTS):
        x = jnp.where(is_prompt, xp_ref[sl, :], xs_ref[sl, :])
        h = _rms_mod(x, g_ref[...], mod_ref[0, 1:2, :], mod_ref[0, 0:1, :])
        proj = jnp.dot(h.astype(BF16), w_ref[...], preferred_element_type=F32)
        ga = proj[:, :cw]
        gb = proj[:, cw:2 * cw]
        u_ref[sl, :] = ga * jax.nn.sigmoid(gb)
        q = proj[:, 2 * cw:2 * cw + qw]
        k = proj[:, 2 * cw + qw:2 * cw + 2 * qw]
        v = proj[:, 2 * cw + 2 * qw:]
        kv_sc[0, sl, :] = k
        kv_sc[1, sl, :] = v
        cos = cos_ref[sl, :]
        sins = sin_ref[sl, :]
        q_ref[sl, :] = (_rope(q, cos, sins, half) * qscale).astype(BF16)
        k_ref[sl, :] = _rope(k, cos, sins, half).astype(BF16)
        vt_ref[:, sl] = v.T.astype(BF16)

    @pl.when(is_prompt)
    def _():
        heads, halves, dh = kc_ref.shape[3:]
        for h in range(heads):
            vc_ref[0, 0, :, h, :] = kv_sc[1, :, h * halves * dh:(h + 1) * halves * dh]
            for c in range(halves):
                lo = (h * halves + c) * dh
                kc_ref[0, 0, :, h, c, :] = kv_sc[0, :, lo:lo + dh]


def _attn_kernel(lam_ref, q_ref, k_ref, vt_ref, *rest, nk, tk, nk_ctx, post_scale):
    if nk_ctx:
        kc_ref, vtc_ref, g_ref, o_ref, s_sc, mc_sc, m_sc, acc_sc = rest
    else:
        g_ref, o_ref, s_sc, mc_sc, m_sc, acc_sc = rest
    q = q_ref[...]
    hw = q.shape[1]
    dh = hw // 2
    lane = lax.broadcasted_iota(jnp.int32, (1, hw), 1)
    zero = jnp.zeros_like(q)
    qs = (jnp.where(lane < dh, q, zero), jnp.where(lane >= dh, q, zero))
    m_sc[...] = jnp.full(m_sc.shape, -jnp.inf, F32)
    acc_sc[...] = jnp.zeros(acc_sc.shape, F32)
    ones_rows = (lax.broadcasted_iota(jnp.int32, (ONES_ROWS, tk), 0) == 0).astype(BF16)

    def scores(kk, slot):
        for c in range(2):
            s = lax.dot_general(kk, qs[c], (((1,), (1,)), ((), ())), preferred_element_type=F32)
            s_sc[slot, c] = s
            mc_sc[slot, c] = jnp.max(s, axis=0, keepdims=True)

    def consume(vt, slot):
        vta = jnp.concatenate([vt, ones_rows], axis=0)
        for c in range(2):
            m_prev = m_sc[c]
            m_new = jnp.maximum(m_prev, mc_sc[slot, c])
            alpha = jnp.exp2(m_prev - m_new)
            p = jnp.exp2(s_sc[slot, c] - m_new).astype(BF16)
            acc_sc[c] = alpha * acc_sc[c] + jnp.dot(vta, p, preferred_element_type=F32)
            m_sc[c] = m_new

    def k_at(i):
        if i < nk_ctx:
            return kc_ref[i * tk:(i + 1) * tk, :]
        return k_ref[(i - nk_ctx) * tk:(i - nk_ctx + 1) * tk, :]

    def vt_at(i):
        if i < nk_ctx:
            return vtc_ref[0, :, i * tk:(i + 1) * tk]
        return vt_ref[:, (i - nk_ctx) * tk:(i - nk_ctx + 1) * tk]

    n_total = nk_ctx + nk
    peel = nk_ctx + (n_total - 1 - nk_ctx) % 2
    scores(k_at(0), 0)
    for i in range(peel):
        scores(k_at(i + 1), (i + 1) % 2)
        consume(vt_at(i), i % 2)

    def body(t, carry):
        for r in range(2):
            j = 2 * t + (peel + r - nk_ctx)
            nxt = pl.multiple_of((j + 1) * tk, tk)
            cur = pl.multiple_of(j * tk, tk)
            scores(k_ref[pl.ds(nxt, tk), :], (peel + r + 1) % 2)
            consume(vt_ref[:, pl.ds(cur, tk)], (peel + r) % 2)
        return carry

    lax.fori_loop(0, (n_total - 1 - peel) // 2, body, 0)
    consume(vt_at(n_total - 1), (n_total - 1) % 2)
    o = (acc_sc[0, :hw, :] / acc_sc[0, hw:hw + 1, :]
         - lam_ref[0] * (acc_sc[1, :hw, :] / acc_sc[1, hw:hw + 1, :]))
    ms = jnp.mean(o * o, axis=0, keepdims=True)
    o = o * lax.rsqrt(ms + EPS) * g_ref[...] * post_scale
    o_ref[...] = o.T.astype(BF16)


def _attn_heads_kernel(lam_ref, q_ref, k_ref, vt_ref, g_ref, o_ref, *scratch, heads, **kw):
    hw = q_ref.shape[1] // heads
    for h in range(heads):
        cols = pl.ds(h * hw, hw)
        _attn_kernel(lam_ref, q_ref.at[:, cols], k_ref.at[:, cols], vt_ref.at[cols, :], g_ref, o_ref.at[:, cols],
                     *scratch, **kw)


def _attention(lam, q, k, vt, g_col, ctx, *, batch, t_len, row0, tq, tk, post_scale, heads_per_step=1):
    hw = q.shape[1] // DIFF_HEADS
    bw = hw * heads_per_step
    nq = t_len // tq
    assert row0 % t_len == 0 and t_len % tk == 0 and (ctx is None or heads_per_step == 1)
    qb0 = row0 // tq
    sb0 = row0 // t_len
    in_specs = [pl.BlockSpec(memory_space=pltpu.SMEM),
                pl.BlockSpec((tq, bw), lambda b, h, i: (qb0 + b * nq + i, h)),
                pl.BlockSpec((t_len, bw), lambda b, h, i: (sb0 + b, h)),
                pl.BlockSpec((bw, t_len), lambda b, h, i: (h, sb0 + b))]
    args = [lam, q, k, vt]
    nk_ctx = 0
    if ctx is not None:
        k_ctx, vt_ctx = ctx
        past = vt_ctx.shape[2]
        assert past % tk == 0
        nk_ctx = past // tk
        in_specs += [pl.BlockSpec((past, hw), lambda b, h, i: (b, h)),
                     pl.BlockSpec((1, hw, past), lambda b, h, i: (b, h, 0))]
        args += [k_ctx, vt_ctx]
    in_specs.append(pl.BlockSpec((hw, 1), lambda b, h, i: (0, 0)))
    args.append(g_col)
    body = functools.partial(_attn_kernel, nk=t_len // tk, tk=tk, nk_ctx=nk_ctx, post_scale=post_scale)
    if heads_per_step > 1:
        body = functools.partial(_attn_heads_kernel, heads=heads_per_step, nk=t_len // tk, tk=tk, nk_ctx=nk_ctx,
                                 post_scale=post_scale)
    return pl.pallas_call(
        body,
        grid=(batch, DIFF_HEADS // heads_per_step, nq),
        in_specs=in_specs,
        out_specs=pl.BlockSpec((tq, bw), lambda b, h, i: (b * nq + i, h)),
        out_shape=jax.ShapeDtypeStruct((batch * t_len, q.shape[1]), BF16),
        scratch_shapes=[pltpu.VMEM((2, 2, tk, tq), F32), pltpu.VMEM((2, 2, 1, tq), F32),
                        pltpu.VMEM((2, 1, tq), F32), pltpu.VMEM((2, hw + ONES_ROWS, tq), F32)],
        compiler_params=_cparams(("parallel", "parallel", "parallel")),
        name="diff_attention",
    )(*args)


def _seq_edges(i, npt, tps):
    r = (i - npt) % tps
    is_prompt = i < npt
    first = jnp.logical_or(is_prompt, r == 0)
    last = jnp.logical_or(is_prompt, r == tps - 1)
    return first, last


def _fill_ext(ext_sc, cur_ref, prev_ref, next_ref, first, last):
    prev = prev_ref[...]
    nxt = next_ref[...]
    ext_sc[0:HALO, :] = jnp.where(first, jnp.zeros_like(prev), prev)
    ext_sc[HALO:HALO + TM, :] = cur_ref[...]
    ext_sc[HALO + TM:, :] = jnp.where(last, jnp.zeros_like(nxt), nxt)


def _route_tile(logits, b_ref, info_ref, cnt_ref):
    lt = logits.T + b_ref[...]
    gl = lt[N_EXPERTS:N_EXPERTS + N_GROUPS]
    ge = jnp.exp(gl - jnp.max(gl, axis=0, keepdims=True))
    gp = ge / jnp.sum(ge, axis=0, keepdims=True)
    best = gp[0:1]
    gidx = jnp.zeros(best.shape, jnp.int32)
    el = lt[0:EXPERTS_PER_GROUP]
    for g in range(1, N_GROUPS):
        better = gp[g:g + 1] > best
        gidx = jnp.where(better, g, gidx)
        best = jnp.where(better, gp[g:g + 1], best)
    for g in range(1, N_GROUPS):
        el = jnp.where(gidx == g, lt[g * EXPERTS_PER_GROUP:(g + 1) * EXPERTS_PER_GROUP], el)
    rio = lax.broadcasted_iota(jnp.int32, el.shape, 0)
    v0 = jnp.max(el, axis=0, keepdims=True)
    i0 = jnp.min(jnp.where(el == v0, rio, EXPERTS_PER_GROUP), axis=0, keepdims=True)
    el2 = jnp.where(rio == i0, -jnp.inf, el)
    v1 = jnp.max(el2, axis=0, keepdims=True)
    i1 = jnp.min(jnp.where(el2 == v1, rio, EXPERTS_PER_GROUP), axis=0, keepdims=True)
    e1 = jnp.exp(v1 - v0)
    den = 1.0 + e1
    w0 = (1.0 / den) * best
    w1 = (e1 / den) * best
    eid0 = gidx * EXPERTS_PER_GROUP + i0
    eid1 = gidx * EXPERTS_PER_GROUP + i1
    tm = lt.shape[1]
    eio = lax.broadcasted_iota(jnp.int32, (N_EXPERTS, tm), 0)
    oh0 = eio == eid0
    oh1 = eio == eid1
    ohs = jnp.where(oh0, 1.0, 0.0) + jnp.where(oh1, 1.0, 0.0)
    upper = (lax.broadcasted_iota(jnp.int32, (tm, tm), 0)
             < lax.broadcasted_iota(jnp.int32, (tm, tm), 1)).astype(BF16)
    before = jnp.dot(ohs.astype(BF16), upper, preferred_element_type=F32)
    rank0 = jnp.sum(jnp.where(oh0, before, 0.0), axis=0, keepdims=True)
    rank1 = jnp.sum(jnp.where(oh1, before, 0.0), axis=0, keepdims=True)
    info_ref[0] = jnp.concatenate([eid0.astype(F32), eid1.astype(F32), w0, w1, rank0, rank1,
                                   jnp.zeros((INFO_ROWS - 6, tm), F32)], axis=0)
    cnt_ref[0] = jnp.sum(ohs, axis=1, keepdims=True)


def _moe_prologue(x_new, sl, mod_ref, g2_ref, wr_ref, rb_ref, h2_ref, info_ref, cnt_ref):
    h2 = _rms_mod(x_new, g2_ref[...], mod_ref[0, 4:5, :], mod_ref[0, 3:4, :])
    h2_ref[sl, :] = _pack_bf16_pairs(h2)
    assert sl.stop - sl.start == TM
    _route_tile(jnp.dot(h2.astype(BF16), wr_ref[...], preferred_element_type=F32), rb_ref, info_ref, cnt_ref)


def _post_even_kernel(xp_ref, xs_ref, u_ref, up_ref, un_ref, op_ref, os_ref, mod_ref, cw_ref, cb_ref,
                      lng_ref, lnb_ref, wo_ref, g2_ref, wr_ref, rb_ref, x1_ref, h2_ref, info_ref, cnt_ref,
                      ext_sc, sh_sc, *, npt, tps):
    i = pl.program_id(0)
    first, last = _seq_edges(i, npt, tps)
    _fill_ext(ext_sc, u_ref, up_ref, un_ref, first, last)
    cw = u_ref.shape[1]
    span = sh_sc.shape[1]
    for r in range(SUBLANES):
        sh_sc[r] = ext_sc[r:r + span, :]
    rows = CONV_ROWS
    is_prompt = i < npt
    for sl in _row_parts(MIX_ROW_PARTS):
        chunks = []
        for r0 in range(sl.start, sl.stop, rows):
            acc = None
            for k in range(CONV_K):
                off = HALO - CONV_K // 2 + k
                start = r0 + off - off % SUBLANES
                term = sh_sc[off % SUBLANES, start:start + rows, :] * cw_ref[k:k + 1, :]
                acc = term if acc is None else acc + term
            chunks.append(acc)
        conv = jnp.concatenate(chunks, axis=0) + cb_ref[...]
        mu = jnp.mean(conv, axis=-1, keepdims=True)
        cen = conv - mu
        var = jnp.mean(cen * cen, axis=-1, keepdims=True)
        uu = _silu(cen * lax.rsqrt(var + EPS) * lng_ref[...] + lnb_ref[...])
        o = jnp.where(is_prompt, op_ref[sl, :], os_ref[sl, :])
        m = (jnp.dot(uu.astype(BF16), wo_ref[0:cw, :], preferred_element_type=F32)
             + jnp.dot(o, wo_ref[cw:, :], preferred_element_type=F32))
        x1 = jnp.where(is_prompt, xp_ref[sl, :], xs_ref[sl, :]) + mod_ref[0, 2:3, :] * m
        x1_ref[sl, :] = x1
        _moe_prologue(x1, sl, mod_ref, g2_ref, wr_ref, rb_ref, h2_ref, info_ref, cnt_ref)


def _gather_start(dest_ref, yb_hbm, ybuf, sem, slot):
    def body(i, carry):
        for j in range(SUBLANES):
            for k in range(TOP_K):
                row = dest_ref[0, 0, k * TM + i * SUBLANES + j]
                pltpu.make_async_copy(yb_hbm.at[pl.ds(row, 1)], ybuf.at[slot, k, i, pl.ds(j, 1)],
                                      sem.at[slot]).start(priority=k % 2)
        return carry

    lax.fori_loop(0, TM // SUBLANES, body, 0)


def _moe_gather(dcur_ref, dnxt_ref, yb_hbm, ybuf, sem, prefetch_next):
    t = pl.program_id(0)
    slot = t % 2

    @pl.when(t == 0)
    def _():
        _gather_start(dcur_ref, yb_hbm, ybuf, sem, 0)

    if prefetch_next:
        @pl.when(t + 1 < pl.num_programs(0))
        def _():
            _gather_start(dnxt_ref, yb_hbm, ybuf, sem, 1 - slot)

    _gather_wait(ybuf, sem, slot)
    return slot


def _gather_wait(ybuf, sem, slot):
    for k in range(TOP_K):
        pltpu.make_async_copy(ybuf.at[slot, k], ybuf.at[slot, k], sem.at[slot]).wait()


def _moe_prefetch_group(dnxt_ref, yb_hbm, ybuf, sem, slot, g, n_groups):
    n_i = TM // SUBLANES
    for i in range(g * n_i // n_groups, (g + 1) * n_i // n_groups):
        for j in range(SUBLANES):
            for k in range(TOP_K):
                row = dnxt_ref[0, 0, k * TM + i * SUBLANES + j]
                pltpu.make_async_copy(yb_hbm.at[pl.ds(row, 1)], ybuf.at[1 - slot, k, i, pl.ds(j, 1)],
                                      sem.at[1 - slot]).start(priority=k % 2)


def _moe_prefetch_finish(ybuf, sem, slot):
    @pl.when(pl.program_id(0) == pl.num_programs(0) - 1)
    def _():
        _gather_wait(ybuf, sem, 1 - slot)


def _combined_rows(cw_ref, ybuf, slot, sl):
    dw = ybuf.shape[-1]
    g = slice(sl.start // SUBLANES, sl.stop // SUBLANES)
    w = cw_ref[sl, :]
    y = w[:, 0:1] * _unpack_bf16_pairs(ybuf[slot, 0, g].reshape(sl.stop - sl.start, dw))
    for k in range(1, TOP_K):
        y = y + w[:, k:k + 1] * _unpack_bf16_pairs(ybuf[slot, k, g].reshape(sl.stop - sl.start, dw))
    return y


def _dispatch_kernel(dest_ref, pend_ref, h2_hbm, xb_hbm, stage, zeros, in_sem, row_sem, zero_sem):
    t = pl.program_id(0)
    nt = pl.num_programs(0)
    slot = t % DISPATCH_SLOTS

    tile_rows = TM // SUBLANES

    def stage_in(tile_idx, s):
        return pltpu.make_async_copy(h2_hbm.at[pl.ds(tile_idx * tile_rows, tile_rows)], stage.at[s], in_sem.at[s])

    def drain_rows(s):
        for _ in range(TOP_K):
            pltpu.make_async_copy(stage.at[s], stage.at[s], row_sem.at[s]).wait()

    def zero_block(start):
        return pltpu.make_async_copy(zeros, xb_hbm.at[pl.ds(pl.multiple_of(start, MOE_BLK), MOE_BLK)], zero_sem)

    def zero_fill(op):
        for e in range(N_EXPERTS):
            op(zero_block(jnp.maximum(pend_ref[e] - MOE_BLK, 0)))
        for j in range(N_EXPERTS):
            start = pend_ref[N_EXPERTS - 1] + j * MOE_BLK

            @pl.when(start < xb_hbm.shape[0])
            def _():
                op(zero_block(start))

    @pl.when(t == 0)
    def _():
        stage_in(0, 0).start()
        zeros[...] = jnp.zeros(zeros.shape, zeros.dtype)
        zero_fill(lambda cp: cp.start())
        zero_fill(lambda cp: cp.wait())

    @pl.when(jnp.logical_and(t == 0, nt > 1))
    def _():
        stage_in(1, 1).start()

    stage_in(t, slot).wait()

    def body(i, carry):
        for j in range(SUBLANES):
            for k in range(TOP_K):
                row = dest_ref[0, 0, k * TM + i * SUBLANES + j]
                pltpu.make_async_copy(stage.at[slot, i, pl.ds(j, 1)], xb_hbm.at[pl.ds(row, 1)],
                                      row_sem.at[slot]).start(priority=k % 2)
        return carry

    lax.fori_loop(0, tile_rows, body, 0)

    @pl.when(t > 0)
    def _():
        drain_rows((t + DISPATCH_SLOTS - 1) % DISPATCH_SLOTS)

    @pl.when(t + 2 < nt)
    def _():
        stage_in(t + 2, (t + 2) % DISPATCH_SLOTS).start()

    @pl.when(t == nt - 1)
    def _():
        drain_rows(slot)


def _dispatch(dest3, pend, h2, p_len):
    n, d = h2.shape
    return pl.pallas_call(
        _dispatch_kernel,
        grid=(n // TM,),
        in_specs=[pl.BlockSpec((1, 1, TOP_K * TM), lambda t: (t, 0, 0), memory_space=pltpu.SMEM),
                  pl.BlockSpec(memory_space=pltpu.SMEM), pl.BlockSpec(memory_space=pl.ANY)],
        out_specs=pl.BlockSpec(memory_space=pl.ANY),
        out_shape=jax.ShapeDtypeStruct((p_len, d), h2.dtype),
        scratch_shapes=[pltpu.VMEM((DISPATCH_SLOTS, TM // SUBLANES, SUBLANES, d), h2.dtype),
                        pltpu.VMEM((MOE_BLK, d), h2.dtype),
                        pltpu.SemaphoreType.DMA((DISPATCH_SLOTS,)), pltpu.SemaphoreType.DMA((DISPATCH_SLOTS,)),
                        pltpu.SemaphoreType.DMA(())],
        compiler_params=_cparams(("arbitrary",)),
        name="moe_dispatch",
    )(dest3, pend, h2.reshape(n // SUBLANES, SUBLANES, d))


def _pre_odd_kernel(x_ref, dcur_ref, dnxt_ref, cw_ref, yb_hbm, modp_ref, mod_ref, g_ref, w_ref,
                    x2_ref, xc_ref, p_ref, bg_ref, ybuf, sem, *, cw):
    slot = _moe_gather(dcur_ref, dnxt_ref, yb_hbm, ybuf, sem, prefetch_next=False)
    col_blocks = w_ref.shape[1] // PROJ_COLS
    n_groups = PROJ_ROW_PARTS * (1 + col_blocks)
    group = 0
    for sl in _row_parts(PROJ_ROW_PARTS):
        x2 = x_ref[sl, :] + modp_ref[0, 5:6, :] * _combined_rows(cw_ref, ybuf, slot, sl)
        x2_ref[sl, :] = x2
        _moe_prefetch_group(dnxt_ref, yb_hbm, ybuf, sem, slot, group, n_groups)
        group += 1
        h = _rms_mod(x2, g_ref[...], mod_ref[0, 1:2, :], mod_ref[0, 0:1, :]).astype(BF16)
        hd = []
        for c in range(col_blocks):
            cols = slice(c * PROJ_COLS, (c + 1) * PROJ_COLS)
            blk = jnp.dot(h, w_ref[:, cols], preferred_element_type=F32)
            which, off = divmod(c * PROJ_COLS, cw)
            dst = slice(off, off + PROJ_COLS)
            if which == 0:
                xc_ref[sl, dst] = blk
            elif which == 1:
                hd.append(blk)
            elif which == 2:
                bg_ref[sl, dst] = blk
            else:
                p_ref[sl, dst] = blk * hd[off // PROJ_COLS]
            _moe_prefetch_group(dnxt_ref, yb_hbm, ybuf, sem, slot, group, n_groups)
            group += 1
    _moe_prefetch_finish(ybuf, sem, slot)


def _post_odd_kernel(x_ref, xc_ref, xcp_ref, xcn_ref, p_ref, pp_ref, pn_ref, bg_ref, mod_ref,
                     pw_ref, ps_ref, dw_ref, wo_ref, g2_ref, wr_ref, rb_ref,
                     x3_ref, h2_ref, info_ref, cnt_ref, extc_sc, extp_sc, *, npt, tps):
    i = pl.program_id(0)
    first, last = _seq_edges(i, npt, tps)
    _fill_ext(extc_sc, xc_ref, xcp_ref, xcn_ref, first, last)
    _fill_ext(extp_sc, p_ref, pp_ref, pn_ref, first, last)
    cw = xc_ref.shape[1]
    gc = cw // len(POOL_WINDOWS)
    is_prompt = i < npt
    t_len = jnp.where(is_prompt, TM, TM * tps)
    tile_pos = jnp.where(is_prompt, 0, ((i - npt) % tps) * TM)
    for rs in _row_parts(MIX_ROW_PARTS):
        nrows = rs.stop - rs.start
        base = HALO + rs.start
        pos = lax.broadcasted_iota(jnp.int32, (nrows, 1), 0) + (tile_pos + rs.start)
        ds = []
        for g, w in enumerate(POOL_WINDOWS):
            sl = slice(g * gc, (g + 1) * gc)
            acc = None
            for o in range(-(w // 2), w // 2):
                term = extc_sc[base + o:base + o + nrows, sl]
                acc = term if acc is None else acc + term
            lo = jnp.maximum(pos - w // 2, 0)
            hi = jnp.minimum(pos + w // 2 - 1, t_len - 1)
            cnt = (hi - lo + 1).astype(F32)
            ds.append(acc / cnt - extc_sc[base:base + nrows, sl])
        d = jnp.concatenate(ds, axis=1).astype(BF16)
        yc = jnp.dot(d, pw_ref[...], preferred_element_type=F32) * ps_ref[...]
        conv = None
        for k in range(SCONV_K):
            start = base - SCONV_K // 2 + k
            term = extp_sc[start:start + nrows, :] * dw_ref[k:k + 1, :]
            conv = term if conv is None else conv + term
        yd = bg_ref[rs, :] * conv
        m = (jnp.dot(yc.astype(BF16), wo_ref[0:cw, :], preferred_element_type=F32)
             + jnp.dot(yd.astype(BF16), wo_ref[cw:, :], preferred_element_type=F32))
        x3 = x_ref[rs, :] + mod_ref[0, 2:3, :] * m
        x3_ref[rs, :] = x3
        _moe_prologue(x3, rs, mod_ref, g2_ref, wr_ref, rb_ref, h2_ref, info_ref, cnt_ref)


def _final_kernel(x_ref, dcur_ref, dnxt_ref, cw_ref, yb_hbm, modp_ref, g_ref, op_ref, os_ref, ybuf, sem, *, npt):
    i = pl.program_id(0)
    slot = _moe_gather(dcur_ref, dnxt_ref, yb_hbm, ybuf, sem, prefetch_next=True)
    x = x_ref[...] + modp_ref[0, 5:6, :] * _combined_rows(cw_ref, ybuf, slot, slice(0, TM))
    ms = jnp.mean(x * x, axis=-1, keepdims=True)
    out = x * lax.rsqrt(ms + EPS) * g_ref[...]

    @pl.when(i < npt)
    def _():
        op_ref[...] = out

    @pl.when(i >= npt)
    def _():
        os_ref[...] = out


def _expert_kernel(be_ref, nv_ref, x_ref, wg_ref, wu_ref, wd_ref, y_ref, wg_sc, wu_sc, wd_sc):
    i = pl.program_id(0)

    @pl.when(jnp.logical_or(i == 0, be_ref[i] != be_ref[jnp.maximum(i - 1, 0)]))
    def _():
        wg_sc[...] = wg_ref[0, 0].astype(BF16)
        wu_sc[...] = wu_ref[0, 0].astype(BF16)
        wd_sc[...] = wd_ref[0, 0].astype(BF16)

    @pl.when(i < nv_ref[0])
    def _():
        x = _unpack_bf16_pairs(x_ref[...]).astype(BF16)
        hg = jnp.dot(x, wg_sc[...], preferred_element_type=F32)
        hu = jnp.dot(x, wu_sc[...], preferred_element_type=F32)
        hid = (_silu(hg) * hu).astype(BF16)
        y_ref[...] = _pack_bf16_pairs(jnp.dot(hid, wd_sc[...], preferred_element_type=F32))

    @pl.when(i >= nv_ref[0])
    def _():
        y_ref[...] = jnp.zeros(y_ref.shape, y_ref.dtype)


def _experts(blk_e, nvalid, xb, layer, wg, wu, wd):
    p_len = xb.shape[0]
    d, hid = wg.shape[2:]
    nblk = p_len // MOE_BLK
    grid_spec = pltpu.PrefetchScalarGridSpec(
        num_scalar_prefetch=2,
        grid=(nblk,),
        in_specs=[pl.BlockSpec((MOE_BLK, d // 2), lambda i, be, nv: (jnp.minimum(i, nv[0] - 1), 0)),
                  pl.BlockSpec((1, 1, d, hid), lambda i, be, nv: (layer, be[i], 0, 0)),
                  pl.BlockSpec((1, 1, d, hid), lambda i, be, nv: (layer, be[i], 0, 0)),
                  pl.BlockSpec((1, 1, hid, d), lambda i, be, nv: (layer, be[i], 0, 0))],
        out_specs=pl.BlockSpec((MOE_BLK, d // 2), lambda i, be, nv: (i, 0)),
        scratch_shapes=[pltpu.VMEM((d, hid), BF16), pltpu.VMEM((d, hid), BF16), pltpu.VMEM((hid, d), BF16)],
    )
    return pl.pallas_call(
        _expert_kernel,
        grid_spec=grid_spec,
        out_shape=jax.ShapeDtypeStruct((p_len, d // 2), jnp.uint32),
        compiler_params=_cparams(("arbitrary",)),
        name="moe_experts",
    )(blk_e, nvalid, xb, wg, wu, wd)


def _moe(h2, info, cnt, layer, wg, wu, wd):
    n, d = h2.shape
    cnt = cnt[:, :, 0].astype(jnp.int32)
    counts = jnp.sum(cnt, axis=0)
    pc = ((counts + MOE_BLK - 1) // MOE_BLK) * MOE_BLK
    pend = jnp.cumsum(pc)
    base = (pend - pc)[None, :] + jnp.cumsum(cnt, axis=0) - cnt
    eid = info[:, 0:2, :].astype(jnp.int32)
    rank = info[:, 4:6, :].astype(jnp.int32)
    sel = eid[..., None] == jnp.arange(N_EXPERTS, dtype=jnp.int32)
    dest = jnp.sum(jnp.where(sel, base[:, None, None, :], 0), axis=-1) + rank
    p_len = n * TOP_K + N_EXPERTS * MOE_BLK
    nblk = p_len // MOE_BLK
    blk_start = jnp.arange(nblk, dtype=jnp.int32) * MOE_BLK
    blk_e = jnp.minimum(jnp.sum(pend[None, :] <= blk_start[:, None], axis=1), N_EXPERTS - 1).astype(jnp.int32)
    nvalid = (pend[-1:] // MOE_BLK).astype(jnp.int32)
    dest3 = dest.reshape(dest.shape[0], 1, TOP_K * TM)
    wcol = jnp.transpose(info[:, 2:2 + TOP_K, :], (0, 2, 1)).reshape(n, TOP_K)
    xb = _dispatch(dest3, pend.astype(jnp.int32), h2, p_len)
    yb = _experts(blk_e, nvalid, xb, layer, wg, wu, wd)
    return dest3, wcol, yb


def _rope_tables(t_len, qw):
    rows = t_len // GRID_W
    ax = qw // (DIFF_HEADS * 2) // 2
    row = jnp.repeat(jnp.arange(rows, dtype=F32), GRID_W)
    col = jnp.tile(jnp.arange(GRID_W, dtype=F32), rows)
    inv = ROPE_BASE ** (-jnp.arange(0, ax, 2, dtype=F32) / ax)
    ar = row[:, None] * inv[None, :]
    ac = col[:, None] * inv[None, :]
    cr, sr, cc, sc = jnp.cos(ar), jnp.sin(ar), jnp.cos(ac), jnp.sin(ac)
    cos = jnp.concatenate([cr, cr, cc, cc], axis=1)
    sins = jnp.concatenate([-sr, sr, -sc, sc], axis=1)
    reps = qw // cos.shape[1]
    cos = jnp.concatenate([jnp.ones((TM, cos.shape[1]), F32), cos], axis=0)
    sins = jnp.concatenate([jnp.zeros((TM, sins.shape[1]), F32), sins], axis=0)
    return jnp.tile(cos, (1, reps)), jnp.tile(sins, (1, reps))


def kernel(x_prompt, x_sample, cache_k, cache_v, c, c_ctx, w_ada, b_ada, g_norm1, g_norm2, g_final, w_in_e, conv_a_w, conv_a_b, ln_a_g, ln_a_b, lam_q1, lam_k1, lam_q2, lam_k2, subln_g, w_out_e, w_in_o, pool_w, pool_scale, conv_d_w, w_out_o, w_router_g, b_router_g, w_router_e, b_router_e, w_gate, w_up, w_down):
    bp, tp, d = x_prompt.shape
    bs, ts, _ = x_sample.shape
    past = cache_k.shape[2]
    assert tp == TM and ts % TM == 0 and d % LANES == 0
    npt = bp
    tps = ts // TM
    n_p, n_s = bp * tp, bs * ts
    n = n_p + n_s
    nt = n // TM
    cw = conv_a_w.shape[2]
    qw = (w_in_e.shape[2] - 2 * cw) // 3
    dh = qw // (DIFF_HEADS * 2)
    hpt = TM // HALO

    def cond_of(i):
        return jnp.where(i < npt, 0, 1 + (i - npt) // tps)

    def tile(w):
        return pl.BlockSpec((TM, w), lambda i: (i, 0))

    def halo_prev(w):
        return pl.BlockSpec((HALO, w), lambda i: (jnp.maximum(i * hpt - 1, 0), 0))

    def halo_next(w):
        return pl.BlockSpec((HALO, w), lambda i: (jnp.minimum((i + 1) * hpt, nt * hpt - 1), 0))

    def whole(shape):
        return pl.BlockSpec(shape, lambda i: (0,) * len(shape))

    mod_spec = pl.BlockSpec((1, 6, d), lambda i: (cond_of(i), 0, 0))
    combine_specs = [pl.BlockSpec((1, 1, TOP_K * TM), lambda i: (i, 0, 0), memory_space=pltpu.SMEM),
                     pl.BlockSpec((1, 1, TOP_K * TM), lambda i: (jnp.minimum(i + 1, nt - 1), 0, 0),
                                  memory_space=pltpu.SMEM),
                     tile(TOP_K), pl.BlockSpec(memory_space=pl.ANY)]
    combine_scratch = [pltpu.VMEM((2, TOP_K, TM // SUBLANES, SUBLANES, d // 2), jnp.uint32),
                       pltpu.SemaphoreType.DMA((2,))]

    rows = SUBLANES * pl.cdiv(1 + bs, SUBLANES)
    cond = jnp.concatenate([c_ctx[None, :], c, jnp.zeros((rows - 1 - bs, d), F32)], axis=0)
    mod = _ada_table(cond, w_ada, b_ada)

    xp2, xs2 = x_prompt.reshape(n_p, d), x_sample.reshape(n_s, d)

    def prompt_tile(w):
        return pl.BlockSpec((TM, w), lambda i: (jnp.minimum(i, npt - 1), 0))

    def sample_tile(w):
        return pl.BlockSpec((TM, w), lambda i: (jnp.maximum(i - npt, 0), 0))

    def router_w(i):
        wre = jnp.transpose(w_router_e[i], (1, 0, 2)).reshape(d, N_EXPERTS)
        wr = jnp.concatenate([wre, w_router_g[i]], axis=1)
        return jnp.pad(wr, ((0, 0), (0, ROUTER_PAD - wr.shape[1]))).astype(BF16)

    def router_b(i):
        b = jnp.concatenate([b_router_e[i].reshape(N_EXPERTS), b_router_g[i]])
        return jnp.pad(b, (0, ROUTER_PAD - b.shape[0]))[:, None]

    cos, sins = _rope_tables(ts, qw)
    tab_spec = pl.BlockSpec((TM, qw), lambda i: (jnp.where(i < npt, 0, 1 + (i - npt) % tps), 0))
    kc_spec = pl.BlockSpec((1, 1, TM, DIFF_HEADS, 2, dh), lambda i: (jnp.minimum(i, npt - 1), 0, 0, 0, 0, 0))
    vc_spec = pl.BlockSpec((1, 1, TM, DIFF_HEADS, 2 * dh), lambda i: (jnp.minimum(i, npt - 1), 0, 0, 0, 0))
    u, q, k, vt, new_k, new_v = pl.pallas_call(
        functools.partial(_pre_even_kernel, npt=npt, cw=cw, qw=qw, qscale=dh ** -0.5 * math.log2(math.e)),
        grid=(nt,),
        in_specs=[prompt_tile(d), sample_tile(d), mod_spec, whole((1, d)), whole(w_in_e.shape[1:]),
                  tab_spec, tab_spec],
        out_specs=[tile(cw), tile(qw), tile(qw), pl.BlockSpec((qw, TM), lambda i: (0, i)),
                   kc_spec, vc_spec],
        out_shape=[jax.ShapeDtypeStruct((n, cw), F32), jax.ShapeDtypeStruct((n, qw), BF16),
                   jax.ShapeDtypeStruct((n, qw), BF16), jax.ShapeDtypeStruct((qw, n), BF16),
                   jax.ShapeDtypeStruct((bp, 1, tp, DIFF_HEADS, 2, dh), F32),
                   jax.ShapeDtypeStruct((bp, 1, tp, DIFF_HEADS, 2 * dh), F32)],
        scratch_shapes=[pltpu.VMEM((2, TM, qw), F32)],
        compiler_params=_cparams(("arbitrary",)),
        name="pre_even",
    )(xp2, xs2, mod[0], g_norm1[0][None, :], w_in_e[0].astype(BF16), cos, sins)

    lam_init = 0.8 - 0.6 * math.exp(-0.3 * 0)
    lam = (jnp.exp(jnp.sum(lam_q1[0] * lam_k1[0])) - jnp.exp(jnp.sum(lam_q2[0] * lam_k2[0])) + lam_init)
    lam = lam.reshape(1).astype(F32)
    sub_g = subln_g[0][:, None]
    o_p = _attention(lam, q, k, vt, sub_g, None, batch=bp, t_len=tp, row0=0, tq=TM, tk=TM,
                     post_scale=1.0 - lam_init, heads_per_step=DIFF_HEADS)
    k_ctx = cache_k[:, 0].reshape(bs * past, qw).astype(BF16)
    vt_ctx = jnp.transpose(cache_v[:, 0].reshape(bs, past, qw), (0, 2, 1)).astype(BF16)
    o_s = _attention(lam, q, k, vt, sub_g, (k_ctx, vt_ctx), batch=bs, t_len=ts, row0=n_p, tq=ATTN_TQ, tk=ATTN_TK,
                     post_scale=1.0 - lam_init)

    moe_outs = [jax.ShapeDtypeStruct((n, d), F32), jax.ShapeDtypeStruct((n, d // 2), jnp.uint32),
                jax.ShapeDtypeStruct((nt, INFO_ROWS, TM), F32), jax.ShapeDtypeStruct((nt, N_EXPERTS, 1), F32)]
    moe_out_specs = [tile(d), tile(d // 2), pl.BlockSpec((1, INFO_ROWS, TM), lambda i: (i, 0, 0)),
                     pl.BlockSpec((1, N_EXPERTS, 1), lambda i: (i, 0, 0))]
    x1, h2, info, cnt = pl.pallas_call(
        functools.partial(_post_even_kernel, npt=npt, tps=tps),
        grid=(nt,),
        in_specs=[prompt_tile(d), sample_tile(d), tile(cw), halo_prev(cw), halo_next(cw),
                  prompt_tile(qw), sample_tile(qw), mod_spec,
                  whole((CONV_K, cw)), whole((1, cw)), whole((1, cw)), whole((1, cw)),
                  whole(w_out_e.shape[1:]), whole((1, d)), whole((d, ROUTER_PAD)), whole((ROUTER_PAD, 1))],
        out_specs=moe_out_specs,
        out_shape=moe_outs,
        scratch_shapes=[pltpu.VMEM((TM + 2 * HALO, cw), F32),
                        pltpu.VMEM((SUBLANES, TM + 2 * HALO - SUBLANES, cw), F32)],
        compiler_params=_cparams(("parallel",)),
        name="post_even",
    )(xp2, xs2, u, u, u, o_p, o_s, mod[0], conv_a_w[0], conv_a_b[0][None, :], ln_a_g[0][None, :], ln_a_b[0][None, :],
      w_out_e[0].astype(BF16), g_norm2[0][None, :], router_w(0), router_b(0))
    dest3, wcol, yb = _moe(h2, info, cnt, 0, w_gate, w_up, w_down)

    pw = pool_w.shape[2]
    x2, xc, p, bg = pl.pallas_call(
        functools.partial(_pre_odd_kernel, cw=cw),
        grid=(nt,),
        in_specs=[tile(d)] + combine_specs + [mod_spec, mod_spec, whole((1, d)), whole(w_in_o.shape[1:])],
        out_specs=[tile(d), tile(cw), tile(cw), tile(cw)],
        out_shape=[jax.ShapeDtypeStruct((n, d), F32)] + [jax.ShapeDtypeStruct((n, cw), F32)] * 3,
        scratch_shapes=combine_scratch,
        compiler_params=_cparams(("arbitrary",)),
        name="pre_odd",
    )(x1, dest3, dest3, wcol, yb, mod[0], mod[1], g_norm1[1][None, :], w_in_o[0].astype(BF16))
    pool_bd = jnp.zeros((cw, cw), F32)
    for g in range(len(POOL_WINDOWS)):
        pool_bd = pool_bd.at[g * pw:(g + 1) * pw, g * pw:(g + 1) * pw].set(pool_w[0, g])
    x3, h2, info, cnt = pl.pallas_call(
        functools.partial(_post_odd_kernel, npt=npt, tps=tps),
        grid=(nt,),
        in_specs=[tile(d), tile(cw), halo_prev(cw), halo_next(cw), tile(cw), halo_prev(cw), halo_next(cw),
                  tile(cw), mod_spec, whole((cw, cw)), whole((1, cw)), whole((SCONV_K, cw)),
                  whole(w_out_o.shape[1:]), whole((1, d)), whole((d, ROUTER_PAD)), whole((ROUTER_PAD, 1))],
        out_specs=moe_out_specs,
        out_shape=moe_outs,
        scratch_shapes=[pltpu.VMEM((TM + 2 * HALO, cw), F32), pltpu.VMEM((TM + 2 * HALO, cw), F32)],
        compiler_params=_cparams(("parallel",)),
        name="post_odd",
    )(x2, xc, xc, xc, p, p, p, bg, mod[1], pool_bd.astype(BF16), pool_scale[0][None, :], conv_d_w[0],
      w_out_o[0].astype(BF16), g_norm2[1][None, :], router_w(1), router_b(1))
    dest3, wcol, yb = _moe(h2, info, cnt, 1, w_gate, w_up, w_down)

    out_p, out_s = pl.pallas_call(
        functools.partial(_final_kernel, npt=npt),
        grid=(nt,),
        in_specs=[tile(d)] + combine_specs + [mod_spec, whole((1, d))],
        out_specs=[pl.BlockSpec((TM, d), lambda i: (jnp.minimum(i, npt - 1), 0)),
                   pl.BlockSpec((TM, d), lambda i: (jnp.maximum(i - npt, 0), 0))],
        out_shape=[jax.ShapeDtypeStruct((n_p, d), F32), jax.ShapeDtypeStruct((n_s, d), F32)],
        scratch_shapes=combine_scratch,
        compiler_params=_cparams(("arbitrary",)),
        name="final_norm",
    )(x3, dest3, dest3, wcol, yb, mod[1], g_final[None, :])

    return (out_p.reshape(bp, tp, d), out_s.reshape(bs, ts, d), new_k, new_v)
```

```python
import functools
import math

import jax
import jax.numpy as jnp
from jax import lax
from jax.experimental import pallas as pl
from jax.experimental.pallas import tpu as pltpu

F32 = jnp.float32
BF16 = jnp.bfloat16

EPS = 1e-6
GRID_W = 64
ROPE_BASE = 10000.0
CONV_K = 31
SCONV_K = 3
POOL_WINDOWS = (2, 4, 8, 16)
DIFF_HEADS = 4
N_GROUPS = 4
EXPERTS_PER_GROUP = 8
N_EXPERTS = N_GROUPS * EXPERTS_PER_GROUP
TOP_K = 2

LANES = 128
SUBLANES = 8
TM = 256
HALO = 16
PROJ_ROW_PARTS = 2
MIX_ROW_PARTS = 1
PROJ_COLS = 256
MOE_BLK = 1024
ATTN_TQ = 2048
ATTN_TK = 512
CONV_ROWS = 32
INFO_ROWS = 8
ROUTER_PAD = LANES
ONES_ROWS = 2 * SUBLANES
DISPATCH_SLOTS = 3
VMEM_LIMIT = 56 * 1024 * 1024


def _cparams(sem):
    return pltpu.CompilerParams(dimension_semantics=sem, vmem_limit_bytes=VMEM_LIMIT)


def _rms_mod(x, g, sc, sh):
    ms = jnp.mean(x * x, axis=-1, keepdims=True)
    return (x * lax.rsqrt(ms + EPS)) * g * (1.0 + sc) + sh


def _silu(x):
    return x * jax.nn.sigmoid(x)


def _pack_bf16_pairs(y):
    half = y.shape[1] // 2
    lo = pltpu.bitcast(y[:, :half].astype(BF16).astype(F32), jnp.uint32)
    hi = pltpu.bitcast(y[:, half:].astype(BF16).astype(F32), jnp.uint32)
    return lax.shift_right_logical(lo, jnp.uint32(16)) | hi


def _unpack_bf16_pairs(w):
    lo = pltpu.bitcast(lax.shift_left(w, jnp.uint32(16)), F32)
    hi = pltpu.bitcast(w & jnp.uint32(0xFFFF0000), F32)
    return jnp.concatenate([lo, hi], axis=1)


def _ada_kernel(c_ref, w_ref, b_ref, o_ref):
    cs = _silu(c_ref[...])
    o_ref[0] = jnp.dot(cs, w_ref[0], precision=lax.Precision.HIGHEST,
                       preferred_element_type=F32) + b_ref[0]


def _ada_table(cond, w_ada, b_ada):
    depth, d, six_d = w_ada.shape
    rows = cond.shape[0]
    nj = six_d // d
    out = pl.pallas_call(
        _ada_kernel,
        grid=(depth, nj),
        in_specs=[pl.BlockSpec((rows, d), lambda l, j: (0, 0)),
                  pl.BlockSpec((1, d, d), lambda l, j: (l, 0, j)),
                  pl.BlockSpec((1, 1, d), lambda l, j: (l, 0, j))],
        out_specs=pl.BlockSpec((1, rows, d), lambda l, j: (l, 0, j)),
        out_shape=jax.ShapeDtypeStruct((depth, rows, six_d), F32),
        compiler_params=_cparams(("parallel", "parallel")),
        name="ada_table",
    )(cond, w_ada, b_ada.reshape(depth, 1, six_d))
    return out.reshape(depth, rows, nj, d)


def _row_parts(parts):
    rows = TM // parts
    return [slice(p * rows, (p + 1) * rows) for p in range(parts)]


def _rope(x, cos, sins, half):
    lane = lax.broadcasted_iota(jnp.int32, (1, LANES), 1)
    first_half = (lane % (2 * half)) < half
    outs = []
    for j in range(x.shape[1] // LANES):
        sl = slice(j * LANES, (j + 1) * LANES)
        xs = x[:, sl]
        nxt = pltpu.roll(xs, LANES - half, 1)
        prv = pltpu.roll(xs, half, 1)
        rot = jnp.where(first_half, nxt, prv)
        outs.append(xs * cos[:, sl] + rot * sins[:, sl])
    return jnp.concatenate(outs, axis=1)


def _pre_even_kernel(xp_ref, xs_ref, mod_ref, g_ref, w_ref, cos_ref, sin_ref,
                     u_ref, q_ref, k_ref, vt_ref, kc_ref, vc_ref, kv_sc, *, npt, cw, qw, qscale):
    i = pl.program_id(0)
    is_prompt = i < npt
    half = kc_ref.shape[-1] // 4
    for sl in _row_parts(PROJ_ROW_PARTS):
        x = jnp.where(is_prompt, xp_ref[sl, :], xs_ref[sl, :])
        h = _rms_mod(x, g_ref[...], mod_ref[0, 1:2, :], mod_ref[0, 0:1, :])
        proj = jnp.dot(h.astype(BF16), w_ref[...], preferred_element_type=F32)
        ga = proj[:, :cw]
        gb = proj[:, cw:2 * cw]
        u_ref[sl, :] = ga * jax.nn.sigmoid(gb)
        q = proj[:, 2 * cw:2 * cw + qw]
        k = proj[:, 2 * cw + qw:2 * cw + 2 * qw]
        v = proj[:, 2 * cw + 2 * qw:]
        kv_sc[0, sl, :] = k
        kv_sc[1, sl, :] = v
        cos = cos_ref[sl, :]
        sins = sin_ref[sl, :]
        q_ref[sl, :] = (_rope(q, cos, sins, half) * qscale).astype(BF16)
        k_ref[sl, :] = _rope(k, cos, sins, half).astype(BF16)
        vt_ref[:, sl] = v.T.astype(BF16)

    @pl.when(is_prompt)
    def _():
        heads, halves, dh = kc_ref.shape[3:]
        for h in range(heads):
            vc_ref[0, 0, :, h, :] = kv_sc[1, :, h * halves * dh:(h + 1) * halves * dh]
            for c in range(halves):
                lo = (h * halves + c) * dh
                kc_ref[0, 0, :, h, c, :] = kv_sc[0, :, lo:lo + dh]


def _attn_kernel(lam_ref, q_ref, k_ref, vt_ref, *rest, nk, tk, nk_ctx, post_scale):
    if nk_ctx:
        kc_ref, vtc_ref, g_ref, o_ref, s_sc, mc_sc, m_sc, acc_sc = rest
    else:
        g_ref, o_ref, s_sc, mc_sc, m_sc, acc_sc = rest
    q = q_ref[...]
    hw = q.shape[1]
    dh = hw // 2
    lane = lax.broadcasted_iota(jnp.int32, (1, hw), 1)
    zero = jnp.zeros_like(q)
    qs = (jnp.where(lane < dh, q, zero), jnp.where(lane >= dh, q, zero))
    m_sc[...] = jnp.full(m_sc.shape, -jnp.inf, F32)
    acc_sc[...] = jnp.zeros(acc_sc.shape, F32)
    ones_rows = (lax.broadcasted_iota(jnp.int32, (ONES_ROWS, tk), 0) == 0).astype(BF16)

    def scores(kk, slot):
        for c in range(2):
            s = lax.dot_general(kk, qs[c], (((1,), (1,)), ((), ())), preferred_element_type=F32)
            s_sc[slot, c] = s
            mc_sc[slot, c] = jnp.max(s, axis=0, keepdims=True)

    def consume(vt, slot):
        vta = jnp.concatenate([vt, ones_rows], axis=0)
        for c in range(2):
            m_prev = m_sc[c]
            m_new = jnp.maximum(m_prev, mc_sc[slot, c])
            alpha = jnp.exp2(m_prev - m_new)
            p = jnp.exp2(s_sc[slot, c] - m_new).astype(BF16)
            acc_sc[c] = alpha * acc_sc[c] + jnp.dot(vta, p, preferred_element_type=F32)
            m_sc[c] = m_new

    def k_at(i):
        if i < nk_ctx:
            return kc_ref[i * tk:(i + 1) * tk, :]
        return k_ref[(i - nk_ctx) * tk:(i - nk_ctx + 1) * tk, :]

    def vt_at(i):
        if i < nk_ctx:
            return vtc_ref[0, :, i * tk:(i + 1) * tk]
        return vt_ref[:, (i - nk_ctx) * tk:(i - nk_ctx + 1) * tk]

    n_total = nk_ctx + nk
    peel = nk_ctx + (n_total - 1 - nk_ctx) % 2
    scores(k_at(0), 0)
    for i in range(peel):
        scores(k_at(i + 1), (i + 1) % 2)
        consume(vt_at(i), i % 2)

    def body(t, carry):
        for r in range(2):
            j = 2 * t + (peel + r - nk_ctx)
            nxt = pl.multiple_of((j + 1) * tk, tk)
            cur = pl.multiple_of(j * tk, tk)
            scores(k_ref[pl.ds(nxt, tk), :], (peel + r + 1) % 2)
            consume(vt_ref[:, pl.ds(cur, tk)], (peel + r) % 2)
        return carry

    lax.fori_loop(0, (n_total - 1 - peel) // 2, body, 0)
    consume(vt_at(n_total - 1), (n_total - 1) % 2)
    o = (acc_sc[0, :hw, :] / acc_sc[0, hw:hw + 1, :]
         - lam_ref[0] * (acc_sc[1, :hw, :] / acc_sc[1, hw:hw + 1, :]))
    ms = jnp.mean(o * o, axis=0, keepdims=True)
    o = o * lax.rsqrt(ms + EPS) * g_ref[...] * post_scale
    o_ref[...] = o.T.astype(BF16)


def _attn_heads_kernel(lam_ref, q_ref, k_ref, vt_ref, g_ref, o_ref, *scratch, heads, **kw):
    hw = q_ref.shape[1] // heads
    for h in range(heads):
        cols = pl.ds(h * hw, hw)
        _attn_kernel(lam_ref, q_ref.at[:, cols], k_ref.at[:, cols], vt_ref.at[cols, :], g_ref, o_ref.at[:, cols],
                     *scratch, **kw)


def _attention(lam, q, k, vt, g_col, ctx, *, batch, t_len, row0, tq, tk, post_scale, heads_per_step=1):
    hw = q.shape[1] // DIFF_HEADS
    bw = hw * heads_per_step
    nq = t_len // tq
    assert row0 % t_len == 0 and t_len % tk == 0 and (ctx is None or heads_per_step == 1)
    qb0 = row0 // tq
    sb0 = row0 // t_len
    in_specs = [pl.BlockSpec(memory_space=pltpu.SMEM),
                pl.BlockSpec((tq, bw), lambda b, h, i: (qb0 + b * nq + i, h)),
                pl.BlockSpec((t_len, bw), lambda b, h, i: (sb0 + b, h)),
                pl.BlockSpec((bw, t_len), lambda b, h, i: (h, sb0 + b))]
    args = [lam, q, k, vt]
    nk_ctx = 0
    if ctx is not None:
        k_ctx, vt_ctx = ctx
        past = vt_ctx.shape[2]
        assert past % tk == 0
        nk_ctx = past // tk
        in_specs += [pl.BlockSpec((past, hw), lambda b, h, i: (b, h)),
                     pl.BlockSpec((1, hw, past), lambda b, h, i: (b, h, 0))]
        args += [k_ctx, vt_ctx]
    in_specs.append(pl.BlockSpec((hw, 1), lambda b, h, i: (0, 0)))
    args.append(g_col)
    body = functools.partial(_attn_kernel, nk=t_len // tk, tk=tk, nk_ctx=nk_ctx, post_scale=post_scale)
    if heads_per_step > 1:
        body = functools.partial(_attn_heads_kernel, heads=heads_per_step, nk=t_len // tk, tk=tk, nk_ctx=nk_ctx,
                                 post_scale=post_scale)
    return pl.pallas_call(
        body,
        grid=(batch, DIFF_HEADS // heads_per_step, nq),
        in_specs=in_specs,
        out_specs=pl.BlockSpec((tq, bw), lambda b, h, i: (b * nq + i, h)),
        out_shape=jax.ShapeDtypeStruct((batch * t_len, q.shape[1]), BF16),
        scratch_shapes=[pltpu.VMEM((2, 2, tk, tq), F32), pltpu.VMEM((2, 2, 1, tq), F32),
                        pltpu.VMEM((2, 1, tq), F32), pltpu.VMEM((2, hw + ONES_ROWS, tq), F32)],
        compiler_params=_cparams(("parallel", "parallel", "parallel")),
        name="diff_attention",
    )(*args)


def _seq_edges(i, npt, tps):
    r = (i - npt) % tps
    is_prompt = i < npt
    first = jnp.logical_or(is_prompt, r == 0)
    last = jnp.logical_or(is_prompt, r == tps - 1)
    return first, last


def _fill_ext(ext_sc, cur_ref, prev_ref, next_ref, first, last):
    prev = prev_ref[...]
    nxt = next_ref[...]
    ext_sc[0:HALO, :] = jnp.where(first, jnp.zeros_like(prev), prev)
    ext_sc[HALO:HALO + TM, :] = cur_ref[...]
    ext_sc[HALO + TM:, :] = jnp.where(last, jnp.zeros_like(nxt), nxt)


def _route_tile(logits, b_ref, info_ref, cnt_ref):
    lt = logits.T + b_ref[...]
    gl = lt[N_EXPERTS:N_EXPERTS + N_GROUPS]
    ge = jnp.exp(gl - jnp.max(gl, axis=0, keepdims=True))
    gp = ge / jnp.sum(ge, axis=0, keepdims=True)
    best = gp[0:1]
    gidx = jnp.zeros(best.shape, jnp.int32)
    el = lt[0:EXPERTS_PER_GROUP]
    for g in range(1, N_GROUPS):
        better = gp[g:g + 1] > best
        gidx = jnp.where(better, g, gidx)
        best = jnp.where(better, gp[g:g + 1], best)
    for g in range(1, N_GROUPS):
        el = jnp.where(gidx == g, lt[g * EXPERTS_PER_GROUP:(g + 1) * EXPERTS_PER_GROUP], el)
    rio = lax.broadcasted_iota(jnp.int32, el.shape, 0)
    v0 = jnp.max(el, axis=0, keepdims=True)
    i0 = jnp.min(jnp.where(el == v0, rio, EXPERTS_PER_GROUP), axis=0, keepdims=True)
    el2 = jnp.where(rio == i0, -jnp.inf, el)
    v1 = jnp.max(el2, axis=0, keepdims=True)
    i1 = jnp.min(jnp.where(el2 == v1, rio, EXPERTS_PER_GROUP), axis=0, keepdims=True)
    e1 = jnp.exp(v1 - v0)
    den = 1.0 + e1
    w0 = (1.0 / den) * best
    w1 = (e1 / den) * best
    eid0 = gidx * EXPERTS_PER_GROUP + i0
    eid1 = gidx * EXPERTS_PER_GROUP + i1
    tm = lt.shape[1]
    eio = lax.broadcasted_iota(jnp.int32, (N_EXPERTS, tm), 0)
    oh0 = eio == eid0
    oh1 = eio == eid1
    ohs = jnp.where(oh0, 1.0, 0.0) + jnp.where(oh1, 1.0, 0.0)
    upper = (lax.broadcasted_iota(jnp.int32, (tm, tm), 0)
             < lax.broadcasted_iota(jnp.int32, (tm, tm), 1)).astype(BF16)
    before = jnp.dot(ohs.astype(BF16), upper, preferred_element_type=F32)
    rank0 = jnp.sum(jnp.where(oh0, before, 0.0), axis=0, keepdims=True)
    rank1 = jnp.sum(jnp.where(oh1, before, 0.0), axis=0, keepdims=True)
    info_ref[0] = jnp.concatenate([eid0.astype(F32), eid1.astype(F32), w0, w1, rank0, rank1,
                                   jnp.zeros((INFO_ROWS - 6, tm), F32)], axis=0)
    cnt_ref[0] = jnp.sum(ohs, axis=1, keepdims=True)


def _moe_prologue(x_new, sl, mod_ref, g2_ref, wr_ref, rb_ref, h2_ref, info_ref, cnt_ref):
    h2 = _rms_mod(x_new, g2_ref[...], mod_ref[0, 4:5, :], mod_ref[0, 3:4, :])
    h2_ref[sl, :] = _pack_bf16_pairs(h2)
    assert sl.stop - sl.start == TM
    _route_tile(jnp.dot(h2.astype(BF16), wr_ref[...], preferred_element_type=F32), rb_ref, info_ref, cnt_ref)


def _post_even_kernel(xp_ref, xs_ref, u_ref, up_ref, un_ref, op_ref, os_ref, mod_ref, cw_ref, cb_ref,
                      lng_ref, lnb_ref, wo_ref, g2_ref, wr_ref, rb_ref, x1_ref, h2_ref, info_ref, cnt_ref,
                      ext_sc, sh_sc, *, npt, tps):
    i = pl.program_id(0)
    first, last = _seq_edges(i, npt, tps)
    _fill_ext(ext_sc, u_ref, up_ref, un_ref, first, last)
    cw = u_ref.shape[1]
    span = sh_sc.shape[1]
    for r in range(SUBLANES):
        sh_sc[r] = ext_sc[r:r + span, :]
    rows = CONV_ROWS
    is_prompt = i < npt
    for sl in _row_parts(MIX_ROW_PARTS):
        chunks = []
        for r0 in range(sl.start, sl.stop, rows):
            acc = None
            for k in range(CONV_K):
                off = HALO - CONV_K // 2 + k
                start = r0 + off - off % SUBLANES
                term = sh_sc[off % SUBLANES, start:start + rows, :] * cw_ref[k:k + 1, :]
                acc = term if acc is None else acc + term
            chunks.append(acc)
        conv = jnp.concatenate(chunks, axis=0) + cb_ref[...]
        mu = jnp.mean(conv, axis=-1, keepdims=True)
        cen = conv - mu
        var = jnp.mean(cen * cen, axis=-1, keepdims=True)
        uu = _silu(cen * lax.rsqrt(var + EPS) * lng_ref[...] + lnb_ref[...])
        o = jnp.where(is_prompt, op_ref[sl, :], os_ref[sl, :])
        m = (jnp.dot(uu.astype(BF16), wo_ref[0:cw, :], preferred_element_type=F32)
             + jnp.dot(o, wo_ref[cw:, :], preferred_element_type=F32))
        x1 = jnp.where(is_prompt, xp_ref[sl, :], xs_ref[sl, :]) + mod_ref[0, 2:3, :] * m
        x1_ref[sl, :] = x1
        _moe_prologue(x1, sl, mod_ref, g2_ref, wr_ref, rb_ref, h2_ref, info_ref, cnt_ref)


def _gather_start(dest_ref, yb_hbm, ybuf, sem, slot):
    def body(i, carry):
        for j in range(SUBLANES):
            for k in range(TOP_K):
                row = dest_ref[0, 0, k * TM + i * SUBLANES + j]
                pltpu.make_async_copy(yb_hbm.at[pl.ds(row, 1)], ybuf.at[slot, k, i, pl.ds(j, 1)],
                                      sem.at[slot]).start(priority=k % 2)
        return carry

    lax.fori_loop(0, TM // SUBLANES, body, 0)


def _moe_gather(dcur_ref, dnxt_ref, yb_hbm, ybuf, sem, prefetch_next):
    t = pl.program_id(0)
    slot = t % 2

    @pl.when(t == 0)
    def _():
        _gather_start(dcur_ref, yb_hbm, ybuf, sem, 0)

    if prefetch_next:
        @pl.when(t + 1 < pl.num_programs(0))
        def _():
            _gather_start(dnxt_ref, yb_hbm, ybuf, sem, 1 - slot)

    _gather_wait(ybuf, sem, slot)
    return slot


def _gather_wait(ybuf, sem, slot):
    for k in range(TOP_K):
        pltpu.make_async_copy(ybuf.at[slot, k], ybuf.at[slot, k], sem.at[slot]).wait()


def _moe_prefetch_group(dnxt_ref, yb_hbm, ybuf, sem, slot, g, n_groups):
    n_i = TM // SUBLANES
    for i in range(g * n_i // n_groups, (g + 1) * n_i // n_groups):
        for j in range(SUBLANES):
            for k in range(TOP_K):
                row = dnxt_ref[0, 0, k * TM + i * SUBLANES + j]
                pltpu.make_async_copy(yb_hbm.at[pl.ds(row, 1)], ybuf.at[1 - slot, k, i, pl.ds(j, 1)],
                                      sem.at[1 - slot]).start(priority=k % 2)


def _moe_prefetch_finish(ybuf, sem, slot):
    @pl.when(pl.program_id(0) == pl.num_programs(0) - 1)
    def _():
        _gather_wait(ybuf, sem, 1 - slot)


def _combined_rows(cw_ref, ybuf, slot, sl):
    dw = ybuf.shape[-1]
    g = slice(sl.start // SUBLANES, sl.stop // SUBLANES)
    w = cw_ref[sl, :]
    y = w[:, 0:1] * _unpack_bf16_pairs(ybuf[slot, 0, g].reshape(sl.stop - sl.start, dw))
    for k in range(1, TOP_K):
        y = y + w[:, k:k + 1] * _unpack_bf16_pairs(ybuf[slot, k, g].reshape(sl.stop - sl.start, dw))
    return y


def _dispatch_kernel(dest_ref, pend_ref, h2_hbm, xb_hbm, stage, zeros, in_sem, row_sem, zero_sem):
    t = pl.program_id(0)
    nt = pl.num_programs(0)
    slot = t % DISPATCH_SLOTS

    tile_rows = TM // SUBLANES

    def stage_in(tile_idx, s):
        return pltpu.make_async_copy(h2_hbm.at[pl.ds(tile_idx * tile_rows, tile_rows)], stage.at[s], in_sem.at[s])

    def drain_rows(s):
        for _ in range(TOP_K):
            pltpu.make_async_copy(stage.at[s], stage.at[s], row_sem.at[s]).wait()

    def zero_block(start):
        return pltpu.make_async_copy(zeros, xb_hbm.at[pl.ds(pl.multiple_of(start, MOE_BLK), MOE_BLK)], zero_sem)

    def zero_fill(op):
        for e in range(N_EXPERTS):
            op(zero_block(jnp.maximum(pend_ref[e] - MOE_BLK, 0)))
        for j in range(N_EXPERTS):
            start = pend_ref[N_EXPERTS - 1] + j * MOE_BLK

            @pl.when(start < xb_hbm.shape[0])
            def _():
                op(zero_block(start))

    @pl.when(t == 0)
    def _():
        stage_in(0, 0).start()
        zeros[...] = jnp.zeros(zeros.shape, zeros.dtype)
        zero_fill(lambda cp: cp.start())
        zero_fill(lambda cp: cp.wait())

    @pl.when(jnp.logical_and(t == 0, nt > 1))
    def _():
        stage_in(1, 1).start()

    stage_in(t, slot).wait()

    def body(i, carry):
        for j in range(SUBLANES):
            for k in range(TOP_K):
                row = dest_ref[0, 0, k * TM + i * SUBLANES + j]
                pltpu.make_async_copy(stage.at[slot, i, pl.ds(j, 1)], xb_hbm.at[pl.ds(row, 1)],
                                      row_sem.at[slot]).start(priority=k % 2)
        return carry

    lax.fori_loop(0, tile_rows, body, 0)

    @pl.when(t > 0)
    def _():
        drain_rows((t + DISPATCH_SLOTS - 1) % DISPATCH_SLOTS)

    @pl.when(t + 2 < nt)
    def _():
        stage_in(t + 2, (t + 2) % DISPATCH_SLOTS).start()

    @pl.when(t == nt - 1)
    def _():
        drain_rows(slot)


def _dispatch(dest3, pend, h2, p_len):
    n, d = h2.shape
    return pl.pallas_call(
        _dispatch_kernel,
        grid=(n // TM,),
        in_specs=[pl.BlockSpec((1, 1, TOP_K * TM), lambda t: (t, 0, 0), memory_space=pltpu.SMEM),
                  pl.BlockSpec(memory_space=pltpu.SMEM), pl.BlockSpec(memory_space=pl.ANY)],
        out_specs=pl.BlockSpec(memory_space=pl.ANY),
        out_shape=jax.ShapeDtypeStruct((p_len, d), h2.dtype),
        scratch_shapes=[pltpu.VMEM((DISPATCH_SLOTS, TM // SUBLANES, SUBLANES, d), h2.dtype),
                        pltpu.VMEM((MOE_BLK, d), h2.dtype),
                        pltpu.SemaphoreType.DMA((DISPATCH_SLOTS,)), pltpu.SemaphoreType.DMA((DISPATCH_SLOTS,)),
                        pltpu.SemaphoreType.DMA(())],
        compiler_params=_cparams(("arbitrary",)),
        name="moe_dispatch",
    )(dest3, pend, h2.reshape(n // SUBLANES, SUBLANES, d))


def _pre_odd_kernel(x_ref, dcur_ref, dnxt_ref, cw_ref, yb_hbm, modp_ref, mod_ref, g_ref, w_ref,
                    x2_ref, xc_ref, p_ref, bg_ref, ybuf, sem, *, cw):
    slot = _moe_gather(dcur_ref, dnxt_ref, yb_hbm, ybuf, sem, prefetch_next=False)
    col_blocks = w_ref.shape[1] // PROJ_COLS
    n_groups = PROJ_ROW_PARTS * (1 + col_blocks)
    group = 0
    for sl in _row_parts(PROJ_ROW_PARTS):
        x2 = x_ref[sl, :] + modp_ref[0, 5:6, :] * _combined_rows(cw_ref, ybuf, slot, sl)
        x2_ref[sl, :] = x2
        _moe_prefetch_group(dnxt_ref, yb_hbm, ybuf, sem, slot, group, n_groups)
        group += 1
        h = _rms_mod(x2, g_ref[...], mod_ref[0, 1:2, :], mod_ref[0, 0:1, :]).astype(BF16)
        hd = []
        for c in range(col_blocks):
            cols = slice(c * PROJ_COLS, (c + 1) * PROJ_COLS)
            blk = jnp.dot(h, w_ref[:, cols], preferred_element_type=F32)
            which, off = divmod(c * PROJ_COLS, cw)
            dst = slice(off, off + PROJ_COLS)
            if which == 0:
                xc_ref[sl, dst] = blk
            elif which == 1:
                hd.append(blk)
            elif which == 2:
                bg_ref[sl, dst] = blk
            else:
                p_ref[sl, dst] = blk * hd[off // PROJ_COLS]
            _moe_prefetch_group(dnxt_ref, yb_hbm, ybuf, sem, slot, group, n_groups)
            group += 1
    _moe_prefetch_finish(ybuf, sem, slot)


def _post_odd_kernel(x_ref, xc_ref, xcp_ref, xcn_ref, p_ref, pp_ref, pn_ref, bg_ref, mod_ref,
                     pw_ref, ps_ref, dw_ref, wo_ref, g2_ref, wr_ref, rb_ref,
                     x3_ref, h2_ref, info_ref, cnt_ref, extc_sc, extp_sc, *, npt, tps):
    i = pl.program_id(0)
    first, last = _seq_edges(i, npt, tps)
    _fill_ext(extc_sc, xc_ref, xcp_ref, xcn_ref, first, last)
    _fill_ext(extp_sc, p_ref, pp_ref, pn_ref, first, last)
    cw = xc_ref.shape[1]
    gc = cw // len(POOL_WINDOWS)
    is_prompt = i < npt
    t_len = jnp.where(is_prompt, TM, TM * tps)
    tile_pos = jnp.where(is_prompt, 0, ((i - npt) % tps) * TM)
    for rs in _row_parts(MIX_ROW_PARTS):
        nrows = rs.stop - rs.start
        base = HALO + rs.start
        pos = lax.broadcasted_iota(jnp.int32, (nrows, 1), 0) + (tile_pos + rs.start)
        ds = []
        for g, w in enumerate(POOL_WINDOWS):
            sl = slice(g * gc, (g + 1) * gc)
            acc = None
            for o in range(-(w // 2), w // 2):
                term = extc_sc[base + o:base + o + nrows, sl]
                acc = term if acc is None else acc + term
            lo = jnp.maximum(pos - w // 2, 0)
            hi = jnp.minimum(pos + w // 2 - 1, t_len - 1)
            cnt = (hi - lo + 1).astype(F32)
            ds.append(acc / cnt - extc_sc[base:base + nrows, sl])
        d = jnp.concatenate(ds, axis=1).astype(BF16)
        yc = jnp.dot(d, pw_ref[...], preferred_element_type=F32) * ps_ref[...]
        conv = None
        for k in range(SCONV_K):
            start = base - SCONV_K // 2 + k
            term = extp_sc[start:start + nrows, :] * dw_ref[k:k + 1, :]
            conv = term if conv is None else conv + term
        yd = bg_ref[rs, :] * conv
        m = (jnp.dot(yc.astype(BF16), wo_ref[0:cw, :], preferred_element_type=F32)
             + jnp.dot(yd.astype(BF16), wo_ref[cw:, :], preferred_element_type=F32))
        x3 = x_ref[rs, :] + mod_ref[0, 2:3, :] * m
        x3_ref[rs, :] = x3
        _moe_prologue(x3, rs, mod_ref, g2_ref, wr_ref, rb_ref, h2_ref, info_ref, cnt_ref)


def _final_kernel(x_ref, dcur_ref, dnxt_ref, cw_ref, yb_hbm, modp_ref, g_ref, op_ref, os_ref, ybuf, sem, *, npt):
    i = pl.program_id(0)
    slot = _moe_gather(dcur_ref, dnxt_ref, yb_hbm, ybuf, sem, prefetch_next=True)
    x = x_ref[...] + modp_ref[0, 5:6, :] * _combined_rows(cw_ref, ybuf, slot, slice(0, TM))
    ms = jnp.mean(x * x, axis=-1, keepdims=True)
    out = x * lax.rsqrt(ms + EPS) * g_ref[...]

    @pl.when(i < npt)
    def _():
        op_ref[...] = out

    @pl.when(i >= npt)
    def _():
        os_ref[...] = out


def _expert_kernel(be_ref, nv_ref, x_ref, wg_ref, wu_ref, wd_ref, y_ref, wg_sc, wu_sc, wd_sc):
    i = pl.program_id(0)

    @pl.when(jnp.logical_or(i == 0, be_ref[i] != be_ref[jnp.maximum(i - 1, 0)]))
    def _():
        wg_sc[...] = wg_ref[0, 0].astype(BF16)
        wu_sc[...] = wu_ref[0, 0].astype(BF16)
        wd_sc[...] = wd_ref[0, 0].astype(BF16)

    @pl.when(i < nv_ref[0])
    def _():
        x = _unpack_bf16_pairs(x_ref[...]).astype(BF16)
        hg = jnp.dot(x, wg_sc[...], preferred_element_type=F32)
        hu = jnp.dot(x, wu_sc[...], preferred_element_type=F32)
        hid = (_silu(hg) * hu).astype(BF16)
        y_ref[...] = _pack_bf16_pairs(jnp.dot(hid, wd_sc[...], preferred_element_type=F32))

    @pl.when(i >= nv_ref[0])
    def _():
        y_ref[...] = jnp.zeros(y_ref.shape, y_ref.dtype)


def _experts(blk_e, nvalid, xb, layer, wg, wu, wd):
    p_len = xb.shape[0]
    d, hid = wg.shape[2:]
    nblk = p_len // MOE_BLK
    grid_spec = pltpu.PrefetchScalarGridSpec(
        num_scalar_prefetch=2,
        grid=(nblk,),
        in_specs=[pl.BlockSpec((MOE_BLK, d // 2), lambda i, be, nv: (jnp.minimum(i, nv[0] - 1), 0)),
                  pl.BlockSpec((1, 1, d, hid), lambda i, be, nv: (layer, be[i], 0, 0)),
                  pl.BlockSpec((1, 1, d, hid), lambda i, be, nv: (layer, be[i], 0, 0)),
                  pl.BlockSpec((1, 1, hid, d), lambda i, be, nv: (layer, be[i], 0, 0))],
        out_specs=pl.BlockSpec((MOE_BLK, d // 2), lambda i, be, nv: (i, 0)),
        scratch_shapes=[pltpu.VMEM((d, hid), BF16), pltpu.VMEM((d, hid), BF16), pltpu.VMEM((hid, d), BF16)],
    )
    return pl.pallas_call(
        _expert_kernel,
        grid_spec=grid_spec,
        out_shape=jax.ShapeDtypeStruct((p_len, d // 2), jnp.uint32),
        compiler_params=_cparams(("arbitrary",)),
        name="moe_experts",
    )(blk_e, nvalid, xb, wg, wu, wd)


def _moe(h2, info, cnt, layer, wg, wu, wd):
    n, d = h2.shape
    cnt = cnt[:, :, 0].astype(jnp.int32)
    counts = jnp.sum(cnt, axis=0)
    pc = ((counts + MOE_BLK - 1) // MOE_BLK) * MOE_BLK
    pend = jnp.cumsum(pc)
    base = (pend - pc)[None, :] + jnp.cumsum(cnt, axis=0) - cnt
    eid = info[:, 0:2, :].astype(jnp.int32)
    rank = info[:, 4:6, :].astype(jnp.int32)
    sel = eid[..., None] == jnp.arange(N_EXPERTS, dtype=jnp.int32)
    dest = jnp.sum(jnp.where(sel, base[:, None, None, :], 0), axis=-1) + rank
    p_len = n * TOP_K + N_EXPERTS * MOE_BLK
    nblk = p_len // MOE_BLK
    blk_start = jnp.arange(nblk, dtype=jnp.int32) * MOE_BLK
    blk_e = jnp.minimum(jnp.sum(pend[None, :] <= blk_start[:, None], axis=1), N_EXPERTS - 1).astype(jnp.int32)
    nvalid = (pend[-1:] // MOE_BLK).astype(jnp.int32)
    dest3 = dest.reshape(dest.shape[0], 1, TOP_K * TM)
    wcol = jnp.transpose(info[:, 2:2 + TOP_K, :], (0, 2, 1)).reshape(n, TOP_K)
    xb = _dispatch(dest3, pend.astype(jnp.int32), h2, p_len)
    yb = _experts(blk_e, nvalid, xb, layer, wg, wu, wd)
    return dest3, wcol, yb


def _rope_tables(t_len, qw):
    rows = t_len // GRID_W
    ax = qw // (DIFF_HEADS * 2) // 2
    row = jnp.repeat(jnp.arange(rows, dtype=F32), GRID_W)
    col = jnp.tile(jnp.arange(GRID_W, dtype=F32), rows)
    inv = ROPE_BASE ** (-jnp.arange(0, ax, 2, dtype=F32) / ax)
    ar = row[:, None] * inv[None, :]
    ac = col[:, None] * inv[None, :]
    cr, sr, cc, sc = jnp.cos(ar), jnp.sin(ar), jnp.cos(ac), jnp.sin(ac)
    cos = jnp.concatenate([cr, cr, cc, cc], axis=1)
    sins = jnp.concatenate([-sr, sr, -sc, sc], axis=1)
    reps = qw // cos.shape[1]
    cos = jnp.concatenate([jnp.ones((TM, cos.shape[1]), F32), cos], axis=0)
    sins = jnp.concatenate([jnp.zeros((TM, sins.shape[1]), F32), sins], axis=0)
    return jnp.tile(cos, (1, reps)), jnp.tile(sins, (1, reps))


def kernel(x_prompt, x_sample, cache_k, cache_v, c, c_ctx, w_ada, b_ada, g_norm1, g_norm2, g_final, w_in_e, conv_a_w, conv_a_b, ln_a_g, ln_a_b, lam_q1, lam_k1, lam_q2, lam_k2, subln_g, w_out_e, w_in_o, pool_w, pool_scale, conv_d_w, w_out_o, w_router_g, b_router_g, w_router_e, b_router_e, w_gate, w_up, w_down):
    bp, tp, d = x_prompt.shape
    bs, ts, _ = x_sample.shape
    past = cache_k.shape[2]
    assert tp == TM and ts % TM == 0 and d % LANES == 0
    npt = bp
    tps = ts // TM
    n_p, n_s = bp * tp, bs * ts
    n = n_p + n_s
    nt = n // TM
    cw = conv_a_w.shape[2]
    qw = (w_in_e.shape[2] - 2 * cw) // 3
    dh = qw // (DIFF_HEADS * 2)
    hpt = TM // HALO

    def cond_of(i):
        return jnp.where(i < npt, 0, 1 + (i - npt) // tps)

    def tile(w):
        return pl.BlockSpec((TM, w), lambda i: (i, 0))

    def halo_prev(w):
        return pl.BlockSpec((HALO, w), lambda i: (jnp.maximum(i * hpt - 1, 0), 0))

    def halo_next(w):
        return pl.BlockSpec((HALO, w), lambda i: (jnp.minimum((i + 1) * hpt, nt * hpt - 1), 0))

    def whole(shape):
        return pl.BlockSpec(shape, lambda i: (0,) * len(shape))

    mod_spec = pl.BlockSpec((1, 6, d), lambda i: (cond_of(i), 0, 0))
    combine_specs = [pl.BlockSpec((1, 1, TOP_K * TM), lambda i: (i, 0, 0), memory_space=pltpu.SMEM),
                     pl.BlockSpec((1, 1, TOP_K * TM), lambda i: (jnp.minimum(i + 1, nt - 1), 0, 0),
                                  memory_space=pltpu.SMEM),
                     tile(TOP_K), pl.BlockSpec(memory_space=pl.ANY)]
    combine_scratch = [pltpu.VMEM((2, TOP_K, TM // SUBLANES, SUBLANES, d // 2), jnp.uint32),
                       pltpu.SemaphoreType.DMA((2,))]

    rows = SUBLANES * pl.cdiv(1 + bs, SUBLANES)
    cond = jnp.concatenate([c_ctx[None, :], c, jnp.zeros((rows - 1 - bs, d), F32)], axis=0)
    mod = _ada_table(cond, w_ada, b_ada)

    xp2, xs2 = x_prompt.reshape(n_p, d), x_sample.reshape(n_s, d)

    def prompt_tile(w):
        return pl.BlockSpec((TM, w), lambda i: (jnp.minimum(i, npt - 1), 0))

    def sample_tile(w):
        return pl.BlockSpec((TM, w), lambda i: (jnp.maximum(i - npt, 0), 0))

    def router_w(i):
        wre = jnp.transpose(w_router_e[i], (1, 0, 2)).reshape(d, N_EXPERTS)
        wr = jnp.concatenate([wre, w_router_g[i]], axis=1)
        return jnp.pad(wr, ((0, 0), (0, ROUTER_PAD - wr.shape[1]))).astype(BF16)

    def router_b(i):
        b = jnp.concatenate([b_router_e[i].reshape(N_EXPERTS), b_router_g[i]])
        return jnp.pad(b, (0, ROUTER_PAD - b.shape[0]))[:, None]

    cos, sins = _rope_tables(ts, qw)
    tab_spec = pl.BlockSpec((TM, qw), lambda i: (jnp.where(i < npt, 0, 1 + (i - npt) % tps), 0))
    kc_spec = pl.BlockSpec((1, 1, TM, DIFF_HEADS, 2, dh), lambda i: (jnp.minimum(i, npt - 1), 0, 0, 0, 0, 0))
    vc_spec = pl.BlockSpec((1, 1, TM, DIFF_HEADS, 2 * dh), lambda i: (jnp.minimum(i, npt - 1), 0, 0, 0, 0))
    u, q, k, vt, new_k, new_v = pl.pallas_call(
        functools.partial(_pre_even_kernel, npt=npt, cw=cw, qw=qw, qscale=dh ** -0.5 * math.log2(math.e)),
        grid=(nt,),
        in_specs=[prompt_tile(d), sample_tile(d), mod_spec, whole((1, d)), whole(w_in_e.shape[1:]),
                  tab_spec, tab_spec],
        out_specs=[tile(cw), tile(qw), tile(qw), pl.BlockSpec((qw, TM), lambda i: (0, i)),
                   kc_spec, vc_spec],
        out_shape=[jax.ShapeDtypeStruct((n, cw), F32), jax.ShapeDtypeStruct((n, qw), BF16),
                   jax.ShapeDtypeStruct((n, qw), BF16), jax.ShapeDtypeStruct((qw, n), BF16),
                   jax.ShapeDtypeStruct((bp, 1, tp, DIFF_HEADS, 2, dh), F32),
                   jax.ShapeDtypeStruct((bp, 1, tp, DIFF_HEADS, 2 * dh), F32)],
        scratch_shapes=[pltpu.VMEM((2, TM, qw), F32)],
        compiler_params=_cparams(("arbitrary",)),
        name="pre_even",
    )(xp2, xs2, mod[0], g_norm1[0][None, :], w_in_e[0].astype(BF16), cos, sins)

    lam_init = 0.8 - 0.6 * math.exp(-0.3 * 0)
    lam = (jnp.exp(jnp.sum(lam_q1[0] * lam_k1[0])) - jnp.exp(jnp.sum(lam_q2[0] * lam_k2[0])) + lam_init)
    lam = lam.reshape(1).astype(F32)
    sub_g = subln_g[0][:, None]
    o_p = _attention(lam, q, k, vt, sub_g, None, batch=bp, t_len=tp, row0=0, tq=TM, tk=TM,
                     post_scale=1.0 - lam_init, heads_per_step=DIFF_HEADS)
    k_ctx = cache_k[:, 0].reshape(bs * past, qw).astype(BF16)
    vt_ctx = jnp.transpose(cache_v[:, 0].reshape(bs, past, qw), (0, 2, 1)).astype(BF16)
    o_s = _attention(lam, q, k, vt, sub_g, (k_ctx, vt_ctx), batch=bs, t_len=ts, row0=n_p, tq=ATTN_TQ, tk=ATTN_TK,
                     post_scale=1.0 - lam_init)

    moe_outs = [jax.ShapeDtypeStruct((n, d), F32), jax.ShapeDtypeStruct((n, d // 2), jnp.uint32),
                jax.ShapeDtypeStruct((nt, INFO_ROWS, TM), F32), jax.ShapeDtypeStruct((nt, N_EXPERTS, 1), F32)]
    moe_out_specs = [tile(d), tile(d // 2), pl.BlockSpec((1, INFO_ROWS, TM), lambda i: (i, 0, 0)),
                     pl.BlockSpec((1, N_EXPERTS, 1), lambda i: (i, 0, 0))]
    x1, h2, info, cnt = pl.pallas_call(
        functools.partial(_post_even_kernel, npt=npt, tps=tps),
        grid=(nt,),
        in_specs=[prompt_tile(d), sample_tile(d), tile(cw), halo_prev(cw), halo_next(cw),
                  prompt_tile(qw), sample_tile(qw), mod_spec,
                  whole((CONV_K, cw)), whole((1, cw)), whole((1, cw)), whole((1, cw)),
                  whole(w_out_e.shape[1:]), whole((1, d)), whole((d, ROUTER_PAD)), whole((ROUTER_PAD, 1))],
        out_specs=moe_out_specs,
        out_shape=moe_outs,
        scratch_shapes=[pltpu.VMEM((TM + 2 * HALO, cw), F32),
                        pltpu.VMEM((SUBLANES, TM + 2 * HALO - SUBLANES, cw), F32)],
        compiler_params=_cparams(("parallel",)),
        name="post_even",
    )(xp2, xs2, u, u, u, o_p, o_s, mod[0], conv_a_w[0], conv_a_b[0][None, :], ln_a_g[0][None, :], ln_a_b[0][None, :],
      w_out_e[0].astype(BF16), g_norm2[0][None, :], router_w(0), router_b(0))
    dest3, wcol, yb = _moe(h2, info, cnt, 0, w_gate, w_up, w_down)

    pw = pool_w.shape[2]
    x2, xc, p, bg = pl.pallas_call(
        functools.partial(_pre_odd_kernel, cw=cw),
        grid=(nt,),
        in_specs=[tile(d)] + combine_specs + [mod_spec, mod_spec, whole((1, d)), whole(w_in_o.shape[1:])],
        out_specs=[tile(d), tile(cw), tile(cw), tile(cw)],
        out_shape=[jax.ShapeDtypeStruct((n, d), F32)] + [jax.ShapeDtypeStruct((n, cw), F32)] * 3,
        scratch_shapes=combine_scratch,
        compiler_params=_cparams(("arbitrary",)),
        name="pre_odd",
    )(x1, dest3, dest3, wcol, yb, mod[0], mod[1], g_norm1[1][None, :], w_in_o[0].astype(BF16))
    pool_bd = jnp.zeros((cw, cw), F32)
    for g in range(len(POOL_WINDOWS)):
        pool_bd = pool_bd.at[g * pw:(g + 1) * pw, g * pw:(g + 1) * pw].set(pool_w[0, g])
    x3, h2, info, cnt = pl.pallas_call(
        functools.partial(_post_odd_kernel, npt=npt, tps=tps),
        grid=(nt,),
        in_specs=[tile(d), tile(cw), halo_prev(cw), halo_next(cw), tile(cw), halo_prev(cw), halo_next(cw),
                  tile(cw), mod_spec, whole((cw, cw)), whole((1, cw)), whole((SCONV_K, cw)),
                  whole(w_out_o.shape[1:]), whole((1, d)), whole((d, ROUTER_PAD)), whole((ROUTER_PAD, 1))],
        out_specs=moe_out_specs,
        out_shape=moe_outs,
        scratch_shapes=[pltpu.VMEM((TM + 2 * HALO, cw), F32), pltpu.VMEM((TM + 2 * HALO, cw), F32)],
        compiler_params=_cparams(("parallel",)),
        name="post_odd",
    )(x2, xc, xc, xc, p, p, p, bg, mod[1], pool_bd.astype(BF16), pool_scale[0][None, :], conv_d_w[0],
      w_out_o[0].astype(BF16), g_norm2[1][None, :], router_w(1), router_b(1))
    dest3, wcol, yb = _moe(h2, info, cnt, 1, w_gate, w_up, w_down)

    out_p, out_s = pl.pallas_call(
        functools.partial(_final_kernel, npt=npt),
        grid=(nt,),
        in_specs=[tile(d)] + combine_specs + [mod_spec, whole((1, d))],
        out_specs=[pl.BlockSpec((TM, d), lambda i: (jnp.minimum(i, npt - 1), 0)),
                   pl.BlockSpec((TM, d), lambda i: (jnp.maximum(i - npt, 0), 0))],
        out_shape=[jax.ShapeDtypeStruct((n_p, d), F32), jax.ShapeDtypeStruct((n_s, d), F32)],
        scratch_shapes=combine_scratch,
        compiler_params=_cparams(("arbitrary",)),
        name="final_norm",
    )(x3, dest3, dest3, wcol, yb, mod[1], g_final[None, :])

    return (out_p.reshape(bp, tp, d), out_s.reshape(bs, ts, d), new_k, new_v)
```

```python
import functools
import math

import jax
import jax.numpy as jnp
from jax import lax
from jax.experimental import pallas as pl
from jax.experimental.pallas import tpu as pltpu

F32 = jnp.float32
BF16 = jnp.bfloat16

EPS = 1e-6
GRID_W = 64
ROPE_BASE = 10000.0
CONV_K = 31
SCONV_K = 3
POOL_WINDOWS = (2, 4, 8, 16)
DIFF_HEADS = 4
N_GROUPS = 4
EXPERTS_PER_GROUP = 8
N_EXPERTS = N_GROUPS * EXPERTS_PER_GROUP
TOP_K = 2

LANES = 128
SUBLANES = 8
TM = 256
HALO = 16
PROJ_ROW_PARTS = 2
MIX_ROW_PARTS = 1
PROJ_COLS = 256
MOE_BLK = 1024
ZERO_ROWS = 256
ATTN_TQ = 2048
ATTN_TK = 512
CONV_ROWS = 32
INFO_ROWS = 8
ROUTER_PAD = LANES
ONES_ROWS = 2 * SUBLANES
DISPATCH_SLOTS = 3
VMEM_LIMIT = 56 * 1024 * 1024


def _cparams(sem):
    return pltpu.CompilerParams(dimension_semantics=sem, vmem_limit_bytes=VMEM_LIMIT)


def _rms_mod(x, g, sc, sh):
    ms = jnp.mean(x * x, axis=-1, keepdims=True)
    return (x * lax.rsqrt(ms + EPS)) * g * (1.0 + sc) + sh


def _silu(x):
    return x * jax.nn.sigmoid(x)


def _pack_bf16_pairs(y):
    half = y.shape[1] // 2
    lo = pltpu.bitcast(y[:, :half].astype(BF16).astype(F32), jnp.uint32)
    hi = pltpu.bitcast(y[:, half:].astype(BF16).astype(F32), jnp.uint32)
    return lax.shift_right_logical(lo, jnp.uint32(16)) | hi


def _unpack_bf16_pairs(w):
    lo = pltpu.bitcast(lax.shift_left(w, jnp.uint32(16)), F32)
    hi = pltpu.bitcast(w & jnp.uint32(0xFFFF0000), F32)
    return jnp.concatenate([lo, hi], axis=1)


def _ada_kernel(c_ref, w_ref, b_ref, o_ref):
    cs = _silu(c_ref[...])
    o_ref[0] = jnp.dot(cs, w_ref[0], precision=lax.Precision.HIGHEST,
                       preferred_element_type=F32) + b_ref[0]


def _ada_table(cond, w_ada, b_ada):
    depth, d, six_d = w_ada.shape
    rows = cond.shape[0]
    nj = six_d // d
    out = pl.pallas_call(
        _ada_kernel,
        grid=(depth, nj),
        in_specs=[pl.BlockSpec((rows, d), lambda l, j: (0, 0)),
                  pl.BlockSpec((1, d, d), lambda l, j: (l, 0, j)),
                  pl.BlockSpec((1, 1, d), lambda l, j: (l, 0, j))],
        out_specs=pl.BlockSpec((1, rows, d), lambda l, j: (l, 0, j)),
        out_shape=jax.ShapeDtypeStruct((depth, rows, six_d), F32),
        compiler_params=_cparams(("parallel", "parallel")),
        name="ada_table",
    )(cond, w_ada, b_ada.reshape(depth, 1, six_d))
    return out.reshape(depth, rows, nj, d)


def _row_parts(parts):
    rows = TM // parts
    return [slice(p * rows, (p + 1) * rows) for p in range(parts)]


def _rope(x, cos, sins, half):
    lane = lax.broadcasted_iota(jnp.int32, (1, LANES), 1)
    first_half = (lane % (2 * half)) < half
    outs = []
    for j in range(x.shape[1] // LANES):
        sl = slice(j * LANES, (j + 1) * LANES)
        xs = x[:, sl]
        nxt = pltpu.roll(xs, LANES - half, 1)
        prv = pltpu.roll(xs, half, 1)
        rot = jnp.where(first_half, nxt, prv)
        outs.append(xs * cos[:, sl] + rot * sins[:, sl])
    return jnp.concatenate(outs, axis=1)


def _pre_even_kernel(xp_ref, xs_ref, mod_ref, g_ref, w_ref, cos_ref, sin_ref,
                     u_ref, q_ref, k_ref, vt_ref, kc_ref, vc_ref, kv_sc, *, npt, cw, qw, qscale):
    i = pl.program_id(0)
    is_prompt = i < npt
    half = kc_ref.shape[-1] // 4
    for sl in _row_parts(PROJ_ROW_PARTS):
        x = jnp.where(is_prompt, xp_ref[sl, :], xs_ref[sl, :])
        h = _rms_mod(x, g_ref[...], mod_ref[0, 1:2, :], mod_ref[0, 0:1, :])
        proj = jnp.dot(h.astype(BF16), w_ref[...], preferred_element_type=F32)
        ga = proj[:, :cw]
        gb = proj[:, cw:2 * cw]
        u_ref[sl, :] = ga * jax.nn.sigmoid(gb)
        q = proj[:, 2 * cw:2 * cw + qw]
        k = proj[:, 2 * cw + qw:2 * cw + 2 * qw]
        v = proj[:, 2 * cw + 2 * qw:]
        kv_sc[0, sl, :] = k
        kv_sc[1, sl, :] = v
        cos = cos_ref[sl, :]
        sins = sin_ref[sl, :]
        q_ref[sl, :] = (_rope(q, cos, sins, half) * qscale).astype(BF16)
        k_ref[sl, :] = _rope(k, cos, sins, half).astype(BF16)
        vt_ref[:, sl] = v.T.astype(BF16)

    @pl.when(is_prompt)
    def _():
        heads, halves, dh = kc_ref.shape[3:]
        for h in range(heads):
            vc_ref[0, 0, :, h, :] = kv_sc[1, :, h * halves * dh:(h + 1) * halves * dh]
            for c in range(halves):
                lo = (h * halves + c) * dh
                kc_ref[0, 0, :, h, c, :] = kv_sc[0, :, lo:lo + dh]


def _attn_kernel(lam_ref, q_ref, k_ref, vt_ref, *rest, nk, tk, nk_ctx, post_scale):
    if nk_ctx:
        kc_ref, vtc_ref, g_ref, o_ref, s_sc, mc_sc, m_sc, acc_sc = rest
    else:
        g_ref, o_ref, s_sc, mc_sc, m_sc, acc_sc = rest
    q = q_ref[...]
    hw = q.shape[1]
    dh = hw // 2
    lane = lax.broadcasted_iota(jnp.int32, (1, hw), 1)
    zero = jnp.zeros_like(q)
    qs = (jnp.where(lane < dh, q, zero), jnp.where(lane >= dh, q, zero))
    m_sc[...] = jnp.full(m_sc.shape, -jnp.inf, F32)
    acc_sc[...] = jnp.zeros(acc_sc.shape, F32)
    ones_rows = (lax.broadcasted_iota(jnp.int32, (ONES_ROWS, tk), 0) == 0).astype(BF16)

    def scores(kk, slot):
        for c in range(2):
            s = lax.dot_general(kk, qs[c], (((1,), (1,)), ((), ())), preferred_element_type=F32)
            s_sc[slot, c] = s
            mc_sc[slot, c] = jnp.max(s, axis=0, keepdims=True)

    def consume(vt, slot):
        vta = jnp.concatenate([vt, ones_rows], axis=0)
        for c in range(2):
            m_prev = m_sc[c]
            m_new = jnp.maximum(m_prev, mc_sc[slot, c])
            alpha = jnp.exp2(m_prev - m_new)
            p = jnp.exp2(s_sc[slot, c] - m_new).astype(BF16)
            acc_sc[c] = alpha * acc_sc[c] + jnp.dot(vta, p, preferred_element_type=F32)
            m_sc[c] = m_new

    def k_at(i):
        if i < nk_ctx:
            return kc_ref[i * tk:(i + 1) * tk, :]
        return k_ref[(i - nk_ctx) * tk:(i - nk_ctx + 1) * tk, :]

    def vt_at(i):
        if i < nk_ctx:
            return vtc_ref[0, :, i * tk:(i + 1) * tk]
        return vt_ref[:, (i - nk_ctx) * tk:(i - nk_ctx + 1) * tk]

    n_total = nk_ctx + nk
    peel = nk_ctx + (n_total - 1 - nk_ctx) % 2
    scores(k_at(0), 0)
    for i in range(peel):
        scores(k_at(i + 1), (i + 1) % 2)
        consume(vt_at(i), i % 2)

    def body(t, carry):
        for r in range(2):
            j = 2 * t + (peel + r - nk_ctx)
            nxt = pl.multiple_of((j + 1) * tk, tk)
            cur = pl.multiple_of(j * tk, tk)
            scores(k_ref[pl.ds(nxt, tk), :], (peel + r + 1) % 2)
            consume(vt_ref[:, pl.ds(cur, tk)], (peel + r) % 2)
        return carry

    lax.fori_loop(0, (n_total - 1 - peel) // 2, body, 0)
    consume(vt_at(n_total - 1), (n_total - 1) % 2)
    o = (acc_sc[0, :hw, :] / acc_sc[0, hw:hw + 1, :]
         - lam_ref[0] * (acc_sc[1, :hw, :] / acc_sc[1, hw:hw + 1, :]))
    ms = jnp.mean(o * o, axis=0, keepdims=True)
    o = o * lax.rsqrt(ms + EPS) * g_ref[...] * post_scale
    o_ref[...] = o.T.astype(BF16)


def _attn_heads_kernel(lam_ref, q_ref, k_ref, vt_ref, g_ref, o_ref, *scratch, heads, **kw):
    hw = q_ref.shape[1] // heads
    for h in range(heads):
        cols = pl.ds(h * hw, hw)
        _attn_kernel(lam_ref, q_ref.at[:, cols], k_ref.at[:, cols], vt_ref.at[cols, :], g_ref, o_ref.at[:, cols],
                     *scratch, **kw)


def _attention(lam, q, k, vt, g_col, ctx, *, batch, t_len, row0, tq, tk, post_scale, heads_per_step=1):
    hw = q.shape[1] // DIFF_HEADS
    bw = hw * heads_per_step
    nq = t_len // tq
    assert row0 % t_len == 0 and t_len % tk == 0 and (ctx is None or heads_per_step == 1)
    qb0 = row0 // tq
    sb0 = row0 // t_len
    in_specs = [pl.BlockSpec(memory_space=pltpu.SMEM),
                pl.BlockSpec((tq, bw), lambda b, h, i: (qb0 + b * nq + i, h)),
                pl.BlockSpec((t_len, bw), lambda b, h, i: (sb0 + b, h)),
                pl.BlockSpec((bw, t_len), lambda b, h, i: (h, sb0 + b))]
    args = [lam, q, k, vt]
    nk_ctx = 0
    if ctx is not None:
        k_ctx, vt_ctx = ctx
        past = vt_ctx.shape[2]
        assert past % tk == 0
        nk_ctx = past // tk
        in_specs += [pl.BlockSpec((past, hw), lambda b, h, i: (b, h)),
                     pl.BlockSpec((1, hw, past), lambda b, h, i: (b, h, 0))]
        args += [k_ctx, vt_ctx]
    in_specs.append(pl.BlockSpec((hw, 1), lambda b, h, i: (0, 0)))
    args.append(g_col)
    body = functools.partial(_attn_kernel, nk=t_len // tk, tk=tk, nk_ctx=nk_ctx, post_scale=post_scale)
    if heads_per_step > 1:
        body = functools.partial(_attn_heads_kernel, heads=heads_per_step, nk=t_len // tk, tk=tk, nk_ctx=nk_ctx,
                                 post_scale=post_scale)
    return pl.pallas_call(
        body,
        grid=(batch, DIFF_HEADS // heads_per_step, nq),
        in_specs=in_specs,
        out_specs=pl.BlockSpec((tq, bw), lambda b, h, i: (b * nq + i, h)),
        out_shape=jax.ShapeDtypeStruct((batch * t_len, q.shape[1]), BF16),
        scratch_shapes=[pltpu.VMEM((2, 2, tk, tq), F32), pltpu.VMEM((2, 2, 1, tq), F32),
                        pltpu.VMEM((2, 1, tq), F32), pltpu.VMEM((2, hw + ONES_ROWS, tq), F32)],
        compiler_params=_cparams(("parallel", "parallel", "parallel")),
        name="diff_attention",
    )(*args)


def _seq_edges(i, npt, tps):
    r = (i - npt) % tps
    is_prompt = i < npt
    first = jnp.logical_or(is_prompt, r == 0)
    last = jnp.logical_or(is_prompt, r == tps - 1)
    return first, last


def _fill_ext(ext_sc, cur_ref, prev_ref, next_ref, first, last):
    prev = prev_ref[...]
    nxt = next_ref[...]
    ext_sc[0:HALO, :] = jnp.where(first, jnp.zeros_like(prev), prev)
    ext_sc[HALO:HALO + TM, :] = cur_ref[...]
    ext_sc[HALO + TM:, :] = jnp.where(last, jnp.zeros_like(nxt), nxt)


def _route_tile(logits, b_ref, info_ref, cnt_ref):
    lt = logits.T + b_ref[...]
    gl = lt[N_EXPERTS:N_EXPERTS + N_GROUPS]
    ge = jnp.exp(gl - jnp.max(gl, axis=0, keepdims=True))
    gp = ge / jnp.sum(ge, axis=0, keepdims=True)
    best = gp[0:1]
    gidx = jnp.zeros(best.shape, jnp.int32)
    el = lt[0:EXPERTS_PER_GROUP]
    for g in range(1, N_GROUPS):
        better = gp[g:g + 1] > best
        gidx = jnp.where(better, g, gidx)
        best = jnp.where(better, gp[g:g + 1], best)
    for g in range(1, N_GROUPS):
        el = jnp.where(gidx == g, lt[g * EXPERTS_PER_GROUP:(g + 1) * EXPERTS_PER_GROUP], el)
    rio = lax.broadcasted_iota(jnp.int32, el.shape, 0)
    v0 = jnp.max(el, axis=0, keepdims=True)
    i0 = jnp.min(jnp.where(el == v0, rio, EXPERTS_PER_GROUP), axis=0, keepdims=True)
    el2 = jnp.where(rio == i0, -jnp.inf, el)
    v1 = jnp.max(el2, axis=0, keepdims=True)
    i1 = jnp.min(jnp.where(el2 == v1, rio, EXPERTS_PER_GROUP), axis=0, keepdims=True)
    e1 = jnp.exp(v1 - v0)
    den = 1.0 + e1
    w0 = (1.0 / den) * best
    w1 = (e1 / den) * best
    eid0 = gidx * EXPERTS_PER_GROUP + i0
    eid1 = gidx * EXPERTS_PER_GROUP + i1
    tm = lt.shape[1]
    eio = lax.broadcasted_iota(jnp.int32, (N_EXPERTS, tm), 0)
    oh0 = eio == eid0
    oh1 = eio == eid1
    ohs = jnp.where(oh0, 1.0, 0.0) + jnp.where(oh1, 1.0, 0.0)
    upper = (lax.broadcasted_iota(jnp.int32, (tm, tm), 0)
             < lax.broadcasted_iota(jnp.int32, (tm, tm), 1)).astype(BF16)
    before = jnp.dot(ohs.astype(BF16), upper, preferred_element_type=F32)
    rank0 = jnp.sum(jnp.where(oh0, before, 0.0), axis=0, keepdims=True)
    rank1 = jnp.sum(jnp.where(oh1, before, 0.0), axis=0, keepdims=True)
    info_ref[0] = jnp.concatenate([eid0.astype(F32), eid1.astype(F32), w0, w1, rank0, rank1,
                                   jnp.zeros((INFO_ROWS - 6, tm), F32)], axis=0)
    cnt_ref[0] = jnp.sum(ohs, axis=1, keepdims=True)


def _moe_prologue(x_new, sl, mod_ref, g2_ref, wr_ref, rb_ref, h2_ref, info_ref, cnt_ref):
    h2 = _rms_mod(x_new, g2_ref[...], mod_ref[0, 4:5, :], mod_ref[0, 3:4, :])
    h2_ref[sl, :] = _pack_bf16_pairs(h2)
    assert sl.stop - sl.start == TM
    _route_tile(jnp.dot(h2.astype(BF16), wr_ref[...], preferred_element_type=F32), rb_ref, info_ref, cnt_ref)


def _post_even_kernel(xp_ref, xs_ref, u_ref, up_ref, un_ref, op_ref, os_ref, mod_ref, cw_ref, cb_ref,
                      lng_ref, lnb_ref, wo_ref, g2_ref, wr_ref, rb_ref, x1_ref, h2_ref, info_ref, cnt_ref,
                      ext_sc, sh_sc, *, npt, tps):
    i = pl.program_id(0)
    first, last = _seq_edges(i, npt, tps)
    _fill_ext(ext_sc, u_ref, up_ref, un_ref, first, last)
    cw = u_ref.shape[1]
    span = sh_sc.shape[1]
    for r in range(SUBLANES):
        sh_sc[r] = ext_sc[r:r + span, :]
    rows = CONV_ROWS
    is_prompt = i < npt
    for sl in _row_parts(MIX_ROW_PARTS):
        chunks = []
        for r0 in range(sl.start, sl.stop, rows):
            acc = None
            for k in range(CONV_K):
                off = HALO - CONV_K // 2 + k
                start = r0 + off - off % SUBLANES
                term = sh_sc[off % SUBLANES, start:start + rows, :] * cw_ref[k:k + 1, :]
                acc = term if acc is None else acc + term
            chunks.append(acc)
        conv = jnp.concatenate(chunks, axis=0) + cb_ref[...]
        mu = jnp.mean(conv, axis=-1, keepdims=True)
        cen = conv - mu
        var = jnp.mean(cen * cen, axis=-1, keepdims=True)
        uu = _silu(cen * lax.rsqrt(var + EPS) * lng_ref[...] + lnb_ref[...])
        o = jnp.where(is_prompt, op_ref[sl, :], os_ref[sl, :])
        m = (jnp.dot(uu.astype(BF16), wo_ref[0:cw, :], preferred_element_type=F32)
             + jnp.dot(o, wo_ref[cw:, :], preferred_element_type=F32))
        x1 = jnp.where(is_prompt, xp_ref[sl, :], xs_ref[sl, :]) + mod_ref[0, 2:3, :] * m
        x1_ref[sl, :] = x1
        _moe_prologue(x1, sl, mod_ref, g2_ref, wr_ref, rb_ref, h2_ref, info_ref, cnt_ref)


def _gather_start(dest_ref, yb_hbm, ybuf, sem, slot):
    def body(i, carry):
        for j in range(SUBLANES):
            for k in range(TOP_K):
                row = dest_ref[0, 0, k * TM + i * SUBLANES + j]
                pltpu.make_async_copy(yb_hbm.at[pl.ds(row, 1)], ybuf.at[slot, k, i, pl.ds(j, 1)],
                                      sem.at[slot]).start(priority=k % 2)
        return carry

    lax.fori_loop(0, TM // SUBLANES, body, 0)


def _moe_gather(dcur_ref, dnxt_ref, yb_hbm, ybuf, sem, prefetch_next):
    t = pl.program_id(0)
    slot = t % 2

    @pl.when(t == 0)
    def _():
        _gather_start(dcur_ref, yb_hbm, ybuf, sem, 0)

    if prefetch_next:
        @pl.when(t + 1 < pl.num_programs(0))
        def _():
            _gather_start(dnxt_ref, yb_hbm, ybuf, sem, 1 - slot)

    _gather_wait(ybuf, sem, slot)
    return slot


def _gather_wait(ybuf, sem, slot):
    for k in range(TOP_K):
        pltpu.make_async_copy(ybuf.at[slot, k], ybuf.at[slot, k], sem.at[slot]).wait()


def _moe_prefetch_group(dnxt_ref, yb_hbm, ybuf, sem, slot, g, n_groups):
    n_i = TM // SUBLANES
    for i in range(g * n_i // n_groups, (g + 1) * n_i // n_groups):
        for j in range(SUBLANES):
            for k in range(TOP_K):
                row = dnxt_ref[0, 0, k * TM + i * SUBLANES + j]
                pltpu.make_async_copy(yb_hbm.at[pl.ds(row, 1)], ybuf.at[1 - slot, k, i, pl.ds(j, 1)],
                                      sem.at[1 - slot]).start(priority=k % 2)


def _moe_prefetch_finish(ybuf, sem, slot):
    @pl.when(pl.program_id(0) == pl.num_programs(0) - 1)
    def _():
        _gather_wait(ybuf, sem, 1 - slot)


def _combined_rows(cw_ref, ybuf, slot, sl):
    dw = ybuf.shape[-1]
    g = slice(sl.start // SUBLANES, sl.stop // SUBLANES)
    w = cw_ref[sl, :]
    y = w[:, 0:1] * _unpack_bf16_pairs(ybuf[slot, 0, g].reshape(sl.stop - sl.start, dw))
    for k in range(1, TOP_K):
        y = y + w[:, k:k + 1] * _unpack_bf16_pairs(ybuf[slot, k, g].reshape(sl.stop - sl.start, dw))
    return y


def _dispatch_kernel(dest_ref, pend_ref, h2_hbm, xb_hbm, stage, zeros, in_sem, row_sem, zero_sem):
    t = pl.program_id(0)
    nt = pl.num_programs(0)
    slot = t % DISPATCH_SLOTS

    tile_rows = TM // SUBLANES

    def stage_in(tile_idx, s):
        return pltpu.make_async_copy(h2_hbm.at[pl.ds(tile_idx * tile_rows, tile_rows)], stage.at[s], in_sem.at[s])

    def drain_rows(s):
        for _ in range(TOP_K):
            pltpu.make_async_copy(stage.at[s], stage.at[s], row_sem.at[s]).wait()

    def zero_block(start):
        return pltpu.make_async_copy(zeros, xb_hbm.at[pl.ds(pl.multiple_of(start, MOE_BLK), MOE_BLK)], zero_sem)

    def zero_granule(start):
        return pltpu.make_async_copy(zeros.at[pl.ds(0, ZERO_ROWS)],
                                     xb_hbm.at[pl.ds(pl.multiple_of(start, ZERO_ROWS), ZERO_ROWS)], zero_sem)

    def zero_fill(op):
        for e in range(N_EXPERTS):
            for q in range(MOE_BLK // ZERO_ROWS):
                end = pend_ref[0, e] - q * ZERO_ROWS

                @pl.when(end > pend_ref[1, e])
                def _():
                    op(zero_granule(end - ZERO_ROWS))
        for j in range(N_EXPERTS):
            start = pend_ref[0, N_EXPERTS - 1] + j * MOE_BLK

            @pl.when(start < xb_hbm.shape[0])
            def _():
                op(zero_block(start))

    @pl.when(t == 0)
    def _():
        stage_in(0, 0).start()
        zeros[...] = jnp.zeros(zeros.shape, zeros.dtype)
        zero_fill(lambda cp: cp.start())
        zero_fill(lambda cp: cp.wait())

    @pl.when(jnp.logical_and(t == 0, nt > 1))
    def _():
        stage_in(1, 1).start()

    stage_in(t, slot).wait()

    def body(i, carry):
        for j in range(SUBLANES):
            for k in range(TOP_K):
                row = dest_ref[0, 0, k * TM + i * SUBLANES + j]
                pltpu.make_async_copy(stage.at[slot, i, pl.ds(j, 1)], xb_hbm.at[pl.ds(row, 1)],
                                      row_sem.at[slot]).start(priority=k % 2)
        return carry

    lax.fori_loop(0, tile_rows, body, 0)

    @pl.when(t > 0)
    def _():
        drain_rows((t + DISPATCH_SLOTS - 1) % DISPATCH_SLOTS)

    @pl.when(t + 2 < nt)
    def _():
        stage_in(t + 2, (t + 2) % DISPATCH_SLOTS).start()

    @pl.when(t == nt - 1)
    def _():
        drain_rows(slot)


def _dispatch(dest3, pend, h2, p_len):
    n, d = h2.shape
    return pl.pallas_call(
        _dispatch_kernel,
        grid=(n // TM,),
        in_specs=[pl.BlockSpec((1, 1, TOP_K * TM), lambda t: (t, 0, 0), memory_space=pltpu.SMEM),
                  pl.BlockSpec(memory_space=pltpu.SMEM), pl.BlockSpec(memory_space=pl.ANY)],
        out_specs=pl.BlockSpec(memory_space=pl.ANY),
        out_shape=jax.ShapeDtypeStruct((p_len, d), h2.dtype),
        scratch_shapes=[pltpu.VMEM((DISPATCH_SLOTS, TM // SUBLANES, SUBLANES, d), h2.dtype),
                        pltpu.VMEM((MOE_BLK, d), h2.dtype),
                        pltpu.SemaphoreType.DMA((DISPATCH_SLOTS,)), pltpu.SemaphoreType.DMA((DISPATCH_SLOTS,)),
                        pltpu.SemaphoreType.DMA(())],
        compiler_params=_cparams(("arbitrary",)),
        name="moe_dispatch",
    )(dest3, pend, h2.reshape(n // SUBLANES, SUBLANES, d))


def _pre_odd_kernel(x_ref, dcur_ref, dnxt_ref, cw_ref, yb_hbm, modp_ref, mod_ref, g_ref, w_ref,
                    x2_ref, xc_ref, p_ref, bg_ref, ybuf, sem, *, cw):
    slot = _moe_gather(dcur_ref, dnxt_ref, yb_hbm, ybuf, sem, prefetch_next=False)
    col_blocks = w_ref.shape[1] // PROJ_COLS
    n_groups = PROJ_ROW_PARTS * (1 + col_blocks)
    group = 0
    for sl in _row_parts(PROJ_ROW_PARTS):
        x2 = x_ref[sl, :] + modp_ref[0, 5:6, :] * _combined_rows(cw_ref, ybuf, slot, sl)
        x2_ref[sl, :] = x2
        _moe_prefetch_group(dnxt_ref, yb_hbm, ybuf, sem, slot, group, n_groups)
        group += 1
        h = _rms_mod(x2, g_ref[...], mod_ref[0, 1:2, :], mod_ref[0, 0:1, :]).astype(BF16)
        hd = []
        for c in range(col_blocks):
            cols = slice(c * PROJ_COLS, (c + 1) * PROJ_COLS)
            blk = jnp.dot(h, w_ref[:, cols], preferred_element_type=F32)
            which, off = divmod(c * PROJ_COLS, cw)
            dst = slice(off, off + PROJ_COLS)
            if which == 0:
                xc_ref[sl, dst] = blk
            elif which == 1:
                hd.append(blk)
            elif which == 2:
                bg_ref[sl, dst] = blk
            else:
                p_ref[sl, dst] = blk * hd[off // PROJ_COLS]
            _moe_prefetch_group(dnxt_ref, yb_hbm, ybuf, sem, slot, group, n_groups)
            group += 1
    _moe_prefetch_finish(ybuf, sem, slot)


def _post_odd_kernel(x_ref, xc_ref, xcp_ref, xcn_ref, p_ref, pp_ref, pn_ref, bg_ref, mod_ref,
                     pw_ref, ps_ref, dw_ref, wo_ref, g2_ref, wr_ref, rb_ref,
                     x3_ref, h2_ref, info_ref, cnt_ref, extc_sc, extp_sc, *, npt, tps):
    i = pl.program_id(0)
    first, last = _seq_edges(i, npt, tps)
    _fill_ext(extc_sc, xc_ref, xcp_ref, xcn_ref, first, last)
    _fill_ext(extp_sc, p_ref, pp_ref, pn_ref, first, last)
    cw = xc_ref.shape[1]
    gc = cw // len(POOL_WINDOWS)
    is_prompt = i < npt
    t_len = jnp.where(is_prompt, TM, TM * tps)
    tile_pos = jnp.where(is_prompt, 0, ((i - npt) % tps) * TM)
    for rs in _row_parts(MIX_ROW_PARTS):
        nrows = rs.stop - rs.start
        base = HALO + rs.start
        pos = lax.broadcasted_iota(jnp.int32, (nrows, 1), 0) + (tile_pos + rs.start)
        ds = []
        for g, w in enumerate(POOL_WINDOWS):
            sl = slice(g * gc, (g + 1) * gc)
            acc = None
            for o in range(-(w // 2), w // 2):
                term = extc_sc[base + o:base + o + nrows, sl]
                acc = term if acc is None else acc + term
            lo = jnp.maximum(pos - w // 2, 0)
            hi = jnp.minimum(pos + w // 2 - 1, t_len - 1)
            cnt = (hi - lo + 1).astype(F32)
            ds.append(acc / cnt - extc_sc[base:base + nrows, sl])
        d = jnp.concatenate(ds, axis=1).astype(BF16)
        yc = jnp.dot(d, pw_ref[...], preferred_element_type=F32) * ps_ref[...]
        conv = None
        for k in range(SCONV_K):
            start = base - SCONV_K // 2 + k
            term = extp_sc[start:start + nrows, :] * dw_ref[k:k + 1, :]
            conv = term if conv is None else conv + term
        yd = bg_ref[rs, :] * conv
        m = (jnp.dot(yc.astype(BF16), wo_ref[0:cw, :], preferred_element_type=F32)
             + jnp.dot(yd.astype(BF16), wo_ref[cw:, :], preferred_element_type=F32))
        x3 = x_ref[rs, :] + mod_ref[0, 2:3, :] * m
        x3_ref[rs, :] = x3
        _moe_prologue(x3, rs, mod_ref, g2_ref, wr_ref, rb_ref, h2_ref, info_ref, cnt_ref)


def _final_kernel(x_ref, dcur_ref, dnxt_ref, cw_ref, yb_hbm, modp_ref, g_ref, op_ref, os_ref, ybuf, sem, *, npt):
    i = pl.program_id(0)
    slot = _moe_gather(dcur_ref, dnxt_ref, yb_hbm, ybuf, sem, prefetch_next=True)
    x = x_ref[...] + modp_ref[0, 5:6, :] * _combined_rows(cw_ref, ybuf, slot, slice(0, TM))
    ms = jnp.mean(x * x, axis=-1, keepdims=True)
    out = x * lax.rsqrt(ms + EPS) * g_ref[...]

    @pl.when(i < npt)
    def _():
        op_ref[...] = out

    @pl.when(i >= npt)
    def _():
        os_ref[...] = out


def _expert_kernel(be_ref, nv_ref, x_ref, wg_ref, wu_ref, wd_ref, y_ref, wg_sc, wu_sc, wd_sc):
    i = pl.program_id(0)

    @pl.when(jnp.logical_or(i == 0, be_ref[i] != be_ref[jnp.maximum(i - 1, 0)]))
    def _():
        wg_sc[...] = wg_ref[0, 0].astype(BF16)
        wu_sc[...] = wu_ref[0, 0].astype(BF16)
        wd_sc[...] = wd_ref[0, 0].astype(BF16)

    @pl.when(i < nv_ref[0])
    def _():
        x = _unpack_bf16_pairs(x_ref[...]).astype(BF16)
        hg = jnp.dot(x, wg_sc[...], preferred_element_type=F32)
        hu = jnp.dot(x, wu_sc[...], preferred_element_type=F32)
        hid = (_silu(hg) * hu).astype(BF16)
        y_ref[...] = _pack_bf16_pairs(jnp.dot(hid, wd_sc[...], preferred_element_type=F32))

    @pl.when(i >= nv_ref[0])
    def _():
        y_ref[...] = jnp.zeros(y_ref.shape, y_ref.dtype)


def _experts(blk_e, nvalid, xb, layer, wg, wu, wd):
    p_len = xb.shape[0]
    d, hid = wg.shape[2:]
    nblk = p_len // MOE_BLK
    grid_spec = pltpu.PrefetchScalarGridSpec(
        num_scalar_prefetch=2,
        grid=(nblk,),
        in_specs=[pl.BlockSpec((MOE_BLK, d // 2), lambda i, be, nv: (jnp.minimum(i, nv[0] - 1), 0)),
                  pl.BlockSpec((1, 1, d, hid), lambda i, be, nv: (layer, be[i], 0, 0)),
                  pl.BlockSpec((1, 1, d, hid), lambda i, be, nv: (layer, be[i], 0, 0)),
                  pl.BlockSpec((1, 1, hid, d), lambda i, be, nv: (layer, be[i], 0, 0))],
        out_specs=pl.BlockSpec((MOE_BLK, d // 2), lambda i, be, nv: (i, 0)),
        scratch_shapes=[pltpu.VMEM((d, hid), BF16), pltpu.VMEM((d, hid), BF16), pltpu.VMEM((hid, d), BF16)],
    )
    return pl.pallas_call(
        _expert_kernel,
        grid_spec=grid_spec,
        out_shape=jax.ShapeDtypeStruct((p_len, d // 2), jnp.uint32),
        compiler_params=_cparams(("arbitrary",)),
        name="moe_experts",
    )(blk_e, nvalid, xb, wg, wu, wd)


def _moe(h2, info, cnt, layer, wg, wu, wd):
    n, d = h2.shape
    cnt = cnt[:, :, 0].astype(jnp.int32)
    counts = jnp.sum(cnt, axis=0)
    pc = ((counts + MOE_BLK - 1) // MOE_BLK) * MOE_BLK
    pend = jnp.cumsum(pc)
    base = (pend - pc)[None, :] + jnp.cumsum(cnt, axis=0) - cnt
    eid = info[:, 0:2, :].astype(jnp.int32)
    rank = info[:, 4:6, :].astype(jnp.int32)
    sel = eid[..., None] == jnp.arange(N_EXPERTS, dtype=jnp.int32)
    dest = jnp.sum(jnp.where(sel, base[:, None, None, :], 0), axis=-1) + rank
    p_len = n * TOP_K + N_EXPERTS * MOE_BLK
    nblk = p_len // MOE_BLK
    blk_start = jnp.arange(nblk, dtype=jnp.int32) * MOE_BLK
    blk_e = jnp.minimum(jnp.sum(pend[None, :] <= blk_start[:, None], axis=1), N_EXPERTS - 1).astype(jnp.int32)
    nvalid = (pend[-1:] // MOE_BLK).astype(jnp.int32)
    dest3 = dest.reshape(dest.shape[0], 1, TOP_K * TM)
    wcol = jnp.transpose(info[:, 2:2 + TOP_K, :], (0, 2, 1)).reshape(n, TOP_K)
    bounds = jnp.stack([pend, pend - pc + counts]).astype(jnp.int32)
    xb = _dispatch(dest3, bounds, h2, p_len)
    yb = _experts(blk_e, nvalid, xb, layer, wg, wu, wd)
    return dest3, wcol, yb


def _rope_tables(t_len, qw):
    rows = t_len // GRID_W
    ax = qw // (DIFF_HEADS * 2) // 2
    row = jnp.repeat(jnp.arange(rows, dtype=F32), GRID_W)
    col = jnp.tile(jnp.arange(GRID_W, dtype=F32), rows)
    inv = ROPE_BASE ** (-jnp.arange(0, ax, 2, dtype=F32) / ax)
    ar = row[:, None] * inv[None, :]
    ac = col[:, None] * inv[None, :]
    cr, sr, cc, sc = jnp.cos(ar), jnp.sin(ar), jnp.cos(ac), jnp.sin(ac)
    cos = jnp.concatenate([cr, cr, cc, cc], axis=1)
    sins = jnp.concatenate([-sr, sr, -sc, sc], axis=1)
    reps = qw // cos.shape[1]
    cos = jnp.concatenate([jnp.ones((TM, cos.shape[1]), F32), cos], axis=0)
    sins = jnp.concatenate([jnp.zeros((TM, sins.shape[1]), F32), sins], axis=0)
    return jnp.tile(cos, (1, reps)), jnp.tile(sins, (1, reps))


def kernel(x_prompt, x_sample, cache_k, cache_v, c, c_ctx, w_ada, b_ada, g_norm1, g_norm2, g_final, w_in_e, conv_a_w, conv_a_b, ln_a_g, ln_a_b, lam_q1, lam_k1, lam_q2, lam_k2, subln_g, w_out_e, w_in_o, pool_w, pool_scale, conv_d_w, w_out_o, w_router_g, b_router_g, w_router_e, b_router_e, w_gate, w_up, w_down):
    bp, tp, d = x_prompt.shape
    bs, ts, _ = x_sample.shape
    past = cache_k.shape[2]
    assert tp == TM and ts % TM == 0 and d % LANES == 0
    npt = bp
    tps = ts // TM
    n_p, n_s = bp * tp, bs * ts
    n = n_p + n_s
    nt = n // TM
    cw = conv_a_w.shape[2]
    qw = (w_in_e.shape[2] - 2 * cw) // 3
    dh = qw // (DIFF_HEADS * 2)
    hpt = TM // HALO

    def cond_of(i):
        return jnp.where(i < npt, 0, 1 + (i - npt) // tps)

    def tile(w):
        return pl.BlockSpec((TM, w), lambda i: (i, 0))

    def halo_prev(w):
        return pl.BlockSpec((HALO, w), lambda i: (jnp.maximum(i * hpt - 1, 0), 0))

    def halo_next(w):
        return pl.BlockSpec((HALO, w), lambda i: (jnp.minimum((i + 1) * hpt, nt * hpt - 1), 0))

    def whole(shape):
        return pl.BlockSpec(shape, lambda i: (0,) * len(shape))

    mod_spec = pl.BlockSpec((1, 6, d), lambda i: (cond_of(i), 0, 0))
    combine_specs = [pl.BlockSpec((1, 1, TOP_K * TM), lambda i: (i, 0, 0), memory_space=pltpu.SMEM),
                     pl.BlockSpec((1, 1, TOP_K * TM), lambda i: (jnp.minimum(i + 1, nt - 1), 0, 0),
                                  memory_space=pltpu.SMEM),
                     tile(TOP_K), pl.BlockSpec(memory_space=pl.ANY)]
    combine_scratch = [pltpu.VMEM((2, TOP_K, TM // SUBLANES, SUBLANES, d // 2), jnp.uint32),
                       pltpu.SemaphoreType.DMA((2,))]

    rows = SUBLANES * pl.cdiv(1 + bs, SUBLANES)
    cond = jnp.concatenate([c_ctx[None, :], c, jnp.zeros((rows - 1 - bs, d), F32)], axis=0)
    mod = _ada_table(cond, w_ada, b_ada)

    xp2, xs2 = x_prompt.reshape(n_p, d), x_sample.reshape(n_s, d)

    def prompt_tile(w):
        return pl.BlockSpec((TM, w), lambda i: (jnp.minimum(i, npt - 1), 0))

    def sample_tile(w):
        return pl.BlockSpec((TM, w), lambda i: (jnp.maximum(i - npt, 0), 0))

    def router_w(i):
        wre = jnp.transpose(w_router_e[i], (1, 0, 2)).reshape(d, N_EXPERTS)
        wr = jnp.concatenate([wre, w_router_g[i]], axis=1)
        return jnp.pad(wr, ((0, 0), (0, ROUTER_PAD - wr.shape[1]))).astype(BF16)

    def router_b(i):
        b = jnp.concatenate([b_router_e[i].reshape(N_EXPERTS), b_router_g[i]])
        return jnp.pad(b, (0, ROUTER_PAD - b.shape[0]))[:, None]

    cos, sins = _rope_tables(ts, qw)
    tab_spec = pl.BlockSpec((TM, qw), lambda i: (jnp.where(i < npt, 0, 1 + (i - npt) % tps), 0))
    kc_spec = pl.BlockSpec((1, 1, TM, DIFF_HEADS, 2, dh), lambda i: (jnp.minimum(i, npt - 1), 0, 0, 0, 0, 0))
    vc_spec = pl.BlockSpec((1, 1, TM, DIFF_HEADS, 2 * dh), lambda i: (jnp.minimum(i, npt - 1), 0, 0, 0, 0))
    u, q, k, vt, new_k, new_v = pl.pallas_call(
        functools.partial(_pre_even_kernel, npt=npt, cw=cw, qw=qw, qscale=dh ** -0.5 * math.log2(math.e)),
        grid=(nt,),
        in_specs=[prompt_tile(d), sample_tile(d), mod_spec, whole((1, d)), whole(w_in_e.shape[1:]),
                  tab_spec, tab_spec],
        out_specs=[tile(cw), tile(qw), tile(qw), pl.BlockSpec((qw, TM), lambda i: (0, i)),
                   kc_spec, vc_spec],
        out_shape=[jax.ShapeDtypeStruct((n, cw), F32), jax.ShapeDtypeStruct((n, qw), BF16),
                   jax.ShapeDtypeStruct((n, qw), BF16), jax.ShapeDtypeStruct((qw, n), BF16),
                   jax.ShapeDtypeStruct((bp, 1, tp, DIFF_HEADS, 2, dh), F32),
                   jax.ShapeDtypeStruct((bp, 1, tp, DIFF_HEADS, 2 * dh), F32)],
        scratch_shapes=[pltpu.VMEM((2, TM, qw), F32)],
        compiler_params=_cparams(("arbitrary",)),
        name="pre_even",
    )(xp2, xs2, mod[0], g_norm1[0][None, :], w_in_e[0].astype(BF16), cos, sins)

    lam_init = 0.8 - 0.6 * math.exp(-0.3 * 0)
    lam = (jnp.exp(jnp.sum(lam_q1[0] * lam_k1[0])) - jnp.exp(jnp.sum(lam_q2[0] * lam_k2[0])) + lam_init)
    lam = lam.reshape(1).astype(F32)
    sub_g = subln_g[0][:, None]
    o_p = _attention(lam, q, k, vt, sub_g, None, batch=bp, t_len=tp, row0=0, tq=TM, tk=TM,
                     post_scale=1.0 - lam_init, heads_per_step=DIFF_HEADS)
    k_ctx = cache_k[:, 0].reshape(bs * past, qw).astype(BF16)
    vt_ctx = jnp.transpose(cache_v[:, 0].reshape(bs, past, qw), (0, 2, 1)).astype(BF16)
    o_s = _attention(lam, q, k, vt, sub_g, (k_ctx, vt_ctx), batch=bs, t_len=ts, row0=n_p, tq=ATTN_TQ, tk=ATTN_TK,
                     post_scale=1.0 - lam_init)

    moe_outs = [jax.ShapeDtypeStruct((n, d), F32), jax.ShapeDtypeStruct((n, d // 2), jnp.uint32),
                jax.ShapeDtypeStruct((nt, INFO_ROWS, TM), F32), jax.ShapeDtypeStruct((nt, N_EXPERTS, 1), F32)]
    moe_out_specs = [tile(d), tile(d // 2), pl.BlockSpec((1, INFO_ROWS, TM), lambda i: (i, 0, 0)),
                     pl.BlockSpec((1, N_EXPERTS, 1), lambda i: (i, 0, 0))]
    x1, h2, info, cnt = pl.pallas_call(
        functools.partial(_post_even_kernel, npt=npt, tps=tps),
        grid=(nt,),
        in_specs=[prompt_tile(d), sample_tile(d), tile(cw), halo_prev(cw), halo_next(cw),
                  prompt_tile(qw), sample_tile(qw), mod_spec,
                  whole((CONV_K, cw)), whole((1, cw)), whole((1, cw)), whole((1, cw)),
                  whole(w_out_e.shape[1:]), whole((1, d)), whole((d, ROUTER_PAD)), whole((ROUTER_PAD, 1))],
        out_specs=moe_out_specs,
        out_shape=moe_outs,
        scratch_shapes=[pltpu.VMEM((TM + 2 * HALO, cw), F32),
                        pltpu.VMEM((SUBLANES, TM + 2 * HALO - SUBLANES, cw), F32)],
        compiler_params=_cparams(("parallel",)),
        name="post_even",
    )(xp2, xs2, u, u, u, o_p, o_s, mod[0], conv_a_w[0], conv_a_b[0][None, :], ln_a_g[0][None, :], ln_a_b[0][None, :],
      w_out_e[0].astype(BF16), g_norm2[0][None, :], router_w(0), router_b(0))
    dest3, wcol, yb = _moe(h2, info, cnt, 0, w_gate, w_up, w_down)

    pw = pool_w.shape[2]
    x2, xc, p, bg = pl.pallas_call(
        functools.partial(_pre_odd_kernel, cw=cw),
        grid=(nt,),
        in_specs=[tile(d)] + combine_specs + [mod_spec, mod_spec, whole((1, d)), whole(w_in_o.shape[1:])],
        out_specs=[tile(d), tile(cw), tile(cw), tile(cw)],
        out_shape=[jax.ShapeDtypeStruct((n, d), F32)] + [jax.ShapeDtypeStruct((n, cw), F32)] * 3,
        scratch_shapes=combine_scratch,
        compiler_params=_cparams(("arbitrary",)),
        name="pre_odd",
    )(x1, dest3, dest3, wcol, yb, mod[0], mod[1], g_norm1[1][None, :], w_in_o[0].astype(BF16))
    pool_bd = jnp.zeros((cw, cw), F32)
    for g in range(len(POOL_WINDOWS)):
        pool_bd = pool_bd.at[g * pw:(g + 1) * pw, g * pw:(g + 1) * pw].set(pool_w[0, g])
    x3, h2, info, cnt = pl.pallas_call(
        functools.partial(_post_odd_kernel, npt=npt, tps=tps),
        grid=(nt,),
        in_specs=[tile(d), tile(cw), halo_prev(cw), halo_next(cw), tile(cw), halo_prev(cw), halo_next(cw),
                  tile(cw), mod_spec, whole((cw, cw)), whole((1, cw)), whole((SCONV_K, cw)),
                  whole(w_out_o.shape[1:]), whole((1, d)), whole((d, ROUTER_PAD)), whole((ROUTER_PAD, 1))],
        out_specs=moe_out_specs,
        out_shape=moe_outs,
        scratch_shapes=[pltpu.VMEM((TM + 2 * HALO, cw), F32), pltpu.VMEM((TM + 2 * HALO, cw), F32)],
        compiler_params=_cparams(("parallel",)),
        name="post_odd",
    )(x2, xc, xc, xc, p, p, p, bg, mod[1], pool_bd.astype(BF16), pool_scale[0][None, :], conv_d_w[0],
      w_out_o[0].astype(BF16), g_norm2[1][None, :], router_w(1), router_b(1))
    dest3, wcol, yb = _moe(h2, info, cnt, 1, w_gate, w_up, w_down)

    out_p, out_s = pl.pallas_call(
        functools.partial(_final_kernel, npt=npt),
        grid=(nt,),
        in_specs=[tile(d)] + combine_specs + [mod_spec, whole((1, d))],
        out_specs=[pl.BlockSpec((TM, d), lambda i: (jnp.minimum(i, npt - 1), 0)),
                   pl.BlockSpec((TM, d), lambda i: (jnp.maximum(i - npt, 0), 0))],
        out_shape=[jax.ShapeDtypeStruct((n_p, d), F32), jax.ShapeDtypeStruct((n_s, d), F32)],
        scratch_shapes=combine_scratch,
        compiler_params=_cparams(("arbitrary",)),
        name="final_norm",
    )(x3, dest3, dest3, wcol, yb, mod[1], g_final[None, :])

    return (out_p.reshape(bp, tp, d), out_s.reshape(bs, ts, d), new_k, new_v)
```

```python
import functools
import math

import jax
import jax.numpy as jnp
from jax import lax
from jax.experimental import pallas as pl
from jax.experimental.pallas import tpu as pltpu

F32 = jnp.float32
BF16 = jnp.bfloat16

EPS = 1e-6
GRID_W = 64
ROPE_BASE = 10000.0
CONV_K = 31
SCONV_K = 3
POOL_WINDOWS = (2, 4, 8, 16)
DIFF_HEADS = 4
N_GROUPS = 4
EXPERTS_PER_GROUP = 8
N_EXPERTS = N_GROUPS * EXPERTS_PER_GROUP
TOP_K = 2

LANES = 128
SUBLANES = 8
TM = 256
HALO = 16
PROJ_ROW_PARTS = 2
MIX_ROW_PARTS = 1
PROJ_COLS = 256
MOE_BLK = 1024
ZERO_ROWS = 256
ATTN_TQ = 2048
ATTN_TK = 512
CONV_ROWS = 32
INFO_ROWS = 8
ROUTER_PAD = LANES
ONES_ROWS = 2 * SUBLANES
DISPATCH_SLOTS = 3
VMEM_LIMIT = 56 * 1024 * 1024


def _cparams(sem):
    return pltpu.CompilerParams(dimension_semantics=sem, vmem_limit_bytes=VMEM_LIMIT)


def _rms_mod(x, g, sc, sh):
    ms = jnp.mean(x * x, axis=-1, keepdims=True)
    return (x * lax.rsqrt(ms + EPS)) * g * (1.0 + sc) + sh


def _silu(x):
    return x * jax.nn.sigmoid(x)


def _pack_bf16_pairs(y):
    half = y.shape[1] // 2
    return pltpu.pack_elementwise([y[:, :half], y[:, half:]], packed_dtype=BF16)


def _unpack_bf16_pairs(w):
    return jnp.concatenate([pltpu.unpack_elementwise(w, index=i, packed_dtype=BF16, unpacked_dtype=F32)
                            for i in range(2)], axis=1)


def _ada_kernel(c_ref, w_ref, b_ref, o_ref):
    cs = _silu(c_ref[...])
    o_ref[0] = jnp.dot(cs, w_ref[0], precision=lax.Precision.HIGHEST,
                       preferred_element_type=F32) + b_ref[0]


def _ada_table(cond, w_ada, b_ada):
    depth, d, six_d = w_ada.shape
    rows = cond.shape[0]
    nj = six_d // d
    out = pl.pallas_call(
        _ada_kernel,
        grid=(depth, nj),
        in_specs=[pl.BlockSpec((rows, d), lambda l, j: (0, 0)),
                  pl.BlockSpec((1, d, d), lambda l, j: (l, 0, j)),
                  pl.BlockSpec((1, 1, d), lambda l, j: (l, 0, j))],
        out_specs=pl.BlockSpec((1, rows, d), lambda l, j: (l, 0, j)),
        out_shape=jax.ShapeDtypeStruct((depth, rows, six_d), F32),
        compiler_params=_cparams(("parallel", "parallel")),
        name="ada_table",
    )(cond, w_ada, b_ada.reshape(depth, 1, six_d))
    return out.reshape(depth, rows, nj, d)


def _row_parts(parts):
    rows = TM // parts
    return [slice(p * rows, (p + 1) * rows) for p in range(parts)]


def _rope(x, cos, sins, half):
    lane = lax.broadcasted_iota(jnp.int32, (1, LANES), 1)
    first_half = (lane % (2 * half)) < half
    outs = []
    for j in range(x.shape[1] // LANES):
        sl = slice(j * LANES, (j + 1) * LANES)
        xs = x[:, sl]
        nxt = pltpu.roll(xs, LANES - half, 1)
        prv = pltpu.roll(xs, half, 1)
        rot = jnp.where(first_half, nxt, prv)
        outs.append(xs * cos[:, sl] + rot * sins[:, sl])
    return jnp.concatenate(outs, axis=1)


def _pre_even_kernel(xp_ref, xs_ref, mod_ref, g_ref, w_ref, cos_ref, sin_ref,
                     u_ref, q_ref, k_ref, vt_ref, kc_ref, vc_ref, kv_sc, *, npt, cw, qw, qscale):
    i = pl.program_id(0)
    is_prompt = i < npt
    half = kc_ref.shape[-1] // 4
    for sl in _row_parts(PROJ_ROW_PARTS):
        x = jnp.where(is_prompt, xp_ref[sl, :], xs_ref[sl, :])
        h = _rms_mod(x, g_ref[...], mod_ref[0, 1:2, :], mod_ref[0, 0:1, :])
        proj = jnp.dot(h.astype(BF16), w_ref[...], preferred_element_type=F32)
        ga = proj[:, :cw]
        gb = proj[:, cw:2 * cw]
        u_ref[sl, :] = ga * jax.nn.sigmoid(gb)
        q = proj[:, 2 * cw:2 * cw + qw]
        k = proj[:, 2 * cw + qw:2 * cw + 2 * qw]
        v = proj[:, 2 * cw + 2 * qw:]
        kv_sc[0, sl, :] = k
        kv_sc[1, sl, :] = v
        cos = cos_ref[sl, :]
        sins = sin_ref[sl, :]
        q_ref[sl, :] = (_rope(q, cos, sins, half) * qscale).astype(BF16)
        k_ref[sl, :] = _rope(k, cos, sins, half).astype(BF16)
        vt_ref[:, sl] = v.T.astype(BF16)

    @pl.when(is_prompt)
    def _():
        heads, halves, dh = kc_ref.shape[3:]
        for h in range(heads):
            vc_ref[0, 0, :, h, :] = kv_sc[1, :, h * halves * dh:(h + 1) * halves * dh]
            for c in range(halves):
                lo = (h * halves + c) * dh
                kc_ref[0, 0, :, h, c, :] = kv_sc[0, :, lo:lo + dh]


def _attn_kernel(lam_ref, q_ref, k_ref, vt_ref, *rest, nk, tk, nk_ctx, post_scale):
    if nk_ctx:
        kc_ref, vtc_ref, g_ref, o_ref, s_sc, mc_sc, m_sc, acc_sc = rest
    else:
        g_ref, o_ref, s_sc, mc_sc, m_sc, acc_sc = rest
    q = q_ref[...]
    hw = q.shape[1]
    dh = hw // 2
    lane = lax.broadcasted_iota(jnp.int32, (1, hw), 1)
    zero = jnp.zeros_like(q)
    qs = (jnp.where(lane < dh, q, zero), jnp.where(lane >= dh, q, zero))
    m_sc[...] = jnp.full(m_sc.shape, -jnp.inf, F32)
    acc_sc[...] = jnp.zeros(acc_sc.shape, F32)
    ones_rows = (lax.broadcasted_iota(jnp.int32, (ONES_ROWS, tk), 0) == 0).astype(BF16)

    def scores(kk, slot):
        for c in range(2):
            s = lax.dot_general(kk, qs[c], (((1,), (1,)), ((), ())), preferred_element_type=F32)
            s_sc[slot, c] = s
            mc_sc[slot, c] = jnp.max(s, axis=0, keepdims=True)

    def consume(vt, slot):
        vta = jnp.concatenate([vt, ones_rows], axis=0)
        for c in range(2):
            m_prev = m_sc[c]
            m_new = jnp.maximum(m_prev, mc_sc[slot, c])
            alpha = jnp.exp2(m_prev - m_new)
            p = jnp.exp2(s_sc[slot, c] - m_new).astype(BF16)
            acc_sc[c] = alpha * acc_sc[c] + jnp.dot(vta, p, preferred_element_type=F32)
            m_sc[c] = m_new

    def k_at(i):
        if i < nk_ctx:
            return kc_ref[i * tk:(i + 1) * tk, :]
        return k_ref[(i - nk_ctx) * tk:(i - nk_ctx + 1) * tk, :]

    def vt_at(i):
        if i < nk_ctx:
            return vtc_ref[0, :, i * tk:(i + 1) * tk]
        return vt_ref[:, (i - nk_ctx) * tk:(i - nk_ctx + 1) * tk]

    n_total = nk_ctx + nk
    peel = nk_ctx + (n_total - 1 - nk_ctx) % 2
    scores(k_at(0), 0)
    for i in range(peel):
        scores(k_at(i + 1), (i + 1) % 2)
        consume(vt_at(i), i % 2)

    def body(t, carry):
        for r in range(2):
            j = 2 * t + (peel + r - nk_ctx)
            nxt = pl.multiple_of((j + 1) * tk, tk)
            cur = pl.multiple_of(j * tk, tk)
            scores(k_ref[pl.ds(nxt, tk), :], (peel + r + 1) % 2)
            consume(vt_ref[:, pl.ds(cur, tk)], (peel + r) % 2)
        return carry

    lax.fori_loop(0, (n_total - 1 - peel) // 2, body, 0)
    consume(vt_at(n_total - 1), (n_total - 1) % 2)
    o = (acc_sc[0, :hw, :] / acc_sc[0, hw:hw + 1, :]
         - lam_ref[0] * (acc_sc[1, :hw, :] / acc_sc[1, hw:hw + 1, :]))
    ms = jnp.mean(o * o, axis=0, keepdims=True)
    o = o * lax.rsqrt(ms + EPS) * g_ref[...] * post_scale
    o_ref[...] = o.T.astype(BF16)


def _attn_heads_kernel(lam_ref, q_ref, k_ref, vt_ref, g_ref, o_ref, *scratch, heads, **kw):
    hw = q_ref.shape[1] // heads
    for h in range(heads):
        cols = pl.ds(h * hw, hw)
        _attn_kernel(lam_ref, q_ref.at[:, cols], k_ref.at[:, cols], vt_ref.at[cols, :], g_ref, o_ref.at[:, cols],
                     *scratch, **kw)


def _attention(lam, q, k, vt, g_col, ctx, *, batch, t_len, row0, tq, tk, post_scale, heads_per_step=1):
    hw = q.shape[1] // DIFF_HEADS
    bw = hw * heads_per_step
    nq = t_len // tq
    assert row0 % t_len == 0 and t_len % tk == 0 and (ctx is None or heads_per_step == 1)
    qb0 = row0 // tq
    sb0 = row0 // t_len
    in_specs = [pl.BlockSpec(memory_space=pltpu.SMEM),
                pl.BlockSpec((tq, bw), lambda b, h, i: (qb0 + b * nq + i, h)),
                pl.BlockSpec((t_len, bw), lambda b, h, i: (sb0 + b, h)),
                pl.BlockSpec((bw, t_len), lambda b, h, i: (h, sb0 + b))]
    args = [lam, q, k, vt]
    nk_ctx = 0
    if ctx is not None:
        k_ctx, vt_ctx = ctx
        past = vt_ctx.shape[2]
        assert past % tk == 0
        nk_ctx = past // tk
        in_specs += [pl.BlockSpec((past, hw), lambda b, h, i: (b, h)),
                     pl.BlockSpec((1, hw, past), lambda b, h, i: (b, h, 0))]
        args += [k_ctx, vt_ctx]
    in_specs.append(pl.BlockSpec((hw, 1), lambda b, h, i: (0, 0)))
    args.append(g_col)
    body = functools.partial(_attn_kernel, nk=t_len // tk, tk=tk, nk_ctx=nk_ctx, post_scale=post_scale)
    if heads_per_step > 1:
        body = functools.partial(_attn_heads_kernel, heads=heads_per_step, nk=t_len // tk, tk=tk, nk_ctx=nk_ctx,
                                 post_scale=post_scale)
    return pl.pallas_call(
        body,
        grid=(batch, DIFF_HEADS // heads_per_step, nq),
        in_specs=in_specs,
        out_specs=pl.BlockSpec((tq, bw), lambda b, h, i: (b * nq + i, h)),
        out_shape=jax.ShapeDtypeStruct((batch * t_len, q.shape[1]), BF16),
        scratch_shapes=[pltpu.VMEM((2, 2, tk, tq), F32), pltpu.VMEM((2, 2, 1, tq), F32),
                        pltpu.VMEM((2, 1, tq), F32), pltpu.VMEM((2, hw + ONES_ROWS, tq), F32)],
        compiler_params=_cparams(("parallel", "parallel", "parallel")),
        name="diff_attention",
    )(*args)


def _seq_edges(i, npt, tps):
    r = (i - npt) % tps
    is_prompt = i < npt
    first = jnp.logical_or(is_prompt, r == 0)
    last = jnp.logical_or(is_prompt, r == tps - 1)
    return first, last


def _fill_ext(ext_sc, cur_ref, prev_ref, next_ref, first, last):
    prev = prev_ref[...]
    nxt = next_ref[...]
    ext_sc[0:HALO, :] = jnp.where(first, jnp.zeros_like(prev), prev)
    ext_sc[HALO:HALO + TM, :] = cur_ref[...]
    ext_sc[HALO + TM:, :] = jnp.where(last, jnp.zeros_like(nxt), nxt)


def _route_tile(logits, b_ref, info_ref, cnt_ref):
    lt = logits.T + b_ref[...]
    gl = lt[N_EXPERTS:N_EXPERTS + N_GROUPS]
    ge = jnp.exp(gl - jnp.max(gl, axis=0, keepdims=True))
    gp = ge / jnp.sum(ge, axis=0, keepdims=True)
    best = gp[0:1]
    gidx = jnp.zeros(best.shape, jnp.int32)
    el = lt[0:EXPERTS_PER_GROUP]
    for g in range(1, N_GROUPS):
        better = gp[g:g + 1] > best
        gidx = jnp.where(better, g, gidx)
        best = jnp.where(better, gp[g:g + 1], best)
    for g in range(1, N_GROUPS):
        el = jnp.where(gidx == g, lt[g * EXPERTS_PER_GROUP:(g + 1) * EXPERTS_PER_GROUP], el)
    rio = lax.broadcasted_iota(jnp.int32, el.shape, 0)
    v0 = jnp.max(el, axis=0, keepdims=True)
    i0 = jnp.min(jnp.where(el == v0, rio, EXPERTS_PER_GROUP), axis=0, keepdims=True)
    el2 = jnp.where(rio == i0, -jnp.inf, el)
    v1 = jnp.max(el2, axis=0, keepdims=True)
    i1 = jnp.min(jnp.where(el2 == v1, rio, EXPERTS_PER_GROUP), axis=0, keepdims=True)
    e1 = jnp.exp(v1 - v0)
    den = 1.0 + e1
    w0 = (1.0 / den) * best
    w1 = (e1 / den) * best
    eid0 = gidx * EXPERTS_PER_GROUP + i0
    eid1 = gidx * EXPERTS_PER_GROUP + i1
    tm = lt.shape[1]
    eio = lax.broadcasted_iota(jnp.int32, (N_EXPERTS, tm), 0)
    oh0 = eio == eid0
    oh1 = eio == eid1
    ohs = jnp.where(oh0, 1.0, 0.0) + jnp.where(oh1, 1.0, 0.0)
    upper = (lax.broadcasted_iota(jnp.int32, (tm, tm), 0)
             < lax.broadcasted_iota(jnp.int32, (tm, tm), 1)).astype(BF16)
    before = jnp.dot(ohs.astype(BF16), upper, preferred_element_type=F32)
    rank0 = jnp.sum(jnp.where(oh0, before, 0.0), axis=0, keepdims=True)
    rank1 = jnp.sum(jnp.where(oh1, before, 0.0), axis=0, keepdims=True)
    info_ref[0] = jnp.concatenate([eid0.astype(F32), eid1.astype(F32), w0, w1, rank0, rank1,
                                   jnp.zeros((INFO_ROWS - 6, tm), F32)], axis=0)
    cnt_ref[0] = jnp.sum(ohs, axis=1, keepdims=True)


def _moe_prologue(x_new, sl, mod_ref, g2_ref, wr_ref, rb_ref, h2_ref, info_ref, cnt_ref):
    h2 = _rms_mod(x_new, g2_ref[...], mod_ref[0, 4:5, :], mod_ref[0, 3:4, :])
    h2_ref[sl, :] = _pack_bf16_pairs(h2)
    assert sl.stop - sl.start == TM
    _route_tile(jnp.dot(h2.astype(BF16), wr_ref[...], preferred_element_type=F32), rb_ref, info_ref, cnt_ref)


def _post_even_kernel(xp_ref, xs_ref, u_ref, up_ref, un_ref, op_ref, os_ref, mod_ref, cw_ref, cb_ref,
                      lng_ref, lnb_ref, wo_ref, g2_ref, wr_ref, rb_ref, x1_ref, h2_ref, info_ref, cnt_ref,
                      ext_sc, sh_sc, *, npt, tps):
    i = pl.program_id(0)
    first, last = _seq_edges(i, npt, tps)
    _fill_ext(ext_sc, u_ref, up_ref, un_ref, first, last)
    cw = u_ref.shape[1]
    span = sh_sc.shape[1]
    for r in range(SUBLANES):
        sh_sc[r] = ext_sc[r:r + span, :]
    rows = CONV_ROWS
    is_prompt = i < npt
    for sl in _row_parts(MIX_ROW_PARTS):
        chunks = []
        for r0 in range(sl.start, sl.stop, rows):
            acc = None
            for k in range(CONV_K):
                off = HALO - CONV_K // 2 + k
                start = r0 + off - off % SUBLANES
                term = sh_sc[off % SUBLANES, start:start + rows, :] * cw_ref[k:k + 1, :]
                acc = term if acc is None else acc + term
            chunks.append(acc)
        conv = jnp.concatenate(chunks, axis=0) + cb_ref[...]
        mu = jnp.mean(conv, axis=-1, keepdims=True)
        cen = conv - mu
        var = jnp.mean(cen * cen, axis=-1, keepdims=True)
        uu = _silu(cen * lax.rsqrt(var + EPS) * lng_ref[...] + lnb_ref[...])
        o = jnp.where(is_prompt, op_ref[sl, :], os_ref[sl, :])
        m = (jnp.dot(uu.astype(BF16), wo_ref[0:cw, :], preferred_element_type=F32)
             + jnp.dot(o, wo_ref[cw:, :], preferred_element_type=F32))
        x1 = jnp.where(is_prompt, xp_ref[sl, :], xs_ref[sl, :]) + mod_ref[0, 2:3, :] * m
        x1_ref[sl, :] = x1
        _moe_prologue(x1, sl, mod_ref, g2_ref, wr_ref, rb_ref, h2_ref, info_ref, cnt_ref)


def _gather_start(dest_ref, yb_hbm, ybuf, sem, slot):
    def body(i, carry):
        for j in range(SUBLANES):
            for k in range(TOP_K):
                row = dest_ref[0, 0, k * TM + i * SUBLANES + j]
                pltpu.make_async_copy(yb_hbm.at[pl.ds(row, 1)], ybuf.at[slot, k, i, pl.ds(j, 1)],
                                      sem.at[slot]).start(priority=k % 2)
        return carry

    lax.fori_loop(0, TM // SUBLANES, body, 0)


def _moe_gather(dcur_ref, dnxt_ref, yb_hbm, ybuf, sem, prefetch_next):
    t = pl.program_id(0)
    slot = t % 2

    @pl.when(t == 0)
    def _():
        _gather_start(dcur_ref, yb_hbm, ybuf, sem, 0)

    if prefetch_next:
        @pl.when(t + 1 < pl.num_programs(0))
        def _():
            _gather_start(dnxt_ref, yb_hbm, ybuf, sem, 1 - slot)

    _gather_wait(ybuf, sem, slot)
    return slot


def _gather_wait(ybuf, sem, slot):
    for k in range(TOP_K):
        pltpu.make_async_copy(ybuf.at[slot, k], ybuf.at[slot, k], sem.at[slot]).wait()


def _moe_prefetch_group(dnxt_ref, yb_hbm, ybuf, sem, slot, g, n_groups):
    n_i = TM // SUBLANES
    for i in range(g * n_i // n_groups, (g + 1) * n_i // n_groups):
        for j in range(SUBLANES):
            for k in range(TOP_K):
                row = dnxt_ref[0, 0, k * TM + i * SUBLANES + j]
                pltpu.make_async_copy(yb_hbm.at[pl.ds(row, 1)], ybuf.at[1 - slot, k, i, pl.ds(j, 1)],
                                      sem.at[1 - slot]).start(priority=k % 2)


def _moe_prefetch_finish(ybuf, sem, slot):
    @pl.when(pl.program_id(0) == pl.num_programs(0) - 1)
    def _():
        _gather_wait(ybuf, sem, 1 - slot)


def _combined_rows(cw_ref, ybuf, slot, sl):
    dw = ybuf.shape[-1]
    g = slice(sl.start // SUBLANES, sl.stop // SUBLANES)
    w = cw_ref[sl, :]
    y = w[:, 0:1] * _unpack_bf16_pairs(ybuf[slot, 0, g].reshape(sl.stop - sl.start, dw))
    for k in range(1, TOP_K):
        y = y + w[:, k:k + 1] * _unpack_bf16_pairs(ybuf[slot, k, g].reshape(sl.stop - sl.start, dw))
    return y


def _dispatch_kernel(dest_ref, pend_ref, h2_hbm, xb_hbm, stage, zeros, in_sem, row_sem, zero_sem):
    t = pl.program_id(0)
    nt = pl.num_programs(0)
    slot = t % DISPATCH_SLOTS

    tile_rows = TM // SUBLANES

    def stage_in(tile_idx, s):
        return pltpu.make_async_copy(h2_hbm.at[pl.ds(tile_idx * tile_rows, tile_rows)], stage.at[s], in_sem.at[s])

    def drain_rows(s):
        for _ in range(TOP_K):
            pltpu.make_async_copy(stage.at[s], stage.at[s], row_sem.at[s]).wait()

    def zero_block(start):
        return pltpu.make_async_copy(zeros, xb_hbm.at[pl.ds(pl.multiple_of(start, MOE_BLK), MOE_BLK)], zero_sem)

    def zero_granule(start):
        return pltpu.make_async_copy(zeros.at[pl.ds(0, ZERO_ROWS)],
                                     xb_hbm.at[pl.ds(pl.multiple_of(start, ZERO_ROWS), ZERO_ROWS)], zero_sem)

    def zero_fill(op):
        for e in range(N_EXPERTS):
            for q in range(MOE_BLK // ZERO_ROWS):
                end = pend_ref[0, e] - q * ZERO_ROWS

                @pl.when(end > pend_ref[1, e])
                def _():
                    op(zero_granule(end - ZERO_ROWS))
        for j in range(N_EXPERTS):
            start = pend_ref[0, N_EXPERTS - 1] + j * MOE_BLK

            @pl.when(start < xb_hbm.shape[0])
            def _():
                op(zero_block(start))

    @pl.when(t == 0)
    def _():
        stage_in(0, 0).start()
        zeros[...] = jnp.zeros(zeros.shape, zeros.dtype)
        zero_fill(lambda cp: cp.start())
        zero_fill(lambda cp: cp.wait())

    @pl.when(jnp.logical_and(t == 0, nt > 1))
    def _():
        stage_in(1, 1).start()

    stage_in(t, slot).wait()

    def body(i, carry):
        for j in range(SUBLANES):
            for k in range(TOP_K):
                row = dest_ref[0, 0, k * TM + i * SUBLANES + j]
                pltpu.make_async_copy(stage.at[slot, i, pl.ds(j, 1)], xb_hbm.at[pl.ds(row, 1)],
                                      row_sem.at[slot]).start(priority=k % 2)
        return carry

    lax.fori_loop(0, tile_rows, body, 0)

    @pl.when(t > 0)
    def _():
        drain_rows((t + DISPATCH_SLOTS - 1) % DISPATCH_SLOTS)

    @pl.when(t + 2 < nt)
    def _():
        stage_in(t + 2, (t + 2) % DISPATCH_SLOTS).start()

    @pl.when(t == nt - 1)
    def _():
        drain_rows(slot)


def _dispatch(dest3, pend, h2, p_len):
    n, d = h2.shape
    return pl.pallas_call(
        _dispatch_kernel,
        grid=(n // TM,),
        in_specs=[pl.BlockSpec((1, 1, TOP_K * TM), lambda t: (t, 0, 0), memory_space=pltpu.SMEM),
                  pl.BlockSpec(memory_space=pltpu.SMEM), pl.BlockSpec(memory_space=pl.ANY)],
        out_specs=pl.BlockSpec(memory_space=pl.ANY),
        out_shape=jax.ShapeDtypeStruct((p_len, d), h2.dtype),
        scratch_shapes=[pltpu.VMEM((DISPATCH_SLOTS, TM // SUBLANES, SUBLANES, d), h2.dtype),
                        pltpu.VMEM((MOE_BLK, d), h2.dtype),
                        pltpu.SemaphoreType.DMA((DISPATCH_SLOTS,)), pltpu.SemaphoreType.DMA((DISPATCH_SLOTS,)),
                        pltpu.SemaphoreType.DMA(())],
        compiler_params=_cparams(("arbitrary",)),
        name="moe_dispatch",
    )(dest3, pend, h2.reshape(n // SUBLANES, SUBLANES, d))


def _pre_odd_kernel(x_ref, dcur_ref, dnxt_ref, cw_ref, yb_hbm, modp_ref, mod_ref, g_ref, w_ref,
                    x2_ref, xc_ref, p_ref, bg_ref, ybuf, sem, *, cw):
    slot = _moe_gather(dcur_ref, dnxt_ref, yb_hbm, ybuf, sem, prefetch_next=False)
    col_blocks = w_ref.shape[1] // PROJ_COLS
    n_groups = PROJ_ROW_PARTS * (1 + col_blocks)
    group = 0
    for sl in _row_parts(PROJ_ROW_PARTS):
        x2 = x_ref[sl, :] + modp_ref[0, 5:6, :] * _combined_rows(cw_ref, ybuf, slot, sl)
        x2_ref[sl, :] = x2
        _moe_prefetch_group(dnxt_ref, yb_hbm, ybuf, sem, slot, group, n_groups)
        group += 1
        h = _rms_mod(x2, g_ref[...], mod_ref[0, 1:2, :], mod_ref[0, 0:1, :]).astype(BF16)
        hd = []
        for c in range(col_blocks):
            cols = slice(c * PROJ_COLS, (c + 1) * PROJ_COLS)
            blk = jnp.dot(h, w_ref[:, cols], preferred_element_type=F32)
            which, off = divmod(c * PROJ_COLS, cw)
            dst = slice(off, off + PROJ_COLS)
            if which == 0:
                xc_ref[sl, dst] = blk
            elif which == 1:
                hd.append(blk)
            elif which == 2:
                bg_ref[sl, dst] = blk
            else:
                p_ref[sl, dst] = blk * hd[off // PROJ_COLS]
            _moe_prefetch_group(dnxt_ref, yb_hbm, ybuf, sem, slot, group, n_groups)
            group += 1
    _moe_prefetch_finish(ybuf, sem, slot)


def _post_odd_kernel(x_ref, xc_ref, xcp_ref, xcn_ref, p_ref, pp_ref, pn_ref, bg_ref, mod_ref,
                     pw_ref, ps_ref, dw_ref, wo_ref, g2_ref, wr_ref, rb_ref,
                     x3_ref, h2_ref, info_ref, cnt_ref, extc_sc, extp_sc, *, npt, tps):
    i = pl.program_id(0)
    first, last = _seq_edges(i, npt, tps)
    _fill_ext(extc_sc, xc_ref, xcp_ref, xcn_ref, first, last)
    _fill_ext(extp_sc, p_ref, pp_ref, pn_ref, first, last)
    cw = xc_ref.shape[1]
    gc = cw // len(POOL_WINDOWS)
    is_prompt = i < npt
    t_len = jnp.where(is_prompt, TM, TM * tps)
    tile_pos = jnp.where(is_prompt, 0, ((i - npt) % tps) * TM)
    for rs in _row_parts(MIX_ROW_PARTS):
        nrows = rs.stop - rs.start
        base = HALO + rs.start
        pos = lax.broadcasted_iota(jnp.int32, (nrows, 1), 0) + (tile_pos + rs.start)
        ds = []
        for g, w in enumerate(POOL_WINDOWS):
            sl = slice(g * gc, (g + 1) * gc)
            acc = None
            for o in range(-(w // 2), w // 2):
                term = extc_sc[base + o:base + o + nrows, sl]
                acc = term if acc is None else acc + term
            lo = jnp.maximum(pos - w // 2, 0)
            hi = jnp.minimum(pos + w // 2 - 1, t_len - 1)
            cnt = (hi - lo + 1).astype(F32)
            ds.append(acc / cnt - extc_sc[base:base + nrows, sl])
        d = jnp.concatenate(ds, axis=1).astype(BF16)
        yc = jnp.dot(d, pw_ref[...], preferred_element_type=F32) * ps_ref[...]
        conv = None
        for k in range(SCONV_K):
            start = base - SCONV_K // 2 + k
            term = extp_sc[start:start + nrows, :] * dw_ref[k:k + 1, :]
            conv = term if conv is None else conv + term
        yd = bg_ref[rs, :] * conv
        m = (jnp.dot(yc.astype(BF16), wo_ref[0:cw, :], preferred_element_type=F32)
             + jnp.dot(yd.astype(BF16), wo_ref[cw:, :], preferred_element_type=F32))
        x3 = x_ref[rs, :] + mod_ref[0, 2:3, :] * m
        x3_ref[rs, :] = x3
        _moe_prologue(x3, rs, mod_ref, g2_ref, wr_ref, rb_ref, h2_ref, info_ref, cnt_ref)


def _final_kernel(x_ref, dcur_ref, dnxt_ref, cw_ref, yb_hbm, modp_ref, g_ref, op_ref, os_ref, ybuf, sem, *, npt):
    i = pl.program_id(0)
    slot = _moe_gather(dcur_ref, dnxt_ref, yb_hbm, ybuf, sem, prefetch_next=True)
    x = x_ref[...] + modp_ref[0, 5:6, :] * _combined_rows(cw_ref, ybuf, slot, slice(0, TM))
    ms = jnp.mean(x * x, axis=-1, keepdims=True)
    out = x * lax.rsqrt(ms + EPS) * g_ref[...]

    @pl.when(i < npt)
    def _():
        op_ref[...] = out

    @pl.when(i >= npt)
    def _():
        os_ref[...] = out


def _expert_kernel(be_ref, nv_ref, x_ref, wg_ref, wu_ref, wd_ref, y_ref, wg_sc, wu_sc, wd_sc):
    i = pl.program_id(0)

    @pl.when(jnp.logical_or(i == 0, be_ref[i] != be_ref[jnp.maximum(i - 1, 0)]))
    def _():
        wg_sc[...] = wg_ref[0, 0].astype(BF16)
        wu_sc[...] = wu_ref[0, 0].astype(BF16)
        wd_sc[...] = wd_ref[0, 0].astype(BF16)

    @pl.when(i < nv_ref[0])
    def _():
        x = _unpack_bf16_pairs(x_ref[...]).astype(BF16)
        hg = jnp.dot(x, wg_sc[...], preferred_element_type=F32)
        hu = jnp.dot(x, wu_sc[...], preferred_element_type=F32)
        hid = (_silu(hg) * hu).astype(BF16)
        y_ref[...] = _pack_bf16_pairs(jnp.dot(hid, wd_sc[...], preferred_element_type=F32))

    @pl.when(i >= nv_ref[0])
    def _():
        y_ref[...] = jnp.zeros(y_ref.shape, y_ref.dtype)


def _experts(blk_e, nvalid, xb, layer, wg, wu, wd):
    p_len = xb.shape[0]
    d, hid = wg.shape[2:]
    nblk = p_len // MOE_BLK
    grid_spec = pltpu.PrefetchScalarGridSpec(
        num_scalar_prefetch=2,
        grid=(nblk,),
        in_specs=[pl.BlockSpec((MOE_BLK, d // 2), lambda i, be, nv: (jnp.minimum(i, nv[0] - 1), 0)),
                  pl.BlockSpec((1, 1, d, hid), lambda i, be, nv: (layer, be[i], 0, 0)),
                  pl.BlockSpec((1, 1, d, hid), lambda i, be, nv: (layer, be[i], 0, 0)),
                  pl.BlockSpec((1, 1, hid, d), lambda i, be, nv: (layer, be[i], 0, 0))],
        out_specs=pl.BlockSpec((MOE_BLK, d // 2), lambda i, be, nv: (i, 0)),
        scratch_shapes=[pltpu.VMEM((d, hid), BF16), pltpu.VMEM((d, hid), BF16), pltpu.VMEM((hid, d), BF16)],
    )
    return pl.pallas_call(
        _expert_kernel,
        grid_spec=grid_spec,
        out_shape=jax.ShapeDtypeStruct((p_len, d // 2), jnp.uint32),
        compiler_params=_cparams(("arbitrary",)),
        name="moe_experts",
    )(blk_e, nvalid, xb, wg, wu, wd)


def _moe(h2, info, cnt, layer, wg, wu, wd):
    n, d = h2.shape
    cnt = cnt[:, :, 0].astype(jnp.int32)
    counts = jnp.sum(cnt, axis=0)
    pc = ((counts + MOE_BLK - 1) // MOE_BLK) * MOE_BLK
    pend = jnp.cumsum(pc)
    base = (pend - pc)[None, :] + jnp.cumsum(cnt, axis=0) - cnt
    eid = info[:, 0:2, :].astype(jnp.int32)
    rank = info[:, 4:6, :].astype(jnp.int32)
    sel = eid[..., None] == jnp.arange(N_EXPERTS, dtype=jnp.int32)
    dest = jnp.sum(jnp.where(sel, base[:, None, None, :], 0), axis=-1) + rank
    p_len = n * TOP_K + N_EXPERTS * MOE_BLK
    nblk = p_len // MOE_BLK
    blk_start = jnp.arange(nblk, dtype=jnp.int32) * MOE_BLK
    blk_e = jnp.minimum(jnp.sum(pend[None, :] <= blk_start[:, None], axis=1), N_EXPERTS - 1).astype(jnp.int32)
    nvalid = (pend[-1:] // MOE_BLK).astype(jnp.int32)
    dest3 = dest.reshape(dest.shape[0], 1, TOP_K * TM)
    wcol = jnp.transpose(info[:, 2:2 + TOP_K, :], (0, 2, 1)).reshape(n, TOP_K)
    bounds = jnp.stack([pend, pend - pc + counts]).astype(jnp.int32)
    xb = _dispatch(dest3, bounds, h2, p_len)
    yb = _experts(blk_e, nvalid, xb, layer, wg, wu, wd)
    return dest3, wcol, yb


def _rope_tables(t_len, qw):
    rows = t_len // GRID_W
    ax = qw // (DIFF_HEADS * 2) // 2
    row = jnp.repeat(jnp.arange(rows, dtype=F32), GRID_W)
    col = jnp.tile(jnp.arange(GRID_W, dtype=F32), rows)
    inv = ROPE_BASE ** (-jnp.arange(0, ax, 2, dtype=F32) / ax)
    ar = row[:, None] * inv[None, :]
    ac = col[:, None] * inv[None, :]
    cr, sr, cc, sc = jnp.cos(ar), jnp.sin(ar), jnp.cos(ac), jnp.sin(ac)
    cos = jnp.concatenate([cr, cr, cc, cc], axis=1)
    sins = jnp.concatenate([-sr, sr, -sc, sc], axis=1)
    reps = qw // cos.shape[1]
    cos = jnp.concatenate([jnp.ones((TM, cos.shape[1]), F32), cos], axis=0)
    sins = jnp.concatenate([jnp.zeros((TM, sins.shape[1]), F32), sins], axis=0)
    return jnp.tile(cos, (1, reps)), jnp.tile(sins, (1, reps))


def kernel(x_prompt, x_sample, cache_k, cache_v, c, c_ctx, w_ada, b_ada, g_norm1, g_norm2, g_final, w_in_e, conv_a_w, conv_a_b, ln_a_g, ln_a_b, lam_q1, lam_k1, lam_q2, lam_k2, subln_g, w_out_e, w_in_o, pool_w, pool_scale, conv_d_w, w_out_o, w_router_g, b_router_g, w_router_e, b_router_e, w_gate, w_up, w_down):
    bp, tp, d = x_prompt.shape
    bs, ts, _ = x_sample.shape
    past = cache_k.shape[2]
    assert tp == TM and ts % TM == 0 and d % LANES == 0
    npt = bp
    tps = ts // TM
    n_p, n_s = bp * tp, bs * ts
    n = n_p + n_s
    nt = n // TM
    cw = conv_a_w.shape[2]
    qw = (w_in_e.shape[2] - 2 * cw) // 3
    dh = qw // (DIFF_HEADS * 2)
    hpt = TM // HALO

    def cond_of(i):
        return jnp.where(i < npt, 0, 1 + (i - npt) // tps)

    def tile(w):
        return pl.BlockSpec((TM, w), lambda i: (i, 0))

    def halo_prev(w):
        return pl.BlockSpec((HALO, w), lambda i: (jnp.maximum(i * hpt - 1, 0), 0))

    def halo_next(w):
        return pl.BlockSpec((HALO, w), lambda i: (jnp.minimum((i + 1) * hpt, nt * hpt - 1), 0))

    def whole(shape):
        return pl.BlockSpec(shape, lambda i: (0,) * len(shape))

    mod_spec = pl.BlockSpec((1, 6, d), lambda i: (cond_of(i), 0, 0))
    combine_specs = [pl.BlockSpec((1, 1, TOP_K * TM), lambda i: (i, 0, 0), memory_space=pltpu.SMEM),
                     pl.BlockSpec((1, 1, TOP_K * TM), lambda i: (jnp.minimum(i + 1, nt - 1), 0, 0),
                                  memory_space=pltpu.SMEM),
                     tile(TOP_K), pl.BlockSpec(memory_space=pl.ANY)]
    combine_scratch = [pltpu.VMEM((2, TOP_K, TM // SUBLANES, SUBLANES, d // 2), jnp.uint32),
                       pltpu.SemaphoreType.DMA((2,))]

    rows = SUBLANES * pl.cdiv(1 + bs, SUBLANES)
    cond = jnp.concatenate([c_ctx[None, :], c, jnp.zeros((rows - 1 - bs, d), F32)], axis=0)
    mod = _ada_table(cond, w_ada, b_ada)

    xp2, xs2 = x_prompt.reshape(n_p, d), x_sample.reshape(n_s, d)

    def prompt_tile(w):
        return pl.BlockSpec((TM, w), lambda i: (jnp.minimum(i, npt - 1), 0))

    def sample_tile(w):
        return pl.BlockSpec((TM, w), lambda i: (jnp.maximum(i - npt, 0), 0))

    def router_w(i):
        wre = jnp.transpose(w_router_e[i], (1, 0, 2)).reshape(d, N_EXPERTS)
        wr = jnp.concatenate([wre, w_router_g[i]], axis=1)
        return jnp.pad(wr, ((0, 0), (0, ROUTER_PAD - wr.shape[1]))).astype(BF16)

    def router_b(i):
        b = jnp.concatenate([b_router_e[i].reshape(N_EXPERTS), b_router_g[i]])
        return jnp.pad(b, (0, ROUTER_PAD - b.shape[0]))[:, None]

    cos, sins = _rope_tables(ts, qw)
    tab_spec = pl.BlockSpec((TM, qw), lambda i: (jnp.where(i < npt, 0, 1 + (i - npt) % tps), 0))
    kc_spec = pl.BlockSpec((1, 1, TM, DIFF_HEADS, 2, dh), lambda i: (jnp.minimum(i, npt - 1), 0, 0, 0, 0, 0))
    vc_spec = pl.BlockSpec((1, 1, TM, DIFF_HEADS, 2 * dh), lambda i: (jnp.minimum(i, npt - 1), 0, 0, 0, 0))
    u, q, k, vt, new_k, new_v = pl.pallas_call(
        functools.partial(_pre_even_kernel, npt=npt, cw=cw, qw=qw, qscale=dh ** -0.5 * math.log2(math.e)),
        grid=(nt,),
        in_specs=[prompt_tile(d), sample_tile(d), mod_spec, whole((1, d)), whole(w_in_e.shape[1:]),
                  tab_spec, tab_spec],
        out_specs=[tile(cw), tile(qw), tile(qw), pl.BlockSpec((qw, TM), lambda i: (0, i)),
                   kc_spec, vc_spec],
        out_shape=[jax.ShapeDtypeStruct((n, cw), F32), jax.ShapeDtypeStruct((n, qw), BF16),
                   jax.ShapeDtypeStruct((n, qw), BF16), jax.ShapeDtypeStruct((qw, n), BF16),
                   jax.ShapeDtypeStruct((bp, 1, tp, DIFF_HEADS, 2, dh), F32),
                   jax.ShapeDtypeStruct((bp, 1, tp, DIFF_HEADS, 2 * dh), F32)],
        scratch_shapes=[pltpu.VMEM((2, TM, qw), F32)],
        compiler_params=_cparams(("arbitrary",)),
        name="pre_even",
    )(xp2, xs2, mod[0], g_norm1[0][None, :], w_in_e[0].astype(BF16), cos, sins)

    lam_init = 0.8 - 0.6 * math.exp(-0.3 * 0)
    lam = (jnp.exp(jnp.sum(lam_q1[0] * lam_k1[0])) - jnp.exp(jnp.sum(lam_q2[0] * lam_k2[0])) + lam_init)
    lam = lam.reshape(1).astype(F32)
    sub_g = subln_g[0][:, None]
    o_p = _attention(lam, q, k, vt, sub_g, None, batch=bp, t_len=tp, row0=0, tq=TM, tk=TM,
                     post_scale=1.0 - lam_init, heads_per_step=DIFF_HEADS)
    k_ctx = cache_k[:, 0].reshape(bs * past, qw).astype(BF16)
    vt_ctx = jnp.transpose(cache_v[:, 0].reshape(bs, past, qw), (0, 2, 1)).astype(BF16)
    o_s = _attention(lam, q, k, vt, sub_g, (k_ctx, vt_ctx), batch=bs, t_len=ts, row0=n_p, tq=ATTN_TQ, tk=ATTN_TK,
                     post_scale=1.0 - lam_init)

    moe_outs = [jax.ShapeDtypeStruct((n, d), F32), jax.ShapeDtypeStruct((n, d // 2), jnp.uint32),
                jax.ShapeDtypeStruct((nt, INFO_ROWS, TM), F32), jax.ShapeDtypeStruct((nt, N_EXPERTS, 1), F32)]
    moe_out_specs = [tile(d), tile(d // 2), pl.BlockSpec((1, INFO_ROWS, TM), lambda i: (i, 0, 0)),
                     pl.BlockSpec((1, N_EXPERTS, 1), lambda i: (i, 0, 0))]
    x1, h2, info, cnt = pl.pallas_call(
        functools.partial(_post_even_kernel, npt=npt, tps=tps),
        grid=(nt,),
        in_specs=[prompt_tile(d), sample_tile(d), tile(cw), halo_prev(cw), halo_next(cw),
                  prompt_tile(qw), sample_tile(qw), mod_spec,
                  whole((CONV_K, cw)), whole((1, cw)), whole((1, cw)), whole((1, cw)),
                  whole(w_out_e.shape[1:]), whole((1, d)), whole((d, ROUTER_PAD)), whole((ROUTER_PAD, 1))],
        out_specs=moe_out_specs,
        out_shape=moe_outs,
        scratch_shapes=[pltpu.VMEM((TM + 2 * HALO, cw), F32),
                        pltpu.VMEM((SUBLANES, TM + 2 * HALO - SUBLANES, cw), F32)],
        compiler_params=_cparams(("parallel",)),
        name="post_even",
    )(xp2, xs2, u, u, u, o_p, o_s, mod[0], conv_a_w[0], conv_a_b[0][None, :], ln_a_g[0][None, :], ln_a_b[0][None, :],
      w_out_e[0].astype(BF16), g_norm2[0][None, :], router_w(0), router_b(0))
    dest3, wcol, yb = _moe(h2, info, cnt, 0, w_gate, w_up, w_down)

    pw = pool_w.shape[2]
    x2, xc, p, bg = pl.pallas_call(
        functools.partial(_pre_odd_kernel, cw=cw),
        grid=(nt,),
        in_specs=[tile(d)] + combine_specs + [mod_spec, mod_spec, whole((1, d)), whole(w_in_o.shape[1:])],
        out_specs=[tile(d), tile(cw), tile(cw), tile(cw)],
        out_shape=[jax.ShapeDtypeStruct((n, d), F32)] + [jax.ShapeDtypeStruct((n, cw), F32)] * 3,
        scratch_shapes=combine_scratch,
        compiler_params=_cparams(("arbitrary",)),
        name="pre_odd",
    )(x1, dest3, dest3, wcol, yb, mod[0], mod[1], g_norm1[1][None, :], w_in_o[0].astype(BF16))
    pool_bd = jnp.zeros((cw, cw), F32)
    for g in range(len(POOL_WINDOWS)):
        pool_bd = pool_bd.at[g * pw:(g + 1) * pw, g * pw:(g + 1) * pw].set(pool_w[0, g])
    x3, h2, info, cnt = pl.pallas_call(
        functools.partial(_post_odd_kernel, npt=npt, tps=tps),
        grid=(nt,),
        in_specs=[tile(d), tile(cw), halo_prev(cw), halo_next(cw), tile(cw), halo_prev(cw), halo_next(cw),
                  tile(cw), mod_spec, whole((cw, cw)), whole((1, cw)), whole((SCONV_K, cw)),
                  whole(w_out_o.shape[1:]), whole((1, d)), whole((d, ROUTER_PAD)), whole((ROUTER_PAD, 1))],
        out_specs=moe_out_specs,
        out_shape=moe_outs,
        scratch_shapes=[pltpu.VMEM((TM + 2 * HALO, cw), F32), pltpu.VMEM((TM + 2 * HALO, cw), F32)],
        compiler_params=_cparams(("parallel",)),
        name="post_odd",
    )(x2, xc, xc, xc, p, p, p, bg, mod[1], pool_bd.astype(BF16), pool_scale[0][None, :], conv_d_w[0],
      w_out_o[0].astype(BF16), g_norm2[1][None, :], router_w(1), router_b(1))
    dest3, wcol, yb = _moe(h2, info, cnt, 1, w_gate, w_up, w_down)

    out_p, out_s = pl.pallas_call(
        functools.partial(_final_kernel, npt=npt),
        grid=(nt,),
        in_specs=[tile(d)] + combine_specs + [mod_spec, whole((1, d))],
        out_specs=[pl.BlockSpec((TM, d), lambda i: (jnp.minimum(i, npt - 1), 0)),
                   pl.BlockSpec((TM, d), lambda i: (jnp.maximum(i - npt, 0), 0))],
        out_shape=[jax.ShapeDtypeStruct((n_p, d), F32), jax.ShapeDtypeStruct((n_s, d), F32)],
        scratch_shapes=combine_scratch,
        compiler_params=_cparams(("arbitrary",)),
        name="final_norm",
    )(x3, dest3, dest3, wcol, yb, mod[1], g_final[None, :])

    return (out_p.reshape(bp, tp, d), out_s.reshape(bs, ts, d), new_k, new_v)
```
